```python
import math
import jax, jax.numpy as jnp
from jax import lax
import numpy as np

D_MODEL = 1024
BATCH = 4
SEQ = 8192
DEPTH = 2

N_HEADS = 16
HEAD_DIM = D_MODEL // N_HEADS
ROPE_THETA = 10000.0
N_MIXERS = 2
DN_ALPHA = (2 * DEPTH) ** 0.25
DN_BETA = (8 * DEPTH) ** -0.25
LN_EPS = 1e-5
NEG = -1e30
FORCE = 1e9

NSA_KV_GROUPS = 4
NSA_HPG = N_HEADS // NSA_KV_GROUPS
NSA_KV_DIM = NSA_KV_GROUPS * HEAD_DIM
NSA_CMP_LEN = 32
NSA_CMP_STRIDE = 16
NSA_CMP_HIDDEN = 256
NSA_SEL_BLOCK = 64
NSA_SEL_TOPN = 16
NSA_WINDOW = 512
NSA_QB = 64
NSA_IN_DIM = D_MODEL + 6 * NSA_KV_DIM + 3 * N_HEADS

MOBA_BLOCK = 256
MOBA_TOPK = 3
MOBA_QB = 16
MOBA_IN_DIM = 3 * D_MODEL

N_EXPERTS = 16
N_GROUPS = 4
EXPERTS_PER_GROUP = N_EXPERTS // N_GROUPS
TOP_K = 2
D_EXPERT = 512

N_NSA_LAYERS = (DEPTH + 1) // 2
N_MOBA_LAYERS = DEPTH // 2

kernel_name = 'nsa_moba_interleaved_grouped_moe_deepnorm'


def layer_norm(x, g, b):
    xf = x.astype(jnp.float32)
    mu = xf.mean(-1, keepdims=True)
    var = jnp.mean(jnp.square(xf - mu), -1, keepdims=True)
    return ((xf - mu) * lax.rsqrt(var + LN_EPS) * g.astype(jnp.float32) + b.astype(jnp.float32)).astype(x.dtype)


def rope_tables(pos):
    inv = 1.0 / (ROPE_THETA ** (jnp.arange(0, HEAD_DIM, 2, dtype=jnp.float32) / HEAD_DIM))
    ang = pos.astype(jnp.float32)[:, None] * inv[None, :]
    return jnp.cos(ang), jnp.sin(ang)


def apply_rope(x, cos, sin):
    x1, x2 = jnp.split(x.astype(jnp.float32), 2, axis=-1)
    return jnp.concatenate([x1 * cos - x2 * sin, x2 * cos + x1 * sin], axis=-1).astype(x.dtype)


def _split_heads(t, n):
    B, S, _ = t.shape
    return t.reshape(B, S, n, HEAD_DIM).transpose(0, 2, 1, 3)


def _merge_chunks(o):
    nc, B, H, qb, dh = o.shape
    return o.transpose(1, 0, 3, 2, 4).reshape(B, nc * qb, H * dh)


def nsa_mixer(h, w_in, w_out, cmp_k_w1, cmp_k_w2, cmp_v_w1, cmp_v_w2, cmp_k_pos, cmp_v_pos):
    B, S, _ = h.shape
    G, HPG, dh = NSA_KV_GROUPS, NSA_HPG, HEAD_DIM
    L, STR, SB, W, QB = NSA_CMP_LEN, NSA_CMP_STRIDE, NSA_SEL_BLOCK, NSA_WINDOW, NSA_QB
    n_cmp = (S - L) // STR + 1
    n_sel = S // SB
    top_n = min(NSA_SEL_TOPN, n_sel)
    scale = dh ** -0.5

    proj = h @ w_in
    splits = np.cumsum([D_MODEL] + [NSA_KV_DIM] * 6).tolist()
    q, kc_raw, vc_raw, ks, vs, kw, vw, gates = jnp.split(proj, splits, axis=-1)

    cos, sin = rope_tables(jnp.arange(S))
    q = apply_rope(_split_heads(q, N_HEADS), cos, sin).reshape(B, G, HPG, S, dh)
    ks = apply_rope(_split_heads(ks, G), cos, sin)
    kw = apply_rope(_split_heads(kw, G), cos, sin)
    vs = _split_heads(vs, G)
    vw = _split_heads(vw, G)

    win_idx = jnp.arange(n_cmp)[:, None] * STR + jnp.arange(L)[None, :]
    cmp_end = win_idx[:, -1]

    def compress(t, pos_emb, w1, w2):
        blocks = _split_heads(t, G)[:, :, win_idx] + pos_emb
        return jax.nn.gelu(blocks.reshape(B, G, n_cmp, L * dh) @ w1) @ w2

    ccos, csin = rope_tables(cmp_end)
    kc = apply_rope(compress(kc_raw, cmp_k_pos, cmp_k_w1, cmp_k_w2), ccos, csin)
    vc = compress(vc_raw, cmp_v_pos, cmp_v_w1, cmp_v_w2)

    ci = jnp.arange(n_cmp)[:, None]
    sj = jnp.arange(n_sel)[None, :]
    overlap = ((ci * STR < (sj + 1) * SB) & (ci * STR + L > sj * SB)).astype(jnp.float32)

    ks_blk = ks.reshape(B, G, n_sel, SB, dh)
    vs_blk = vs.reshape(B, G, n_sel, SB, dh)
    pad = ((0, 0), (0, 0), (W, 0), (0, 0))
    kw_pad = jnp.pad(kw, pad)
    vw_pad = jnp.pad(vw, pad)
    bi = jnp.arange(B)[:, None, None, None]
    gi = jnp.arange(G)[None, :, None, None]
    blk_ids = jnp.arange(n_sel)

    def chunk(c):
        s0 = c * QB
        tq = s0 + jnp.arange(QB)
        qc = lax.dynamic_slice_in_dim(q, s0, QB, axis=3)

        sc = jnp.einsum('bghqd,bgnd->bghqn', qc, kc, preferred_element_type=jnp.float32) * scale
        cmask = cmp_end[None, :] <= tq[:, None]
        pc = jax.nn.softmax(jnp.where(cmask, sc, NEG), axis=-1) * cmask
        o_cmp = jnp.einsum('bghqn,bgnd->bghqd', pc.astype(vc.dtype), vc)

        imp = jnp.einsum('bghqn,nj->bgqj', pc, overlap)
        jq = s0 // SB
        forced = (blk_ids == 0) | (blk_ids == jq) | (blk_ids == jq - 1)
        imp = jnp.where(blk_ids > jq, NEG, jnp.where(forced, FORCE, imp))
        _, sel = lax.top_k(imp, top_n)
        kg = ks_blk[bi, gi, sel].reshape(B, G, QB, top_n * SB, dh)
        vg = vs_blk[bi, gi, sel].reshape(B, G, QB, top_n * SB, dh)
        kpos = (sel[..., None] * SB + jnp.arange(SB)).reshape(B, G, QB, top_n * SB)
        smask = (kpos <= tq[None, None, :, None])[:, :, None]
        ss = jnp.einsum('bghqd,bgqkd->bghqk', qc, kg, preferred_element_type=jnp.float32) * scale
        ps = jax.nn.softmax(jnp.where(smask, ss, NEG), axis=-1)
        o_slc = jnp.einsum('bghqk,bgqkd->bghqd', ps.astype(vg.dtype), vg)

        kwc = lax.dynamic_slice_in_dim(kw_pad, s0, W + QB, axis=2)
        vwc = lax.dynamic_slice_in_dim(vw_pad, s0, W + QB, axis=2)
        wpos = s0 - W + jnp.arange(W + QB)
        wmask = (wpos[None, :] <= tq[:, None]) & (wpos[None, :] > tq[:, None] - W) & (wpos[None, :] >= 0)
        sw = jnp.einsum('bghqd,bgkd->bghqk', qc, kwc, preferred_element_type=jnp.float32) * scale
        pw = jax.nn.softmax(jnp.where(wmask, sw, NEG), axis=-1)
        o_win = jnp.einsum('bghqk,bgkd->bghqd', pw.astype(vwc.dtype), vwc)
        return o_cmp, o_slc, o_win

    o_cmp, o_slc, o_win = lax.map(chunk, jnp.arange(S // QB))

    def heads(o):
        return _merge_chunks(o.reshape(-1, B, N_HEADS, QB, dh)).reshape(B, S, N_HEADS, dh)

    g = jax.nn.sigmoid(gates).reshape(B, S, 3, N_HEADS, 1)
    o = g[:, :, 0] * heads(o_cmp) + g[:, :, 1] * heads(o_slc) + g[:, :, 2] * heads(o_win)
    return o.reshape(B, S, D_MODEL) @ w_out


def moba_mixer(h, w_in, w_out):
    B, S, _ = h.shape
    BLK, QB, dh, H = MOBA_BLOCK, MOBA_QB, HEAD_DIM, N_HEADS
    nb = -(-S // BLK)
    s_pad = nb * BLK
    top_k = min(MOBA_TOPK, nb)
    n_g = top_k * BLK
    scale = dh ** -0.5

    q, k, v = jnp.split(h @ w_in, 3, axis=-1)
    cos, sin = rope_tables(jnp.arange(S))
    q = apply_rope(_split_heads(q, H), cos, sin)
    k = apply_rope(_split_heads(k, H), cos, sin)
    v = _split_heads(v, H)
    padw = ((0, 0), (0, 0), (0, s_pad - S), (0, 0))
    k_blk = jnp.pad(k, padw).reshape(B, H, nb, BLK, dh)
    v_blk = jnp.pad(v, padw).reshape(B, H, nb, BLK, dh)
    k_mean = k_blk.astype(jnp.float32).mean(axis=3)
    bi = jnp.arange(B)[:, None, None, None]
    hi = jnp.arange(H)[None, :, None, None]
    blk_ids = jnp.arange(nb)

    def chunk(c):
        s0 = c * QB
        tq = s0 + jnp.arange(QB)
        cb = s0 // BLK
        qc = lax.dynamic_slice_in_dim(q, s0, QB, axis=2)
        gsc = jnp.einsum('bhqd,bhnd->bhqn', qc.astype(jnp.float32), k_mean)
        gsc = jnp.where(blk_ids < cb, gsc, NEG)
        _, sel = lax.top_k(gsc, top_k)
        valid = jnp.repeat(sel < cb, BLK, axis=-1)
        kg = k_blk[bi, hi, sel].reshape(B, H, QB, n_g, dh)
        vg = v_blk[bi, hi, sel].reshape(B, H, QB, n_g, dh)
        ko = k_blk[:, :, cb]
        vo = v_blk[:, :, cb]
        omask = (cb * BLK + jnp.arange(BLK))[None, :] <= tq[:, None]
        sg = jnp.einsum('bhqd,bhqkd->bhqk', qc, kg, preferred_element_type=jnp.float32) * scale
        so = jnp.einsum('bhqd,bhkd->bhqk', qc, ko, preferred_element_type=jnp.float32) * scale
        s = jnp.concatenate([jnp.where(valid, sg, NEG), jnp.where(omask, so, NEG)], axis=-1)
        p = jax.nn.softmax(s, axis=-1).astype(v.dtype)
        return (jnp.einsum('bhqk,bhqkd->bhqd', p[..., :n_g], vg)
                + jnp.einsum('bhqk,bhkd->bhqd', p[..., n_g:], vo))

    o = lax.map(chunk, jnp.arange(S // QB))
    return _merge_chunks(o) @ w_out


def grouped_moe(h, router_w, router_bias, w_gate, w_up, w_down):
    B, S, D = h.shape
    xt = h.reshape(B * S, D)
    scores = jax.nn.sigmoid((xt @ router_w).astype(jnp.float32))
    biased = scores + router_bias.astype(jnp.float32)
    grp = biased.reshape(-1, N_GROUPS, EXPERTS_PER_GROUP)
    grp_score = lax.top_k(grp, 2)[0].sum(-1)
    _, g_sel = lax.top_k(grp_score, 1)
    emask = jnp.repeat(jnp.arange(N_GROUPS)[None, :] == g_sel, EXPERTS_PER_GROUP, axis=-1)
    _, e_sel = lax.top_k(jnp.where(emask, biased, NEG), TOP_K)
    w_sel = jnp.take_along_axis(scores, e_sel, axis=-1)
    w_sel = w_sel / w_sel.sum(-1, keepdims=True)
    gate = jnp.einsum('tk,tke->te', w_sel, jax.nn.one_hot(e_sel, N_EXPERTS, dtype=jnp.float32)).astype(xt.dtype)
    out = jnp.zeros_like(xt)
    for e in range(N_EXPERTS):
        he = jax.nn.silu(xt @ w_gate[e]) * (xt @ w_up[e])
        out = out + (he @ w_down[e]) * gate[:, e:e + 1]
    return out.reshape(B, S, D)


def setup_inputs(seed: int = 0) -> dict:
    key = jax.random.key(seed)
    ks = jax.random.split(key, 18)
    f32 = jnp.float32
    nrm = lambda k, shape, s: jax.random.normal(k, shape, f32) * s
    beta = DN_BETA
    KV = NSA_KV_DIM
    nsa_col = np.concatenate([np.ones(D_MODEL), np.ones(KV), np.full(KV, beta), np.ones(KV),
                              np.full(KV, beta), np.ones(KV), np.full(KV, beta),
                              np.ones(3 * N_HEADS)]).astype(np.float32)
    moba_col = np.concatenate([np.ones(2 * D_MODEL), np.full(D_MODEL, beta)]).astype(np.float32)
    cmp_in = NSA_CMP_LEN * HEAD_DIM
    return {
        'x': jax.random.normal(ks[0], (BATCH, SEQ, D_MODEL), f32),
        'nsa_w_in': nrm(ks[1], (N_NSA_LAYERS, D_MODEL, NSA_IN_DIM), D_MODEL ** -0.5) * jnp.asarray(nsa_col),
        'nsa_w_out': nrm(ks[2], (N_NSA_LAYERS, D_MODEL, D_MODEL), D_MODEL ** -0.5 * beta),
        'nsa_cmp_k_w1': nrm(ks[3], (N_NSA_LAYERS, cmp_in, NSA_CMP_HIDDEN), cmp_in ** -0.5),
        'nsa_cmp_k_w2': nrm(ks[4], (N_NSA_LAYERS, NSA_CMP_HIDDEN, HEAD_DIM), NSA_CMP_HIDDEN ** -0.5),
        'nsa_cmp_v_w1': nrm(ks[5], (N_NSA_LAYERS, cmp_in, NSA_CMP_HIDDEN), cmp_in ** -0.5),
        'nsa_cmp_v_w2': nrm(ks[6], (N_NSA_LAYERS, NSA_CMP_HIDDEN, HEAD_DIM), NSA_CMP_HIDDEN ** -0.5 * beta),
        'nsa_cmp_k_pos': nrm(ks[7], (N_NSA_LAYERS, NSA_CMP_LEN, HEAD_DIM), 0.1),
        'nsa_cmp_v_pos': nrm(ks[8], (N_NSA_LAYERS, NSA_CMP_LEN, HEAD_DIM), 0.1),
        'moba_w_in': nrm(ks[9], (N_MOBA_LAYERS, D_MODEL, MOBA_IN_DIM), D_MODEL ** -0.5) * jnp.asarray(moba_col),
        'moba_w_out': nrm(ks[10], (N_MOBA_LAYERS, D_MODEL, D_MODEL), D_MODEL ** -0.5 * beta),
        'router_w': nrm(ks[11], (D_MODEL, N_EXPERTS), D_MODEL ** -0.5),
        'router_bias': nrm(ks[12], (N_EXPERTS,), 0.01),
        'moe_w_gate': nrm(ks[13], (DEPTH, N_EXPERTS, D_MODEL, D_EXPERT), D_MODEL ** -0.5),
        'moe_w_up': nrm(ks[14], (DEPTH, N_EXPERTS, D_MODEL, D_EXPERT), D_MODEL ** -0.5),
        'moe_w_down': nrm(ks[15], (DEPTH, N_EXPERTS, D_EXPERT, D_MODEL), D_EXPERT ** -0.5 * beta),
        'ln_g': 1.0 + nrm(ks[16], (DEPTH, 2, D_MODEL), 0.05),
        'ln_b': nrm(ks[17], (DEPTH, 2, D_MODEL), 0.05),
    }


def reference(x, nsa_w_in, nsa_w_out, nsa_cmp_k_w1, nsa_cmp_k_w2, nsa_cmp_v_w1, nsa_cmp_v_w2,
              nsa_cmp_k_pos, nsa_cmp_v_pos, moba_w_in, moba_w_out, router_w, router_bias,
              moe_w_gate, moe_w_up, moe_w_down, ln_g, ln_b):
    for layer in range(DEPTH):
        j = layer // N_MIXERS
        if layer % N_MIXERS == 0:
            mix = nsa_mixer(x, nsa_w_in[j], nsa_w_out[j], nsa_cmp_k_w1[j], nsa_cmp_k_w2[j],
                            nsa_cmp_v_w1[j], nsa_cmp_v_w2[j], nsa_cmp_k_pos[j], nsa_cmp_v_pos[j])
        else:
            mix = moba_mixer(x, moba_w_in[j], moba_w_out[j])
        x = layer_norm(DN_ALPHA * x + mix, ln_g[layer, 0], ln_b[layer, 0])
        ffn = grouped_moe(x, router_w, router_bias, moe_w_gate[layer], moe_w_up[layer], moe_w_down[layer])
        x = layer_norm(DN_ALPHA * x + ffn, ln_g[layer, 1], ln_b[layer, 1])
    return x
```

```python
import functools

import jax
import jax.numpy as jnp
import numpy as np
from jax import lax
from jax.experimental import pallas as pl
from jax.experimental.pallas import tpu as pltpu

F32 = jnp.float32
BF16 = jnp.bfloat16

D_MODEL = 1024
N_HEADS = 16
HEAD_DIM = 64
HALF = HEAD_DIM // 2
ROPE_THETA = 10000.0
DEPTH = 2
DN_ALPHA = (2 * DEPTH) ** 0.25
LN_EPS = 1e-5
NEG = -1e30
FORCE = 1e9
MASK = -1e30
REMOVED = -3.0e38
LANES = 128
Q_SCALE = HEAD_DIM ** -0.5

NSA_KV_GROUPS = 4
NSA_HPG = N_HEADS // NSA_KV_GROUPS
NSA_KV_DIM = NSA_KV_GROUPS * HEAD_DIM
NSA_CMP_LEN = 32
NSA_CMP_STRIDE = 16
NSA_SEL_BLOCK = 64
NSA_SEL_TOPN = 16
NSA_WINDOW = 512

MOBA_BLOCK = 256
MOBA_TOPK = 3

N_EXPERTS = 16
N_GROUPS = 4
EXPERTS_PER_GROUP = N_EXPERTS // N_GROUPS
D_EXPERT = 512

VMEM_LIMIT = 48 * 1024 * 1024


def _cparams(sem):
    return pltpu.CompilerParams(dimension_semantics=sem, vmem_limit_bytes=VMEM_LIMIT)


def _dot(a, b):
    return jnp.dot(a, b, preferred_element_type=F32)


def _dot_nt(a, b):
    return lax.dot_general(a, b, (((1,), (1,)), ((), ())), preferred_element_type=F32)


def _split_bf16(x):
    hi = x.astype(BF16)
    lo = (x - hi.astype(F32)).astype(BF16)
    return hi, lo


def _sigmoid(x):
    return 1.0 / (1.0 + jnp.exp(-x))


def _proj_body(mode_ref, x_ref, w_ref, cos_ref, sin_ref, o_ref, *, tn):
    j = pl.program_id(0)
    mode = mode_ref[j]
    acc = _dot(x_ref[...], w_ref[...])

    @pl.when(mode == 0)
    def _():
        o_ref[...] = acc.astype(o_ref.dtype)

    @pl.when(mode != 0)
    def _():
        sc = jnp.where(mode == 2, Q_SCALE, 1.0).astype(F32)
        cos = cos_ref[...] * sc
        sin = sin_ref[...] * sc
        lane = lax.broadcasted_iota(jnp.int32, cos.shape, 1)
        lower = (lane % HEAD_DIM) < HALF
        for c in range(tn // LANES):
            a = acc[:, c * LANES:(c + 1) * LANES]
            up = pltpu.roll(a, LANES - HALF, 1)
            dn = pltpu.roll(a, HALF, 1)
            partner = jnp.where(lower, -up, dn)
            o_ref[:, c * LANES:(c + 1) * LANES] = (a * cos + partner * sin).astype(o_ref.dtype)


def _proj(xb, w, cos2, sin2, modes, seq, out_dtype=BF16, tm=512, tn=256):
    T, K = xb.shape
    N = w.shape[1]
    tm = min(tm, seq)
    assert T % tm == 0 and N % tn == 0 and seq % tm == 0 and len(modes) == N // tn
    n_pos = seq // tm
    grid_spec = pltpu.PrefetchScalarGridSpec(
        num_scalar_prefetch=1,
        grid=(N // tn, T // tm),
        in_specs=[
            pl.BlockSpec((tm, K), lambda j, i, m: (i, 0)),
            pl.BlockSpec((K, tn), lambda j, i, m: (0, j)),
            pl.BlockSpec((tm, LANES), lambda j, i, m: (i % n_pos, 0)),
            pl.BlockSpec((tm, LANES), lambda j, i, m: (i % n_pos, 0)),
        ],
        out_specs=pl.BlockSpec((tm, tn), lambda j, i, m: (i, j)),
    )
    return pl.pallas_call(
        functools.partial(_proj_body, tn=tn),
        grid_spec=grid_spec,
        out_shape=jax.ShapeDtypeStruct((T, N), out_dtype),
        compiler_params=_cparams(("parallel", "parallel")),
    )(jnp.asarray(modes, jnp.int32), xb, w, cos2, sin2)


def _rope_tables(pos):
    inv = 1.0 / (ROPE_THETA ** (jnp.arange(0, HEAD_DIM, 2, dtype=F32) / HEAD_DIM))
    ang = pos.astype(F32)[:, None] * inv[None, :]
    return jnp.cos(ang), jnp.sin(ang)


def _gelu_tanh(x):
    c = np.float32(np.sqrt(2.0 / np.pi))
    return 0.5 * x * (1.0 + jnp.tanh(c * (x + 0.044715 * (x * x * x))))


def _compress_body(r_ref, w1_ref, pos_ref, w2_ref, w2r_ref, cos_ref, sin_ref, o_ref, *, rope, n_cmp):
    r = r_ref[0]
    nc = r.shape[0]
    half = NSA_CMP_STRIDE * HEAD_DIM
    a = _dot(r, w1_ref[0])
    b = _dot(r, w1_ref[1])
    pos = pos_ref[...]
    pb = _dot(pos[:, :half], w1_ref[0]) + _dot(pos[:, half:], w1_ref[1])
    b_next = pltpu.roll(b, nc - 1, 0)
    h = _gelu_tanh(a + b_next + pb[0:1, :]).astype(BF16)
    o = _dot(h, w2_ref[...])
    if rope:
        o = o * cos_ref[...] + _dot(h, w2r_ref[...]) * sin_ref[...]
    row = lax.broadcasted_iota(jnp.int32, o.shape, 0)
    o_ref[0] = jnp.where(row < n_cmp, o, 0.0).astype(o_ref.dtype)


def _compress(r, w1, pos, w2, cos_c, sin_c, rope, n_cmp):
    BG, NC, K = r.shape
    hidden = w1.shape[1]
    w1s = w1.astype(BF16).reshape(2, K, hidden)
    pos8 = jnp.zeros((8, 2 * K), BF16).at[0].set(pos.reshape(-1).astype(BF16))
    w2r = jnp.concatenate([-w2[:, HALF:], w2[:, :HALF]], axis=1).astype(BF16)
    full = lambda shape: pl.BlockSpec(shape, lambda i: (0,) * len(shape))
    return pl.pallas_call(
        functools.partial(_compress_body, rope=rope, n_cmp=n_cmp),
        grid=(BG,),
        in_specs=[
            pl.BlockSpec((1, NC, K), lambda i: (i, 0, 0)),
            full((2, K, hidden)),
            full((8, 2 * K)),
            full((hidden, HEAD_DIM)),
            full((hidden, HEAD_DIM)),
            full((NC, HEAD_DIM)),
            full((NC, HEAD_DIM)),
        ],
        out_specs=pl.BlockSpec((1, NC, HEAD_DIM), lambda i: (i, 0, 0)),
        out_shape=jax.ShapeDtypeStruct((BG, NC, HEAD_DIM), BF16),
        compiler_params=_cparams(("parallel",)),
    )(r, w1s, pos8, w2.astype(BF16), w2r, cos_c, sin_c)


def _topk_mask(v, k):
    lane = lax.broadcasted_iota(jnp.int32, v.shape, 1).astype(F32)

    def step(_, carry):
        v, sel = carry
        m = jnp.max(v, axis=-1, keepdims=True)
        idx = jnp.min(jnp.where(v == m, lane, float(LANES)), axis=-1, keepdims=True)
        hit = lane == idx
        return jnp.where(hit, REMOVED, v), jnp.where(hit, 1.0, sel)

    _, sel = lax.fori_loop(0, k, step, (v, jnp.zeros(v.shape, F32)), unroll=True)
    return sel > 0.5


def _nsa_cmp_body(q_ref, kc_ref, vc_ref, ov_ref, g_ref, o_ref, selb_ref, *, tq, n_cmp, n_sel, top_n):
    s0 = pl.program_id(2) * tq
    kc = kc_ref[0, 0]
    vc = vc_ref[0, 0]
    nc = kc.shape[0]
    tpos = s0 + lax.broadcasted_iota(jnp.int32, (tq, nc), 0)
    nidx = lax.broadcasted_iota(jnp.int32, (tq, nc), 1)
    cmask = (nidx * NSA_CMP_STRIDE + (NSA_CMP_LEN - 1) <= tpos) & (nidx < n_cmp)
    psum = jnp.zeros((tq, nc), F32)
    for h in range(NSA_HPG):
        s = jnp.where(cmask, _dot_nt(q_ref[0, 0, h], kc), NEG)
        m = jnp.max(s, axis=-1, keepdims=True)
        e = jnp.where(cmask, jnp.exp(s - m), 0.0)
        l = jnp.sum(e, axis=-1, keepdims=True)
        p = e / jnp.maximum(l, 1e-30)
        psum = psum + p
        o = _dot(p.astype(BF16), vc) * _sigmoid(g_ref[0, 0, h])
        o_ref[0, 0, h] = o.astype(o_ref.dtype)

    p_hi, p_lo = _split_bf16(psum)
    imp = _dot(p_hi, ov_ref[...]) + _dot(p_lo, ov_ref[...])
    blk = lax.broadcasted_iota(jnp.int32, imp.shape, 1)
    jq = (s0 + lax.broadcasted_iota(jnp.int32, imp.shape, 0)) // NSA_SEL_BLOCK
    forced = (blk == 0) | (blk == jq) | (blk == jq - 1)
    imp = jnp.where(blk > jq, NEG, jnp.where(forced, FORCE, imp))
    imp = jnp.where(blk < n_sel, imp, REMOVED)
    sel = _topk_mask(imp, top_n)
    selb_ref[0, 0] = jnp.where(sel, 0.0, MASK).astype(selb_ref.dtype)


def _nsa_cmp(qh, kc, vc, overlap, g_cmp, n_cmp, n_sel, top_n, tq=128):
    B, G, HPG, S, dh = qh.shape
    NC = kc.shape[2]
    tq = min(tq, S)
    return pl.pallas_call(
        functools.partial(_nsa_cmp_body, tq=tq, n_cmp=n_cmp, n_sel=n_sel, top_n=top_n),
        grid=(B, G, S // tq),
        in_specs=[
            pl.BlockSpec((1, 1, HPG, tq, dh), lambda b, g, i: (b, g, 0, i, 0)),
            pl.BlockSpec((1, 1, NC, dh), lambda b, g, i: (b, g, 0, 0)),
            pl.BlockSpec((1, 1, NC, dh), lambda b, g, i: (b, g, 0, 0)),
            pl.BlockSpec((NC, LANES), lambda b, g, i: (0, 0)),
            pl.BlockSpec((1, 1, HPG, tq, 1), lambda b, g, i: (b, g, 0, i, 0)),
        ],
        out_specs=[
            pl.BlockSpec((1, 1, HPG, tq, dh), lambda b, g, i: (b, g, 0, i, 0)),
            pl.BlockSpec((1, 1, tq, LANES), lambda b, g, i: (b, g, i, 0)),
        ],
        out_shape=[
            jax.ShapeDtypeStruct((B, G, HPG, S, dh), BF16),
            jax.ShapeDtypeStruct((B, G, S, LANES), BF16),
        ],
        compiler_params=_cparams(("parallel", "parallel", "parallel")),
    )(qh, kc, vc, overlap, g_cmp)


def _flash_init(m_ref, l_ref, acc_ref):
    m_ref[...] = jnp.full(m_ref.shape, -jnp.inf, F32)
    l_ref[...] = jnp.zeros(l_ref.shape, F32)
    acc_ref[...] = jnp.zeros(acc_ref.shape, F32)


def _flash_update(h, s, v, m_ref, l_ref, acc_ref):
    m_prev = m_ref[h]
    m_new = jnp.maximum(m_prev, jnp.max(s, axis=-1, keepdims=True))
    alpha = jnp.exp(m_prev - m_new)
    p = jnp.exp(s - m_new)
    l_ref[h] = alpha * l_ref[h] + jnp.sum(p, axis=-1, keepdims=True)
    acc_ref[h] = alpha * acc_ref[h] + _dot(p.astype(BF16), v)
    m_ref[h] = m_new


def _block_onehot(first_block, n_rows, tk, block):
    r = lax.broadcasted_iota(jnp.int32, (n_rows, tk), 0)
    c = lax.broadcasted_iota(jnp.int32, (n_rows, tk), 1)
    return jnp.where(r == first_block + c // block, 1.0, 0.0).astype(BF16)


def _nsa_slc_body(q_ref, k_ref, v_ref, selb_ref, g_ref, o_ref, m_ref, l_ref, acc_ref, *, tq, tk):
    s0 = pl.program_id(2) * tq
    _flash_init(m_ref, l_ref, acc_ref)
    selb = selb_ref[0, 0]
    row = lax.broadcasted_iota(jnp.int32, (tq, tk), 0)
    col = lax.broadcasted_iota(jnp.int32, (tq, tk), 1)

    def kstep(ki, carry):
        k0 = pl.multiple_of(ki * tk, tk)
        k = k_ref[0, 0, pl.ds(k0, tk), :]
        v = v_ref[0, 0, pl.ds(k0, tk), :]
        e = _block_onehot(ki * (tk // NSA_SEL_BLOCK), LANES, tk, NSA_SEL_BLOCK)
        bias = _dot(selb, e)
        bias = jnp.where(k0 + col <= s0 + row, bias, MASK)
        for h in range(NSA_HPG):
            _flash_update(h, _dot_nt(q_ref[0, 0, h], k) + bias, v, m_ref, l_ref, acc_ref)
        return carry

    lax.fori_loop(0, (s0 + tq + tk - 1) // tk, kstep, 0)
    for h in range(NSA_HPG):
        o = acc_ref[h] / l_ref[h] * _sigmoid(g_ref[0, 0, h])
        o_ref[0, 0, h] = o.astype(o_ref.dtype)


def _nsa_win_body(q_ref, k_ref, v_ref, g_ref, o_ref, m_ref, l_ref, acc_ref, *, tq, tk):
    s0 = pl.program_id(2) * tq
    _flash_init(m_ref, l_ref, acc_ref)
    row = lax.broadcasted_iota(jnp.int32, (tq, tk), 0)
    col = lax.broadcasted_iota(jnp.int32, (tq, tk), 1)

    def kstep(ki, carry):
        k0 = pl.multiple_of(ki * tk, tk)
        k = k_ref[0, 0, pl.ds(k0, tk), :]
        v = v_ref[0, 0, pl.ds(k0, tk), :]
        d = (s0 + row) - (k0 + col)
        bias = jnp.where((d >= 0) & (d < NSA_WINDOW), 0.0, MASK)
        for h in range(NSA_HPG):
            _flash_update(h, _dot_nt(q_ref[0, 0, h], k) + bias, v, m_ref, l_ref, acc_ref)
        return carry

    first = jnp.maximum(s0 - (NSA_WINDOW - 1), 0) // tk
    lax.fori_loop(first, (s0 + tq + tk - 1) // tk, kstep, 0)
    for h in range(NSA_HPG):
        o = acc_ref[h] / l_ref[h] * _sigmoid(g_ref[0, 0, h])
        o_ref[0, 0, h] = o.astype(o_ref.dtype)


def _nsa_flash(body, qh, k, v, extra, g, tq, tk):
    B, G, HPG, S, dh = qh.shape
    tq = min(tq, S)
    tk = min(tk, S)
    in_specs = [
        pl.BlockSpec((1, 1, HPG, tq, dh), lambda b, g_, i: (b, g_, 0, i, 0)),
        pl.BlockSpec((1, 1, S, dh), lambda b, g_, i: (b, g_, 0, 0)),
        pl.BlockSpec((1, 1, S, dh), lambda b, g_, i: (b, g_, 0, 0)),
    ]
    args = [qh, k, v]
    if extra is not None:
        in_specs.append(pl.BlockSpec((1, 1, tq, LANES), lambda b, g_, i: (b, g_, i, 0)))
        args.append(extra)
    in_specs.append(pl.BlockSpec((1, 1, HPG, tq, 1), lambda b, g_, i: (b, g_, 0, i, 0)))
    args.append(g)
    return pl.pallas_call(
        functools.partial(body, tq=tq, tk=tk),
        grid=(B, G, S // tq),
        in_specs=in_specs,
        out_specs=pl.BlockSpec((1, 1, HPG, tq, dh), lambda b, g_, i: (b, g_, 0, i, 0)),
        out_shape=jax.ShapeDtypeStruct((B, G, HPG, S, dh), BF16),
        scratch_shapes=[
            pltpu.VMEM((HPG, tq, 1), F32),
            pltpu.VMEM((HPG, tq, 1), F32),
            pltpu.VMEM((HPG, tq, dh), F32),
        ],
        compiler_params=_cparams(("parallel", "parallel", "parallel")),
    )(*args)


def _moba_kmean_body(k_ref, hi_ref, lo_ref, *, nb):
    k = k_ref[0, 0]
    S = k.shape[0]
    avg = _block_onehot(0, LANES, S, MOBA_BLOCK)
    km = _dot(avg, k) * (1.0 / MOBA_BLOCK)
    hi, lo = _split_bf16(km)
    hi_ref[0, 0] = hi
    lo_ref[0, 0] = lo


def _moba_kmean(k, nb):
    B, H, S, dh = k.shape
    spec = pl.BlockSpec((1, 1, LANES, dh), lambda b, h: (b, h, 0, 0))
    return pl.pallas_call(
        functools.partial(_moba_kmean_body, nb=nb),
        grid=(B, H),
        in_specs=[pl.BlockSpec((1, 1, S, dh), lambda b, h: (b, h, 0, 0))],
        out_specs=[spec, spec],
        out_shape=[jax.ShapeDtypeStruct((B, H, LANES, dh), BF16)] * 2,
        compiler_params=_cparams(("parallel", "parallel")),
    )(k)


def _moba_body(q_ref, k_ref, v_ref, kh_ref, kl_ref, o_ref, selb_ref, m_ref, l_ref, acc_ref,
               *, tq, hb, nb, top_k):
    tk = MOBA_BLOCK
    s0 = pl.program_id(2) * tq
    cb = s0 // MOBA_BLOCK
    _flash_init(m_ref, l_ref, acc_ref)

    blk = lax.broadcasted_iota(jnp.int32, (tq, LANES), 1)
    for h in range(hb):
        q = q_ref[0, h]
        gsc = _dot_nt(q, kh_ref[0, h]) + _dot_nt(q, kl_ref[0, h])
        gsc = jnp.where(blk < cb, gsc, NEG)
        gsc = jnp.where(blk < nb, gsc, REMOVED)
        sel = _topk_mask(gsc, top_k) & (blk < cb)
        selb_ref[h] = jnp.where(sel, 0.0, MASK).astype(BF16)

    def kstep(ki, carry):
        k0 = pl.multiple_of(ki * tk, tk)
        e = _block_onehot(ki, LANES, tk, tk)
        for h in range(hb):
            k = k_ref[0, h, pl.ds(k0, tk), :]
            v = v_ref[0, h, pl.ds(k0, tk), :]
            s = _dot_nt(q_ref[0, h], k) + _dot(selb_ref[h], e)
            _flash_update(h, s, v, m_ref, l_ref, acc_ref)
        return carry

    lax.fori_loop(0, cb, kstep, 0)

    k0 = pl.multiple_of(cb * tk, tk)
    row = lax.broadcasted_iota(jnp.int32, (tq, tk), 0)
    col = lax.broadcasted_iota(jnp.int32, (tq, tk), 1)
    bias = jnp.where(k0 + col <= s0 + row, 0.0, MASK)
    for h in range(hb):
        k = k_ref[0, h, pl.ds(k0, tk), :]
        v = v_ref[0, h, pl.ds(k0, tk), :]
        _flash_update(h, _dot_nt(q_ref[0, h], k) + bias, v, m_ref, l_ref, acc_ref)
        o_ref[0, h] = (acc_ref[h] / l_ref[h]).astype(o_ref.dtype)


def _moba_attn(q, k, v, kh, kl, nb, top_k, tq=256, hb=4):
    B, H, S, dh = q.shape
    tq = min(tq, MOBA_BLOCK, S)
    assert MOBA_BLOCK % tq == 0 and S % MOBA_BLOCK == 0 and H % hb == 0 and nb <= LANES
    return pl.pallas_call(
        functools.partial(_moba_body, tq=tq, hb=hb, nb=nb, top_k=top_k),
        grid=(B, H // hb, S // tq),
        in_specs=[
            pl.BlockSpec((1, hb, tq, dh), lambda b, h, i: (b, h, i, 0)),
            pl.BlockSpec((1, hb, S, dh), lambda b, h, i: (b, h, 0, 0)),
            pl.BlockSpec((1, hb, S, dh), lambda b, h, i: (b, h, 0, 0)),
            pl.BlockSpec((1, hb, LANES, dh), lambda b, h, i: (b, h, 0, 0)),
            pl.BlockSpec((1, hb, LANES, dh), lambda b, h, i: (b, h, 0, 0)),
        ],
        out_specs=pl.BlockSpec((1, hb, tq, dh), lambda b, h, i: (b, h, i, 0)),
        out_shape=jax.ShapeDtypeStruct((B, H, S, dh), BF16),
        scratch_shapes=[
            pltpu.VMEM((hb, tq, LANES), BF16),
            pltpu.VMEM((hb, tq, 1), F32),
            pltpu.VMEM((hb, tq, 1), F32),
            pltpu.VMEM((hb, tq, dh), F32),
        ],
        compiler_params=_cparams(("parallel", "parallel", "parallel")),
    )(q, k, v, kh, kl)


def _layer_norm(r, g, b):
    mu = jnp.mean(r, axis=-1, keepdims=True)
    c = r - mu
    var = jnp.mean(c * c, axis=-1, keepdims=True)
    return c * lax.rsqrt(var + LN_EPS) * g + b


def _outproj_body(*refs, n_o):
    o_refs = refs[:n_o]
    w_ref, x_ref, g_ref, b_ref, y_ref, yb_ref = refs[n_o:]
    o = o_refs[0][...].astype(F32)
    for r in o_refs[1:]:
        o = o + r[...].astype(F32)
    mix = _dot(o.astype(BF16), w_ref[...])
    y = _layer_norm(DN_ALPHA * x_ref[...] + mix, g_ref[...], b_ref[...])
    y_ref[...] = y
    yb_ref[...] = y.astype(BF16)


def _outproj_ln(os_, w, x, g, b, tm=512):
    T, D = x.shape
    tm = min(tm, T)
    n_o = len(os_)
    row = pl.BlockSpec((tm, D), lambda i: (i, 0))
    vec = pl.BlockSpec((1, D), lambda i: (0, 0))
    return pl.pallas_call(
        functools.partial(_outproj_body, n_o=n_o),
        grid=(T // tm,),
        in_specs=[row] * n_o + [pl.BlockSpec((D, D), lambda i: (0, 0)), row, vec, vec],
        out_specs=[row, row],
        out_shape=[jax.ShapeDtypeStruct((T, D), F32), jax.ShapeDtypeStruct((T, D), BF16)],
        compiler_params=_cparams(("parallel",)),
    )(*os_, w, x, g.reshape(1, D), b.reshape(1, D))


def _router_body(x_ref, wh_ref, wl_ref, bias_ref, gate_ref):
    x_hi, x_lo = _split_bf16(x_ref[...])
    wh = wh_ref[...]
    logits = _dot(x_hi, wh) + _dot(x_lo, wh) + _dot(x_hi, wl_ref[...])
    scores = _sigmoid(logits)
    lane = lax.broadcasted_iota(jnp.int32, scores.shape, 1)
    lanef = lane.astype(F32)
    live = lane < N_EXPERTS
    biased = jnp.where(live, scores + bias_ref[...], REMOVED)

    def top2(mask):
        v = jnp.where(mask, biased, REMOVED)
        m1 = jnp.max(v, axis=-1, keepdims=True)
        i1 = jnp.min(jnp.where(v == m1, lanef, float(LANES)), axis=-1, keepdims=True)
        v2 = jnp.where(lanef == i1, REMOVED, v)
        m2 = jnp.max(v2, axis=-1, keepdims=True)
        i2 = jnp.min(jnp.where(v2 == m2, lanef, float(LANES)), axis=-1, keepdims=True)
        return m1 + m2, jnp.where((lanef == i1) | (lanef == i2), 1.0, 0.0)

    best, best_sel = top2(lane // EXPERTS_PER_GROUP == 0)
    for grp in range(1, N_GROUPS):
        score, sel = top2(lane // EXPERTS_PER_GROUP == grp)
        better = score > best
        best = jnp.where(better, score, best)
        best_sel = jnp.where(better, sel, best_sel)
    w = best_sel * scores
    gate_ref[...] = w / jnp.sum(w, axis=-1, keepdims=True)


def _router(x, router_w, router_bias, tm=512):
    T, D = x.shape
    tm = min(tm, T)
    wpad = jnp.zeros((D, LANES), F32).at[:, :N_EXPERTS].set(router_w)
    wh, wl = _split_bf16(wpad)
    bpad = jnp.zeros((1, LANES), F32).at[0, :N_EXPERTS].set(router_bias)
    return pl.pallas_call(
        _router_body,
        grid=(T // tm,),
        in_specs=[
            pl.BlockSpec((tm, D), lambda i: (i, 0)),
            pl.BlockSpec((D, LANES), lambda i: (0, 0)),
            pl.BlockSpec((D, LANES), lambda i: (0, 0)),
            pl.BlockSpec((1, LANES), lambda i: (0, 0)),
        ],
        out_specs=pl.BlockSpec((tm, LANES), lambda i: (i, 0)),
        out_shape=jax.ShapeDtypeStruct((T, LANES), F32),
        compiler_params=_cparams(("parallel",)),
    )(x, wh, wl, bpad)


def _experts_body(xb_ref, x_ref, gate_ref, wg_ref, wu_ref, wd_ref, g_ref, b_ref, y_ref, yb_ref, acc_ref):
    e = pl.program_id(1)

    @pl.when(e == 0)
    def _():
        acc_ref[...] = jnp.zeros(acc_ref.shape, F32)

    xb = xb_ref[...]
    a = _dot(xb, wg_ref[0])
    u = _dot(xb, wu_ref[0])
    lane = lax.broadcasted_iota(jnp.int32, gate_ref.shape, 1)
    gcol = jnp.sum(jnp.where(lane == e, gate_ref[...], 0.0), axis=-1, keepdims=True)
    h = a * _sigmoid(a) * u * gcol
    acc_ref[...] += _dot(h.astype(BF16), wd_ref[0])

    @pl.when(e == N_EXPERTS - 1)
    def _():
        y = _layer_norm(DN_ALPHA * x_ref[...] + acc_ref[...], g_ref[...], b_ref[...])
        y_ref[...] = y
        yb_ref[...] = y.astype(BF16)


def _experts_ln(xb, x, gate, wg, wu, wd, g, b, tm=1024):
    T, D = x.shape
    tm = min(tm, T)
    E, _, DE = wg.shape
    row = pl.BlockSpec((tm, D), lambda i, e: (i, 0))
    vec = pl.BlockSpec((1, D), lambda i, e: (0, 0))
    return pl.pallas_call(
        _experts_body,
        grid=(T // tm, E),
        in_specs=[
            row, row,
            pl.BlockSpec((tm, LANES), lambda i, e: (i, 0)),
            pl.BlockSpec((1, D, DE), lambda i, e: (e, 0, 0)),
            pl.BlockSpec((1, D, DE), lambda i, e: (e, 0, 0)),
            pl.BlockSpec((1, DE, D), lambda i, e: (e, 0, 0)),
            vec, vec,
        ],
        out_specs=[row, row],
        out_shape=[jax.ShapeDtypeStruct((T, D), F32), jax.ShapeDtypeStruct((T, D), BF16)],
        scratch_shapes=[pltpu.VMEM((tm, D), F32)],
        compiler_params=_cparams(("parallel", "arbitrary")),
    )(xb, x, gate, wg, wu, wd, g.reshape(1, D), b.reshape(1, D))


def _heads_major(t, B, S, n):
    return t.reshape(B, S, n, HEAD_DIM).transpose(0, 2, 1, 3)


def _tokens_major(o, B, S):
    n = o.shape[1]
    return o.transpose(0, 2, 1, 3).reshape(B * S, n * HEAD_DIM)


def _rope_tiled(S):
    cos, sin = _rope_tables(jnp.arange(S))
    reps = LANES // HALF
    return jnp.tile(cos, (1, reps)), jnp.tile(sin, (1, reps))


def _nsa_mixer(xb, B, S, w_in, cmp_k_w1, cmp_k_w2, cmp_v_w1, cmp_v_w2, cmp_k_pos, cmp_v_pos):
    G, HPG, KV = NSA_KV_GROUPS, NSA_HPG, NSA_KV_DIM
    L, STR, SB = NSA_CMP_LEN, NSA_CMP_STRIDE, NSA_SEL_BLOCK
    assert L == 2 * STR and S % SB == 0 and S // SB <= LANES
    n_cmp = (S - L) // STR + 1
    NC = S // STR
    n_sel = S // SB
    top_n = min(NSA_SEL_TOPN, n_sel)

    cos2, sin2 = _rope_tiled(S)
    n_main = D_MODEL + 6 * KV
    wb = w_in.astype(BF16)
    tn = 256
    modes = [2] * (D_MODEL // tn) + [0, 0, 1, 0, 1, 0]
    proj = _proj(xb, wb[:, :n_main], cos2, sin2, modes, S, tn=tn)
    wg = jnp.zeros((D_MODEL, LANES), BF16).at[:, :3 * N_HEADS].set(wb[:, n_main:])
    gates = _proj(xb, wg, cos2, sin2, [0], S, out_dtype=F32, tn=LANES)[:, :3 * N_HEADS]

    col = lambda i: proj[:, D_MODEL + i * KV: D_MODEL + (i + 1) * KV]
    qh = _heads_major(proj[:, :D_MODEL], B, S, N_HEADS).reshape(B, G, HPG, S, HEAD_DIM)
    kc_raw, vc_raw, ks, vs, kw, vw = [_heads_major(col(i), B, S, G) for i in range(6)]
    gcols = gates.reshape(B, S, 3, G, HPG).transpose(2, 0, 3, 4, 1)[..., None]

    ccos, csin = _rope_tables(jnp.arange(NC) * STR + (L - 1))
    ccos = jnp.concatenate([ccos, ccos], axis=1)
    csin = jnp.concatenate([csin, csin], axis=1)
    to_rows = lambda t: t.reshape(B * G, NC, STR * HEAD_DIM)
    kc = _compress(to_rows(kc_raw), cmp_k_w1, cmp_k_pos, cmp_k_w2, ccos, csin, True, n_cmp)
    vc = _compress(to_rows(vc_raw), cmp_v_w1, cmp_v_pos, cmp_v_w2, ccos, csin, False, n_cmp)
    kc = kc.reshape(B, G, NC, HEAD_DIM)
    vc = vc.reshape(B, G, NC, HEAD_DIM)

    ci = np.arange(NC)[:, None]
    sj = np.arange(LANES)[None, :]
    overlap = ((ci * STR < (sj + 1) * SB) & (ci * STR + L > sj * SB) & (ci < n_cmp) & (sj < n_sel))
    overlap = jnp.asarray(overlap.astype(np.float32), BF16)

    o_cmp, selb = _nsa_cmp(qh, kc, vc, overlap, gcols[0], n_cmp, n_sel, top_n)
    o_slc = _nsa_flash(_nsa_slc_body, qh, ks, vs, selb, gcols[1], tq=128, tk=512)
    o_win = _nsa_flash(_nsa_win_body, qh, kw, vw, None, gcols[2], tq=128, tk=256)
    flat = lambda o: _tokens_major(o.reshape(B, N_HEADS, S, HEAD_DIM), B, S)
    return [flat(o_cmp), flat(o_slc), flat(o_win)]


def _moba_mixer(xb, B, S, w_in):
    H = N_HEADS
    nb = S // MOBA_BLOCK
    top_k = min(MOBA_TOPK, nb)
    cos2, sin2 = _rope_tiled(S)
    tn = 256
    n_t = D_MODEL // tn
    modes = [2] * n_t + [1] * n_t + [0] * n_t
    proj = _proj(xb, w_in.astype(BF16), cos2, sin2, modes, S, tn=tn)
    q, k, v = [_heads_major(proj[:, i * D_MODEL:(i + 1) * D_MODEL], B, S, H) for i in range(3)]
    kh, kl = _moba_kmean(k, nb)
    o = _moba_attn(q, k, v, kh, kl, nb, top_k)
    return [_tokens_major(o, B, S)]


def kernel(x, nsa_w_in, nsa_w_out, nsa_cmp_k_w1, nsa_cmp_k_w2, nsa_cmp_v_w1, nsa_cmp_v_w2, nsa_cmp_k_pos, nsa_cmp_v_pos, moba_w_in, moba_w_out, router_w, router_bias, moe_w_gate, moe_w_up, moe_w_down, ln_g, ln_b):
    B, S, D = x.shape
    xf = x.reshape(B * S, D)
    xb = xf.astype(BF16)
    for layer in range(DEPTH):
        j = layer // 2
        if layer % 2 == 0:
            os_ = _nsa_mixer(xb, B, S, nsa_w_in[j], nsa_cmp_k_w1[j], nsa_cmp_k_w2[j], nsa_cmp_v_w1[j],
                             nsa_cmp_v_w2[j], nsa_cmp_k_pos[j], nsa_cmp_v_pos[j])
            w_out = nsa_w_out[j]
        else:
            os_ = _moba_mixer(xb, B, S, moba_w_in[j])
            w_out = moba_w_out[j]
        xf, xb = _outproj_ln(os_, w_out.astype(BF16), xf, ln_g[layer, 0], ln_b[layer, 0])
        gate = _router(xf, router_w, router_bias)
        xf, xb = _experts_ln(xb, xf, gate, moe_w_gate[layer].astype(BF16), moe_w_up[layer].astype(BF16),
                             moe_w_down[layer].astype(BF16), ln_g[layer, 1], ln_b[layer, 1])
    return xf.reshape(B, S, D)
```

```python
import functools

import jax
import jax.numpy as jnp
import numpy as np
from jax import lax
from jax.experimental import pallas as pl
from jax.experimental.pallas import tpu as pltpu

F32 = jnp.float32
BF16 = jnp.bfloat16

D_MODEL = 1024
N_HEADS = 16
HEAD_DIM = 64
HALF = HEAD_DIM // 2
ROPE_THETA = 10000.0
DEPTH = 2
DN_ALPHA = (2 * DEPTH) ** 0.25
LN_EPS = 1e-5
NEG = -1e30
FORCE = 1e9
MASK = -1e30
REMOVED = -3.0e38
LANES = 128
Q_SCALE_LOG2 = float(HEAD_DIM ** -0.5 * np.log2(np.e))
ROW_CHUNK = 128

NSA_KV_GROUPS = 4
NSA_HPG = N_HEADS // NSA_KV_GROUPS
NSA_KV_DIM = NSA_KV_GROUPS * HEAD_DIM
NSA_CMP_LEN = 32
NSA_CMP_STRIDE = 16
NSA_SEL_BLOCK = 64
NSA_SEL_TOPN = 16
NSA_WINDOW = 512

MOBA_BLOCK = 256
MOBA_TOPK = 3

N_EXPERTS = 16
N_GROUPS = 4
EXPERTS_PER_GROUP = N_EXPERTS // N_GROUPS
D_EXPERT = 512

VMEM_LIMIT = 48 * 1024 * 1024


def _cparams(sem):
    return pltpu.CompilerParams(dimension_semantics=sem, vmem_limit_bytes=VMEM_LIMIT)


def _dot(a, b):
    return jnp.dot(a, b, preferred_element_type=F32)


def _split_bf16(x):
    hi = x.astype(BF16)
    lo = (x - hi.astype(F32)).astype(BF16)
    return hi, lo


def _sigmoid(x):
    return 1.0 / (1.0 + jnp.exp(-x))


def _proj_body(mode_ref, x_ref, w_ref, cos_ref, sin_ref, o_ref, *, tn):
    j = pl.program_id(0)
    mode = mode_ref[j]
    acc = _dot(x_ref[...], w_ref[...])

    @pl.when(mode == 0)
    def _():
        o_ref[...] = acc.astype(o_ref.dtype)

    @pl.when(mode != 0)
    def _():
        sc = jnp.where(mode == 2, Q_SCALE_LOG2, 1.0).astype(F32)
        cos = cos_ref[...] * sc
        sin = sin_ref[...] * sc
        lane = lax.broadcasted_iota(jnp.int32, cos.shape, 1)
        lower = (lane % HEAD_DIM) < HALF
        for c in range(tn // LANES):
            a = acc[:, c * LANES:(c + 1) * LANES]
            up = pltpu.roll(a, LANES - HALF, 1)
            dn = pltpu.roll(a, HALF, 1)
            partner = jnp.where(lower, -up, dn)
            o_ref[:, c * LANES:(c + 1) * LANES] = (a * cos + partner * sin).astype(o_ref.dtype)


def _proj(xb, w, cos2, sin2, modes, seq, out_dtype=BF16, tm=512, tn=256):
    T, K = xb.shape
    N = w.shape[1]
    tm = min(tm, seq)
    assert T % tm == 0 and N % tn == 0 and seq % tm == 0 and len(modes) == N // tn
    n_pos = seq // tm
    grid_spec = pltpu.PrefetchScalarGridSpec(
        num_scalar_prefetch=1,
        grid=(N // tn, T // tm),
        in_specs=[
            pl.BlockSpec((tm, K), lambda j, i, m: (i, 0)),
            pl.BlockSpec((K, tn), lambda j, i, m: (0, j)),
            pl.BlockSpec((tm, LANES), lambda j, i, m: (i % n_pos, 0)),
            pl.BlockSpec((tm, LANES), lambda j, i, m: (i % n_pos, 0)),
        ],
        out_specs=pl.BlockSpec((tm, tn), lambda j, i, m: (i, j)),
    )
    return pl.pallas_call(
        functools.partial(_proj_body, tn=tn),
        grid_spec=grid_spec,
        out_shape=jax.ShapeDtypeStruct((T, N), out_dtype),
        compiler_params=_cparams(("parallel", "parallel")),
    )(jnp.asarray(modes, jnp.int32), xb, w, cos2, sin2)


def _rope_tables(pos):
    inv = 1.0 / (ROPE_THETA ** (jnp.arange(0, HEAD_DIM, 2, dtype=F32) / HEAD_DIM))
    ang = pos.astype(F32)[:, None] * inv[None, :]
    return jnp.cos(ang), jnp.sin(ang)


def _gelu_tanh(x):
    c = np.float32(np.sqrt(2.0 / np.pi))
    return 0.5 * x * (1.0 + jnp.tanh(c * (x + 0.044715 * (x * x * x))))


def _compress_body(r_ref, w1_ref, pos_ref, w2_ref, w2r_ref, cos_ref, sin_ref, o_ref, *, rope, n_cmp):
    r = r_ref[0]
    nc = r.shape[0]
    half = NSA_CMP_STRIDE * HEAD_DIM
    a = _dot(r, w1_ref[0])
    b = _dot(r, w1_ref[1])
    pos = pos_ref[...]
    pb = _dot(pos[:, :half], w1_ref[0]) + _dot(pos[:, half:], w1_ref[1])
    b_next = pltpu.roll(b, nc - 1, 0)
    h = _gelu_tanh(a + b_next + pb[0:1, :]).astype(BF16)
    o = _dot(h, w2_ref[...])
    if rope:
        o = o * cos_ref[...] + _dot(h, w2r_ref[...]) * sin_ref[...]
    row = lax.broadcasted_iota(jnp.int32, o.shape, 0)
    o_ref[0] = jnp.where(row < n_cmp, o, 0.0).astype(o_ref.dtype)


def _compress(r, w1, pos, w2, cos_c, sin_c, rope, n_cmp):
    BG, NC, K = r.shape
    hidden = w1.shape[1]
    w1s = w1.astype(BF16).reshape(2, K, hidden)
    pos8 = jnp.zeros((8, 2 * K), BF16).at[0].set(pos.reshape(-1).astype(BF16))
    w2r = jnp.concatenate([-w2[:, HALF:], w2[:, :HALF]], axis=1).astype(BF16)
    full = lambda shape: pl.BlockSpec(shape, lambda i: (0,) * len(shape))
    return pl.pallas_call(
        functools.partial(_compress_body, rope=rope, n_cmp=n_cmp),
        grid=(BG,),
        in_specs=[
            pl.BlockSpec((1, NC, K), lambda i: (i, 0, 0)),
            full((2, K, hidden)),
            full((8, 2 * K)),
            full((hidden, HEAD_DIM)),
            full((hidden, HEAD_DIM)),
            full((NC, HEAD_DIM)),
            full((NC, HEAD_DIM)),
        ],
        out_specs=pl.BlockSpec((1, NC, HEAD_DIM), lambda i: (i, 0, 0)),
        out_shape=jax.ShapeDtypeStruct((BG, NC, HEAD_DIM), BF16),
        compiler_params=_cparams(("parallel",)),
    )(r, w1s, pos8, w2.astype(BF16), w2r, cos_c, sin_c)


def _topk_mask(v, k):
    lane = lax.broadcasted_iota(jnp.int32, v.shape, 1).astype(F32)

    def step(_, carry):
        v, sel = carry
        m = jnp.max(v, axis=-1, keepdims=True)
        idx = jnp.min(jnp.where(v == m, lane, float(LANES)), axis=-1, keepdims=True)
        hit = lane == idx
        return jnp.where(hit, REMOVED, v), jnp.where(hit, 1.0, sel)

    _, sel = lax.fori_loop(0, k, step, (v, jnp.zeros(v.shape, F32)), unroll=True)
    return sel > 0.5


def _topk_mask_t(v, k):
    idx = lax.broadcasted_iota(jnp.int32, v.shape, 0).astype(F32)

    def step(_, cur):
        m = jnp.max(cur, axis=0, keepdims=True)
        first = jnp.min(jnp.where(cur == m, idx, float(LANES)), axis=0, keepdims=True)
        return jnp.where(idx == first, REMOVED, cur)

    return lax.fori_loop(0, k, step, v, unroll=True) != v


def _nsa_cmp_body(q_ref, kct_ref, vc_ref, ov_ref, g_ref, o_ref, selb_ref, *, tq, n_cmp, n_sel, top_n):
    s0 = pl.program_id(2) * tq
    kct = kct_ref[0, 0]
    vc = vc_ref[0, 0]
    nc = vc.shape[0]
    tpos = s0 + lax.broadcasted_iota(jnp.int32, (tq, nc), 0)
    nidx = lax.broadcasted_iota(jnp.int32, (tq, nc), 1)
    cmask = (nidx * NSA_CMP_STRIDE + (NSA_CMP_LEN - 1) <= tpos) & (nidx < n_cmp)
    psum = jnp.zeros((tq, nc), F32)
    for h in range(NSA_HPG):
        s = jnp.where(cmask, _dot(q_ref[0, 0, h], kct), NEG)
        m = jnp.max(s, axis=-1, keepdims=True)
        e = jnp.where(cmask, jnp.exp2(s - m), 0.0)
        l = jnp.sum(e, axis=-1, keepdims=True)
        p = e / jnp.maximum(l, 1e-30)
        psum = psum + p
        o = _dot(p.astype(BF16), vc) * _sigmoid(g_ref[0, 0, h])
        o_ref[0, 0, h] = o.astype(o_ref.dtype)

    p_hi, p_lo = _split_bf16(psum)
    imp = _dot(p_hi, ov_ref[...]) + _dot(p_lo, ov_ref[...])
    blk = lax.broadcasted_iota(jnp.int32, imp.shape, 1)
    jq = (s0 + lax.broadcasted_iota(jnp.int32, imp.shape, 0)) // NSA_SEL_BLOCK
    forced = (blk == 0) | (blk == jq) | (blk == jq - 1)
    imp = jnp.where(blk > jq, NEG, jnp.where(forced, FORCE, imp))
    imp = jnp.where(blk < n_sel, imp, REMOVED)
    sel_t = _topk_mask_t(imp.T, top_n)
    selb_ref[0, 0] = jnp.where(sel_t, 0.0, MASK).T.astype(selb_ref.dtype)


def _nsa_cmp(qp, kct, vc, overlap, g_cmp, n_cmp, n_sel, top_n, tq=256):
    B, G, HPG, S, _ = qp.shape
    NC = vc.shape[2]
    tq = min(tq, S)
    return pl.pallas_call(
        functools.partial(_nsa_cmp_body, tq=tq, n_cmp=n_cmp, n_sel=n_sel, top_n=top_n),
        grid=(B, G, S // tq),
        in_specs=[
            pl.BlockSpec((1, 1, HPG, tq, LANES), lambda b, g, i: (b, g, 0, i, 0)),
            pl.BlockSpec((1, 1, LANES, NC), lambda b, g, i: (b, g, 0, 0)),
            pl.BlockSpec((1, 1, NC, HEAD_DIM), lambda b, g, i: (b, g, 0, 0)),
            pl.BlockSpec((NC, LANES), lambda b, g, i: (0, 0)),
            pl.BlockSpec((1, 1, HPG, tq, 1), lambda b, g, i: (b, g, 0, i, 0)),
        ],
        out_specs=[
            pl.BlockSpec((1, 1, HPG, tq, HEAD_DIM), lambda b, g, i: (b, g, 0, i, 0)),
            pl.BlockSpec((1, 1, tq, LANES), lambda b, g, i: (b, g, i, 0)),
        ],
        out_shape=[
            jax.ShapeDtypeStruct((B, G, HPG, S, HEAD_DIM), BF16),
            jax.ShapeDtypeStruct((B, G, S, LANES), BF16),
        ],
        compiler_params=_cparams(("parallel", "parallel", "parallel")),
    )(qp, kct, vc, overlap, g_cmp)


def _flash_init(m_ref, acc_ref):
    m_ref[...] = jnp.full(m_ref.shape, -jnp.inf, F32)
    acc_ref[...] = jnp.zeros(acc_ref.shape, F32)


def _flash_step(jobs, s_ref, m_ref, alpha_ref, acc_ref):
    for r0, q, kt, _, mask in jobs:
        rows = q.shape[0]
        s = _dot(q, kt)
        if mask is not None:
            s = jnp.where(mask, s, MASK)
        s_ref[r0:r0 + rows, :] = s
        cm = s[:, :LANES]
        for c in range(1, s.shape[1] // LANES):
            cm = jnp.maximum(cm, s[:, c * LANES:(c + 1) * LANES])
        m_prev = m_ref[r0:r0 + rows, :]
        m_new = jnp.maximum(m_prev, jnp.max(cm, axis=-1, keepdims=True))
        alpha_ref[r0:r0 + rows, :] = jnp.exp2(m_prev - m_new)
        m_ref[r0:r0 + rows, :] = m_new
    for r0, q, _, v, _ in jobs:
        for r in range(r0, r0 + q.shape[0], ROW_CHUNK):
            rows = min(ROW_CHUNK, r0 + q.shape[0] - r)
            m = m_ref[r:r + rows, :]
            p = jnp.concatenate(
                [jnp.exp2(s_ref[r:r + rows, c * LANES:(c + 1) * LANES] - m)
                 for c in range(s_ref.shape[1] // LANES)], axis=1).astype(BF16)
            acc_ref[r:r + rows, :] = alpha_ref[r:r + rows, :] * acc_ref[r:r + rows, :] + _dot(p, v)


def _flash_scratch(rows, tk):
    return [pltpu.VMEM((rows, tk), F32), pltpu.VMEM((rows, LANES), F32),
            pltpu.VMEM((rows, LANES), F32), pltpu.VMEM((rows, LANES), F32)]


def _flash_out(acc_ref, r0, rows, gate):
    acc = acc_ref[r0:r0 + rows, :]
    return acc[:, :HEAD_DIM] / acc[:, HEAD_DIM:HEAD_DIM + 1] * gate


def _tile_iotas(rows, tk):
    return (lax.broadcasted_iota(jnp.int32, (rows, tk), 0),
            lax.broadcasted_iota(jnp.int32, (rows, tk), 1))


def _nsa_slc_body(q_ref, kt_ref, v_ref, e_ref, selb_ref, g_ref, o_ref, qa_ref, s_ref, m_ref, alpha_ref,
                  acc_ref, *, tq, tk):
    s0 = pl.program_id(2) * tq
    _flash_init(m_ref, acc_ref)
    selb = selb_ref[0, 0]
    for h in range(NSA_HPG):
        qa_ref[h * tq:(h + 1) * tq, :] = jnp.concatenate([selb, q_ref[0, 0, h]], axis=1)

    def tile(ki, causal):
        k0 = pl.multiple_of(ki * tk, tk)
        kt = jnp.concatenate([e_ref[ki], kt_ref[0, 0, ki]], axis=0)
        v = v_ref[0, 0, pl.ds(k0, tk), :]
        row, col = _tile_iotas(tq, tk)
        mask = (k0 + col <= s0 + row) if causal else None
        jobs = [(h * tq, qa_ref[h * tq:(h + 1) * tq, :], kt, v, mask) for h in range(NSA_HPG)]
        _flash_step(jobs, s_ref, m_ref, alpha_ref, acc_ref)

    last = (s0 + tq - 1) // tk

    def full_tile(ki, carry):
        tile(ki, False)
        return carry

    lax.fori_loop(0, last, full_tile, 0)
    tile(last, True)
    for h in range(NSA_HPG):
        o_ref[0, 0, h] = _flash_out(acc_ref, h * tq, tq, _sigmoid(g_ref[0, 0, h])).astype(o_ref.dtype)


def _nsa_slc(qp, kt, va, e, selb, g, tq, tk):
    B, G, HPG, S, _ = qp.shape
    nk = S // tk
    assert tk % tq == 0 and kt.shape == (B, G, nk, LANES, tk) and e.shape == (nk, LANES, tk)
    return pl.pallas_call(
        functools.partial(_nsa_slc_body, tq=tq, tk=tk),
        grid=(B, G, S // tq),
        in_specs=[
            pl.BlockSpec((1, 1, HPG, tq, LANES), lambda b, g_, i: (b, g_, 0, i, 0)),
            pl.BlockSpec((1, 1, nk, LANES, tk), lambda b, g_, i: (b, g_, 0, 0, 0)),
            pl.BlockSpec((1, 1, S, LANES), lambda b, g_, i: (b, g_, 0, 0)),
            pl.BlockSpec((nk, LANES, tk), lambda b, g_, i: (0, 0, 0)),
            pl.BlockSpec((1, 1, tq, LANES), lambda b, g_, i: (b, g_, i, 0)),
            pl.BlockSpec((1, 1, HPG, tq, 1), lambda b, g_, i: (b, g_, 0, i, 0)),
        ],
        out_specs=pl.BlockSpec((1, 1, HPG, tq, HEAD_DIM), lambda b, g_, i: (b, g_, 0, i, 0)),
        out_shape=jax.ShapeDtypeStruct((B, G, HPG, S, HEAD_DIM), BF16),
        scratch_shapes=[pltpu.VMEM((HPG * tq, 2 * LANES), BF16)] + _flash_scratch(HPG * tq, tk),
        compiler_params=_cparams(("parallel", "parallel", "parallel")),
    )(qp, kt, va, e, selb, g)


def _nsa_win_body(q_ref, kt_ref, v_ref, g_ref, o_ref, s_ref, m_ref, alpha_ref, acc_ref, *, tq):
    qi = pl.program_id(2)
    _flash_init(m_ref, acc_ref)
    row, col = _tile_iotas(tq, tq)
    for j in range(3):
        ki = qi - 2 + j

        @pl.when(ki >= 0)
        def _():
            kt = kt_ref[0, 0, ki]
            v = v_ref[0, 0, pl.ds(pl.multiple_of(ki * tq, tq), tq), :]
            mask = (col > row) if j == 0 else ((col <= row) if j == 2 else None)
            jobs = [(h * tq, q_ref[0, 0, h], kt, v, mask) for h in range(NSA_HPG)]
            _flash_step(jobs, s_ref, m_ref, alpha_ref, acc_ref)

    for h in range(NSA_HPG):
        o_ref[0, 0, h] = _flash_out(acc_ref, h * tq, tq, _sigmoid(g_ref[0, 0, h])).astype(o_ref.dtype)


def _nsa_win(qp, kt, va, g, tq):
    B, G, HPG, S, _ = qp.shape
    nk = S // tq
    assert NSA_WINDOW == 2 * tq and kt.shape == (B, G, nk, LANES, tq)
    return pl.pallas_call(
        functools.partial(_nsa_win_body, tq=tq),
        grid=(B, G, S // tq),
        in_specs=[
            pl.BlockSpec((1, 1, HPG, tq, LANES), lambda b, g_, i: (b, g_, 0, i, 0)),
            pl.BlockSpec((1, 1, nk, LANES, tq), lambda b, g_, i: (b, g_, 0, 0, 0)),
            pl.BlockSpec((1, 1, S, LANES), lambda b, g_, i: (b, g_, 0, 0)),
            pl.BlockSpec((1, 1, HPG, tq, 1), lambda b, g_, i: (b, g_, 0, i, 0)),
        ],
        out_specs=pl.BlockSpec((1, 1, HPG, tq, HEAD_DIM), lambda b, g_, i: (b, g_, 0, i, 0)),
        out_shape=jax.ShapeDtypeStruct((B, G, HPG, S, HEAD_DIM), BF16),
        scratch_shapes=_flash_scratch(HPG * tq, tq),
        compiler_params=_cparams(("parallel", "parallel", "parallel")),
    )(qp, kt, va, g)


def _moba_kmean_body(kt_ref, et_ref, hi_ref, lo_ref):
    nk = kt_ref.shape[2]
    km = jnp.zeros((LANES, LANES), F32)
    for ki in range(nk):
        km = km + _dot(kt_ref[0, 0, ki], et_ref[ki])
    hi, lo = _split_bf16(km * (1.0 / MOBA_BLOCK))
    hi_ref[0, 0] = hi
    lo_ref[0, 0] = lo


def _moba_kmean(kt, et):
    B, H, nk, _, tk = kt.shape
    spec = pl.BlockSpec((1, 1, LANES, LANES), lambda b, h: (b, h, 0, 0))
    return pl.pallas_call(
        _moba_kmean_body,
        grid=(B, H),
        in_specs=[
            pl.BlockSpec((1, 1, nk, LANES, tk), lambda b, h: (b, h, 0, 0, 0)),
            pl.BlockSpec((nk, tk, LANES), lambda b, h: (0, 0, 0)),
        ],
        out_specs=[spec, spec],
        out_shape=[jax.ShapeDtypeStruct((B, H, LANES, LANES), BF16)] * 2,
        compiler_params=_cparams(("parallel", "parallel")),
    )(kt, et)


def _moba_body(q_ref, kt_ref, v_ref, e_ref, kh_ref, kl_ref, o_ref, qa_ref, s_ref, m_ref, alpha_ref,
               acc_ref, *, tq, tk, hb, nb, top_k):
    s0 = pl.program_id(2) * tq
    cb = s0 // MOBA_BLOCK
    _flash_init(m_ref, acc_ref)

    blk = lax.broadcasted_iota(jnp.int32, (tq, LANES), 1)
    for h in range(hb):
        q = q_ref[0, h]
        gsc = _dot(q, kh_ref[0, h]) + _dot(q, kl_ref[0, h])
        gsc = jnp.where(blk < cb, gsc, NEG)
        gsc = jnp.where(blk < nb, gsc, REMOVED)
        sel = (_topk_mask(gsc, top_k) & (blk < cb)) | (blk == cb)
        qa_ref[h * tq:(h + 1) * tq, :] = jnp.concatenate(
            [jnp.where(sel, 0.0, MASK).astype(BF16), q], axis=1)

    def tile(ki, causal):
        k0 = pl.multiple_of(ki * tk, tk)
        e = e_ref[ki]
        row, col = _tile_iotas(tq, tk)
        mask = (k0 + col <= s0 + row) if causal else None
        jobs = [(h * tq, qa_ref[h * tq:(h + 1) * tq, :], jnp.concatenate([e, kt_ref[0, h, ki]], axis=0),
                 v_ref[0, h, pl.ds(k0, tk), :], mask) for h in range(hb)]
        _flash_step(jobs, s_ref, m_ref, alpha_ref, acc_ref)

    last = (cb * MOBA_BLOCK) // tk

    def full_tile(ki, carry):
        tile(ki, False)
        return carry

    lax.fori_loop(0, last, full_tile, 0)
    tile(last, True)
    for h in range(hb):
        o_ref[0, h] = _flash_out(acc_ref, h * tq, tq, 1.0).astype(o_ref.dtype)


def _moba_attn(qp, kt, va, e, kh, kl, nb, top_k, tq, tk, hb=2):
    B, H, S, _ = qp.shape
    nk = S // tk
    assert MOBA_BLOCK % tq == 0 and tk % MOBA_BLOCK == 0 and H % hb == 0 and nb <= LANES
    return pl.pallas_call(
        functools.partial(_moba_body, tq=tq, tk=tk, hb=hb, nb=nb, top_k=top_k),
        grid=(B, H // hb, S // tq),
        in_specs=[
            pl.BlockSpec((1, hb, tq, LANES), lambda b, h, i: (b, h, i, 0)),
            pl.BlockSpec((1, hb, nk, LANES, tk), lambda b, h, i: (b, h, 0, 0, 0)),
            pl.BlockSpec((1, hb, S, LANES), lambda b, h, i: (b, h, 0, 0)),
            pl.BlockSpec((nk, LANES, tk), lambda b, h, i: (0, 0, 0)),
            pl.BlockSpec((1, hb, LANES, LANES), lambda b, h, i: (b, h, 0, 0)),
            pl.BlockSpec((1, hb, LANES, LANES), lambda b, h, i: (b, h, 0, 0)),
        ],
        out_specs=pl.BlockSpec((1, hb, tq, HEAD_DIM), lambda b, h, i: (b, h, i, 0)),
        out_shape=jax.ShapeDtypeStruct((B, H, S, HEAD_DIM), BF16),
        scratch_shapes=[pltpu.VMEM((hb * tq, 2 * LANES), BF16)] + _flash_scratch(hb * tq, tk),
        compiler_params=_cparams(("parallel", "parallel", "parallel")),
    )(qp, kt, va, e, kh, kl)


def _layer_norm(r, g, b):
    mu = jnp.mean(r, axis=-1, keepdims=True)
    c = r - mu
    var = jnp.mean(c * c, axis=-1, keepdims=True)
    return c * lax.rsqrt(var + LN_EPS) * g + b


def _outproj_body(*refs, n_o):
    o_refs = refs[:n_o]
    w_ref, x_ref, g_ref, b_ref, y_ref, yb_ref = refs[n_o:]
    o = o_refs[0][...].astype(F32)
    for r in o_refs[1:]:
        o = o + r[...].astype(F32)
    mix = _dot(o.astype(BF16), w_ref[...])
    y = _layer_norm(DN_ALPHA * x_ref[...] + mix, g_ref[...], b_ref[...])
    y_ref[...] = y
    yb_ref[...] = y.astype(BF16)


def _outproj_ln(os_, w, x, g, b, tm=512):
    T, D = x.shape
    tm = min(tm, T)
    n_o = len(os_)
    row = pl.BlockSpec((tm, D), lambda i: (i, 0))
    vec = pl.BlockSpec((1, D), lambda i: (0, 0))
    return pl.pallas_call(
        functools.partial(_outproj_body, n_o=n_o),
        grid=(T // tm,),
        in_specs=[row] * n_o + [pl.BlockSpec((D, D), lambda i: (0, 0)), row, vec, vec],
        out_specs=[row, row],
        out_shape=[jax.ShapeDtypeStruct((T, D), F32), jax.ShapeDtypeStruct((T, D), BF16)],
        compiler_params=_cparams(("parallel",)),
    )(*os_, w, x, g.reshape(1, D), b.reshape(1, D))


def _router_body(x_ref, wh_ref, wl_ref, bias_ref, gate_ref):
    x_hi, x_lo = _split_bf16(x_ref[...])
    wh = wh_ref[...]
    logits = _dot(x_hi, wh) + _dot(x_lo, wh) + _dot(x_hi, wl_ref[...])
    scores = _sigmoid(logits)
    lane = lax.broadcasted_iota(jnp.int32, scores.shape, 1)
    lanef = lane.astype(F32)
    live = lane < N_EXPERTS
    biased = jnp.where(live, scores + bias_ref[...], REMOVED)

    def top2(mask):
        v = jnp.where(mask, biased, REMOVED)
        m1 = jnp.max(v, axis=-1, keepdims=True)
        i1 = jnp.min(jnp.where(v == m1, lanef, float(LANES)), axis=-1, keepdims=True)
        v2 = jnp.where(lanef == i1, REMOVED, v)
        m2 = jnp.max(v2, axis=-1, keepdims=True)
        i2 = jnp.min(jnp.where(v2 == m2, lanef, float(LANES)), axis=-1, keepdims=True)
        return m1 + m2, jnp.where((lanef == i1) | (lanef == i2), 1.0, 0.0)

    best, best_sel = top2(lane // EXPERTS_PER_GROUP == 0)
    for grp in range(1, N_GROUPS):
        score, sel = top2(lane // EXPERTS_PER_GROUP == grp)
        better = score > best
        best = jnp.where(better, score, best)
        best_sel = jnp.where(better, sel, best_sel)
    w = best_sel * scores
    gate_ref[...] = w / jnp.sum(w, axis=-1, keepdims=True)


def _router(x, router_w, router_bias, tm=512):
    T, D = x.shape
    tm = min(tm, T)
    wpad = jnp.zeros((D, LANES), F32).at[:, :N_EXPERTS].set(router_w)
    wh, wl = _split_bf16(wpad)
    bpad = jnp.zeros((1, LANES), F32).at[0, :N_EXPERTS].set(router_bias)
    return pl.pallas_call(
        _router_body,
        grid=(T // tm,),
        in_specs=[
            pl.BlockSpec((tm, D), lambda i: (i, 0)),
            pl.BlockSpec((D, LANES), lambda i: (0, 0)),
            pl.BlockSpec((D, LANES), lambda i: (0, 0)),
            pl.BlockSpec((1, LANES), lambda i: (0, 0)),
        ],
        out_specs=pl.BlockSpec((tm, LANES), lambda i: (i, 0)),
        out_shape=jax.ShapeDtypeStruct((T, LANES), F32),
        compiler_params=_cparams(("parallel",)),
    )(x, wh, wl, bpad)


def _experts_body(xb_ref, x_ref, gate_ref, wg_ref, wu_ref, wd_ref, g_ref, b_ref, y_ref, yb_ref, acc_ref):
    e = pl.program_id(1)

    @pl.when(e == 0)
    def _():
        acc_ref[...] = jnp.zeros(acc_ref.shape, F32)

    xb = xb_ref[...]
    a = _dot(xb, wg_ref[0])
    u = _dot(xb, wu_ref[0])
    lane = lax.broadcasted_iota(jnp.int32, gate_ref.shape, 1)
    gcol = jnp.sum(jnp.where(lane == e, gate_ref[...], 0.0), axis=-1, keepdims=True)
    h = a * _sigmoid(a) * u * gcol
    acc_ref[...] += _dot(h.astype(BF16), wd_ref[0])

    @pl.when(e == N_EXPERTS - 1)
    def _():
        y = _layer_norm(DN_ALPHA * x_ref[...] + acc_ref[...], g_ref[...], b_ref[...])
        y_ref[...] = y
        yb_ref[...] = y.astype(BF16)


def _experts_ln(xb, x, gate, wg, wu, wd, g, b, tm=1024):
    T, D = x.shape
    tm = min(tm, T)
    E, _, DE = wg.shape
    row = pl.BlockSpec((tm, D), lambda i, e: (i, 0))
    vec = pl.BlockSpec((1, D), lambda i, e: (0, 0))
    return pl.pallas_call(
        _experts_body,
        grid=(T // tm, E),
        in_specs=[
            row, row,
            pl.BlockSpec((tm, LANES), lambda i, e: (i, 0)),
            pl.BlockSpec((1, D, DE), lambda i, e: (e, 0, 0)),
            pl.BlockSpec((1, D, DE), lambda i, e: (e, 0, 0)),
            pl.BlockSpec((1, DE, D), lambda i, e: (e, 0, 0)),
            vec, vec,
        ],
        out_specs=[row, row],
        out_shape=[jax.ShapeDtypeStruct((T, D), F32), jax.ShapeDtypeStruct((T, D), BF16)],
        scratch_shapes=[pltpu.VMEM((tm, D), F32)],
        compiler_params=_cparams(("parallel", "arbitrary")),
    )(xb, x, gate, wg, wu, wd, g.reshape(1, D), b.reshape(1, D))


def _heads_major(t, B, S, n):
    return t.reshape(B, S, n, HEAD_DIM).transpose(0, 2, 1, 3)


def _tokens_major(o, B, S):
    n = o.shape[1]
    return o.transpose(0, 2, 1, 3).reshape(B * S, n * HEAD_DIM)


def _q_pad(t, B, S, n):
    return jnp.pad(_heads_major(t, B, S, n), ((0, 0), (0, 0), (0, 0), (0, LANES - HEAD_DIM)))


def _v_aug(t, B, S, n):
    v = _heads_major(t, B, S, n)
    ones = jnp.ones((B, n, S, 1), v.dtype)
    zeros = jnp.zeros((B, n, S, LANES - HEAD_DIM - 1), v.dtype)
    return jnp.concatenate([v, ones, zeros], axis=-1)


def _kt_pad(t, B, S, n):
    kt = t.reshape(B, S, n, HEAD_DIM).transpose(0, 2, 3, 1)
    return jnp.pad(kt, ((0, 0), (0, 0), (0, LANES - HEAD_DIM), (0, 0)))


def _kt_tiles(t, B, S, n, tk):
    kt = _kt_pad(t, B, S, n).reshape(B, n, LANES, S // tk, tk)
    return kt.transpose(0, 1, 3, 2, 4)


def _block_onehots(S, block, tk):
    key = np.arange(S).reshape(S // tk, 1, tk)
    r = np.arange(LANES).reshape(1, LANES, 1)
    return jnp.asarray((key // block == r).astype(np.float32), BF16)


def _rope_tiled(S):
    cos, sin = _rope_tables(jnp.arange(S))
    reps = LANES // HALF
    return jnp.tile(cos, (1, reps)), jnp.tile(sin, (1, reps))


def _nsa_mixer(xb, B, S, w_in, cmp_k_w1, cmp_k_w2, cmp_v_w1, cmp_v_w2, cmp_k_pos, cmp_v_pos):
    G, HPG, KV = NSA_KV_GROUPS, NSA_HPG, NSA_KV_DIM
    L, STR, SB = NSA_CMP_LEN, NSA_CMP_STRIDE, NSA_SEL_BLOCK
    assert L == 2 * STR and S % SB == 0 and S // SB <= LANES
    n_cmp = (S - L) // STR + 1
    NC = S // STR
    n_sel = S // SB
    top_n = min(NSA_SEL_TOPN, n_sel)

    cos2, sin2 = _rope_tiled(S)
    n_main = D_MODEL + 6 * KV
    wb = w_in.astype(BF16)
    tn = 256
    modes = [2] * (D_MODEL // tn) + [0, 0, 1, 0, 1, 0]
    proj = _proj(xb, wb[:, :n_main], cos2, sin2, modes, S, tn=tn)
    wg = jnp.zeros((D_MODEL, LANES), BF16).at[:, :3 * N_HEADS].set(wb[:, n_main:])
    gates = _proj(xb, wg, cos2, sin2, [0], S, out_dtype=F32, tn=LANES)[:, :3 * N_HEADS]

    col = lambda i: proj[:, D_MODEL + i * KV: D_MODEL + (i + 1) * KV]
    qp = _q_pad(proj[:, :D_MODEL], B, S, N_HEADS).reshape(B, G, HPG, S, LANES)
    gcols = gates.reshape(B, S, 3, G, HPG).transpose(2, 0, 3, 4, 1)[..., None]

    ccos, csin = _rope_tables(jnp.arange(NC) * STR + (L - 1))
    ccos = jnp.concatenate([ccos, ccos], axis=1)
    csin = jnp.concatenate([csin, csin], axis=1)
    to_rows = lambda t: _heads_major(t, B, S, G).reshape(B * G, NC, STR * HEAD_DIM)
    kc = _compress(to_rows(col(0)), cmp_k_w1, cmp_k_pos, cmp_k_w2, ccos, csin, True, n_cmp)
    vc = _compress(to_rows(col(1)), cmp_v_w1, cmp_v_pos, cmp_v_w2, ccos, csin, False, n_cmp)
    kct = jnp.pad(kc.reshape(B, G, NC, HEAD_DIM).transpose(0, 1, 3, 2),
                  ((0, 0), (0, 0), (0, LANES - HEAD_DIM), (0, 0)))
    vc = vc.reshape(B, G, NC, HEAD_DIM)

    ci = np.arange(NC)[:, None]
    sj = np.arange(LANES)[None, :]
    overlap = ((ci * STR < (sj + 1) * SB) & (ci * STR + L > sj * SB) & (ci < n_cmp) & (sj < n_sel))
    overlap = jnp.asarray(overlap.astype(np.float32), BF16)

    o_cmp, selb = _nsa_cmp(qp, kct, vc, overlap, gcols[0], n_cmp, n_sel, top_n)
    tk_s = min(512, S)
    o_slc = _nsa_slc(qp, _kt_tiles(col(2), B, S, G, tk_s), _v_aug(col(3), B, S, G),
                     _block_onehots(S, SB, tk_s), selb, gcols[1], tq=256, tk=tk_s)
    tq_w = NSA_WINDOW // 2
    o_win = _nsa_win(qp, _kt_tiles(col(4), B, S, G, tq_w), _v_aug(col(5), B, S, G), gcols[2], tq=tq_w)
    flat = lambda o: _tokens_major(o.reshape(B, N_HEADS, S, HEAD_DIM), B, S)
    return [flat(o_cmp), flat(o_slc), flat(o_win)]


def _moba_mixer(xb, B, S, w_in):
    H = N_HEADS
    nb = S // MOBA_BLOCK
    top_k = min(MOBA_TOPK, nb)
    cos2, sin2 = _rope_tiled(S)
    tn = 256
    n_t = D_MODEL // tn
    modes = [2] * n_t + [1] * n_t + [0] * n_t
    proj = _proj(xb, w_in.astype(BF16), cos2, sin2, modes, S, tn=tn)
    part = lambda i: proj[:, i * D_MODEL:(i + 1) * D_MODEL]
    tk = min(2 * MOBA_BLOCK, S)
    kt = _kt_tiles(part(1), B, S, H, tk)
    e = _block_onehots(S, MOBA_BLOCK, tk)
    kh, kl = _moba_kmean(kt, e.transpose(0, 2, 1))
    o = _moba_attn(_q_pad(part(0), B, S, H), kt, _v_aug(part(2), B, S, H), e, kh, kl, nb, top_k,
                   tq=MOBA_BLOCK, tk=tk)
    return [_tokens_major(o, B, S)]


def kernel(x, nsa_w_in, nsa_w_out, nsa_cmp_k_w1, nsa_cmp_k_w2, nsa_cmp_v_w1, nsa_cmp_v_w2, nsa_cmp_k_pos, nsa_cmp_v_pos, moba_w_in, moba_w_out, router_w, router_bias, moe_w_gate, moe_w_up, moe_w_down, ln_g, ln_b):
    B, S, D = x.shape
    xf = x.reshape(B * S, D)
    xb = xf.astype(BF16)
    for layer in range(DEPTH):
        j = layer // 2
        if layer % 2 == 0:
            os_ = _nsa_mixer(xb, B, S, nsa_w_in[j], nsa_cmp_k_w1[j], nsa_cmp_k_w2[j], nsa_cmp_v_w1[j],
                             nsa_cmp_v_w2[j], nsa_cmp_k_pos[j], nsa_cmp_v_pos[j])
            w_out = nsa_w_out[j]
        else:
            os_ = _moba_mixer(xb, B, S, moba_w_in[j])
            w_out = moba_w_out[j]
        xf, xb = _outproj_ln(os_, w_out.astype(BF16), xf, ln_g[layer, 0], ln_b[layer, 0])
        gate = _router(xf, router_w, router_bias)
        xf, xb = _experts_ln(xb, xf, gate, moe_w_gate[layer].astype(BF16), moe_w_up[layer].astype(BF16),
                             moe_w_down[layer].astype(BF16), ln_g[layer, 1], ln_b[layer, 1])
    return xf.reshape(B, S, D)
```

```python
import functools

import jax
import jax.numpy as jnp
import numpy as np
from jax import lax
from jax.experimental import pallas as pl
from jax.experimental.pallas import tpu as pltpu

F32 = jnp.float32
BF16 = jnp.bfloat16

D_MODEL = 1024
N_HEADS = 16
HEAD_DIM = 64
HALF = HEAD_DIM // 2
ROPE_THETA = 10000.0
DEPTH = 2
DN_ALPHA = (2 * DEPTH) ** 0.25
LN_EPS = 1e-5
NEG = -1e30
FORCE = 1e9
MASK = -1e30
REMOVED = -3.0e38
LANES = 128
Q_SCALE_LOG2 = float(HEAD_DIM ** -0.5 * np.log2(np.e))
ROW_CHUNK = 128

NSA_KV_GROUPS = 4
NSA_HPG = N_HEADS // NSA_KV_GROUPS
NSA_KV_DIM = NSA_KV_GROUPS * HEAD_DIM
NSA_CMP_LEN = 32
NSA_CMP_STRIDE = 16
NSA_SEL_BLOCK = 64
NSA_SEL_TOPN = 16
NSA_WINDOW = 512

MOBA_BLOCK = 256
MOBA_TOPK = 3

N_EXPERTS = 16
N_GROUPS = 4
EXPERTS_PER_GROUP = N_EXPERTS // N_GROUPS
D_EXPERT = 512

VMEM_LIMIT = 48 * 1024 * 1024


def _cparams(sem):
    return pltpu.CompilerParams(dimension_semantics=sem, vmem_limit_bytes=VMEM_LIMIT)


def _dot(a, b):
    return jnp.dot(a, b, preferred_element_type=F32)


def _split_bf16(x):
    hi = x.astype(BF16)
    lo = (x - hi.astype(F32)).astype(BF16)
    return hi, lo


def _sigmoid(x):
    return 1.0 / (1.0 + jnp.exp(-x))


def _proj_body(mode_ref, x_ref, w_ref, cos_ref, sin_ref, o_ref, *, tn):
    j = pl.program_id(0)
    mode = mode_ref[j]
    acc = _dot(x_ref[...], w_ref[...])

    @pl.when(mode == 0)
    def _():
        o_ref[...] = acc.astype(o_ref.dtype)

    @pl.when(mode != 0)
    def _():
        sc = jnp.where(mode == 2, Q_SCALE_LOG2, 1.0).astype(F32)
        cos = cos_ref[...] * sc
        sin = sin_ref[...] * sc
        lane = lax.broadcasted_iota(jnp.int32, cos.shape, 1)
        lower = (lane % HEAD_DIM) < HALF
        for c in range(tn // LANES):
            a = acc[:, c * LANES:(c + 1) * LANES]
            up = pltpu.roll(a, LANES - HALF, 1)
            dn = pltpu.roll(a, HALF, 1)
            partner = jnp.where(lower, -up, dn)
            o_ref[:, c * LANES:(c + 1) * LANES] = (a * cos + partner * sin).astype(o_ref.dtype)


def _proj(xb, w, cos2, sin2, modes, seq, out_dtype=BF16, tm=1024, tn=512):
    T, K = xb.shape
    N = w.shape[1]
    tm = min(tm, seq)
    assert T % tm == 0 and N % tn == 0 and seq % tm == 0 and len(modes) == N // tn
    n_pos = seq // tm
    grid_spec = pltpu.PrefetchScalarGridSpec(
        num_scalar_prefetch=1,
        grid=(N // tn, T // tm),
        in_specs=[
            pl.BlockSpec((tm, K), lambda j, i, m: (i, 0)),
            pl.BlockSpec((K, tn), lambda j, i, m: (0, j)),
            pl.BlockSpec((tm, LANES), lambda j, i, m: (i % n_pos, 0)),
            pl.BlockSpec((tm, LANES), lambda j, i, m: (i % n_pos, 0)),
        ],
        out_specs=pl.BlockSpec((tm, tn), lambda j, i, m: (i, j)),
    )
    return pl.pallas_call(
        functools.partial(_proj_body, tn=tn),
        grid_spec=grid_spec,
        out_shape=jax.ShapeDtypeStruct((T, N), out_dtype),
        compiler_params=_cparams(("parallel", "parallel")),
    )(jnp.asarray(modes, jnp.int32), xb, w, cos2, sin2)


def _rope_tables(pos):
    inv = 1.0 / (ROPE_THETA ** (jnp.arange(0, HEAD_DIM, 2, dtype=F32) / HEAD_DIM))
    ang = pos.astype(F32)[:, None] * inv[None, :]
    return jnp.cos(ang), jnp.sin(ang)


def _gelu_tanh(x):
    c = np.float32(np.sqrt(2.0 / np.pi))
    return 0.5 * x * (1.0 + jnp.tanh(c * (x + 0.044715 * (x * x * x))))


def _compress_body(r_ref, w1_ref, pos_ref, w2_ref, w2r_ref, cos_ref, sin_ref, o_ref, *, rope, n_cmp):
    r = r_ref[0]
    nc = r.shape[0]
    half = NSA_CMP_STRIDE * HEAD_DIM
    a = _dot(r, w1_ref[0])
    b = _dot(r, w1_ref[1])
    pos = pos_ref[...]
    pb = _dot(pos[:, :half], w1_ref[0]) + _dot(pos[:, half:], w1_ref[1])
    b_next = pltpu.roll(b, nc - 1, 0)
    h = _gelu_tanh(a + b_next + pb[0:1, :]).astype(BF16)
    o = _dot(h, w2_ref[...])
    if rope:
        o = o * cos_ref[...] + _dot(h, w2r_ref[...]) * sin_ref[...]
    row = lax.broadcasted_iota(jnp.int32, o.shape, 0)
    o_ref[0] = jnp.where(row < n_cmp, o, 0.0).astype(o_ref.dtype)


def _compress(r, w1, pos, w2, cos_c, sin_c, rope, n_cmp):
    BG, NC, K = r.shape
    hidden = w1.shape[1]
    w1s = w1.astype(BF16).reshape(2, K, hidden)
    pos8 = jnp.zeros((8, 2 * K), BF16).at[0].set(pos.reshape(-1).astype(BF16))
    w2r = jnp.concatenate([-w2[:, HALF:], w2[:, :HALF]], axis=1).astype(BF16)
    full = lambda shape: pl.BlockSpec(shape, lambda i: (0,) * len(shape))
    return pl.pallas_call(
        functools.partial(_compress_body, rope=rope, n_cmp=n_cmp),
        grid=(BG,),
        in_specs=[
            pl.BlockSpec((1, NC, K), lambda i: (i, 0, 0)),
            full((2, K, hidden)),
            full((8, 2 * K)),
            full((hidden, HEAD_DIM)),
            full((hidden, HEAD_DIM)),
            full((NC, HEAD_DIM)),
            full((NC, HEAD_DIM)),
        ],
        out_specs=pl.BlockSpec((1, NC, HEAD_DIM), lambda i: (i, 0, 0)),
        out_shape=jax.ShapeDtypeStruct((BG, NC, HEAD_DIM), BF16),
        compiler_params=_cparams(("parallel",)),
    )(r, w1s, pos8, w2.astype(BF16), w2r, cos_c, sin_c)


def _topk_mask(v, k):
    lane = lax.broadcasted_iota(jnp.int32, v.shape, 1).astype(F32)

    def step(_, carry):
        v, sel = carry
        m = jnp.max(v, axis=-1, keepdims=True)
        idx = jnp.min(jnp.where(v == m, lane, float(LANES)), axis=-1, keepdims=True)
        hit = lane == idx
        return jnp.where(hit, REMOVED, v), jnp.where(hit, 1.0, sel)

    _, sel = lax.fori_loop(0, k, step, (v, jnp.zeros(v.shape, F32)), unroll=True)
    return sel > 0.5


def _topk_mask_t(v, k):
    idx = lax.broadcasted_iota(jnp.int32, v.shape, 0).astype(F32)

    def step(_, cur):
        m = jnp.max(cur, axis=0, keepdims=True)
        first = jnp.min(jnp.where(cur == m, idx, float(LANES)), axis=0, keepdims=True)
        return jnp.where(idx == first, REMOVED, cur)

    return lax.fori_loop(0, k, step, v, unroll=True) != v


def _nsa_cmp_body(q_ref, kct_ref, vo_ref, g_ref, o_ref, selb_ref, s_ref, m_ref, *, tq, n_cmp, n_sel, top_n):
    s0 = pl.program_id(2) * tq
    kct = kct_ref[0, 0]
    vo = vo_ref[0, 0]
    nc = vo.shape[0]
    tpos = s0 + lax.broadcasted_iota(jnp.int32, (tq, nc), 0)
    nidx = lax.broadcasted_iota(jnp.int32, (tq, nc), 1)
    cmask = (nidx * NSA_CMP_STRIDE + (NSA_CMP_LEN - 1) <= tpos) & (nidx < n_cmp)
    for h in range(NSA_HPG):
        s = jnp.where(cmask, _dot(q_ref[0, 0, h], kct), NEG)
        s_ref[h * tq:(h + 1) * tq, :] = s
        cm = s[:, :LANES]
        for c in range(1, nc // LANES):
            cm = jnp.maximum(cm, s[:, c * LANES:(c + 1) * LANES])
        m_ref[h * tq:(h + 1) * tq, :] = jnp.broadcast_to(jnp.max(cm, axis=-1, keepdims=True), (tq, LANES))

    rc = min(ROW_CHUNK, tq)
    imps = []
    for r0 in range(0, tq, rc):
        imp = jnp.zeros((rc, LANES), F32)
        for h in range(NSA_HPG):
            r = h * tq + r0
            m = m_ref[r:r + rc, :]
            e = jnp.concatenate(
                [jnp.exp2(s_ref[r:r + rc, c * LANES:(c + 1) * LANES] - m) for c in range(nc // LANES)],
                axis=1).astype(BF16)
            res = _dot(e, vo)
            inv = jnp.where(m[:, :1] > 0.5 * NEG, 1.0 / res[:, HEAD_DIM:HEAD_DIM + 1], 0.0)
            o = res[:, :HEAD_DIM] * (inv * _sigmoid(g_ref[0, 0, h, r0:r0 + rc, :]))
            o_ref[0, 0, h, r0:r0 + rc, :] = o.astype(o_ref.dtype)
            imp = imp + res[:, LANES:] * inv
        imps.append(imp)
    imp = jnp.concatenate(imps, axis=0)

    blk = lax.broadcasted_iota(jnp.int32, imp.shape, 1)
    jq = (s0 + lax.broadcasted_iota(jnp.int32, imp.shape, 0)) // NSA_SEL_BLOCK
    forced = (blk == 0) | (blk == jq) | (blk == jq - 1)
    imp = jnp.where(blk > jq, NEG, jnp.where(forced, FORCE, imp))
    imp = jnp.where(blk < n_sel, imp, REMOVED)
    sel_t = _topk_mask_t(imp.T, top_n)
    selb_ref[0, 0] = jnp.where(sel_t, 0.0, MASK).T.astype(selb_ref.dtype)


def _nsa_cmp(qp, kct, vo, g_cmp, n_cmp, n_sel, top_n, tq=256):
    B, G, HPG, S, _ = qp.shape
    NC = vo.shape[2]
    tq = min(tq, S)
    assert NC % LANES == 0
    return pl.pallas_call(
        functools.partial(_nsa_cmp_body, tq=tq, n_cmp=n_cmp, n_sel=n_sel, top_n=top_n),
        grid=(B, G, S // tq),
        in_specs=[
            pl.BlockSpec((1, 1, HPG, tq, LANES), lambda b, g, i: (b, g, 0, i, 0)),
            pl.BlockSpec((1, 1, LANES, NC), lambda b, g, i: (b, g, 0, 0)),
            pl.BlockSpec((1, 1, NC, 2 * LANES), lambda b, g, i: (b, g, 0, 0)),
            pl.BlockSpec((1, 1, HPG, tq, 1), lambda b, g, i: (b, g, 0, i, 0)),
        ],
        out_specs=[
            pl.BlockSpec((1, 1, HPG, tq, HEAD_DIM), lambda b, g, i: (b, g, 0, i, 0)),
            pl.BlockSpec((1, 1, tq, LANES), lambda b, g, i: (b, g, i, 0)),
        ],
        out_shape=[
            jax.ShapeDtypeStruct((B, G, HPG, S, HEAD_DIM), BF16),
            jax.ShapeDtypeStruct((B, G, S, LANES), BF16),
        ],
        scratch_shapes=[pltpu.VMEM((HPG * tq, NC), F32), pltpu.VMEM((HPG * tq, LANES), F32)],
        compiler_params=_cparams(("parallel", "parallel", "parallel")),
    )(qp, kct, vo, g_cmp)


def _flash_init(m_ref, acc_ref):
    m_ref[...] = jnp.full(m_ref.shape, -jnp.inf, F32)
    acc_ref[...] = jnp.zeros(acc_ref.shape, F32)


def _flash_step(jobs, s_ref, m_ref, alpha_ref, acc_ref):
    for r0, q, kt, _, mask in jobs:
        rows = q.shape[0]
        s = _dot(q, kt)
        if mask is not None:
            s = jnp.where(mask, s, MASK)
        s_ref[r0:r0 + rows, :] = s
        cm = s[:, :LANES]
        for c in range(1, s.shape[1] // LANES):
            cm = jnp.maximum(cm, s[:, c * LANES:(c + 1) * LANES])
        m_prev = m_ref[r0:r0 + rows, :]
        m_new = jnp.maximum(m_prev, jnp.max(cm, axis=-1, keepdims=True))
        alpha_ref[r0:r0 + rows, :] = jnp.exp2(m_prev - m_new)
        m_ref[r0:r0 + rows, :] = m_new
    for r0, q, _, v, _ in jobs:
        for r in range(r0, r0 + q.shape[0], ROW_CHUNK):
            rows = min(ROW_CHUNK, r0 + q.shape[0] - r)
            m = m_ref[r:r + rows, :]
            p = jnp.concatenate(
                [jnp.exp2(s_ref[r:r + rows, c * LANES:(c + 1) * LANES] - m)
                 for c in range(s_ref.shape[1] // LANES)], axis=1).astype(BF16)
            acc_ref[r:r + rows, :] = alpha_ref[r:r + rows, :] * acc_ref[r:r + rows, :] + _dot(p, v)


def _flash_scratch(rows, tk):
    return [pltpu.VMEM((rows, tk), F32), pltpu.VMEM((rows, LANES), F32),
            pltpu.VMEM((rows, LANES), F32), pltpu.VMEM((rows, LANES), F32)]


def _flash_out(acc_ref, r0, rows, gate):
    acc = acc_ref[r0:r0 + rows, :]
    return acc[:, :HEAD_DIM] / acc[:, HEAD_DIM:HEAD_DIM + 1] * gate


def _tile_iotas(rows, tk):
    return (lax.broadcasted_iota(jnp.int32, (rows, tk), 0),
            lax.broadcasted_iota(jnp.int32, (rows, tk), 1))


def _nsa_slc_body(q_ref, kt_ref, v_ref, e_ref, selb_ref, g_ref, o_ref, qa_ref, s_ref, m_ref, alpha_ref,
                  acc_ref, *, tq, tk):
    s0 = pl.program_id(2) * tq
    _flash_init(m_ref, acc_ref)
    selb = selb_ref[0, 0]
    for h in range(NSA_HPG):
        qa_ref[h * tq:(h + 1) * tq, :] = jnp.concatenate([selb, q_ref[0, 0, h]], axis=1)

    def tile(ki, causal):
        k0 = pl.multiple_of(ki * tk, tk)
        kt = jnp.concatenate([e_ref[ki], kt_ref[0, 0, ki]], axis=0)
        v = v_ref[0, 0, pl.ds(k0, tk), :]
        row, col = _tile_iotas(tq, tk)
        mask = (k0 + col <= s0 + row) if causal else None
        jobs = [(h * tq, qa_ref[h * tq:(h + 1) * tq, :], kt, v, mask) for h in range(NSA_HPG)]
        _flash_step(jobs, s_ref, m_ref, alpha_ref, acc_ref)

    last = (s0 + tq - 1) // tk

    def full_tile(ki, carry):
        tile(ki, False)
        return carry

    lax.fori_loop(0, last, full_tile, 0)
    tile(last, True)
    for h in range(NSA_HPG):
        o_ref[0, 0, h] = _flash_out(acc_ref, h * tq, tq, _sigmoid(g_ref[0, 0, h])).astype(o_ref.dtype)


def _nsa_slc(qp, kt, va, e, selb, g, tq, tk):
    B, G, HPG, S, _ = qp.shape
    nk = S // tk
    assert tk % tq == 0 and kt.shape == (B, G, nk, LANES, tk) and e.shape == (nk, LANES, tk)
    return pl.pallas_call(
        functools.partial(_nsa_slc_body, tq=tq, tk=tk),
        grid=(B, G, S // tq),
        in_specs=[
            pl.BlockSpec((1, 1, HPG, tq, LANES), lambda b, g_, i: (b, g_, 0, i, 0)),
            pl.BlockSpec((1, 1, nk, LANES, tk), lambda b, g_, i: (b, g_, 0, 0, 0)),
            pl.BlockSpec((1, 1, S, LANES), lambda b, g_, i: (b, g_, 0, 0)),
            pl.BlockSpec((nk, LANES, tk), lambda b, g_, i: (0, 0, 0)),
            pl.BlockSpec((1, 1, tq, LANES), lambda b, g_, i: (b, g_, i, 0)),
            pl.BlockSpec((1, 1, HPG, tq, 1), lambda b, g_, i: (b, g_, 0, i, 0)),
        ],
        out_specs=pl.BlockSpec((1, 1, HPG, tq, HEAD_DIM), lambda b, g_, i: (b, g_, 0, i, 0)),
        out_shape=jax.ShapeDtypeStruct((B, G, HPG, S, HEAD_DIM), BF16),
        scratch_shapes=[pltpu.VMEM((HPG * tq, 2 * LANES), BF16)] + _flash_scratch(HPG * tq, tk),
        compiler_params=_cparams(("parallel", "parallel", "parallel")),
    )(qp, kt, va, e, selb, g)


def _nsa_win_body(q_ref, kt_ref, v_ref, g_ref, o_ref, s_ref, m_ref, alpha_ref, acc_ref, *, tq):
    qi = pl.program_id(2)
    _flash_init(m_ref, acc_ref)
    row, col = _tile_iotas(tq, tq)
    for j in range(3):
        ki = qi - 2 + j

        @pl.when(ki >= 0)
        def _():
            kt = kt_ref[0, 0, ki]
            v = v_ref[0, 0, pl.ds(pl.multiple_of(ki * tq, tq), tq), :]
            mask = (col > row) if j == 0 else ((col <= row) if j == 2 else None)
            jobs = [(h * tq, q_ref[0, 0, h], kt, v, mask) for h in range(NSA_HPG)]
            _flash_step(jobs, s_ref, m_ref, alpha_ref, acc_ref)

    for h in range(NSA_HPG):
        o_ref[0, 0, h] = _flash_out(acc_ref, h * tq, tq, _sigmoid(g_ref[0, 0, h])).astype(o_ref.dtype)


def _nsa_win(qp, kt, va, g, tq):
    B, G, HPG, S, _ = qp.shape
    nk = S // tq
    assert NSA_WINDOW == 2 * tq and kt.shape == (B, G, nk, LANES, tq)
    return pl.pallas_call(
        functools.partial(_nsa_win_body, tq=tq),
        grid=(B, G, S // tq),
        in_specs=[
            pl.BlockSpec((1, 1, HPG, tq, LANES), lambda b, g_, i: (b, g_, 0, i, 0)),
            pl.BlockSpec((1, 1, nk, LANES, tq), lambda b, g_, i: (b, g_, 0, 0, 0)),
            pl.BlockSpec((1, 1, S, LANES), lambda b, g_, i: (b, g_, 0, 0)),
            pl.BlockSpec((1, 1, HPG, tq, 1), lambda b, g_, i: (b, g_, 0, i, 0)),
        ],
        out_specs=pl.BlockSpec((1, 1, HPG, tq, HEAD_DIM), lambda b, g_, i: (b, g_, 0, i, 0)),
        out_shape=jax.ShapeDtypeStruct((B, G, HPG, S, HEAD_DIM), BF16),
        scratch_shapes=_flash_scratch(HPG * tq, tq),
        compiler_params=_cparams(("parallel", "parallel", "parallel")),
    )(qp, kt, va, g)


def _moba_kmean_body(kt_ref, et_ref, hi_ref, lo_ref):
    nk = kt_ref.shape[2]
    km = jnp.zeros((LANES, LANES), F32)
    for ki in range(nk):
        km = km + _dot(kt_ref[0, 0, ki], et_ref[ki])
    hi, lo = _split_bf16(km * (1.0 / MOBA_BLOCK))
    hi_ref[0, 0] = hi
    lo_ref[0, 0] = lo


def _moba_kmean(kt, et):
    B, H, nk, _, tk = kt.shape
    spec = pl.BlockSpec((1, 1, LANES, LANES), lambda b, h: (b, h, 0, 0))
    return pl.pallas_call(
        _moba_kmean_body,
        grid=(B, H),
        in_specs=[
            pl.BlockSpec((1, 1, nk, LANES, tk), lambda b, h: (b, h, 0, 0, 0)),
            pl.BlockSpec((nk, tk, LANES), lambda b, h: (0, 0, 0)),
        ],
        out_specs=[spec, spec],
        out_shape=[jax.ShapeDtypeStruct((B, H, LANES, LANES), BF16)] * 2,
        compiler_params=_cparams(("parallel", "parallel")),
    )(kt, et)


def _moba_body(q_ref, kt_ref, v_ref, e_ref, kh_ref, kl_ref, o_ref, qa_ref, s_ref, m_ref, alpha_ref,
               acc_ref, *, tq, tk, hb, nb, top_k):
    s0 = pl.program_id(2) * tq
    cb = s0 // MOBA_BLOCK
    _flash_init(m_ref, acc_ref)

    blk = lax.broadcasted_iota(jnp.int32, (LANES, tq), 0)
    for h in range(hb):
        q = q_ref[0, h]
        gsc = (_dot(q, kh_ref[0, h]) + _dot(q, kl_ref[0, h])).T
        gsc = jnp.where(blk < cb, gsc, NEG)
        gsc = jnp.where(blk < nb, gsc, REMOVED)
        sel = (_topk_mask_t(gsc, top_k) & (blk < cb)) | (blk == cb)
        qa_ref[h * tq:(h + 1) * tq, :] = jnp.concatenate(
            [jnp.where(sel, 0.0, MASK).T.astype(BF16), q], axis=1)

    def tile(ki, causal):
        k0 = pl.multiple_of(ki * tk, tk)
        e = e_ref[ki]
        row, col = _tile_iotas(tq, tk)
        mask = (k0 + col <= s0 + row) if causal else None
        jobs = [(h * tq, qa_ref[h * tq:(h + 1) * tq, :], jnp.concatenate([e, kt_ref[0, h, ki]], axis=0),
                 v_ref[0, h, pl.ds(k0, tk), :], mask) for h in range(hb)]
        _flash_step(jobs, s_ref, m_ref, alpha_ref, acc_ref)

    last = (cb * MOBA_BLOCK) // tk

    def full_tile(ki, carry):
        tile(ki, False)
        return carry

    lax.fori_loop(0, last, full_tile, 0)
    tile(last, True)
    for h in range(hb):
        o_ref[0, h] = _flash_out(acc_ref, h * tq, tq, 1.0).astype(o_ref.dtype)


def _moba_attn(qp, kt, va, e, kh, kl, nb, top_k, tq, tk, hb=4):
    B, H, S, _ = qp.shape
    nk = S // tk
    assert MOBA_BLOCK % tq == 0 and tk % MOBA_BLOCK == 0 and H % hb == 0 and nb <= LANES
    once = pl.Buffered(1)
    return pl.pallas_call(
        functools.partial(_moba_body, tq=tq, tk=tk, hb=hb, nb=nb, top_k=top_k),
        grid=(B, H // hb, S // tq),
        in_specs=[
            pl.BlockSpec((1, hb, tq, LANES), lambda b, h, i: (b, h, i, 0)),
            pl.BlockSpec((1, hb, nk, LANES, tk), lambda b, h, i: (b, h, 0, 0, 0), pipeline_mode=once),
            pl.BlockSpec((1, hb, S, LANES), lambda b, h, i: (b, h, 0, 0), pipeline_mode=once),
            pl.BlockSpec((nk, LANES, tk), lambda b, h, i: (0, 0, 0), pipeline_mode=once),
            pl.BlockSpec((1, hb, LANES, LANES), lambda b, h, i: (b, h, 0, 0), pipeline_mode=once),
            pl.BlockSpec((1, hb, LANES, LANES), lambda b, h, i: (b, h, 0, 0), pipeline_mode=once),
        ],
        out_specs=pl.BlockSpec((1, hb, tq, HEAD_DIM), lambda b, h, i: (b, h, i, 0)),
        out_shape=jax.ShapeDtypeStruct((B, H, S, HEAD_DIM), BF16),
        scratch_shapes=[pltpu.VMEM((hb * tq, 2 * LANES), BF16)] + _flash_scratch(hb * tq, tk),
        compiler_params=_cparams(("parallel", "parallel", "parallel")),
    )(qp, kt, va, e, kh, kl)


def _layer_norm(r, g, b):
    mu = jnp.mean(r, axis=-1, keepdims=True)
    c = r - mu
    var = jnp.mean(c * c, axis=-1, keepdims=True)
    return c * lax.rsqrt(var + LN_EPS) * g + b


def _outproj_body(*refs, n_o):
    o_refs = refs[:n_o]
    w_ref, x_ref, g_ref, b_ref, y_ref, yb_ref = refs[n_o:]
    o = o_refs[0][...].astype(F32)
    for r in o_refs[1:]:
        o = o + r[...].astype(F32)
    mix = _dot(o.astype(BF16), w_ref[...])
    y = _layer_norm(DN_ALPHA * x_ref[...] + mix, g_ref[...], b_ref[...])
    y_ref[...] = y
    yb_ref[...] = y.astype(BF16)


def _outproj_ln(os_, w, x, g, b, tm=512):
    T, D = x.shape
    tm = min(tm, T)
    n_o = len(os_)
    row = pl.BlockSpec((tm, D), lambda i: (i, 0))
    vec = pl.BlockSpec((1, D), lambda i: (0, 0))
    return pl.pallas_call(
        functools.partial(_outproj_body, n_o=n_o),
        grid=(T // tm,),
        in_specs=[row] * n_o + [pl.BlockSpec((D, D), lambda i: (0, 0)), row, vec, vec],
        out_specs=[row, row],
        out_shape=[jax.ShapeDtypeStruct((T, D), F32), jax.ShapeDtypeStruct((T, D), BF16)],
        compiler_params=_cparams(("parallel",)),
    )(*os_, w, x, g.reshape(1, D), b.reshape(1, D))


def _router_body(x_ref, wh_ref, wl_ref, bias_ref, gate_ref):
    x_hi, x_lo = _split_bf16(x_ref[...])
    wh = wh_ref[...]
    logits = _dot(x_hi, wh) + _dot(x_lo, wh) + _dot(x_hi, wl_ref[...])
    scores = _sigmoid(logits)
    lane = lax.broadcasted_iota(jnp.int32, scores.shape, 1)
    lanef = lane.astype(F32)
    live = lane < N_EXPERTS
    biased = jnp.where(live, scores + bias_ref[...], REMOVED)

    def top2(mask):
        v = jnp.where(mask, biased, REMOVED)
        m1 = jnp.max(v, axis=-1, keepdims=True)
        i1 = jnp.min(jnp.where(v == m1, lanef, float(LANES)), axis=-1, keepdims=True)
        v2 = jnp.where(lanef == i1, REMOVED, v)
        m2 = jnp.max(v2, axis=-1, keepdims=True)
        i2 = jnp.min(jnp.where(v2 == m2, lanef, float(LANES)), axis=-1, keepdims=True)
        return m1 + m2, jnp.where((lanef == i1) | (lanef == i2), 1.0, 0.0)

    best, best_sel = top2(lane // EXPERTS_PER_GROUP == 0)
    for grp in range(1, N_GROUPS):
        score, sel = top2(lane // EXPERTS_PER_GROUP == grp)
        better = score > best
        best = jnp.where(better, score, best)
        best_sel = jnp.where(better, sel, best_sel)
    w = best_sel * scores
    gate_ref[...] = w / jnp.sum(w, axis=-1, keepdims=True)


def _router(x, router_w, router_bias, tm=512):
    T, D = x.shape
    tm = min(tm, T)
    wpad = jnp.zeros((D, LANES), F32).at[:, :N_EXPERTS].set(router_w)
    wh, wl = _split_bf16(wpad)
    bpad = jnp.zeros((1, LANES), F32).at[0, :N_EXPERTS].set(router_bias)
    return pl.pallas_call(
        _router_body,
        grid=(T // tm,),
        in_specs=[
            pl.BlockSpec((tm, D), lambda i: (i, 0)),
            pl.BlockSpec((D, LANES), lambda i: (0, 0)),
            pl.BlockSpec((D, LANES), lambda i: (0, 0)),
            pl.BlockSpec((1, LANES), lambda i: (0, 0)),
        ],
        out_specs=pl.BlockSpec((tm, LANES), lambda i: (i, 0)),
        out_shape=jax.ShapeDtypeStruct((T, LANES), F32),
        compiler_params=_cparams(("parallel",)),
    )(x, wh, wl, bpad)


def _experts_body(xb_ref, x_ref, gate_ref, wg_ref, wu_ref, wd_ref, g_ref, b_ref, y_ref, yb_ref, acc_ref):
    e = pl.program_id(1)

    @pl.when(e == 0)
    def _():
        acc_ref[...] = jnp.zeros(acc_ref.shape, F32)

    xb = xb_ref[...]
    a = _dot(xb, wg_ref[0])
    u = _dot(xb, wu_ref[0])
    lane = lax.broadcasted_iota(jnp.int32, gate_ref.shape, 1)
    gcol = jnp.sum(jnp.where(lane == e, gate_ref[...], 0.0), axis=-1, keepdims=True)
    h = a * _sigmoid(a) * u * gcol
    acc_ref[...] += _dot(h.astype(BF16), wd_ref[0])

    @pl.when(e == N_EXPERTS - 1)
    def _():
        y = _layer_norm(DN_ALPHA * x_ref[...] + acc_ref[...], g_ref[...], b_ref[...])
        y_ref[...] = y
        yb_ref[...] = y.astype(BF16)


def _experts_ln(xb, x, gate, wg, wu, wd, g, b, tm=1024):
    T, D = x.shape
    tm = min(tm, T)
    E, _, DE = wg.shape
    row = pl.BlockSpec((tm, D), lambda i, e: (i, 0))
    vec = pl.BlockSpec((1, D), lambda i, e: (0, 0))
    return pl.pallas_call(
        _experts_body,
        grid=(T // tm, E),
        in_specs=[
            row, row,
            pl.BlockSpec((tm, LANES), lambda i, e: (i, 0)),
            pl.BlockSpec((1, D, DE), lambda i, e: (e, 0, 0)),
            pl.BlockSpec((1, D, DE), lambda i, e: (e, 0, 0)),
            pl.BlockSpec((1, DE, D), lambda i, e: (e, 0, 0)),
            vec, vec,
        ],
        out_specs=[row, row],
        out_shape=[jax.ShapeDtypeStruct((T, D), F32), jax.ShapeDtypeStruct((T, D), BF16)],
        scratch_shapes=[pltpu.VMEM((tm, D), F32)],
        compiler_params=_cparams(("parallel", "arbitrary")),
    )(xb, x, gate, wg, wu, wd, g.reshape(1, D), b.reshape(1, D))


def _heads_major(t, B, S, n):
    return t.reshape(B, S, n, HEAD_DIM).transpose(0, 2, 1, 3)


def _tokens_major(o, B, S):
    n = o.shape[1]
    return o.transpose(0, 2, 1, 3).reshape(B * S, n * HEAD_DIM)


def _q_pad(t, B, S, n):
    return jnp.pad(_heads_major(t, B, S, n), ((0, 0), (0, 0), (0, 0), (0, LANES - HEAD_DIM)))


def _v_aug(t, B, S, n):
    v = _heads_major(t, B, S, n)
    ones = jnp.ones((B, n, S, 1), v.dtype)
    zeros = jnp.zeros((B, n, S, LANES - HEAD_DIM - 1), v.dtype)
    return jnp.concatenate([v, ones, zeros], axis=-1)


def _kt_pad(t, B, S, n):
    kt = t.reshape(B, S, n, HEAD_DIM).transpose(0, 2, 3, 1)
    return jnp.pad(kt, ((0, 0), (0, 0), (0, LANES - HEAD_DIM), (0, 0)))


def _kt_tiles(t, B, S, n, tk):
    kt = _kt_pad(t, B, S, n).reshape(B, n, LANES, S // tk, tk)
    return kt.transpose(0, 1, 3, 2, 4)


def _block_onehots(S, block, tk):
    key = np.arange(S).reshape(S // tk, 1, tk)
    r = np.arange(LANES).reshape(1, LANES, 1)
    return jnp.asarray((key // block == r).astype(np.float32), BF16)


def _rope_tiled(S):
    cos, sin = _rope_tables(jnp.arange(S))
    reps = LANES // HALF
    return jnp.tile(cos, (1, reps)), jnp.tile(sin, (1, reps))


def _nsa_mixer(xb, B, S, w_in, cmp_k_w1, cmp_k_w2, cmp_v_w1, cmp_v_w2, cmp_k_pos, cmp_v_pos):
    G, HPG, KV = NSA_KV_GROUPS, NSA_HPG, NSA_KV_DIM
    L, STR, SB = NSA_CMP_LEN, NSA_CMP_STRIDE, NSA_SEL_BLOCK
    assert L == 2 * STR and S % SB == 0 and S // SB <= LANES
    n_cmp = (S - L) // STR + 1
    NC = S // STR
    n_sel = S // SB
    top_n = min(NSA_SEL_TOPN, n_sel)

    cos2, sin2 = _rope_tiled(S)
    n_main = D_MODEL + 6 * KV
    wb = w_in.astype(BF16)
    tn = 2 * KV
    src = [2, 4, 0, 1, 3, 5]
    wcols = [wb[:, D_MODEL + i * KV: D_MODEL + (i + 1) * KV] for i in src]
    modes = [2] * (D_MODEL // tn) + [1, 0, 0]
    proj = _proj(xb, jnp.concatenate([wb[:, :D_MODEL]] + wcols, axis=1), cos2, sin2, modes, S, tn=tn)
    wg = jnp.zeros((D_MODEL, LANES), BF16).at[:, :3 * N_HEADS].set(wb[:, n_main:])
    gates = _proj(xb, wg, cos2, sin2, [0], S, out_dtype=F32, tn=LANES)[:, :3 * N_HEADS]

    col = lambda i: proj[:, D_MODEL + src.index(i) * KV: D_MODEL + (src.index(i) + 1) * KV]
    qp = _q_pad(proj[:, :D_MODEL], B, S, N_HEADS).reshape(B, G, HPG, S, LANES)
    gcols = gates.reshape(B, S, 3, G, HPG).transpose(2, 0, 3, 4, 1)[..., None]

    ccos, csin = _rope_tables(jnp.arange(NC) * STR + (L - 1))
    ccos = jnp.concatenate([ccos, ccos], axis=1)
    csin = jnp.concatenate([csin, csin], axis=1)
    to_rows = lambda t: _heads_major(t, B, S, G).reshape(B * G, NC, STR * HEAD_DIM)
    kc = _compress(to_rows(col(0)), cmp_k_w1, cmp_k_pos, cmp_k_w2, ccos, csin, True, n_cmp)
    vc = _compress(to_rows(col(1)), cmp_v_w1, cmp_v_pos, cmp_v_w2, ccos, csin, False, n_cmp)
    kct = jnp.pad(kc.reshape(B, G, NC, HEAD_DIM).transpose(0, 1, 3, 2),
                  ((0, 0), (0, 0), (0, LANES - HEAD_DIM), (0, 0)))
    ci = np.arange(NC)[:, None]
    sj = np.arange(LANES)[None, :]
    overlap = ((ci * STR < (sj + 1) * SB) & (ci * STR + L > sj * SB) & (ci < n_cmp) & (sj < n_sel))
    overlap = jnp.broadcast_to(jnp.asarray(overlap.astype(np.float32), BF16), (B, G, NC, LANES))
    vo = jnp.concatenate([vc.reshape(B, G, NC, HEAD_DIM), jnp.ones((B, G, NC, 1), BF16),
                          jnp.zeros((B, G, NC, LANES - HEAD_DIM - 1), BF16), overlap], axis=-1)

    o_cmp, selb = _nsa_cmp(qp, kct, vo, gcols[0], n_cmp, n_sel, top_n)
    tk_s = min(512, S)
    o_slc = _nsa_slc(qp, _kt_tiles(col(2), B, S, G, tk_s), _v_aug(col(3), B, S, G),
                     _block_onehots(S, SB, tk_s), selb, gcols[1], tq=256, tk=tk_s)
    tq_w = NSA_WINDOW // 2
    o_win = _nsa_win(qp, _kt_tiles(col(4), B, S, G, tq_w), _v_aug(col(5), B, S, G), gcols[2], tq=tq_w)
    flat = lambda o: _tokens_major(o.reshape(B, N_HEADS, S, HEAD_DIM), B, S)
    return [flat(o_cmp), flat(o_slc), flat(o_win)]


def _moba_mixer(xb, B, S, w_in):
    H = N_HEADS
    nb = S // MOBA_BLOCK
    top_k = min(MOBA_TOPK, nb)
    cos2, sin2 = _rope_tiled(S)
    tn = 512
    n_t = D_MODEL // tn
    modes = [2] * n_t + [1] * n_t + [0] * n_t
    proj = _proj(xb, w_in.astype(BF16), cos2, sin2, modes, S, tn=tn)
    part = lambda i: proj[:, i * D_MODEL:(i + 1) * D_MODEL]
    tk = min(2 * MOBA_BLOCK, S)
    kt = _kt_tiles(part(1), B, S, H, tk)
    e = _block_onehots(S, MOBA_BLOCK, tk)
    kh, kl = _moba_kmean(kt, e.transpose(0, 2, 1))
    o = _moba_attn(_q_pad(part(0), B, S, H), kt, _v_aug(part(2), B, S, H), e, kh, kl, nb, top_k,
                   tq=MOBA_BLOCK, tk=tk)
    return [_tokens_major(o, B, S)]


def kernel(x, nsa_w_in, nsa_w_out, nsa_cmp_k_w1, nsa_cmp_k_w2, nsa_cmp_v_w1, nsa_cmp_v_w2, nsa_cmp_k_pos, nsa_cmp_v_pos, moba_w_in, moba_w_out, router_w, router_bias, moe_w_gate, moe_w_up, moe_w_down, ln_g, ln_b):
    B, S, D = x.shape
    xf = x.reshape(B * S, D)
    xb = xf.astype(BF16)
    for layer in range(DEPTH):
        j = layer // 2
        if layer % 2 == 0:
            os_ = _nsa_mixer(xb, B, S, nsa_w_in[j], nsa_cmp_k_w1[j], nsa_cmp_k_w2[j], nsa_cmp_v_w1[j],
                             nsa_cmp_v_w2[j], nsa_cmp_k_pos[j], nsa_cmp_v_pos[j])
            w_out = nsa_w_out[j]
        else:
            os_ = _moba_mixer(xb, B, S, moba_w_in[j])
            w_out = moba_w_out[j]
        xf, xb = _outproj_ln(os_, w_out.astype(BF16), xf, ln_g[layer, 0], ln_b[layer, 0])
        gate = _router(xf, router_w, router_bias)
        xf, xb = _experts_ln(xb, xf, gate, moe_w_gate[layer].astype(BF16), moe_w_up[layer].astype(BF16),
                             moe_w_down[layer].astype(BF16), ln_g[layer, 1], ln_b[layer, 1])
    return xf.reshape(B, S, D)
```

```python
import functools

import jax
import jax.numpy as jnp
import numpy as np
from jax import lax
from jax.experimental import pallas as pl
from jax.experimental.pallas import tpu as pltpu

F32 = jnp.float32
BF16 = jnp.bfloat16

D_MODEL = 1024
N_HEADS = 16
HEAD_DIM = 64
HALF = HEAD_DIM // 2
ROPE_THETA = 10000.0
DEPTH = 2
DN_ALPHA = (2 * DEPTH) ** 0.25
LN_EPS = 1e-5
NEG = -1e30
FORCE = 1e9
MASK = -1e30
REMOVED = -3.0e38
LANES = 128
Q_SCALE_LOG2 = float(HEAD_DIM ** -0.5 * np.log2(np.e))
ROW_CHUNK = 128

NSA_KV_GROUPS = 4
NSA_HPG = N_HEADS // NSA_KV_GROUPS
NSA_KV_DIM = NSA_KV_GROUPS * HEAD_DIM
NSA_CMP_LEN = 32
NSA_CMP_STRIDE = 16
NSA_SEL_BLOCK = 64
NSA_SEL_TOPN = 16
NSA_WINDOW = 512

MOBA_BLOCK = 256
MOBA_TOPK = 3

N_EXPERTS = 16
N_GROUPS = 4
EXPERTS_PER_GROUP = N_EXPERTS // N_GROUPS
D_EXPERT = 512

VMEM_LIMIT = 48 * 1024 * 1024


def _cparams(sem):
    return pltpu.CompilerParams(dimension_semantics=sem, vmem_limit_bytes=VMEM_LIMIT)


def _dot(a, b):
    return jnp.dot(a, b, preferred_element_type=F32)


def _split_bf16(x):
    hi = x.astype(BF16)
    lo = (x - hi.astype(F32)).astype(BF16)
    return hi, lo


def _sigmoid(x):
    return 1.0 / (1.0 + jnp.exp(-x))


def _proj_body(mode_ref, x_ref, w_ref, cos_ref, sin_ref, o_ref, *, tn):
    j = pl.program_id(0)
    mode = mode_ref[j]
    acc = _dot(x_ref[...], w_ref[...])

    @pl.when(mode == 0)
    def _():
        o_ref[...] = acc.astype(o_ref.dtype)

    @pl.when(mode != 0)
    def _():
        sc = jnp.where(mode == 2, Q_SCALE_LOG2, 1.0).astype(F32)
        cos = cos_ref[...] * sc
        sin = sin_ref[...] * sc
        lane = lax.broadcasted_iota(jnp.int32, cos.shape, 1)
        lower = (lane % HEAD_DIM) < HALF
        for c in range(tn // LANES):
            a = acc[:, c * LANES:(c + 1) * LANES]
            up = pltpu.roll(a, LANES - HALF, 1)
            dn = pltpu.roll(a, HALF, 1)
            partner = jnp.where(lower, -up, dn)
            o_ref[:, c * LANES:(c + 1) * LANES] = (a * cos + partner * sin).astype(o_ref.dtype)


def _proj(xb, w, cos2, sin2, modes, seq, out_dtype=BF16, tm=1024, tn=512):
    T, K = xb.shape
    N = w.shape[1]
    tm = min(tm, seq)
    assert T % tm == 0 and N % tn == 0 and seq % tm == 0 and len(modes) == N // tn
    n_pos = seq // tm
    grid_spec = pltpu.PrefetchScalarGridSpec(
        num_scalar_prefetch=1,
        grid=(N // tn, T // tm),
        in_specs=[
            pl.BlockSpec((tm, K), lambda j, i, m: (i, 0)),
            pl.BlockSpec((K, tn), lambda j, i, m: (0, j)),
            pl.BlockSpec((tm, LANES), lambda j, i, m: (i % n_pos, 0)),
            pl.BlockSpec((tm, LANES), lambda j, i, m: (i % n_pos, 0)),
        ],
        out_specs=pl.BlockSpec((tm, tn), lambda j, i, m: (i, j)),
    )
    return pl.pallas_call(
        functools.partial(_proj_body, tn=tn),
        grid_spec=grid_spec,
        out_shape=jax.ShapeDtypeStruct((T, N), out_dtype),
        compiler_params=_cparams(("parallel", "parallel")),
    )(jnp.asarray(modes, jnp.int32), xb, w, cos2, sin2)


def _rope_tables(pos):
    inv = 1.0 / (ROPE_THETA ** (jnp.arange(0, HEAD_DIM, 2, dtype=F32) / HEAD_DIM))
    ang = pos.astype(F32)[:, None] * inv[None, :]
    return jnp.cos(ang), jnp.sin(ang)


def _proj_vaug_body(x_ref, w_ref, o_ref, *, tn):
    acc = _dot(x_ref[...], w_ref[...])
    lane = lax.broadcasted_iota(jnp.int32, (acc.shape[0], LANES), 1)
    tail = jnp.where(lane == HEAD_DIM, 1.0, 0.0)
    for c in range(tn // LANES):
        a = acc[:, c * LANES:(c + 1) * LANES]
        for k, head in enumerate((a, pltpu.roll(a, HEAD_DIM, 1))):
            o_ref[:, (2 * c + k) * LANES:(2 * c + k + 1) * LANES] = (
                jnp.where(lane < HEAD_DIM, head, tail).astype(o_ref.dtype))


def _proj_vaug(xb, w, tm=1024, tn=512):
    T, K = xb.shape
    N = w.shape[1]
    tm = min(tm, T)
    assert T % tm == 0 and N % tn == 0
    return pl.pallas_call(
        functools.partial(_proj_vaug_body, tn=tn),
        grid=(N // tn, T // tm),
        in_specs=[pl.BlockSpec((tm, K), lambda j, i: (i, 0)), pl.BlockSpec((K, tn), lambda j, i: (0, j))],
        out_specs=pl.BlockSpec((tm, 2 * tn), lambda j, i: (i, j)),
        out_shape=jax.ShapeDtypeStruct((T, 2 * N), BF16),
        compiler_params=_cparams(("parallel", "parallel")),
    )(xb, w)


def _gelu_tanh(x):
    c = np.float32(np.sqrt(2.0 / np.pi))
    return 0.5 * x * (1.0 + jnp.tanh(c * (x + 0.044715 * (x * x * x))))


def _compress_body(r_ref, w1_ref, pos_ref, w2_ref, w2r_ref, cos_ref, sin_ref, o_ref, *, rope, n_cmp):
    r = r_ref[0]
    nc = r.shape[0]
    half = NSA_CMP_STRIDE * HEAD_DIM
    a = _dot(r, w1_ref[0])
    b = _dot(r, w1_ref[1])
    pos = pos_ref[...]
    pb = _dot(pos[:, :half], w1_ref[0]) + _dot(pos[:, half:], w1_ref[1])
    b_next = pltpu.roll(b, nc - 1, 0)
    h = _gelu_tanh(a + b_next + pb[0:1, :]).astype(BF16)
    o = _dot(h, w2_ref[...])
    if rope:
        o = o * cos_ref[...] + _dot(h, w2r_ref[...]) * sin_ref[...]
    row = lax.broadcasted_iota(jnp.int32, o.shape, 0)
    o_ref[0] = jnp.where(row < n_cmp, o, 0.0).astype(o_ref.dtype)


def _compress(r, w1, pos, w2, cos_c, sin_c, rope, n_cmp):
    BG, NC, K = r.shape
    hidden = w1.shape[1]
    w1s = w1.astype(BF16).reshape(2, K, hidden)
    pos8 = jnp.zeros((8, 2 * K), BF16).at[0].set(pos.reshape(-1).astype(BF16))
    w2r = jnp.concatenate([-w2[:, HALF:], w2[:, :HALF]], axis=1).astype(BF16)
    full = lambda shape: pl.BlockSpec(shape, lambda i: (0,) * len(shape))
    return pl.pallas_call(
        functools.partial(_compress_body, rope=rope, n_cmp=n_cmp),
        grid=(BG,),
        in_specs=[
            pl.BlockSpec((1, NC, K), lambda i: (i, 0, 0)),
            full((2, K, hidden)),
            full((8, 2 * K)),
            full((hidden, HEAD_DIM)),
            full((hidden, HEAD_DIM)),
            full((NC, HEAD_DIM)),
            full((NC, HEAD_DIM)),
        ],
        out_specs=pl.BlockSpec((1, NC, HEAD_DIM), lambda i: (i, 0, 0)),
        out_shape=jax.ShapeDtypeStruct((BG, NC, HEAD_DIM), BF16),
        compiler_params=_cparams(("parallel",)),
    )(r, w1s, pos8, w2.astype(BF16), w2r, cos_c, sin_c)


def _topk_mask_t(v, k):
    idx = lax.broadcasted_iota(jnp.int32, v.shape, 0).astype(F32)

    def step(_, cur):
        m = jnp.max(cur, axis=0, keepdims=True)
        first = jnp.min(jnp.where(cur == m, idx, float(LANES)), axis=0, keepdims=True)
        return jnp.where(idx == first, REMOVED, cur)

    return lax.fori_loop(0, k, step, v, unroll=True) != v


def _head_kt(e, kt, parity):
    z = jnp.zeros_like(kt)
    parts = ([] if e is None else [e]) + ([kt, z] if parity == 0 else [z, kt])
    return jnp.concatenate(parts, axis=0)


def _flash_init(m_ref, acc_ref):
    m_ref[...] = jnp.full(m_ref.shape, -jnp.inf, F32)
    acc_ref[...] = jnp.zeros(acc_ref.shape, F32)


def _flash_step(jobs, s_ref, m_ref, alpha_ref, acc_ref):
    for r0, q, kt, _, mask in jobs:
        rows = q.shape[0]
        s = _dot(q, kt)
        if mask is not None:
            s = jnp.where(mask, s, MASK)
        s_ref[r0:r0 + rows, :] = s
        cm = s[:, :LANES]
        for c in range(1, s.shape[1] // LANES):
            cm = jnp.maximum(cm, s[:, c * LANES:(c + 1) * LANES])
        m_prev = m_ref[r0:r0 + rows, :]
        m_new = jnp.maximum(m_prev, jnp.max(cm, axis=-1, keepdims=True))
        alpha_ref[r0:r0 + rows, :] = jnp.exp2(m_prev - m_new)
        m_ref[r0:r0 + rows, :] = m_new
    for r0, q, _, v, _ in jobs:
        for r in range(r0, r0 + q.shape[0], ROW_CHUNK):
            rows = min(ROW_CHUNK, r0 + q.shape[0] - r)
            m = m_ref[r:r + rows, :]
            p = jnp.concatenate(
                [jnp.exp2(s_ref[r:r + rows, c * LANES:(c + 1) * LANES] - m)
                 for c in range(s_ref.shape[1] // LANES)], axis=1).astype(BF16)
            acc_ref[r:r + rows, :] = alpha_ref[r:r + rows, :] * acc_ref[r:r + rows, :] + _dot(p, v)


def _flash_scratch(rows, tk):
    return [pltpu.VMEM((rows, tk), F32), pltpu.VMEM((rows, LANES), F32),
            pltpu.VMEM((rows, LANES), F32), pltpu.VMEM((rows, LANES), F32)]


def _pair_merge(even, odd):
    lane = lax.broadcasted_iota(jnp.int32, even.shape, 1)
    return jnp.where(lane < HEAD_DIM, even, pltpu.roll(odd, HEAD_DIM, 1))


def _flash_pairs_out(acc_ref, tq, n_heads, gates):
    def head(h):
        acc = acc_ref[h * tq:(h + 1) * tq, :]
        return acc * (gates[h] / acc[:, HEAD_DIM:HEAD_DIM + 1])
    return jnp.concatenate([_pair_merge(head(h), head(h + 1)) for h in range(0, n_heads, 2)], axis=1)


def _tile_iotas(rows, tk):
    return (lax.broadcasted_iota(jnp.int32, (rows, tk), 0),
            lax.broadcasted_iota(jnp.int32, (rows, tk), 1))


def _nsa_cmp_body(q_ref, kct_ref, vo_ref, g_ref, o_ref, selb_ref, s_ref, m_ref, *, tq, n_cmp, n_sel, top_n):
    s0 = pl.program_id(2) * tq
    q4 = q_ref[0]
    kct = kct_ref[0, 0]
    vo = vo_ref[0, 0]
    nc = vo.shape[0]
    tpos = s0 + lax.broadcasted_iota(jnp.int32, (tq, nc), 0)
    nidx = lax.broadcasted_iota(jnp.int32, (tq, nc), 1)
    cmask = (nidx * NSA_CMP_STRIDE + (NSA_CMP_LEN - 1) <= tpos) & (nidx < n_cmp)
    for h in range(NSA_HPG):
        pair = q4[:, (h // 2) * LANES:(h // 2 + 1) * LANES]
        s = jnp.where(cmask, _dot(pair, _head_kt(None, kct, h % 2)), NEG)
        s_ref[h * tq:(h + 1) * tq, :] = s
        cm = s[:, :LANES]
        for c in range(1, nc // LANES):
            cm = jnp.maximum(cm, s[:, c * LANES:(c + 1) * LANES])
        m_ref[h * tq:(h + 1) * tq, :] = jnp.broadcast_to(jnp.max(cm, axis=-1, keepdims=True), (tq, LANES))

    rc = min(ROW_CHUNK, tq)
    imps = []
    for r0 in range(0, tq, rc):
        imp = jnp.zeros((rc, LANES), F32)
        heads = []
        for h in range(NSA_HPG):
            r = h * tq + r0
            m = m_ref[r:r + rc, :]
            e = jnp.concatenate(
                [jnp.exp2(s_ref[r:r + rc, c * LANES:(c + 1) * LANES] - m) for c in range(nc // LANES)],
                axis=1).astype(BF16)
            res = _dot(e, vo)
            inv = jnp.where(m[:, :1] > 0.5 * NEG, 1.0 / res[:, HEAD_DIM:HEAD_DIM + 1], 0.0)
            heads.append(res[:, :LANES] * (inv * _sigmoid(g_ref[0, 0, h, r0:r0 + rc, :])))
            imp = imp + res[:, LANES:] * inv
        o_ref[0, r0:r0 + rc, :] = jnp.concatenate(
            [_pair_merge(heads[h], heads[h + 1]) for h in range(0, NSA_HPG, 2)], axis=1).astype(o_ref.dtype)
        imps.append(imp)
    imp = jnp.concatenate(imps, axis=0)

    blk = lax.broadcasted_iota(jnp.int32, imp.shape, 1)
    jq = (s0 + lax.broadcasted_iota(jnp.int32, imp.shape, 0)) // NSA_SEL_BLOCK
    forced = (blk == 0) | (blk == jq) | (blk == jq - 1)
    imp = jnp.where(blk > jq, NEG, jnp.where(forced, FORCE, imp))
    imp = jnp.where(blk < n_sel, imp, REMOVED)
    sel_t = _topk_mask_t(imp.T, top_n)
    selb_ref[0, 0] = jnp.where(sel_t, 0.0, MASK).T.astype(selb_ref.dtype)


def _nsa_cmp(proj3, kct, vo, g_cmp, n_cmp, n_sel, top_n, tq=256):
    B, S, _ = proj3.shape
    G, HPG = NSA_KV_GROUPS, NSA_HPG
    NC = vo.shape[2]
    tq = min(tq, S)
    assert NC % LANES == 0
    return pl.pallas_call(
        functools.partial(_nsa_cmp_body, tq=tq, n_cmp=n_cmp, n_sel=n_sel, top_n=top_n),
        grid=(B, G, S // tq),
        in_specs=[
            pl.BlockSpec((1, tq, HPG * HEAD_DIM), lambda b, g, i: (b, i, g)),
            pl.BlockSpec((1, 1, HEAD_DIM, NC), lambda b, g, i: (b, g, 0, 0)),
            pl.BlockSpec((1, 1, NC, 2 * LANES), lambda b, g, i: (b, g, 0, 0)),
            pl.BlockSpec((1, 1, HPG, tq, 1), lambda b, g, i: (b, g, 0, i, 0)),
        ],
        out_specs=[
            pl.BlockSpec((1, tq, HPG * HEAD_DIM), lambda b, g, i: (b, i, g)),
            pl.BlockSpec((1, 1, tq, LANES), lambda b, g, i: (b, g, i, 0)),
        ],
        out_shape=[
            jax.ShapeDtypeStruct((B, S, D_MODEL), BF16),
            jax.ShapeDtypeStruct((B, G, S, LANES), BF16),
        ],
        scratch_shapes=[pltpu.VMEM((HPG * tq, NC), F32), pltpu.VMEM((HPG * tq, LANES), F32)],
        compiler_params=_cparams(("parallel", "parallel", "parallel")),
    )(proj3, kct, vo, g_cmp)


def _nsa_slc_body(q_ref, kt_ref, v_ref, e_ref, selb_ref, g_ref, o_ref, qa_ref, s_ref, m_ref, alpha_ref,
                  acc_ref, *, tq, tk):
    s0 = pl.program_id(2) * tq
    _flash_init(m_ref, acc_ref)
    selb = selb_ref[0, 0]
    q4 = q_ref[0]
    for h in range(NSA_HPG):
        qa_ref[h * tq:(h + 1) * tq, :] = jnp.concatenate(
            [selb, q4[:, (h // 2) * LANES:(h // 2 + 1) * LANES]], axis=1)

    def tile(ki, causal):
        k0 = pl.multiple_of(ki * tk, tk)
        kts = [_head_kt(e_ref[ki], kt_ref[0, 0, ki], parity) for parity in range(2)]
        v = v_ref[0, pl.ds(k0, tk), :]
        row, col = _tile_iotas(tq, tk)
        mask = (k0 + col <= s0 + row) if causal else None
        jobs = [(h * tq, qa_ref[h * tq:(h + 1) * tq, :], kts[h % 2], v, mask) for h in range(NSA_HPG)]
        _flash_step(jobs, s_ref, m_ref, alpha_ref, acc_ref)

    last = (s0 + tq - 1) // tk

    def full_tile(ki, carry):
        tile(ki, False)
        return carry

    lax.fori_loop(0, last, full_tile, 0)
    tile(last, True)
    gates = [_sigmoid(g_ref[0, 0, h]) for h in range(NSA_HPG)]
    o_ref[0] = _flash_pairs_out(acc_ref, tq, NSA_HPG, gates).astype(o_ref.dtype)


def _nsa_slc(proj3, kt, vaug3, v_block, e, selb, g, tq, tk):
    B, S, _ = proj3.shape
    G, HPG = NSA_KV_GROUPS, NSA_HPG
    nk = S // tk
    assert tk % tq == 0 and kt.shape == (B, G, nk, HEAD_DIM, tk) and e.shape == (nk, LANES, tk)
    once = pl.Buffered(1)
    return pl.pallas_call(
        functools.partial(_nsa_slc_body, tq=tq, tk=tk),
        grid=(B, G, S // tq),
        in_specs=[
            pl.BlockSpec((1, tq, HPG * HEAD_DIM), lambda b, g_, i: (b, i, g_)),
            pl.BlockSpec((1, 1, nk, HEAD_DIM, tk), lambda b, g_, i: (b, g_, 0, 0, 0), pipeline_mode=once),
            pl.BlockSpec((1, S, LANES), lambda b, g_, i: (b, 0, v_block + g_), pipeline_mode=once),
            pl.BlockSpec((nk, LANES, tk), lambda b, g_, i: (0, 0, 0), pipeline_mode=once),
            pl.BlockSpec((1, 1, tq, LANES), lambda b, g_, i: (b, g_, i, 0)),
            pl.BlockSpec((1, 1, HPG, tq, 1), lambda b, g_, i: (b, g_, 0, i, 0)),
        ],
        out_specs=pl.BlockSpec((1, tq, HPG * HEAD_DIM), lambda b, g_, i: (b, i, g_)),
        out_shape=jax.ShapeDtypeStruct((B, S, D_MODEL), BF16),
        scratch_shapes=[pltpu.VMEM((HPG * tq, 2 * LANES), BF16)] + _flash_scratch(HPG * tq, tk),
        compiler_params=_cparams(("parallel", "parallel", "parallel")),
    )(proj3, kt, vaug3, e, selb, g)


def _nsa_win_body(q_ref, kt_ref, v_ref, g_ref, o_ref, s_ref, m_ref, alpha_ref, acc_ref, *, tq):
    qi = pl.program_id(2)
    _flash_init(m_ref, acc_ref)
    row, col = _tile_iotas(tq, tq)
    q4 = q_ref[0]
    for j in range(3):
        ki = qi - 2 + j

        @pl.when(ki >= 0)
        def _():
            kts = [_head_kt(None, kt_ref[0, 0, ki], parity) for parity in range(2)]
            v = v_ref[0, pl.ds(pl.multiple_of(ki * tq, tq), tq), :]
            mask = (col > row) if j == 0 else ((col <= row) if j == 2 else None)
            jobs = [(h * tq, q4[:, (h // 2) * LANES:(h // 2 + 1) * LANES], kts[h % 2], v, mask)
                    for h in range(NSA_HPG)]
            _flash_step(jobs, s_ref, m_ref, alpha_ref, acc_ref)

    gates = [_sigmoid(g_ref[0, 0, h]) for h in range(NSA_HPG)]
    o_ref[0] = _flash_pairs_out(acc_ref, tq, NSA_HPG, gates).astype(o_ref.dtype)


def _nsa_win(proj3, kt, vaug3, v_block, g, tq):
    B, S, _ = proj3.shape
    G, HPG = NSA_KV_GROUPS, NSA_HPG
    nk = S // tq
    assert NSA_WINDOW == 2 * tq and kt.shape == (B, G, nk, HEAD_DIM, tq)
    once = pl.Buffered(1)
    return pl.pallas_call(
        functools.partial(_nsa_win_body, tq=tq),
        grid=(B, G, S // tq),
        in_specs=[
            pl.BlockSpec((1, tq, HPG * HEAD_DIM), lambda b, g_, i: (b, i, g_)),
            pl.BlockSpec((1, 1, nk, HEAD_DIM, tq), lambda b, g_, i: (b, g_, 0, 0, 0), pipeline_mode=once),
            pl.BlockSpec((1, S, LANES), lambda b, g_, i: (b, 0, v_block + g_), pipeline_mode=once),
            pl.BlockSpec((1, 1, HPG, tq, 1), lambda b, g_, i: (b, g_, 0, i, 0)),
        ],
        out_specs=pl.BlockSpec((1, tq, HPG * HEAD_DIM), lambda b, g_, i: (b, i, g_)),
        out_shape=jax.ShapeDtypeStruct((B, S, D_MODEL), BF16),
        scratch_shapes=_flash_scratch(HPG * tq, tq),
        compiler_params=_cparams(("parallel", "parallel", "parallel")),
    )(proj3, kt, vaug3, g)


def _moba_kmean_body(kt_ref, et_ref, hi_ref, lo_ref):
    nk = kt_ref.shape[2]
    km = jnp.zeros((HEAD_DIM, LANES), F32)
    for ki in range(nk):
        km = km + _dot(kt_ref[0, 0, ki], et_ref[ki])
    hi, lo = _split_bf16(km * (1.0 / MOBA_BLOCK))
    hi_ref[0, 0] = hi
    lo_ref[0, 0] = lo


def _moba_kmean(kt, et):
    B, H, nk, _, tk = kt.shape
    spec = pl.BlockSpec((1, 1, HEAD_DIM, LANES), lambda b, h: (b, h, 0, 0))
    return pl.pallas_call(
        _moba_kmean_body,
        grid=(B, H),
        in_specs=[
            pl.BlockSpec((1, 1, nk, HEAD_DIM, tk), lambda b, h: (b, h, 0, 0, 0)),
            pl.BlockSpec((nk, tk, LANES), lambda b, h: (0, 0, 0)),
        ],
        out_specs=[spec, spec],
        out_shape=[jax.ShapeDtypeStruct((B, H, HEAD_DIM, LANES), BF16)] * 2,
        compiler_params=_cparams(("parallel", "parallel")),
    )(kt, et)


def _moba_body(q_ref, kt_ref, v_ref, e_ref, kh_ref, kl_ref, o_ref, qa_ref, s_ref, m_ref, alpha_ref,
               acc_ref, *, tq, tk, hb, nb, top_k):
    s0 = pl.program_id(2) * tq
    cb = s0 // MOBA_BLOCK
    _flash_init(m_ref, acc_ref)

    blk = lax.broadcasted_iota(jnp.int32, (LANES, tq), 0)
    qs = q_ref[0]
    for h in range(hb):
        pair = qs[:, (h // 2) * LANES:(h // 2 + 1) * LANES]
        gsc = (_dot(pair, _head_kt(None, kh_ref[0, h], h % 2))
               + _dot(pair, _head_kt(None, kl_ref[0, h], h % 2))).T
        gsc = jnp.where(blk < cb, gsc, NEG)
        gsc = jnp.where(blk < nb, gsc, REMOVED)
        sel = (_topk_mask_t(gsc, top_k) & (blk < cb)) | (blk == cb)
        qa_ref[h * tq:(h + 1) * tq, :] = jnp.concatenate(
            [jnp.where(sel, 0.0, MASK).T.astype(BF16), pair], axis=1)

    def tile(ki, causal):
        k0 = pl.multiple_of(ki * tk, tk)
        e = e_ref[ki]
        row, col = _tile_iotas(tq, tk)
        mask = (k0 + col <= s0 + row) if causal else None
        jobs = [(h * tq, qa_ref[h * tq:(h + 1) * tq, :], _head_kt(e, kt_ref[0, h, ki], h % 2),
                 v_ref[0, pl.ds(k0, tk), h * LANES:(h + 1) * LANES], mask) for h in range(hb)]
        _flash_step(jobs, s_ref, m_ref, alpha_ref, acc_ref)

    last = (cb * MOBA_BLOCK) // tk

    def full_tile(ki, carry):
        tile(ki, False)
        return carry

    lax.fori_loop(0, last, full_tile, 0)
    tile(last, True)
    o_ref[0] = _flash_pairs_out(acc_ref, tq, hb, [1.0] * hb).astype(o_ref.dtype)


def _moba_attn(proj3, kt, vaug3, e, kh, kl, nb, top_k, tq, tk, hb=4):
    B, S, _ = proj3.shape
    H = N_HEADS
    nk = S // tk
    assert MOBA_BLOCK % tq == 0 and tk % MOBA_BLOCK == 0 and H % hb == 0 and hb % 2 == 0 and nb <= LANES
    once = pl.Buffered(1)
    return pl.pallas_call(
        functools.partial(_moba_body, tq=tq, tk=tk, hb=hb, nb=nb, top_k=top_k),
        grid=(B, H // hb, S // tq),
        in_specs=[
            pl.BlockSpec((1, tq, hb * HEAD_DIM), lambda b, h, i: (b, i, h)),
            pl.BlockSpec((1, hb, nk, HEAD_DIM, tk), lambda b, h, i: (b, h, 0, 0, 0), pipeline_mode=once),
            pl.BlockSpec((1, S, hb * LANES), lambda b, h, i: (b, 0, h), pipeline_mode=once),
            pl.BlockSpec((nk, LANES, tk), lambda b, h, i: (0, 0, 0), pipeline_mode=once),
            pl.BlockSpec((1, hb, HEAD_DIM, LANES), lambda b, h, i: (b, h, 0, 0), pipeline_mode=once),
            pl.BlockSpec((1, hb, HEAD_DIM, LANES), lambda b, h, i: (b, h, 0, 0), pipeline_mode=once),
        ],
        out_specs=pl.BlockSpec((1, tq, hb * HEAD_DIM), lambda b, h, i: (b, i, h)),
        out_shape=jax.ShapeDtypeStruct((B, S, D_MODEL), BF16),
        scratch_shapes=[pltpu.VMEM((hb * tq, 2 * LANES), BF16)] + _flash_scratch(hb * tq, tk),
        compiler_params=_cparams(("parallel", "parallel", "parallel")),
    )(proj3, kt, vaug3, e, kh, kl)


def _layer_norm(r, g, b):
    mu = jnp.mean(r, axis=-1, keepdims=True)
    c = r - mu
    var = jnp.mean(c * c, axis=-1, keepdims=True)
    return c * lax.rsqrt(var + LN_EPS) * g + b


def _outproj_body(*refs, n_o):
    o_refs = refs[:n_o]
    w_ref, x_ref, g_ref, b_ref, y_ref, yb_ref = refs[n_o:]
    o = o_refs[0][...].astype(F32)
    for r in o_refs[1:]:
        o = o + r[...].astype(F32)
    mix = _dot(o.astype(BF16), w_ref[...])
    y = _layer_norm(DN_ALPHA * x_ref[...] + mix, g_ref[...], b_ref[...])
    y_ref[...] = y
    yb_ref[...] = y.astype(BF16)


def _outproj_ln(os_, w, x, g, b, tm=512):
    T, D = x.shape
    tm = min(tm, T)
    n_o = len(os_)
    row = pl.BlockSpec((tm, D), lambda i: (i, 0))
    vec = pl.BlockSpec((1, D), lambda i: (0, 0))
    return pl.pallas_call(
        functools.partial(_outproj_body, n_o=n_o),
        grid=(T // tm,),
        in_specs=[row] * n_o + [pl.BlockSpec((D, D), lambda i: (0, 0)), row, vec, vec],
        out_specs=[row, row],
        out_shape=[jax.ShapeDtypeStruct((T, D), F32), jax.ShapeDtypeStruct((T, D), BF16)],
        compiler_params=_cparams(("parallel",)),
    )(*os_, w, x, g.reshape(1, D), b.reshape(1, D))


def _router_body(x_ref, wh_ref, wl_ref, bias_ref, gate_ref):
    x_hi, x_lo = _split_bf16(x_ref[...])
    wh = wh_ref[...]
    logits = _dot(x_hi, wh) + _dot(x_lo, wh) + _dot(x_hi, wl_ref[...])
    scores = _sigmoid(logits)
    lane = lax.broadcasted_iota(jnp.int32, scores.shape, 1)
    lanef = lane.astype(F32)
    live = lane < N_EXPERTS
    biased = jnp.where(live, scores + bias_ref[...], REMOVED)

    def top2(mask):
        v = jnp.where(mask, biased, REMOVED)
        m1 = jnp.max(v, axis=-1, keepdims=True)
        i1 = jnp.min(jnp.where(v == m1, lanef, float(LANES)), axis=-1, keepdims=True)
        v2 = jnp.where(lanef == i1, REMOVED, v)
        m2 = jnp.max(v2, axis=-1, keepdims=True)
        i2 = jnp.min(jnp.where(v2 == m2, lanef, float(LANES)), axis=-1, keepdims=True)
        return m1 + m2, jnp.where((lanef == i1) | (lanef == i2), 1.0, 0.0)

    best, best_sel = top2(lane // EXPERTS_PER_GROUP == 0)
    for grp in range(1, N_GROUPS):
        score, sel = top2(lane // EXPERTS_PER_GROUP == grp)
        better = score > best
        best = jnp.where(better, score, best)
        best_sel = jnp.where(better, sel, best_sel)
    w = best_sel * scores
    gate_ref[...] = w / jnp.sum(w, axis=-1, keepdims=True)


def _router(x, router_w, router_bias, tm=512):
    T, D = x.shape
    tm = min(tm, T)
    wpad = jnp.zeros((D, LANES), F32).at[:, :N_EXPERTS].set(router_w)
    wh, wl = _split_bf16(wpad)
    bpad = jnp.zeros((1, LANES), F32).at[0, :N_EXPERTS].set(router_bias)
    return pl.pallas_call(
        _router_body,
        grid=(T // tm,),
        in_specs=[
            pl.BlockSpec((tm, D), lambda i: (i, 0)),
            pl.BlockSpec((D, LANES), lambda i: (0, 0)),
            pl.BlockSpec((D, LANES), lambda i: (0, 0)),
            pl.BlockSpec((1, LANES), lambda i: (0, 0)),
        ],
        out_specs=pl.BlockSpec((tm, LANES), lambda i: (i, 0)),
        out_shape=jax.ShapeDtypeStruct((T, LANES), F32),
        compiler_params=_cparams(("parallel",)),
    )(x, wh, wl, bpad)


def _experts_body(xb_ref, x_ref, gate_ref, wg_ref, wu_ref, wd_ref, g_ref, b_ref, y_ref, yb_ref, acc_ref):
    e = pl.program_id(1)

    @pl.when(e == 0)
    def _():
        acc_ref[...] = jnp.zeros(acc_ref.shape, F32)

    xb = xb_ref[...]
    a = _dot(xb, wg_ref[0])
    u = _dot(xb, wu_ref[0])
    lane = lax.broadcasted_iota(jnp.int32, gate_ref.shape, 1)
    gcol = jnp.sum(jnp.where(lane == e, gate_ref[...], 0.0), axis=-1, keepdims=True)
    h = a * _sigmoid(a) * u * gcol
    acc_ref[...] += _dot(h.astype(BF16), wd_ref[0])

    @pl.when(e == N_EXPERTS - 1)
    def _():
        y = _layer_norm(DN_ALPHA * x_ref[...] + acc_ref[...], g_ref[...], b_ref[...])
        y_ref[...] = y
        yb_ref[...] = y.astype(BF16)


def _experts_ln(xb, x, gate, wg, wu, wd, g, b, tm=1024):
    T, D = x.shape
    tm = min(tm, T)
    E, _, DE = wg.shape
    row = pl.BlockSpec((tm, D), lambda i, e: (i, 0))
    vec = pl.BlockSpec((1, D), lambda i, e: (0, 0))
    return pl.pallas_call(
        _experts_body,
        grid=(T // tm, E),
        in_specs=[
            row, row,
            pl.BlockSpec((tm, LANES), lambda i, e: (i, 0)),
            pl.BlockSpec((1, D, DE), lambda i, e: (e, 0, 0)),
            pl.BlockSpec((1, D, DE), lambda i, e: (e, 0, 0)),
            pl.BlockSpec((1, DE, D), lambda i, e: (e, 0, 0)),
            vec, vec,
        ],
        out_specs=[row, row],
        out_shape=[jax.ShapeDtypeStruct((T, D), F32), jax.ShapeDtypeStruct((T, D), BF16)],
        scratch_shapes=[pltpu.VMEM((tm, D), F32)],
        compiler_params=_cparams(("parallel", "arbitrary")),
    )(xb, x, gate, wg, wu, wd, g.reshape(1, D), b.reshape(1, D))


def _kt_tiles(t, B, S, n, tk):
    kt = t.reshape(B, S // tk, tk, n, HEAD_DIM)
    return kt.transpose(0, 3, 1, 4, 2)


def _block_onehots(S, block, tk):
    key = np.arange(S).reshape(S // tk, 1, tk)
    r = np.arange(LANES).reshape(1, LANES, 1)
    return jnp.asarray((key // block == r).astype(np.float32), BF16)


def _rope_tiled(S):
    cos, sin = _rope_tables(jnp.arange(S))
    reps = LANES // HALF
    return jnp.tile(cos, (1, reps)), jnp.tile(sin, (1, reps))


def _nsa_mixer(xb, B, S, w_in, cmp_k_w1, cmp_k_w2, cmp_v_w1, cmp_v_w2, cmp_k_pos, cmp_v_pos):
    G, HPG, KV = NSA_KV_GROUPS, NSA_HPG, NSA_KV_DIM
    L, STR, SB = NSA_CMP_LEN, NSA_CMP_STRIDE, NSA_SEL_BLOCK
    assert L == 2 * STR and S % SB == 0 and S // SB <= LANES
    T = B * S
    n_cmp = (S - L) // STR + 1
    NC = S // STR
    n_sel = S // SB
    top_n = min(NSA_SEL_TOPN, n_sel)

    cos2, sin2 = _rope_tiled(S)
    wb = w_in.astype(BF16)
    wcol = lambda i: wb[:, D_MODEL + i * KV: D_MODEL + (i + 1) * KV]
    tn = 2 * KV
    w_main = jnp.concatenate([wb[:, :D_MODEL], wcol(2), wcol(4), wcol(0), wcol(1)], axis=1)
    proj = _proj(xb, w_main, cos2, sin2, [2] * (D_MODEL // tn) + [1, 0], S, tn=tn)
    vaug = _proj_vaug(xb, jnp.concatenate([wcol(3), wcol(5)], axis=1), tn=tn)
    wg = jnp.zeros((D_MODEL, LANES), BF16).at[:, :3 * N_HEADS].set(wb[:, D_MODEL + 6 * KV:])
    gates = _proj(xb, wg, cos2, sin2, [0], S, out_dtype=F32, tn=LANES)[:, :3 * N_HEADS]

    col = lambda i: proj[:, D_MODEL + i * KV: D_MODEL + (i + 1) * KV]
    proj3 = proj.reshape(B, S, proj.shape[1])
    vaug3 = vaug.reshape(B, S, vaug.shape[1])
    gcols = gates.reshape(B, S, 3, G, HPG).transpose(2, 0, 3, 4, 1)[..., None]

    ccos, csin = _rope_tables(jnp.arange(NC) * STR + (L - 1))
    ccos = jnp.concatenate([ccos, ccos], axis=1)
    csin = jnp.concatenate([csin, csin], axis=1)
    to_rows = lambda t: t.reshape(B, S, G, HEAD_DIM).transpose(0, 2, 1, 3).reshape(B * G, NC, STR * HEAD_DIM)
    kc = _compress(to_rows(col(2)), cmp_k_w1, cmp_k_pos, cmp_k_w2, ccos, csin, True, n_cmp)
    vc = _compress(to_rows(col(3)), cmp_v_w1, cmp_v_pos, cmp_v_w2, ccos, csin, False, n_cmp)
    kct = kc.reshape(B, G, NC, HEAD_DIM).transpose(0, 1, 3, 2)

    ci = np.arange(NC)[:, None]
    sj = np.arange(LANES)[None, :]
    overlap = ((ci * STR < (sj + 1) * SB) & (ci * STR + L > sj * SB) & (ci < n_cmp) & (sj < n_sel))
    overlap = jnp.broadcast_to(jnp.asarray(overlap.astype(np.float32), BF16), (B, G, NC, LANES))
    vo = jnp.concatenate([vc.reshape(B, G, NC, HEAD_DIM), jnp.ones((B, G, NC, 1), BF16),
                          jnp.zeros((B, G, NC, LANES - HEAD_DIM - 1), BF16), overlap], axis=-1)

    o_cmp, selb = _nsa_cmp(proj3, kct, vo, gcols[0], n_cmp, n_sel, top_n)
    tk_s = min(512, S)
    o_slc = _nsa_slc(proj3, _kt_tiles(col(0), B, S, G, tk_s), vaug3, 0,
                     _block_onehots(S, SB, tk_s), selb, gcols[1], tq=256, tk=tk_s)
    tq_w = NSA_WINDOW // 2
    o_win = _nsa_win(proj3, _kt_tiles(col(1), B, S, G, tq_w), vaug3, G, gcols[2], tq=tq_w)
    return [o.reshape(T, D_MODEL) for o in (o_cmp, o_slc, o_win)]


def _moba_mixer(xb, B, S, w_in):
    H = N_HEADS
    nb = S // MOBA_BLOCK
    top_k = min(MOBA_TOPK, nb)
    cos2, sin2 = _rope_tiled(S)
    tn = 512
    n_t = D_MODEL // tn
    wb = w_in.astype(BF16)
    proj = _proj(xb, wb[:, :2 * D_MODEL], cos2, sin2, [2] * n_t + [1] * n_t, S, tn=tn)
    vaug = _proj_vaug(xb, wb[:, 2 * D_MODEL:], tn=tn)
    tk = min(2 * MOBA_BLOCK, S)
    kt = _kt_tiles(proj[:, D_MODEL:], B, S, H, tk)
    e = _block_onehots(S, MOBA_BLOCK, tk)
    kh, kl = _moba_kmean(kt, e.transpose(0, 2, 1))
    o = _moba_attn(proj.reshape(B, S, 2 * D_MODEL), kt, vaug.reshape(B, S, H * LANES), e, kh, kl, nb, top_k,
                   tq=MOBA_BLOCK, tk=tk)
    return [o.reshape(B * S, D_MODEL)]


def kernel(x, nsa_w_in, nsa_w_out, nsa_cmp_k_w1, nsa_cmp_k_w2, nsa_cmp_v_w1, nsa_cmp_v_w2, nsa_cmp_k_pos, nsa_cmp_v_pos, moba_w_in, moba_w_out, router_w, router_bias, moe_w_gate, moe_w_up, moe_w_down, ln_g, ln_b):
    B, S, D = x.shape
    xf = x.reshape(B * S, D)
    xb = xf.astype(BF16)
    for layer in range(DEPTH):
        j = layer // 2
        if layer % 2 == 0:
            os_ = _nsa_mixer(xb, B, S, nsa_w_in[j], nsa_cmp_k_w1[j], nsa_cmp_k_w2[j], nsa_cmp_v_w1[j],
                             nsa_cmp_v_w2[j], nsa_cmp_k_pos[j], nsa_cmp_v_pos[j])
            w_out = nsa_w_out[j]
        else:
            os_ = _moba_mixer(xb, B, S, moba_w_in[j])
            w_out = moba_w_out[j]
        xf, xb = _outproj_ln(os_, w_out.astype(BF16), xf, ln_g[layer, 0], ln_b[layer, 0])
        gate = _router(xf, router_w, router_bias)
        xf, xb = _experts_ln(xb, xf, gate, moe_w_gate[layer].astype(BF16), moe_w_up[layer].astype(BF16),
                             moe_w_down[layer].astype(BF16), ln_g[layer, 1], ln_b[layer, 1])
    return xf.reshape(B, S, D)
```

```python
import functools

import jax
import jax.numpy as jnp
import numpy as np
from jax import lax
from jax.experimental import pallas as pl
from jax.experimental.pallas import tpu as pltpu

F32 = jnp.float32
BF16 = jnp.bfloat16

D_MODEL = 1024
N_HEADS = 16
HEAD_DIM = 64
HALF = HEAD_DIM // 2
ROPE_THETA = 10000.0
DEPTH = 2
DN_ALPHA = (2 * DEPTH) ** 0.25
LN_EPS = 1e-5
NEG = -1e30
FORCE = 1e9
MASK = -1e30
REMOVED = -3.0e38
LANES = 128
Q_SCALE_LOG2 = float(HEAD_DIM ** -0.5 * np.log2(np.e))
ROW_CHUNK = 128

NSA_KV_GROUPS = 4
NSA_HPG = N_HEADS // NSA_KV_GROUPS
NSA_KV_DIM = NSA_KV_GROUPS * HEAD_DIM
NSA_CMP_LEN = 32
NSA_CMP_STRIDE = 16
NSA_SEL_BLOCK = 64
NSA_SEL_TOPN = 16
NSA_WINDOW = 512

MOBA_BLOCK = 256
MOBA_TOPK = 3

N_EXPERTS = 16
N_GROUPS = 4
EXPERTS_PER_GROUP = N_EXPERTS // N_GROUPS
D_EXPERT = 512

VMEM_LIMIT = 48 * 1024 * 1024


def _cparams(sem):
    return pltpu.CompilerParams(dimension_semantics=sem, vmem_limit_bytes=VMEM_LIMIT)


def _dot(a, b):
    return jnp.dot(a, b, preferred_element_type=F32)


def _split_bf16(x):
    hi = x.astype(BF16)
    lo = (x - hi.astype(F32)).astype(BF16)
    return hi, lo


def _sigmoid(x):
    return 1.0 / (1.0 + jnp.exp(-x))


def _proj_body(mode_ref, x_ref, w_ref, cos_ref, sin_ref, o_ref, *, tn):
    j = pl.program_id(0)
    mode = mode_ref[j]
    acc = _dot(x_ref[...], w_ref[...])

    @pl.when(mode == 0)
    def _():
        o_ref[...] = acc.astype(o_ref.dtype)

    @pl.when(mode != 0)
    def _():
        sc = jnp.where(mode == 2, Q_SCALE_LOG2, 1.0).astype(F32)
        cos = cos_ref[...] * sc
        sin = sin_ref[...] * sc
        lane = lax.broadcasted_iota(jnp.int32, cos.shape, 1)
        lower = (lane % HEAD_DIM) < HALF
        for c in range(tn // LANES):
            a = acc[:, c * LANES:(c + 1) * LANES]
            up = pltpu.roll(a, LANES - HALF, 1)
            dn = pltpu.roll(a, HALF, 1)
            partner = jnp.where(lower, -up, dn)
            o_ref[:, c * LANES:(c + 1) * LANES] = (a * cos + partner * sin).astype(o_ref.dtype)


def _proj(xb, w, cos2, sin2, modes, seq, out_dtype=BF16, tm=1024, tn=512):
    T, K = xb.shape
    N = w.shape[1]
    tm = min(tm, seq)
    assert T % tm == 0 and N % tn == 0 and seq % tm == 0 and len(modes) == N // tn
    n_pos = seq // tm
    grid_spec = pltpu.PrefetchScalarGridSpec(
        num_scalar_prefetch=1,
        grid=(N // tn, T // tm),
        in_specs=[
            pl.BlockSpec((tm, K), lambda j, i, m: (i, 0)),
            pl.BlockSpec((K, tn), lambda j, i, m: (0, j)),
            pl.BlockSpec((tm, LANES), lambda j, i, m: (i % n_pos, 0)),
            pl.BlockSpec((tm, LANES), lambda j, i, m: (i % n_pos, 0)),
        ],
        out_specs=pl.BlockSpec((tm, tn), lambda j, i, m: (i, j)),
    )
    return pl.pallas_call(
        functools.partial(_proj_body, tn=tn),
        grid_spec=grid_spec,
        out_shape=jax.ShapeDtypeStruct((T, N), out_dtype),
        compiler_params=_cparams(("parallel", "parallel")),
    )(jnp.asarray(modes, jnp.int32), xb, w, cos2, sin2)


def _rope_tables(pos):
    inv = 1.0 / (ROPE_THETA ** (jnp.arange(0, HEAD_DIM, 2, dtype=F32) / HEAD_DIM))
    ang = pos.astype(F32)[:, None] * inv[None, :]
    return jnp.cos(ang), jnp.sin(ang)


def _proj_vaug_body(x_ref, w_ref, o_ref, *, tn):
    acc = _dot(x_ref[...], w_ref[...])
    lane = lax.broadcasted_iota(jnp.int32, (acc.shape[0], LANES), 1)
    tail = jnp.where(lane == HEAD_DIM, 1.0, 0.0)
    for c in range(tn // LANES):
        a = acc[:, c * LANES:(c + 1) * LANES]
        for k, head in enumerate((a, pltpu.roll(a, HEAD_DIM, 1))):
            o_ref[:, (2 * c + k) * LANES:(2 * c + k + 1) * LANES] = (
                jnp.where(lane < HEAD_DIM, head, tail).astype(o_ref.dtype))


def _proj_vaug(xb, w, tm=1024, tn=512):
    T, K = xb.shape
    N = w.shape[1]
    tm = min(tm, T)
    assert T % tm == 0 and N % tn == 0
    return pl.pallas_call(
        functools.partial(_proj_vaug_body, tn=tn),
        grid=(N // tn, T // tm),
        in_specs=[pl.BlockSpec((tm, K), lambda j, i: (i, 0)), pl.BlockSpec((K, tn), lambda j, i: (0, j))],
        out_specs=pl.BlockSpec((tm, 2 * tn), lambda j, i: (i, j)),
        out_shape=jax.ShapeDtypeStruct((T, 2 * N), BF16),
        compiler_params=_cparams(("parallel", "parallel")),
    )(xb, w)


def _gelu_tanh(x):
    c = np.float32(np.sqrt(2.0 / np.pi))
    return 0.5 * x * (1.0 + jnp.tanh(c * (x + 0.044715 * (x * x * x))))


def _compress_body(r_ref, w1_ref, pos_ref, w2_ref, w2r_ref, cos_ref, sin_ref, o_ref, *, rope, n_cmp):
    r = r_ref[0]
    nc = r.shape[0]
    half = NSA_CMP_STRIDE * HEAD_DIM
    a = _dot(r, w1_ref[0])
    b = _dot(r, w1_ref[1])
    pos = pos_ref[...]
    pb = _dot(pos[:, :half], w1_ref[0]) + _dot(pos[:, half:], w1_ref[1])
    b_next = pltpu.roll(b, nc - 1, 0)
    h = _gelu_tanh(a + b_next + pb[0:1, :]).astype(BF16)
    o = _dot(h, w2_ref[...])
    if rope:
        o = o * cos_ref[...] + _dot(h, w2r_ref[...]) * sin_ref[...]
    row = lax.broadcasted_iota(jnp.int32, o.shape, 0)
    o_ref[0] = jnp.where(row < n_cmp, o, 0.0).astype(o_ref.dtype)


def _compress(r, w1, pos, w2, cos_c, sin_c, rope, n_cmp):
    BG, NC, K = r.shape
    hidden = w1.shape[1]
    w1s = w1.astype(BF16).reshape(2, K, hidden)
    pos8 = jnp.zeros((8, 2 * K), BF16).at[0].set(pos.reshape(-1).astype(BF16))
    w2r = jnp.concatenate([-w2[:, HALF:], w2[:, :HALF]], axis=1).astype(BF16)
    full = lambda shape: pl.BlockSpec(shape, lambda i: (0,) * len(shape))
    return pl.pallas_call(
        functools.partial(_compress_body, rope=rope, n_cmp=n_cmp),
        grid=(BG,),
        in_specs=[
            pl.BlockSpec((1, NC, K), lambda i: (i, 0, 0)),
            full((2, K, hidden)),
            full((8, 2 * K)),
            full((hidden, HEAD_DIM)),
            full((hidden, HEAD_DIM)),
            full((NC, HEAD_DIM)),
            full((NC, HEAD_DIM)),
        ],
        out_specs=pl.BlockSpec((1, NC, HEAD_DIM), lambda i: (i, 0, 0)),
        out_shape=jax.ShapeDtypeStruct((BG, NC, HEAD_DIM), BF16),
        compiler_params=_cparams(("parallel",)),
    )(r, w1s, pos8, w2.astype(BF16), w2r, cos_c, sin_c)


def _topk_mask_t(v, k):
    idx = lax.broadcasted_iota(jnp.int32, v.shape, 0).astype(F32)

    def step(_, cur):
        m = jnp.max(cur, axis=0, keepdims=True)
        first = jnp.min(jnp.where(cur == m, idx, float(LANES)), axis=0, keepdims=True)
        return jnp.where(idx == first, REMOVED, cur)

    return lax.fori_loop(0, k, step, v, unroll=True) != v


def _head_kt(e, kt, parity):
    z = jnp.zeros_like(kt)
    parts = ([] if e is None else [e]) + ([kt, z] if parity == 0 else [z, kt])
    return jnp.concatenate(parts, axis=0)


def _flash_init(m_ref, acc_ref):
    m_ref[...] = jnp.full(m_ref.shape, -jnp.inf, F32)
    acc_ref[...] = jnp.zeros(acc_ref.shape, F32)


def _flash_step(jobs, s_ref, m_ref, alpha_ref, acc_ref):
    for r0, q, kt, _, mask in jobs:
        rows = q.shape[0]
        s = _dot(q, kt)
        if mask is not None:
            s = jnp.where(mask, s, MASK)
        s_ref[r0:r0 + rows, :] = s
        cm = s[:, :LANES]
        for c in range(1, s.shape[1] // LANES):
            cm = jnp.maximum(cm, s[:, c * LANES:(c + 1) * LANES])
        m_prev = m_ref[r0:r0 + rows, :]
        m_new = jnp.maximum(m_prev, jnp.max(cm, axis=-1, keepdims=True))
        alpha_ref[r0:r0 + rows, :] = jnp.exp2(m_prev - m_new)
        m_ref[r0:r0 + rows, :] = m_new
    for r0, q, _, v, _ in jobs:
        for r in range(r0, r0 + q.shape[0], ROW_CHUNK):
            rows = min(ROW_CHUNK, r0 + q.shape[0] - r)
            m = m_ref[r:r + rows, :]
            p = jnp.concatenate(
                [jnp.exp2(s_ref[r:r + rows, c * LANES:(c + 1) * LANES] - m)
                 for c in range(s_ref.shape[1] // LANES)], axis=1).astype(BF16)
            acc_ref[r:r + rows, :] = alpha_ref[r:r + rows, :] * acc_ref[r:r + rows, :] + _dot(p, v)


def _flash_scratch(rows, tk):
    return [pltpu.VMEM((rows, tk), F32), pltpu.VMEM((rows, LANES), F32),
            pltpu.VMEM((rows, LANES), F32), pltpu.VMEM((rows, LANES), F32)]


def _pair_merge(even, odd):
    lane = lax.broadcasted_iota(jnp.int32, even.shape, 1)
    return jnp.where(lane < HEAD_DIM, even, pltpu.roll(odd, HEAD_DIM, 1))


def _flash_pairs_out(acc_ref, tq, n_heads, gates):
    def head(h):
        acc = acc_ref[h * tq:(h + 1) * tq, :]
        return acc * (gates[h] / acc[:, HEAD_DIM:HEAD_DIM + 1])
    return jnp.concatenate([_pair_merge(head(h), head(h + 1)) for h in range(0, n_heads, 2)], axis=1)


def _tile_iotas(rows, tk):
    return (lax.broadcasted_iota(jnp.int32, (rows, tk), 0),
            lax.broadcasted_iota(jnp.int32, (rows, tk), 1))


def _nsa_cmp_body(q_ref, kct_ref, vo_ref, g_ref, o_ref, selb_ref, s_ref, m_ref, *, tq, n_cmp, n_sel, top_n):
    s0 = pl.program_id(2) * tq
    q4 = q_ref[0]
    kct = kct_ref[0, 0]
    vo = vo_ref[0, 0]
    nc = vo.shape[0]
    tpos = s0 + lax.broadcasted_iota(jnp.int32, (tq, nc), 0)
    nidx = lax.broadcasted_iota(jnp.int32, (tq, nc), 1)
    cmask = (nidx * NSA_CMP_STRIDE + (NSA_CMP_LEN - 1) <= tpos) & (nidx < n_cmp)
    for h in range(NSA_HPG):
        pair = q4[:, (h // 2) * LANES:(h // 2 + 1) * LANES]
        s = jnp.where(cmask, _dot(pair, _head_kt(None, kct, h % 2)), NEG)
        s_ref[h * tq:(h + 1) * tq, :] = s
        cm = s[:, :LANES]
        for c in range(1, nc // LANES):
            cm = jnp.maximum(cm, s[:, c * LANES:(c + 1) * LANES])
        m_ref[h * tq:(h + 1) * tq, :] = jnp.broadcast_to(jnp.max(cm, axis=-1, keepdims=True), (tq, LANES))

    rc = min(ROW_CHUNK, tq)
    imps = []
    for r0 in range(0, tq, rc):
        imp = jnp.zeros((rc, LANES), F32)
        heads = []
        for h in range(NSA_HPG):
            r = h * tq + r0
            m = m_ref[r:r + rc, :]
            e = jnp.concatenate(
                [jnp.exp2(s_ref[r:r + rc, c * LANES:(c + 1) * LANES] - m) for c in range(nc // LANES)],
                axis=1).astype(BF16)
            res = _dot(e, vo)
            inv = jnp.where(m[:, :1] > 0.5 * NEG, 1.0 / res[:, HEAD_DIM:HEAD_DIM + 1], 0.0)
            heads.append(res[:, :LANES] * (inv * _sigmoid(g_ref[0, 0, h, r0:r0 + rc, :])))
            imp = imp + res[:, LANES:] * inv
        o_ref[0, r0:r0 + rc, :] = jnp.concatenate(
            [_pair_merge(heads[h], heads[h + 1]) for h in range(0, NSA_HPG, 2)], axis=1).astype(o_ref.dtype)
        imps.append(imp)
    imp = jnp.concatenate(imps, axis=0)

    blk = lax.broadcasted_iota(jnp.int32, imp.shape, 1)
    jq = (s0 + lax.broadcasted_iota(jnp.int32, imp.shape, 0)) // NSA_SEL_BLOCK
    forced = (blk == 0) | (blk == jq) | (blk == jq - 1)
    imp = jnp.where(blk > jq, NEG, jnp.where(forced, FORCE, imp))
    imp = jnp.where(blk < n_sel, imp, REMOVED)
    sel_t = _topk_mask_t(imp.T, top_n)
    selb_ref[0, 0] = jnp.where(sel_t, 0.0, MASK).T.astype(selb_ref.dtype)


def _nsa_cmp(proj3, kct, vo, g_cmp, n_cmp, n_sel, top_n, tq=256):
    B, S, _ = proj3.shape
    G, HPG = NSA_KV_GROUPS, NSA_HPG
    NC = vo.shape[2]
    tq = min(tq, S)
    assert NC % LANES == 0
    return pl.pallas_call(
        functools.partial(_nsa_cmp_body, tq=tq, n_cmp=n_cmp, n_sel=n_sel, top_n=top_n),
        grid=(B, G, S // tq),
        in_specs=[
            pl.BlockSpec((1, tq, HPG * HEAD_DIM), lambda b, g, i: (b, i, g)),
            pl.BlockSpec((1, 1, HEAD_DIM, NC), lambda b, g, i: (b, g, 0, 0)),
            pl.BlockSpec((1, 1, NC, 2 * LANES), lambda b, g, i: (b, g, 0, 0)),
            pl.BlockSpec((1, 1, HPG, tq, 1), lambda b, g, i: (b, g, 0, i, 0)),
        ],
        out_specs=[
            pl.BlockSpec((1, tq, HPG * HEAD_DIM), lambda b, g, i: (b, i, g)),
            pl.BlockSpec((1, 1, tq, LANES), lambda b, g, i: (b, g, i, 0)),
        ],
        out_shape=[
            jax.ShapeDtypeStruct((B, S, D_MODEL), BF16),
            jax.ShapeDtypeStruct((B, G, S, LANES), BF16),
        ],
        scratch_shapes=[pltpu.VMEM((HPG * tq, NC), F32), pltpu.VMEM((HPG * tq, LANES), F32)],
        compiler_params=_cparams(("parallel", "parallel", "parallel")),
    )(proj3, kct, vo, g_cmp)


def _nsa_slc_body(q_ref, kt_ref, v_ref, e_ref, selb_ref, g_ref, o_ref, qa_ref, s_ref, m_ref, alpha_ref,
                  acc_ref, *, tq, tk):
    s0 = pl.program_id(2) * tq
    _flash_init(m_ref, acc_ref)
    selb = selb_ref[0, 0]
    q4 = q_ref[0]
    for h in range(NSA_HPG):
        qa_ref[h * tq:(h + 1) * tq, :] = jnp.concatenate(
            [selb, q4[:, (h // 2) * LANES:(h // 2 + 1) * LANES]], axis=1)

    def tile(ki, causal):
        k0 = pl.multiple_of(ki * tk, tk)
        kts = [_head_kt(e_ref[ki], kt_ref[0, 0, ki], parity) for parity in range(2)]
        v = v_ref[0, pl.ds(k0, tk), :]
        row, col = _tile_iotas(tq, tk)
        mask = (k0 + col <= s0 + row) if causal else None
        jobs = [(h * tq, qa_ref[h * tq:(h + 1) * tq, :], kts[h % 2], v, mask) for h in range(NSA_HPG)]
        _flash_step(jobs, s_ref, m_ref, alpha_ref, acc_ref)

    last = (s0 + tq - 1) // tk

    def full_tile(ki, carry):
        tile(ki, False)
        return carry

    lax.fori_loop(0, last, full_tile, 0)
    tile(last, True)
    gates = [_sigmoid(g_ref[0, 0, h]) for h in range(NSA_HPG)]
    o_ref[0] = _flash_pairs_out(acc_ref, tq, NSA_HPG, gates).astype(o_ref.dtype)


def _nsa_slc(proj3, kt, vaug3, v_block, e, selb, g, tq, tk):
    B, S, _ = proj3.shape
    G, HPG = NSA_KV_GROUPS, NSA_HPG
    nk = S // tk
    assert tk % tq == 0 and kt.shape == (B, G, nk, HEAD_DIM, tk) and e.shape == (nk, LANES, tk)
    once = pl.Buffered(1)
    return pl.pallas_call(
        functools.partial(_nsa_slc_body, tq=tq, tk=tk),
        grid=(B, G, S // tq),
        in_specs=[
            pl.BlockSpec((1, tq, HPG * HEAD_DIM), lambda b, g_, i: (b, i, g_)),
            pl.BlockSpec((1, 1, nk, HEAD_DIM, tk), lambda b, g_, i: (b, g_, 0, 0, 0), pipeline_mode=once),
            pl.BlockSpec((1, S, LANES), lambda b, g_, i: (b, 0, v_block + g_), pipeline_mode=once),
            pl.BlockSpec((nk, LANES, tk), lambda b, g_, i: (0, 0, 0), pipeline_mode=once),
            pl.BlockSpec((1, 1, tq, LANES), lambda b, g_, i: (b, g_, i, 0)),
            pl.BlockSpec((1, 1, HPG, tq, 1), lambda b, g_, i: (b, g_, 0, i, 0)),
        ],
        out_specs=pl.BlockSpec((1, tq, HPG * HEAD_DIM), lambda b, g_, i: (b, i, g_)),
        out_shape=jax.ShapeDtypeStruct((B, S, D_MODEL), BF16),
        scratch_shapes=[pltpu.VMEM((HPG * tq, 2 * LANES), BF16)] + _flash_scratch(HPG * tq, tk),
        compiler_params=_cparams(("parallel", "parallel", "parallel")),
    )(proj3, kt, vaug3, e, selb, g)


def _nsa_win_body(q_ref, kt_ref, v_ref, g_ref, o_ref, s_ref, m_ref, *, tq, n_tiles):
    s0 = pl.program_id(2) * tq
    tk = n_tiles * tq
    k_first = jnp.maximum(pl.program_id(2) - (n_tiles - 1), 0)
    k0 = pl.multiple_of(k_first * tq, tq)
    kt = jnp.concatenate([kt_ref[0, 0, k_first + j] for j in range(n_tiles)], axis=1)
    v = v_ref[0, pl.ds(k0, tk), :]
    row, col = _tile_iotas(tq, tk)
    dist = (s0 - k0) + row - col
    valid = (dist >= 0) & (dist < NSA_WINDOW)
    q4 = q_ref[0]
    for h in range(NSA_HPG):
        pair = q4[:, (h // 2) * LANES:(h // 2 + 1) * LANES]
        s = jnp.where(valid, _dot(pair, _head_kt(None, kt, h % 2)), MASK)
        s_ref[h * tq:(h + 1) * tq, :] = s
        cm = s[:, :LANES]
        for c in range(1, tk // LANES):
            cm = jnp.maximum(cm, s[:, c * LANES:(c + 1) * LANES])
        m_ref[h * tq:(h + 1) * tq, :] = jnp.broadcast_to(jnp.max(cm, axis=-1, keepdims=True), (tq, LANES))

    rc = min(ROW_CHUNK, tq)
    for r0 in range(0, tq, rc):
        heads = []
        for h in range(NSA_HPG):
            r = h * tq + r0
            m = m_ref[r:r + rc, :]
            p = jnp.concatenate(
                [jnp.exp2(s_ref[r:r + rc, c * LANES:(c + 1) * LANES] - m) for c in range(tk // LANES)],
                axis=1).astype(BF16)
            res = _dot(p, v)
            heads.append(res * (_sigmoid(g_ref[0, 0, h, r0:r0 + rc, :]) / res[:, HEAD_DIM:HEAD_DIM + 1]))
        o_ref[0, r0:r0 + rc, :] = jnp.concatenate(
            [_pair_merge(heads[h], heads[h + 1]) for h in range(0, NSA_HPG, 2)], axis=1).astype(o_ref.dtype)


def _nsa_win(proj3, kt, vaug3, v_block, g, tq):
    B, S, _ = proj3.shape
    G, HPG = NSA_KV_GROUPS, NSA_HPG
    nk = S // tq
    n_tiles = NSA_WINDOW // tq + 1
    assert NSA_WINDOW % tq == 0 and nk >= n_tiles and kt.shape == (B, G, nk, HEAD_DIM, tq)
    once = pl.Buffered(1)
    return pl.pallas_call(
        functools.partial(_nsa_win_body, tq=tq, n_tiles=n_tiles),
        grid=(B, G, S // tq),
        in_specs=[
            pl.BlockSpec((1, tq, HPG * HEAD_DIM), lambda b, g_, i: (b, i, g_)),
            pl.BlockSpec((1, 1, nk, HEAD_DIM, tq), lambda b, g_, i: (b, g_, 0, 0, 0), pipeline_mode=once),
            pl.BlockSpec((1, S, LANES), lambda b, g_, i: (b, 0, v_block + g_), pipeline_mode=once),
            pl.BlockSpec((1, 1, HPG, tq, 1), lambda b, g_, i: (b, g_, 0, i, 0)),
        ],
        out_specs=pl.BlockSpec((1, tq, HPG * HEAD_DIM), lambda b, g_, i: (b, i, g_)),
        out_shape=jax.ShapeDtypeStruct((B, S, D_MODEL), BF16),
        scratch_shapes=[pltpu.VMEM((HPG * tq, n_tiles * tq), F32), pltpu.VMEM((HPG * tq, LANES), F32)],
        compiler_params=_cparams(("parallel", "parallel", "parallel")),
    )(proj3, kt, vaug3, g)


def _moba_kmean_body(kt_ref, et_ref, hi_ref, lo_ref):
    nk = kt_ref.shape[2]
    km = jnp.zeros((HEAD_DIM, LANES), F32)
    for ki in range(nk):
        km = km + _dot(kt_ref[0, 0, ki], et_ref[ki])
    hi, lo = _split_bf16(km * (1.0 / MOBA_BLOCK))
    hi_ref[0, 0] = hi
    lo_ref[0, 0] = lo


def _moba_kmean(kt, et):
    B, H, nk, _, tk = kt.shape
    spec = pl.BlockSpec((1, 1, HEAD_DIM, LANES), lambda b, h: (b, h, 0, 0))
    return pl.pallas_call(
        _moba_kmean_body,
        grid=(B, H),
        in_specs=[
            pl.BlockSpec((1, 1, nk, HEAD_DIM, tk), lambda b, h: (b, h, 0, 0, 0)),
            pl.BlockSpec((nk, tk, LANES), lambda b, h: (0, 0, 0)),
        ],
        out_specs=[spec, spec],
        out_shape=[jax.ShapeDtypeStruct((B, H, HEAD_DIM, LANES), BF16)] * 2,
        compiler_params=_cparams(("parallel", "parallel")),
    )(kt, et)


def _moba_body(q_ref, kt_ref, v_ref, e_ref, kh_ref, kl_ref, o_ref, qa_ref, s_ref, m_ref, alpha_ref,
               acc_ref, *, tq, tk, hb, nb, top_k):
    s0 = pl.program_id(2) * tq
    cb = s0 // MOBA_BLOCK
    _flash_init(m_ref, acc_ref)

    blk = lax.broadcasted_iota(jnp.int32, (LANES, tq), 0)
    qs = q_ref[0]
    for h in range(hb):
        pair = qs[:, (h // 2) * LANES:(h // 2 + 1) * LANES]
        gsc = (_dot(pair, _head_kt(None, kh_ref[0, h], h % 2))
               + _dot(pair, _head_kt(None, kl_ref[0, h], h % 2))).T
        gsc = jnp.where(blk < cb, gsc, NEG)
        gsc = jnp.where(blk < nb, gsc, REMOVED)
        sel = (_topk_mask_t(gsc, top_k) & (blk < cb)) | (blk == cb)
        qa_ref[h * tq:(h + 1) * tq, :] = jnp.concatenate(
            [jnp.where(sel, 0.0, MASK).T.astype(BF16), pair], axis=1)

    def tile(ki, causal):
        k0 = pl.multiple_of(ki * tk, tk)
        e = e_ref[ki]
        row, col = _tile_iotas(tq, tk)
        mask = (k0 + col <= s0 + row) if causal else None
        jobs = [(h * tq, qa_ref[h * tq:(h + 1) * tq, :], _head_kt(e, kt_ref[0, h, ki], h % 2),
                 v_ref[0, pl.ds(k0, tk), h * LANES:(h + 1) * LANES], mask) for h in range(hb)]
        _flash_step(jobs, s_ref, m_ref, alpha_ref, acc_ref)

    last = (cb * MOBA_BLOCK) // tk

    def full_tile(ki, carry):
        tile(ki, False)
        return carry

    lax.fori_loop(0, last, full_tile, 0)
    tile(last, True)
    o_ref[0] = _flash_pairs_out(acc_ref, tq, hb, [1.0] * hb).astype(o_ref.dtype)


def _moba_attn(proj3, kt, vaug3, e, kh, kl, nb, top_k, tq, tk, hb=4):
    B, S, _ = proj3.shape
    H = N_HEADS
    nk = S // tk
    assert MOBA_BLOCK % tq == 0 and tk % MOBA_BLOCK == 0 and H % hb == 0 and hb % 2 == 0 and nb <= LANES
    once = pl.Buffered(1)
    return pl.pallas_call(
        functools.partial(_moba_body, tq=tq, tk=tk, hb=hb, nb=nb, top_k=top_k),
        grid=(B, H // hb, S // tq),
        in_specs=[
            pl.BlockSpec((1, tq, hb * HEAD_DIM), lambda b, h, i: (b, i, h)),
            pl.BlockSpec((1, hb, nk, HEAD_DIM, tk), lambda b, h, i: (b, h, 0, 0, 0), pipeline_mode=once),
            pl.BlockSpec((1, S, hb * LANES), lambda b, h, i: (b, 0, h), pipeline_mode=once),
            pl.BlockSpec((nk, LANES, tk), lambda b, h, i: (0, 0, 0), pipeline_mode=once),
            pl.BlockSpec((1, hb, HEAD_DIM, LANES), lambda b, h, i: (b, h, 0, 0), pipeline_mode=once),
            pl.BlockSpec((1, hb, HEAD_DIM, LANES), lambda b, h, i: (b, h, 0, 0), pipeline_mode=once),
        ],
        out_specs=pl.BlockSpec((1, tq, hb * HEAD_DIM), lambda b, h, i: (b, i, h)),
        out_shape=jax.ShapeDtypeStruct((B, S, D_MODEL), BF16),
        scratch_shapes=[pltpu.VMEM((hb * tq, 2 * LANES), BF16)] + _flash_scratch(hb * tq, tk),
        compiler_params=_cparams(("parallel", "parallel", "parallel")),
    )(proj3, kt, vaug3, e, kh, kl)


def _layer_norm(r, g, b):
    mu = jnp.mean(r, axis=-1, keepdims=True)
    c = r - mu
    var = jnp.mean(c * c, axis=-1, keepdims=True)
    return c * lax.rsqrt(var + LN_EPS) * g + b


def _outproj_body(*refs, n_o):
    o_refs = refs[:n_o]
    w_ref, x_ref, g_ref, b_ref, y_ref, yb_ref = refs[n_o:]
    o = o_refs[0][...].astype(F32)
    for r in o_refs[1:]:
        o = o + r[...].astype(F32)
    mix = _dot(o.astype(BF16), w_ref[...])
    y = _layer_norm(DN_ALPHA * x_ref[...] + mix, g_ref[...], b_ref[...])
    y_ref[...] = y
    yb_ref[...] = y.astype(BF16)


def _outproj_ln(os_, w, x, g, b, tm=512):
    T, D = x.shape
    tm = min(tm, T)
    n_o = len(os_)
    row = pl.BlockSpec((tm, D), lambda i: (i, 0))
    vec = pl.BlockSpec((1, D), lambda i: (0, 0))
    return pl.pallas_call(
        functools.partial(_outproj_body, n_o=n_o),
        grid=(T // tm,),
        in_specs=[row] * n_o + [pl.BlockSpec((D, D), lambda i: (0, 0)), row, vec, vec],
        out_specs=[row, row],
        out_shape=[jax.ShapeDtypeStruct((T, D), F32), jax.ShapeDtypeStruct((T, D), BF16)],
        compiler_params=_cparams(("parallel",)),
    )(*os_, w, x, g.reshape(1, D), b.reshape(1, D))


def _router_body(x_ref, wh_ref, wl_ref, bias_ref, gate_ref):
    x_hi, x_lo = _split_bf16(x_ref[...])
    wh = wh_ref[...]
    logits = _dot(x_hi, wh) + _dot(x_lo, wh) + _dot(x_hi, wl_ref[...])
    scores = _sigmoid(logits)
    lane = lax.broadcasted_iota(jnp.int32, scores.shape, 1)
    lanef = lane.astype(F32)
    live = lane < N_EXPERTS
    biased = jnp.where(live, scores + bias_ref[...], REMOVED)

    def top2(mask):
        v = jnp.where(mask, biased, REMOVED)
        m1 = jnp.max(v, axis=-1, keepdims=True)
        i1 = jnp.min(jnp.where(v == m1, lanef, float(LANES)), axis=-1, keepdims=True)
        v2 = jnp.where(lanef == i1, REMOVED, v)
        m2 = jnp.max(v2, axis=-1, keepdims=True)
        i2 = jnp.min(jnp.where(v2 == m2, lanef, float(LANES)), axis=-1, keepdims=True)
        return m1 + m2, jnp.where((lanef == i1) | (lanef == i2), 1.0, 0.0)

    best, best_sel = top2(lane // EXPERTS_PER_GROUP == 0)
    for grp in range(1, N_GROUPS):
        score, sel = top2(lane // EXPERTS_PER_GROUP == grp)
        better = score > best
        best = jnp.where(better, score, best)
        best_sel = jnp.where(better, sel, best_sel)
    w = best_sel * scores
    gate_ref[...] = w / jnp.sum(w, axis=-1, keepdims=True)


def _router(x, router_w, router_bias, tm=512):
    T, D = x.shape
    tm = min(tm, T)
    wpad = jnp.zeros((D, LANES), F32).at[:, :N_EXPERTS].set(router_w)
    wh, wl = _split_bf16(wpad)
    bpad = jnp.zeros((1, LANES), F32).at[0, :N_EXPERTS].set(router_bias)
    return pl.pallas_call(
        _router_body,
        grid=(T // tm,),
        in_specs=[
            pl.BlockSpec((tm, D), lambda i: (i, 0)),
            pl.BlockSpec((D, LANES), lambda i: (0, 0)),
            pl.BlockSpec((D, LANES), lambda i: (0, 0)),
            pl.BlockSpec((1, LANES), lambda i: (0, 0)),
        ],
        out_specs=pl.BlockSpec((tm, LANES), lambda i: (i, 0)),
        out_shape=jax.ShapeDtypeStruct((T, LANES), F32),
        compiler_params=_cparams(("parallel",)),
    )(x, wh, wl, bpad)


def _experts_body(xb_ref, x_ref, gate_ref, wg_ref, wu_ref, wd_ref, g_ref, b_ref, y_ref, yb_ref, acc_ref):
    e = pl.program_id(1)

    @pl.when(e == 0)
    def _():
        acc_ref[...] = jnp.zeros(acc_ref.shape, F32)

    xb = xb_ref[...]
    a = _dot(xb, wg_ref[0])
    u = _dot(xb, wu_ref[0])
    lane = lax.broadcasted_iota(jnp.int32, gate_ref.shape, 1)
    gcol = jnp.sum(jnp.where(lane == e, gate_ref[...], 0.0), axis=-1, keepdims=True)
    h = a * _sigmoid(a) * u * gcol
    acc_ref[...] += _dot(h.astype(BF16), wd_ref[0])

    @pl.when(e == N_EXPERTS - 1)
    def _():
        y = _layer_norm(DN_ALPHA * x_ref[...] + acc_ref[...], g_ref[...], b_ref[...])
        y_ref[...] = y
        yb_ref[...] = y.astype(BF16)


def _experts_ln(xb, x, gate, wg, wu, wd, g, b, tm=1024):
    T, D = x.shape
    tm = min(tm, T)
    E, _, DE = wg.shape
    row = pl.BlockSpec((tm, D), lambda i, e: (i, 0))
    vec = pl.BlockSpec((1, D), lambda i, e: (0, 0))
    return pl.pallas_call(
        _experts_body,
        grid=(T // tm, E),
        in_specs=[
            row, row,
            pl.BlockSpec((tm, LANES), lambda i, e: (i, 0)),
            pl.BlockSpec((1, D, DE), lambda i, e: (e, 0, 0)),
            pl.BlockSpec((1, D, DE), lambda i, e: (e, 0, 0)),
            pl.BlockSpec((1, DE, D), lambda i, e: (e, 0, 0)),
            vec, vec,
        ],
        out_specs=[row, row],
        out_shape=[jax.ShapeDtypeStruct((T, D), F32), jax.ShapeDtypeStruct((T, D), BF16)],
        scratch_shapes=[pltpu.VMEM((tm, D), F32)],
        compiler_params=_cparams(("parallel", "arbitrary")),
    )(xb, x, gate, wg, wu, wd, g.reshape(1, D), b.reshape(1, D))


GID_LANE = N_EXPERTS
MOE_WINDOW = 1024
MOE_CHUNK = 128
MOE_VMEM_LIMIT = 60 * 1024 * 1024


def _router_sorted_body(x_ref, wh_ref, wl_ref, bias_ref, gate_ref, gidt_ref, cnt_ref):
    x_hi, x_lo = _split_bf16(x_ref[...])
    wh = wh_ref[...]
    logits = _dot(x_hi, wh) + _dot(x_lo, wh) + _dot(x_hi, wl_ref[...])
    scores = _sigmoid(logits)
    lane = lax.broadcasted_iota(jnp.int32, scores.shape, 1)
    lanef = lane.astype(F32)
    biased = jnp.where(lane < N_EXPERTS, scores + bias_ref[...], REMOVED)

    def top2(mask):
        v = jnp.where(mask, biased, REMOVED)
        m1 = jnp.max(v, axis=-1, keepdims=True)
        i1 = jnp.min(jnp.where(v == m1, lanef, float(LANES)), axis=-1, keepdims=True)
        v2 = jnp.where(lanef == i1, REMOVED, v)
        m2 = jnp.max(v2, axis=-1, keepdims=True)
        i2 = jnp.min(jnp.where(v2 == m2, lanef, float(LANES)), axis=-1, keepdims=True)
        return m1 + m2, jnp.where((lanef == i1) | (lanef == i2), 1.0, 0.0)

    best, best_sel = top2(lane // EXPERTS_PER_GROUP == 0)
    gid = jnp.zeros_like(best)
    for grp in range(1, N_GROUPS):
        score, sel = top2(lane // EXPERTS_PER_GROUP == grp)
        better = score > best
        best = jnp.where(better, score, best)
        best_sel = jnp.where(better, sel, best_sel)
        gid = jnp.where(better, float(grp), gid)
    w = best_sel * scores
    gate = w / jnp.sum(w, axis=-1, keepdims=True)
    gate_ref[...] = jnp.where(lane == GID_LANE, gid, gate)
    gid_b = jnp.broadcast_to(gid, scores.shape)
    gidt_ref[...] = gid_b.T[:8, :]
    cnt_ref[0] = jnp.broadcast_to(
        jnp.sum(jnp.where(lanef == gid_b, 1.0, 0.0), axis=0, keepdims=True), (8, LANES))


def _router_sorted(x, router_w, router_bias, tm):
    T, D = x.shape
    wpad = jnp.zeros((D, LANES), F32).at[:, :N_EXPERTS].set(router_w)
    wh, wl = _split_bf16(wpad)
    bpad = jnp.zeros((1, LANES), F32).at[0, :N_EXPERTS].set(router_bias)
    return pl.pallas_call(
        _router_sorted_body,
        grid=(T // tm,),
        in_specs=[
            pl.BlockSpec((tm, D), lambda i: (i, 0)),
            pl.BlockSpec((D, LANES), lambda i: (0, 0)),
            pl.BlockSpec((D, LANES), lambda i: (0, 0)),
            pl.BlockSpec((1, LANES), lambda i: (0, 0)),
        ],
        out_specs=[
            pl.BlockSpec((tm, LANES), lambda i: (i, 0)),
            pl.BlockSpec((8, tm), lambda i: (0, i)),
            pl.BlockSpec((1, 8, LANES), lambda i: (i, 0, 0)),
        ],
        out_shape=[
            jax.ShapeDtypeStruct((T, LANES), F32),
            jax.ShapeDtypeStruct((8, T), F32),
            jax.ShapeDtypeStruct((T // tm, 8, LANES), F32),
        ],
        compiler_params=_cparams(("parallel",)),
    )(x, wh, wl, bpad)


def _experts_sorted_body(cnt_ref, xb_ref, x_ref, gate_ref, gidt_ref, ltri_ref, utri_ref, wg_ref, wu_ref,
                         wd_ref, g_ref, b_ref, y_ref, yb_ref, xs_ref, gs_ref, acc_ref, rank_ref,
                         *, W, Wp, chunk):
    win = pl.program_id(0)
    e = pl.program_id(1)
    grp = e // EXPERTS_PER_GROUP
    padded = [((cnt_ref[win * N_GROUPS + g] + chunk - 1) // chunk) * chunk for g in range(N_GROUPS)]
    starts = [0]
    for g in range(N_GROUPS - 1):
        starts.append(starts[-1] + padded[g])
    start = starts[0]
    for g in range(1, N_GROUPS):
        start = jnp.where(grp == g, starts[g], start)
    n_chunks = (cnt_ref[win * N_GROUPS + grp] + chunk - 1) // chunk

    @pl.when(e == 0)
    def _():
        gate = gate_ref[...]
        lane = lax.broadcasted_iota(jnp.int32, gate.shape, 1)
        lanef = lane.astype(F32)
        gid = jnp.sum(jnp.where(lane == GID_LANE, gate, 0.0), axis=-1, keepdims=True)
        member = jnp.where((lanef == gid) & (lane < N_GROUPS), 1.0, 0.0)
        earlier = _dot(ltri_ref[...], member.astype(BF16))
        first = jnp.zeros(gate.shape, F32)
        for g in range(1, N_GROUPS):
            first = jnp.where(lane == g, starts[g].astype(F32), first)
        rank = jnp.sum(member * (first + earlier), axis=-1, keepdims=True)
        rank_ref[...] = jnp.broadcast_to(rank, gate.shape)

        gid_r = gidt_ref[...]
        sub = lax.broadcasted_iota(jnp.int32, gid_r.shape, 0)
        member_r = jnp.where(sub.astype(F32) == gid_r, 1.0, 0.0)
        earlier_r = _dot(member_r.astype(BF16), utri_ref[...])
        first_r = jnp.zeros(gid_r.shape, F32)
        for g in range(1, N_GROUPS):
            first_r = jnp.where(sub == g, starts[g].astype(F32), first_r)
        rank_r = jnp.sum(member_r * (first_r + earlier_r), axis=0, keepdims=True)
        rows = lax.broadcasted_iota(jnp.int32, (Wp, W), 0).astype(F32)
        perm = jnp.where(rows == rank_r, 1.0, 0.0).astype(BF16)
        xs_ref[...] = _dot(perm, xb_ref[...]).astype(BF16)
        g_hi, g_lo = _split_bf16(gate)
        gs_ref[...] = _dot(perm, g_hi) + _dot(perm, g_lo)
        acc_ref[...] = jnp.zeros(acc_ref.shape, F32)

    def one_chunk(i, carry):
        r0 = pl.multiple_of(start + i * chunk, chunk)
        xc = xs_ref[pl.ds(r0, chunk), :]
        a = _dot(xc, wg_ref[0])
        u = _dot(xc, wu_ref[0])
        gs = gs_ref[pl.ds(r0, chunk), :]
        lane = lax.broadcasted_iota(jnp.int32, gs.shape, 1)
        gcol = jnp.sum(jnp.where(lane == e, gs, 0.0), axis=-1, keepdims=True)
        h = a * _sigmoid(a) * u * gcol
        acc_ref[pl.ds(r0, chunk), :] += _dot(h.astype(BF16), wd_ref[0])
        return carry

    lax.fori_loop(0, n_chunks, one_chunk, 0)

    @pl.when(e == N_EXPERTS - 1)
    def _():
        cols = lax.broadcasted_iota(jnp.int32, (W, Wp), 1).astype(F32)
        unperm = jnp.where(cols == rank_ref[...][:, :1], 1.0, 0.0).astype(BF16)
        ffn = _dot(unperm, acc_ref[...].astype(BF16))
        y = _layer_norm(DN_ALPHA * x_ref[...] + ffn, g_ref[...], b_ref[...])
        y_ref[...] = y
        yb_ref[...] = y.astype(BF16)


def _moe_ln(xb, x, router_w, router_bias, wg, wu, wd, g, b):
    T, D = x.shape
    W = min(MOE_WINDOW, T)
    chunk = MOE_CHUNK
    Wp = W + N_GROUPS * chunk
    E, _, DE = wg.shape
    assert T % W == 0 and W % chunk == 0
    gate, gidt, cnt = _router_sorted(x, router_w, router_bias, W)
    counts = cnt[:, 0, :N_GROUPS].astype(jnp.int32).reshape(-1)
    t = np.arange(W)
    ltri = jnp.asarray((t[None, :] < t[:, None]).astype(np.float32), BF16)
    once = pl.Buffered(1)
    row = lambda shape: pl.BlockSpec(shape, lambda i, e, c: (i, 0))
    vec = pl.BlockSpec((1, D), lambda i, e, c: (0, 0))
    tri = pl.BlockSpec((W, W), lambda i, e, c: (0, 0), pipeline_mode=once)
    grid_spec = pltpu.PrefetchScalarGridSpec(
        num_scalar_prefetch=1,
        grid=(T // W, E),
        in_specs=[
            row((W, D)),
            pl.BlockSpec((W, D), lambda i, e, c: (i, 0), pipeline_mode=once),
            row((W, LANES)),
            pl.BlockSpec((8, W), lambda i, e, c: (0, i)),
            tri, tri,
            pl.BlockSpec((1, D, DE), lambda i, e, c: (e, 0, 0)),
            pl.BlockSpec((1, D, DE), lambda i, e, c: (e, 0, 0)),
            pl.BlockSpec((1, DE, D), lambda i, e, c: (e, 0, 0)),
            vec, vec,
        ],
        out_specs=[row((W, D)), row((W, D))],
        scratch_shapes=[
            pltpu.VMEM((Wp, D), BF16),
            pltpu.VMEM((Wp, LANES), F32),
            pltpu.VMEM((Wp, D), F32),
            pltpu.VMEM((W, LANES), F32),
        ],
    )
    return pl.pallas_call(
        functools.partial(_experts_sorted_body, W=W, Wp=Wp, chunk=chunk),
        grid_spec=grid_spec,
        out_shape=[jax.ShapeDtypeStruct((T, D), F32), jax.ShapeDtypeStruct((T, D), BF16)],
        compiler_params=pltpu.CompilerParams(dimension_semantics=("parallel", "arbitrary"),
                                             vmem_limit_bytes=MOE_VMEM_LIMIT),
    )(counts, xb, x, gate, gidt, ltri, ltri.T, wg, wu, wd, g.reshape(1, D), b.reshape(1, D))


def _kt_tiles(t, B, S, n, tk):
    kt = t.reshape(B, S // tk, tk, n, HEAD_DIM)
    return kt.transpose(0, 3, 1, 4, 2)


def _block_onehots(S, block, tk):
    key = np.arange(S).reshape(S // tk, 1, tk)
    r = np.arange(LANES).reshape(1, LANES, 1)
    return jnp.asarray((key // block == r).astype(np.float32), BF16)


def _rope_tiled(S):
    cos, sin = _rope_tables(jnp.arange(S))
    reps = LANES // HALF
    return jnp.tile(cos, (1, reps)), jnp.tile(sin, (1, reps))


def _nsa_mixer(xb, B, S, w_in, cmp_k_w1, cmp_k_w2, cmp_v_w1, cmp_v_w2, cmp_k_pos, cmp_v_pos):
    G, HPG, KV = NSA_KV_GROUPS, NSA_HPG, NSA_KV_DIM
    L, STR, SB = NSA_CMP_LEN, NSA_CMP_STRIDE, NSA_SEL_BLOCK
    assert L == 2 * STR and S % SB == 0 and S // SB <= LANES
    T = B * S
    n_cmp = (S - L) // STR + 1
    NC = S // STR
    n_sel = S // SB
    top_n = min(NSA_SEL_TOPN, n_sel)

    cos2, sin2 = _rope_tiled(S)
    wb = w_in.astype(BF16)
    wcol = lambda i: wb[:, D_MODEL + i * KV: D_MODEL + (i + 1) * KV]
    tn = 2 * KV
    w_main = jnp.concatenate([wb[:, :D_MODEL], wcol(2), wcol(4), wcol(0), wcol(1)], axis=1)
    proj = _proj(xb, w_main, cos2, sin2, [2] * (D_MODEL // tn) + [1, 0], S, tn=tn)
    vaug = _proj_vaug(xb, jnp.concatenate([wcol(3), wcol(5)], axis=1), tn=tn)
    wg = jnp.zeros((D_MODEL, LANES), BF16).at[:, :3 * N_HEADS].set(wb[:, D_MODEL + 6 * KV:])
    gates = _proj(xb, wg, cos2, sin2, [0], S, out_dtype=F32, tn=LANES)[:, :3 * N_HEADS]

    col = lambda i: proj[:, D_MODEL + i * KV: D_MODEL + (i + 1) * KV]
    proj3 = proj.reshape(B, S, proj.shape[1])
    vaug3 = vaug.reshape(B, S, vaug.shape[1])
    gcols = gates.reshape(B, S, 3, G, HPG).transpose(2, 0, 3, 4, 1)[..., None]

    ccos, csin = _rope_tables(jnp.arange(NC) * STR + (L - 1))
    ccos = jnp.concatenate([ccos, ccos], axis=1)
    csin = jnp.concatenate([csin, csin], axis=1)
    to_rows = lambda t: t.reshape(B, S, G, HEAD_DIM).transpose(0, 2, 1, 3).reshape(B * G, NC, STR * HEAD_DIM)
    kc = _compress(to_rows(col(2)), cmp_k_w1, cmp_k_pos, cmp_k_w2, ccos, csin, True, n_cmp)
    vc = _compress(to_rows(col(3)), cmp_v_w1, cmp_v_pos, cmp_v_w2, ccos, csin, False, n_cmp)
    kct = kc.reshape(B, G, NC, HEAD_DIM).transpose(0, 1, 3, 2)

    ci = np.arange(NC)[:, None]
    sj = np.arange(LANES)[None, :]
    overlap = ((ci * STR < (sj + 1) * SB) & (ci * STR + L > sj * SB) & (ci < n_cmp) & (sj < n_sel))
    overlap = jnp.broadcast_to(jnp.asarray(overlap.astype(np.float32), BF16), (B, G, NC, LANES))
    vo = jnp.concatenate([vc.reshape(B, G, NC, HEAD_DIM), jnp.ones((B, G, NC, 1), BF16),
                          jnp.zeros((B, G, NC, LANES - HEAD_DIM - 1), BF16), overlap], axis=-1)

    o_cmp, selb = _nsa_cmp(proj3, kct, vo, gcols[0], n_cmp, n_sel, top_n)
    tk_s = min(512, S)
    o_slc = _nsa_slc(proj3, _kt_tiles(col(0), B, S, G, tk_s), vaug3, 0,
                     _block_onehots(S, SB, tk_s), selb, gcols[1], tq=256, tk=tk_s)
    tq_w = NSA_WINDOW // 2
    o_win = _nsa_win(proj3, _kt_tiles(col(1), B, S, G, tq_w), vaug3, G, gcols[2], tq=tq_w)
    return [o.reshape(T, D_MODEL) for o in (o_cmp, o_slc, o_win)]


def _moba_mixer(xb, B, S, w_in):
    H = N_HEADS
    nb = S // MOBA_BLOCK
    top_k = min(MOBA_TOPK, nb)
    cos2, sin2 = _rope_tiled(S)
    tn = 512
    n_t = D_MODEL // tn
    wb = w_in.astype(BF16)
    proj = _proj(xb, wb[:, :2 * D_MODEL], cos2, sin2, [2] * n_t + [1] * n_t, S, tn=tn)
    vaug = _proj_vaug(xb, wb[:, 2 * D_MODEL:], tn=tn)
    tk = min(2 * MOBA_BLOCK, S)
    kt = _kt_tiles(proj[:, D_MODEL:], B, S, H, tk)
    e = _block_onehots(S, MOBA_BLOCK, tk)
    kh, kl = _moba_kmean(kt, e.transpose(0, 2, 1))
    o = _moba_attn(proj.reshape(B, S, 2 * D_MODEL), kt, vaug.reshape(B, S, H * LANES), e, kh, kl, nb, top_k,
                   tq=MOBA_BLOCK, tk=tk)
    return [o.reshape(B * S, D_MODEL)]


def kernel(x, nsa_w_in, nsa_w_out, nsa_cmp_k_w1, nsa_cmp_k_w2, nsa_cmp_v_w1, nsa_cmp_v_w2, nsa_cmp_k_pos, nsa_cmp_v_pos, moba_w_in, moba_w_out, router_w, router_bias, moe_w_gate, moe_w_up, moe_w_down, ln_g, ln_b):
    B, S, D = x.shape
    xf = x.reshape(B * S, D)
    xb = xf.astype(BF16)
    for layer in range(DEPTH):
        j = layer // 2
        if layer % 2 == 0:
            os_ = _nsa_mixer(xb, B, S, nsa_w_in[j], nsa_cmp_k_w1[j], nsa_cmp_k_w2[j], nsa_cmp_v_w1[j],
                             nsa_cmp_v_w2[j], nsa_cmp_k_pos[j], nsa_cmp_v_pos[j])
            w_out = nsa_w_out[j]
        else:
            os_ = _moba_mixer(xb, B, S, moba_w_in[j])
            w_out = moba_w_out[j]
        xf, xb = _outproj_ln(os_, w_out.astype(BF16), xf, ln_g[layer, 0], ln_b[layer, 0])
        xf, xb = _moe_ln(xb, xf, router_w, router_bias, moe_w_gate[layer].astype(BF16),
                         moe_w_up[layer].astype(BF16), moe_w_down[layer].astype(BF16),
                         ln_g[layer, 1], ln_b[layer, 1])
    return xf.reshape(B, S, D)
```

```python
import functools

import jax
import jax.numpy as jnp
import numpy as np
from jax import lax
from jax.experimental import pallas as pl
from jax.experimental.pallas import tpu as pltpu

F32 = jnp.float32
BF16 = jnp.bfloat16

D_MODEL = 1024
N_HEADS = 16
HEAD_DIM = 64
HALF = HEAD_DIM // 2
ROPE_THETA = 10000.0
DEPTH = 2
DN_ALPHA = (2 * DEPTH) ** 0.25
LN_EPS = 1e-5
NEG = -1e30
FORCE = 1e9
MASK = -1e30
REMOVED = -3.0e38
LANES = 128
Q_SCALE_LOG2 = float(HEAD_DIM ** -0.5 * np.log2(np.e))
ROW_CHUNK = 128

NSA_KV_GROUPS = 4
NSA_HPG = N_HEADS // NSA_KV_GROUPS
NSA_KV_DIM = NSA_KV_GROUPS * HEAD_DIM
NSA_CMP_LEN = 32
NSA_CMP_STRIDE = 16
NSA_SEL_BLOCK = 64
NSA_SEL_TOPN = 16
NSA_WINDOW = 512

MOBA_BLOCK = 256
MOBA_TOPK = 3

N_EXPERTS = 16
N_GROUPS = 4
EXPERTS_PER_GROUP = N_EXPERTS // N_GROUPS
D_EXPERT = 512

VMEM_LIMIT = 48 * 1024 * 1024


def _cparams(sem):
    return pltpu.CompilerParams(dimension_semantics=sem, vmem_limit_bytes=VMEM_LIMIT)


def _dot(a, b):
    return jnp.dot(a, b, preferred_element_type=F32)


def _split_bf16(x):
    hi = x.astype(BF16)
    lo = (x - hi.astype(F32)).astype(BF16)
    return hi, lo


def _sigmoid(x):
    return 1.0 / (1.0 + jnp.exp(-x))


def _proj_body(mode_ref, x_ref, w_ref, cos_ref, sin_ref, o_ref, *, tn):
    j = pl.program_id(0)
    mode = mode_ref[j]
    acc = _dot(x_ref[...], w_ref[...])

    @pl.when(mode == 0)
    def _():
        o_ref[...] = acc.astype(o_ref.dtype)

    @pl.when(mode != 0)
    def _():
        sc = jnp.where(mode == 2, Q_SCALE_LOG2, 1.0).astype(F32)
        cos = cos_ref[...] * sc
        sin = sin_ref[...] * sc
        lane = lax.broadcasted_iota(jnp.int32, cos.shape, 1)
        lower = (lane % HEAD_DIM) < HALF
        for c in range(tn // LANES):
            a = acc[:, c * LANES:(c + 1) * LANES]
            up = pltpu.roll(a, LANES - HALF, 1)
            dn = pltpu.roll(a, HALF, 1)
            partner = jnp.where(lower, -up, dn)
            o_ref[:, c * LANES:(c + 1) * LANES] = (a * cos + partner * sin).astype(o_ref.dtype)


def _proj(xb, w, cos2, sin2, modes, seq, out_dtype=BF16, tm=1024, tn=512):
    T, K = xb.shape
    N = w.shape[1]
    tm = min(tm, seq)
    assert T % tm == 0 and N % tn == 0 and seq % tm == 0 and len(modes) == N // tn
    n_pos = seq // tm
    grid_spec = pltpu.PrefetchScalarGridSpec(
        num_scalar_prefetch=1,
        grid=(N // tn, T // tm),
        in_specs=[
            pl.BlockSpec((tm, K), lambda j, i, m: (i, 0)),
            pl.BlockSpec((K, tn), lambda j, i, m: (0, j)),
            pl.BlockSpec((tm, LANES), lambda j, i, m: (i % n_pos, 0)),
            pl.BlockSpec((tm, LANES), lambda j, i, m: (i % n_pos, 0)),
        ],
        out_specs=pl.BlockSpec((tm, tn), lambda j, i, m: (i, j)),
    )
    return pl.pallas_call(
        functools.partial(_proj_body, tn=tn),
        grid_spec=grid_spec,
        out_shape=jax.ShapeDtypeStruct((T, N), out_dtype),
        compiler_params=_cparams(("parallel", "parallel")),
    )(jnp.asarray(modes, jnp.int32), xb, w, cos2, sin2)


def _rope_tables(pos):
    inv = 1.0 / (ROPE_THETA ** (jnp.arange(0, HEAD_DIM, 2, dtype=F32) / HEAD_DIM))
    ang = pos.astype(F32)[:, None] * inv[None, :]
    return jnp.cos(ang), jnp.sin(ang)


def _proj_vaug_body(x_ref, w_ref, o_ref, *, tn):
    acc = _dot(x_ref[...], w_ref[...])
    lane = lax.broadcasted_iota(jnp.int32, (acc.shape[0], LANES), 1)
    tail = jnp.where(lane == HEAD_DIM, 1.0, 0.0)
    for c in range(tn // LANES):
        a = acc[:, c * LANES:(c + 1) * LANES]
        for k, head in enumerate((a, pltpu.roll(a, HEAD_DIM, 1))):
            o_ref[:, (2 * c + k) * LANES:(2 * c + k + 1) * LANES] = (
                jnp.where(lane < HEAD_DIM, head, tail).astype(o_ref.dtype))


def _proj_vaug(xb, w, tm=1024, tn=512):
    T, K = xb.shape
    N = w.shape[1]
    tm = min(tm, T)
    assert T % tm == 0 and N % tn == 0
    return pl.pallas_call(
        functools.partial(_proj_vaug_body, tn=tn),
        grid=(N // tn, T // tm),
        in_specs=[pl.BlockSpec((tm, K), lambda j, i: (i, 0)), pl.BlockSpec((K, tn), lambda j, i: (0, j))],
        out_specs=pl.BlockSpec((tm, 2 * tn), lambda j, i: (i, j)),
        out_shape=jax.ShapeDtypeStruct((T, 2 * N), BF16),
        compiler_params=_cparams(("parallel", "parallel")),
    )(xb, w)


def _gelu_tanh(x):
    c = np.float32(np.sqrt(2.0 / np.pi))
    return 0.5 * x * (1.0 + jnp.tanh(c * (x + 0.044715 * (x * x * x))))


def _compress_body(r_ref, w1_ref, pos_ref, w2_ref, w2r_ref, cos_ref, sin_ref, o_ref, *, rope, n_cmp):
    r = r_ref[0]
    nc = r.shape[0]
    half = NSA_CMP_STRIDE * HEAD_DIM
    a = _dot(r, w1_ref[0])
    b = _dot(r, w1_ref[1])
    pos = pos_ref[...]
    pb = _dot(pos[:, :half], w1_ref[0]) + _dot(pos[:, half:], w1_ref[1])
    b_next = pltpu.roll(b, nc - 1, 0)
    h = _gelu_tanh(a + b_next + pb[0:1, :]).astype(BF16)
    o = _dot(h, w2_ref[...])
    if rope:
        o = o * cos_ref[...] + _dot(h, w2r_ref[...]) * sin_ref[...]
    row = lax.broadcasted_iota(jnp.int32, o.shape, 0)
    o_ref[0] = jnp.where(row < n_cmp, o, 0.0).astype(o_ref.dtype)


def _compress(r, w1, pos, w2, cos_c, sin_c, rope, n_cmp):
    BG, NC, K = r.shape
    hidden = w1.shape[1]
    w1s = w1.astype(BF16).reshape(2, K, hidden)
    pos8 = jnp.zeros((8, 2 * K), BF16).at[0].set(pos.reshape(-1).astype(BF16))
    w2r = jnp.concatenate([-w2[:, HALF:], w2[:, :HALF]], axis=1).astype(BF16)
    full = lambda shape: pl.BlockSpec(shape, lambda i: (0,) * len(shape))
    return pl.pallas_call(
        functools.partial(_compress_body, rope=rope, n_cmp=n_cmp),
        grid=(BG,),
        in_specs=[
            pl.BlockSpec((1, NC, K), lambda i: (i, 0, 0)),
            full((2, K, hidden)),
            full((8, 2 * K)),
            full((hidden, HEAD_DIM)),
            full((hidden, HEAD_DIM)),
            full((NC, HEAD_DIM)),
            full((NC, HEAD_DIM)),
        ],
        out_specs=pl.BlockSpec((1, NC, HEAD_DIM), lambda i: (i, 0, 0)),
        out_shape=jax.ShapeDtypeStruct((BG, NC, HEAD_DIM), BF16),
        compiler_params=_cparams(("parallel",)),
    )(r, w1s, pos8, w2.astype(BF16), w2r, cos_c, sin_c)


def _topk_mask_t(v, k):
    idx = lax.broadcasted_iota(jnp.int32, v.shape, 0).astype(F32)

    def step(_, cur):
        m = jnp.max(cur, axis=0, keepdims=True)
        first = jnp.min(jnp.where(cur == m, idx, float(LANES)), axis=0, keepdims=True)
        return jnp.where(idx == first, REMOVED, cur)

    return lax.fori_loop(0, k, step, v, unroll=True) != v


def _head_kt(e, kt, parity):
    z = jnp.zeros_like(kt)
    parts = ([] if e is None else [e]) + ([kt, z] if parity == 0 else [z, kt])
    return jnp.concatenate(parts, axis=0)


def _flash_init(m_ref, acc_ref):
    m_ref[...] = jnp.full(m_ref.shape, -jnp.inf, F32)
    acc_ref[...] = jnp.zeros(acc_ref.shape, F32)


def _flash_step(jobs, s_ref, m_ref, alpha_ref, acc_ref):
    for r0, q, kt, _, mask in jobs:
        rows = q.shape[0]
        s = _dot(q, kt)
        if mask is not None:
            s = jnp.where(mask, s, MASK)
        s_ref[r0:r0 + rows, :] = s
        cm = s[:, :LANES]
        for c in range(1, s.shape[1] // LANES):
            cm = jnp.maximum(cm, s[:, c * LANES:(c + 1) * LANES])
        m_prev = m_ref[r0:r0 + rows, :]
        m_new = jnp.maximum(m_prev, jnp.max(cm, axis=-1, keepdims=True))
        alpha_ref[r0:r0 + rows, :] = jnp.exp2(m_prev - m_new)
        m_ref[r0:r0 + rows, :] = m_new
    for r0, q, _, v, _ in jobs:
        for r in range(r0, r0 + q.shape[0], ROW_CHUNK):
            rows = min(ROW_CHUNK, r0 + q.shape[0] - r)
            m = m_ref[r:r + rows, :]
            p = jnp.concatenate(
                [jnp.exp2(s_ref[r:r + rows, c * LANES:(c + 1) * LANES] - m)
                 for c in range(s_ref.shape[1] // LANES)], axis=1).astype(BF16)
            acc_ref[r:r + rows, :] = alpha_ref[r:r + rows, :] * acc_ref[r:r + rows, :] + _dot(p, v)


def _flash_scratch(rows, tk):
    return [pltpu.VMEM((rows, tk), F32), pltpu.VMEM((rows, LANES), F32),
            pltpu.VMEM((rows, LANES), F32), pltpu.VMEM((rows, LANES), F32)]


def _pair_merge(even, odd):
    lane = lax.broadcasted_iota(jnp.int32, even.shape, 1)
    return jnp.where(lane < HEAD_DIM, even, pltpu.roll(odd, HEAD_DIM, 1))


def _flash_pairs_out(acc_ref, tq, n_heads, gates):
    def head(h):
        acc = acc_ref[h * tq:(h + 1) * tq, :]
        return acc * (gates[h] / acc[:, HEAD_DIM:HEAD_DIM + 1])
    return jnp.concatenate([_pair_merge(head(h), head(h + 1)) for h in range(0, n_heads, 2)], axis=1)


def _tile_iotas(rows, tk):
    return (lax.broadcasted_iota(jnp.int32, (rows, tk), 0),
            lax.broadcasted_iota(jnp.int32, (rows, tk), 1))


def _nsa_cmp_body(q_ref, kct_ref, vo_ref, g_ref, o_ref, selb_ref, s_ref, m_ref, *, tq, n_cmp, n_sel, top_n):
    s0 = pl.program_id(2) * tq
    q4 = q_ref[0]
    kct = kct_ref[0, 0]
    vo = vo_ref[0, 0]
    nc = vo.shape[0]
    tpos = s0 + lax.broadcasted_iota(jnp.int32, (tq, nc), 0)
    nidx = lax.broadcasted_iota(jnp.int32, (tq, nc), 1)
    cmask = (nidx * NSA_CMP_STRIDE + (NSA_CMP_LEN - 1) <= tpos) & (nidx < n_cmp)
    for h in range(NSA_HPG):
        pair = q4[:, (h // 2) * LANES:(h // 2 + 1) * LANES]
        s = jnp.where(cmask, _dot(pair, _head_kt(None, kct, h % 2)), NEG)
        s_ref[h * tq:(h + 1) * tq, :] = s
        cm = s[:, :LANES]
        for c in range(1, nc // LANES):
            cm = jnp.maximum(cm, s[:, c * LANES:(c + 1) * LANES])
        m_ref[h * tq:(h + 1) * tq, :] = jnp.broadcast_to(jnp.max(cm, axis=-1, keepdims=True), (tq, LANES))

    rc = min(ROW_CHUNK, tq)
    imps = []
    for r0 in range(0, tq, rc):
        imp = jnp.zeros((rc, LANES), F32)
        heads = []
        for h in range(NSA_HPG):
            r = h * tq + r0
            m = m_ref[r:r + rc, :]
            e = jnp.concatenate(
                [jnp.exp2(s_ref[r:r + rc, c * LANES:(c + 1) * LANES] - m) for c in range(nc // LANES)],
                axis=1).astype(BF16)
            res = _dot(e, vo)
            inv = jnp.where(m[:, :1] > 0.5 * NEG, 1.0 / res[:, HEAD_DIM:HEAD_DIM + 1], 0.0)
            heads.append(res[:, :LANES] * (inv * _sigmoid(g_ref[0, 0, h, r0:r0 + rc, :])))
            imp = imp + res[:, LANES:] * inv
        o_ref[0, r0:r0 + rc, :] = jnp.concatenate(
            [_pair_merge(heads[h], heads[h + 1]) for h in range(0, NSA_HPG, 2)], axis=1).astype(o_ref.dtype)
        imps.append(imp)
    imp = jnp.concatenate(imps, axis=0)

    blk = lax.broadcasted_iota(jnp.int32, imp.shape, 1)
    jq = (s0 + lax.broadcasted_iota(jnp.int32, imp.shape, 0)) // NSA_SEL_BLOCK
    forced = (blk == 0) | (blk == jq) | (blk == jq - 1)
    imp = jnp.where(blk > jq, NEG, jnp.where(forced, FORCE, imp))
    imp = jnp.where(blk < n_sel, imp, REMOVED)
    sel_t = _topk_mask_t(imp.T, top_n)
    selb_ref[0, 0] = jnp.where(sel_t, 0.0, MASK).T.astype(selb_ref.dtype)


def _nsa_cmp(proj3, kct, vo, g_cmp, n_cmp, n_sel, top_n, tq=256):
    B, S, _ = proj3.shape
    G, HPG = NSA_KV_GROUPS, NSA_HPG
    NC = vo.shape[2]
    tq = min(tq, S)
    assert NC % LANES == 0
    return pl.pallas_call(
        functools.partial(_nsa_cmp_body, tq=tq, n_cmp=n_cmp, n_sel=n_sel, top_n=top_n),
        grid=(B, G, S // tq),
        in_specs=[
            pl.BlockSpec((1, tq, HPG * HEAD_DIM), lambda b, g, i: (b, i, g)),
            pl.BlockSpec((1, 1, HEAD_DIM, NC), lambda b, g, i: (b, g, 0, 0)),
            pl.BlockSpec((1, 1, NC, 2 * LANES), lambda b, g, i: (b, g, 0, 0)),
            pl.BlockSpec((1, 1, HPG, tq, 1), lambda b, g, i: (b, g, 0, i, 0)),
        ],
        out_specs=[
            pl.BlockSpec((1, tq, HPG * HEAD_DIM), lambda b, g, i: (b, i, g)),
            pl.BlockSpec((1, 1, tq, LANES), lambda b, g, i: (b, g, i, 0)),
        ],
        out_shape=[
            jax.ShapeDtypeStruct((B, S, D_MODEL), BF16),
            jax.ShapeDtypeStruct((B, G, S, LANES), BF16),
        ],
        scratch_shapes=[pltpu.VMEM((HPG * tq, NC), F32), pltpu.VMEM((HPG * tq, LANES), F32)],
        compiler_params=_cparams(("parallel", "parallel", "parallel")),
    )(proj3, kct, vo, g_cmp)


def _nsa_slc_body(q_ref, kt_ref, v_ref, e_ref, selb_ref, g_ref, o_ref, qa_ref, s_ref, m_ref, alpha_ref,
                  acc_ref, *, tq, tk):
    s0 = pl.program_id(2) * tq
    _flash_init(m_ref, acc_ref)
    selb = selb_ref[0, 0]
    q4 = q_ref[0]
    for h in range(NSA_HPG):
        qa_ref[h * tq:(h + 1) * tq, :] = jnp.concatenate(
            [selb, q4[:, (h // 2) * LANES:(h // 2 + 1) * LANES]], axis=1)

    def tile(ki, causal):
        k0 = pl.multiple_of(ki * tk, tk)
        kts = [_head_kt(e_ref[ki], kt_ref[0, 0, ki], parity) for parity in range(2)]
        v = v_ref[0, pl.ds(k0, tk), :]
        row, col = _tile_iotas(tq, tk)
        mask = (k0 + col <= s0 + row) if causal else None
        jobs = [(h * tq, qa_ref[h * tq:(h + 1) * tq, :], kts[h % 2], v, mask) for h in range(NSA_HPG)]
        _flash_step(jobs, s_ref, m_ref, alpha_ref, acc_ref)

    last = (s0 + tq - 1) // tk

    def full_tile(ki, carry):
        tile(ki, False)
        return carry

    lax.fori_loop(0, last, full_tile, 0)
    tile(last, True)
    gates = [_sigmoid(g_ref[0, 0, h]) for h in range(NSA_HPG)]
    o_ref[0] = _flash_pairs_out(acc_ref, tq, NSA_HPG, gates).astype(o_ref.dtype)


def _nsa_slc(proj3, kt, vaug3, v_block, e, selb, g, tq, tk):
    B, S, _ = proj3.shape
    G, HPG = NSA_KV_GROUPS, NSA_HPG
    nk = S // tk
    assert tk % tq == 0 and kt.shape == (B, G, nk, HEAD_DIM, tk) and e.shape == (nk, LANES, tk)
    once = pl.Buffered(1)
    return pl.pallas_call(
        functools.partial(_nsa_slc_body, tq=tq, tk=tk),
        grid=(B, G, S // tq),
        in_specs=[
            pl.BlockSpec((1, tq, HPG * HEAD_DIM), lambda b, g_, i: (b, i, g_)),
            pl.BlockSpec((1, 1, nk, HEAD_DIM, tk), lambda b, g_, i: (b, g_, 0, 0, 0), pipeline_mode=once),
            pl.BlockSpec((1, S, LANES), lambda b, g_, i: (b, 0, v_block + g_), pipeline_mode=once),
            pl.BlockSpec((nk, LANES, tk), lambda b, g_, i: (0, 0, 0), pipeline_mode=once),
            pl.BlockSpec((1, 1, tq, LANES), lambda b, g_, i: (b, g_, i, 0)),
            pl.BlockSpec((1, 1, HPG, tq, 1), lambda b, g_, i: (b, g_, 0, i, 0)),
        ],
        out_specs=pl.BlockSpec((1, tq, HPG * HEAD_DIM), lambda b, g_, i: (b, i, g_)),
        out_shape=jax.ShapeDtypeStruct((B, S, D_MODEL), BF16),
        scratch_shapes=[pltpu.VMEM((HPG * tq, 2 * LANES), BF16)] + _flash_scratch(HPG * tq, tk),
        compiler_params=_cparams(("parallel", "parallel", "parallel")),
    )(proj3, kt, vaug3, e, selb, g)


def _nsa_win_body(q_ref, kt_ref, v_ref, g_ref, o_ref, s_ref, m_ref, *, tq, n_tiles):
    s0 = pl.program_id(2) * tq
    tk = n_tiles * tq
    k_first = jnp.maximum(pl.program_id(2) - (n_tiles - 1), 0)
    k0 = pl.multiple_of(k_first * tq, tq)
    kt = jnp.concatenate([kt_ref[0, 0, k_first + j] for j in range(n_tiles)], axis=1)
    v = v_ref[0, pl.ds(k0, tk), :]
    row, col = _tile_iotas(tq, tk)
    dist = (s0 - k0) + row - col
    valid = (dist >= 0) & (dist < NSA_WINDOW)
    q4 = q_ref[0]
    for h in range(NSA_HPG):
        pair = q4[:, (h // 2) * LANES:(h // 2 + 1) * LANES]
        s = jnp.where(valid, _dot(pair, _head_kt(None, kt, h % 2)), MASK)
        s_ref[h * tq:(h + 1) * tq, :] = s
        cm = s[:, :LANES]
        for c in range(1, tk // LANES):
            cm = jnp.maximum(cm, s[:, c * LANES:(c + 1) * LANES])
        m_ref[h * tq:(h + 1) * tq, :] = jnp.broadcast_to(jnp.max(cm, axis=-1, keepdims=True), (tq, LANES))

    rc = min(ROW_CHUNK, tq)
    for r0 in range(0, tq, rc):
        heads = []
        for h in range(NSA_HPG):
            r = h * tq + r0
            m = m_ref[r:r + rc, :]
            p = jnp.concatenate(
                [jnp.exp2(s_ref[r:r + rc, c * LANES:(c + 1) * LANES] - m) for c in range(tk // LANES)],
                axis=1).astype(BF16)
            res = _dot(p, v)
            heads.append(res * (_sigmoid(g_ref[0, 0, h, r0:r0 + rc, :]) / res[:, HEAD_DIM:HEAD_DIM + 1]))
        o_ref[0, r0:r0 + rc, :] = jnp.concatenate(
            [_pair_merge(heads[h], heads[h + 1]) for h in range(0, NSA_HPG, 2)], axis=1).astype(o_ref.dtype)


def _nsa_win(proj3, kt, vaug3, v_block, g, tq):
    B, S, _ = proj3.shape
    G, HPG = NSA_KV_GROUPS, NSA_HPG
    nk = S // tq
    n_tiles = NSA_WINDOW // tq + 1
    assert NSA_WINDOW % tq == 0 and nk >= n_tiles and kt.shape == (B, G, nk, HEAD_DIM, tq)
    once = pl.Buffered(1)
    return pl.pallas_call(
        functools.partial(_nsa_win_body, tq=tq, n_tiles=n_tiles),
        grid=(B, G, S // tq),
        in_specs=[
            pl.BlockSpec((1, tq, HPG * HEAD_DIM), lambda b, g_, i: (b, i, g_)),
            pl.BlockSpec((1, 1, nk, HEAD_DIM, tq), lambda b, g_, i: (b, g_, 0, 0, 0), pipeline_mode=once),
            pl.BlockSpec((1, S, LANES), lambda b, g_, i: (b, 0, v_block + g_), pipeline_mode=once),
            pl.BlockSpec((1, 1, HPG, tq, 1), lambda b, g_, i: (b, g_, 0, i, 0)),
        ],
        out_specs=pl.BlockSpec((1, tq, HPG * HEAD_DIM), lambda b, g_, i: (b, i, g_)),
        out_shape=jax.ShapeDtypeStruct((B, S, D_MODEL), BF16),
        scratch_shapes=[pltpu.VMEM((HPG * tq, n_tiles * tq), F32), pltpu.VMEM((HPG * tq, LANES), F32)],
        compiler_params=_cparams(("parallel", "parallel", "parallel")),
    )(proj3, kt, vaug3, g)


def _moba_kmean_body(kt_ref, et_ref, hi_ref, lo_ref):
    nk = kt_ref.shape[2]
    km = jnp.zeros((HEAD_DIM, LANES), F32)
    for ki in range(nk):
        km = km + _dot(kt_ref[0, 0, ki], et_ref[ki])
    hi, lo = _split_bf16(km * (1.0 / MOBA_BLOCK))
    hi_ref[0, 0] = hi
    lo_ref[0, 0] = lo


def _moba_kmean(kt, et):
    B, H, nk, _, tk = kt.shape
    spec = pl.BlockSpec((1, 1, HEAD_DIM, LANES), lambda b, h: (b, h, 0, 0))
    return pl.pallas_call(
        _moba_kmean_body,
        grid=(B, H),
        in_specs=[
            pl.BlockSpec((1, 1, nk, HEAD_DIM, tk), lambda b, h: (b, h, 0, 0, 0)),
            pl.BlockSpec((nk, tk, LANES), lambda b, h: (0, 0, 0)),
        ],
        out_specs=[spec, spec],
        out_shape=[jax.ShapeDtypeStruct((B, H, HEAD_DIM, LANES), BF16)] * 2,
        compiler_params=_cparams(("parallel", "parallel")),
    )(kt, et)


def _moba_body(q_ref, kt_ref, v_ref, e_ref, kh_ref, kl_ref, o_ref, qa_ref, s_ref, m_ref, alpha_ref,
               acc_ref, *, tq, tk, hb, nb, top_k):
    s0 = pl.program_id(2) * tq
    _flash_init(m_ref, acc_ref)

    blk = lax.broadcasted_iota(jnp.int32, (LANES, tq), 0)
    cb = (s0 + lax.broadcasted_iota(jnp.int32, (LANES, tq), 1)) // MOBA_BLOCK
    qs = q_ref[0]
    for h in range(hb):
        pair = qs[:, (h // 2) * LANES:(h // 2 + 1) * LANES]
        gsc = (_dot(pair, _head_kt(None, kh_ref[0, h], h % 2))
               + _dot(pair, _head_kt(None, kl_ref[0, h], h % 2))).T
        gsc = jnp.where(blk < cb, gsc, NEG)
        gsc = jnp.where(blk < nb, gsc, REMOVED)
        sel = (_topk_mask_t(gsc, top_k) & (blk < cb)) | (blk == cb)
        qa_ref[h * tq:(h + 1) * tq, :] = jnp.concatenate(
            [jnp.where(sel, 0.0, MASK).T.astype(BF16), pair], axis=1)

    def tile(ki, causal):
        k0 = pl.multiple_of(ki * tk, tk)
        e = e_ref[ki]
        row, col = _tile_iotas(tq, tk)
        mask = (k0 + col <= s0 + row) if causal else None
        jobs = [(h * tq, qa_ref[h * tq:(h + 1) * tq, :], _head_kt(e, kt_ref[0, h, ki], h % 2),
                 v_ref[0, pl.ds(k0, tk), h * LANES:(h + 1) * LANES], mask) for h in range(hb)]
        _flash_step(jobs, s_ref, m_ref, alpha_ref, acc_ref)

    last = (s0 + tq - 1) // tk

    def full_tile(ki, carry):
        tile(ki, False)
        return carry

    lax.fori_loop(0, last, full_tile, 0)
    tile(last, True)
    o_ref[0] = _flash_pairs_out(acc_ref, tq, hb, [1.0] * hb).astype(o_ref.dtype)


def _moba_attn(proj3, kt, vaug3, e, kh, kl, nb, top_k, tq, tk, hb=4):
    B, S, _ = proj3.shape
    H = N_HEADS
    nk = S // tk
    assert tk % tq == 0 and tk % MOBA_BLOCK == 0 and H % hb == 0 and hb % 2 == 0 and nb <= LANES
    once = pl.Buffered(1)
    return pl.pallas_call(
        functools.partial(_moba_body, tq=tq, tk=tk, hb=hb, nb=nb, top_k=top_k),
        grid=(B, H // hb, S // tq),
        in_specs=[
            pl.BlockSpec((1, tq, hb * HEAD_DIM), lambda b, h, i: (b, i, h)),
            pl.BlockSpec((1, hb, nk, HEAD_DIM, tk), lambda b, h, i: (b, h, 0, 0, 0), pipeline_mode=once),
            pl.BlockSpec((1, S, hb * LANES), lambda b, h, i: (b, 0, h), pipeline_mode=once),
            pl.BlockSpec((nk, LANES, tk), lambda b, h, i: (0, 0, 0), pipeline_mode=once),
            pl.BlockSpec((1, hb, HEAD_DIM, LANES), lambda b, h, i: (b, h, 0, 0), pipeline_mode=once),
            pl.BlockSpec((1, hb, HEAD_DIM, LANES), lambda b, h, i: (b, h, 0, 0), pipeline_mode=once),
        ],
        out_specs=pl.BlockSpec((1, tq, hb * HEAD_DIM), lambda b, h, i: (b, i, h)),
        out_shape=jax.ShapeDtypeStruct((B, S, D_MODEL), BF16),
        scratch_shapes=[pltpu.VMEM((hb * tq, 2 * LANES), BF16)] + _flash_scratch(hb * tq, tk),
        compiler_params=_cparams(("parallel", "parallel", "parallel")),
    )(proj3, kt, vaug3, e, kh, kl)


def _layer_norm(r, g, b):
    mu = jnp.mean(r, axis=-1, keepdims=True)
    c = r - mu
    var = jnp.mean(c * c, axis=-1, keepdims=True)
    return c * lax.rsqrt(var + LN_EPS) * g + b


def _outproj_body(*refs, n_o):
    o_refs = refs[:n_o]
    w_ref, x_ref, g_ref, b_ref, y_ref, yb_ref = refs[n_o:]
    o = o_refs[0][...].astype(F32)
    for r in o_refs[1:]:
        o = o + r[...].astype(F32)
    mix = _dot(o.astype(BF16), w_ref[...])
    y = _layer_norm(DN_ALPHA * x_ref[...] + mix, g_ref[...], b_ref[...])
    y_ref[...] = y
    yb_ref[...] = y.astype(BF16)


def _outproj_ln(os_, w, x, g, b, tm=512):
    T, D = x.shape
    tm = min(tm, T)
    n_o = len(os_)
    row = pl.BlockSpec((tm, D), lambda i: (i, 0))
    vec = pl.BlockSpec((1, D), lambda i: (0, 0))
    return pl.pallas_call(
        functools.partial(_outproj_body, n_o=n_o),
        grid=(T // tm,),
        in_specs=[row] * n_o + [pl.BlockSpec((D, D), lambda i: (0, 0)), row, vec, vec],
        out_specs=[row, row],
        out_shape=[jax.ShapeDtypeStruct((T, D), F32), jax.ShapeDtypeStruct((T, D), BF16)],
        compiler_params=_cparams(("parallel",)),
    )(*os_, w, x, g.reshape(1, D), b.reshape(1, D))


def _router_body(x_ref, wh_ref, wl_ref, bias_ref, gate_ref):
    x_hi, x_lo = _split_bf16(x_ref[...])
    wh = wh_ref[...]
    logits = _dot(x_hi, wh) + _dot(x_lo, wh) + _dot(x_hi, wl_ref[...])
    scores = _sigmoid(logits)
    lane = lax.broadcasted_iota(jnp.int32, scores.shape, 1)
    lanef = lane.astype(F32)
    live = lane < N_EXPERTS
    biased = jnp.where(live, scores + bias_ref[...], REMOVED)

    def top2(mask):
        v = jnp.where(mask, biased, REMOVED)
        m1 = jnp.max(v, axis=-1, keepdims=True)
        i1 = jnp.min(jnp.where(v == m1, lanef, float(LANES)), axis=-1, keepdims=True)
        v2 = jnp.where(lanef == i1, REMOVED, v)
        m2 = jnp.max(v2, axis=-1, keepdims=True)
        i2 = jnp.min(jnp.where(v2 == m2, lanef, float(LANES)), axis=-1, keepdims=True)
        return m1 + m2, jnp.where((lanef == i1) | (lanef == i2), 1.0, 0.0)

    best, best_sel = top2(lane // EXPERTS_PER_GROUP == 0)
    for grp in range(1, N_GROUPS):
        score, sel = top2(lane // EXPERTS_PER_GROUP == grp)
        better = score > best
        best = jnp.where(better, score, best)
        best_sel = jnp.where(better, sel, best_sel)
    w = best_sel * scores
    gate_ref[...] = w / jnp.sum(w, axis=-1, keepdims=True)


def _router(x, router_w, router_bias, tm=512):
    T, D = x.shape
    tm = min(tm, T)
    wpad = jnp.zeros((D, LANES), F32).at[:, :N_EXPERTS].set(router_w)
    wh, wl = _split_bf16(wpad)
    bpad = jnp.zeros((1, LANES), F32).at[0, :N_EXPERTS].set(router_bias)
    return pl.pallas_call(
        _router_body,
        grid=(T // tm,),
        in_specs=[
            pl.BlockSpec((tm, D), lambda i: (i, 0)),
            pl.BlockSpec((D, LANES), lambda i: (0, 0)),
            pl.BlockSpec((D, LANES), lambda i: (0, 0)),
            pl.BlockSpec((1, LANES), lambda i: (0, 0)),
        ],
        out_specs=pl.BlockSpec((tm, LANES), lambda i: (i, 0)),
        out_shape=jax.ShapeDtypeStruct((T, LANES), F32),
        compiler_params=_cparams(("parallel",)),
    )(x, wh, wl, bpad)


def _experts_body(xb_ref, x_ref, gate_ref, wg_ref, wu_ref, wd_ref, g_ref, b_ref, y_ref, yb_ref, acc_ref):
    e = pl.program_id(1)

    @pl.when(e == 0)
    def _():
        acc_ref[...] = jnp.zeros(acc_ref.shape, F32)

    xb = xb_ref[...]
    a = _dot(xb, wg_ref[0])
    u = _dot(xb, wu_ref[0])
    lane = lax.broadcasted_iota(jnp.int32, gate_ref.shape, 1)
    gcol = jnp.sum(jnp.where(lane == e, gate_ref[...], 0.0), axis=-1, keepdims=True)
    h = a * _sigmoid(a) * u * gcol
    acc_ref[...] += _dot(h.astype(BF16), wd_ref[0])

    @pl.when(e == N_EXPERTS - 1)
    def _():
        y = _layer_norm(DN_ALPHA * x_ref[...] + acc_ref[...], g_ref[...], b_ref[...])
        y_ref[...] = y
        yb_ref[...] = y.astype(BF16)


def _experts_ln(xb, x, gate, wg, wu, wd, g, b, tm=1024):
    T, D = x.shape
    tm = min(tm, T)
    E, _, DE = wg.shape
    row = pl.BlockSpec((tm, D), lambda i, e: (i, 0))
    vec = pl.BlockSpec((1, D), lambda i, e: (0, 0))
    return pl.pallas_call(
        _experts_body,
        grid=(T // tm, E),
        in_specs=[
            row, row,
            pl.BlockSpec((tm, LANES), lambda i, e: (i, 0)),
            pl.BlockSpec((1, D, DE), lambda i, e: (e, 0, 0)),
            pl.BlockSpec((1, D, DE), lambda i, e: (e, 0, 0)),
            pl.BlockSpec((1, DE, D), lambda i, e: (e, 0, 0)),
            vec, vec,
        ],
        out_specs=[row, row],
        out_shape=[jax.ShapeDtypeStruct((T, D), F32), jax.ShapeDtypeStruct((T, D), BF16)],
        scratch_shapes=[pltpu.VMEM((tm, D), F32)],
        compiler_params=_cparams(("parallel", "arbitrary")),
    )(xb, x, gate, wg, wu, wd, g.reshape(1, D), b.reshape(1, D))


GID_LANE = N_EXPERTS
MOE_WINDOW = 1024
MOE_CHUNK = 128
MOE_VMEM_LIMIT = 60 * 1024 * 1024


def _router_sorted_body(x_ref, wh_ref, wl_ref, bias_ref, gate_ref, gidt_ref, cnt_ref):
    x_hi, x_lo = _split_bf16(x_ref[...])
    wh = wh_ref[...]
    logits = _dot(x_hi, wh) + _dot(x_lo, wh) + _dot(x_hi, wl_ref[...])
    scores = _sigmoid(logits)
    lane = lax.broadcasted_iota(jnp.int32, scores.shape, 1)
    lanef = lane.astype(F32)
    biased = jnp.where(lane < N_EXPERTS, scores + bias_ref[...], REMOVED)

    def top2(mask):
        v = jnp.where(mask, biased, REMOVED)
        m1 = jnp.max(v, axis=-1, keepdims=True)
        i1 = jnp.min(jnp.where(v == m1, lanef, float(LANES)), axis=-1, keepdims=True)
        v2 = jnp.where(lanef == i1, REMOVED, v)
        m2 = jnp.max(v2, axis=-1, keepdims=True)
        i2 = jnp.min(jnp.where(v2 == m2, lanef, float(LANES)), axis=-1, keepdims=True)
        return m1 + m2, jnp.where((lanef == i1) | (lanef == i2), 1.0, 0.0)

    best, best_sel = top2(lane // EXPERTS_PER_GROUP == 0)
    gid = jnp.zeros_like(best)
    for grp in range(1, N_GROUPS):
        score, sel = top2(lane // EXPERTS_PER_GROUP == grp)
        better = score > best
        best = jnp.where(better, score, best)
        best_sel = jnp.where(better, sel, best_sel)
        gid = jnp.where(better, float(grp), gid)
    w = best_sel * scores
    gate = w / jnp.sum(w, axis=-1, keepdims=True)
    gate_ref[...] = jnp.where(lane == GID_LANE, gid, gate)
    gid_b = jnp.broadcast_to(gid, scores.shape)
    gidt_ref[...] = gid_b.T[:8, :]
    cnt_ref[0] = jnp.broadcast_to(
        jnp.sum(jnp.where(lanef == gid_b, 1.0, 0.0), axis=0, keepdims=True), (8, LANES))


def _router_sorted(x, router_w, router_bias, tm):
    T, D = x.shape
    wpad = jnp.zeros((D, LANES), F32).at[:, :N_EXPERTS].set(router_w)
    wh, wl = _split_bf16(wpad)
    bpad = jnp.zeros((1, LANES), F32).at[0, :N_EXPERTS].set(router_bias)
    return pl.pallas_call(
        _router_sorted_body,
        grid=(T // tm,),
        in_specs=[
            pl.BlockSpec((tm, D), lambda i: (i, 0)),
            pl.BlockSpec((D, LANES), lambda i: (0, 0)),
            pl.BlockSpec((D, LANES), lambda i: (0, 0)),
            pl.BlockSpec((1, LANES), lambda i: (0, 0)),
        ],
        out_specs=[
            pl.BlockSpec((tm, LANES), lambda i: (i, 0)),
            pl.BlockSpec((8, tm), lambda i: (0, i)),
            pl.BlockSpec((1, 8, LANES), lambda i: (i, 0, 0)),
        ],
        out_shape=[
            jax.ShapeDtypeStruct((T, LANES), F32),
            jax.ShapeDtypeStruct((8, T), F32),
            jax.ShapeDtypeStruct((T // tm, 8, LANES), F32),
        ],
        compiler_params=_cparams(("parallel",)),
    )(x, wh, wl, bpad)


def _experts_sorted_body(cnt_ref, xb_ref, x_ref, gate_ref, gidt_ref, ltri_ref, utri_ref, wg_ref, wu_ref,
                         wd_ref, g_ref, b_ref, y_ref, yb_ref, xs_ref, gs_ref, acc_ref, rank_ref,
                         *, W, Wp, chunk):
    win = pl.program_id(0)
    e = pl.program_id(1)
    grp = e // EXPERTS_PER_GROUP
    padded = [((cnt_ref[win * N_GROUPS + g] + chunk - 1) // chunk) * chunk for g in range(N_GROUPS)]
    starts = [0]
    for g in range(N_GROUPS - 1):
        starts.append(starts[-1] + padded[g])
    start = starts[0]
    for g in range(1, N_GROUPS):
        start = jnp.where(grp == g, starts[g], start)
    n_chunks = (cnt_ref[win * N_GROUPS + grp] + chunk - 1) // chunk

    @pl.when(e == 0)
    def _():
        gate = gate_ref[...]
        lane = lax.broadcasted_iota(jnp.int32, gate.shape, 1)
        lanef = lane.astype(F32)
        gid = jnp.sum(jnp.where(lane == GID_LANE, gate, 0.0), axis=-1, keepdims=True)
        member = jnp.where((lanef == gid) & (lane < N_GROUPS), 1.0, 0.0)
        earlier = _dot(ltri_ref[...], member.astype(BF16))
        first = jnp.zeros(gate.shape, F32)
        for g in range(1, N_GROUPS):
            first = jnp.where(lane == g, starts[g].astype(F32), first)
        rank = jnp.sum(member * (first + earlier), axis=-1, keepdims=True)
        rank_ref[...] = jnp.broadcast_to(rank, gate.shape)

        gid_r = gidt_ref[...]
        sub = lax.broadcasted_iota(jnp.int32, gid_r.shape, 0)
        member_r = jnp.where(sub.astype(F32) == gid_r, 1.0, 0.0)
        earlier_r = _dot(member_r.astype(BF16), utri_ref[...])
        first_r = jnp.zeros(gid_r.shape, F32)
        for g in range(1, N_GROUPS):
            first_r = jnp.where(sub == g, starts[g].astype(F32), first_r)
        rank_r = jnp.sum(member_r * (first_r + earlier_r), axis=0, keepdims=True)
        rows = lax.broadcasted_iota(jnp.int32, (Wp, W), 0).astype(F32)
        perm = jnp.where(rows == rank_r, 1.0, 0.0).astype(BF16)
        xs_ref[...] = _dot(perm, xb_ref[...]).astype(BF16)
        g_hi, g_lo = _split_bf16(gate)
        gs_ref[...] = _dot(perm, g_hi) + _dot(perm, g_lo)
        acc_ref[...] = jnp.zeros(acc_ref.shape, F32)

    def one_chunk(i, carry):
        r0 = pl.multiple_of(start + i * chunk, chunk)
        xc = xs_ref[pl.ds(r0, chunk), :]
        a = _dot(xc, wg_ref[0])
        u = _dot(xc, wu_ref[0])
        gs = gs_ref[pl.ds(r0, chunk), :]
        lane = lax.broadcasted_iota(jnp.int32, gs.shape, 1)
        gcol = jnp.sum(jnp.where(lane == e, gs, 0.0), axis=-1, keepdims=True)
        h = a * _sigmoid(a) * u * gcol
        acc_ref[pl.ds(r0, chunk), :] += _dot(h.astype(BF16), wd_ref[0])
        return carry

    lax.fori_loop(0, n_chunks, one_chunk, 0)

    @pl.when(e == N_EXPERTS - 1)
    def _():
        cols = lax.broadcasted_iota(jnp.int32, (W, Wp), 1).astype(F32)
        unperm = jnp.where(cols == rank_ref[...][:, :1], 1.0, 0.0).astype(BF16)
        ffn = _dot(unperm, acc_ref[...].astype(BF16))
        y = _layer_norm(DN_ALPHA * x_ref[...] + ffn, g_ref[...], b_ref[...])
        y_ref[...] = y
        yb_ref[...] = y.astype(BF16)


def _moe_ln(xb, x, router_w, router_bias, wg, wu, wd, g, b):
    T, D = x.shape
    W = min(MOE_WINDOW, T)
    chunk = MOE_CHUNK
    Wp = W + N_GROUPS * chunk
    E, _, DE = wg.shape
    assert T % W == 0 and W % chunk == 0
    gate, gidt, cnt = _router_sorted(x, router_w, router_bias, W)
    counts = cnt[:, 0, :N_GROUPS].astype(jnp.int32).reshape(-1)
    t = np.arange(W)
    ltri = jnp.asarray((t[None, :] < t[:, None]).astype(np.float32), BF16)
    once = pl.Buffered(1)
    row = lambda shape: pl.BlockSpec(shape, lambda i, e, c: (i, 0))
    vec = pl.BlockSpec((1, D), lambda i, e, c: (0, 0))
    tri = pl.BlockSpec((W, W), lambda i, e, c: (0, 0), pipeline_mode=once)
    grid_spec = pltpu.PrefetchScalarGridSpec(
        num_scalar_prefetch=1,
        grid=(T // W, E),
        in_specs=[
            row((W, D)),
            pl.BlockSpec((W, D), lambda i, e, c: (i, 0), pipeline_mode=once),
            row((W, LANES)),
            pl.BlockSpec((8, W), lambda i, e, c: (0, i)),
            tri, tri,
            pl.BlockSpec((1, D, DE), lambda i, e, c: (e, 0, 0)),
            pl.BlockSpec((1, D, DE), lambda i, e, c: (e, 0, 0)),
            pl.BlockSpec((1, DE, D), lambda i, e, c: (e, 0, 0)),
            vec, vec,
        ],
        out_specs=[row((W, D)), row((W, D))],
        scratch_shapes=[
            pltpu.VMEM((Wp, D), BF16),
            pltpu.VMEM((Wp, LANES), F32),
            pltpu.VMEM((Wp, D), F32),
            pltpu.VMEM((W, LANES), F32),
        ],
    )
    return pl.pallas_call(
        functools.partial(_experts_sorted_body, W=W, Wp=Wp, chunk=chunk),
        grid_spec=grid_spec,
        out_shape=[jax.ShapeDtypeStruct((T, D), F32), jax.ShapeDtypeStruct((T, D), BF16)],
        compiler_params=pltpu.CompilerParams(dimension_semantics=("parallel", "arbitrary"),
                                             vmem_limit_bytes=MOE_VMEM_LIMIT),
    )(counts, xb, x, gate, gidt, ltri, ltri.T, wg, wu, wd, g.reshape(1, D), b.reshape(1, D))


def _kt_tiles(t, B, S, n, tk):
    kt = t.reshape(B, S // tk, tk, n, HEAD_DIM)
    return kt.transpose(0, 3, 1, 4, 2)


def _block_onehots(S, block, tk):
    key = np.arange(S).reshape(S // tk, 1, tk)
    r = np.arange(LANES).reshape(1, LANES, 1)
    return jnp.asarray((key // block == r).astype(np.float32), BF16)


def _rope_tiled(S):
    cos, sin = _rope_tables(jnp.arange(S))
    reps = LANES // HALF
    return jnp.tile(cos, (1, reps)), jnp.tile(sin, (1, reps))


def _nsa_mixer(xb, B, S, w_in, cmp_k_w1, cmp_k_w2, cmp_v_w1, cmp_v_w2, cmp_k_pos, cmp_v_pos):
    G, HPG, KV = NSA_KV_GROUPS, NSA_HPG, NSA_KV_DIM
    L, STR, SB = NSA_CMP_LEN, NSA_CMP_STRIDE, NSA_SEL_BLOCK
    assert L == 2 * STR and S % SB == 0 and S // SB <= LANES
    T = B * S
    n_cmp = (S - L) // STR + 1
    NC = S // STR
    n_sel = S // SB
    top_n = min(NSA_SEL_TOPN, n_sel)

    cos2, sin2 = _rope_tiled(S)
    wb = w_in.astype(BF16)
    wcol = lambda i: wb[:, D_MODEL + i * KV: D_MODEL + (i + 1) * KV]
    tn = 2 * KV
    w_main = jnp.concatenate([wb[:, :D_MODEL], wcol(2), wcol(4), wcol(0), wcol(1)], axis=1)
    proj = _proj(xb, w_main, cos2, sin2, [2] * (D_MODEL // tn) + [1, 0], S, tn=tn)
    vaug = _proj_vaug(xb, jnp.concatenate([wcol(3), wcol(5)], axis=1), tn=tn)
    wg = jnp.zeros((D_MODEL, LANES), BF16).at[:, :3 * N_HEADS].set(wb[:, D_MODEL + 6 * KV:])
    gates = _proj(xb, wg, cos2, sin2, [0], S, out_dtype=F32, tn=LANES)[:, :3 * N_HEADS]

    col = lambda i: proj[:, D_MODEL + i * KV: D_MODEL + (i + 1) * KV]
    proj3 = proj.reshape(B, S, proj.shape[1])
    vaug3 = vaug.reshape(B, S, vaug.shape[1])
    gcols = gates.reshape(B, S, 3, G, HPG).transpose(2, 0, 3, 4, 1)[..., None]

    ccos, csin = _rope_tables(jnp.arange(NC) * STR + (L - 1))
    ccos = jnp.concatenate([ccos, ccos], axis=1)
    csin = jnp.concatenate([csin, csin], axis=1)
    to_rows = lambda t: t.reshape(B, S, G, HEAD_DIM).transpose(0, 2, 1, 3).reshape(B * G, NC, STR * HEAD_DIM)
    kc = _compress(to_rows(col(2)), cmp_k_w1, cmp_k_pos, cmp_k_w2, ccos, csin, True, n_cmp)
    vc = _compress(to_rows(col(3)), cmp_v_w1, cmp_v_pos, cmp_v_w2, ccos, csin, False, n_cmp)
    kct = kc.reshape(B, G, NC, HEAD_DIM).transpose(0, 1, 3, 2)

    ci = np.arange(NC)[:, None]
    sj = np.arange(LANES)[None, :]
    overlap = ((ci * STR < (sj + 1) * SB) & (ci * STR + L > sj * SB) & (ci < n_cmp) & (sj < n_sel))
    overlap = jnp.broadcast_to(jnp.asarray(overlap.astype(np.float32), BF16), (B, G, NC, LANES))
    vo = jnp.concatenate([vc.reshape(B, G, NC, HEAD_DIM), jnp.ones((B, G, NC, 1), BF16),
                          jnp.zeros((B, G, NC, LANES - HEAD_DIM - 1), BF16), overlap], axis=-1)

    o_cmp, selb = _nsa_cmp(proj3, kct, vo, gcols[0], n_cmp, n_sel, top_n)
    tk_s = min(512, S)
    o_slc = _nsa_slc(proj3, _kt_tiles(col(0), B, S, G, tk_s), vaug3, 0,
                     _block_onehots(S, SB, tk_s), selb, gcols[1], tq=tk_s, tk=tk_s)
    tq_w = NSA_WINDOW // 2
    o_win = _nsa_win(proj3, _kt_tiles(col(1), B, S, G, tq_w), vaug3, G, gcols[2], tq=tq_w)
    return [o.reshape(T, D_MODEL) for o in (o_cmp, o_slc, o_win)]


def _moba_mixer(xb, B, S, w_in):
    H = N_HEADS
    nb = S // MOBA_BLOCK
    top_k = min(MOBA_TOPK, nb)
    cos2, sin2 = _rope_tiled(S)
    tn = 512
    n_t = D_MODEL // tn
    wb = w_in.astype(BF16)
    proj = _proj(xb, wb[:, :2 * D_MODEL], cos2, sin2, [2] * n_t + [1] * n_t, S, tn=tn)
    vaug = _proj_vaug(xb, wb[:, 2 * D_MODEL:], tn=tn)
    tk = min(2 * MOBA_BLOCK, S)
    kt = _kt_tiles(proj[:, D_MODEL:], B, S, H, tk)
    e = _block_onehots(S, MOBA_BLOCK, tk)
    kh, kl = _moba_kmean(kt, e.transpose(0, 2, 1))
    o = _moba_attn(proj.reshape(B, S, 2 * D_MODEL), kt, vaug.reshape(B, S, H * LANES), e, kh, kl, nb, top_k,
                   tq=tk, tk=tk)
    return [o.reshape(B * S, D_MODEL)]


def kernel(x, nsa_w_in, nsa_w_out, nsa_cmp_k_w1, nsa_cmp_k_w2, nsa_cmp_v_w1, nsa_cmp_v_w2, nsa_cmp_k_pos, nsa_cmp_v_pos, moba_w_in, moba_w_out, router_w, router_bias, moe_w_gate, moe_w_up, moe_w_down, ln_g, ln_b):
    B, S, D = x.shape
    xf = x.reshape(B * S, D)
    xb = xf.astype(BF16)
    for layer in range(DEPTH):
        j = layer // 2
        if layer % 2 == 0:
            os_ = _nsa_mixer(xb, B, S, nsa_w_in[j], nsa_cmp_k_w1[j], nsa_cmp_k_w2[j], nsa_cmp_v_w1[j],
                             nsa_cmp_v_w2[j], nsa_cmp_k_pos[j], nsa_cmp_v_pos[j])
            w_out = nsa_w_out[j]
        else:
            os_ = _moba_mixer(xb, B, S, moba_w_in[j])
            w_out = moba_w_out[j]
        xf, xb = _outproj_ln(os_, w_out.astype(BF16), xf, ln_g[layer, 0], ln_b[layer, 0])
        xf, xb = _moe_ln(xb, xf, router_w, router_bias, moe_w_gate[layer].astype(BF16),
                         moe_w_up[layer].astype(BF16), moe_w_down[layer].astype(BF16),
                         ln_g[layer, 1], ln_b[layer, 1])
    return xf.reshape(B, S, D)
```

```python
import functools

import jax
import jax.numpy as jnp
import numpy as np
from jax import lax
from jax.experimental import pallas as pl
from jax.experimental.pallas import tpu as pltpu

F32 = jnp.float32
BF16 = jnp.bfloat16

D_MODEL = 1024
N_HEADS = 16
HEAD_DIM = 64
HALF = HEAD_DIM // 2
ROPE_THETA = 10000.0
DEPTH = 2
DN_ALPHA = (2 * DEPTH) ** 0.25
LN_EPS = 1e-5
NEG = -1e30
FORCE = 1e9
MASK = -1e30
REMOVED = -3.0e38
LANES = 128
Q_SCALE_LOG2 = float(HEAD_DIM ** -0.5 * np.log2(np.e))
ROW_CHUNK = 128

NSA_KV_GROUPS = 4
NSA_HPG = N_HEADS // NSA_KV_GROUPS
NSA_KV_DIM = NSA_KV_GROUPS * HEAD_DIM
NSA_CMP_LEN = 32
NSA_CMP_STRIDE = 16
NSA_SEL_BLOCK = 64
NSA_SEL_TOPN = 16
NSA_WINDOW = 512

MOBA_BLOCK = 256
MOBA_TOPK = 3

N_EXPERTS = 16
N_GROUPS = 4
EXPERTS_PER_GROUP = N_EXPERTS // N_GROUPS
D_EXPERT = 512

VMEM_LIMIT = 48 * 1024 * 1024


def _cparams(sem):
    return pltpu.CompilerParams(dimension_semantics=sem, vmem_limit_bytes=VMEM_LIMIT)


def _dot(a, b):
    return jnp.dot(a, b, preferred_element_type=F32)


def _split_bf16(x):
    hi = x.astype(BF16)
    lo = (x - hi.astype(F32)).astype(BF16)
    return hi, lo


def _sigmoid(x):
    return 1.0 / (1.0 + jnp.exp(-x))


def _proj_body(mode_ref, x_ref, w_ref, cos_ref, sin_ref, o_ref, *, tn):
    j = pl.program_id(0)
    mode = mode_ref[j]
    acc = _dot(x_ref[...], w_ref[...])

    @pl.when(mode == 0)
    def _():
        o_ref[...] = acc.astype(o_ref.dtype)

    @pl.when(mode != 0)
    def _():
        sc = jnp.where(mode == 2, Q_SCALE_LOG2, 1.0).astype(F32)
        cos = cos_ref[...] * sc
        sin = sin_ref[...] * sc
        for c in range(tn // LANES):
            o_ref[:, c * LANES:(c + 1) * LANES] = (
                _rope_chunk(acc[:, c * LANES:(c + 1) * LANES], cos, sin).astype(o_ref.dtype))


def _proj(xb, w, cos2, sin2, modes, seq, out_dtype=BF16, tm=1024, tn=512):
    T, K = xb.shape
    N = w.shape[1]
    tm = min(tm, seq)
    assert T % tm == 0 and N % tn == 0 and seq % tm == 0 and len(modes) == N // tn
    n_pos = seq // tm
    grid_spec = pltpu.PrefetchScalarGridSpec(
        num_scalar_prefetch=1,
        grid=(N // tn, T // tm),
        in_specs=[
            pl.BlockSpec((tm, K), lambda j, i, m: (i, 0)),
            pl.BlockSpec((K, tn), lambda j, i, m: (0, j)),
            pl.BlockSpec((tm, LANES), lambda j, i, m: (i % n_pos, 0)),
            pl.BlockSpec((tm, LANES), lambda j, i, m: (i % n_pos, 0)),
        ],
        out_specs=pl.BlockSpec((tm, tn), lambda j, i, m: (i, j)),
    )
    return pl.pallas_call(
        functools.partial(_proj_body, tn=tn),
        grid_spec=grid_spec,
        out_shape=jax.ShapeDtypeStruct((T, N), out_dtype),
        compiler_params=_cparams(("parallel", "parallel")),
    )(jnp.asarray(modes, jnp.int32), xb, w, cos2, sin2)


def _rope_tables(pos):
    inv = 1.0 / (ROPE_THETA ** (jnp.arange(0, HEAD_DIM, 2, dtype=F32) / HEAD_DIM))
    ang = pos.astype(F32)[:, None] * inv[None, :]
    return jnp.cos(ang), jnp.sin(ang)


def _rope_chunk(a, cos, sin):
    lane = lax.broadcasted_iota(jnp.int32, a.shape, 1)
    up = pltpu.roll(a, LANES - HALF, 1)
    dn = pltpu.roll(a, HALF, 1)
    return a * cos + jnp.where((lane % HEAD_DIM) < HALF, -up, dn) * sin


def _proj_kt_body(x_ref, w_ref, cos_ref, sin_ref, o_ref, *, tn, tk):
    acc = _dot(x_ref[...], w_ref[...])
    cos = cos_ref[...]
    sin = sin_ref[...]
    tm = acc.shape[0]
    for c in range(tn // LANES):
        kt = _rope_chunk(acc[:, c * LANES:(c + 1) * LANES], cos, sin).T
        for hh in range(2):
            for kk in range(tm // tk):
                o_ref[0, 2 * c + hh, kk] = (
                    kt[hh * HEAD_DIM:(hh + 1) * HEAD_DIM, kk * tk:(kk + 1) * tk].astype(o_ref.dtype))


def _proj_kt(xb, w, cos2, sin2, B, S, tk, tm=1024, tn=256):
    T, K = xb.shape
    N = w.shape[1]
    tm = min(tm, S)
    tn = min(tn, N)
    assert S % tm == 0 and tm % tk == 0 and N % tn == 0
    n_pos = S // tm
    return pl.pallas_call(
        functools.partial(_proj_kt_body, tn=tn, tk=tk),
        grid=(N // tn, T // tm),
        in_specs=[
            pl.BlockSpec((tm, K), lambda j, i: (i, 0)),
            pl.BlockSpec((K, tn), lambda j, i: (0, j)),
            pl.BlockSpec((tm, LANES), lambda j, i: (i % n_pos, 0)),
            pl.BlockSpec((tm, LANES), lambda j, i: (i % n_pos, 0)),
        ],
        out_specs=pl.BlockSpec((1, tn // HEAD_DIM, tm // tk, HEAD_DIM, tk),
                               lambda j, i: (i // n_pos, j, i % n_pos, 0, 0)),
        out_shape=jax.ShapeDtypeStruct((B, N // HEAD_DIM, S // tk, HEAD_DIM, tk), BF16),
        compiler_params=_cparams(("parallel", "parallel")),
    )(xb, w, cos2, sin2)


def _proj_vaug_body(x_ref, w_ref, o_ref, *, tn):
    acc = _dot(x_ref[...], w_ref[...])
    lane = lax.broadcasted_iota(jnp.int32, (acc.shape[0], LANES), 1)
    tail = jnp.where(lane == HEAD_DIM, 1.0, 0.0)
    for c in range(tn // LANES):
        a = acc[:, c * LANES:(c + 1) * LANES]
        for k, head in enumerate((a, pltpu.roll(a, HEAD_DIM, 1))):
            o_ref[:, (2 * c + k) * LANES:(2 * c + k + 1) * LANES] = (
                jnp.where(lane < HEAD_DIM, head, tail).astype(o_ref.dtype))


def _proj_vaug(xb, w, tm=1024, tn=512):
    T, K = xb.shape
    N = w.shape[1]
    tm = min(tm, T)
    assert T % tm == 0 and N % tn == 0
    return pl.pallas_call(
        functools.partial(_proj_vaug_body, tn=tn),
        grid=(N // tn, T // tm),
        in_specs=[pl.BlockSpec((tm, K), lambda j, i: (i, 0)), pl.BlockSpec((K, tn), lambda j, i: (0, j))],
        out_specs=pl.BlockSpec((tm, 2 * tn), lambda j, i: (i, j)),
        out_shape=jax.ShapeDtypeStruct((T, 2 * N), BF16),
        compiler_params=_cparams(("parallel", "parallel")),
    )(xb, w)


def _gelu_tanh(x):
    c = np.float32(np.sqrt(2.0 / np.pi))
    return 0.5 * x * (1.0 + jnp.tanh(c * (x + 0.044715 * (x * x * x))))


def _compress_body(r_ref, w1_ref, pos_ref, w2_ref, w2r_ref, cos_ref, sin_ref, o_ref, *, rope, n_cmp):
    r = r_ref[0]
    nc = r.shape[0]
    half = NSA_CMP_STRIDE * HEAD_DIM
    a = _dot(r, w1_ref[0])
    b = _dot(r, w1_ref[1])
    pos = pos_ref[...]
    pb = _dot(pos[:, :half], w1_ref[0]) + _dot(pos[:, half:], w1_ref[1])
    b_next = pltpu.roll(b, nc - 1, 0)
    h = _gelu_tanh(a + b_next + pb[0:1, :]).astype(BF16)
    o = _dot(h, w2_ref[...])
    if rope:
        o = o * cos_ref[...] + _dot(h, w2r_ref[...]) * sin_ref[...]
    row = lax.broadcasted_iota(jnp.int32, o.shape, 0)
    o_ref[0] = jnp.where(row < n_cmp, o, 0.0).astype(o_ref.dtype)


def _compress(r, w1, pos, w2, cos_c, sin_c, rope, n_cmp):
    BG, NC, K = r.shape
    hidden = w1.shape[1]
    w1s = w1.astype(BF16).reshape(2, K, hidden)
    pos8 = jnp.zeros((8, 2 * K), BF16).at[0].set(pos.reshape(-1).astype(BF16))
    w2r = jnp.concatenate([-w2[:, HALF:], w2[:, :HALF]], axis=1).astype(BF16)
    full = lambda shape: pl.BlockSpec(shape, lambda i: (0,) * len(shape))
    return pl.pallas_call(
        functools.partial(_compress_body, rope=rope, n_cmp=n_cmp),
        grid=(BG,),
        in_specs=[
            pl.BlockSpec((1, NC, K), lambda i: (i, 0, 0)),
            full((2, K, hidden)),
            full((8, 2 * K)),
            full((hidden, HEAD_DIM)),
            full((hidden, HEAD_DIM)),
            full((NC, HEAD_DIM)),
            full((NC, HEAD_DIM)),
        ],
        out_specs=pl.BlockSpec((1, NC, HEAD_DIM), lambda i: (i, 0, 0)),
        out_shape=jax.ShapeDtypeStruct((BG, NC, HEAD_DIM), BF16),
        compiler_params=_cparams(("parallel",)),
    )(r, w1s, pos8, w2.astype(BF16), w2r, cos_c, sin_c)


def _topk_mask_t(v, k):
    idx = lax.broadcasted_iota(jnp.int32, v.shape, 0).astype(F32)

    def step(_, cur):
        m = jnp.max(cur, axis=0, keepdims=True)
        first = jnp.min(jnp.where(cur == m, idx, float(LANES)), axis=0, keepdims=True)
        return jnp.where(idx == first, REMOVED, cur)

    return lax.fori_loop(0, k, step, v, unroll=True) != v


def _head_kt(e, kt, parity):
    z = jnp.zeros_like(kt)
    parts = ([] if e is None else [e]) + ([kt, z] if parity == 0 else [z, kt])
    return jnp.concatenate(parts, axis=0)


def _flash_init(m_ref, acc_ref):
    m_ref[...] = jnp.full(m_ref.shape, -jnp.inf, F32)
    acc_ref[...] = jnp.zeros(acc_ref.shape, F32)


def _flash_step(jobs, s_ref, m_ref, alpha_ref, acc_ref):
    for r0, q, kt, _, mask in jobs:
        rows = q.shape[0]
        s = _dot(q, kt)
        if mask is not None:
            s = jnp.where(mask, s, MASK)
        s_ref[r0:r0 + rows, :] = s
        cm = s[:, :LANES]
        for c in range(1, s.shape[1] // LANES):
            cm = jnp.maximum(cm, s[:, c * LANES:(c + 1) * LANES])
        m_prev = m_ref[r0:r0 + rows, :]
        m_new = jnp.maximum(m_prev, jnp.max(cm, axis=-1, keepdims=True))
        alpha_ref[r0:r0 + rows, :] = jnp.exp2(m_prev - m_new)
        m_ref[r0:r0 + rows, :] = m_new
    for r0, q, _, v, _ in jobs:
        for r in range(r0, r0 + q.shape[0], ROW_CHUNK):
            rows = min(ROW_CHUNK, r0 + q.shape[0] - r)
            m = m_ref[r:r + rows, :]
            p = jnp.concatenate(
                [jnp.exp2(s_ref[r:r + rows, c * LANES:(c + 1) * LANES] - m)
                 for c in range(s_ref.shape[1] // LANES)], axis=1).astype(BF16)
            acc_ref[r:r + rows, :] = alpha_ref[r:r + rows, :] * acc_ref[r:r + rows, :] + _dot(p, v)


def _flash_scratch(rows, tk):
    return [pltpu.VMEM((rows, tk), F32), pltpu.VMEM((rows, LANES), F32),
            pltpu.VMEM((rows, LANES), F32), pltpu.VMEM((rows, LANES), F32)]


def _pair_merge(even, odd):
    lane = lax.broadcasted_iota(jnp.int32, even.shape, 1)
    return jnp.where(lane < HEAD_DIM, even, pltpu.roll(odd, HEAD_DIM, 1))


def _flash_pairs_out(acc_ref, tq, n_heads, gates):
    def head(h):
        acc = acc_ref[h * tq:(h + 1) * tq, :]
        return acc * (gates[h] / acc[:, HEAD_DIM:HEAD_DIM + 1])
    return jnp.concatenate([_pair_merge(head(h), head(h + 1)) for h in range(0, n_heads, 2)], axis=1)


def _gate_col(logits, branch, h):
    idx = branch * N_HEADS + pl.program_id(1) * NSA_HPG + h
    lane = lax.broadcasted_iota(jnp.int32, logits.shape, 1)
    return _sigmoid(jnp.sum(jnp.where(lane == idx, logits, 0.0), axis=-1, keepdims=True))


def _tile_iotas(rows, tk):
    return (lax.broadcasted_iota(jnp.int32, (rows, tk), 0),
            lax.broadcasted_iota(jnp.int32, (rows, tk), 1))


def _nsa_cmp_body(q_ref, kct_ref, vo_ref, g_ref, o_ref, selb_ref, s_ref, m_ref, *, tq, n_cmp, n_sel, top_n):
    s0 = pl.program_id(2) * tq
    q4 = q_ref[0]
    kct = kct_ref[0, 0]
    vo = vo_ref[0, 0]
    nc = vo.shape[0]
    tpos = s0 + lax.broadcasted_iota(jnp.int32, (tq, nc), 0)
    nidx = lax.broadcasted_iota(jnp.int32, (tq, nc), 1)
    cmask = (nidx * NSA_CMP_STRIDE + (NSA_CMP_LEN - 1) <= tpos) & (nidx < n_cmp)
    for h in range(NSA_HPG):
        pair = q4[:, (h // 2) * LANES:(h // 2 + 1) * LANES]
        s = jnp.where(cmask, _dot(pair, _head_kt(None, kct, h % 2)), NEG)
        s_ref[h * tq:(h + 1) * tq, :] = s
        cm = s[:, :LANES]
        for c in range(1, nc // LANES):
            cm = jnp.maximum(cm, s[:, c * LANES:(c + 1) * LANES])
        m_ref[h * tq:(h + 1) * tq, :] = jnp.broadcast_to(jnp.max(cm, axis=-1, keepdims=True), (tq, LANES))

    rc = min(ROW_CHUNK, tq)
    imps = []
    for r0 in range(0, tq, rc):
        imp = jnp.zeros((rc, LANES), F32)
        heads = []
        for h in range(NSA_HPG):
            r = h * tq + r0
            m = m_ref[r:r + rc, :]
            e = jnp.concatenate(
                [jnp.exp2(s_ref[r:r + rc, c * LANES:(c + 1) * LANES] - m) for c in range(nc // LANES)],
                axis=1).astype(BF16)
            res = _dot(e, vo)
            inv = jnp.where(m[:, :1] > 0.5 * NEG, 1.0 / res[:, HEAD_DIM:HEAD_DIM + 1], 0.0)
            gate = _gate_col(g_ref[0, r0:r0 + rc, :], 0, h)
            heads.append(res[:, :LANES] * (inv * gate))
            imp = imp + res[:, LANES:] * inv
        o_ref[0, r0:r0 + rc, :] = jnp.concatenate(
            [_pair_merge(heads[h], heads[h + 1]) for h in range(0, NSA_HPG, 2)], axis=1).astype(o_ref.dtype)
        imps.append(imp)
    imp = jnp.concatenate(imps, axis=0)

    blk = lax.broadcasted_iota(jnp.int32, imp.shape, 1)
    jq = (s0 + lax.broadcasted_iota(jnp.int32, imp.shape, 0)) // NSA_SEL_BLOCK
    forced = (blk == 0) | (blk == jq) | (blk == jq - 1)
    imp = jnp.where(blk > jq, NEG, jnp.where(forced, FORCE, imp))
    imp = jnp.where(blk < n_sel, imp, REMOVED)
    sel_t = _topk_mask_t(imp.T, top_n)
    selb_ref[0, 0] = jnp.where(sel_t, 0.0, MASK).T.astype(selb_ref.dtype)


def _nsa_cmp(proj3, kct, vo, g_cmp, n_cmp, n_sel, top_n, tq=256):
    B, S, _ = proj3.shape
    G, HPG = NSA_KV_GROUPS, NSA_HPG
    NC = vo.shape[2]
    tq = min(tq, S)
    assert NC % LANES == 0
    return pl.pallas_call(
        functools.partial(_nsa_cmp_body, tq=tq, n_cmp=n_cmp, n_sel=n_sel, top_n=top_n),
        grid=(B, G, S // tq),
        in_specs=[
            pl.BlockSpec((1, tq, HPG * HEAD_DIM), lambda b, g, i: (b, i, g)),
            pl.BlockSpec((1, 1, HEAD_DIM, NC), lambda b, g, i: (b, g, 0, 0)),
            pl.BlockSpec((1, 1, NC, 2 * LANES), lambda b, g, i: (b, g, 0, 0)),
            pl.BlockSpec((1, tq, LANES), lambda b, g, i: (b, i, 0)),
        ],
        out_specs=[
            pl.BlockSpec((1, tq, HPG * HEAD_DIM), lambda b, g, i: (b, i, g)),
            pl.BlockSpec((1, 1, tq, LANES), lambda b, g, i: (b, g, i, 0)),
        ],
        out_shape=[
            jax.ShapeDtypeStruct((B, S, D_MODEL), BF16),
            jax.ShapeDtypeStruct((B, G, S, LANES), BF16),
        ],
        scratch_shapes=[pltpu.VMEM((HPG * tq, NC), F32), pltpu.VMEM((HPG * tq, LANES), F32)],
        compiler_params=_cparams(("parallel", "parallel", "parallel")),
    )(proj3, kct, vo, g_cmp)


def _nsa_slc_body(q_ref, kt_ref, v_ref, e_ref, selb_ref, g_ref, o_ref, qa_ref, s_ref, m_ref, alpha_ref,
                  acc_ref, *, tq, tk):
    s0 = pl.program_id(2) * tq
    _flash_init(m_ref, acc_ref)
    selb = selb_ref[0, 0]
    q4 = q_ref[0]
    for h in range(NSA_HPG):
        qa_ref[h * tq:(h + 1) * tq, :] = jnp.concatenate(
            [selb, q4[:, (h // 2) * LANES:(h // 2 + 1) * LANES]], axis=1)

    def tile(ki, causal):
        k0 = pl.multiple_of(ki * tk, tk)
        kts = [_head_kt(e_ref[ki], kt_ref[0, 0, ki], parity) for parity in range(2)]
        v = v_ref[0, pl.ds(k0, tk), :]
        row, col = _tile_iotas(tq, tk)
        mask = (k0 + col <= s0 + row) if causal else None
        jobs = [(h * tq, qa_ref[h * tq:(h + 1) * tq, :], kts[h % 2], v, mask) for h in range(NSA_HPG)]
        _flash_step(jobs, s_ref, m_ref, alpha_ref, acc_ref)

    last = (s0 + tq - 1) // tk

    def full_tile(ki, carry):
        tile(ki, False)
        return carry

    lax.fori_loop(0, last, full_tile, 0)
    tile(last, True)
    gates = [_gate_col(g_ref[0], 1, h) for h in range(NSA_HPG)]
    o_ref[0] = _flash_pairs_out(acc_ref, tq, NSA_HPG, gates).astype(o_ref.dtype)


def _nsa_slc(proj3, kt, vaug3, v_block, e, selb, g, tq, tk):
    B, S, _ = proj3.shape
    G, HPG = NSA_KV_GROUPS, NSA_HPG
    nk = S // tk
    assert tk % tq == 0 and kt.shape == (B, G, nk, HEAD_DIM, tk) and e.shape == (nk, LANES, tk)
    once = pl.Buffered(1)
    return pl.pallas_call(
        functools.partial(_nsa_slc_body, tq=tq, tk=tk),
        grid=(B, G, S // tq),
        in_specs=[
            pl.BlockSpec((1, tq, HPG * HEAD_DIM), lambda b, g_, i: (b, i, g_)),
            pl.BlockSpec((1, 1, nk, HEAD_DIM, tk), lambda b, g_, i: (b, g_, 0, 0, 0), pipeline_mode=once),
            pl.BlockSpec((1, S, LANES), lambda b, g_, i: (b, 0, v_block + g_), pipeline_mode=once),
            pl.BlockSpec((nk, LANES, tk), lambda b, g_, i: (0, 0, 0), pipeline_mode=once),
            pl.BlockSpec((1, 1, tq, LANES), lambda b, g_, i: (b, g_, i, 0)),
            pl.BlockSpec((1, tq, LANES), lambda b, g_, i: (b, i, 0)),
        ],
        out_specs=pl.BlockSpec((1, tq, HPG * HEAD_DIM), lambda b, g_, i: (b, i, g_)),
        out_shape=jax.ShapeDtypeStruct((B, S, D_MODEL), BF16),
        scratch_shapes=[pltpu.VMEM((HPG * tq, 2 * LANES), BF16)] + _flash_scratch(HPG * tq, tk),
        compiler_params=_cparams(("parallel", "parallel", "parallel")),
    )(proj3, kt, vaug3, e, selb, g)


def _nsa_win_body(q_ref, kt_ref, v_ref, g_ref, o_ref, s_ref, m_ref, *, tq, n_tiles):
    s0 = pl.program_id(2) * tq
    tk = n_tiles * tq
    k_first = jnp.maximum(pl.program_id(2) - (n_tiles - 1), 0)
    k0 = pl.multiple_of(k_first * tq, tq)
    kt = jnp.concatenate([kt_ref[0, 0, k_first + j] for j in range(n_tiles)], axis=1)
    v = v_ref[0, pl.ds(k0, tk), :]
    row, col = _tile_iotas(tq, tk)
    dist = (s0 - k0) + row - col
    valid = (dist >= 0) & (dist < NSA_WINDOW)
    q4 = q_ref[0]
    for h in range(NSA_HPG):
        pair = q4[:, (h // 2) * LANES:(h // 2 + 1) * LANES]
        s = jnp.where(valid, _dot(pair, _head_kt(None, kt, h % 2)), MASK)
        s_ref[h * tq:(h + 1) * tq, :] = s
        cm = s[:, :LANES]
        for c in range(1, tk // LANES):
            cm = jnp.maximum(cm, s[:, c * LANES:(c + 1) * LANES])
        m_ref[h * tq:(h + 1) * tq, :] = jnp.broadcast_to(jnp.max(cm, axis=-1, keepdims=True), (tq, LANES))

    rc = min(ROW_CHUNK, tq)
    for r0 in range(0, tq, rc):
        heads = []
        for h in range(NSA_HPG):
            r = h * tq + r0
            m = m_ref[r:r + rc, :]
            p = jnp.concatenate(
                [jnp.exp2(s_ref[r:r + rc, c * LANES:(c + 1) * LANES] - m) for c in range(tk // LANES)],
                axis=1).astype(BF16)
            res = _dot(p, v)
            heads.append(res * (_gate_col(g_ref[0, r0:r0 + rc, :], 2, h) / res[:, HEAD_DIM:HEAD_DIM + 1]))
        o_ref[0, r0:r0 + rc, :] = jnp.concatenate(
            [_pair_merge(heads[h], heads[h + 1]) for h in range(0, NSA_HPG, 2)], axis=1).astype(o_ref.dtype)


def _nsa_win(proj3, kt, vaug3, v_block, g, tq):
    B, S, _ = proj3.shape
    G, HPG = NSA_KV_GROUPS, NSA_HPG
    nk = S // tq
    n_tiles = NSA_WINDOW // tq + 1
    assert NSA_WINDOW % tq == 0 and nk >= n_tiles and kt.shape == (B, G, nk, HEAD_DIM, tq)
    once = pl.Buffered(1)
    return pl.pallas_call(
        functools.partial(_nsa_win_body, tq=tq, n_tiles=n_tiles),
        grid=(B, G, S // tq),
        in_specs=[
            pl.BlockSpec((1, tq, HPG * HEAD_DIM), lambda b, g_, i: (b, i, g_)),
            pl.BlockSpec((1, 1, nk, HEAD_DIM, tq), lambda b, g_, i: (b, g_, 0, 0, 0), pipeline_mode=once),
            pl.BlockSpec((1, S, LANES), lambda b, g_, i: (b, 0, v_block + g_), pipeline_mode=once),
            pl.BlockSpec((1, tq, LANES), lambda b, g_, i: (b, i, 0)),
        ],
        out_specs=pl.BlockSpec((1, tq, HPG * HEAD_DIM), lambda b, g_, i: (b, i, g_)),
        out_shape=jax.ShapeDtypeStruct((B, S, D_MODEL), BF16),
        scratch_shapes=[pltpu.VMEM((HPG * tq, n_tiles * tq), F32), pltpu.VMEM((HPG * tq, LANES), F32)],
        compiler_params=_cparams(("parallel", "parallel", "parallel")),
    )(proj3, kt, vaug3, g)


def _moba_kmean_body(kt_ref, et_ref, hi_ref, lo_ref):
    nk = kt_ref.shape[2]
    km = jnp.zeros((HEAD_DIM, LANES), F32)
    for ki in range(nk):
        km = km + _dot(kt_ref[0, 0, ki], et_ref[ki])
    hi, lo = _split_bf16(km * (1.0 / MOBA_BLOCK))
    hi_ref[0, 0] = hi
    lo_ref[0, 0] = lo


def _moba_kmean(kt, et):
    B, H, nk, _, tk = kt.shape
    spec = pl.BlockSpec((1, 1, HEAD_DIM, LANES), lambda b, h: (b, h, 0, 0))
    return pl.pallas_call(
        _moba_kmean_body,
        grid=(B, H),
        in_specs=[
            pl.BlockSpec((1, 1, nk, HEAD_DIM, tk), lambda b, h: (b, h, 0, 0, 0)),
            pl.BlockSpec((nk, tk, LANES), lambda b, h: (0, 0, 0)),
        ],
        out_specs=[spec, spec],
        out_shape=[jax.ShapeDtypeStruct((B, H, HEAD_DIM, LANES), BF16)] * 2,
        compiler_params=_cparams(("parallel", "parallel")),
    )(kt, et)


def _moba_body(q_ref, kt_ref, v_ref, e_ref, kh_ref, kl_ref, o_ref, qa_ref, s_ref, m_ref, alpha_ref,
               acc_ref, *, tq, tk, hb, nb, top_k):
    s0 = pl.program_id(2) * tq
    _flash_init(m_ref, acc_ref)

    blk = lax.broadcasted_iota(jnp.int32, (LANES, tq), 0)
    cb = (s0 + lax.broadcasted_iota(jnp.int32, (LANES, tq), 1)) // MOBA_BLOCK
    qs = q_ref[0]
    for h in range(hb):
        pair = qs[:, (h // 2) * LANES:(h // 2 + 1) * LANES]
        gsc = (_dot(pair, _head_kt(None, kh_ref[0, h], h % 2))
               + _dot(pair, _head_kt(None, kl_ref[0, h], h % 2))).T
        gsc = jnp.where(blk < cb, gsc, NEG)
        gsc = jnp.where(blk < nb, gsc, REMOVED)
        sel = (_topk_mask_t(gsc, top_k) & (blk < cb)) | (blk == cb)
        qa_ref[h * tq:(h + 1) * tq, :] = jnp.concatenate(
            [jnp.where(sel, 0.0, MASK).T.astype(BF16), pair], axis=1)

    def tile(ki, causal):
        k0 = pl.multiple_of(ki * tk, tk)
        e = e_ref[ki]
        row, col = _tile_iotas(tq, tk)
        mask = (k0 + col <= s0 + row) if causal else None
        jobs = [(h * tq, qa_ref[h * tq:(h + 1) * tq, :], _head_kt(e, kt_ref[0, h, ki], h % 2),
                 v_ref[0, pl.ds(k0, tk), h * LANES:(h + 1) * LANES], mask) for h in range(hb)]
        _flash_step(jobs, s_ref, m_ref, alpha_ref, acc_ref)

    last = (s0 + tq - 1) // tk

    def full_tile(ki, carry):
        tile(ki, False)
        return carry

    lax.fori_loop(0, last, full_tile, 0)
    tile(last, True)
    o_ref[0] = _flash_pairs_out(acc_ref, tq, hb, [1.0] * hb).astype(o_ref.dtype)


def _moba_attn(proj3, kt, vaug3, e, kh, kl, nb, top_k, tq, tk, hb=4):
    B, S, _ = proj3.shape
    H = N_HEADS
    nk = S // tk
    assert tk % tq == 0 and tk % MOBA_BLOCK == 0 and H % hb == 0 and hb % 2 == 0 and nb <= LANES
    once = pl.Buffered(1)
    return pl.pallas_call(
        functools.partial(_moba_body, tq=tq, tk=tk, hb=hb, nb=nb, top_k=top_k),
        grid=(B, H // hb, S // tq),
        in_specs=[
            pl.BlockSpec((1, tq, hb * HEAD_DIM), lambda b, h, i: (b, i, h)),
            pl.BlockSpec((1, hb, nk, HEAD_DIM, tk), lambda b, h, i: (b, h, 0, 0, 0), pipeline_mode=once),
            pl.BlockSpec((1, S, hb * LANES), lambda b, h, i: (b, 0, h), pipeline_mode=once),
            pl.BlockSpec((nk, LANES, tk), lambda b, h, i: (0, 0, 0), pipeline_mode=once),
            pl.BlockSpec((1, hb, HEAD_DIM, LANES), lambda b, h, i: (b, h, 0, 0), pipeline_mode=once),
            pl.BlockSpec((1, hb, HEAD_DIM, LANES), lambda b, h, i: (b, h, 0, 0), pipeline_mode=once),
        ],
        out_specs=pl.BlockSpec((1, tq, hb * HEAD_DIM), lambda b, h, i: (b, i, h)),
        out_shape=jax.ShapeDtypeStruct((B, S, D_MODEL), BF16),
        scratch_shapes=[pltpu.VMEM((hb * tq, 2 * LANES), BF16)] + _flash_scratch(hb * tq, tk),
        compiler_params=_cparams(("parallel", "parallel", "parallel")),
    )(proj3, kt, vaug3, e, kh, kl)


def _layer_norm(r, g, b):
    mu = jnp.mean(r, axis=-1, keepdims=True)
    c = r - mu
    var = jnp.mean(c * c, axis=-1, keepdims=True)
    return c * lax.rsqrt(var + LN_EPS) * g + b


def _outproj_body(*refs, n_o):
    o_refs = refs[:n_o]
    w_ref, x_ref, g_ref, b_ref, y_ref, yb_ref = refs[n_o:]
    o = o_refs[0][...].astype(F32)
    for r in o_refs[1:]:
        o = o + r[...].astype(F32)
    mix = _dot(o.astype(BF16), w_ref[...])
    y = _layer_norm(DN_ALPHA * x_ref[...] + mix, g_ref[...], b_ref[...])
    y_ref[...] = y
    yb_ref[...] = y.astype(BF16)


def _outproj_ln(os_, w, x, g, b, tm=512):
    T, D = x.shape
    tm = min(tm, T)
    n_o = len(os_)
    row = pl.BlockSpec((tm, D), lambda i: (i, 0))
    vec = pl.BlockSpec((1, D), lambda i: (0, 0))
    return pl.pallas_call(
        functools.partial(_outproj_body, n_o=n_o),
        grid=(T // tm,),
        in_specs=[row] * n_o + [pl.BlockSpec((D, D), lambda i: (0, 0)), row, vec, vec],
        out_specs=[row, row],
        out_shape=[jax.ShapeDtypeStruct((T, D), F32), jax.ShapeDtypeStruct((T, D), BF16)],
        compiler_params=_cparams(("parallel",)),
    )(*os_, w, x, g.reshape(1, D), b.reshape(1, D))


def _router_body(x_ref, wh_ref, wl_ref, bias_ref, gate_ref):
    x_hi, x_lo = _split_bf16(x_ref[...])
    wh = wh_ref[...]
    logits = _dot(x_hi, wh) + _dot(x_lo, wh) + _dot(x_hi, wl_ref[...])
    scores = _sigmoid(logits)
    lane = lax.broadcasted_iota(jnp.int32, scores.shape, 1)
    lanef = lane.astype(F32)
    live = lane < N_EXPERTS
    biased = jnp.where(live, scores + bias_ref[...], REMOVED)

    def top2(mask):
        v = jnp.where(mask, biased, REMOVED)
        m1 = jnp.max(v, axis=-1, keepdims=True)
        i1 = jnp.min(jnp.where(v == m1, lanef, float(LANES)), axis=-1, keepdims=True)
        v2 = jnp.where(lanef == i1, REMOVED, v)
        m2 = jnp.max(v2, axis=-1, keepdims=True)
        i2 = jnp.min(jnp.where(v2 == m2, lanef, float(LANES)), axis=-1, keepdims=True)
        return m1 + m2, jnp.where((lanef == i1) | (lanef == i2), 1.0, 0.0)

    best, best_sel = top2(lane // EXPERTS_PER_GROUP == 0)
    for grp in range(1, N_GROUPS):
        score, sel = top2(lane // EXPERTS_PER_GROUP == grp)
        better = score > best
        best = jnp.where(better, score, best)
        best_sel = jnp.where(better, sel, best_sel)
    w = best_sel * scores
    gate_ref[...] = w / jnp.sum(w, axis=-1, keepdims=True)


def _router(x, router_w, router_bias, tm=512):
    T, D = x.shape
    tm = min(tm, T)
    wpad = jnp.zeros((D, LANES), F32).at[:, :N_EXPERTS].set(router_w)
    wh, wl = _split_bf16(wpad)
    bpad = jnp.zeros((1, LANES), F32).at[0, :N_EXPERTS].set(router_bias)
    return pl.pallas_call(
        _router_body,
        grid=(T // tm,),
        in_specs=[
            pl.BlockSpec((tm, D), lambda i: (i, 0)),
            pl.BlockSpec((D, LANES), lambda i: (0, 0)),
            pl.BlockSpec((D, LANES), lambda i: (0, 0)),
            pl.BlockSpec((1, LANES), lambda i: (0, 0)),
        ],
        out_specs=pl.BlockSpec((tm, LANES), lambda i: (i, 0)),
        out_shape=jax.ShapeDtypeStruct((T, LANES), F32),
        compiler_params=_cparams(("parallel",)),
    )(x, wh, wl, bpad)


def _experts_body(xb_ref, x_ref, gate_ref, wg_ref, wu_ref, wd_ref, g_ref, b_ref, y_ref, yb_ref, acc_ref):
    e = pl.program_id(1)

    @pl.when(e == 0)
    def _():
        acc_ref[...] = jnp.zeros(acc_ref.shape, F32)

    xb = xb_ref[...]
    a = _dot(xb, wg_ref[0])
    u = _dot(xb, wu_ref[0])
    lane = lax.broadcasted_iota(jnp.int32, gate_ref.shape, 1)
    gcol = jnp.sum(jnp.where(lane == e, gate_ref[...], 0.0), axis=-1, keepdims=True)
    h = a * _sigmoid(a) * u * gcol
    acc_ref[...] += _dot(h.astype(BF16), wd_ref[0])

    @pl.when(e == N_EXPERTS - 1)
    def _():
        y = _layer_norm(DN_ALPHA * x_ref[...] + acc_ref[...], g_ref[...], b_ref[...])
        y_ref[...] = y
        yb_ref[...] = y.astype(BF16)


def _experts_ln(xb, x, gate, wg, wu, wd, g, b, tm=1024):
    T, D = x.shape
    tm = min(tm, T)
    E, _, DE = wg.shape
    row = pl.BlockSpec((tm, D), lambda i, e: (i, 0))
    vec = pl.BlockSpec((1, D), lambda i, e: (0, 0))
    return pl.pallas_call(
        _experts_body,
        grid=(T // tm, E),
        in_specs=[
            row, row,
            pl.BlockSpec((tm, LANES), lambda i, e: (i, 0)),
            pl.BlockSpec((1, D, DE), lambda i, e: (e, 0, 0)),
            pl.BlockSpec((1, D, DE), lambda i, e: (e, 0, 0)),
            pl.BlockSpec((1, DE, D), lambda i, e: (e, 0, 0)),
            vec, vec,
        ],
        out_specs=[row, row],
        out_shape=[jax.ShapeDtypeStruct((T, D), F32), jax.ShapeDtypeStruct((T, D), BF16)],
        scratch_shapes=[pltpu.VMEM((tm, D), F32)],
        compiler_params=_cparams(("parallel", "arbitrary")),
    )(xb, x, gate, wg, wu, wd, g.reshape(1, D), b.reshape(1, D))


GID_LANE = N_EXPERTS
MOE_WINDOW = 1024
MOE_CHUNK = 128
MOE_VMEM_LIMIT = 60 * 1024 * 1024


def _router_sorted_body(x_ref, wh_ref, wl_ref, bias_ref, gate_ref, gidt_ref, cnt_ref):
    x_hi, x_lo = _split_bf16(x_ref[...])
    wh = wh_ref[...]
    logits = _dot(x_hi, wh) + _dot(x_lo, wh) + _dot(x_hi, wl_ref[...])
    scores = _sigmoid(logits)
    lane = lax.broadcasted_iota(jnp.int32, scores.shape, 1)
    lanef = lane.astype(F32)
    biased = jnp.where(lane < N_EXPERTS, scores + bias_ref[...], REMOVED)

    def top2(mask):
        v = jnp.where(mask, biased, REMOVED)
        m1 = jnp.max(v, axis=-1, keepdims=True)
        i1 = jnp.min(jnp.where(v == m1, lanef, float(LANES)), axis=-1, keepdims=True)
        v2 = jnp.where(lanef == i1, REMOVED, v)
        m2 = jnp.max(v2, axis=-1, keepdims=True)
        i2 = jnp.min(jnp.where(v2 == m2, lanef, float(LANES)), axis=-1, keepdims=True)
        return m1 + m2, jnp.where((lanef == i1) | (lanef == i2), 1.0, 0.0)

    best, best_sel = top2(lane // EXPERTS_PER_GROUP == 0)
    gid = jnp.zeros_like(best)
    for grp in range(1, N_GROUPS):
        score, sel = top2(lane // EXPERTS_PER_GROUP == grp)
        better = score > best
        best = jnp.where(better, score, best)
        best_sel = jnp.where(better, sel, best_sel)
        gid = jnp.where(better, float(grp), gid)
    w = best_sel * scores
    gate = w / jnp.sum(w, axis=-1, keepdims=True)
    gate_ref[...] = jnp.where(lane == GID_LANE, gid, gate)
    gid_b = jnp.broadcast_to(gid, scores.shape)
    gidt_ref[...] = gid_b.T[:8, :]
    cnt_ref[0] = jnp.broadcast_to(
        jnp.sum(jnp.where(lanef == gid_b, 1.0, 0.0), axis=0, keepdims=True), (8, LANES))


def _router_sorted(x, router_w, router_bias, tm):
    T, D = x.shape
    wpad = jnp.zeros((D, LANES), F32).at[:, :N_EXPERTS].set(router_w)
    wh, wl = _split_bf16(wpad)
    bpad = jnp.zeros((1, LANES), F32).at[0, :N_EXPERTS].set(router_bias)
    return pl.pallas_call(
        _router_sorted_body,
        grid=(T // tm,),
        in_specs=[
            pl.BlockSpec((tm, D), lambda i: (i, 0)),
            pl.BlockSpec((D, LANES), lambda i: (0, 0)),
            pl.BlockSpec((D, LANES), lambda i: (0, 0)),
            pl.BlockSpec((1, LANES), lambda i: (0, 0)),
        ],
        out_specs=[
            pl.BlockSpec((tm, LANES), lambda i: (i, 0)),
            pl.BlockSpec((8, tm), lambda i: (0, i)),
            pl.BlockSpec((1, 8, LANES), lambda i: (i, 0, 0)),
        ],
        out_shape=[
            jax.ShapeDtypeStruct((T, LANES), F32),
            jax.ShapeDtypeStruct((8, T), F32),
            jax.ShapeDtypeStruct((T // tm, 8, LANES), F32),
        ],
        compiler_params=_cparams(("parallel",)),
    )(x, wh, wl, bpad)


def _experts_sorted_body(cnt_ref, xb_ref, x_ref, gate_ref, gidt_ref, ltri_ref, utri_ref, wg_ref, wu_ref,
                         wd_ref, g_ref, b_ref, y_ref, yb_ref, xs_ref, gs_ref, acc_ref, rank_ref,
                         *, W, Wp, chunk):
    win = pl.program_id(0)
    e = pl.program_id(1)
    grp = e // EXPERTS_PER_GROUP
    padded = [((cnt_ref[win * N_GROUPS + g] + chunk - 1) // chunk) * chunk for g in range(N_GROUPS)]
    starts = [0]
    for g in range(N_GROUPS - 1):
        starts.append(starts[-1] + padded[g])
    start = starts[0]
    for g in range(1, N_GROUPS):
        start = jnp.where(grp == g, starts[g], start)
    n_chunks = (cnt_ref[win * N_GROUPS + grp] + chunk - 1) // chunk

    @pl.when(e == 0)
    def _():
        gate = gate_ref[...]
        lane = lax.broadcasted_iota(jnp.int32, gate.shape, 1)
        lanef = lane.astype(F32)
        gid = jnp.sum(jnp.where(lane == GID_LANE, gate, 0.0), axis=-1, keepdims=True)
        member = jnp.where((lanef == gid) & (lane < N_GROUPS), 1.0, 0.0)
        earlier = _dot(ltri_ref[...], member.astype(BF16))
        first = jnp.zeros(gate.shape, F32)
        for g in range(1, N_GROUPS):
            first = jnp.where(lane == g, starts[g].astype(F32), first)
        rank = jnp.sum(member * (first + earlier), axis=-1, keepdims=True)
        rank_ref[...] = jnp.broadcast_to(rank, gate.shape)

        gid_r = gidt_ref[...]
        sub = lax.broadcasted_iota(jnp.int32, gid_r.shape, 0)
        member_r = jnp.where(sub.astype(F32) == gid_r, 1.0, 0.0)
        earlier_r = _dot(member_r.astype(BF16), utri_ref[...])
        first_r = jnp.zeros(gid_r.shape, F32)
        for g in range(1, N_GROUPS):
            first_r = jnp.where(sub == g, starts[g].astype(F32), first_r)
        rank_r = jnp.sum(member_r * (first_r + earlier_r), axis=0, keepdims=True)
        rows = lax.broadcasted_iota(jnp.int32, (Wp, W), 0).astype(F32)
        perm = jnp.where(rows == rank_r, 1.0, 0.0).astype(BF16)
        xs_ref[...] = _dot(perm, xb_ref[...]).astype(BF16)
        g_hi, g_lo = _split_bf16(gate)
        gs_ref[...] = _dot(perm, g_hi) + _dot(perm, g_lo)
        acc_ref[...] = jnp.zeros(acc_ref.shape, F32)

    def one_chunk(i, carry):
        r0 = pl.multiple_of(start + i * chunk, chunk)
        xc = xs_ref[pl.ds(r0, chunk), :]
        a = _dot(xc, wg_ref[0])
        u = _dot(xc, wu_ref[0])
        gs = gs_ref[pl.ds(r0, chunk), :]
        lane = lax.broadcasted_iota(jnp.int32, gs.shape, 1)
        gcol = jnp.sum(jnp.where(lane == e, gs, 0.0), axis=-1, keepdims=True)
        h = a * _sigmoid(a) * u * gcol
        acc_ref[pl.ds(r0, chunk), :] += _dot(h.astype(BF16), wd_ref[0])
        return carry

    lax.fori_loop(0, n_chunks, one_chunk, 0)

    @pl.when(e == N_EXPERTS - 1)
    def _():
        cols = lax.broadcasted_iota(jnp.int32, (W, Wp), 1).astype(F32)
        unperm = jnp.where(cols == rank_ref[...][:, :1], 1.0, 0.0).astype(BF16)
        ffn = _dot(unperm, acc_ref[...].astype(BF16))
        y = _layer_norm(DN_ALPHA * x_ref[...] + ffn, g_ref[...], b_ref[...])
        y_ref[...] = y
        yb_ref[...] = y.astype(BF16)


def _moe_ln(xb, x, router_w, router_bias, wg, wu, wd, g, b):
    T, D = x.shape
    W = min(MOE_WINDOW, T)
    chunk = MOE_CHUNK
    Wp = W + N_GROUPS * chunk
    E, _, DE = wg.shape
    assert T % W == 0 and W % chunk == 0
    gate, gidt, cnt = _router_sorted(x, router_w, router_bias, W)
    counts = cnt[:, 0, :N_GROUPS].astype(jnp.int32).reshape(-1)
    t = np.arange(W)
    ltri = jnp.asarray((t[None, :] < t[:, None]).astype(np.float32), BF16)
    once = pl.Buffered(1)
    row = lambda shape: pl.BlockSpec(shape, lambda i, e, c: (i, 0))
    vec = pl.BlockSpec((1, D), lambda i, e, c: (0, 0))
    tri = pl.BlockSpec((W, W), lambda i, e, c: (0, 0), pipeline_mode=once)
    grid_spec = pltpu.PrefetchScalarGridSpec(
        num_scalar_prefetch=1,
        grid=(T // W, E),
        in_specs=[
            row((W, D)),
            pl.BlockSpec((W, D), lambda i, e, c: (i, 0), pipeline_mode=once),
            row((W, LANES)),
            pl.BlockSpec((8, W), lambda i, e, c: (0, i)),
            tri, tri,
            pl.BlockSpec((1, D, DE), lambda i, e, c: (e, 0, 0)),
            pl.BlockSpec((1, D, DE), lambda i, e, c: (e, 0, 0)),
            pl.BlockSpec((1, DE, D), lambda i, e, c: (e, 0, 0)),
            vec, vec,
        ],
        out_specs=[row((W, D)), row((W, D))],
        scratch_shapes=[
            pltpu.VMEM((Wp, D), BF16),
            pltpu.VMEM((Wp, LANES), F32),
            pltpu.VMEM((Wp, D), F32),
            pltpu.VMEM((W, LANES), F32),
        ],
    )
    return pl.pallas_call(
        functools.partial(_experts_sorted_body, W=W, Wp=Wp, chunk=chunk),
        grid_spec=grid_spec,
        out_shape=[jax.ShapeDtypeStruct((T, D), F32), jax.ShapeDtypeStruct((T, D), BF16)],
        compiler_params=pltpu.CompilerParams(dimension_semantics=("parallel", "arbitrary"),
                                             vmem_limit_bytes=MOE_VMEM_LIMIT),
    )(counts, xb, x, gate, gidt, ltri, ltri.T, wg, wu, wd, g.reshape(1, D), b.reshape(1, D))


def _kt_tiles(t, B, S, n, tk):
    kt = t.reshape(B, S // tk, tk, n, HEAD_DIM)
    return kt.transpose(0, 3, 1, 4, 2)


def _block_onehots(S, block, tk):
    key = np.arange(S).reshape(S // tk, 1, tk)
    r = np.arange(LANES).reshape(1, LANES, 1)
    return jnp.asarray((key // block == r).astype(np.float32), BF16)


def _rope_tiled(S):
    cos, sin = _rope_tables(jnp.arange(S))
    reps = LANES // HALF
    return jnp.tile(cos, (1, reps)), jnp.tile(sin, (1, reps))


def _nsa_mixer(xb, B, S, w_in, cmp_k_w1, cmp_k_w2, cmp_v_w1, cmp_v_w2, cmp_k_pos, cmp_v_pos):
    G, HPG, KV = NSA_KV_GROUPS, NSA_HPG, NSA_KV_DIM
    L, STR, SB = NSA_CMP_LEN, NSA_CMP_STRIDE, NSA_SEL_BLOCK
    assert L == 2 * STR and S % SB == 0 and S // SB <= LANES
    T = B * S
    n_cmp = (S - L) // STR + 1
    NC = S // STR
    n_sel = S // SB
    top_n = min(NSA_SEL_TOPN, n_sel)

    cos2, sin2 = _rope_tiled(S)
    wb = w_in.astype(BF16)
    wcol = lambda i: wb[:, D_MODEL + i * KV: D_MODEL + (i + 1) * KV]
    tn = 2 * KV
    tk_s = min(512, S)
    tq_w = NSA_WINDOW // 2
    proj = _proj(xb, wb[:, :D_MODEL + 2 * KV], cos2, sin2, [2] * (D_MODEL // tn) + [0], S, tn=tn)
    kt_s = _proj_kt(xb, wcol(2), cos2, sin2, B, S, tk_s)
    kt_w = _proj_kt(xb, wcol(4), cos2, sin2, B, S, tq_w)
    vaug = _proj_vaug(xb, jnp.concatenate([wcol(3), wcol(5)], axis=1), tn=tn)
    wg = jnp.zeros((D_MODEL, LANES), BF16).at[:, :3 * N_HEADS].set(wb[:, D_MODEL + 6 * KV:])
    gates = _proj(xb, wg, cos2, sin2, [0], S, out_dtype=F32, tn=LANES).reshape(B, S, LANES)

    col = lambda i: proj[:, D_MODEL + i * KV: D_MODEL + (i + 1) * KV]
    proj3 = proj.reshape(B, S, proj.shape[1])
    vaug3 = vaug.reshape(B, S, vaug.shape[1])

    ccos, csin = _rope_tables(jnp.arange(NC) * STR + (L - 1))
    ccos = jnp.concatenate([ccos, ccos], axis=1)
    csin = jnp.concatenate([csin, csin], axis=1)
    to_rows = lambda t: t.reshape(B, S, G, HEAD_DIM).transpose(0, 2, 1, 3).reshape(B * G, NC, STR * HEAD_DIM)
    kc = _compress(to_rows(col(0)), cmp_k_w1, cmp_k_pos, cmp_k_w2, ccos, csin, True, n_cmp)
    vc = _compress(to_rows(col(1)), cmp_v_w1, cmp_v_pos, cmp_v_w2, ccos, csin, False, n_cmp)
    kct = kc.reshape(B, G, NC, HEAD_DIM).transpose(0, 1, 3, 2)

    ci = np.arange(NC)[:, None]
    sj = np.arange(LANES)[None, :]
    overlap = ((ci * STR < (sj + 1) * SB) & (ci * STR + L > sj * SB) & (ci < n_cmp) & (sj < n_sel))
    overlap = jnp.broadcast_to(jnp.asarray(overlap.astype(np.float32), BF16), (B, G, NC, LANES))
    vo = jnp.concatenate([vc.reshape(B, G, NC, HEAD_DIM), jnp.ones((B, G, NC, 1), BF16),
                          jnp.zeros((B, G, NC, LANES - HEAD_DIM - 1), BF16), overlap], axis=-1)

    o_cmp, selb = _nsa_cmp(proj3, kct, vo, gates, n_cmp, n_sel, top_n)
    o_slc = _nsa_slc(proj3, kt_s, vaug3, 0,
                     _block_onehots(S, SB, tk_s), selb, gates, tq=tk_s, tk=tk_s)
    o_win = _nsa_win(proj3, kt_w, vaug3, G, gates, tq=tq_w)
    return [o.reshape(T, D_MODEL) for o in (o_cmp, o_slc, o_win)]


def _moba_mixer(xb, B, S, w_in):
    H = N_HEADS
    nb = S // MOBA_BLOCK
    top_k = min(MOBA_TOPK, nb)
    cos2, sin2 = _rope_tiled(S)
    tn = 512
    n_t = D_MODEL // tn
    wb = w_in.astype(BF16)
    tk = min(2 * MOBA_BLOCK, S)
    q = _proj(xb, wb[:, :D_MODEL], cos2, sin2, [2] * n_t, S, tn=tn)
    kt = _proj_kt(xb, wb[:, D_MODEL:2 * D_MODEL], cos2, sin2, B, S, tk, tn=tn)
    vaug = _proj_vaug(xb, wb[:, 2 * D_MODEL:], tn=tn)
    e = _block_onehots(S, MOBA_BLOCK, tk)
    kh, kl = _moba_kmean(kt, e.transpose(0, 2, 1))
    o = _moba_attn(q.reshape(B, S, D_MODEL), kt, vaug.reshape(B, S, H * LANES), e, kh, kl, nb, top_k,
                   tq=tk, tk=tk)
    return [o.reshape(B * S, D_MODEL)]


def kernel(x, nsa_w_in, nsa_w_out, nsa_cmp_k_w1, nsa_cmp_k_w2, nsa_cmp_v_w1, nsa_cmp_v_w2, nsa_cmp_k_pos, nsa_cmp_v_pos, moba_w_in, moba_w_out, router_w, router_bias, moe_w_gate, moe_w_up, moe_w_down, ln_g, ln_b):
    B, S, D = x.shape
    xf = x.reshape(B * S, D)
    xb = xf.astype(BF16)
    for layer in range(DEPTH):
        j = layer // 2
        if layer % 2 == 0:
            os_ = _nsa_mixer(xb, B, S, nsa_w_in[j], nsa_cmp_k_w1[j], nsa_cmp_k_w2[j], nsa_cmp_v_w1[j],
                             nsa_cmp_v_w2[j], nsa_cmp_k_pos[j], nsa_cmp_v_pos[j])
            w_out = nsa_w_out[j]
        else:
            os_ = _moba_mixer(xb, B, S, moba_w_in[j])
            w_out = moba_w_out[j]
        xf, xb = _outproj_ln(os_, w_out.astype(BF16), xf, ln_g[layer, 0], ln_b[layer, 0])
        xf, xb = _moe_ln(xb, xf, router_w, router_bias, moe_w_gate[layer].astype(BF16),
                         moe_w_up[layer].astype(BF16), moe_w_down[layer].astype(BF16),
                         ln_g[layer, 1], ln_b[layer, 1])
    return xf.reshape(B, S, D)
```

```python
import functools

import jax
import jax.numpy as jnp
import numpy as np
from jax import lax
from jax.experimental import pallas as pl
from jax.experimental.pallas import tpu as pltpu

F32 = jnp.float32
BF16 = jnp.bfloat16

D_MODEL = 1024
N_HEADS = 16
HEAD_DIM = 64
HALF = HEAD_DIM // 2
ROPE_THETA = 10000.0
DEPTH = 2
DN_ALPHA = (2 * DEPTH) ** 0.25
LN_EPS = 1e-5
NEG = -1e30
FORCE = 1e9
MASK = -1e30
REMOVED = -3.0e38
LANES = 128
Q_SCALE_LOG2 = float(HEAD_DIM ** -0.5 * np.log2(np.e))
ROW_CHUNK = 128

NSA_KV_GROUPS = 4
NSA_HPG = N_HEADS // NSA_KV_GROUPS
NSA_KV_DIM = NSA_KV_GROUPS * HEAD_DIM
NSA_CMP_LEN = 32
NSA_CMP_STRIDE = 16
NSA_SEL_BLOCK = 64
NSA_SEL_TOPN = 16
NSA_WINDOW = 512

MOBA_BLOCK = 256
MOBA_TOPK = 3

N_EXPERTS = 16
N_GROUPS = 4
EXPERTS_PER_GROUP = N_EXPERTS // N_GROUPS
D_EXPERT = 512

VMEM_LIMIT = 48 * 1024 * 1024


def _cparams(sem):
    return pltpu.CompilerParams(dimension_semantics=sem, vmem_limit_bytes=VMEM_LIMIT)


def _dot(a, b):
    return jnp.dot(a, b, preferred_element_type=F32)


def _split_bf16(x):
    hi = x.astype(BF16)
    lo = (x - hi.astype(F32)).astype(BF16)
    return hi, lo


def _sigmoid(x):
    return 1.0 / (1.0 + jnp.exp(-x))


def _proj_body(mode_ref, x_ref, w_ref, cos_ref, sin_ref, o_ref, *, tn):
    j = pl.program_id(0)
    mode = mode_ref[j]
    acc = _dot(x_ref[...], w_ref[...])

    @pl.when(mode == 0)
    def _():
        o_ref[...] = acc.astype(o_ref.dtype)

    @pl.when(mode != 0)
    def _():
        sc = jnp.where(mode == 2, Q_SCALE_LOG2, 1.0).astype(F32)
        cos = cos_ref[...] * sc
        sin = sin_ref[...] * sc
        for c in range(tn // LANES):
            o_ref[:, c * LANES:(c + 1) * LANES] = (
                _rope_chunk(acc[:, c * LANES:(c + 1) * LANES], cos, sin).astype(o_ref.dtype))


def _proj(xb, w, cos2, sin2, modes, seq, out_dtype=BF16, tm=1024, tn=512):
    T, K = xb.shape
    N = w.shape[1]
    tm = min(tm, seq)
    assert T % tm == 0 and N % tn == 0 and seq % tm == 0 and len(modes) == N // tn
    n_pos = seq // tm
    grid_spec = pltpu.PrefetchScalarGridSpec(
        num_scalar_prefetch=1,
        grid=(N // tn, T // tm),
        in_specs=[
            pl.BlockSpec((tm, K), lambda j, i, m: (i, 0)),
            pl.BlockSpec((K, tn), lambda j, i, m: (0, j)),
            pl.BlockSpec((tm, LANES), lambda j, i, m: (i % n_pos, 0)),
            pl.BlockSpec((tm, LANES), lambda j, i, m: (i % n_pos, 0)),
        ],
        out_specs=pl.BlockSpec((tm, tn), lambda j, i, m: (i, j)),
    )
    return pl.pallas_call(
        functools.partial(_proj_body, tn=tn),
        grid_spec=grid_spec,
        out_shape=jax.ShapeDtypeStruct((T, N), out_dtype),
        compiler_params=_cparams(("parallel", "parallel")),
    )(jnp.asarray(modes, jnp.int32), xb, w, cos2, sin2)


def _rope_tables(pos):
    inv = 1.0 / (ROPE_THETA ** (jnp.arange(0, HEAD_DIM, 2, dtype=F32) / HEAD_DIM))
    ang = pos.astype(F32)[:, None] * inv[None, :]
    return jnp.cos(ang), jnp.sin(ang)


def _rope_chunk(a, cos, sin):
    lane = lax.broadcasted_iota(jnp.int32, a.shape, 1)
    up = pltpu.roll(a, LANES - HALF, 1)
    dn = pltpu.roll(a, HALF, 1)
    return a * cos + jnp.where((lane % HEAD_DIM) < HALF, -up, dn) * sin


def _proj_kt_body(x_ref, w_ref, cos_ref, sin_ref, o_ref, *, tn, tk):
    acc = _dot(x_ref[...], w_ref[...])
    cos = cos_ref[...]
    sin = sin_ref[...]
    tm = acc.shape[0]
    for c in range(tn // LANES):
        kt = _rope_chunk(acc[:, c * LANES:(c + 1) * LANES], cos, sin).T
        for hh in range(2):
            for kk in range(tm // tk):
                o_ref[0, 2 * c + hh, kk] = (
                    kt[hh * HEAD_DIM:(hh + 1) * HEAD_DIM, kk * tk:(kk + 1) * tk].astype(o_ref.dtype))


def _proj_kt(xb, w, cos2, sin2, B, S, tk, tm=1024, tn=256):
    T, K = xb.shape
    N = w.shape[1]
    tm = min(tm, S)
    tn = min(tn, N)
    assert S % tm == 0 and tm % tk == 0 and N % tn == 0
    n_pos = S // tm
    return pl.pallas_call(
        functools.partial(_proj_kt_body, tn=tn, tk=tk),
        grid=(N // tn, T // tm),
        in_specs=[
            pl.BlockSpec((tm, K), lambda j, i: (i, 0)),
            pl.BlockSpec((K, tn), lambda j, i: (0, j)),
            pl.BlockSpec((tm, LANES), lambda j, i: (i % n_pos, 0)),
            pl.BlockSpec((tm, LANES), lambda j, i: (i % n_pos, 0)),
        ],
        out_specs=pl.BlockSpec((1, tn // HEAD_DIM, tm // tk, HEAD_DIM, tk),
                               lambda j, i: (i // n_pos, j, i % n_pos, 0, 0)),
        out_shape=jax.ShapeDtypeStruct((B, N // HEAD_DIM, S // tk, HEAD_DIM, tk), BF16),
        compiler_params=_cparams(("parallel", "parallel")),
    )(xb, w, cos2, sin2)


def _proj_vaug_body(x_ref, w_ref, o_ref, *, tn):
    acc = _dot(x_ref[...], w_ref[...])
    lane = lax.broadcasted_iota(jnp.int32, (acc.shape[0], LANES), 1)
    tail = jnp.where(lane == HEAD_DIM, 1.0, 0.0)
    for c in range(tn // LANES):
        a = acc[:, c * LANES:(c + 1) * LANES]
        for k, head in enumerate((a, pltpu.roll(a, HEAD_DIM, 1))):
            o_ref[:, (2 * c + k) * LANES:(2 * c + k + 1) * LANES] = (
                jnp.where(lane < HEAD_DIM, head, tail).astype(o_ref.dtype))


def _proj_vaug(xb, w, tm=1024, tn=512):
    T, K = xb.shape
    N = w.shape[1]
    tm = min(tm, T)
    assert T % tm == 0 and N % tn == 0
    return pl.pallas_call(
        functools.partial(_proj_vaug_body, tn=tn),
        grid=(N // tn, T // tm),
        in_specs=[pl.BlockSpec((tm, K), lambda j, i: (i, 0)), pl.BlockSpec((K, tn), lambda j, i: (0, j))],
        out_specs=pl.BlockSpec((tm, 2 * tn), lambda j, i: (i, j)),
        out_shape=jax.ShapeDtypeStruct((T, 2 * N), BF16),
        compiler_params=_cparams(("parallel", "parallel")),
    )(xb, w)


def _gelu_tanh(x):
    c = np.float32(np.sqrt(2.0 / np.pi))
    return 0.5 * x * (1.0 + jnp.tanh(c * (x + 0.044715 * (x * x * x))))


def _compress_body(r_ref, w1_ref, pos_ref, w2_ref, w2r_ref, cos_ref, sin_ref, o_ref, *, rope, n_cmp):
    r = r_ref[0]
    nc = r.shape[0]
    half = NSA_CMP_STRIDE * HEAD_DIM
    a = _dot(r, w1_ref[0])
    b = _dot(r, w1_ref[1])
    pos = pos_ref[...]
    pb = _dot(pos[:, :half], w1_ref[0]) + _dot(pos[:, half:], w1_ref[1])
    b_next = pltpu.roll(b, nc - 1, 0)
    h = _gelu_tanh(a + b_next + pb[0:1, :]).astype(BF16)
    o = _dot(h, w2_ref[...])
    if rope:
        o = o * cos_ref[...] + _dot(h, w2r_ref[...]) * sin_ref[...]
    row = lax.broadcasted_iota(jnp.int32, o.shape, 0)
    o_ref[0] = jnp.where(row < n_cmp, o, 0.0).astype(o_ref.dtype)


def _compress(r, w1, pos, w2, cos_c, sin_c, rope, n_cmp):
    BG, NC, K = r.shape
    hidden = w1.shape[1]
    w1s = w1.astype(BF16).reshape(2, K, hidden)
    pos8 = jnp.zeros((8, 2 * K), BF16).at[0].set(pos.reshape(-1).astype(BF16))
    w2r = jnp.concatenate([-w2[:, HALF:], w2[:, :HALF]], axis=1).astype(BF16)
    full = lambda shape: pl.BlockSpec(shape, lambda i: (0,) * len(shape))
    return pl.pallas_call(
        functools.partial(_compress_body, rope=rope, n_cmp=n_cmp),
        grid=(BG,),
        in_specs=[
            pl.BlockSpec((1, NC, K), lambda i: (i, 0, 0)),
            full((2, K, hidden)),
            full((8, 2 * K)),
            full((hidden, HEAD_DIM)),
            full((hidden, HEAD_DIM)),
            full((NC, HEAD_DIM)),
            full((NC, HEAD_DIM)),
        ],
        out_specs=pl.BlockSpec((1, NC, HEAD_DIM), lambda i: (i, 0, 0)),
        out_shape=jax.ShapeDtypeStruct((BG, NC, HEAD_DIM), BF16),
        compiler_params=_cparams(("parallel",)),
    )(r, w1s, pos8, w2.astype(BF16), w2r, cos_c, sin_c)


def _topk_mask_t(v, k):
    idx = lax.broadcasted_iota(jnp.int32, v.shape, 0).astype(F32)

    def step(_, cur):
        m = jnp.max(cur, axis=0, keepdims=True)
        first = jnp.min(jnp.where(cur == m, idx, float(LANES)), axis=0, keepdims=True)
        return jnp.where(idx == first, REMOVED, cur)

    return lax.fori_loop(0, k, step, v, unroll=True) != v


def _head_kt(e, kt, parity):
    z = jnp.zeros_like(kt)
    parts = ([] if e is None else [e]) + ([kt, z] if parity == 0 else [z, kt])
    return jnp.concatenate(parts, axis=0)


def _flash_init(m_ref, acc_ref):
    m_ref[...] = jnp.full(m_ref.shape, -jnp.inf, F32)
    acc_ref[...] = jnp.zeros(acc_ref.shape, F32)


def _flash_step(jobs, s_ref, m_ref, alpha_ref, acc_ref):
    for r0, q, kt, _, mask in jobs:
        rows = q.shape[0]
        s = _dot(q, kt)
        if mask is not None:
            s = jnp.where(mask, s, MASK)
        s_ref[r0:r0 + rows, :s.shape[1]] = s
        cm = s[:, :LANES]
        for c in range(1, s.shape[1] // LANES):
            cm = jnp.maximum(cm, s[:, c * LANES:(c + 1) * LANES])
        m_prev = m_ref[r0:r0 + rows, :]
        m_new = jnp.maximum(m_prev, jnp.max(cm, axis=-1, keepdims=True))
        alpha_ref[r0:r0 + rows, :] = jnp.exp2(m_prev - m_new)
        m_ref[r0:r0 + rows, :] = m_new
    for r0, q, _, v, _ in jobs:
        for r in range(r0, r0 + q.shape[0], ROW_CHUNK):
            rows = min(ROW_CHUNK, r0 + q.shape[0] - r)
            m = m_ref[r:r + rows, :]
            p = jnp.concatenate(
                [jnp.exp2(s_ref[r:r + rows, c * LANES:(c + 1) * LANES] - m)
                 for c in range(v.shape[0] // LANES)], axis=1).astype(BF16)
            acc_ref[r:r + rows, :] = alpha_ref[r:r + rows, :] * acc_ref[r:r + rows, :] + _dot(p, v)


def _diagonal_jobs(r0, q, kt, v, tq, tk):
    half = tq // 2
    row_a, col_a = _tile_iotas(half, half)
    row_b, col_b = _tile_iotas(half, tk)
    return [(r0, q[:half], kt[:, :half], v[:half], col_a <= row_a),
            (r0 + half, q[half:], kt, v, col_b <= half + row_b)]


def _flash_scratch(rows, tk):
    return [pltpu.VMEM((rows, tk), F32), pltpu.VMEM((rows, LANES), F32),
            pltpu.VMEM((rows, LANES), F32), pltpu.VMEM((rows, LANES), F32)]


def _pair_merge(even, odd):
    lane = lax.broadcasted_iota(jnp.int32, even.shape, 1)
    return jnp.where(lane < HEAD_DIM, even, pltpu.roll(odd, HEAD_DIM, 1))


def _flash_pairs_out(acc_ref, tq, n_heads, gates):
    def head(h):
        acc = acc_ref[h * tq:(h + 1) * tq, :]
        return acc * (gates[h] / acc[:, HEAD_DIM:HEAD_DIM + 1])
    return jnp.concatenate([_pair_merge(head(h), head(h + 1)) for h in range(0, n_heads, 2)], axis=1)


def _gate_col(logits, branch, h):
    idx = branch * N_HEADS + pl.program_id(1) * NSA_HPG + h
    lane = lax.broadcasted_iota(jnp.int32, logits.shape, 1)
    return _sigmoid(jnp.sum(jnp.where(lane == idx, logits, 0.0), axis=-1, keepdims=True))


def _tile_iotas(rows, tk):
    return (lax.broadcasted_iota(jnp.int32, (rows, tk), 0),
            lax.broadcasted_iota(jnp.int32, (rows, tk), 1))


def _nsa_cmp_body(q_ref, kct_ref, vo_ref, g_ref, o_ref, selb_ref, s_ref, m_ref, imp_ref,
                  *, tq, n_cmp, n_sel, top_n):
    s0 = pl.program_id(2) * tq
    q4 = q_ref[0]
    n_chunks = vo_ref.shape[2] // LANES
    rc = min(ROW_CHUNK, tq)

    def attend(nv):
        nc = nv * LANES
        kct = kct_ref[0, 0, :, :nc]
        vo = vo_ref[0, 0, :nc, :]
        tpos = s0 + lax.broadcasted_iota(jnp.int32, (tq, nc), 0)
        nidx = lax.broadcasted_iota(jnp.int32, (tq, nc), 1)
        cmask = (nidx * NSA_CMP_STRIDE + (NSA_CMP_LEN - 1) <= tpos) & (nidx < n_cmp)
        for h in range(NSA_HPG):
            pair = q4[:, (h // 2) * LANES:(h // 2 + 1) * LANES]
            s = jnp.where(cmask, _dot(pair, _head_kt(None, kct, h % 2)), NEG)
            s_ref[h * tq:(h + 1) * tq, :nc] = s
            cm = s[:, :LANES]
            for c in range(1, nv):
                cm = jnp.maximum(cm, s[:, c * LANES:(c + 1) * LANES])
            m_ref[h * tq:(h + 1) * tq, :] = jnp.broadcast_to(jnp.max(cm, axis=-1, keepdims=True), (tq, LANES))

        for r0 in range(0, tq, rc):
            imp = jnp.zeros((rc, LANES), F32)
            heads = []
            for h in range(NSA_HPG):
                r = h * tq + r0
                m = m_ref[r:r + rc, :]
                e = jnp.concatenate(
                    [jnp.exp2(s_ref[r:r + rc, c * LANES:(c + 1) * LANES] - m) for c in range(nv)],
                    axis=1).astype(BF16)
                res = _dot(e, vo)
                inv = jnp.where(m[:, :1] > 0.5 * NEG, 1.0 / res[:, HEAD_DIM:HEAD_DIM + 1], 0.0)
                gate = _gate_col(g_ref[0, r0:r0 + rc, :], 0, h)
                heads.append(res[:, :LANES] * (inv * gate))
                imp = imp + res[:, LANES:] * inv
            o_ref[0, r0:r0 + rc, :] = jnp.concatenate(
                [_pair_merge(heads[h], heads[h + 1]) for h in range(0, NSA_HPG, 2)], axis=1).astype(o_ref.dtype)
            imp_ref[r0:r0 + rc, :] = imp

    last_end = jnp.maximum(s0 + tq - NSA_CMP_LEN, 0)
    n_vis = jnp.minimum(last_end // (NSA_CMP_STRIDE * LANES) + 1, n_chunks)
    for nv in range(1, n_chunks + 1):
        pl.when(n_vis == nv)(functools.partial(attend, nv))
    imp = imp_ref[...]

    blk = lax.broadcasted_iota(jnp.int32, imp.shape, 1)
    jq = (s0 + lax.broadcasted_iota(jnp.int32, imp.shape, 0)) // NSA_SEL_BLOCK
    forced = (blk == 0) | (blk == jq) | (blk == jq - 1)
    imp = jnp.where(blk > jq, NEG, jnp.where(forced, FORCE, imp))
    imp = jnp.where(blk < n_sel, imp, REMOVED)
    sel_t = _topk_mask_t(imp.T, top_n)
    selb_ref[0, 0] = jnp.where(sel_t, 0.0, MASK).T.astype(selb_ref.dtype)


def _nsa_cmp(proj3, kct, vo, g_cmp, n_cmp, n_sel, top_n, tq=256):
    B, S, _ = proj3.shape
    G, HPG = NSA_KV_GROUPS, NSA_HPG
    NC = vo.shape[2]
    tq = min(tq, S)
    assert NC % LANES == 0
    return pl.pallas_call(
        functools.partial(_nsa_cmp_body, tq=tq, n_cmp=n_cmp, n_sel=n_sel, top_n=top_n),
        grid=(B, G, S // tq),
        in_specs=[
            pl.BlockSpec((1, tq, HPG * HEAD_DIM), lambda b, g, i: (b, i, g)),
            pl.BlockSpec((1, 1, HEAD_DIM, NC), lambda b, g, i: (b, g, 0, 0)),
            pl.BlockSpec((1, 1, NC, 2 * LANES), lambda b, g, i: (b, g, 0, 0)),
            pl.BlockSpec((1, tq, LANES), lambda b, g, i: (b, i, 0)),
        ],
        out_specs=[
            pl.BlockSpec((1, tq, HPG * HEAD_DIM), lambda b, g, i: (b, i, g)),
            pl.BlockSpec((1, 1, tq, LANES), lambda b, g, i: (b, g, i, 0)),
        ],
        out_shape=[
            jax.ShapeDtypeStruct((B, S, D_MODEL), BF16),
            jax.ShapeDtypeStruct((B, G, S, LANES), BF16),
        ],
        scratch_shapes=[pltpu.VMEM((HPG * tq, NC), F32), pltpu.VMEM((HPG * tq, LANES), F32),
                        pltpu.VMEM((tq, LANES), F32)],
        compiler_params=_cparams(("parallel", "parallel", "parallel")),
    )(proj3, kct, vo, g_cmp)


def _nsa_slc_body(q_ref, kt_ref, v_ref, e_ref, selb_ref, g_ref, o_ref, qa_ref, s_ref, m_ref, alpha_ref,
                  acc_ref, *, tq, tk):
    s0 = pl.program_id(2) * tq
    _flash_init(m_ref, acc_ref)
    selb = selb_ref[0, 0]
    q4 = q_ref[0]
    for h in range(NSA_HPG):
        qa_ref[h * tq:(h + 1) * tq, :] = jnp.concatenate(
            [selb, q4[:, (h // 2) * LANES:(h // 2 + 1) * LANES]], axis=1)

    def tile(ki, causal):
        k0 = pl.multiple_of(ki * tk, tk)
        kts = [_head_kt(e_ref[ki], kt_ref[0, 0, ki], parity) for parity in range(2)]
        v = v_ref[0, pl.ds(k0, tk), :]
        jobs = []
        for h in range(NSA_HPG):
            q = qa_ref[h * tq:(h + 1) * tq, :]
            jobs += _diagonal_jobs(h * tq, q, kts[h % 2], v, tq, tk) if causal else [(h * tq, q, kts[h % 2], v, None)]
        _flash_step(jobs, s_ref, m_ref, alpha_ref, acc_ref)

    last = (s0 + tq - 1) // tk

    def full_tile(ki, carry):
        tile(ki, False)
        return carry

    lax.fori_loop(0, last, full_tile, 0)
    tile(last, True)
    gates = [_gate_col(g_ref[0], 1, h) for h in range(NSA_HPG)]
    o_ref[0] = _flash_pairs_out(acc_ref, tq, NSA_HPG, gates).astype(o_ref.dtype)


def _nsa_slc(proj3, kt, vaug3, v_block, e, selb, g, tq, tk):
    B, S, _ = proj3.shape
    G, HPG = NSA_KV_GROUPS, NSA_HPG
    nk = S // tk
    assert tk == tq and kt.shape == (B, G, nk, HEAD_DIM, tk) and e.shape == (nk, LANES, tk)
    once = pl.Buffered(1)
    return pl.pallas_call(
        functools.partial(_nsa_slc_body, tq=tq, tk=tk),
        grid=(B, G, S // tq),
        in_specs=[
            pl.BlockSpec((1, tq, HPG * HEAD_DIM), lambda b, g_, i: (b, i, g_)),
            pl.BlockSpec((1, 1, nk, HEAD_DIM, tk), lambda b, g_, i: (b, g_, 0, 0, 0), pipeline_mode=once),
            pl.BlockSpec((1, S, LANES), lambda b, g_, i: (b, 0, v_block + g_), pipeline_mode=once),
            pl.BlockSpec((nk, LANES, tk), lambda b, g_, i: (0, 0, 0), pipeline_mode=once),
            pl.BlockSpec((1, 1, tq, LANES), lambda b, g_, i: (b, g_, i, 0)),
            pl.BlockSpec((1, tq, LANES), lambda b, g_, i: (b, i, 0)),
        ],
        out_specs=pl.BlockSpec((1, tq, HPG * HEAD_DIM), lambda b, g_, i: (b, i, g_)),
        out_shape=jax.ShapeDtypeStruct((B, S, D_MODEL), BF16),
        scratch_shapes=[pltpu.VMEM((HPG * tq, 2 * LANES), BF16)] + _flash_scratch(HPG * tq, tk),
        compiler_params=_cparams(("parallel", "parallel", "parallel")),
    )(proj3, kt, vaug3, e, selb, g)


def _nsa_win_body(q_ref, kt_ref, v_ref, g_ref, o_ref, s_ref, m_ref, *, tq, n_tiles):
    s0 = pl.program_id(2) * tq
    tk = n_tiles * tq
    k_first = jnp.maximum(pl.program_id(2) - (n_tiles - 1), 0)
    k0 = pl.multiple_of(k_first * tq, tq)
    kt = jnp.concatenate([kt_ref[0, 0, k_first + j] for j in range(n_tiles)], axis=1)
    v = v_ref[0, pl.ds(k0, tk), :]
    row, col = _tile_iotas(tq, tk)
    dist = (s0 - k0) + row - col
    valid = (dist >= 0) & (dist < NSA_WINDOW)
    q4 = q_ref[0]
    for h in range(NSA_HPG):
        pair = q4[:, (h // 2) * LANES:(h // 2 + 1) * LANES]
        s = jnp.where(valid, _dot(pair, _head_kt(None, kt, h % 2)), MASK)
        s_ref[h * tq:(h + 1) * tq, :] = s
        cm = s[:, :LANES]
        for c in range(1, tk // LANES):
            cm = jnp.maximum(cm, s[:, c * LANES:(c + 1) * LANES])
        m_ref[h * tq:(h + 1) * tq, :] = jnp.broadcast_to(jnp.max(cm, axis=-1, keepdims=True), (tq, LANES))

    rc = min(ROW_CHUNK, tq)
    for r0 in range(0, tq, rc):
        heads = []
        for h in range(NSA_HPG):
            r = h * tq + r0
            m = m_ref[r:r + rc, :]
            p = jnp.concatenate(
                [jnp.exp2(s_ref[r:r + rc, c * LANES:(c + 1) * LANES] - m) for c in range(tk // LANES)],
                axis=1).astype(BF16)
            res = _dot(p, v)
            heads.append(res * (_gate_col(g_ref[0, r0:r0 + rc, :], 2, h) / res[:, HEAD_DIM:HEAD_DIM + 1]))
        o_ref[0, r0:r0 + rc, :] = jnp.concatenate(
            [_pair_merge(heads[h], heads[h + 1]) for h in range(0, NSA_HPG, 2)], axis=1).astype(o_ref.dtype)


def _nsa_win(proj3, kt, vaug3, v_block, g, tq):
    B, S, _ = proj3.shape
    G, HPG = NSA_KV_GROUPS, NSA_HPG
    nk = S // tq
    n_tiles = NSA_WINDOW // tq + 1
    assert NSA_WINDOW % tq == 0 and nk >= n_tiles and kt.shape == (B, G, nk, HEAD_DIM, tq)
    once = pl.Buffered(1)
    return pl.pallas_call(
        functools.partial(_nsa_win_body, tq=tq, n_tiles=n_tiles),
        grid=(B, G, S // tq),
        in_specs=[
            pl.BlockSpec((1, tq, HPG * HEAD_DIM), lambda b, g_, i: (b, i, g_)),
            pl.BlockSpec((1, 1, nk, HEAD_DIM, tq), lambda b, g_, i: (b, g_, 0, 0, 0), pipeline_mode=once),
            pl.BlockSpec((1, S, LANES), lambda b, g_, i: (b, 0, v_block + g_), pipeline_mode=once),
            pl.BlockSpec((1, tq, LANES), lambda b, g_, i: (b, i, 0)),
        ],
        out_specs=pl.BlockSpec((1, tq, HPG * HEAD_DIM), lambda b, g_, i: (b, i, g_)),
        out_shape=jax.ShapeDtypeStruct((B, S, D_MODEL), BF16),
        scratch_shapes=[pltpu.VMEM((HPG * tq, n_tiles * tq), F32), pltpu.VMEM((HPG * tq, LANES), F32)],
        compiler_params=_cparams(("parallel", "parallel", "parallel")),
    )(proj3, kt, vaug3, g)


def _moba_kmean_body(kt_ref, et_ref, hi_ref, lo_ref):
    nk = kt_ref.shape[2]
    km = jnp.zeros((HEAD_DIM, LANES), F32)
    for ki in range(nk):
        km = km + _dot(kt_ref[0, 0, ki], et_ref[ki])
    hi, lo = _split_bf16(km * (1.0 / MOBA_BLOCK))
    hi_ref[0, 0] = hi
    lo_ref[0, 0] = lo


def _moba_kmean(kt, et):
    B, H, nk, _, tk = kt.shape
    spec = pl.BlockSpec((1, 1, HEAD_DIM, LANES), lambda b, h: (b, h, 0, 0))
    return pl.pallas_call(
        _moba_kmean_body,
        grid=(B, H),
        in_specs=[
            pl.BlockSpec((1, 1, nk, HEAD_DIM, tk), lambda b, h: (b, h, 0, 0, 0)),
            pl.BlockSpec((nk, tk, LANES), lambda b, h: (0, 0, 0)),
        ],
        out_specs=[spec, spec],
        out_shape=[jax.ShapeDtypeStruct((B, H, HEAD_DIM, LANES), BF16)] * 2,
        compiler_params=_cparams(("parallel", "parallel")),
    )(kt, et)


def _moba_body(q_ref, kt_ref, v_ref, e_ref, kh_ref, kl_ref, o_ref, qa_ref, s_ref, m_ref, alpha_ref,
               acc_ref, *, tq, tk, hb, nb, top_k):
    s0 = pl.program_id(2) * tq
    _flash_init(m_ref, acc_ref)

    blk = lax.broadcasted_iota(jnp.int32, (LANES, tq), 0)
    cb = (s0 + lax.broadcasted_iota(jnp.int32, (LANES, tq), 1)) // MOBA_BLOCK
    qs = q_ref[0]
    for h in range(hb):
        pair = qs[:, (h // 2) * LANES:(h // 2 + 1) * LANES]
        gsc = (_dot(pair, _head_kt(None, kh_ref[0, h], h % 2))
               + _dot(pair, _head_kt(None, kl_ref[0, h], h % 2))).T
        gsc = jnp.where(blk < cb, gsc, NEG)
        gsc = jnp.where(blk < nb, gsc, REMOVED)
        sel = (_topk_mask_t(gsc, top_k) & (blk < cb)) | (blk == cb)
        qa_ref[h * tq:(h + 1) * tq, :] = jnp.concatenate(
            [jnp.where(sel, 0.0, MASK).T.astype(BF16), pair], axis=1)

    def tile(ki, causal):
        k0 = pl.multiple_of(ki * tk, tk)
        e = e_ref[ki]
        jobs = []
        for h in range(hb):
            q = qa_ref[h * tq:(h + 1) * tq, :]
            kt = _head_kt(e, kt_ref[0, h, ki], h % 2)
            v = v_ref[0, pl.ds(k0, tk), h * LANES:(h + 1) * LANES]
            jobs += _diagonal_jobs(h * tq, q, kt, v, tq, tk) if causal else [(h * tq, q, kt, v, None)]
        _flash_step(jobs, s_ref, m_ref, alpha_ref, acc_ref)

    last = (s0 + tq - 1) // tk

    def full_tile(ki, carry):
        tile(ki, False)
        return carry

    lax.fori_loop(0, last, full_tile, 0)
    tile(last, True)
    o_ref[0] = _flash_pairs_out(acc_ref, tq, hb, [1.0] * hb).astype(o_ref.dtype)


def _moba_attn(proj3, kt, vaug3, e, kh, kl, nb, top_k, tq, tk, hb=4):
    B, S, _ = proj3.shape
    H = N_HEADS
    nk = S // tk
    assert tk == tq and tk % MOBA_BLOCK == 0 and H % hb == 0 and hb % 2 == 0 and nb <= LANES
    once = pl.Buffered(1)
    return pl.pallas_call(
        functools.partial(_moba_body, tq=tq, tk=tk, hb=hb, nb=nb, top_k=top_k),
        grid=(B, H // hb, S // tq),
        in_specs=[
            pl.BlockSpec((1, tq, hb * HEAD_DIM), lambda b, h, i: (b, i, h)),
            pl.BlockSpec((1, hb, nk, HEAD_DIM, tk), lambda b, h, i: (b, h, 0, 0, 0), pipeline_mode=once),
            pl.BlockSpec((1, S, hb * LANES), lambda b, h, i: (b, 0, h), pipeline_mode=once),
            pl.BlockSpec((nk, LANES, tk), lambda b, h, i: (0, 0, 0), pipeline_mode=once),
            pl.BlockSpec((1, hb, HEAD_DIM, LANES), lambda b, h, i: (b, h, 0, 0), pipeline_mode=once),
            pl.BlockSpec((1, hb, HEAD_DIM, LANES), lambda b, h, i: (b, h, 0, 0), pipeline_mode=once),
        ],
        out_specs=pl.BlockSpec((1, tq, hb * HEAD_DIM), lambda b, h, i: (b, i, h)),
        out_shape=jax.ShapeDtypeStruct((B, S, D_MODEL), BF16),
        scratch_shapes=[pltpu.VMEM((hb * tq, 2 * LANES), BF16)] + _flash_scratch(hb * tq, tk),
        compiler_params=_cparams(("parallel", "parallel", "parallel")),
    )(proj3, kt, vaug3, e, kh, kl)


def _layer_norm(r, g, b):
    mu = jnp.mean(r, axis=-1, keepdims=True)
    c = r - mu
    var = jnp.mean(c * c, axis=-1, keepdims=True)
    return c * lax.rsqrt(var + LN_EPS) * g + b


def _outproj_body(*refs, n_o):
    o_refs = refs[:n_o]
    w_ref, x_ref, g_ref, b_ref, y_ref, yb_ref = refs[n_o:]
    o = o_refs[0][...].astype(F32)
    for r in o_refs[1:]:
        o = o + r[...].astype(F32)
    mix = _dot(o.astype(BF16), w_ref[...])
    y = _layer_norm(DN_ALPHA * x_ref[...] + mix, g_ref[...], b_ref[...])
    y_ref[...] = y
    yb_ref[...] = y.astype(BF16)


def _outproj_ln(os_, w, x, g, b, tm=512):
    T, D = x.shape
    tm = min(tm, T)
    n_o = len(os_)
    row = pl.BlockSpec((tm, D), lambda i: (i, 0))
    vec = pl.BlockSpec((1, D), lambda i: (0, 0))
    return pl.pallas_call(
        functools.partial(_outproj_body, n_o=n_o),
        grid=(T // tm,),
        in_specs=[row] * n_o + [pl.BlockSpec((D, D), lambda i: (0, 0)), row, vec, vec],
        out_specs=[row, row],
        out_shape=[jax.ShapeDtypeStruct((T, D), F32), jax.ShapeDtypeStruct((T, D), BF16)],
        compiler_params=_cparams(("parallel",)),
    )(*os_, w, x, g.reshape(1, D), b.reshape(1, D))


def _router_body(x_ref, wh_ref, wl_ref, bias_ref, gate_ref):
    x_hi, x_lo = _split_bf16(x_ref[...])
    wh = wh_ref[...]
    logits = _dot(x_hi, wh) + _dot(x_lo, wh) + _dot(x_hi, wl_ref[...])
    scores = _sigmoid(logits)
    lane = lax.broadcasted_iota(jnp.int32, scores.shape, 1)
    lanef = lane.astype(F32)
    live = lane < N_EXPERTS
    biased = jnp.where(live, scores + bias_ref[...], REMOVED)

    def top2(mask):
        v = jnp.where(mask, biased, REMOVED)
        m1 = jnp.max(v, axis=-1, keepdims=True)
        i1 = jnp.min(jnp.where(v == m1, lanef, float(LANES)), axis=-1, keepdims=True)
        v2 = jnp.where(lanef == i1, REMOVED, v)
        m2 = jnp.max(v2, axis=-1, keepdims=True)
        i2 = jnp.min(jnp.where(v2 == m2, lanef, float(LANES)), axis=-1, keepdims=True)
        return m1 + m2, jnp.where((lanef == i1) | (lanef == i2), 1.0, 0.0)

    best, best_sel = top2(lane // EXPERTS_PER_GROUP == 0)
    for grp in range(1, N_GROUPS):
        score, sel = top2(lane // EXPERTS_PER_GROUP == grp)
        better = score > best
        best = jnp.where(better, score, best)
        best_sel = jnp.where(better, sel, best_sel)
    w = best_sel * scores
    gate_ref[...] = w / jnp.sum(w, axis=-1, keepdims=True)


def _router(x, router_w, router_bias, tm=512):
    T, D = x.shape
    tm = min(tm, T)
    wpad = jnp.zeros((D, LANES), F32).at[:, :N_EXPERTS].set(router_w)
    wh, wl = _split_bf16(wpad)
    bpad = jnp.zeros((1, LANES), F32).at[0, :N_EXPERTS].set(router_bias)
    return pl.pallas_call(
        _router_body,
        grid=(T // tm,),
        in_specs=[
            pl.BlockSpec((tm, D), lambda i: (i, 0)),
            pl.BlockSpec((D, LANES), lambda i: (0, 0)),
            pl.BlockSpec((D, LANES), lambda i: (0, 0)),
            pl.BlockSpec((1, LANES), lambda i: (0, 0)),
        ],
        out_specs=pl.BlockSpec((tm, LANES), lambda i: (i, 0)),
        out_shape=jax.ShapeDtypeStruct((T, LANES), F32),
        compiler_params=_cparams(("parallel",)),
    )(x, wh, wl, bpad)


def _experts_body(xb_ref, x_ref, gate_ref, wg_ref, wu_ref, wd_ref, g_ref, b_ref, y_ref, yb_ref, acc_ref):
    e = pl.program_id(1)

    @pl.when(e == 0)
    def _():
        acc_ref[...] = jnp.zeros(acc_ref.shape, F32)

    xb = xb_ref[...]
    a = _dot(xb, wg_ref[0])
    u = _dot(xb, wu_ref[0])
    lane = lax.broadcasted_iota(jnp.int32, gate_ref.shape, 1)
    gcol = jnp.sum(jnp.where(lane == e, gate_ref[...], 0.0), axis=-1, keepdims=True)
    h = a * _sigmoid(a) * u * gcol
    acc_ref[...] += _dot(h.astype(BF16), wd_ref[0])

    @pl.when(e == N_EXPERTS - 1)
    def _():
        y = _layer_norm(DN_ALPHA * x_ref[...] + acc_ref[...], g_ref[...], b_ref[...])
        y_ref[...] = y
        yb_ref[...] = y.astype(BF16)


def _experts_ln(xb, x, gate, wg, wu, wd, g, b, tm=1024):
    T, D = x.shape
    tm = min(tm, T)
    E, _, DE = wg.shape
    row = pl.BlockSpec((tm, D), lambda i, e: (i, 0))
    vec = pl.BlockSpec((1, D), lambda i, e: (0, 0))
    return pl.pallas_call(
        _experts_body,
        grid=(T // tm, E),
        in_specs=[
            row, row,
            pl.BlockSpec((tm, LANES), lambda i, e: (i, 0)),
            pl.BlockSpec((1, D, DE), lambda i, e: (e, 0, 0)),
            pl.BlockSpec((1, D, DE), lambda i, e: (e, 0, 0)),
            pl.BlockSpec((1, DE, D), lambda i, e: (e, 0, 0)),
            vec, vec,
        ],
        out_specs=[row, row],
        out_shape=[jax.ShapeDtypeStruct((T, D), F32), jax.ShapeDtypeStruct((T, D), BF16)],
        scratch_shapes=[pltpu.VMEM((tm, D), F32)],
        compiler_params=_cparams(("parallel", "arbitrary")),
    )(xb, x, gate, wg, wu, wd, g.reshape(1, D), b.reshape(1, D))


GID_LANE = N_EXPERTS
MOE_WINDOW = 1024
MOE_CHUNK = 128
MOE_VMEM_LIMIT = 60 * 1024 * 1024


def _router_sorted_body(x_ref, wh_ref, wl_ref, bias_ref, gate_ref, gidt_ref, cnt_ref):
    x_hi, x_lo = _split_bf16(x_ref[...])
    wh = wh_ref[...]
    logits = _dot(x_hi, wh) + _dot(x_lo, wh) + _dot(x_hi, wl_ref[...])
    scores = _sigmoid(logits)
    lane = lax.broadcasted_iota(jnp.int32, scores.shape, 1)
    lanef = lane.astype(F32)
    biased = jnp.where(lane < N_EXPERTS, scores + bias_ref[...], REMOVED)

    def top2(mask):
        v = jnp.where(mask, biased, REMOVED)
        m1 = jnp.max(v, axis=-1, keepdims=True)
        i1 = jnp.min(jnp.where(v == m1, lanef, float(LANES)), axis=-1, keepdims=True)
        v2 = jnp.where(lanef == i1, REMOVED, v)
        m2 = jnp.max(v2, axis=-1, keepdims=True)
        i2 = jnp.min(jnp.where(v2 == m2, lanef, float(LANES)), axis=-1, keepdims=True)
        return m1 + m2, jnp.where((lanef == i1) | (lanef == i2), 1.0, 0.0)

    best, best_sel = top2(lane // EXPERTS_PER_GROUP == 0)
    gid = jnp.zeros_like(best)
    for grp in range(1, N_GROUPS):
        score, sel = top2(lane // EXPERTS_PER_GROUP == grp)
        better = score > best
        best = jnp.where(better, score, best)
        best_sel = jnp.where(better, sel, best_sel)
        gid = jnp.where(better, float(grp), gid)
    w = best_sel * scores
    gate = w / jnp.sum(w, axis=-1, keepdims=True)
    gate_ref[...] = jnp.where(lane == GID_LANE, gid, gate)
    gid_b = jnp.broadcast_to(gid, scores.shape)
    gidt_ref[...] = gid_b.T[:8, :]
    cnt_ref[0] = jnp.broadcast_to(
        jnp.sum(jnp.where(lanef == gid_b, 1.0, 0.0), axis=0, keepdims=True), (8, LANES))


def _router_sorted(x, router_w, router_bias, tm):
    T, D = x.shape
    wpad = jnp.zeros((D, LANES), F32).at[:, :N_EXPERTS].set(router_w)
    wh, wl = _split_bf16(wpad)
    bpad = jnp.zeros((1, LANES), F32).at[0, :N_EXPERTS].set(router_bias)
    return pl.pallas_call(
        _router_sorted_body,
        grid=(T // tm,),
        in_specs=[
            pl.BlockSpec((tm, D), lambda i: (i, 0)),
            pl.BlockSpec((D, LANES), lambda i: (0, 0)),
            pl.BlockSpec((D, LANES), lambda i: (0, 0)),
            pl.BlockSpec((1, LANES), lambda i: (0, 0)),
        ],
        out_specs=[
            pl.BlockSpec((tm, LANES), lambda i: (i, 0)),
            pl.BlockSpec((8, tm), lambda i: (0, i)),
            pl.BlockSpec((1, 8, LANES), lambda i: (i, 0, 0)),
        ],
        out_shape=[
            jax.ShapeDtypeStruct((T, LANES), F32),
            jax.ShapeDtypeStruct((8, T), F32),
            jax.ShapeDtypeStruct((T // tm, 8, LANES), F32),
        ],
        compiler_params=_cparams(("parallel",)),
    )(x, wh, wl, bpad)


def _experts_sorted_body(cnt_ref, xb_ref, x_ref, gate_ref, gidt_ref, ltri_ref, utri_ref, wg_ref, wu_ref,
                         wd_ref, g_ref, b_ref, y_ref, yb_ref, xs_ref, gs_ref, acc_ref, rank_ref,
                         *, W, Wp, chunk):
    win = pl.program_id(0)
    e = pl.program_id(1)
    grp = e // EXPERTS_PER_GROUP
    padded = [((cnt_ref[win * N_GROUPS + g] + chunk - 1) // chunk) * chunk for g in range(N_GROUPS)]
    starts = [0]
    for g in range(N_GROUPS - 1):
        starts.append(starts[-1] + padded[g])
    start = starts[0]
    for g in range(1, N_GROUPS):
        start = jnp.where(grp == g, starts[g], start)
    n_chunks = (cnt_ref[win * N_GROUPS + grp] + chunk - 1) // chunk

    @pl.when(e == 0)
    def _():
        gate = gate_ref[...]
        lane = lax.broadcasted_iota(jnp.int32, gate.shape, 1)
        lanef = lane.astype(F32)
        gid = jnp.sum(jnp.where(lane == GID_LANE, gate, 0.0), axis=-1, keepdims=True)
        member = jnp.where((lanef == gid) & (lane < N_GROUPS), 1.0, 0.0)
        earlier = _dot(ltri_ref[...], member.astype(BF16))
        first = jnp.zeros(gate.shape, F32)
        for g in range(1, N_GROUPS):
            first = jnp.where(lane == g, starts[g].astype(F32), first)
        rank = jnp.sum(member * (first + earlier), axis=-1, keepdims=True)
        rank_ref[...] = jnp.broadcast_to(rank, gate.shape)

        gid_r = gidt_ref[...]
        sub = lax.broadcasted_iota(jnp.int32, gid_r.shape, 0)
        member_r = jnp.where(sub.astype(F32) == gid_r, 1.0, 0.0)
        earlier_r = _dot(member_r.astype(BF16), utri_ref[...])
        first_r = jnp.zeros(gid_r.shape, F32)
        for g in range(1, N_GROUPS):
            first_r = jnp.where(sub == g, starts[g].astype(F32), first_r)
        rank_r = jnp.sum(member_r * (first_r + earlier_r), axis=0, keepdims=True)
        rows = lax.broadcasted_iota(jnp.int32, (Wp, W), 0).astype(F32)
        perm = jnp.where(rows == rank_r, 1.0, 0.0).astype(BF16)
        xs_ref[...] = _dot(perm, xb_ref[...]).astype(BF16)
        g_hi, g_lo = _split_bf16(gate)
        gs_ref[...] = _dot(perm, g_hi) + _dot(perm, g_lo)
        acc_ref[...] = jnp.zeros(acc_ref.shape, F32)

    def one_chunk(i, carry):
        r0 = pl.multiple_of(start + i * chunk, chunk)
        xc = xs_ref[pl.ds(r0, chunk), :]
        a = _dot(xc, wg_ref[0])
        u = _dot(xc, wu_ref[0])
        gs = gs_ref[pl.ds(r0, chunk), :]
        lane = lax.broadcasted_iota(jnp.int32, gs.shape, 1)
        gcol = jnp.sum(jnp.where(lane == e, gs, 0.0), axis=-1, keepdims=True)
        h = a * _sigmoid(a) * u * gcol
        acc_ref[pl.ds(r0, chunk), :] += _dot(h.astype(BF16), wd_ref[0])
        return carry

    lax.fori_loop(0, n_chunks, one_chunk, 0)

    @pl.when(e == N_EXPERTS - 1)
    def _():
        cols = lax.broadcasted_iota(jnp.int32, (W, Wp), 1).astype(F32)
        unperm = jnp.where(cols == rank_ref[...][:, :1], 1.0, 0.0).astype(BF16)
        ffn = _dot(unperm, acc_ref[...].astype(BF16))
        y = _layer_norm(DN_ALPHA * x_ref[...] + ffn, g_ref[...], b_ref[...])
        y_ref[...] = y
        yb_ref[...] = y.astype(BF16)


def _moe_ln(xb, x, router_w, router_bias, wg, wu, wd, g, b):
    T, D = x.shape
    W = min(MOE_WINDOW, T)
    chunk = MOE_CHUNK
    Wp = W + N_GROUPS * chunk
    E, _, DE = wg.shape
    assert T % W == 0 and W % chunk == 0
    gate, gidt, cnt = _router_sorted(x, router_w, router_bias, W)
    counts = cnt[:, 0, :N_GROUPS].astype(jnp.int32).reshape(-1)
    t = np.arange(W)
    ltri = jnp.asarray((t[None, :] < t[:, None]).astype(np.float32), BF16)
    once = pl.Buffered(1)
    row = lambda shape: pl.BlockSpec(shape, lambda i, e, c: (i, 0))
    vec = pl.BlockSpec((1, D), lambda i, e, c: (0, 0))
    tri = pl.BlockSpec((W, W), lambda i, e, c: (0, 0), pipeline_mode=once)
    grid_spec = pltpu.PrefetchScalarGridSpec(
        num_scalar_prefetch=1,
        grid=(T // W, E),
        in_specs=[
            row((W, D)),
            pl.BlockSpec((W, D), lambda i, e, c: (i, 0), pipeline_mode=once),
            row((W, LANES)),
            pl.BlockSpec((8, W), lambda i, e, c: (0, i)),
            tri, tri,
            pl.BlockSpec((1, D, DE), lambda i, e, c: (e, 0, 0)),
            pl.BlockSpec((1, D, DE), lambda i, e, c: (e, 0, 0)),
            pl.BlockSpec((1, DE, D), lambda i, e, c: (e, 0, 0)),
            vec, vec,
        ],
        out_specs=[row((W, D)), row((W, D))],
        scratch_shapes=[
            pltpu.VMEM((Wp, D), BF16),
            pltpu.VMEM((Wp, LANES), F32),
            pltpu.VMEM((Wp, D), F32),
            pltpu.VMEM((W, LANES), F32),
        ],
    )
    return pl.pallas_call(
        functools.partial(_experts_sorted_body, W=W, Wp=Wp, chunk=chunk),
        grid_spec=grid_spec,
        out_shape=[jax.ShapeDtypeStruct((T, D), F32), jax.ShapeDtypeStruct((T, D), BF16)],
        compiler_params=pltpu.CompilerParams(dimension_semantics=("parallel", "arbitrary"),
                                             vmem_limit_bytes=MOE_VMEM_LIMIT),
    )(counts, xb, x, gate, gidt, ltri, ltri.T, wg, wu, wd, g.reshape(1, D), b.reshape(1, D))


def _kt_tiles(t, B, S, n, tk):
    kt = t.reshape(B, S // tk, tk, n, HEAD_DIM)
    return kt.transpose(0, 3, 1, 4, 2)


def _block_onehots(S, block, tk):
    key = np.arange(S).reshape(S // tk, 1, tk)
    r = np.arange(LANES).reshape(1, LANES, 1)
    return jnp.asarray((key // block == r).astype(np.float32), BF16)


def _rope_tiled(S):
    cos, sin = _rope_tables(jnp.arange(S))
    reps = LANES // HALF
    return jnp.tile(cos, (1, reps)), jnp.tile(sin, (1, reps))


def _nsa_mixer(xb, B, S, w_in, cmp_k_w1, cmp_k_w2, cmp_v_w1, cmp_v_w2, cmp_k_pos, cmp_v_pos):
    G, HPG, KV = NSA_KV_GROUPS, NSA_HPG, NSA_KV_DIM
    L, STR, SB = NSA_CMP_LEN, NSA_CMP_STRIDE, NSA_SEL_BLOCK
    assert L == 2 * STR and S % SB == 0 and S // SB <= LANES
    T = B * S
    n_cmp = (S - L) // STR + 1
    NC = S // STR
    n_sel = S // SB
    top_n = min(NSA_SEL_TOPN, n_sel)

    cos2, sin2 = _rope_tiled(S)
    wb = w_in.astype(BF16)
    wcol = lambda i: wb[:, D_MODEL + i * KV: D_MODEL + (i + 1) * KV]
    tn = 2 * KV
    tk_s = min(512, S)
    tq_w = NSA_WINDOW // 2
    proj = _proj(xb, wb[:, :D_MODEL + 2 * KV], cos2, sin2, [2] * (D_MODEL // tn) + [0], S, tn=tn)
    kt_s = _proj_kt(xb, wcol(2), cos2, sin2, B, S, tk_s)
    kt_w = _proj_kt(xb, wcol(4), cos2, sin2, B, S, tq_w)
    vaug = _proj_vaug(xb, jnp.concatenate([wcol(3), wcol(5)], axis=1), tn=tn)
    wg = jnp.zeros((D_MODEL, LANES), BF16).at[:, :3 * N_HEADS].set(wb[:, D_MODEL + 6 * KV:])
    gates = _proj(xb, wg, cos2, sin2, [0], S, out_dtype=F32, tn=LANES).reshape(B, S, LANES)

    col = lambda i: proj[:, D_MODEL + i * KV: D_MODEL + (i + 1) * KV]
    proj3 = proj.reshape(B, S, proj.shape[1])
    vaug3 = vaug.reshape(B, S, vaug.shape[1])

    ccos, csin = _rope_tables(jnp.arange(NC) * STR + (L - 1))
    ccos = jnp.concatenate([ccos, ccos], axis=1)
    csin = jnp.concatenate([csin, csin], axis=1)
    to_rows = lambda t: t.reshape(B, S, G, HEAD_DIM).transpose(0, 2, 1, 3).reshape(B * G, NC, STR * HEAD_DIM)
    kc = _compress(to_rows(col(0)), cmp_k_w1, cmp_k_pos, cmp_k_w2, ccos, csin, True, n_cmp)
    vc = _compress(to_rows(col(1)), cmp_v_w1, cmp_v_pos, cmp_v_w2, ccos, csin, False, n_cmp)
    kct = kc.reshape(B, G, NC, HEAD_DIM).transpose(0, 1, 3, 2)

    ci = np.arange(NC)[:, None]
    sj = np.arange(LANES)[None, :]
    overlap = ((ci * STR < (sj + 1) * SB) & (ci * STR + L > sj * SB) & (ci < n_cmp) & (sj < n_sel))
    overlap = jnp.broadcast_to(jnp.asarray(overlap.astype(np.float32), BF16), (B, G, NC, LANES))
    vo = jnp.concatenate([vc.reshape(B, G, NC, HEAD_DIM), jnp.ones((B, G, NC, 1), BF16),
                          jnp.zeros((B, G, NC, LANES - HEAD_DIM - 1), BF16), overlap], axis=-1)

    o_cmp, selb = _nsa_cmp(proj3, kct, vo, gates, n_cmp, n_sel, top_n)
    o_slc = _nsa_slc(proj3, kt_s, vaug3, 0,
                     _block_onehots(S, SB, tk_s), selb, gates, tq=tk_s, tk=tk_s)
    o_win = _nsa_win(proj3, kt_w, vaug3, G, gates, tq=tq_w)
    return [o.reshape(T, D_MODEL) for o in (o_cmp, o_slc, o_win)]


def _moba_mixer(xb, B, S, w_in):
    H = N_HEADS
    nb = S // MOBA_BLOCK
    top_k = min(MOBA_TOPK, nb)
    cos2, sin2 = _rope_tiled(S)
    tn = 512
    n_t = D_MODEL // tn
    wb = w_in.astype(BF16)
    tk = min(2 * MOBA_BLOCK, S)
    q = _proj(xb, wb[:, :D_MODEL], cos2, sin2, [2] * n_t, S, tn=tn)
    kt = _proj_kt(xb, wb[:, D_MODEL:2 * D_MODEL], cos2, sin2, B, S, tk, tn=tn)
    vaug = _proj_vaug(xb, wb[:, 2 * D_MODEL:], tn=tn)
    e = _block_onehots(S, MOBA_BLOCK, tk)
    kh, kl = _moba_kmean(kt, e.transpose(0, 2, 1))
    o = _moba_attn(q.reshape(B, S, D_MODEL), kt, vaug.reshape(B, S, H * LANES), e, kh, kl, nb, top_k,
                   tq=tk, tk=tk)
    return [o.reshape(B * S, D_MODEL)]


def kernel(x, nsa_w_in, nsa_w_out, nsa_cmp_k_w1, nsa_cmp_k_w2, nsa_cmp_v_w1, nsa_cmp_v_w2, nsa_cmp_k_pos, nsa_cmp_v_pos, moba_w_in, moba_w_out, router_w, router_bias, moe_w_gate, moe_w_up, moe_w_down, ln_g, ln_b):
    B, S, D = x.shape
    xf = x.reshape(B * S, D)
    xb = xf.astype(BF16)
    for layer in range(DEPTH):
        j = layer // 2
        if layer % 2 == 0:
            os_ = _nsa_mixer(xb, B, S, nsa_w_in[j], nsa_cmp_k_w1[j], nsa_cmp_k_w2[j], nsa_cmp_v_w1[j],
                             nsa_cmp_v_w2[j], nsa_cmp_k_pos[j], nsa_cmp_v_pos[j])
            w_out = nsa_w_out[j]
        else:
            os_ = _moba_mixer(xb, B, S, moba_w_in[j])
            w_out = moba_w_out[j]
        xf, xb = _outproj_ln(os_, w_out.astype(BF16), xf, ln_g[layer, 0], ln_b[layer, 0])
        xf, xb = _moe_ln(xb, xf, router_w, router_bias, moe_w_gate[layer].astype(BF16),
                         moe_w_up[layer].astype(BF16), moe_w_down[layer].astype(BF16),
                         ln_g[layer, 1], ln_b[layer, 1])
    return xf.reshape(B, S, D)
```

```python
import functools

import jax
import jax.numpy as jnp
import numpy as np
from jax import lax
from jax.experimental import pallas as pl
from jax.experimental.pallas import tpu as pltpu

F32 = jnp.float32
BF16 = jnp.bfloat16

D_MODEL = 1024
N_HEADS = 16
HEAD_DIM = 64
HALF = HEAD_DIM // 2
ROPE_THETA = 10000.0
DEPTH = 2
DN_ALPHA = (2 * DEPTH) ** 0.25
LN_EPS = 1e-5
NEG = -1e30
FORCE = 1e9
MASK = -1e30
REMOVED = -3.0e38
LANES = 128
Q_SCALE_LOG2 = float(HEAD_DIM ** -0.5 * np.log2(np.e))
ROW_CHUNK = 128

NSA_KV_GROUPS = 4
NSA_HPG = N_HEADS // NSA_KV_GROUPS
NSA_KV_DIM = NSA_KV_GROUPS * HEAD_DIM
NSA_CMP_LEN = 32
NSA_CMP_STRIDE = 16
NSA_SEL_BLOCK = 64
NSA_SEL_TOPN = 16
NSA_WINDOW = 512

MOBA_BLOCK = 256
MOBA_TOPK = 3

N_EXPERTS = 16
N_GROUPS = 4
EXPERTS_PER_GROUP = N_EXPERTS // N_GROUPS
D_EXPERT = 512

VMEM_LIMIT = 48 * 1024 * 1024


def _cparams(sem):
    return pltpu.CompilerParams(dimension_semantics=sem, vmem_limit_bytes=VMEM_LIMIT)


def _dot(a, b):
    return jnp.dot(a, b, preferred_element_type=F32)


def _split_bf16(x):
    hi = x.astype(BF16)
    lo = (x - hi.astype(F32)).astype(BF16)
    return hi, lo


def _sigmoid(x):
    return 1.0 / (1.0 + jnp.exp(-x))


def _proj_body(mode_ref, x_ref, w_ref, cos_ref, sin_ref, o_ref, *, tn):
    j = pl.program_id(0)
    mode = mode_ref[j]
    acc = _dot(x_ref[...], w_ref[...])

    @pl.when(mode == 0)
    def _():
        o_ref[...] = acc.astype(o_ref.dtype)

    @pl.when(mode != 0)
    def _():
        sc = jnp.where(mode == 2, Q_SCALE_LOG2, 1.0).astype(F32)
        cos = cos_ref[...] * sc
        sin = sin_ref[...] * sc
        for c in range(tn // LANES):
            o_ref[:, c * LANES:(c + 1) * LANES] = (
                _rope_chunk(acc[:, c * LANES:(c + 1) * LANES], cos, sin).astype(o_ref.dtype))


def _proj(xb, w, cos2, sin2, modes, seq, out_dtype=BF16, tm=1024, tn=512):
    T, K = xb.shape
    N = w.shape[1]
    tm = min(tm, seq)
    assert T % tm == 0 and N % tn == 0 and seq % tm == 0 and len(modes) == N // tn
    n_pos = seq // tm
    grid_spec = pltpu.PrefetchScalarGridSpec(
        num_scalar_prefetch=1,
        grid=(N // tn, T // tm),
        in_specs=[
            pl.BlockSpec((tm, K), lambda j, i, m: (i, 0)),
            pl.BlockSpec((K, tn), lambda j, i, m: (0, j)),
            pl.BlockSpec((tm, LANES), lambda j, i, m: (i % n_pos, 0)),
            pl.BlockSpec((tm, LANES), lambda j, i, m: (i % n_pos, 0)),
        ],
        out_specs=pl.BlockSpec((tm, tn), lambda j, i, m: (i, j)),
    )
    return pl.pallas_call(
        functools.partial(_proj_body, tn=tn),
        grid_spec=grid_spec,
        out_shape=jax.ShapeDtypeStruct((T, N), out_dtype),
        compiler_params=_cparams(("parallel", "parallel")),
    )(jnp.asarray(modes, jnp.int32), xb, w, cos2, sin2)


def _rope_tables(pos):
    inv = 1.0 / (ROPE_THETA ** (jnp.arange(0, HEAD_DIM, 2, dtype=F32) / HEAD_DIM))
    ang = pos.astype(F32)[:, None] * inv[None, :]
    return jnp.cos(ang), jnp.sin(ang)


def _rope_chunk(a, cos, sin):
    lane = lax.broadcasted_iota(jnp.int32, a.shape, 1)
    up = pltpu.roll(a, LANES - HALF, 1)
    dn = pltpu.roll(a, HALF, 1)
    return a * cos + jnp.where((lane % HEAD_DIM) < HALF, -up, dn) * sin


def _proj_kt_body(x_ref, w_ref, cos_ref, sin_ref, o_ref, *, tn, tk):
    acc = _dot(x_ref[...], w_ref[...])
    cos = cos_ref[...]
    sin = sin_ref[...]
    tm = acc.shape[0]
    for c in range(tn // LANES):
        kt = _rope_chunk(acc[:, c * LANES:(c + 1) * LANES], cos, sin).T
        for hh in range(2):
            for kk in range(tm // tk):
                o_ref[0, 2 * c + hh, kk] = (
                    kt[hh * HEAD_DIM:(hh + 1) * HEAD_DIM, kk * tk:(kk + 1) * tk].astype(o_ref.dtype))


def _proj_kt(xb, w, cos2, sin2, B, S, tk, tm=1024, tn=256):
    T, K = xb.shape
    N = w.shape[1]
    tm = min(tm, S)
    tn = min(tn, N)
    assert S % tm == 0 and tm % tk == 0 and N % tn == 0
    n_pos = S // tm
    return pl.pallas_call(
        functools.partial(_proj_kt_body, tn=tn, tk=tk),
        grid=(N // tn, T // tm),
        in_specs=[
            pl.BlockSpec((tm, K), lambda j, i: (i, 0)),
            pl.BlockSpec((K, tn), lambda j, i: (0, j)),
            pl.BlockSpec((tm, LANES), lambda j, i: (i % n_pos, 0)),
            pl.BlockSpec((tm, LANES), lambda j, i: (i % n_pos, 0)),
        ],
        out_specs=pl.BlockSpec((1, tn // HEAD_DIM, tm // tk, HEAD_DIM, tk),
                               lambda j, i: (i // n_pos, j, i % n_pos, 0, 0)),
        out_shape=jax.ShapeDtypeStruct((B, N // HEAD_DIM, S // tk, HEAD_DIM, tk), BF16),
        compiler_params=_cparams(("parallel", "parallel")),
    )(xb, w, cos2, sin2)


def _proj_vaug_body(x_ref, w_ref, o_ref, *, tn):
    acc = _dot(x_ref[...], w_ref[...])
    lane = lax.broadcasted_iota(jnp.int32, (acc.shape[0], LANES), 1)
    tail = jnp.where(lane == HEAD_DIM, 1.0, 0.0)
    for c in range(tn // LANES):
        a = acc[:, c * LANES:(c + 1) * LANES]
        for k, head in enumerate((a, pltpu.roll(a, HEAD_DIM, 1))):
            o_ref[:, (2 * c + k) * LANES:(2 * c + k + 1) * LANES] = (
                jnp.where(lane < HEAD_DIM, head, tail).astype(o_ref.dtype))


def _proj_vaug(xb, w, tm=1024, tn=512):
    T, K = xb.shape
    N = w.shape[1]
    tm = min(tm, T)
    assert T % tm == 0 and N % tn == 0
    return pl.pallas_call(
        functools.partial(_proj_vaug_body, tn=tn),
        grid=(N // tn, T // tm),
        in_specs=[pl.BlockSpec((tm, K), lambda j, i: (i, 0)), pl.BlockSpec((K, tn), lambda j, i: (0, j))],
        out_specs=pl.BlockSpec((tm, 2 * tn), lambda j, i: (i, j)),
        out_shape=jax.ShapeDtypeStruct((T, 2 * N), BF16),
        compiler_params=_cparams(("parallel", "parallel")),
    )(xb, w)


def _gelu_tanh(x):
    c = np.float32(np.sqrt(2.0 / np.pi))
    return 0.5 * x * (1.0 + jnp.tanh(c * (x + 0.044715 * (x * x * x))))


def _compress_body(r_ref, w1_ref, pos_ref, w2_ref, w2r_ref, cos_ref, sin_ref, o_ref, *, rope, n_cmp):
    r = r_ref[0]
    nc = r.shape[0]
    half = NSA_CMP_STRIDE * HEAD_DIM
    a = _dot(r, w1_ref[0])
    b = _dot(r, w1_ref[1])
    pos = pos_ref[...]
    pb = _dot(pos[:, :half], w1_ref[0]) + _dot(pos[:, half:], w1_ref[1])
    b_next = pltpu.roll(b, nc - 1, 0)
    h = _gelu_tanh(a + b_next + pb[0:1, :]).astype(BF16)
    o = _dot(h, w2_ref[...])
    if rope:
        o = o * cos_ref[...] + _dot(h, w2r_ref[...]) * sin_ref[...]
    row = lax.broadcasted_iota(jnp.int32, o.shape, 0)
    o_ref[0] = jnp.where(row < n_cmp, o, 0.0).astype(o_ref.dtype)


def _compress(r, w1, pos, w2, cos_c, sin_c, rope, n_cmp):
    BG, NC, K = r.shape
    hidden = w1.shape[1]
    w1s = w1.astype(BF16).reshape(2, K, hidden)
    pos8 = jnp.zeros((8, 2 * K), BF16).at[0].set(pos.reshape(-1).astype(BF16))
    w2r = jnp.concatenate([-w2[:, HALF:], w2[:, :HALF]], axis=1).astype(BF16)
    full = lambda shape: pl.BlockSpec(shape, lambda i: (0,) * len(shape))
    return pl.pallas_call(
        functools.partial(_compress_body, rope=rope, n_cmp=n_cmp),
        grid=(BG,),
        in_specs=[
            pl.BlockSpec((1, NC, K), lambda i: (i, 0, 0)),
            full((2, K, hidden)),
            full((8, 2 * K)),
            full((hidden, HEAD_DIM)),
            full((hidden, HEAD_DIM)),
            full((NC, HEAD_DIM)),
            full((NC, HEAD_DIM)),
        ],
        out_specs=pl.BlockSpec((1, NC, HEAD_DIM), lambda i: (i, 0, 0)),
        out_shape=jax.ShapeDtypeStruct((BG, NC, HEAD_DIM), BF16),
        compiler_params=_cparams(("parallel",)),
    )(r, w1s, pos8, w2.astype(BF16), w2r, cos_c, sin_c)


def _topk_mask_t(v, k):
    idx = lax.broadcasted_iota(jnp.int32, v.shape, 0).astype(F32)

    def step(_, cur):
        m = jnp.max(cur, axis=0, keepdims=True)
        first = jnp.min(jnp.where(cur == m, idx, float(LANES)), axis=0, keepdims=True)
        return jnp.where(idx == first, REMOVED, cur)

    return lax.fori_loop(0, k, step, v, unroll=True) != v


def _head_kt(e, kt, parity):
    z = jnp.zeros_like(kt)
    parts = ([] if e is None else [e]) + ([kt, z] if parity == 0 else [z, kt])
    return jnp.concatenate(parts, axis=0)


def _flash_init(m_ref, acc_ref):
    m_ref[...] = jnp.full(m_ref.shape, -jnp.inf, F32)
    acc_ref[...] = jnp.zeros(acc_ref.shape, F32)


def _flash_step(jobs, s_ref, m_ref, alpha_ref, acc_ref):
    for r0, q, kt, _, mask in jobs:
        rows = q.shape[0]
        s = _dot(q, kt)
        if mask is not None:
            s = jnp.where(mask, s, MASK)
        s_ref[r0:r0 + rows, :s.shape[1]] = s
        cm = s[:, :LANES]
        for c in range(1, s.shape[1] // LANES):
            cm = jnp.maximum(cm, s[:, c * LANES:(c + 1) * LANES])
        m_prev = m_ref[r0:r0 + rows, :]
        m_new = jnp.maximum(m_prev, jnp.max(cm, axis=-1, keepdims=True))
        alpha_ref[r0:r0 + rows, :] = jnp.exp2(m_prev - m_new)
        m_ref[r0:r0 + rows, :] = m_new
    for r0, q, _, v, _ in jobs:
        for r in range(r0, r0 + q.shape[0], ROW_CHUNK):
            rows = min(ROW_CHUNK, r0 + q.shape[0] - r)
            m = m_ref[r:r + rows, :]
            p = jnp.concatenate(
                [jnp.exp2(s_ref[r:r + rows, c * LANES:(c + 1) * LANES] - m)
                 for c in range(v.shape[0] // LANES)], axis=1).astype(BF16)
            acc_ref[r:r + rows, :] = alpha_ref[r:r + rows, :] * acc_ref[r:r + rows, :] + _dot(p, v)


def _diagonal_jobs(r0, q, kt, v, tq, tk):
    half = tq // 2
    row_a, col_a = _tile_iotas(half, half)
    row_b, col_b = _tile_iotas(half, tk)
    return [(r0, q[:half], kt[:, :half], v[:half], col_a <= row_a),
            (r0 + half, q[half:], kt, v, col_b <= half + row_b)]


def _flash_scratch(rows, tk):
    return [pltpu.VMEM((rows, tk), F32), pltpu.VMEM((rows, LANES), F32),
            pltpu.VMEM((rows, LANES), F32), pltpu.VMEM((rows, LANES), F32)]


def _pair_merge(even, odd):
    lane = lax.broadcasted_iota(jnp.int32, even.shape, 1)
    return jnp.where(lane < HEAD_DIM, even, pltpu.roll(odd, HEAD_DIM, 1))


def _flash_pairs_out(acc_ref, tq, n_heads, gates):
    def head(h):
        acc = acc_ref[h * tq:(h + 1) * tq, :]
        return acc * (gates[h] / acc[:, HEAD_DIM:HEAD_DIM + 1])
    return jnp.concatenate([_pair_merge(head(h), head(h + 1)) for h in range(0, n_heads, 2)], axis=1)


def _gate_col(logits, branch, h):
    idx = branch * N_HEADS + pl.program_id(1) * NSA_HPG + h
    lane = lax.broadcasted_iota(jnp.int32, logits.shape, 1)
    return _sigmoid(jnp.sum(jnp.where(lane == idx, logits, 0.0), axis=-1, keepdims=True))


def _tile_iotas(rows, tk):
    return (lax.broadcasted_iota(jnp.int32, (rows, tk), 0),
            lax.broadcasted_iota(jnp.int32, (rows, tk), 1))


def _nsa_cmp_body(q_ref, kct_ref, vo_ref, g_ref, o_ref, selb_ref, s_ref, m_ref, imp_ref,
                  *, tq, n_cmp, n_sel, top_n):
    s0 = pl.program_id(2) * tq
    q4 = q_ref[0]
    n_chunks = vo_ref.shape[2] // LANES
    rc = min(ROW_CHUNK, tq)

    def attend(nv):
        nc = nv * LANES
        kct = kct_ref[0, 0, :, :nc]
        vo = vo_ref[0, 0, :nc, :]
        tpos = s0 + lax.broadcasted_iota(jnp.int32, (tq, nc), 0)
        nidx = lax.broadcasted_iota(jnp.int32, (tq, nc), 1)
        cmask = (nidx * NSA_CMP_STRIDE + (NSA_CMP_LEN - 1) <= tpos) & (nidx < n_cmp)
        for h in range(NSA_HPG):
            pair = q4[:, (h // 2) * LANES:(h // 2 + 1) * LANES]
            s = jnp.where(cmask, _dot(pair, _head_kt(None, kct, h % 2)), NEG)
            s_ref[h * tq:(h + 1) * tq, :nc] = s
            cm = s[:, :LANES]
            for c in range(1, nv):
                cm = jnp.maximum(cm, s[:, c * LANES:(c + 1) * LANES])
            m_ref[h * tq:(h + 1) * tq, :] = jnp.broadcast_to(jnp.max(cm, axis=-1, keepdims=True), (tq, LANES))

        for r0 in range(0, tq, rc):
            imp = jnp.zeros((rc, LANES), F32)
            heads = []
            for h in range(NSA_HPG):
                r = h * tq + r0
                m = m_ref[r:r + rc, :]
                e = jnp.concatenate(
                    [jnp.exp2(s_ref[r:r + rc, c * LANES:(c + 1) * LANES] - m) for c in range(nv)],
                    axis=1).astype(BF16)
                res = _dot(e, vo)
                inv = jnp.where(m[:, :1] > 0.5 * NEG, 1.0 / res[:, HEAD_DIM:HEAD_DIM + 1], 0.0)
                gate = _gate_col(g_ref[0, r0:r0 + rc, :], 0, h)
                heads.append(res[:, :LANES] * (inv * gate))
                imp = imp + res[:, LANES:] * inv
            o_ref[0, r0:r0 + rc, :] = jnp.concatenate(
                [_pair_merge(heads[h], heads[h + 1]) for h in range(0, NSA_HPG, 2)], axis=1).astype(o_ref.dtype)
            imp_ref[r0:r0 + rc, :] = imp

    attend(n_chunks)
    imp = imp_ref[...]

    blk = lax.broadcasted_iota(jnp.int32, imp.shape, 1)
    jq = (s0 + lax.broadcasted_iota(jnp.int32, imp.shape, 0)) // NSA_SEL_BLOCK
    forced = (blk == 0) | (blk == jq) | (blk == jq - 1)
    imp = jnp.where(blk > jq, NEG, jnp.where(forced, FORCE, imp))
    imp = jnp.where(blk < n_sel, imp, REMOVED)
    sel_t = _topk_mask_t(imp.T, top_n)
    selb_ref[0, 0] = jnp.where(sel_t, 0.0, MASK).T.astype(selb_ref.dtype)


def _nsa_cmp(proj3, kct, vo, g_cmp, n_cmp, n_sel, top_n, tq=256):
    B, S, _ = proj3.shape
    G, HPG = NSA_KV_GROUPS, NSA_HPG
    NC = vo.shape[2]
    tq = min(tq, S)
    assert NC % LANES == 0
    return pl.pallas_call(
        functools.partial(_nsa_cmp_body, tq=tq, n_cmp=n_cmp, n_sel=n_sel, top_n=top_n),
        grid=(B, G, S // tq),
        in_specs=[
            pl.BlockSpec((1, tq, HPG * HEAD_DIM), lambda b, g, i: (b, i, g)),
            pl.BlockSpec((1, 1, HEAD_DIM, NC), lambda b, g, i: (b, g, 0, 0)),
            pl.BlockSpec((1, 1, NC, 2 * LANES), lambda b, g, i: (b, g, 0, 0)),
            pl.BlockSpec((1, tq, LANES), lambda b, g, i: (b, i, 0)),
        ],
        out_specs=[
            pl.BlockSpec((1, tq, HPG * HEAD_DIM), lambda b, g, i: (b, i, g)),
            pl.BlockSpec((1, 1, tq, LANES), lambda b, g, i: (b, g, i, 0)),
        ],
        out_shape=[
            jax.ShapeDtypeStruct((B, S, D_MODEL), BF16),
            jax.ShapeDtypeStruct((B, G, S, LANES), BF16),
        ],
        scratch_shapes=[pltpu.VMEM((HPG * tq, NC), F32), pltpu.VMEM((HPG * tq, LANES), F32),
                        pltpu.VMEM((tq, LANES), F32)],
        compiler_params=_cparams(("parallel", "parallel", "parallel")),
    )(proj3, kct, vo, g_cmp)


def _nsa_slc_body(q_ref, kt_ref, v_ref, e_ref, selb_ref, g_ref, o_ref, qa_ref, s_ref, m_ref, alpha_ref,
                  acc_ref, *, tq, tk):
    s0 = pl.program_id(2) * tq
    _flash_init(m_ref, acc_ref)
    selb = selb_ref[0, 0]
    q4 = q_ref[0]
    for h in range(NSA_HPG):
        qa_ref[h * tq:(h + 1) * tq, :] = jnp.concatenate(
            [selb, q4[:, (h // 2) * LANES:(h // 2 + 1) * LANES]], axis=1)

    def tile(ki, causal):
        k0 = pl.multiple_of(ki * tk, tk)
        kts = [_head_kt(e_ref[ki], kt_ref[0, 0, ki], parity) for parity in range(2)]
        v = v_ref[0, pl.ds(k0, tk), :]
        jobs = []
        for h in range(NSA_HPG):
            q = qa_ref[h * tq:(h + 1) * tq, :]
            jobs += _diagonal_jobs(h * tq, q, kts[h % 2], v, tq, tk) if causal else [(h * tq, q, kts[h % 2], v, None)]
        _flash_step(jobs, s_ref, m_ref, alpha_ref, acc_ref)

    last = (s0 + tq - 1) // tk

    def full_tile(ki, carry):
        tile(ki, False)
        return carry

    lax.fori_loop(0, last, full_tile, 0)
    tile(last, True)
    gates = [_gate_col(g_ref[0], 1, h) for h in range(NSA_HPG)]
    o_ref[0] = _flash_pairs_out(acc_ref, tq, NSA_HPG, gates).astype(o_ref.dtype)


def _nsa_slc(proj3, kt, vaug3, v_block, e, selb, g, tq, tk):
    B, S, _ = proj3.shape
    G, HPG = NSA_KV_GROUPS, NSA_HPG
    nk = S // tk
    assert tk == tq and kt.shape == (B, G, nk, HEAD_DIM, tk) and e.shape == (nk, LANES, tk)
    once = pl.Buffered(1)
    return pl.pallas_call(
        functools.partial(_nsa_slc_body, tq=tq, tk=tk),
        grid=(B, G, S // tq),
        in_specs=[
            pl.BlockSpec((1, tq, HPG * HEAD_DIM), lambda b, g_, i: (b, i, g_)),
            pl.BlockSpec((1, 1, nk, HEAD_DIM, tk), lambda b, g_, i: (b, g_, 0, 0, 0), pipeline_mode=once),
            pl.BlockSpec((1, S, LANES), lambda b, g_, i: (b, 0, v_block + g_), pipeline_mode=once),
            pl.BlockSpec((nk, LANES, tk), lambda b, g_, i: (0, 0, 0), pipeline_mode=once),
            pl.BlockSpec((1, 1, tq, LANES), lambda b, g_, i: (b, g_, i, 0)),
            pl.BlockSpec((1, tq, LANES), lambda b, g_, i: (b, i, 0)),
        ],
        out_specs=pl.BlockSpec((1, tq, HPG * HEAD_DIM), lambda b, g_, i: (b, i, g_)),
        out_shape=jax.ShapeDtypeStruct((B, S, D_MODEL), BF16),
        scratch_shapes=[pltpu.VMEM((HPG * tq, 2 * LANES), BF16)] + _flash_scratch(HPG * tq, tk),
        compiler_params=_cparams(("parallel", "parallel", "parallel")),
    )(proj3, kt, vaug3, e, selb, g)


def _nsa_win_body(q_ref, kt_ref, v_ref, g_ref, o_ref, s_ref, m_ref, *, tq, n_tiles):
    s0 = pl.program_id(2) * tq
    tk = n_tiles * tq
    k_first = jnp.maximum(pl.program_id(2) - (n_tiles - 1), 0)
    k0 = pl.multiple_of(k_first * tq, tq)
    kt = jnp.concatenate([kt_ref[0, 0, k_first + j] for j in range(n_tiles)], axis=1)
    v = v_ref[0, pl.ds(k0, tk), :]
    row, col = _tile_iotas(tq, tk)
    dist = (s0 - k0) + row - col
    valid = (dist >= 0) & (dist < NSA_WINDOW)
    q4 = q_ref[0]
    for h in range(NSA_HPG):
        pair = q4[:, (h // 2) * LANES:(h // 2 + 1) * LANES]
        s = jnp.where(valid, _dot(pair, _head_kt(None, kt, h % 2)), MASK)
        s_ref[h * tq:(h + 1) * tq, :] = s
        cm = s[:, :LANES]
        for c in range(1, tk // LANES):
            cm = jnp.maximum(cm, s[:, c * LANES:(c + 1) * LANES])
        m_ref[h * tq:(h + 1) * tq, :] = jnp.broadcast_to(jnp.max(cm, axis=-1, keepdims=True), (tq, LANES))

    rc = min(ROW_CHUNK, tq)
    for r0 in range(0, tq, rc):
        heads = []
        for h in range(NSA_HPG):
            r = h * tq + r0
            m = m_ref[r:r + rc, :]
            p = jnp.concatenate(
                [jnp.exp2(s_ref[r:r + rc, c * LANES:(c + 1) * LANES] - m) for c in range(tk // LANES)],
                axis=1).astype(BF16)
            res = _dot(p, v)
            heads.append(res * (_gate_col(g_ref[0, r0:r0 + rc, :], 2, h) / res[:, HEAD_DIM:HEAD_DIM + 1]))
        o_ref[0, r0:r0 + rc, :] = jnp.concatenate(
            [_pair_merge(heads[h], heads[h + 1]) for h in range(0, NSA_HPG, 2)], axis=1).astype(o_ref.dtype)


def _nsa_win(proj3, kt, vaug3, v_block, g, tq):
    B, S, _ = proj3.shape
    G, HPG = NSA_KV_GROUPS, NSA_HPG
    nk = S // tq
    n_tiles = NSA_WINDOW // tq + 1
    assert NSA_WINDOW % tq == 0 and nk >= n_tiles and kt.shape == (B, G, nk, HEAD_DIM, tq)
    once = pl.Buffered(1)
    return pl.pallas_call(
        functools.partial(_nsa_win_body, tq=tq, n_tiles=n_tiles),
        grid=(B, G, S // tq),
        in_specs=[
            pl.BlockSpec((1, tq, HPG * HEAD_DIM), lambda b, g_, i: (b, i, g_)),
            pl.BlockSpec((1, 1, nk, HEAD_DIM, tq), lambda b, g_, i: (b, g_, 0, 0, 0), pipeline_mode=once),
            pl.BlockSpec((1, S, LANES), lambda b, g_, i: (b, 0, v_block + g_), pipeline_mode=once),
            pl.BlockSpec((1, tq, LANES), lambda b, g_, i: (b, i, 0)),
        ],
        out_specs=pl.BlockSpec((1, tq, HPG * HEAD_DIM), lambda b, g_, i: (b, i, g_)),
        out_shape=jax.ShapeDtypeStruct((B, S, D_MODEL), BF16),
        scratch_shapes=[pltpu.VMEM((HPG * tq, n_tiles * tq), F32), pltpu.VMEM((HPG * tq, LANES), F32)],
        compiler_params=_cparams(("parallel", "parallel", "parallel")),
    )(proj3, kt, vaug3, g)


def _moba_kmean_body(kt_ref, et_ref, hi_ref, lo_ref):
    nk = kt_ref.shape[2]
    km = jnp.zeros((HEAD_DIM, LANES), F32)
    for ki in range(nk):
        km = km + _dot(kt_ref[0, 0, ki], et_ref[ki])
    hi, lo = _split_bf16(km * (1.0 / MOBA_BLOCK))
    hi_ref[0, 0] = hi
    lo_ref[0, 0] = lo


def _moba_kmean(kt, et):
    B, H, nk, _, tk = kt.shape
    spec = pl.BlockSpec((1, 1, HEAD_DIM, LANES), lambda b, h: (b, h, 0, 0))
    return pl.pallas_call(
        _moba_kmean_body,
        grid=(B, H),
        in_specs=[
            pl.BlockSpec((1, 1, nk, HEAD_DIM, tk), lambda b, h: (b, h, 0, 0, 0)),
            pl.BlockSpec((nk, tk, LANES), lambda b, h: (0, 0, 0)),
        ],
        out_specs=[spec, spec],
        out_shape=[jax.ShapeDtypeStruct((B, H, HEAD_DIM, LANES), BF16)] * 2,
        compiler_params=_cparams(("parallel", "parallel")),
    )(kt, et)


def _moba_body(q_ref, kt_ref, v_ref, e_ref, kh_ref, kl_ref, o_ref, qa_ref, s_ref, m_ref, alpha_ref,
               acc_ref, *, tq, tk, hb, nb, top_k):
    s0 = pl.program_id(2) * tq
    _flash_init(m_ref, acc_ref)

    blk = lax.broadcasted_iota(jnp.int32, (LANES, tq), 0)
    cb = (s0 + lax.broadcasted_iota(jnp.int32, (LANES, tq), 1)) // MOBA_BLOCK
    qs = q_ref[0]
    for h in range(hb):
        pair = qs[:, (h // 2) * LANES:(h // 2 + 1) * LANES]
        gsc = (_dot(pair, _head_kt(None, kh_ref[0, h], h % 2))
               + _dot(pair, _head_kt(None, kl_ref[0, h], h % 2))).T
        gsc = jnp.where(blk < cb, gsc, NEG)
        gsc = jnp.where(blk < nb, gsc, REMOVED)
        sel = (_topk_mask_t(gsc, top_k) & (blk < cb)) | (blk == cb)
        qa_ref[h * tq:(h + 1) * tq, :] = jnp.concatenate(
            [jnp.where(sel, 0.0, MASK).T.astype(BF16), pair], axis=1)

    def tile(ki, causal):
        k0 = pl.multiple_of(ki * tk, tk)
        e = e_ref[ki]
        jobs = []
        for h in range(hb):
            q = qa_ref[h * tq:(h + 1) * tq, :]
            kt = _head_kt(e, kt_ref[0, h, ki], h % 2)
            v = v_ref[0, pl.ds(k0, tk), h * LANES:(h + 1) * LANES]
            jobs += _diagonal_jobs(h * tq, q, kt, v, tq, tk) if causal else [(h * tq, q, kt, v, None)]
        _flash_step(jobs, s_ref, m_ref, alpha_ref, acc_ref)

    last = (s0 + tq - 1) // tk

    def full_tile(ki, carry):
        tile(ki, False)
        return carry

    lax.fori_loop(0, last, full_tile, 0)
    tile(last, True)
    o_ref[0] = _flash_pairs_out(acc_ref, tq, hb, [1.0] * hb).astype(o_ref.dtype)


def _moba_attn(proj3, kt, vaug3, e, kh, kl, nb, top_k, tq, tk, hb=4):
    B, S, _ = proj3.shape
    H = N_HEADS
    nk = S // tk
    assert tk == tq and tk % MOBA_BLOCK == 0 and H % hb == 0 and hb % 2 == 0 and nb <= LANES
    once = pl.Buffered(1)
    return pl.pallas_call(
        functools.partial(_moba_body, tq=tq, tk=tk, hb=hb, nb=nb, top_k=top_k),
        grid=(B, H // hb, S // tq),
        in_specs=[
            pl.BlockSpec((1, tq, hb * HEAD_DIM), lambda b, h, i: (b, i, h)),
            pl.BlockSpec((1, hb, nk, HEAD_DIM, tk), lambda b, h, i: (b, h, 0, 0, 0), pipeline_mode=once),
            pl.BlockSpec((1, S, hb * LANES), lambda b, h, i: (b, 0, h), pipeline_mode=once),
            pl.BlockSpec((nk, LANES, tk), lambda b, h, i: (0, 0, 0), pipeline_mode=once),
            pl.BlockSpec((1, hb, HEAD_DIM, LANES), lambda b, h, i: (b, h, 0, 0), pipeline_mode=once),
            pl.BlockSpec((1, hb, HEAD_DIM, LANES), lambda b, h, i: (b, h, 0, 0), pipeline_mode=once),
        ],
        out_specs=pl.BlockSpec((1, tq, hb * HEAD_DIM), lambda b, h, i: (b, i, h)),
        out_shape=jax.ShapeDtypeStruct((B, S, D_MODEL), BF16),
        scratch_shapes=[pltpu.VMEM((hb * tq, 2 * LANES), BF16)] + _flash_scratch(hb * tq, tk),
        compiler_params=_cparams(("parallel", "parallel", "parallel")),
    )(proj3, kt, vaug3, e, kh, kl)


def _layer_norm(r, g, b):
    mu = jnp.mean(r, axis=-1, keepdims=True)
    c = r - mu
    var = jnp.mean(c * c, axis=-1, keepdims=True)
    return c * lax.rsqrt(var + LN_EPS) * g + b


def _outproj_body(*refs, n_o):
    o_refs = refs[:n_o]
    w_ref, x_ref, g_ref, b_ref, y_ref, yb_ref = refs[n_o:]
    o = o_refs[0][...].astype(F32)
    for r in o_refs[1:]:
        o = o + r[...].astype(F32)
    mix = _dot(o.astype(BF16), w_ref[...])
    y = _layer_norm(DN_ALPHA * x_ref[...] + mix, g_ref[...], b_ref[...])
    y_ref[...] = y
    yb_ref[...] = y.astype(BF16)


def _outproj_ln(os_, w, x, g, b, tm=512):
    T, D = x.shape
    tm = min(tm, T)
    n_o = len(os_)
    row = pl.BlockSpec((tm, D), lambda i: (i, 0))
    vec = pl.BlockSpec((1, D), lambda i: (0, 0))
    return pl.pallas_call(
        functools.partial(_outproj_body, n_o=n_o),
        grid=(T // tm,),
        in_specs=[row] * n_o + [pl.BlockSpec((D, D), lambda i: (0, 0)), row, vec, vec],
        out_specs=[row, row],
        out_shape=[jax.ShapeDtypeStruct((T, D), F32), jax.ShapeDtypeStruct((T, D), BF16)],
        compiler_params=_cparams(("parallel",)),
    )(*os_, w, x, g.reshape(1, D), b.reshape(1, D))


def _router_body(x_ref, wh_ref, wl_ref, bias_ref, gate_ref):
    x_hi, x_lo = _split_bf16(x_ref[...])
    wh = wh_ref[...]
    logits = _dot(x_hi, wh) + _dot(x_lo, wh) + _dot(x_hi, wl_ref[...])
    scores = _sigmoid(logits)
    lane = lax.broadcasted_iota(jnp.int32, scores.shape, 1)
    lanef = lane.astype(F32)
    live = lane < N_EXPERTS
    biased = jnp.where(live, scores + bias_ref[...], REMOVED)

    def top2(mask):
        v = jnp.where(mask, biased, REMOVED)
        m1 = jnp.max(v, axis=-1, keepdims=True)
        i1 = jnp.min(jnp.where(v == m1, lanef, float(LANES)), axis=-1, keepdims=True)
        v2 = jnp.where(lanef == i1, REMOVED, v)
        m2 = jnp.max(v2, axis=-1, keepdims=True)
        i2 = jnp.min(jnp.where(v2 == m2, lanef, float(LANES)), axis=-1, keepdims=True)
        return m1 + m2, jnp.where((lanef == i1) | (lanef == i2), 1.0, 0.0)

    best, best_sel = top2(lane // EXPERTS_PER_GROUP == 0)
    for grp in range(1, N_GROUPS):
        score, sel = top2(lane // EXPERTS_PER_GROUP == grp)
        better = score > best
        best = jnp.where(better, score, best)
        best_sel = jnp.where(better, sel, best_sel)
    w = best_sel * scores
    gate_ref[...] = w / jnp.sum(w, axis=-1, keepdims=True)


def _router(x, router_w, router_bias, tm=512):
    T, D = x.shape
    tm = min(tm, T)
    wpad = jnp.zeros((D, LANES), F32).at[:, :N_EXPERTS].set(router_w)
    wh, wl = _split_bf16(wpad)
    bpad = jnp.zeros((1, LANES), F32).at[0, :N_EXPERTS].set(router_bias)
    return pl.pallas_call(
        _router_body,
        grid=(T // tm,),
        in_specs=[
            pl.BlockSpec((tm, D), lambda i: (i, 0)),
            pl.BlockSpec((D, LANES), lambda i: (0, 0)),
            pl.BlockSpec((D, LANES), lambda i: (0, 0)),
            pl.BlockSpec((1, LANES), lambda i: (0, 0)),
        ],
        out_specs=pl.BlockSpec((tm, LANES), lambda i: (i, 0)),
        out_shape=jax.ShapeDtypeStruct((T, LANES), F32),
        compiler_params=_cparams(("parallel",)),
    )(x, wh, wl, bpad)


def _experts_body(xb_ref, x_ref, gate_ref, wg_ref, wu_ref, wd_ref, g_ref, b_ref, y_ref, yb_ref, acc_ref):
    e = pl.program_id(1)

    @pl.when(e == 0)
    def _():
        acc_ref[...] = jnp.zeros(acc_ref.shape, F32)

    xb = xb_ref[...]
    a = _dot(xb, wg_ref[0])
    u = _dot(xb, wu_ref[0])
    lane = lax.broadcasted_iota(jnp.int32, gate_ref.shape, 1)
    gcol = jnp.sum(jnp.where(lane == e, gate_ref[...], 0.0), axis=-1, keepdims=True)
    h = a * _sigmoid(a) * u * gcol
    acc_ref[...] += _dot(h.astype(BF16), wd_ref[0])

    @pl.when(e == N_EXPERTS - 1)
    def _():
        y = _layer_norm(DN_ALPHA * x_ref[...] + acc_ref[...], g_ref[...], b_ref[...])
        y_ref[...] = y
        yb_ref[...] = y.astype(BF16)


def _experts_ln(xb, x, gate, wg, wu, wd, g, b, tm=1024):
    T, D = x.shape
    tm = min(tm, T)
    E, _, DE = wg.shape
    row = pl.BlockSpec((tm, D), lambda i, e: (i, 0))
    vec = pl.BlockSpec((1, D), lambda i, e: (0, 0))
    return pl.pallas_call(
        _experts_body,
        grid=(T // tm, E),
        in_specs=[
            row, row,
            pl.BlockSpec((tm, LANES), lambda i, e: (i, 0)),
            pl.BlockSpec((1, D, DE), lambda i, e: (e, 0, 0)),
            pl.BlockSpec((1, D, DE), lambda i, e: (e, 0, 0)),
            pl.BlockSpec((1, DE, D), lambda i, e: (e, 0, 0)),
            vec, vec,
        ],
        out_specs=[row, row],
        out_shape=[jax.ShapeDtypeStruct((T, D), F32), jax.ShapeDtypeStruct((T, D), BF16)],
        scratch_shapes=[pltpu.VMEM((tm, D), F32)],
        compiler_params=_cparams(("parallel", "arbitrary")),
    )(xb, x, gate, wg, wu, wd, g.reshape(1, D), b.reshape(1, D))


GID_LANE = N_EXPERTS
MOE_WINDOW = 1024
MOE_CHUNK = 128
MOE_VMEM_LIMIT = 60 * 1024 * 1024


def _router_sorted_body(x_ref, wh_ref, wl_ref, bias_ref, gate_ref, gidt_ref, cnt_ref):
    x_hi, x_lo = _split_bf16(x_ref[...])
    wh = wh_ref[...]
    logits = _dot(x_hi, wh) + _dot(x_lo, wh) + _dot(x_hi, wl_ref[...])
    scores = _sigmoid(logits)
    lane = lax.broadcasted_iota(jnp.int32, scores.shape, 1)
    lanef = lane.astype(F32)
    biased = jnp.where(lane < N_EXPERTS, scores + bias_ref[...], REMOVED)

    def top2(mask):
        v = jnp.where(mask, biased, REMOVED)
        m1 = jnp.max(v, axis=-1, keepdims=True)
        i1 = jnp.min(jnp.where(v == m1, lanef, float(LANES)), axis=-1, keepdims=True)
        v2 = jnp.where(lanef == i1, REMOVED, v)
        m2 = jnp.max(v2, axis=-1, keepdims=True)
        i2 = jnp.min(jnp.where(v2 == m2, lanef, float(LANES)), axis=-1, keepdims=True)
        return m1 + m2, jnp.where((lanef == i1) | (lanef == i2), 1.0, 0.0)

    best, best_sel = top2(lane // EXPERTS_PER_GROUP == 0)
    gid = jnp.zeros_like(best)
    for grp in range(1, N_GROUPS):
        score, sel = top2(lane // EXPERTS_PER_GROUP == grp)
        better = score > best
        best = jnp.where(better, score, best)
        best_sel = jnp.where(better, sel, best_sel)
        gid = jnp.where(better, float(grp), gid)
    w = best_sel * scores
    gate = w / jnp.sum(w, axis=-1, keepdims=True)
    gate_ref[...] = jnp.where(lane == GID_LANE, gid, gate)
    gid_b = jnp.broadcast_to(gid, scores.shape)
    gidt_ref[...] = gid_b.T[:8, :]
    cnt_ref[0] = jnp.broadcast_to(
        jnp.sum(jnp.where(lanef == gid_b, 1.0, 0.0), axis=0, keepdims=True), (8, LANES))


def _router_sorted(x, router_w, router_bias, tm):
    T, D = x.shape
    wpad = jnp.zeros((D, LANES), F32).at[:, :N_EXPERTS].set(router_w)
    wh, wl = _split_bf16(wpad)
    bpad = jnp.zeros((1, LANES), F32).at[0, :N_EXPERTS].set(router_bias)
    return pl.pallas_call(
        _router_sorted_body,
        grid=(T // tm,),
        in_specs=[
            pl.BlockSpec((tm, D), lambda i: (i, 0)),
            pl.BlockSpec((D, LANES), lambda i: (0, 0)),
            pl.BlockSpec((D, LANES), lambda i: (0, 0)),
            pl.BlockSpec((1, LANES), lambda i: (0, 0)),
        ],
        out_specs=[
            pl.BlockSpec((tm, LANES), lambda i: (i, 0)),
            pl.BlockSpec((8, tm), lambda i: (0, i)),
            pl.BlockSpec((1, 8, LANES), lambda i: (i, 0, 0)),
        ],
        out_shape=[
            jax.ShapeDtypeStruct((T, LANES), F32),
            jax.ShapeDtypeStruct((8, T), F32),
            jax.ShapeDtypeStruct((T // tm, 8, LANES), F32),
        ],
        compiler_params=_cparams(("parallel",)),
    )(x, wh, wl, bpad)


def _experts_sorted_body(cnt_ref, xb_ref, x_ref, gate_ref, gidt_ref, ltri_ref, utri_ref, wg_ref, wu_ref,
                         wd_ref, g_ref, b_ref, y_ref, yb_ref, xs_ref, gs_ref, acc_ref, rank_ref,
                         *, W, Wp, chunk):
    win = pl.program_id(0)
    e = pl.program_id(1)
    grp = e // EXPERTS_PER_GROUP
    padded = [((cnt_ref[win * N_GROUPS + g] + chunk - 1) // chunk) * chunk for g in range(N_GROUPS)]
    starts = [0]
    for g in range(N_GROUPS - 1):
        starts.append(starts[-1] + padded[g])
    start = starts[0]
    for g in range(1, N_GROUPS):
        start = jnp.where(grp == g, starts[g], start)
    n_chunks = (cnt_ref[win * N_GROUPS + grp] + chunk - 1) // chunk

    @pl.when(e == 0)
    def _():
        gate = gate_ref[...]
        lane = lax.broadcasted_iota(jnp.int32, gate.shape, 1)
        lanef = lane.astype(F32)
        gid = jnp.sum(jnp.where(lane == GID_LANE, gate, 0.0), axis=-1, keepdims=True)
        member = jnp.where((lanef == gid) & (lane < N_GROUPS), 1.0, 0.0)
        earlier = _dot(ltri_ref[...], member.astype(BF16))
        first = jnp.zeros(gate.shape, F32)
        for g in range(1, N_GROUPS):
            first = jnp.where(lane == g, starts[g].astype(F32), first)
        rank = jnp.sum(member * (first + earlier), axis=-1, keepdims=True)
        rank_ref[...] = jnp.broadcast_to(rank, gate.shape)

        gid_r = gidt_ref[...]
        sub = lax.broadcasted_iota(jnp.int32, gid_r.shape, 0)
        member_r = jnp.where(sub.astype(F32) == gid_r, 1.0, 0.0)
        earlier_r = _dot(member_r.astype(BF16), utri_ref[...])
        first_r = jnp.zeros(gid_r.shape, F32)
        for g in range(1, N_GROUPS):
            first_r = jnp.where(sub == g, starts[g].astype(F32), first_r)
        rank_r = jnp.sum(member_r * (first_r + earlier_r), axis=0, keepdims=True)
        rows = lax.broadcasted_iota(jnp.int32, (Wp, W), 0).astype(F32)
        perm = jnp.where(rows == rank_r, 1.0, 0.0).astype(BF16)
        xs_ref[...] = _dot(perm, xb_ref[...]).astype(BF16)
        g_hi, g_lo = _split_bf16(gate)
        gs_ref[...] = _dot(perm, g_hi) + _dot(perm, g_lo)
        acc_ref[...] = jnp.zeros(acc_ref.shape, F32)

    def expert_rows(r0, rows):
        r0 = pl.multiple_of(r0, chunk)
        xc = xs_ref[pl.ds(r0, rows), :]
        a = _dot(xc, wg_ref[0])
        u = _dot(xc, wu_ref[0])
        gs = gs_ref[pl.ds(r0, rows), :]
        lane = lax.broadcasted_iota(jnp.int32, gs.shape, 1)
        gcol = jnp.sum(jnp.where(lane == e, gs, 0.0), axis=-1, keepdims=True)
        h = a * _sigmoid(a) * u * gcol
        acc_ref[pl.ds(r0, rows), :] += _dot(h.astype(BF16), wd_ref[0])

    def two_chunks(i, carry):
        expert_rows(start + i * (2 * chunk), 2 * chunk)
        return carry

    lax.fori_loop(0, n_chunks // 2, two_chunks, 0)

    @pl.when(n_chunks % 2 == 1)
    def _():
        expert_rows(start + (n_chunks - 1) * chunk, chunk)

    @pl.when(e == N_EXPERTS - 1)
    def _():
        cols = lax.broadcasted_iota(jnp.int32, (W, Wp), 1).astype(F32)
        unperm = jnp.where(cols == rank_ref[...][:, :1], 1.0, 0.0).astype(BF16)
        ffn = _dot(unperm, acc_ref[...].astype(BF16))
        y = _layer_norm(DN_ALPHA * x_ref[...] + ffn, g_ref[...], b_ref[...])
        y_ref[...] = y
        yb_ref[...] = y.astype(BF16)


def _moe_ln(xb, x, router_w, router_bias, wg, wu, wd, g, b):
    T, D = x.shape
    W = min(MOE_WINDOW, T)
    chunk = MOE_CHUNK
    Wp = W + N_GROUPS * chunk
    E, _, DE = wg.shape
    assert T % W == 0 and W % chunk == 0
    gate, gidt, cnt = _router_sorted(x, router_w, router_bias, W)
    counts = cnt[:, 0, :N_GROUPS].astype(jnp.int32).reshape(-1)
    t = np.arange(W)
    ltri = jnp.asarray((t[None, :] < t[:, None]).astype(np.float32), BF16)
    once = pl.Buffered(1)
    row = lambda shape: pl.BlockSpec(shape, lambda i, e, c: (i, 0))
    vec = pl.BlockSpec((1, D), lambda i, e, c: (0, 0))
    tri = pl.BlockSpec((W, W), lambda i, e, c: (0, 0), pipeline_mode=once)
    grid_spec = pltpu.PrefetchScalarGridSpec(
        num_scalar_prefetch=1,
        grid=(T // W, E),
        in_specs=[
            row((W, D)),
            pl.BlockSpec((W, D), lambda i, e, c: (i, 0), pipeline_mode=once),
            row((W, LANES)),
            pl.BlockSpec((8, W), lambda i, e, c: (0, i)),
            tri, tri,
            pl.BlockSpec((1, D, DE), lambda i, e, c: (e, 0, 0)),
            pl.BlockSpec((1, D, DE), lambda i, e, c: (e, 0, 0)),
            pl.BlockSpec((1, DE, D), lambda i, e, c: (e, 0, 0)),
            vec, vec,
        ],
        out_specs=[row((W, D)), row((W, D))],
        scratch_shapes=[
            pltpu.VMEM((Wp, D), BF16),
            pltpu.VMEM((Wp, LANES), F32),
            pltpu.VMEM((Wp, D), F32),
            pltpu.VMEM((W, LANES), F32),
        ],
    )
    return pl.pallas_call(
        functools.partial(_experts_sorted_body, W=W, Wp=Wp, chunk=chunk),
        grid_spec=grid_spec,
        out_shape=[jax.ShapeDtypeStruct((T, D), F32), jax.ShapeDtypeStruct((T, D), BF16)],
        compiler_params=pltpu.CompilerParams(dimension_semantics=("parallel", "arbitrary"),
                                             vmem_limit_bytes=MOE_VMEM_LIMIT),
    )(counts, xb, x, gate, gidt, ltri, ltri.T, wg, wu, wd, g.reshape(1, D), b.reshape(1, D))


def _kt_tiles(t, B, S, n, tk):
    kt = t.reshape(B, S // tk, tk, n, HEAD_DIM)
    return kt.transpose(0, 3, 1, 4, 2)


def _block_onehots(S, block, tk):
    key = np.arange(S).reshape(S // tk, 1, tk)
    r = np.arange(LANES).reshape(1, LANES, 1)
    return jnp.asarray((key // block == r).astype(np.float32), BF16)


def _rope_tiled(S):
    cos, sin = _rope_tables(jnp.arange(S))
    reps = LANES // HALF
    return jnp.tile(cos, (1, reps)), jnp.tile(sin, (1, reps))


def _nsa_mixer(xb, B, S, w_in, cmp_k_w1, cmp_k_w2, cmp_v_w1, cmp_v_w2, cmp_k_pos, cmp_v_pos):
    G, HPG, KV = NSA_KV_GROUPS, NSA_HPG, NSA_KV_DIM
    L, STR, SB = NSA_CMP_LEN, NSA_CMP_STRIDE, NSA_SEL_BLOCK
    assert L == 2 * STR and S % SB == 0 and S // SB <= LANES
    T = B * S
    n_cmp = (S - L) // STR + 1
    NC = S // STR
    n_sel = S // SB
    top_n = min(NSA_SEL_TOPN, n_sel)

    cos2, sin2 = _rope_tiled(S)
    wb = w_in.astype(BF16)
    wcol = lambda i: wb[:, D_MODEL + i * KV: D_MODEL + (i + 1) * KV]
    tn = 2 * KV
    tk_s = min(512, S)
    tq_w = NSA_WINDOW // 2
    proj = _proj(xb, wb[:, :D_MODEL + 2 * KV], cos2, sin2, [2] * (D_MODEL // tn) + [0], S, tn=tn)
    kt_s = _proj_kt(xb, wcol(2), cos2, sin2, B, S, tk_s)
    kt_w = _proj_kt(xb, wcol(4), cos2, sin2, B, S, tq_w)
    vaug = _proj_vaug(xb, jnp.concatenate([wcol(3), wcol(5)], axis=1), tn=tn)
    wg = jnp.zeros((D_MODEL, LANES), BF16).at[:, :3 * N_HEADS].set(wb[:, D_MODEL + 6 * KV:])
    gates = _proj(xb, wg, cos2, sin2, [0], S, out_dtype=F32, tn=LANES).reshape(B, S, LANES)

    col = lambda i: proj[:, D_MODEL + i * KV: D_MODEL + (i + 1) * KV]
    proj3 = proj.reshape(B, S, proj.shape[1])
    vaug3 = vaug.reshape(B, S, vaug.shape[1])

    ccos, csin = _rope_tables(jnp.arange(NC) * STR + (L - 1))
    ccos = jnp.concatenate([ccos, ccos], axis=1)
    csin = jnp.concatenate([csin, csin], axis=1)
    to_rows = lambda t: t.reshape(B, S, G, HEAD_DIM).transpose(0, 2, 1, 3).reshape(B * G, NC, STR * HEAD_DIM)
    kc = _compress(to_rows(col(0)), cmp_k_w1, cmp_k_pos, cmp_k_w2, ccos, csin, True, n_cmp)
    vc = _compress(to_rows(col(1)), cmp_v_w1, cmp_v_pos, cmp_v_w2, ccos, csin, False, n_cmp)
    kct = kc.reshape(B, G, NC, HEAD_DIM).transpose(0, 1, 3, 2)

    ci = np.arange(NC)[:, None]
    sj = np.arange(LANES)[None, :]
    overlap = ((ci * STR < (sj + 1) * SB) & (ci * STR + L > sj * SB) & (ci < n_cmp) & (sj < n_sel))
    overlap = jnp.broadcast_to(jnp.asarray(overlap.astype(np.float32), BF16), (B, G, NC, LANES))
    vo = jnp.concatenate([vc.reshape(B, G, NC, HEAD_DIM), jnp.ones((B, G, NC, 1), BF16),
                          jnp.zeros((B, G, NC, LANES - HEAD_DIM - 1), BF16), overlap], axis=-1)

    o_cmp, selb = _nsa_cmp(proj3, kct, vo, gates, n_cmp, n_sel, top_n)
    o_slc = _nsa_slc(proj3, kt_s, vaug3, 0,
                     _block_onehots(S, SB, tk_s), selb, gates, tq=tk_s, tk=tk_s)
    o_win = _nsa_win(proj3, kt_w, vaug3, G, gates, tq=tq_w)
    return [o.reshape(T, D_MODEL) for o in (o_cmp, o_slc, o_win)]


def _moba_mixer(xb, B, S, w_in):
    H = N_HEADS
    nb = S // MOBA_BLOCK
    top_k = min(MOBA_TOPK, nb)
    cos2, sin2 = _rope_tiled(S)
    tn = 512
    n_t = D_MODEL // tn
    wb = w_in.astype(BF16)
    tk = min(2 * MOBA_BLOCK, S)
    q = _proj(xb, wb[:, :D_MODEL], cos2, sin2, [2] * n_t, S, tn=tn)
    kt = _proj_kt(xb, wb[:, D_MODEL:2 * D_MODEL], cos2, sin2, B, S, tk, tn=tn)
    vaug = _proj_vaug(xb, wb[:, 2 * D_MODEL:], tn=tn)
    e = _block_onehots(S, MOBA_BLOCK, tk)
    kh, kl = _moba_kmean(kt, e.transpose(0, 2, 1))
    o = _moba_attn(q.reshape(B, S, D_MODEL), kt, vaug.reshape(B, S, H * LANES), e, kh, kl, nb, top_k,
                   tq=tk, tk=tk)
    return [o.reshape(B * S, D_MODEL)]


def kernel(x, nsa_w_in, nsa_w_out, nsa_cmp_k_w1, nsa_cmp_k_w2, nsa_cmp_v_w1, nsa_cmp_v_w2, nsa_cmp_k_pos, nsa_cmp_v_pos, moba_w_in, moba_w_out, router_w, router_bias, moe_w_gate, moe_w_up, moe_w_down, ln_g, ln_b):
    B, S, D = x.shape
    xf = x.reshape(B * S, D)
    xb = xf.astype(BF16)
    for layer in range(DEPTH):
        j = layer // 2
        if layer % 2 == 0:
            os_ = _nsa_mixer(xb, B, S, nsa_w_in[j], nsa_cmp_k_w1[j], nsa_cmp_k_w2[j], nsa_cmp_v_w1[j],
                             nsa_cmp_v_w2[j], nsa_cmp_k_pos[j], nsa_cmp_v_pos[j])
            w_out = nsa_w_out[j]
        else:
            os_ = _moba_mixer(xb, B, S, moba_w_in[j])
            w_out = moba_w_out[j]
        xf, xb = _outproj_ln(os_, w_out.astype(BF16), xf, ln_g[layer, 0], ln_b[layer, 0])
        xf, xb = _moe_ln(xb, xf, router_w, router_bias, moe_w_gate[layer].astype(BF16),
                         moe_w_up[layer].astype(BF16), moe_w_down[layer].astype(BF16),
                         ln_g[layer, 1], ln_b[layer, 1])
    return xf.reshape(B, S, D)
```

```python
import functools

import jax
import jax.numpy as jnp
import numpy as np
from jax import lax
from jax.experimental import pallas as pl
from jax.experimental.pallas import tpu as pltpu

F32 = jnp.float32
BF16 = jnp.bfloat16

D_MODEL = 1024
N_HEADS = 16
HEAD_DIM = 64
HALF = HEAD_DIM // 2
ROPE_THETA = 10000.0
DEPTH = 2
DN_ALPHA = (2 * DEPTH) ** 0.25
LN_EPS = 1e-5
NEG = -1e30
FORCE = 1e9
MASK = -1e30
REMOVED = -3.0e38
LANES = 128
Q_SCALE_LOG2 = float(HEAD_DIM ** -0.5 * np.log2(np.e))
ROW_CHUNK = 128

NSA_KV_GROUPS = 4
NSA_HPG = N_HEADS // NSA_KV_GROUPS
NSA_KV_DIM = NSA_KV_GROUPS * HEAD_DIM
NSA_CMP_LEN = 32
NSA_CMP_STRIDE = 16
NSA_SEL_BLOCK = 64
NSA_SEL_TOPN = 16
NSA_WINDOW = 512

MOBA_BLOCK = 256
MOBA_TOPK = 3

N_EXPERTS = 16
N_GROUPS = 4
EXPERTS_PER_GROUP = N_EXPERTS // N_GROUPS
D_EXPERT = 512

VMEM_LIMIT = 48 * 1024 * 1024


def _cparams(sem):
    return pltpu.CompilerParams(dimension_semantics=sem, vmem_limit_bytes=VMEM_LIMIT)


def _dot(a, b):
    return jnp.dot(a, b, preferred_element_type=F32)


def _split_bf16(x):
    hi = x.astype(BF16)
    lo = (x - hi.astype(F32)).astype(BF16)
    return hi, lo


def _sigmoid(x):
    return 1.0 / (1.0 + jnp.exp(-x))


def _proj_body(mode_ref, x_ref, w_ref, cos_ref, sin_ref, o_ref, *, tn):
    j = pl.program_id(0)
    mode = mode_ref[j]
    acc = _dot(x_ref[...], w_ref[...])

    @pl.when(mode == 0)
    def _():
        o_ref[...] = acc.astype(o_ref.dtype)

    @pl.when(mode != 0)
    def _():
        sc = jnp.where(mode == 2, Q_SCALE_LOG2, 1.0).astype(F32)
        cos = cos_ref[...] * sc
        sin = sin_ref[...] * sc
        for c in range(tn // LANES):
            o_ref[:, c * LANES:(c + 1) * LANES] = (
                _rope_chunk(acc[:, c * LANES:(c + 1) * LANES], cos, sin).astype(o_ref.dtype))


def _proj(xb, w, cos2, sin2, modes, seq, out_dtype=BF16, tm=1024, tn=512):
    T, K = xb.shape
    N = w.shape[1]
    tm = min(tm, seq)
    assert T % tm == 0 and N % tn == 0 and seq % tm == 0 and len(modes) == N // tn
    n_pos = seq // tm
    grid_spec = pltpu.PrefetchScalarGridSpec(
        num_scalar_prefetch=1,
        grid=(N // tn, T // tm),
        in_specs=[
            pl.BlockSpec((tm, K), lambda j, i, m: (i, 0)),
            pl.BlockSpec((K, tn), lambda j, i, m: (0, j)),
            pl.BlockSpec((tm, LANES), lambda j, i, m: (i % n_pos, 0)),
            pl.BlockSpec((tm, LANES), lambda j, i, m: (i % n_pos, 0)),
        ],
        out_specs=pl.BlockSpec((tm, tn), lambda j, i, m: (i, j)),
    )
    return pl.pallas_call(
        functools.partial(_proj_body, tn=tn),
        grid_spec=grid_spec,
        out_shape=jax.ShapeDtypeStruct((T, N), out_dtype),
        compiler_params=_cparams(("parallel", "parallel")),
    )(jnp.asarray(modes, jnp.int32), xb, w, cos2, sin2)


def _rope_tables(pos):
    inv = 1.0 / (ROPE_THETA ** (jnp.arange(0, HEAD_DIM, 2, dtype=F32) / HEAD_DIM))
    ang = pos.astype(F32)[:, None] * inv[None, :]
    return jnp.cos(ang), jnp.sin(ang)


def _rope_chunk(a, cos, sin):
    lane = lax.broadcasted_iota(jnp.int32, a.shape, 1)
    up = pltpu.roll(a, LANES - HALF, 1)
    dn = pltpu.roll(a, HALF, 1)
    return a * cos + jnp.where((lane % HEAD_DIM) < HALF, -up, dn) * sin


def _proj_kt_body(x_ref, w_ref, cos_ref, sin_ref, o_ref, *, tn, tk, rope, scale):
    acc = _dot(x_ref[...], w_ref[...])
    cos = cos_ref[...] * scale
    sin = sin_ref[...] * scale
    tm = acc.shape[0]
    for c in range(tn // LANES):
        a = acc[:, c * LANES:(c + 1) * LANES]
        kt = (_rope_chunk(a, cos, sin) if rope else a).T
        for hh in range(2):
            for kk in range(tm // tk):
                o_ref[0, 2 * c + hh, kk] = (
                    kt[hh * HEAD_DIM:(hh + 1) * HEAD_DIM, kk * tk:(kk + 1) * tk].astype(o_ref.dtype))


def _proj_kt(xb, w, cos2, sin2, B, S, tk, tm=1024, tn=256, rope=True, scale=1.0):
    T, K = xb.shape
    N = w.shape[1]
    tm = min(tm, S)
    tn = min(tn, N)
    assert S % tm == 0 and tm % tk == 0 and N % tn == 0
    n_pos = S // tm
    return pl.pallas_call(
        functools.partial(_proj_kt_body, tn=tn, tk=tk, rope=rope, scale=scale),
        grid=(N // tn, T // tm),
        in_specs=[
            pl.BlockSpec((tm, K), lambda j, i: (i, 0)),
            pl.BlockSpec((K, tn), lambda j, i: (0, j)),
            pl.BlockSpec((tm, LANES), lambda j, i: (i % n_pos, 0)),
            pl.BlockSpec((tm, LANES), lambda j, i: (i % n_pos, 0)),
        ],
        out_specs=pl.BlockSpec((1, tn // HEAD_DIM, tm // tk, HEAD_DIM, tk),
                               lambda j, i: (i // n_pos, j, i % n_pos, 0, 0)),
        out_shape=jax.ShapeDtypeStruct((B, N // HEAD_DIM, S // tk, HEAD_DIM, tk), BF16),
        compiler_params=_cparams(("parallel", "parallel")),
    )(xb, w, cos2, sin2)


def _proj_vaug_body(x_ref, w_ref, o_ref, *, tn):
    acc = _dot(x_ref[...], w_ref[...])
    lane = lax.broadcasted_iota(jnp.int32, (acc.shape[0], LANES), 1)
    tail = jnp.where(lane == HEAD_DIM, 1.0, 0.0)
    for c in range(tn // LANES):
        a = acc[:, c * LANES:(c + 1) * LANES]
        for k, head in enumerate((a, pltpu.roll(a, HEAD_DIM, 1))):
            o_ref[:, (2 * c + k) * LANES:(2 * c + k + 1) * LANES] = (
                jnp.where(lane < HEAD_DIM, head, tail).astype(o_ref.dtype))


def _proj_vaug(xb, w, tm=1024, tn=512):
    T, K = xb.shape
    N = w.shape[1]
    tm = min(tm, T)
    assert T % tm == 0 and N % tn == 0
    return pl.pallas_call(
        functools.partial(_proj_vaug_body, tn=tn),
        grid=(N // tn, T // tm),
        in_specs=[pl.BlockSpec((tm, K), lambda j, i: (i, 0)), pl.BlockSpec((K, tn), lambda j, i: (0, j))],
        out_specs=pl.BlockSpec((tm, 2 * tn), lambda j, i: (i, j)),
        out_shape=jax.ShapeDtypeStruct((T, 2 * N), BF16),
        compiler_params=_cparams(("parallel", "parallel")),
    )(xb, w)


def _gelu_tanh(x):
    c = np.float32(np.sqrt(2.0 / np.pi))
    return 0.5 * x * (1.0 + jnp.tanh(c * (x + 0.044715 * (x * x * x))))


def _compress_body(r_ref, w1_ref, pos_ref, w2_ref, w2r_ref, cos_ref, sin_ref, o_ref, *, rope, n_cmp):
    r = r_ref[0]
    nc = r.shape[0]
    half = NSA_CMP_STRIDE * HEAD_DIM
    a = _dot(r, w1_ref[0])
    b = _dot(r, w1_ref[1])
    pos = pos_ref[...]
    pb = _dot(pos[:, :half], w1_ref[0]) + _dot(pos[:, half:], w1_ref[1])
    b_next = pltpu.roll(b, nc - 1, 0)
    h = _gelu_tanh(a + b_next + pb[0:1, :]).astype(BF16)
    o = _dot(h, w2_ref[...])
    if rope:
        o = o * cos_ref[...] + _dot(h, w2r_ref[...]) * sin_ref[...]
    row = lax.broadcasted_iota(jnp.int32, o.shape, 0)
    o_ref[0] = jnp.where(row < n_cmp, o, 0.0).astype(o_ref.dtype)


def _compress(r, w1, pos, w2, cos_c, sin_c, rope, n_cmp):
    BG, NC, K = r.shape
    hidden = w1.shape[1]
    w1s = w1.astype(BF16).reshape(2, K, hidden)
    pos8 = jnp.zeros((8, 2 * K), BF16).at[0].set(pos.reshape(-1).astype(BF16))
    w2r = jnp.concatenate([-w2[:, HALF:], w2[:, :HALF]], axis=1).astype(BF16)
    full = lambda shape: pl.BlockSpec(shape, lambda i: (0,) * len(shape))
    return pl.pallas_call(
        functools.partial(_compress_body, rope=rope, n_cmp=n_cmp),
        grid=(BG,),
        in_specs=[
            pl.BlockSpec((1, NC, K), lambda i: (i, 0, 0)),
            full((2, K, hidden)),
            full((8, 2 * K)),
            full((hidden, HEAD_DIM)),
            full((hidden, HEAD_DIM)),
            full((NC, HEAD_DIM)),
            full((NC, HEAD_DIM)),
        ],
        out_specs=pl.BlockSpec((1, NC, HEAD_DIM), lambda i: (i, 0, 0)),
        out_shape=jax.ShapeDtypeStruct((BG, NC, HEAD_DIM), BF16),
        compiler_params=_cparams(("parallel",)),
    )(r, w1s, pos8, w2.astype(BF16), w2r, cos_c, sin_c)


def _topk_mask_t(v, k):
    idx = lax.broadcasted_iota(jnp.int32, v.shape, 0).astype(F32)

    def step(_, cur):
        m = jnp.max(cur, axis=0, keepdims=True)
        first = jnp.min(jnp.where(cur == m, idx, float(LANES)), axis=0, keepdims=True)
        return jnp.where(idx == first, REMOVED, cur)

    return lax.fori_loop(0, k, step, v, unroll=True) != v


def _head_kt(e, kt, parity):
    z = jnp.zeros_like(kt)
    parts = ([] if e is None else [e]) + ([kt, z] if parity == 0 else [z, kt])
    return jnp.concatenate(parts, axis=0)


def _flash_init(m_ref, acc_ref):
    m_ref[...] = jnp.full(m_ref.shape, -jnp.inf, F32)
    acc_ref[...] = jnp.zeros(acc_ref.shape, F32)


def _flash_step(jobs, s_ref, m_ref, alpha_ref, acc_ref):
    for r0, q, kt, _, mask in jobs:
        rows = q.shape[0]
        s = _dot(q, kt)
        if mask is not None:
            s = jnp.where(mask, s, MASK)
        s_ref[r0:r0 + rows, :s.shape[1]] = s
        cm = s[:, :LANES]
        for c in range(1, s.shape[1] // LANES):
            cm = jnp.maximum(cm, s[:, c * LANES:(c + 1) * LANES])
        m_prev = m_ref[r0:r0 + rows, :]
        m_new = jnp.maximum(m_prev, jnp.max(cm, axis=-1, keepdims=True))
        alpha_ref[r0:r0 + rows, :] = jnp.exp2(m_prev - m_new)
        m_ref[r0:r0 + rows, :] = m_new
    for r0, q, _, v, _ in jobs:
        for r in range(r0, r0 + q.shape[0], ROW_CHUNK):
            rows = min(ROW_CHUNK, r0 + q.shape[0] - r)
            m = m_ref[r:r + rows, :]
            p = jnp.concatenate(
                [jnp.exp2(s_ref[r:r + rows, c * LANES:(c + 1) * LANES] - m)
                 for c in range(v.shape[0] // LANES)], axis=1).astype(BF16)
            acc_ref[r:r + rows, :] = alpha_ref[r:r + rows, :] * acc_ref[r:r + rows, :] + _dot(p, v)


def _diagonal_jobs(r0, q, kt, v, tq, tk):
    half = tq // 2
    row_a, col_a = _tile_iotas(half, half)
    row_b, col_b = _tile_iotas(half, tk)
    return [(r0, q[:half], kt[:, :half], v[:half], col_a <= row_a),
            (r0 + half, q[half:], kt, v, col_b <= half + row_b)]


def _flash_scratch(rows, tk):
    return [pltpu.VMEM((rows, tk), F32), pltpu.VMEM((rows, LANES), F32),
            pltpu.VMEM((rows, LANES), F32), pltpu.VMEM((rows, LANES), F32)]


def _pair_merge(even, odd):
    lane = lax.broadcasted_iota(jnp.int32, even.shape, 1)
    return jnp.where(lane < HEAD_DIM, even, pltpu.roll(odd, HEAD_DIM, 1))


def _flash_pairs_out(acc_ref, tq, n_heads, gates):
    def head(h):
        acc = acc_ref[h * tq:(h + 1) * tq, :]
        return acc * (gates[h] / acc[:, HEAD_DIM:HEAD_DIM + 1])
    return jnp.concatenate([_pair_merge(head(h), head(h + 1)) for h in range(0, n_heads, 2)], axis=1)


def _gate_col(logits, branch, h):
    idx = branch * N_HEADS + pl.program_id(1) * NSA_HPG + h
    lane = lax.broadcasted_iota(jnp.int32, logits.shape, 1)
    return _sigmoid(jnp.sum(jnp.where(lane == idx, logits, 0.0), axis=-1, keepdims=True))


def _tile_iotas(rows, tk):
    return (lax.broadcasted_iota(jnp.int32, (rows, tk), 0),
            lax.broadcasted_iota(jnp.int32, (rows, tk), 1))


def _nsa_cmp_body(q_ref, kct_ref, vo_ref, g_ref, o_ref, selb_ref, s_ref, m_ref, imp_ref,
                  *, tq, n_cmp, n_sel, top_n):
    s0 = pl.program_id(2) * tq
    q4 = q_ref[0]
    n_chunks = vo_ref.shape[2] // LANES
    rc = min(ROW_CHUNK, tq)

    def attend(nv):
        nc = nv * LANES
        kct = kct_ref[0, 0, :, :nc]
        vo = vo_ref[0, 0, :nc, :]
        tpos = s0 + lax.broadcasted_iota(jnp.int32, (tq, nc), 0)
        nidx = lax.broadcasted_iota(jnp.int32, (tq, nc), 1)
        cmask = (nidx * NSA_CMP_STRIDE + (NSA_CMP_LEN - 1) <= tpos) & (nidx < n_cmp)
        for h in range(NSA_HPG):
            pair = q4[:, (h // 2) * LANES:(h // 2 + 1) * LANES]
            s = jnp.where(cmask, _dot(pair, _head_kt(None, kct, h % 2)), NEG)
            s_ref[h * tq:(h + 1) * tq, :nc] = s
            cm = s[:, :LANES]
            for c in range(1, nv):
                cm = jnp.maximum(cm, s[:, c * LANES:(c + 1) * LANES])
            m_ref[h * tq:(h + 1) * tq, :] = jnp.broadcast_to(jnp.max(cm, axis=-1, keepdims=True), (tq, LANES))

        for r0 in range(0, tq, rc):
            imp = jnp.zeros((rc, LANES), F32)
            heads = []
            for h in range(NSA_HPG):
                r = h * tq + r0
                m = m_ref[r:r + rc, :]
                e = jnp.concatenate(
                    [jnp.exp2(s_ref[r:r + rc, c * LANES:(c + 1) * LANES] - m) for c in range(nv)],
                    axis=1).astype(BF16)
                res = _dot(e, vo)
                inv = jnp.where(m[:, :1] > 0.5 * NEG, 1.0 / res[:, HEAD_DIM:HEAD_DIM + 1], 0.0)
                gate = _gate_col(g_ref[0, r0:r0 + rc, :], 0, h)
                heads.append(res[:, :LANES] * (inv * gate))
                imp = imp + res[:, LANES:] * inv
            o_ref[0, r0:r0 + rc, :] = jnp.concatenate(
                [_pair_merge(heads[h], heads[h + 1]) for h in range(0, NSA_HPG, 2)], axis=1).astype(o_ref.dtype)
            imp_ref[r0:r0 + rc, :] = imp

    attend(n_chunks)
    imp = imp_ref[...]

    blk = lax.broadcasted_iota(jnp.int32, imp.shape, 1)
    jq = (s0 + lax.broadcasted_iota(jnp.int32, imp.shape, 0)) // NSA_SEL_BLOCK
    forced = (blk == 0) | (blk == jq) | (blk == jq - 1)
    imp = jnp.where(blk > jq, NEG, jnp.where(forced, FORCE, imp))
    imp = jnp.where(blk < n_sel, imp, REMOVED)
    sel_t = _topk_mask_t(imp.T, top_n)
    selb_ref[0, 0] = jnp.where(sel_t, 0.0, MASK).T.astype(selb_ref.dtype)


def _nsa_cmp(proj3, kct, vo, g_cmp, n_cmp, n_sel, top_n, tq=256):
    B, S, _ = proj3.shape
    G, HPG = NSA_KV_GROUPS, NSA_HPG
    NC = vo.shape[2]
    tq = min(tq, S)
    assert NC % LANES == 0
    return pl.pallas_call(
        functools.partial(_nsa_cmp_body, tq=tq, n_cmp=n_cmp, n_sel=n_sel, top_n=top_n),
        grid=(B, G, S // tq),
        in_specs=[
            pl.BlockSpec((1, tq, HPG * HEAD_DIM), lambda b, g, i: (b, i, g)),
            pl.BlockSpec((1, 1, HEAD_DIM, NC), lambda b, g, i: (b, g, 0, 0)),
            pl.BlockSpec((1, 1, NC, 2 * LANES), lambda b, g, i: (b, g, 0, 0)),
            pl.BlockSpec((1, tq, LANES), lambda b, g, i: (b, i, 0)),
        ],
        out_specs=[
            pl.BlockSpec((1, tq, HPG * HEAD_DIM), lambda b, g, i: (b, i, g)),
            pl.BlockSpec((1, 1, tq, LANES), lambda b, g, i: (b, g, i, 0)),
        ],
        out_shape=[
            jax.ShapeDtypeStruct((B, S, D_MODEL), BF16),
            jax.ShapeDtypeStruct((B, G, S, LANES), BF16),
        ],
        scratch_shapes=[pltpu.VMEM((HPG * tq, NC), F32), pltpu.VMEM((HPG * tq, LANES), F32),
                        pltpu.VMEM((tq, LANES), F32)],
        compiler_params=_cparams(("parallel", "parallel", "parallel")),
    )(proj3, kct, vo, g_cmp)


def _nsa_slc_body(q_ref, kt_ref, v_ref, e_ref, selb_ref, g_ref, o_ref, qa_ref, s_ref, m_ref, alpha_ref,
                  acc_ref, *, tq, tk):
    s0 = pl.program_id(2) * tq
    _flash_init(m_ref, acc_ref)
    selb = selb_ref[0, 0]
    q4 = q_ref[0]
    for h in range(NSA_HPG):
        qa_ref[h * tq:(h + 1) * tq, :] = jnp.concatenate(
            [selb, q4[:, (h // 2) * LANES:(h // 2 + 1) * LANES]], axis=1)

    def tile(ki, causal):
        k0 = pl.multiple_of(ki * tk, tk)
        kts = [_head_kt(e_ref[ki], kt_ref[0, 0, ki], parity) for parity in range(2)]
        v = v_ref[0, pl.ds(k0, tk), :]
        jobs = []
        for h in range(NSA_HPG):
            q = qa_ref[h * tq:(h + 1) * tq, :]
            jobs += _diagonal_jobs(h * tq, q, kts[h % 2], v, tq, tk) if causal else [(h * tq, q, kts[h % 2], v, None)]
        _flash_step(jobs, s_ref, m_ref, alpha_ref, acc_ref)

    last = (s0 + tq - 1) // tk

    def full_tile(ki, carry):
        tile(ki, False)
        return carry

    lax.fori_loop(0, last, full_tile, 0)
    tile(last, True)
    gates = [_gate_col(g_ref[0], 1, h) for h in range(NSA_HPG)]
    o_ref[0] = _flash_pairs_out(acc_ref, tq, NSA_HPG, gates).astype(o_ref.dtype)


def _nsa_slc(proj3, kt, vaug3, v_block, e, selb, g, tq, tk):
    B, S, _ = proj3.shape
    G, HPG = NSA_KV_GROUPS, NSA_HPG
    nk = S // tk
    assert tk == tq and kt.shape == (B, G, nk, HEAD_DIM, tk) and e.shape == (nk, LANES, tk)
    once = pl.Buffered(1)
    return pl.pallas_call(
        functools.partial(_nsa_slc_body, tq=tq, tk=tk),
        grid=(B, G, S // tq),
        in_specs=[
            pl.BlockSpec((1, tq, HPG * HEAD_DIM), lambda b, g_, i: (b, i, g_)),
            pl.BlockSpec((1, 1, nk, HEAD_DIM, tk), lambda b, g_, i: (b, g_, 0, 0, 0), pipeline_mode=once),
            pl.BlockSpec((1, S, LANES), lambda b, g_, i: (b, 0, v_block + g_), pipeline_mode=once),
            pl.BlockSpec((nk, LANES, tk), lambda b, g_, i: (0, 0, 0), pipeline_mode=once),
            pl.BlockSpec((1, 1, tq, LANES), lambda b, g_, i: (b, g_, i, 0)),
            pl.BlockSpec((1, tq, LANES), lambda b, g_, i: (b, i, 0)),
        ],
        out_specs=pl.BlockSpec((1, tq, HPG * HEAD_DIM), lambda b, g_, i: (b, i, g_)),
        out_shape=jax.ShapeDtypeStruct((B, S, D_MODEL), BF16),
        scratch_shapes=[pltpu.VMEM((HPG * tq, 2 * LANES), BF16)] + _flash_scratch(HPG * tq, tk),
        compiler_params=_cparams(("parallel", "parallel", "parallel")),
    )(proj3, kt, vaug3, e, selb, g)


def _nsa_win_body(q_ref, kt_ref, v_ref, g_ref, o_ref, s_ref, m_ref, *, tq, n_tiles):
    s0 = pl.program_id(2) * tq
    tk = n_tiles * tq
    k_first = jnp.maximum(pl.program_id(2) - (n_tiles - 1), 0)
    k0 = pl.multiple_of(k_first * tq, tq)
    kt = jnp.concatenate([kt_ref[0, 0, k_first + j] for j in range(n_tiles)], axis=1)
    v = v_ref[0, pl.ds(k0, tk), :]
    row, col = _tile_iotas(tq, tk)
    dist = (s0 - k0) + row - col
    valid = (dist >= 0) & (dist < NSA_WINDOW)
    q4 = q_ref[0]
    for h in range(NSA_HPG):
        pair = q4[:, (h // 2) * LANES:(h // 2 + 1) * LANES]
        s = jnp.where(valid, _dot(pair, _head_kt(None, kt, h % 2)), MASK)
        s_ref[h * tq:(h + 1) * tq, :] = s
        cm = s[:, :LANES]
        for c in range(1, tk // LANES):
            cm = jnp.maximum(cm, s[:, c * LANES:(c + 1) * LANES])
        m_ref[h * tq:(h + 1) * tq, :] = jnp.broadcast_to(jnp.max(cm, axis=-1, keepdims=True), (tq, LANES))

    rc = min(ROW_CHUNK, tq)
    for r0 in range(0, tq, rc):
        heads = []
        for h in range(NSA_HPG):
            r = h * tq + r0
            m = m_ref[r:r + rc, :]
            p = jnp.concatenate(
                [jnp.exp2(s_ref[r:r + rc, c * LANES:(c + 1) * LANES] - m) for c in range(tk // LANES)],
                axis=1).astype(BF16)
            res = _dot(p, v)
            heads.append(res * (_gate_col(g_ref[0, r0:r0 + rc, :], 2, h) / res[:, HEAD_DIM:HEAD_DIM + 1]))
        o_ref[0, r0:r0 + rc, :] = jnp.concatenate(
            [_pair_merge(heads[h], heads[h + 1]) for h in range(0, NSA_HPG, 2)], axis=1).astype(o_ref.dtype)


def _nsa_win(proj3, kt, vaug3, v_block, g, tq):
    B, S, _ = proj3.shape
    G, HPG = NSA_KV_GROUPS, NSA_HPG
    nk = S // tq
    n_tiles = NSA_WINDOW // tq + 1
    assert NSA_WINDOW % tq == 0 and nk >= n_tiles and kt.shape == (B, G, nk, HEAD_DIM, tq)
    once = pl.Buffered(1)
    return pl.pallas_call(
        functools.partial(_nsa_win_body, tq=tq, n_tiles=n_tiles),
        grid=(B, G, S // tq),
        in_specs=[
            pl.BlockSpec((1, tq, HPG * HEAD_DIM), lambda b, g_, i: (b, i, g_)),
            pl.BlockSpec((1, 1, nk, HEAD_DIM, tq), lambda b, g_, i: (b, g_, 0, 0, 0), pipeline_mode=once),
            pl.BlockSpec((1, S, LANES), lambda b, g_, i: (b, 0, v_block + g_), pipeline_mode=once),
            pl.BlockSpec((1, tq, LANES), lambda b, g_, i: (b, i, 0)),
        ],
        out_specs=pl.BlockSpec((1, tq, HPG * HEAD_DIM), lambda b, g_, i: (b, i, g_)),
        out_shape=jax.ShapeDtypeStruct((B, S, D_MODEL), BF16),
        scratch_shapes=[pltpu.VMEM((HPG * tq, n_tiles * tq), F32), pltpu.VMEM((HPG * tq, LANES), F32)],
        compiler_params=_cparams(("parallel", "parallel", "parallel")),
    )(proj3, kt, vaug3, g)


def _moba_kmean_body(kt_ref, et_ref, hi_ref, lo_ref):
    nk = kt_ref.shape[2]
    km = jnp.zeros((HEAD_DIM, LANES), F32)
    for ki in range(nk):
        km = km + _dot(kt_ref[0, 0, ki], et_ref[ki])
    hi, lo = _split_bf16(km * (1.0 / MOBA_BLOCK))
    hi_ref[0, 0] = hi
    lo_ref[0, 0] = lo


def _moba_kmean(kt, et):
    B, H, nk, _, tk = kt.shape
    spec = pl.BlockSpec((1, 1, HEAD_DIM, LANES), lambda b, h: (b, h, 0, 0))
    return pl.pallas_call(
        _moba_kmean_body,
        grid=(B, H),
        in_specs=[
            pl.BlockSpec((1, 1, nk, HEAD_DIM, tk), lambda b, h: (b, h, 0, 0, 0)),
            pl.BlockSpec((nk, tk, LANES), lambda b, h: (0, 0, 0)),
        ],
        out_specs=[spec, spec],
        out_shape=[jax.ShapeDtypeStruct((B, H, HEAD_DIM, LANES), BF16)] * 2,
        compiler_params=_cparams(("parallel", "parallel")),
    )(kt, et)


def _moba_body(q_ref, kt_ref, v_ref, e_ref, kh_ref, kl_ref, o_ref, qa_ref, s_ref, m_ref, alpha_ref,
               acc_ref, *, tq, tk, hb, nb, top_k):
    s0 = pl.program_id(2) * tq
    _flash_init(m_ref, acc_ref)

    blk = lax.broadcasted_iota(jnp.int32, (LANES, tq), 0)
    cb = (s0 + lax.broadcasted_iota(jnp.int32, (LANES, tq), 1)) // MOBA_BLOCK
    qs = q_ref[0]
    for h in range(hb):
        pair = qs[:, (h // 2) * LANES:(h // 2 + 1) * LANES]
        gsc = (_dot(pair, _head_kt(None, kh_ref[0, h], h % 2))
               + _dot(pair, _head_kt(None, kl_ref[0, h], h % 2))).T
        gsc = jnp.where(blk < cb, gsc, NEG)
        gsc = jnp.where(blk < nb, gsc, REMOVED)
        sel = (_topk_mask_t(gsc, top_k) & (blk < cb)) | (blk == cb)
        qa_ref[h * tq:(h + 1) * tq, :] = jnp.concatenate(
            [jnp.where(sel, 0.0, MASK).T.astype(BF16), pair], axis=1)

    def tile(ki, causal):
        k0 = pl.multiple_of(ki * tk, tk)
        e = e_ref[ki]
        jobs = []
        for h in range(hb):
            q = qa_ref[h * tq:(h + 1) * tq, :]
            kt = _head_kt(e, kt_ref[0, h, ki], h % 2)
            v = v_ref[0, pl.ds(k0, tk), h * LANES:(h + 1) * LANES]
            jobs += _diagonal_jobs(h * tq, q, kt, v, tq, tk) if causal else [(h * tq, q, kt, v, None)]
        _flash_step(jobs, s_ref, m_ref, alpha_ref, acc_ref)

    last = (s0 + tq - 1) // tk

    def full_tile(ki, carry):
        tile(ki, False)
        return carry

    lax.fori_loop(0, last, full_tile, 0)
    tile(last, True)
    o_ref[0] = _flash_pairs_out(acc_ref, tq, hb, [1.0] * hb).astype(o_ref.dtype)


def _moba_attn(proj3, kt, vaug3, e, kh, kl, nb, top_k, tq, tk, hb=4):
    B, S, _ = proj3.shape
    H = N_HEADS
    nk = S // tk
    assert tk == tq and tk % MOBA_BLOCK == 0 and H % hb == 0 and hb % 2 == 0 and nb <= LANES
    once = pl.Buffered(1)
    return pl.pallas_call(
        functools.partial(_moba_body, tq=tq, tk=tk, hb=hb, nb=nb, top_k=top_k),
        grid=(B, H // hb, S // tq),
        in_specs=[
            pl.BlockSpec((1, tq, hb * HEAD_DIM), lambda b, h, i: (b, i, h)),
            pl.BlockSpec((1, hb, nk, HEAD_DIM, tk), lambda b, h, i: (b, h, 0, 0, 0), pipeline_mode=once),
            pl.BlockSpec((1, S, hb * LANES), lambda b, h, i: (b, 0, h), pipeline_mode=once),
            pl.BlockSpec((nk, LANES, tk), lambda b, h, i: (0, 0, 0), pipeline_mode=once),
            pl.BlockSpec((1, hb, HEAD_DIM, LANES), lambda b, h, i: (b, h, 0, 0), pipeline_mode=once),
            pl.BlockSpec((1, hb, HEAD_DIM, LANES), lambda b, h, i: (b, h, 0, 0), pipeline_mode=once),
        ],
        out_specs=pl.BlockSpec((1, tq, hb * HEAD_DIM), lambda b, h, i: (b, i, h)),
        out_shape=jax.ShapeDtypeStruct((B, S, D_MODEL), BF16),
        scratch_shapes=[pltpu.VMEM((hb * tq, 2 * LANES), BF16)] + _flash_scratch(hb * tq, tk),
        compiler_params=_cparams(("parallel", "parallel", "parallel")),
    )(proj3, kt, vaug3, e, kh, kl)


def _moba_kmean_pairs_body(k_ref, e_ref, hi_ref, lo_ref, *, tk):
    nk = e_ref.shape[0]
    km = jnp.zeros((LANES, LANES), F32)
    for ki in range(nk):
        km = km + _dot(e_ref[ki], k_ref[0, ki * tk:(ki + 1) * tk, :])
    hi, lo = _split_bf16(km * (1.0 / MOBA_BLOCK))
    hi_ref[0, 0] = hi
    lo_ref[0, 0] = lo


def _moba_kmean_pairs(k3, e, tk):
    B, S, N = k3.shape
    n_pairs = N // LANES
    nk = S // tk
    spec = pl.BlockSpec((1, 1, LANES, LANES), lambda b, p: (b, p, 0, 0))
    return pl.pallas_call(
        functools.partial(_moba_kmean_pairs_body, tk=tk),
        grid=(B, n_pairs),
        in_specs=[
            pl.BlockSpec((1, S, LANES), lambda b, p: (b, 0, p)),
            pl.BlockSpec((nk, LANES, tk), lambda b, p: (0, 0, 0)),
        ],
        out_specs=[spec, spec],
        out_shape=[jax.ShapeDtypeStruct((B, n_pairs, LANES, LANES), BF16)] * 2,
        compiler_params=_cparams(("parallel", "parallel")),
    )(k3, e)


def _moba_t_body(qt_ref, k_ref, vt_ref, et_ref, kh_ref, kl_ref, o_ref, qa_ref, s_ref, m_ref, alpha_ref,
                 acc_ref, *, tq, tk, hb, nb, top_k):
    s0 = pl.program_id(2) * tq
    m_ref[...] = jnp.full(m_ref.shape, -jnp.inf, F32)
    acc_ref[...] = jnp.zeros(acc_ref.shape, F32)
    aug = 2 * LANES

    blk = lax.broadcasted_iota(jnp.int32, (LANES, tq), 0)
    cb = (s0 + lax.broadcasted_iota(jnp.int32, (LANES, tq), 1)) // MOBA_BLOCK
    for h in range(hb):
        qpad = _head_kt(None, qt_ref[0, h, 0], h % 2)
        gsc = _dot(kh_ref[0, h // 2], qpad) + _dot(kl_ref[0, h // 2], qpad)
        gsc = jnp.where(blk < cb, gsc, NEG)
        gsc = jnp.where(blk < nb, gsc, REMOVED)
        sel = (_topk_mask_t(gsc, top_k) & (blk < cb)) | (blk == cb)
        qa_ref[h * aug:(h + 1) * aug, :] = jnp.concatenate(
            [jnp.where(sel, 0.0, MASK).astype(BF16), qpad], axis=0)

    ones_rows = jnp.where(lax.broadcasted_iota(jnp.int32, (HEAD_DIM, tk), 0) == 0, 1.0, 0.0).astype(BF16)
    sub = LANES * 2

    def tile(ki, causal):
        k0 = pl.multiple_of(ki * tk, tk)
        et = et_ref[ki]
        krow = lax.broadcasted_iota(jnp.int32, (tk, tq), 0)
        qcol = lax.broadcasted_iota(jnp.int32, (tk, tq), 1)
        mask = (k0 + krow <= s0 + qcol) if causal else None

        def scores(h):
            kpair = k_ref[0, pl.ds(k0, tk), (h // 2) * LANES:(h // 2 + 1) * LANES]
            st = _dot(jnp.concatenate([et, kpair], axis=1), qa_ref[h * aug:(h + 1) * aug, :])
            if mask is not None:
                st = jnp.where(mask, st, MASK)
            s_ref[h * tk:(h + 1) * tk, :] = st
            m_prev = m_ref[h * 8:(h + 1) * 8, :]
            m_new = jnp.maximum(m_prev, jnp.max(st, axis=0, keepdims=True))
            alpha_ref[h * 8:(h + 1) * 8, :] = jnp.exp2(m_prev - m_new)
            m_ref[h * 8:(h + 1) * 8, :] = m_new

        def values(h):
            vaug = jnp.concatenate([vt_ref[0, h, ki], ones_rows], axis=0)
            for q0 in range(0, tq, sub):
                m = m_ref[h * 8:h * 8 + 1, q0:q0 + sub]
                part = jnp.zeros((LANES, sub), F32)
                for c0 in range(0, tk, sub):
                    pt = jnp.exp2(s_ref[h * tk + c0:h * tk + c0 + sub, q0:q0 + sub] - m).astype(BF16)
                    part = part + _dot(vaug[:, c0:c0 + sub], pt)
                acc_ref[h * LANES:(h + 1) * LANES, q0:q0 + sub] = (
                    alpha_ref[h * 8:h * 8 + 1, q0:q0 + sub] * acc_ref[h * LANES:(h + 1) * LANES, q0:q0 + sub]
                    + part)

        for h in range(hb):
            scores(h)
        for h in range(hb):
            values(h)

    last = (s0 + tq - 1) // tk

    def full_tile(ki, carry):
        tile(ki, False)
        return carry

    lax.fori_loop(0, last, full_tile, 0)
    tile(last, True)

    def head(h):
        acc = acc_ref[h * LANES:(h + 1) * LANES, :]
        return (acc / acc[HEAD_DIM:HEAD_DIM + 1, :]).T
    o_ref[0] = jnp.concatenate([_pair_merge(head(h), head(h + 1)) for h in range(0, hb, 2)],
                               axis=1).astype(o_ref.dtype)


def _moba_attn_t(qt, k3, vt, et, kh, kl, nb, top_k, tq, tk, hb=4):
    B, H, nq, _, _ = qt.shape
    S = k3.shape[1]
    nk = S // tk
    assert tk == tq and tk % MOBA_BLOCK == 0 and H % hb == 0 and hb % 2 == 0 and nb <= LANES
    once = pl.Buffered(1)
    return pl.pallas_call(
        functools.partial(_moba_t_body, tq=tq, tk=tk, hb=hb, nb=nb, top_k=top_k),
        grid=(B, H // hb, nq),
        in_specs=[
            pl.BlockSpec((1, hb, 1, HEAD_DIM, tq), lambda b, h, i: (b, h, i, 0, 0)),
            pl.BlockSpec((1, S, hb * HEAD_DIM), lambda b, h, i: (b, 0, h), pipeline_mode=once),
            pl.BlockSpec((1, hb, nk, HEAD_DIM, tk), lambda b, h, i: (b, h, 0, 0, 0), pipeline_mode=once),
            pl.BlockSpec((nk, tk, LANES), lambda b, h, i: (0, 0, 0), pipeline_mode=once),
            pl.BlockSpec((1, hb // 2, LANES, LANES), lambda b, h, i: (b, h, 0, 0), pipeline_mode=once),
            pl.BlockSpec((1, hb // 2, LANES, LANES), lambda b, h, i: (b, h, 0, 0), pipeline_mode=once),
        ],
        out_specs=pl.BlockSpec((1, tq, hb * HEAD_DIM), lambda b, h, i: (b, i, h)),
        out_shape=jax.ShapeDtypeStruct((B, S, D_MODEL), BF16),
        scratch_shapes=[
            pltpu.VMEM((hb * 2 * LANES, tq), BF16),
            pltpu.VMEM((hb * tk, tq), F32),
            pltpu.VMEM((hb * 8, tq), F32),
            pltpu.VMEM((hb * 8, tq), F32),
            pltpu.VMEM((hb * LANES, tq), F32),
        ],
        compiler_params=_cparams(("parallel", "parallel", "parallel")),
    )(qt, k3, vt, et, kh, kl)


def _layer_norm(r, g, b):
    mu = jnp.mean(r, axis=-1, keepdims=True)
    c = r - mu
    var = jnp.mean(c * c, axis=-1, keepdims=True)
    return c * lax.rsqrt(var + LN_EPS) * g + b


def _outproj_body(*refs, n_o):
    o_refs = refs[:n_o]
    w_ref, x_ref, g_ref, b_ref, y_ref, yb_ref = refs[n_o:]
    o = o_refs[0][...].astype(F32)
    for r in o_refs[1:]:
        o = o + r[...].astype(F32)
    mix = _dot(o.astype(BF16), w_ref[...])
    y = _layer_norm(DN_ALPHA * x_ref[...] + mix, g_ref[...], b_ref[...])
    y_ref[...] = y
    yb_ref[...] = y.astype(BF16)


def _outproj_ln(os_, w, x, g, b, tm=512):
    T, D = x.shape
    tm = min(tm, T)
    n_o = len(os_)
    row = pl.BlockSpec((tm, D), lambda i: (i, 0))
    vec = pl.BlockSpec((1, D), lambda i: (0, 0))
    return pl.pallas_call(
        functools.partial(_outproj_body, n_o=n_o),
        grid=(T // tm,),
        in_specs=[row] * n_o + [pl.BlockSpec((D, D), lambda i: (0, 0)), row, vec, vec],
        out_specs=[row, row],
        out_shape=[jax.ShapeDtypeStruct((T, D), F32), jax.ShapeDtypeStruct((T, D), BF16)],
        compiler_params=_cparams(("parallel",)),
    )(*os_, w, x, g.reshape(1, D), b.reshape(1, D))


def _router_body(x_ref, wh_ref, wl_ref, bias_ref, gate_ref):
    x_hi, x_lo = _split_bf16(x_ref[...])
    wh = wh_ref[...]
    logits = _dot(x_hi, wh) + _dot(x_lo, wh) + _dot(x_hi, wl_ref[...])
    scores = _sigmoid(logits)
    lane = lax.broadcasted_iota(jnp.int32, scores.shape, 1)
    lanef = lane.astype(F32)
    live = lane < N_EXPERTS
    biased = jnp.where(live, scores + bias_ref[...], REMOVED)

    def top2(mask):
        v = jnp.where(mask, biased, REMOVED)
        m1 = jnp.max(v, axis=-1, keepdims=True)
        i1 = jnp.min(jnp.where(v == m1, lanef, float(LANES)), axis=-1, keepdims=True)
        v2 = jnp.where(lanef == i1, REMOVED, v)
        m2 = jnp.max(v2, axis=-1, keepdims=True)
        i2 = jnp.min(jnp.where(v2 == m2, lanef, float(LANES)), axis=-1, keepdims=True)
        return m1 + m2, jnp.where((lanef == i1) | (lanef == i2), 1.0, 0.0)

    best, best_sel = top2(lane // EXPERTS_PER_GROUP == 0)
    for grp in range(1, N_GROUPS):
        score, sel = top2(lane // EXPERTS_PER_GROUP == grp)
        better = score > best
        best = jnp.where(better, score, best)
        best_sel = jnp.where(better, sel, best_sel)
    w = best_sel * scores
    gate_ref[...] = w / jnp.sum(w, axis=-1, keepdims=True)


def _router(x, router_w, router_bias, tm=512):
    T, D = x.shape
    tm = min(tm, T)
    wpad = jnp.zeros((D, LANES), F32).at[:, :N_EXPERTS].set(router_w)
    wh, wl = _split_bf16(wpad)
    bpad = jnp.zeros((1, LANES), F32).at[0, :N_EXPERTS].set(router_bias)
    return pl.pallas_call(
        _router_body,
        grid=(T // tm,),
        in_specs=[
            pl.BlockSpec((tm, D), lambda i: (i, 0)),
            pl.BlockSpec((D, LANES), lambda i: (0, 0)),
            pl.BlockSpec((D, LANES), lambda i: (0, 0)),
            pl.BlockSpec((1, LANES), lambda i: (0, 0)),
        ],
        out_specs=pl.BlockSpec((tm, LANES), lambda i: (i, 0)),
        out_shape=jax.ShapeDtypeStruct((T, LANES), F32),
        compiler_params=_cparams(("parallel",)),
    )(x, wh, wl, bpad)


def _experts_body(xb_ref, x_ref, gate_ref, wg_ref, wu_ref, wd_ref, g_ref, b_ref, y_ref, yb_ref, acc_ref):
    e = pl.program_id(1)

    @pl.when(e == 0)
    def _():
        acc_ref[...] = jnp.zeros(acc_ref.shape, F32)

    xb = xb_ref[...]
    a = _dot(xb, wg_ref[0])
    u = _dot(xb, wu_ref[0])
    lane = lax.broadcasted_iota(jnp.int32, gate_ref.shape, 1)
    gcol = jnp.sum(jnp.where(lane == e, gate_ref[...], 0.0), axis=-1, keepdims=True)
    h = a * _sigmoid(a) * u * gcol
    acc_ref[...] += _dot(h.astype(BF16), wd_ref[0])

    @pl.when(e == N_EXPERTS - 1)
    def _():
        y = _layer_norm(DN_ALPHA * x_ref[...] + acc_ref[...], g_ref[...], b_ref[...])
        y_ref[...] = y
        yb_ref[...] = y.astype(BF16)


def _experts_ln(xb, x, gate, wg, wu, wd, g, b, tm=1024):
    T, D = x.shape
    tm = min(tm, T)
    E, _, DE = wg.shape
    row = pl.BlockSpec((tm, D), lambda i, e: (i, 0))
    vec = pl.BlockSpec((1, D), lambda i, e: (0, 0))
    return pl.pallas_call(
        _experts_body,
        grid=(T // tm, E),
        in_specs=[
            row, row,
            pl.BlockSpec((tm, LANES), lambda i, e: (i, 0)),
            pl.BlockSpec((1, D, DE), lambda i, e: (e, 0, 0)),
            pl.BlockSpec((1, D, DE), lambda i, e: (e, 0, 0)),
            pl.BlockSpec((1, DE, D), lambda i, e: (e, 0, 0)),
            vec, vec,
        ],
        out_specs=[row, row],
        out_shape=[jax.ShapeDtypeStruct((T, D), F32), jax.ShapeDtypeStruct((T, D), BF16)],
        scratch_shapes=[pltpu.VMEM((tm, D), F32)],
        compiler_params=_cparams(("parallel", "arbitrary")),
    )(xb, x, gate, wg, wu, wd, g.reshape(1, D), b.reshape(1, D))


GID_LANE = N_EXPERTS
MOE_WINDOW = 1024
MOE_CHUNK = 128
MOE_VMEM_LIMIT = 60 * 1024 * 1024


def _router_sorted_body(x_ref, wh_ref, wl_ref, bias_ref, gate_ref, gidt_ref, cnt_ref):
    x_hi, x_lo = _split_bf16(x_ref[...])
    wh = wh_ref[...]
    logits = _dot(x_hi, wh) + _dot(x_lo, wh) + _dot(x_hi, wl_ref[...])
    scores = _sigmoid(logits)
    lane = lax.broadcasted_iota(jnp.int32, scores.shape, 1)
    lanef = lane.astype(F32)
    biased = jnp.where(lane < N_EXPERTS, scores + bias_ref[...], REMOVED)

    def top2(mask):
        v = jnp.where(mask, biased, REMOVED)
        m1 = jnp.max(v, axis=-1, keepdims=True)
        i1 = jnp.min(jnp.where(v == m1, lanef, float(LANES)), axis=-1, keepdims=True)
        v2 = jnp.where(lanef == i1, REMOVED, v)
        m2 = jnp.max(v2, axis=-1, keepdims=True)
        i2 = jnp.min(jnp.where(v2 == m2, lanef, float(LANES)), axis=-1, keepdims=True)
        return m1 + m2, jnp.where((lanef == i1) | (lanef == i2), 1.0, 0.0)

    best, best_sel = top2(lane // EXPERTS_PER_GROUP == 0)
    gid = jnp.zeros_like(best)
    for grp in range(1, N_GROUPS):
        score, sel = top2(lane // EXPERTS_PER_GROUP == grp)
        better = score > best
        best = jnp.where(better, score, best)
        best_sel = jnp.where(better, sel, best_sel)
        gid = jnp.where(better, float(grp), gid)
    w = best_sel * scores
    gate = w / jnp.sum(w, axis=-1, keepdims=True)
    gate_ref[...] = jnp.where(lane == GID_LANE, gid, gate)
    gid_b = jnp.broadcast_to(gid, scores.shape)
    gidt_ref[...] = gid_b.T[:8, :]
    cnt_ref[0] = jnp.broadcast_to(
        jnp.sum(jnp.where(lanef == gid_b, 1.0, 0.0), axis=0, keepdims=True), (8, LANES))


def _router_sorted(x, router_w, router_bias, tm):
    T, D = x.shape
    wpad = jnp.zeros((D, LANES), F32).at[:, :N_EXPERTS].set(router_w)
    wh, wl = _split_bf16(wpad)
    bpad = jnp.zeros((1, LANES), F32).at[0, :N_EXPERTS].set(router_bias)
    return pl.pallas_call(
        _router_sorted_body,
        grid=(T // tm,),
        in_specs=[
            pl.BlockSpec((tm, D), lambda i: (i, 0)),
            pl.BlockSpec((D, LANES), lambda i: (0, 0)),
            pl.BlockSpec((D, LANES), lambda i: (0, 0)),
            pl.BlockSpec((1, LANES), lambda i: (0, 0)),
        ],
        out_specs=[
            pl.BlockSpec((tm, LANES), lambda i: (i, 0)),
            pl.BlockSpec((8, tm), lambda i: (0, i)),
            pl.BlockSpec((1, 8, LANES), lambda i: (i, 0, 0)),
        ],
        out_shape=[
            jax.ShapeDtypeStruct((T, LANES), F32),
            jax.ShapeDtypeStruct((8, T), F32),
            jax.ShapeDtypeStruct((T // tm, 8, LANES), F32),
        ],
        compiler_params=_cparams(("parallel",)),
    )(x, wh, wl, bpad)


def _experts_sorted_body(cnt_ref, xb_ref, x_ref, gate_ref, gidt_ref, ltri_ref, utri_ref, wg_ref, wu_ref,
                         wd_ref, g_ref, b_ref, y_ref, yb_ref, xs_ref, gs_ref, acc_ref, rank_ref,
                         *, W, Wp, chunk):
    win = pl.program_id(0)
    e = pl.program_id(1)
    grp = e // EXPERTS_PER_GROUP
    padded = [((cnt_ref[win * N_GROUPS + g] + chunk - 1) // chunk) * chunk for g in range(N_GROUPS)]
    starts = [0]
    for g in range(N_GROUPS - 1):
        starts.append(starts[-1] + padded[g])
    start = starts[0]
    for g in range(1, N_GROUPS):
        start = jnp.where(grp == g, starts[g], start)
    n_chunks = (cnt_ref[win * N_GROUPS + grp] + chunk - 1) // chunk

    @pl.when(e == 0)
    def _():
        gate = gate_ref[...]
        lane = lax.broadcasted_iota(jnp.int32, gate.shape, 1)
        lanef = lane.astype(F32)
        gid = jnp.sum(jnp.where(lane == GID_LANE, gate, 0.0), axis=-1, keepdims=True)
        member = jnp.where((lanef == gid) & (lane < N_GROUPS), 1.0, 0.0)
        earlier = _dot(ltri_ref[...], member.astype(BF16))
        first = jnp.zeros(gate.shape, F32)
        for g in range(1, N_GROUPS):
            first = jnp.where(lane == g, starts[g].astype(F32), first)
        rank = jnp.sum(member * (first + earlier), axis=-1, keepdims=True)
        rank_ref[...] = jnp.broadcast_to(rank, gate.shape)

        gid_r = gidt_ref[...]
        sub = lax.broadcasted_iota(jnp.int32, gid_r.shape, 0)
        member_r = jnp.where(sub.astype(F32) == gid_r, 1.0, 0.0)
        earlier_r = _dot(member_r.astype(BF16), utri_ref[...])
        first_r = jnp.zeros(gid_r.shape, F32)
        for g in range(1, N_GROUPS):
            first_r = jnp.where(sub == g, starts[g].astype(F32), first_r)
        rank_r = jnp.sum(member_r * (first_r + earlier_r), axis=0, keepdims=True)
        rows = lax.broadcasted_iota(jnp.int32, (Wp, W), 0).astype(F32)
        perm = jnp.where(rows == rank_r, 1.0, 0.0).astype(BF16)
        xs_ref[...] = _dot(perm, xb_ref[...]).astype(BF16)
        g_hi, g_lo = _split_bf16(gate)
        gs_ref[...] = _dot(perm, g_hi) + _dot(perm, g_lo)
        acc_ref[...] = jnp.zeros(acc_ref.shape, F32)

    def expert_rows(r0, rows):
        r0 = pl.multiple_of(r0, chunk)
        xc = xs_ref[pl.ds(r0, rows), :]
        a = _dot(xc, wg_ref[0])
        u = _dot(xc, wu_ref[0])
        gs = gs_ref[pl.ds(r0, rows), :]
        lane = lax.broadcasted_iota(jnp.int32, gs.shape, 1)
        gcol = jnp.sum(jnp.where(lane == e, gs, 0.0), axis=-1, keepdims=True)
        h = a * _sigmoid(a) * u * gcol
        acc_ref[pl.ds(r0, rows), :] += _dot(h.astype(BF16), wd_ref[0])

    def two_chunks(i, carry):
        expert_rows(start + i * (2 * chunk), 2 * chunk)
        return carry

    lax.fori_loop(0, n_chunks // 2, two_chunks, 0)

    @pl.when(n_chunks % 2 == 1)
    def _():
        expert_rows(start + (n_chunks - 1) * chunk, chunk)

    @pl.when(e == N_EXPERTS - 1)
    def _():
        cols = lax.broadcasted_iota(jnp.int32, (W, Wp), 1).astype(F32)
        unperm = jnp.where(cols == rank_ref[...][:, :1], 1.0, 0.0).astype(BF16)
        ffn = _dot(unperm, acc_ref[...].astype(BF16))
        y = _layer_norm(DN_ALPHA * x_ref[...] + ffn, g_ref[...], b_ref[...])
        y_ref[...] = y
        yb_ref[...] = y.astype(BF16)


def _moe_ln(xb, x, router_w, router_bias, wg, wu, wd, g, b):
    T, D = x.shape
    W = min(MOE_WINDOW, T)
    chunk = MOE_CHUNK
    Wp = W + N_GROUPS * chunk
    E, _, DE = wg.shape
    assert T % W == 0 and W % chunk == 0
    gate, gidt, cnt = _router_sorted(x, router_w, router_bias, W)
    counts = cnt[:, 0, :N_GROUPS].astype(jnp.int32).reshape(-1)
    t = np.arange(W)
    ltri = jnp.asarray((t[None, :] < t[:, None]).astype(np.float32), BF16)
    once = pl.Buffered(1)
    row = lambda shape: pl.BlockSpec(shape, lambda i, e, c: (i, 0))
    vec = pl.BlockSpec((1, D), lambda i, e, c: (0, 0))
    tri = pl.BlockSpec((W, W), lambda i, e, c: (0, 0), pipeline_mode=once)
    grid_spec = pltpu.PrefetchScalarGridSpec(
        num_scalar_prefetch=1,
        grid=(T // W, E),
        in_specs=[
            row((W, D)),
            pl.BlockSpec((W, D), lambda i, e, c: (i, 0), pipeline_mode=once),
            row((W, LANES)),
            pl.BlockSpec((8, W), lambda i, e, c: (0, i)),
            tri, tri,
            pl.BlockSpec((1, D, DE), lambda i, e, c: (e, 0, 0)),
            pl.BlockSpec((1, D, DE), lambda i, e, c: (e, 0, 0)),
            pl.BlockSpec((1, DE, D), lambda i, e, c: (e, 0, 0)),
            vec, vec,
        ],
        out_specs=[row((W, D)), row((W, D))],
        scratch_shapes=[
            pltpu.VMEM((Wp, D), BF16),
            pltpu.VMEM((Wp, LANES), F32),
            pltpu.VMEM((Wp, D), F32),
            pltpu.VMEM((W, LANES), F32),
        ],
    )
    return pl.pallas_call(
        functools.partial(_experts_sorted_body, W=W, Wp=Wp, chunk=chunk),
        grid_spec=grid_spec,
        out_shape=[jax.ShapeDtypeStruct((T, D), F32), jax.ShapeDtypeStruct((T, D), BF16)],
        compiler_params=pltpu.CompilerParams(dimension_semantics=("parallel", "arbitrary"),
                                             vmem_limit_bytes=MOE_VMEM_LIMIT),
    )(counts, xb, x, gate, gidt, ltri, ltri.T, wg, wu, wd, g.reshape(1, D), b.reshape(1, D))


def _kt_tiles(t, B, S, n, tk):
    kt = t.reshape(B, S // tk, tk, n, HEAD_DIM)
    return kt.transpose(0, 3, 1, 4, 2)


def _block_onehots(S, block, tk):
    key = np.arange(S).reshape(S // tk, 1, tk)
    r = np.arange(LANES).reshape(1, LANES, 1)
    return jnp.asarray((key // block == r).astype(np.float32), BF16)


def _rope_tiled(S):
    cos, sin = _rope_tables(jnp.arange(S))
    reps = LANES // HALF
    return jnp.tile(cos, (1, reps)), jnp.tile(sin, (1, reps))


def _nsa_mixer(xb, B, S, w_in, cmp_k_w1, cmp_k_w2, cmp_v_w1, cmp_v_w2, cmp_k_pos, cmp_v_pos):
    G, HPG, KV = NSA_KV_GROUPS, NSA_HPG, NSA_KV_DIM
    L, STR, SB = NSA_CMP_LEN, NSA_CMP_STRIDE, NSA_SEL_BLOCK
    assert L == 2 * STR and S % SB == 0 and S // SB <= LANES
    T = B * S
    n_cmp = (S - L) // STR + 1
    NC = S // STR
    n_sel = S // SB
    top_n = min(NSA_SEL_TOPN, n_sel)

    cos2, sin2 = _rope_tiled(S)
    wb = w_in.astype(BF16)
    wcol = lambda i: wb[:, D_MODEL + i * KV: D_MODEL + (i + 1) * KV]
    tn = 2 * KV
    tk_s = min(512, S)
    tq_w = NSA_WINDOW // 2
    proj = _proj(xb, wb[:, :D_MODEL + 2 * KV], cos2, sin2, [2] * (D_MODEL // tn) + [0], S, tn=tn)
    kt_s = _proj_kt(xb, wcol(2), cos2, sin2, B, S, tk_s)
    kt_w = _proj_kt(xb, wcol(4), cos2, sin2, B, S, tq_w)
    vaug = _proj_vaug(xb, jnp.concatenate([wcol(3), wcol(5)], axis=1), tn=tn)
    wg = jnp.zeros((D_MODEL, LANES), BF16).at[:, :3 * N_HEADS].set(wb[:, D_MODEL + 6 * KV:])
    gates = _proj(xb, wg, cos2, sin2, [0], S, out_dtype=F32, tn=LANES).reshape(B, S, LANES)

    col = lambda i: proj[:, D_MODEL + i * KV: D_MODEL + (i + 1) * KV]
    proj3 = proj.reshape(B, S, proj.shape[1])
    vaug3 = vaug.reshape(B, S, vaug.shape[1])

    ccos, csin = _rope_tables(jnp.arange(NC) * STR + (L - 1))
    ccos = jnp.concatenate([ccos, ccos], axis=1)
    csin = jnp.concatenate([csin, csin], axis=1)
    to_rows = lambda t: t.reshape(B, S, G, HEAD_DIM).transpose(0, 2, 1, 3).reshape(B * G, NC, STR * HEAD_DIM)
    kc = _compress(to_rows(col(0)), cmp_k_w1, cmp_k_pos, cmp_k_w2, ccos, csin, True, n_cmp)
    vc = _compress(to_rows(col(1)), cmp_v_w1, cmp_v_pos, cmp_v_w2, ccos, csin, False, n_cmp)
    kct = kc.reshape(B, G, NC, HEAD_DIM).transpose(0, 1, 3, 2)

    ci = np.arange(NC)[:, None]
    sj = np.arange(LANES)[None, :]
    overlap = ((ci * STR < (sj + 1) * SB) & (ci * STR + L > sj * SB) & (ci < n_cmp) & (sj < n_sel))
    overlap = jnp.broadcast_to(jnp.asarray(overlap.astype(np.float32), BF16), (B, G, NC, LANES))
    vo = jnp.concatenate([vc.reshape(B, G, NC, HEAD_DIM), jnp.ones((B, G, NC, 1), BF16),
                          jnp.zeros((B, G, NC, LANES - HEAD_DIM - 1), BF16), overlap], axis=-1)

    o_cmp, selb = _nsa_cmp(proj3, kct, vo, gates, n_cmp, n_sel, top_n)
    o_slc = _nsa_slc(proj3, kt_s, vaug3, 0,
                     _block_onehots(S, SB, tk_s), selb, gates, tq=tk_s, tk=tk_s)
    o_win = _nsa_win(proj3, kt_w, vaug3, G, gates, tq=tq_w)
    return [o.reshape(T, D_MODEL) for o in (o_cmp, o_slc, o_win)]


def _moba_mixer(xb, B, S, w_in):
    H = N_HEADS
    nb = S // MOBA_BLOCK
    top_k = min(MOBA_TOPK, nb)
    cos2, sin2 = _rope_tiled(S)
    tn = 512
    n_t = D_MODEL // tn
    wb = w_in.astype(BF16)
    tk = min(2 * MOBA_BLOCK, S)
    qt = _proj_kt(xb, wb[:, :D_MODEL], cos2, sin2, B, S, tk, tn=tn, scale=Q_SCALE_LOG2)
    k3 = _proj(xb, wb[:, D_MODEL:2 * D_MODEL], cos2, sin2, [1] * n_t, S, tn=tn).reshape(B, S, D_MODEL)
    vt = _proj_kt(xb, wb[:, 2 * D_MODEL:], cos2, sin2, B, S, tk, tn=tn, rope=False)
    e = _block_onehots(S, MOBA_BLOCK, tk)
    kh, kl = _moba_kmean_pairs(k3, e, tk)
    o = _moba_attn_t(qt, k3, vt, e.transpose(0, 2, 1), kh, kl, nb, top_k, tq=tk, tk=tk)
    return [o.reshape(B * S, D_MODEL)]


def kernel(x, nsa_w_in, nsa_w_out, nsa_cmp_k_w1, nsa_cmp_k_w2, nsa_cmp_v_w1, nsa_cmp_v_w2, nsa_cmp_k_pos, nsa_cmp_v_pos, moba_w_in, moba_w_out, router_w, router_bias, moe_w_gate, moe_w_up, moe_w_down, ln_g, ln_b):
    B, S, D = x.shape
    xf = x.reshape(B * S, D)
    xb = xf.astype(BF16)
    for layer in range(DEPTH):
        j = layer // 2
        if layer % 2 == 0:
            os_ = _nsa_mixer(xb, B, S, nsa_w_in[j], nsa_cmp_k_w1[j], nsa_cmp_k_w2[j], nsa_cmp_v_w1[j],
                             nsa_cmp_v_w2[j], nsa_cmp_k_pos[j], nsa_cmp_v_pos[j])
            w_out = nsa_w_out[j]
        else:
            os_ = _moba_mixer(xb, B, S, moba_w_in[j])
            w_out = moba_w_out[j]
        xf, xb = _outproj_ln(os_, w_out.astype(BF16), xf, ln_g[layer, 0], ln_b[layer, 0])
        xf, xb = _moe_ln(xb, xf, router_w, router_bias, moe_w_gate[layer].astype(BF16),
                         moe_w_up[layer].astype(BF16), moe_w_down[layer].astype(BF16),
                         ln_g[layer, 1], ln_b[layer, 1])
    return xf.reshape(B, S, D)
```

```python
import functools

import jax
import jax.numpy as jnp
import numpy as np
from jax import lax
from jax.experimental import pallas as pl
from jax.experimental.pallas import tpu as pltpu

F32 = jnp.float32
BF16 = jnp.bfloat16

D_MODEL = 1024
N_HEADS = 16
HEAD_DIM = 64
HALF = HEAD_DIM // 2
ROPE_THETA = 10000.0
DEPTH = 2
DN_ALPHA = (2 * DEPTH) ** 0.25
LN_EPS = 1e-5
NEG = -1e30
FORCE = 1e9
MASK = -1e30
REMOVED = -3.0e38
LANES = 128
Q_SCALE_LOG2 = float(HEAD_DIM ** -0.5 * np.log2(np.e))
ROW_CHUNK = 128
PROJ_ROW_CHUNK = 256

NSA_KV_GROUPS = 4
NSA_HPG = N_HEADS // NSA_KV_GROUPS
NSA_KV_DIM = NSA_KV_GROUPS * HEAD_DIM
NSA_CMP_LEN = 32
NSA_CMP_STRIDE = 16
NSA_SEL_BLOCK = 64
NSA_SEL_TOPN = 16
NSA_WINDOW = 512

MOBA_BLOCK = 256
MOBA_TOPK = 3

N_EXPERTS = 16
N_GROUPS = 4
EXPERTS_PER_GROUP = N_EXPERTS // N_GROUPS
D_EXPERT = 512

VMEM_LIMIT = 48 * 1024 * 1024


def _cparams(sem):
    return pltpu.CompilerParams(dimension_semantics=sem, vmem_limit_bytes=VMEM_LIMIT)


def _dot(a, b):
    return jnp.dot(a, b, preferred_element_type=F32)


def _split_bf16(x):
    hi = x.astype(BF16)
    lo = (x - hi.astype(F32)).astype(BF16)
    return hi, lo


def _sigmoid(x):
    return 1.0 / (1.0 + jnp.exp(-x))


def _proj_body(mode_ref, x_ref, w_ref, cos_ref, sin_ref, o_ref, *, tn):
    mode = mode_ref[pl.program_id(0)]
    tm = x_ref.shape[0]
    rc = min(PROJ_ROW_CHUNK, tm)

    @pl.when(mode == 0)
    def _():
        for r in range(0, tm, rc):
            o_ref[r:r + rc, :] = _dot(x_ref[r:r + rc, :], w_ref[...]).astype(o_ref.dtype)

    @pl.when(mode != 0)
    def _():
        sc = jnp.where(mode == 2, Q_SCALE_LOG2, 1.0).astype(F32)
        for r in range(0, tm, rc):
            acc = _dot(x_ref[r:r + rc, :], w_ref[...])
            cos = cos_ref[r:r + rc, :] * sc
            sin = sin_ref[r:r + rc, :] * sc
            for c in range(tn // LANES):
                o_ref[r:r + rc, c * LANES:(c + 1) * LANES] = (
                    _rope_chunk(acc[:, c * LANES:(c + 1) * LANES], cos, sin).astype(o_ref.dtype))


def _proj(xb, w, cos2, sin2, modes, seq, out_dtype=BF16, tm=1024, tn=512):
    T, K = xb.shape
    N = w.shape[1]
    tm = min(tm, seq)
    assert T % tm == 0 and N % tn == 0 and seq % tm == 0 and len(modes) == N // tn
    n_pos = seq // tm
    grid_spec = pltpu.PrefetchScalarGridSpec(
        num_scalar_prefetch=1,
        grid=(N // tn, T // tm),
        in_specs=[
            pl.BlockSpec((tm, K), lambda j, i, m: (i, 0)),
            pl.BlockSpec((K, tn), lambda j, i, m: (0, j)),
            pl.BlockSpec((tm, LANES), lambda j, i, m: (i % n_pos, 0)),
            pl.BlockSpec((tm, LANES), lambda j, i, m: (i % n_pos, 0)),
        ],
        out_specs=pl.BlockSpec((tm, tn), lambda j, i, m: (i, j)),
    )
    return pl.pallas_call(
        functools.partial(_proj_body, tn=tn),
        grid_spec=grid_spec,
        out_shape=jax.ShapeDtypeStruct((T, N), out_dtype),
        compiler_params=_cparams(("parallel", "parallel")),
    )(jnp.asarray(modes, jnp.int32), xb, w, cos2, sin2)


def _rope_tables(pos):
    inv = 1.0 / (ROPE_THETA ** (jnp.arange(0, HEAD_DIM, 2, dtype=F32) / HEAD_DIM))
    ang = pos.astype(F32)[:, None] * inv[None, :]
    return jnp.cos(ang), jnp.sin(ang)


def _rope_chunk(a, cos, sin):
    lane = lax.broadcasted_iota(jnp.int32, a.shape, 1)
    up = pltpu.roll(a, LANES - HALF, 1)
    dn = pltpu.roll(a, HALF, 1)
    return a * cos + jnp.where((lane % HEAD_DIM) < HALF, -up, dn) * sin


def _proj_kt_body(x_ref, w_ref, cos_ref, sin_ref, o_ref, *, tn, tk, rope, scale):
    tm = x_ref.shape[0]
    rc = min(PROJ_ROW_CHUNK, tk)
    for r in range(0, tm, rc):
        acc = _dot(x_ref[r:r + rc, :], w_ref[...])
        cos = cos_ref[r:r + rc, :] * scale
        sin = sin_ref[r:r + rc, :] * scale
        for c in range(tn // LANES):
            a = acc[:, c * LANES:(c + 1) * LANES]
            kt = (_rope_chunk(a, cos, sin) if rope else a).T
            for hh in range(2):
                o_ref[0, 2 * c + hh, r // tk, :, r % tk:r % tk + rc] = (
                    kt[hh * HEAD_DIM:(hh + 1) * HEAD_DIM, :].astype(o_ref.dtype))


def _proj_kt(xb, w, cos2, sin2, B, S, tk, tm=1024, tn=256, rope=True, scale=1.0):
    T, K = xb.shape
    N = w.shape[1]
    tm = min(tm, S)
    tn = min(tn, N)
    assert S % tm == 0 and tm % tk == 0 and N % tn == 0
    n_pos = S // tm
    return pl.pallas_call(
        functools.partial(_proj_kt_body, tn=tn, tk=tk, rope=rope, scale=scale),
        grid=(N // tn, T // tm),
        in_specs=[
            pl.BlockSpec((tm, K), lambda j, i: (i, 0)),
            pl.BlockSpec((K, tn), lambda j, i: (0, j)),
            pl.BlockSpec((tm, LANES), lambda j, i: (i % n_pos, 0)),
            pl.BlockSpec((tm, LANES), lambda j, i: (i % n_pos, 0)),
        ],
        out_specs=pl.BlockSpec((1, tn // HEAD_DIM, tm // tk, HEAD_DIM, tk),
                               lambda j, i: (i // n_pos, j, i % n_pos, 0, 0)),
        out_shape=jax.ShapeDtypeStruct((B, N // HEAD_DIM, S // tk, HEAD_DIM, tk), BF16),
        compiler_params=_cparams(("parallel", "parallel")),
    )(xb, w, cos2, sin2)


def _proj_vaug_body(x_ref, w_ref, o_ref, *, tn):
    tm = x_ref.shape[0]
    rc = min(PROJ_ROW_CHUNK, tm)
    lane = lax.broadcasted_iota(jnp.int32, (rc, LANES), 1)
    tail = jnp.where(lane == HEAD_DIM, 1.0, 0.0)
    for r in range(0, tm, rc):
        acc = _dot(x_ref[r:r + rc, :], w_ref[...])
        for c in range(tn // LANES):
            a = acc[:, c * LANES:(c + 1) * LANES]
            for k, head in enumerate((a, pltpu.roll(a, HEAD_DIM, 1))):
                o_ref[r:r + rc, (2 * c + k) * LANES:(2 * c + k + 1) * LANES] = (
                    jnp.where(lane < HEAD_DIM, head, tail).astype(o_ref.dtype))


def _proj_vaug(xb, w, tm=1024, tn=512):
    T, K = xb.shape
    N = w.shape[1]
    tm = min(tm, T)
    assert T % tm == 0 and N % tn == 0
    return pl.pallas_call(
        functools.partial(_proj_vaug_body, tn=tn),
        grid=(N // tn, T // tm),
        in_specs=[pl.BlockSpec((tm, K), lambda j, i: (i, 0)), pl.BlockSpec((K, tn), lambda j, i: (0, j))],
        out_specs=pl.BlockSpec((tm, 2 * tn), lambda j, i: (i, j)),
        out_shape=jax.ShapeDtypeStruct((T, 2 * N), BF16),
        compiler_params=_cparams(("parallel", "parallel")),
    )(xb, w)


def _gelu_tanh(x):
    c = np.float32(np.sqrt(2.0 / np.pi))
    return 0.5 * x * (1.0 + jnp.tanh(c * (x + 0.044715 * (x * x * x))))


def _compress_body(r_ref, w1_ref, pos_ref, w2_ref, w2r_ref, cos_ref, sin_ref, o_ref, *, rope, n_cmp):
    r = r_ref[0]
    nc = r.shape[0]
    half = NSA_CMP_STRIDE * HEAD_DIM
    a = _dot(r, w1_ref[0])
    b = _dot(r, w1_ref[1])
    pos = pos_ref[...]
    pb = _dot(pos[:, :half], w1_ref[0]) + _dot(pos[:, half:], w1_ref[1])
    b_next = pltpu.roll(b, nc - 1, 0)
    h = _gelu_tanh(a + b_next + pb[0:1, :]).astype(BF16)
    o = _dot(h, w2_ref[...])
    if rope:
        o = o * cos_ref[...] + _dot(h, w2r_ref[...]) * sin_ref[...]
    row = lax.broadcasted_iota(jnp.int32, o.shape, 0)
    o_ref[0] = jnp.where(row < n_cmp, o, 0.0).astype(o_ref.dtype)


def _compress(r, w1, pos, w2, cos_c, sin_c, rope, n_cmp):
    BG, NC, K = r.shape
    hidden = w1.shape[1]
    w1s = w1.astype(BF16).reshape(2, K, hidden)
    pos8 = jnp.zeros((8, 2 * K), BF16).at[0].set(pos.reshape(-1).astype(BF16))
    w2r = jnp.concatenate([-w2[:, HALF:], w2[:, :HALF]], axis=1).astype(BF16)
    full = lambda shape: pl.BlockSpec(shape, lambda i: (0,) * len(shape))
    return pl.pallas_call(
        functools.partial(_compress_body, rope=rope, n_cmp=n_cmp),
        grid=(BG,),
        in_specs=[
            pl.BlockSpec((1, NC, K), lambda i: (i, 0, 0)),
            full((2, K, hidden)),
            full((8, 2 * K)),
            full((hidden, HEAD_DIM)),
            full((hidden, HEAD_DIM)),
            full((NC, HEAD_DIM)),
            full((NC, HEAD_DIM)),
        ],
        out_specs=pl.BlockSpec((1, NC, HEAD_DIM), lambda i: (i, 0, 0)),
        out_shape=jax.ShapeDtypeStruct((BG, NC, HEAD_DIM), BF16),
        compiler_params=_cparams(("parallel",)),
    )(r, w1s, pos8, w2.astype(BF16), w2r, cos_c, sin_c)


def _topk_mask_t(v, k):
    idx = lax.broadcasted_iota(jnp.int32, v.shape, 0).astype(F32)

    def step(_, cur):
        m = jnp.max(cur, axis=0, keepdims=True)
        first = jnp.min(jnp.where(cur == m, idx, float(LANES)), axis=0, keepdims=True)
        return jnp.where(idx == first, REMOVED, cur)

    return lax.fori_loop(0, k, step, v, unroll=True) != v


def _head_kt(e, kt, parity):
    z = jnp.zeros_like(kt)
    parts = ([] if e is None else [e]) + ([kt, z] if parity == 0 else [z, kt])
    return jnp.concatenate(parts, axis=0)


def _flash_init(m_ref, acc_ref):
    m_ref[...] = jnp.full(m_ref.shape, -jnp.inf, F32)
    acc_ref[...] = jnp.zeros(acc_ref.shape, F32)


def _flash_step(jobs, s_ref, m_ref, alpha_ref, acc_ref):
    for r0, q, kt, _, mask in jobs:
        rows = q.shape[0]
        s = _dot(q, kt)
        if mask is not None:
            s = jnp.where(mask, s, MASK)
        s_ref[r0:r0 + rows, :s.shape[1]] = s
        cm = s[:, :LANES]
        for c in range(1, s.shape[1] // LANES):
            cm = jnp.maximum(cm, s[:, c * LANES:(c + 1) * LANES])
        m_prev = m_ref[r0:r0 + rows, :]
        m_new = jnp.maximum(m_prev, jnp.max(cm, axis=-1, keepdims=True))
        alpha_ref[r0:r0 + rows, :] = jnp.exp2(m_prev - m_new)
        m_ref[r0:r0 + rows, :] = m_new
    for r0, q, _, v, _ in jobs:
        for r in range(r0, r0 + q.shape[0], ROW_CHUNK):
            rows = min(ROW_CHUNK, r0 + q.shape[0] - r)
            m = m_ref[r:r + rows, :]
            p = jnp.concatenate(
                [jnp.exp2(s_ref[r:r + rows, c * LANES:(c + 1) * LANES] - m)
                 for c in range(v.shape[0] // LANES)], axis=1).astype(BF16)
            acc_ref[r:r + rows, :] = alpha_ref[r:r + rows, :] * acc_ref[r:r + rows, :] + _dot(p, v)


def _diagonal_jobs(r0, q, kt, v, tq, tk):
    half = tq // 2
    row_a, col_a = _tile_iotas(half, half)
    row_b, col_b = _tile_iotas(half, tk)
    return [(r0, q[:half], kt[:, :half], v[:half], col_a <= row_a),
            (r0 + half, q[half:], kt, v, col_b <= half + row_b)]


def _flash_scratch(rows, tk):
    return [pltpu.VMEM((rows, tk), F32), pltpu.VMEM((rows, LANES), F32),
            pltpu.VMEM((rows, LANES), F32), pltpu.VMEM((rows, LANES), F32)]


def _pair_merge(even, odd):
    lane = lax.broadcasted_iota(jnp.int32, even.shape, 1)
    return jnp.where(lane < HEAD_DIM, even, pltpu.roll(odd, HEAD_DIM, 1))


def _flash_pairs_out(acc_ref, tq, n_heads, gates):
    def head(h):
        acc = acc_ref[h * tq:(h + 1) * tq, :]
        return acc * (gates[h] / acc[:, HEAD_DIM:HEAD_DIM + 1])
    return jnp.concatenate([_pair_merge(head(h), head(h + 1)) for h in range(0, n_heads, 2)], axis=1)


def _gate_col(logits, branch, h):
    idx = branch * N_HEADS + pl.program_id(1) * NSA_HPG + h
    lane = lax.broadcasted_iota(jnp.int32, logits.shape, 1)
    return _sigmoid(jnp.sum(jnp.where(lane == idx, logits, 0.0), axis=-1, keepdims=True))


def _tile_iotas(rows, tk):
    return (lax.broadcasted_iota(jnp.int32, (rows, tk), 0),
            lax.broadcasted_iota(jnp.int32, (rows, tk), 1))


def _nsa_cmp_body(q_ref, kct_ref, vo_ref, g_ref, o_ref, selb_ref, s_ref, m_ref, imp_ref,
                  *, tq, n_cmp, n_sel, top_n):
    s0 = pl.program_id(2) * tq
    q4 = q_ref[0]
    n_chunks = vo_ref.shape[2] // LANES
    rc = min(ROW_CHUNK, tq)

    def attend(nv):
        nc = nv * LANES
        kct = kct_ref[0, 0, :, :nc]
        vo = vo_ref[0, 0, :nc, :]
        tpos = s0 + lax.broadcasted_iota(jnp.int32, (tq, nc), 0)
        nidx = lax.broadcasted_iota(jnp.int32, (tq, nc), 1)
        cmask = (nidx * NSA_CMP_STRIDE + (NSA_CMP_LEN - 1) <= tpos) & (nidx < n_cmp)
        for h in range(NSA_HPG):
            pair = q4[:, (h // 2) * LANES:(h // 2 + 1) * LANES]
            s = jnp.where(cmask, _dot(pair, _head_kt(None, kct, h % 2)), NEG)
            s_ref[h * tq:(h + 1) * tq, :nc] = s
            cm = s[:, :LANES]
            for c in range(1, nv):
                cm = jnp.maximum(cm, s[:, c * LANES:(c + 1) * LANES])
            m_ref[h * tq:(h + 1) * tq, :] = jnp.broadcast_to(jnp.max(cm, axis=-1, keepdims=True), (tq, LANES))

        for r0 in range(0, tq, rc):
            imp = jnp.zeros((rc, LANES), F32)
            heads = []
            for h in range(NSA_HPG):
                r = h * tq + r0
                m = m_ref[r:r + rc, :]
                e = jnp.concatenate(
                    [jnp.exp2(s_ref[r:r + rc, c * LANES:(c + 1) * LANES] - m) for c in range(nv)],
                    axis=1).astype(BF16)
                res = _dot(e, vo)
                inv = jnp.where(m[:, :1] > 0.5 * NEG, 1.0 / res[:, HEAD_DIM:HEAD_DIM + 1], 0.0)
                gate = _gate_col(g_ref[0, r0:r0 + rc, :], 0, h)
                heads.append(res[:, :LANES] * (inv * gate))
                imp = imp + res[:, LANES:] * inv
            o_ref[0, r0:r0 + rc, :] = jnp.concatenate(
                [_pair_merge(heads[h], heads[h + 1]) for h in range(0, NSA_HPG, 2)], axis=1).astype(o_ref.dtype)
            imp_ref[r0:r0 + rc, :] = imp

    attend(n_chunks)
    imp = imp_ref[...]

    blk = lax.broadcasted_iota(jnp.int32, imp.shape, 1)
    jq = (s0 + lax.broadcasted_iota(jnp.int32, imp.shape, 0)) // NSA_SEL_BLOCK
    forced = (blk == 0) | (blk == jq) | (blk == jq - 1)
    imp = jnp.where(blk > jq, NEG, jnp.where(forced, FORCE, imp))
    imp = jnp.where(blk < n_sel, imp, REMOVED)
    sel_t = _topk_mask_t(imp.T, top_n)
    selb_ref[0, 0] = jnp.where(sel_t, 0.0, MASK).T.astype(selb_ref.dtype)


def _nsa_cmp(proj3, kct, vo, g_cmp, n_cmp, n_sel, top_n, tq=256):
    B, S, _ = proj3.shape
    G, HPG = NSA_KV_GROUPS, NSA_HPG
    NC = vo.shape[2]
    tq = min(tq, S)
    assert NC % LANES == 0
    return pl.pallas_call(
        functools.partial(_nsa_cmp_body, tq=tq, n_cmp=n_cmp, n_sel=n_sel, top_n=top_n),
        grid=(B, G, S // tq),
        in_specs=[
            pl.BlockSpec((1, tq, HPG * HEAD_DIM), lambda b, g, i: (b, i, g)),
            pl.BlockSpec((1, 1, HEAD_DIM, NC), lambda b, g, i: (b, g, 0, 0)),
            pl.BlockSpec((1, 1, NC, 2 * LANES), lambda b, g, i: (b, g, 0, 0)),
            pl.BlockSpec((1, tq, LANES), lambda b, g, i: (b, i, 0)),
        ],
        out_specs=[
            pl.BlockSpec((1, tq, HPG * HEAD_DIM), lambda b, g, i: (b, i, g)),
            pl.BlockSpec((1, 1, tq, LANES), lambda b, g, i: (b, g, i, 0)),
        ],
        out_shape=[
            jax.ShapeDtypeStruct((B, S, D_MODEL), BF16),
            jax.ShapeDtypeStruct((B, G, S, LANES), BF16),
        ],
        scratch_shapes=[pltpu.VMEM((HPG * tq, NC), F32), pltpu.VMEM((HPG * tq, LANES), F32),
                        pltpu.VMEM((tq, LANES), F32)],
        compiler_params=_cparams(("parallel", "parallel", "parallel")),
    )(proj3, kct, vo, g_cmp)


def _nsa_slc_body(q_ref, kt_ref, v_ref, e_ref, selb_ref, g_ref, o_ref, qa_ref, s_ref, m_ref, alpha_ref,
                  acc_ref, *, tq, tk):
    s0 = pl.program_id(2) * tq
    _flash_init(m_ref, acc_ref)
    selb = selb_ref[0, 0]
    q4 = q_ref[0]
    for h in range(NSA_HPG):
        qa_ref[h * tq:(h + 1) * tq, :] = jnp.concatenate(
            [selb, q4[:, (h // 2) * LANES:(h // 2 + 1) * LANES]], axis=1)

    def tile(ki, causal):
        k0 = pl.multiple_of(ki * tk, tk)
        kts = [_head_kt(e_ref[ki], kt_ref[0, 0, ki], parity) for parity in range(2)]
        v = v_ref[0, pl.ds(k0, tk), :]
        jobs = []
        for h in range(NSA_HPG):
            q = qa_ref[h * tq:(h + 1) * tq, :]
            jobs += _diagonal_jobs(h * tq, q, kts[h % 2], v, tq, tk) if causal else [(h * tq, q, kts[h % 2], v, None)]
        _flash_step(jobs, s_ref, m_ref, alpha_ref, acc_ref)

    last = (s0 + tq - 1) // tk

    def full_tile(ki, carry):
        tile(ki, False)
        return carry

    lax.fori_loop(0, last, full_tile, 0)
    tile(last, True)
    gates = [_gate_col(g_ref[0], 1, h) for h in range(NSA_HPG)]
    o_ref[0] = _flash_pairs_out(acc_ref, tq, NSA_HPG, gates).astype(o_ref.dtype)


def _nsa_slc(proj3, kt, vaug3, v_block, e, selb, g, tq, tk):
    B, S, _ = proj3.shape
    G, HPG = NSA_KV_GROUPS, NSA_HPG
    nk = S // tk
    assert tk == tq and kt.shape == (B, G, nk, HEAD_DIM, tk) and e.shape == (nk, LANES, tk)
    once = pl.Buffered(1)
    return pl.pallas_call(
        functools.partial(_nsa_slc_body, tq=tq, tk=tk),
        grid=(B, G, S // tq),
        in_specs=[
            pl.BlockSpec((1, tq, HPG * HEAD_DIM), lambda b, g_, i: (b, i, g_)),
            pl.BlockSpec((1, 1, nk, HEAD_DIM, tk), lambda b, g_, i: (b, g_, 0, 0, 0), pipeline_mode=once),
            pl.BlockSpec((1, S, LANES), lambda b, g_, i: (b, 0, v_block + g_), pipeline_mode=once),
            pl.BlockSpec((nk, LANES, tk), lambda b, g_, i: (0, 0, 0), pipeline_mode=once),
            pl.BlockSpec((1, 1, tq, LANES), lambda b, g_, i: (b, g_, i, 0)),
            pl.BlockSpec((1, tq, LANES), lambda b, g_, i: (b, i, 0)),
        ],
        out_specs=pl.BlockSpec((1, tq, HPG * HEAD_DIM), lambda b, g_, i: (b, i, g_)),
        out_shape=jax.ShapeDtypeStruct((B, S, D_MODEL), BF16),
        scratch_shapes=[pltpu.VMEM((HPG * tq, 2 * LANES), BF16)] + _flash_scratch(HPG * tq, tk),
        compiler_params=_cparams(("parallel", "parallel", "parallel")),
    )(proj3, kt, vaug3, e, selb, g)


def _nsa_win_body(q_ref, kt_ref, v_ref, g_ref, o_ref, s_ref, m_ref, *, tq, n_tiles):
    s0 = pl.program_id(2) * tq
    tk = n_tiles * tq
    k_first = jnp.maximum(pl.program_id(2) - (n_tiles - 1), 0)
    k0 = pl.multiple_of(k_first * tq, tq)
    kt = jnp.concatenate([kt_ref[0, 0, k_first + j] for j in range(n_tiles)], axis=1)
    v = v_ref[0, pl.ds(k0, tk), :]
    row, col = _tile_iotas(tq, tk)
    dist = (s0 - k0) + row - col
    valid = (dist >= 0) & (dist < NSA_WINDOW)
    q4 = q_ref[0]
    for h in range(NSA_HPG):
        pair = q4[:, (h // 2) * LANES:(h // 2 + 1) * LANES]
        s = jnp.where(valid, _dot(pair, _head_kt(None, kt, h % 2)), MASK)
        s_ref[h * tq:(h + 1) * tq, :] = s
        cm = s[:, :LANES]
        for c in range(1, tk // LANES):
            cm = jnp.maximum(cm, s[:, c * LANES:(c + 1) * LANES])
        m_ref[h * tq:(h + 1) * tq, :] = jnp.broadcast_to(jnp.max(cm, axis=-1, keepdims=True), (tq, LANES))

    rc = min(ROW_CHUNK, tq)
    for r0 in range(0, tq, rc):
        heads = []
        for h in range(NSA_HPG):
            r = h * tq + r0
            m = m_ref[r:r + rc, :]
            p = jnp.concatenate(
                [jnp.exp2(s_ref[r:r + rc, c * LANES:(c + 1) * LANES] - m) for c in range(tk // LANES)],
                axis=1).astype(BF16)
            res = _dot(p, v)
            heads.append(res * (_gate_col(g_ref[0, r0:r0 + rc, :], 2, h) / res[:, HEAD_DIM:HEAD_DIM + 1]))
        o_ref[0, r0:r0 + rc, :] = jnp.concatenate(
            [_pair_merge(heads[h], heads[h + 1]) for h in range(0, NSA_HPG, 2)], axis=1).astype(o_ref.dtype)


def _nsa_win(proj3, kt, vaug3, v_block, g, tq):
    B, S, _ = proj3.shape
    G, HPG = NSA_KV_GROUPS, NSA_HPG
    nk = S // tq
    n_tiles = NSA_WINDOW // tq + 1
    assert NSA_WINDOW % tq == 0 and nk >= n_tiles and kt.shape == (B, G, nk, HEAD_DIM, tq)
    once = pl.Buffered(1)
    return pl.pallas_call(
        functools.partial(_nsa_win_body, tq=tq, n_tiles=n_tiles),
        grid=(B, G, S // tq),
        in_specs=[
            pl.BlockSpec((1, tq, HPG * HEAD_DIM), lambda b, g_, i: (b, i, g_)),
            pl.BlockSpec((1, 1, nk, HEAD_DIM, tq), lambda b, g_, i: (b, g_, 0, 0, 0), pipeline_mode=once),
            pl.BlockSpec((1, S, LANES), lambda b, g_, i: (b, 0, v_block + g_), pipeline_mode=once),
            pl.BlockSpec((1, tq, LANES), lambda b, g_, i: (b, i, 0)),
        ],
        out_specs=pl.BlockSpec((1, tq, HPG * HEAD_DIM), lambda b, g_, i: (b, i, g_)),
        out_shape=jax.ShapeDtypeStruct((B, S, D_MODEL), BF16),
        scratch_shapes=[pltpu.VMEM((HPG * tq, n_tiles * tq), F32), pltpu.VMEM((HPG * tq, LANES), F32)],
        compiler_params=_cparams(("parallel", "parallel", "parallel")),
    )(proj3, kt, vaug3, g)


def _moba_kmean_body(kt_ref, et_ref, hi_ref, lo_ref):
    nk = kt_ref.shape[2]
    km = jnp.zeros((HEAD_DIM, LANES), F32)
    for ki in range(nk):
        km = km + _dot(kt_ref[0, 0, ki], et_ref[ki])
    hi, lo = _split_bf16(km * (1.0 / MOBA_BLOCK))
    hi_ref[0, 0] = hi
    lo_ref[0, 0] = lo


def _moba_kmean(kt, et):
    B, H, nk, _, tk = kt.shape
    spec = pl.BlockSpec((1, 1, HEAD_DIM, LANES), lambda b, h: (b, h, 0, 0))
    return pl.pallas_call(
        _moba_kmean_body,
        grid=(B, H),
        in_specs=[
            pl.BlockSpec((1, 1, nk, HEAD_DIM, tk), lambda b, h: (b, h, 0, 0, 0)),
            pl.BlockSpec((nk, tk, LANES), lambda b, h: (0, 0, 0)),
        ],
        out_specs=[spec, spec],
        out_shape=[jax.ShapeDtypeStruct((B, H, HEAD_DIM, LANES), BF16)] * 2,
        compiler_params=_cparams(("parallel", "parallel")),
    )(kt, et)


def _moba_body(q_ref, kt_ref, v_ref, e_ref, kh_ref, kl_ref, o_ref, qa_ref, s_ref, m_ref, alpha_ref,
               acc_ref, *, tq, tk, hb, nb, top_k):
    s0 = pl.program_id(2) * tq
    _flash_init(m_ref, acc_ref)

    blk = lax.broadcasted_iota(jnp.int32, (LANES, tq), 0)
    cb = (s0 + lax.broadcasted_iota(jnp.int32, (LANES, tq), 1)) // MOBA_BLOCK
    qs = q_ref[0]
    for h in range(hb):
        pair = qs[:, (h // 2) * LANES:(h // 2 + 1) * LANES]
        gsc = (_dot(pair, _head_kt(None, kh_ref[0, h], h % 2))
               + _dot(pair, _head_kt(None, kl_ref[0, h], h % 2))).T
        gsc = jnp.where(blk < cb, gsc, NEG)
        gsc = jnp.where(blk < nb, gsc, REMOVED)
        sel = (_topk_mask_t(gsc, top_k) & (blk < cb)) | (blk == cb)
        qa_ref[h * tq:(h + 1) * tq, :] = jnp.concatenate(
            [jnp.where(sel, 0.0, MASK).T.astype(BF16), pair], axis=1)

    def tile(ki, causal):
        k0 = pl.multiple_of(ki * tk, tk)
        e = e_ref[ki]
        jobs = []
        for h in range(hb):
            q = qa_ref[h * tq:(h + 1) * tq, :]
            kt = _head_kt(e, kt_ref[0, h, ki], h % 2)
            v = v_ref[0, pl.ds(k0, tk), h * LANES:(h + 1) * LANES]
            jobs += _diagonal_jobs(h * tq, q, kt, v, tq, tk) if causal else [(h * tq, q, kt, v, None)]
        _flash_step(jobs, s_ref, m_ref, alpha_ref, acc_ref)

    last = (s0 + tq - 1) // tk

    def full_tile(ki, carry):
        tile(ki, False)
        return carry

    lax.fori_loop(0, last, full_tile, 0)
    tile(last, True)
    o_ref[0] = _flash_pairs_out(acc_ref, tq, hb, [1.0] * hb).astype(o_ref.dtype)


def _moba_attn(proj3, kt, vaug3, e, kh, kl, nb, top_k, tq, tk, hb=4):
    B, S, _ = proj3.shape
    H = N_HEADS
    nk = S // tk
    assert tk == tq and tk % MOBA_BLOCK == 0 and H % hb == 0 and hb % 2 == 0 and nb <= LANES
    once = pl.Buffered(1)
    return pl.pallas_call(
        functools.partial(_moba_body, tq=tq, tk=tk, hb=hb, nb=nb, top_k=top_k),
        grid=(B, H // hb, S // tq),
        in_specs=[
            pl.BlockSpec((1, tq, hb * HEAD_DIM), lambda b, h, i: (b, i, h)),
            pl.BlockSpec((1, hb, nk, HEAD_DIM, tk), lambda b, h, i: (b, h, 0, 0, 0), pipeline_mode=once),
            pl.BlockSpec((1, S, hb * LANES), lambda b, h, i: (b, 0, h), pipeline_mode=once),
            pl.BlockSpec((nk, LANES, tk), lambda b, h, i: (0, 0, 0), pipeline_mode=once),
            pl.BlockSpec((1, hb, HEAD_DIM, LANES), lambda b, h, i: (b, h, 0, 0), pipeline_mode=once),
            pl.BlockSpec((1, hb, HEAD_DIM, LANES), lambda b, h, i: (b, h, 0, 0), pipeline_mode=once),
        ],
        out_specs=pl.BlockSpec((1, tq, hb * HEAD_DIM), lambda b, h, i: (b, i, h)),
        out_shape=jax.ShapeDtypeStruct((B, S, D_MODEL), BF16),
        scratch_shapes=[pltpu.VMEM((hb * tq, 2 * LANES), BF16)] + _flash_scratch(hb * tq, tk),
        compiler_params=_cparams(("parallel", "parallel", "parallel")),
    )(proj3, kt, vaug3, e, kh, kl)


def _moba_kmean_pairs_body(k_ref, e_ref, hi_ref, lo_ref, *, tk):
    nk = e_ref.shape[0]
    km = jnp.zeros((LANES, LANES), F32)
    for ki in range(nk):
        km = km + _dot(e_ref[ki], k_ref[0, ki * tk:(ki + 1) * tk, :])
    hi, lo = _split_bf16(km * (1.0 / MOBA_BLOCK))
    hi_ref[0, 0] = hi
    lo_ref[0, 0] = lo


def _moba_kmean_pairs(k3, e, tk):
    B, S, N = k3.shape
    n_pairs = N // LANES
    nk = S // tk
    spec = pl.BlockSpec((1, 1, LANES, LANES), lambda b, p: (b, p, 0, 0))
    return pl.pallas_call(
        functools.partial(_moba_kmean_pairs_body, tk=tk),
        grid=(B, n_pairs),
        in_specs=[
            pl.BlockSpec((1, S, LANES), lambda b, p: (b, 0, p)),
            pl.BlockSpec((nk, LANES, tk), lambda b, p: (0, 0, 0)),
        ],
        out_specs=[spec, spec],
        out_shape=[jax.ShapeDtypeStruct((B, n_pairs, LANES, LANES), BF16)] * 2,
        compiler_params=_cparams(("parallel", "parallel")),
    )(k3, e)


def _moba_t_body(qt_ref, k_ref, vt_ref, et_ref, kh_ref, kl_ref, o_ref, qa_ref, s_ref, m_ref, alpha_ref,
                 acc_ref, *, tq, tk, hb, nb, top_k):
    s0 = pl.program_id(2) * tq
    m_ref[...] = jnp.full(m_ref.shape, -jnp.inf, F32)
    acc_ref[...] = jnp.zeros(acc_ref.shape, F32)
    aug = 2 * LANES

    blk = lax.broadcasted_iota(jnp.int32, (LANES, tq), 0)
    cb = (s0 + lax.broadcasted_iota(jnp.int32, (LANES, tq), 1)) // MOBA_BLOCK
    for h in range(hb):
        qpad = _head_kt(None, qt_ref[0, h, 0], h % 2)
        gsc = _dot(kh_ref[0, h // 2], qpad) + _dot(kl_ref[0, h // 2], qpad)
        gsc = jnp.where(blk < cb, gsc, NEG)
        gsc = jnp.where(blk < nb, gsc, REMOVED)
        sel = (_topk_mask_t(gsc, top_k) & (blk < cb)) | (blk == cb)
        qa_ref[h * aug:(h + 1) * aug, :] = jnp.concatenate(
            [jnp.where(sel, 0.0, MASK).astype(BF16), qpad], axis=0)

    ones_rows = jnp.where(lax.broadcasted_iota(jnp.int32, (HEAD_DIM, tk), 0) == 0, 1.0, 0.0).astype(BF16)
    sub = LANES * 2

    def tile(ki, causal):
        k0 = pl.multiple_of(ki * tk, tk)
        et = et_ref[ki]
        krow = lax.broadcasted_iota(jnp.int32, (tk, tq), 0)
        qcol = lax.broadcasted_iota(jnp.int32, (tk, tq), 1)
        mask = (k0 + krow <= s0 + qcol) if causal else None

        def scores(h):
            kpair = k_ref[0, pl.ds(k0, tk), (h // 2) * LANES:(h // 2 + 1) * LANES]
            st = _dot(jnp.concatenate([et, kpair], axis=1), qa_ref[h * aug:(h + 1) * aug, :])
            if mask is not None:
                st = jnp.where(mask, st, MASK)
            s_ref[h * tk:(h + 1) * tk, :] = st
            m_prev = m_ref[h * 8:(h + 1) * 8, :]
            m_new = jnp.maximum(m_prev, jnp.max(st, axis=0, keepdims=True))
            alpha_ref[h * 8:(h + 1) * 8, :] = jnp.exp2(m_prev - m_new)
            m_ref[h * 8:(h + 1) * 8, :] = m_new

        def values(h):
            vaug = jnp.concatenate([vt_ref[0, h, ki], ones_rows], axis=0)
            for q0 in range(0, tq, sub):
                m = m_ref[h * 8:h * 8 + 1, q0:q0 + sub]
                part = jnp.zeros((LANES, sub), F32)
                for c0 in range(0, tk, sub):
                    pt = jnp.exp2(s_ref[h * tk + c0:h * tk + c0 + sub, q0:q0 + sub] - m).astype(BF16)
                    part = part + _dot(vaug[:, c0:c0 + sub], pt)
                acc_ref[h * LANES:(h + 1) * LANES, q0:q0 + sub] = (
                    alpha_ref[h * 8:h * 8 + 1, q0:q0 + sub] * acc_ref[h * LANES:(h + 1) * LANES, q0:q0 + sub]
                    + part)

        for h in range(hb):
            scores(h)
        for h in range(hb):
            values(h)

    last = (s0 + tq - 1) // tk

    def full_tile(ki, carry):
        tile(ki, False)
        return carry

    lax.fori_loop(0, last, full_tile, 0)
    tile(last, True)

    def head(h):
        acc = acc_ref[h * LANES:(h + 1) * LANES, :]
        return (acc / acc[HEAD_DIM:HEAD_DIM + 1, :]).T
    o_ref[0] = jnp.concatenate([_pair_merge(head(h), head(h + 1)) for h in range(0, hb, 2)],
                               axis=1).astype(o_ref.dtype)


def _moba_attn_t(qt, k3, vt, et, kh, kl, nb, top_k, tq, tk, hb=4):
    B, H, nq, _, _ = qt.shape
    S = k3.shape[1]
    nk = S // tk
    assert tk == tq and tk % MOBA_BLOCK == 0 and H % hb == 0 and hb % 2 == 0 and nb <= LANES
    once = pl.Buffered(1)
    return pl.pallas_call(
        functools.partial(_moba_t_body, tq=tq, tk=tk, hb=hb, nb=nb, top_k=top_k),
        grid=(B, H // hb, nq),
        in_specs=[
            pl.BlockSpec((1, hb, 1, HEAD_DIM, tq), lambda b, h, i: (b, h, i, 0, 0)),
            pl.BlockSpec((1, S, hb * HEAD_DIM), lambda b, h, i: (b, 0, h), pipeline_mode=once),
            pl.BlockSpec((1, hb, nk, HEAD_DIM, tk), lambda b, h, i: (b, h, 0, 0, 0), pipeline_mode=once),
            pl.BlockSpec((nk, tk, LANES), lambda b, h, i: (0, 0, 0), pipeline_mode=once),
            pl.BlockSpec((1, hb // 2, LANES, LANES), lambda b, h, i: (b, h, 0, 0), pipeline_mode=once),
            pl.BlockSpec((1, hb // 2, LANES, LANES), lambda b, h, i: (b, h, 0, 0), pipeline_mode=once),
        ],
        out_specs=pl.BlockSpec((1, tq, hb * HEAD_DIM), lambda b, h, i: (b, i, h)),
        out_shape=jax.ShapeDtypeStruct((B, S, D_MODEL), BF16),
        scratch_shapes=[
            pltpu.VMEM((hb * 2 * LANES, tq), BF16),
            pltpu.VMEM((hb * tk, tq), F32),
            pltpu.VMEM((hb * 8, tq), F32),
            pltpu.VMEM((hb * 8, tq), F32),
            pltpu.VMEM((hb * LANES, tq), F32),
        ],
        compiler_params=_cparams(("parallel", "parallel", "parallel")),
    )(qt, k3, vt, et, kh, kl)


def _layer_norm(r, g, b):
    mu = jnp.mean(r, axis=-1, keepdims=True)
    c = r - mu
    var = jnp.mean(c * c, axis=-1, keepdims=True)
    return c * lax.rsqrt(var + LN_EPS) * g + b


def _outproj_body(*refs, n_o):
    o_refs = refs[:n_o]
    w_ref, x_ref, g_ref, b_ref, y_ref, yb_ref = refs[n_o:]
    tm = x_ref.shape[0]
    rc = min(PROJ_ROW_CHUNK, tm)
    for r0 in range(0, tm, rc):
        rows = slice(r0, r0 + rc)
        o = o_refs[0][rows, :].astype(F32)
        for ref in o_refs[1:]:
            o = o + ref[rows, :].astype(F32)
        mix = _dot(o.astype(BF16), w_ref[...])
        y = _layer_norm(DN_ALPHA * x_ref[rows, :] + mix, g_ref[...], b_ref[...])
        y_ref[rows, :] = y
        yb_ref[rows, :] = y.astype(BF16)


def _outproj_ln(os_, w, x, g, b, tm=512):
    T, D = x.shape
    tm = min(tm, T)
    n_o = len(os_)
    row = pl.BlockSpec((tm, D), lambda i: (i, 0))
    vec = pl.BlockSpec((1, D), lambda i: (0, 0))
    return pl.pallas_call(
        functools.partial(_outproj_body, n_o=n_o),
        grid=(T // tm,),
        in_specs=[row] * n_o + [pl.BlockSpec((D, D), lambda i: (0, 0)), row, vec, vec],
        out_specs=[row, row],
        out_shape=[jax.ShapeDtypeStruct((T, D), F32), jax.ShapeDtypeStruct((T, D), BF16)],
        compiler_params=_cparams(("parallel",)),
    )(*os_, w, x, g.reshape(1, D), b.reshape(1, D))


def _router_body(x_ref, wh_ref, wl_ref, bias_ref, gate_ref):
    x_hi, x_lo = _split_bf16(x_ref[...])
    wh = wh_ref[...]
    logits = _dot(x_hi, wh) + _dot(x_lo, wh) + _dot(x_hi, wl_ref[...])
    scores = _sigmoid(logits)
    lane = lax.broadcasted_iota(jnp.int32, scores.shape, 1)
    lanef = lane.astype(F32)
    live = lane < N_EXPERTS
    biased = jnp.where(live, scores + bias_ref[...], REMOVED)

    def top2(mask):
        v = jnp.where(mask, biased, REMOVED)
        m1 = jnp.max(v, axis=-1, keepdims=True)
        i1 = jnp.min(jnp.where(v == m1, lanef, float(LANES)), axis=-1, keepdims=True)
        v2 = jnp.where(lanef == i1, REMOVED, v)
        m2 = jnp.max(v2, axis=-1, keepdims=True)
        i2 = jnp.min(jnp.where(v2 == m2, lanef, float(LANES)), axis=-1, keepdims=True)
        return m1 + m2, jnp.where((lanef == i1) | (lanef == i2), 1.0, 0.0)

    best, best_sel = top2(lane // EXPERTS_PER_GROUP == 0)
    for grp in range(1, N_GROUPS):
        score, sel = top2(lane // EXPERTS_PER_GROUP == grp)
        better = score > best
        best = jnp.where(better, score, best)
        best_sel = jnp.where(better, sel, best_sel)
    w = best_sel * scores
    gate_ref[...] = w / jnp.sum(w, axis=-1, keepdims=True)


def _router(x, router_w, router_bias, tm=512):
    T, D = x.shape
    tm = min(tm, T)
    wpad = jnp.zeros((D, LANES), F32).at[:, :N_EXPERTS].set(router_w)
    wh, wl = _split_bf16(wpad)
    bpad = jnp.zeros((1, LANES), F32).at[0, :N_EXPERTS].set(router_bias)
    return pl.pallas_call(
        _router_body,
        grid=(T // tm,),
        in_specs=[
            pl.BlockSpec((tm, D), lambda i: (i, 0)),
            pl.BlockSpec((D, LANES), lambda i: (0, 0)),
            pl.BlockSpec((D, LANES), lambda i: (0, 0)),
            pl.BlockSpec((1, LANES), lambda i: (0, 0)),
        ],
        out_specs=pl.BlockSpec((tm, LANES), lambda i: (i, 0)),
        out_shape=jax.ShapeDtypeStruct((T, LANES), F32),
        compiler_params=_cparams(("parallel",)),
    )(x, wh, wl, bpad)


def _experts_body(xb_ref, x_ref, gate_ref, wg_ref, wu_ref, wd_ref, g_ref, b_ref, y_ref, yb_ref, acc_ref):
    e = pl.program_id(1)

    @pl.when(e == 0)
    def _():
        acc_ref[...] = jnp.zeros(acc_ref.shape, F32)

    xb = xb_ref[...]
    a = _dot(xb, wg_ref[0])
    u = _dot(xb, wu_ref[0])
    lane = lax.broadcasted_iota(jnp.int32, gate_ref.shape, 1)
    gcol = jnp.sum(jnp.where(lane == e, gate_ref[...], 0.0), axis=-1, keepdims=True)
    h = a * _sigmoid(a) * u * gcol
    acc_ref[...] += _dot(h.astype(BF16), wd_ref[0])

    @pl.when(e == N_EXPERTS - 1)
    def _():
        y = _layer_norm(DN_ALPHA * x_ref[...] + acc_ref[...], g_ref[...], b_ref[...])
        y_ref[...] = y
        yb_ref[...] = y.astype(BF16)


def _experts_ln(xb, x, gate, wg, wu, wd, g, b, tm=1024):
    T, D = x.shape
    tm = min(tm, T)
    E, _, DE = wg.shape
    row = pl.BlockSpec((tm, D), lambda i, e: (i, 0))
    vec = pl.BlockSpec((1, D), lambda i, e: (0, 0))
    return pl.pallas_call(
        _experts_body,
        grid=(T // tm, E),
        in_specs=[
            row, row,
            pl.BlockSpec((tm, LANES), lambda i, e: (i, 0)),
            pl.BlockSpec((1, D, DE), lambda i, e: (e, 0, 0)),
            pl.BlockSpec((1, D, DE), lambda i, e: (e, 0, 0)),
            pl.BlockSpec((1, DE, D), lambda i, e: (e, 0, 0)),
            vec, vec,
        ],
        out_specs=[row, row],
        out_shape=[jax.ShapeDtypeStruct((T, D), F32), jax.ShapeDtypeStruct((T, D), BF16)],
        scratch_shapes=[pltpu.VMEM((tm, D), F32)],
        compiler_params=_cparams(("parallel", "arbitrary")),
    )(xb, x, gate, wg, wu, wd, g.reshape(1, D), b.reshape(1, D))


GID_LANE = N_EXPERTS
MOE_WINDOW = 1024
MOE_CHUNK = 128
MOE_VMEM_LIMIT = 60 * 1024 * 1024


def _router_sorted_body(x_ref, wh_ref, wl_ref, bias_ref, gate_ref, gidt_ref, cnt_ref):
    x_hi, x_lo = _split_bf16(x_ref[...])
    wh = wh_ref[...]
    logits = _dot(x_hi, wh) + _dot(x_lo, wh) + _dot(x_hi, wl_ref[...])
    scores = _sigmoid(logits)
    lane = lax.broadcasted_iota(jnp.int32, scores.shape, 1)
    lanef = lane.astype(F32)
    biased = jnp.where(lane < N_EXPERTS, scores + bias_ref[...], REMOVED)

    def top2(mask):
        v = jnp.where(mask, biased, REMOVED)
        m1 = jnp.max(v, axis=-1, keepdims=True)
        i1 = jnp.min(jnp.where(v == m1, lanef, float(LANES)), axis=-1, keepdims=True)
        v2 = jnp.where(lanef == i1, REMOVED, v)
        m2 = jnp.max(v2, axis=-1, keepdims=True)
        i2 = jnp.min(jnp.where(v2 == m2, lanef, float(LANES)), axis=-1, keepdims=True)
        return m1 + m2, jnp.where((lanef == i1) | (lanef == i2), 1.0, 0.0)

    best, best_sel = top2(lane // EXPERTS_PER_GROUP == 0)
    gid = jnp.zeros_like(best)
    for grp in range(1, N_GROUPS):
        score, sel = top2(lane // EXPERTS_PER_GROUP == grp)
        better = score > best
        best = jnp.where(better, score, best)
        best_sel = jnp.where(better, sel, best_sel)
        gid = jnp.where(better, float(grp), gid)
    w = best_sel * scores
    gate = w / jnp.sum(w, axis=-1, keepdims=True)
    gate_ref[...] = jnp.where(lane == GID_LANE, gid, gate)
    gid_b = jnp.broadcast_to(gid, scores.shape)
    gidt_ref[...] = gid_b.T[:8, :]
    cnt_ref[0] = jnp.broadcast_to(
        jnp.sum(jnp.where(lanef == gid_b, 1.0, 0.0), axis=0, keepdims=True), (8, LANES))


def _router_sorted(x, router_w, router_bias, tm):
    T, D = x.shape
    wpad = jnp.zeros((D, LANES), F32).at[:, :N_EXPERTS].set(router_w)
    wh, wl = _split_bf16(wpad)
    bpad = jnp.zeros((1, LANES), F32).at[0, :N_EXPERTS].set(router_bias)
    return pl.pallas_call(
        _router_sorted_body,
        grid=(T // tm,),
        in_specs=[
            pl.BlockSpec((tm, D), lambda i: (i, 0)),
            pl.BlockSpec((D, LANES), lambda i: (0, 0)),
            pl.BlockSpec((D, LANES), lambda i: (0, 0)),
            pl.BlockSpec((1, LANES), lambda i: (0, 0)),
        ],
        out_specs=[
            pl.BlockSpec((tm, LANES), lambda i: (i, 0)),
            pl.BlockSpec((8, tm), lambda i: (0, i)),
            pl.BlockSpec((1, 8, LANES), lambda i: (i, 0, 0)),
        ],
        out_shape=[
            jax.ShapeDtypeStruct((T, LANES), F32),
            jax.ShapeDtypeStruct((8, T), F32),
            jax.ShapeDtypeStruct((T // tm, 8, LANES), F32),
        ],
        compiler_params=_cparams(("parallel",)),
    )(x, wh, wl, bpad)


def _experts_sorted_body(cnt_ref, xb_ref, x_ref, gate_ref, gidt_ref, ltri_ref, utri_ref, wg_ref, wu_ref,
                         wd_ref, g_ref, b_ref, y_ref, yb_ref, xs_ref, gs_ref, acc_ref, rank_ref,
                         *, W, Wp, chunk):
    win = pl.program_id(0)
    e = pl.program_id(1)
    grp = e // EXPERTS_PER_GROUP
    padded = [((cnt_ref[win * N_GROUPS + g] + chunk - 1) // chunk) * chunk for g in range(N_GROUPS)]
    starts = [0]
    for g in range(N_GROUPS - 1):
        starts.append(starts[-1] + padded[g])
    start = starts[0]
    for g in range(1, N_GROUPS):
        start = jnp.where(grp == g, starts[g], start)
    n_chunks = (cnt_ref[win * N_GROUPS + grp] + chunk - 1) // chunk

    @pl.when(e == 0)
    def _():
        gate = gate_ref[...]
        lane = lax.broadcasted_iota(jnp.int32, gate.shape, 1)
        lanef = lane.astype(F32)
        gid = jnp.sum(jnp.where(lane == GID_LANE, gate, 0.0), axis=-1, keepdims=True)
        member = jnp.where((lanef == gid) & (lane < N_GROUPS), 1.0, 0.0)
        earlier = _dot(ltri_ref[...], member.astype(BF16))
        first = jnp.zeros(gate.shape, F32)
        for g in range(1, N_GROUPS):
            first = jnp.where(lane == g, starts[g].astype(F32), first)
        rank = jnp.sum(member * (first + earlier), axis=-1, keepdims=True)
        rank_ref[...] = jnp.broadcast_to(rank, gate.shape)

        gid_r = gidt_ref[...]
        sub = lax.broadcasted_iota(jnp.int32, gid_r.shape, 0)
        member_r = jnp.where(sub.astype(F32) == gid_r, 1.0, 0.0)
        earlier_r = _dot(member_r.astype(BF16), utri_ref[...])
        first_r = jnp.zeros(gid_r.shape, F32)
        for g in range(1, N_GROUPS):
            first_r = jnp.where(sub == g, starts[g].astype(F32), first_r)
        rank_r = jnp.sum(member_r * (first_r + earlier_r), axis=0, keepdims=True)
        rows = lax.broadcasted_iota(jnp.int32, (Wp, W), 0).astype(F32)
        perm = jnp.where(rows == rank_r, 1.0, 0.0).astype(BF16)
        xs_ref[...] = _dot(perm, xb_ref[...]).astype(BF16)
        g_hi, g_lo = _split_bf16(gate)
        gs_ref[...] = _dot(perm, g_hi) + _dot(perm, g_lo)
        acc_ref[...] = jnp.zeros(acc_ref.shape, F32)

    def expert_rows(r0, rows):
        r0 = pl.multiple_of(r0, chunk)
        xc = xs_ref[pl.ds(r0, rows), :]
        a = _dot(xc, wg_ref[0])
        u = _dot(xc, wu_ref[0])
        gs = gs_ref[pl.ds(r0, rows), :]
        lane = lax.broadcasted_iota(jnp.int32, gs.shape, 1)
        gcol = jnp.sum(jnp.where(lane == e, gs, 0.0), axis=-1, keepdims=True)
        h = a * _sigmoid(a) * u * gcol
        acc_ref[pl.ds(r0, rows), :] += _dot(h.astype(BF16), wd_ref[0])

    def two_chunks(i, carry):
        expert_rows(start + i * (2 * chunk), 2 * chunk)
        return carry

    lax.fori_loop(0, n_chunks // 2, two_chunks, 0)

    @pl.when(n_chunks % 2 == 1)
    def _():
        expert_rows(start + (n_chunks - 1) * chunk, chunk)

    @pl.when(e == N_EXPERTS - 1)
    def _():
        cols = lax.broadcasted_iota(jnp.int32, (W, Wp), 1).astype(F32)
        unperm = jnp.where(cols == rank_ref[...][:, :1], 1.0, 0.0).astype(BF16)
        ffn = _dot(unperm, acc_ref[...].astype(BF16))
        y = _layer_norm(DN_ALPHA * x_ref[...] + ffn, g_ref[...], b_ref[...])
        y_ref[...] = y
        yb_ref[...] = y.astype(BF16)


def _moe_ln(xb, x, router_w, router_bias, wg, wu, wd, g, b):
    T, D = x.shape
    W = min(MOE_WINDOW, T)
    chunk = MOE_CHUNK
    Wp = W + N_GROUPS * chunk
    E, _, DE = wg.shape
    assert T % W == 0 and W % chunk == 0
    gate, gidt, cnt = _router_sorted(x, router_w, router_bias, W)
    counts = cnt[:, 0, :N_GROUPS].astype(jnp.int32).reshape(-1)
    t = np.arange(W)
    ltri = jnp.asarray((t[None, :] < t[:, None]).astype(np.float32), BF16)
    once = pl.Buffered(1)
    row = lambda shape: pl.BlockSpec(shape, lambda i, e, c: (i, 0))
    vec = pl.BlockSpec((1, D), lambda i, e, c: (0, 0))
    tri = pl.BlockSpec((W, W), lambda i, e, c: (0, 0), pipeline_mode=once)
    grid_spec = pltpu.PrefetchScalarGridSpec(
        num_scalar_prefetch=1,
        grid=(T // W, E),
        in_specs=[
            row((W, D)),
            pl.BlockSpec((W, D), lambda i, e, c: (i, 0), pipeline_mode=once),
            row((W, LANES)),
            pl.BlockSpec((8, W), lambda i, e, c: (0, i)),
            tri, tri,
            pl.BlockSpec((1, D, DE), lambda i, e, c: (e, 0, 0)),
            pl.BlockSpec((1, D, DE), lambda i, e, c: (e, 0, 0)),
            pl.BlockSpec((1, DE, D), lambda i, e, c: (e, 0, 0)),
            vec, vec,
        ],
        out_specs=[row((W, D)), row((W, D))],
        scratch_shapes=[
            pltpu.VMEM((Wp, D), BF16),
            pltpu.VMEM((Wp, LANES), F32),
            pltpu.VMEM((Wp, D), F32),
            pltpu.VMEM((W, LANES), F32),
        ],
    )
    return pl.pallas_call(
        functools.partial(_experts_sorted_body, W=W, Wp=Wp, chunk=chunk),
        grid_spec=grid_spec,
        out_shape=[jax.ShapeDtypeStruct((T, D), F32), jax.ShapeDtypeStruct((T, D), BF16)],
        compiler_params=pltpu.CompilerParams(dimension_semantics=("parallel", "arbitrary"),
                                             vmem_limit_bytes=MOE_VMEM_LIMIT),
    )(counts, xb, x, gate, gidt, ltri, ltri.T, wg, wu, wd, g.reshape(1, D), b.reshape(1, D))


def _kt_tiles(t, B, S, n, tk):
    kt = t.reshape(B, S // tk, tk, n, HEAD_DIM)
    return kt.transpose(0, 3, 1, 4, 2)


def _block_onehots(S, block, tk):
    key = np.arange(S).reshape(S // tk, 1, tk)
    r = np.arange(LANES).reshape(1, LANES, 1)
    return jnp.asarray((key // block == r).astype(np.float32), BF16)


def _rope_tiled(S):
    cos, sin = _rope_tables(jnp.arange(S))
    reps = LANES // HALF
    return jnp.tile(cos, (1, reps)), jnp.tile(sin, (1, reps))


def _nsa_mixer(xb, B, S, w_in, cmp_k_w1, cmp_k_w2, cmp_v_w1, cmp_v_w2, cmp_k_pos, cmp_v_pos):
    G, HPG, KV = NSA_KV_GROUPS, NSA_HPG, NSA_KV_DIM
    L, STR, SB = NSA_CMP_LEN, NSA_CMP_STRIDE, NSA_SEL_BLOCK
    assert L == 2 * STR and S % SB == 0 and S // SB <= LANES
    T = B * S
    n_cmp = (S - L) // STR + 1
    NC = S // STR
    n_sel = S // SB
    top_n = min(NSA_SEL_TOPN, n_sel)

    cos2, sin2 = _rope_tiled(S)
    wb = w_in.astype(BF16)
    wcol = lambda i: wb[:, D_MODEL + i * KV: D_MODEL + (i + 1) * KV]
    tn = 2 * KV
    tk_s = min(512, S)
    tq_w = NSA_WINDOW // 2
    proj = _proj(xb, wb[:, :D_MODEL + 2 * KV], cos2, sin2, [2] * (D_MODEL // tn) + [0], S, tn=tn)
    kt_s = _proj_kt(xb, wcol(2), cos2, sin2, B, S, tk_s)
    kt_w = _proj_kt(xb, wcol(4), cos2, sin2, B, S, tq_w)
    vaug = _proj_vaug(xb, jnp.concatenate([wcol(3), wcol(5)], axis=1), tn=tn)
    wg = jnp.zeros((D_MODEL, LANES), BF16).at[:, :3 * N_HEADS].set(wb[:, D_MODEL + 6 * KV:])
    gates = _proj(xb, wg, cos2, sin2, [0], S, out_dtype=F32, tn=LANES).reshape(B, S, LANES)

    col = lambda i: proj[:, D_MODEL + i * KV: D_MODEL + (i + 1) * KV]
    proj3 = proj.reshape(B, S, proj.shape[1])
    vaug3 = vaug.reshape(B, S, vaug.shape[1])

    ccos, csin = _rope_tables(jnp.arange(NC) * STR + (L - 1))
    ccos = jnp.concatenate([ccos, ccos], axis=1)
    csin = jnp.concatenate([csin, csin], axis=1)
    to_rows = lambda t: t.reshape(B, S, G, HEAD_DIM).transpose(0, 2, 1, 3).reshape(B * G, NC, STR * HEAD_DIM)
    kc = _compress(to_rows(col(0)), cmp_k_w1, cmp_k_pos, cmp_k_w2, ccos, csin, True, n_cmp)
    vc = _compress(to_rows(col(1)), cmp_v_w1, cmp_v_pos, cmp_v_w2, ccos, csin, False, n_cmp)
    kct = kc.reshape(B, G, NC, HEAD_DIM).transpose(0, 1, 3, 2)

    ci = np.arange(NC)[:, None]
    sj = np.arange(LANES)[None, :]
    overlap = ((ci * STR < (sj + 1) * SB) & (ci * STR + L > sj * SB) & (ci < n_cmp) & (sj < n_sel))
    overlap = jnp.broadcast_to(jnp.asarray(overlap.astype(np.float32), BF16), (B, G, NC, LANES))
    vo = jnp.concatenate([vc.reshape(B, G, NC, HEAD_DIM), jnp.ones((B, G, NC, 1), BF16),
                          jnp.zeros((B, G, NC, LANES - HEAD_DIM - 1), BF16), overlap], axis=-1)

    o_cmp, selb = _nsa_cmp(proj3, kct, vo, gates, n_cmp, n_sel, top_n)
    o_slc = _nsa_slc(proj3, kt_s, vaug3, 0,
                     _block_onehots(S, SB, tk_s), selb, gates, tq=tk_s, tk=tk_s)
    o_win = _nsa_win(proj3, kt_w, vaug3, G, gates, tq=tq_w)
    return [o.reshape(T, D_MODEL) for o in (o_cmp, o_slc, o_win)]


def _moba_mixer(xb, B, S, w_in):
    H = N_HEADS
    nb = S // MOBA_BLOCK
    top_k = min(MOBA_TOPK, nb)
    cos2, sin2 = _rope_tiled(S)
    tn = 512
    n_t = D_MODEL // tn
    wb = w_in.astype(BF16)
    tk = min(2 * MOBA_BLOCK, S)
    qt = _proj_kt(xb, wb[:, :D_MODEL], cos2, sin2, B, S, tk, tn=tn, scale=Q_SCALE_LOG2)
    k3 = _proj(xb, wb[:, D_MODEL:2 * D_MODEL], cos2, sin2, [1] * n_t, S, tn=tn).reshape(B, S, D_MODEL)
    vt = _proj_kt(xb, wb[:, 2 * D_MODEL:], cos2, sin2, B, S, tk, tn=tn, rope=False)
    e = _block_onehots(S, MOBA_BLOCK, tk)
    kh, kl = _moba_kmean_pairs(k3, e, tk)
    o = _moba_attn_t(qt, k3, vt, e.transpose(0, 2, 1), kh, kl, nb, top_k, tq=tk, tk=tk)
    return [o.reshape(B * S, D_MODEL)]


def kernel(x, nsa_w_in, nsa_w_out, nsa_cmp_k_w1, nsa_cmp_k_w2, nsa_cmp_v_w1, nsa_cmp_v_w2, nsa_cmp_k_pos, nsa_cmp_v_pos, moba_w_in, moba_w_out, router_w, router_bias, moe_w_gate, moe_w_up, moe_w_down, ln_g, ln_b):
    B, S, D = x.shape
    xf = x.reshape(B * S, D)
    xb = xf.astype(BF16)
    for layer in range(DEPTH):
        j = layer // 2
        if layer % 2 == 0:
            os_ = _nsa_mixer(xb, B, S, nsa_w_in[j], nsa_cmp_k_w1[j], nsa_cmp_k_w2[j], nsa_cmp_v_w1[j],
                             nsa_cmp_v_w2[j], nsa_cmp_k_pos[j], nsa_cmp_v_pos[j])
            w_out = nsa_w_out[j]
        else:
            os_ = _moba_mixer(xb, B, S, moba_w_in[j])
            w_out = moba_w_out[j]
        xf, xb = _outproj_ln(os_, w_out.astype(BF16), xf, ln_g[layer, 0], ln_b[layer, 0])
        xf, xb = _moe_ln(xb, xf, router_w, router_bias, moe_w_gate[layer].astype(BF16),
                         moe_w_up[layer].astype(BF16), moe_w_down[layer].astype(BF16),
                         ln_g[layer, 1], ln_b[layer, 1])
    return xf.reshape(B, S, D)
```

```python
import functools

import jax
import jax.numpy as jnp
import numpy as np
from jax import lax
from jax.experimental import pallas as pl
from jax.experimental.pallas import tpu as pltpu

F32 = jnp.float32
BF16 = jnp.bfloat16

D_MODEL = 1024
N_HEADS = 16
HEAD_DIM = 64
HALF = HEAD_DIM // 2
ROPE_THETA = 10000.0
DEPTH = 2
DN_ALPHA = (2 * DEPTH) ** 0.25
LN_EPS = 1e-5
NEG = -1e30
FORCE = 1e9
MASK = -1e30
REMOVED = -3.0e38
LANES = 128
Q_SCALE_LOG2 = float(HEAD_DIM ** -0.5 * np.log2(np.e))
ROW_CHUNK = 128
PROJ_ROW_CHUNK = 256

NSA_KV_GROUPS = 4
NSA_HPG = N_HEADS // NSA_KV_GROUPS
NSA_KV_DIM = NSA_KV_GROUPS * HEAD_DIM
NSA_CMP_LEN = 32
NSA_CMP_STRIDE = 16
NSA_SEL_BLOCK = 64
NSA_SEL_TOPN = 16
NSA_WINDOW = 512

MOBA_BLOCK = 256
MOBA_TOPK = 3

N_EXPERTS = 16
N_GROUPS = 4
EXPERTS_PER_GROUP = N_EXPERTS // N_GROUPS
D_EXPERT = 512

VMEM_LIMIT = 48 * 1024 * 1024


def _cparams(sem):
    return pltpu.CompilerParams(dimension_semantics=sem, vmem_limit_bytes=VMEM_LIMIT)


def _dot(a, b):
    return jnp.dot(a, b, preferred_element_type=F32)


def _split_bf16(x):
    hi = x.astype(BF16)
    lo = (x - hi.astype(F32)).astype(BF16)
    return hi, lo


def _sigmoid(x):
    return 1.0 / (1.0 + jnp.exp(-x))


def _proj_body(mode_ref, x_ref, w_ref, cos_ref, sin_ref, o_ref, *, tn):
    mode = mode_ref[pl.program_id(0)]
    tm = x_ref.shape[0]
    rc = min(PROJ_ROW_CHUNK, tm)

    @pl.when(mode == 0)
    def _():
        for r in range(0, tm, rc):
            o_ref[r:r + rc, :] = _dot(x_ref[r:r + rc, :], w_ref[...]).astype(o_ref.dtype)

    @pl.when(mode != 0)
    def _():
        sc = jnp.where(mode == 2, Q_SCALE_LOG2, 1.0).astype(F32)
        for r in range(0, tm, rc):
            acc = _dot(x_ref[r:r + rc, :], w_ref[...])
            cos = cos_ref[r:r + rc, :] * sc
            sin = sin_ref[r:r + rc, :] * sc
            for c in range(tn // LANES):
                o_ref[r:r + rc, c * LANES:(c + 1) * LANES] = (
                    _rope_chunk(acc[:, c * LANES:(c + 1) * LANES], cos, sin).astype(o_ref.dtype))


def _proj(xb, w, cos2, sin2, modes, seq, out_dtype=BF16, tm=1024, tn=512):
    T, K = xb.shape
    N = w.shape[1]
    tm = min(tm, seq)
    assert T % tm == 0 and N % tn == 0 and seq % tm == 0 and len(modes) == N // tn
    n_pos = seq // tm
    grid_spec = pltpu.PrefetchScalarGridSpec(
        num_scalar_prefetch=1,
        grid=(N // tn, T // tm),
        in_specs=[
            pl.BlockSpec((tm, K), lambda j, i, m: (i, 0)),
            pl.BlockSpec((K, tn), lambda j, i, m: (0, j)),
            pl.BlockSpec((tm, LANES), lambda j, i, m: (i % n_pos, 0)),
            pl.BlockSpec((tm, LANES), lambda j, i, m: (i % n_pos, 0)),
        ],
        out_specs=pl.BlockSpec((tm, tn), lambda j, i, m: (i, j)),
    )
    return pl.pallas_call(
        functools.partial(_proj_body, tn=tn),
        grid_spec=grid_spec,
        out_shape=jax.ShapeDtypeStruct((T, N), out_dtype),
        compiler_params=_cparams(("parallel", "parallel")),
    )(jnp.asarray(modes, jnp.int32), xb, w, cos2, sin2)


def _rope_tables(pos):
    inv = 1.0 / (ROPE_THETA ** (jnp.arange(0, HEAD_DIM, 2, dtype=F32) / HEAD_DIM))
    ang = pos.astype(F32)[:, None] * inv[None, :]
    return jnp.cos(ang), jnp.sin(ang)


def _rope_chunk(a, cos, sin):
    lane = lax.broadcasted_iota(jnp.int32, a.shape, 1)
    up = pltpu.roll(a, LANES - HALF, 1)
    dn = pltpu.roll(a, HALF, 1)
    return a * cos + jnp.where((lane % HEAD_DIM) < HALF, -up, dn) * sin


def _proj_kt_body(x_ref, w_ref, cos_ref, sin_ref, o_ref, *, tn, tk, rope, scale):
    tm = x_ref.shape[0]
    rc = min(PROJ_ROW_CHUNK, tk)
    for r in range(0, tm, rc):
        acc = _dot(x_ref[r:r + rc, :], w_ref[...])
        cos = cos_ref[r:r + rc, :] * scale
        sin = sin_ref[r:r + rc, :] * scale
        for c in range(tn // LANES):
            a = acc[:, c * LANES:(c + 1) * LANES]
            kt = (_rope_chunk(a, cos, sin) if rope else a).T
            for hh in range(2):
                o_ref[0, 2 * c + hh, r // tk, :, r % tk:r % tk + rc] = (
                    kt[hh * HEAD_DIM:(hh + 1) * HEAD_DIM, :].astype(o_ref.dtype))


def _proj_kt(xb, w, cos2, sin2, B, S, tk, tm=1024, tn=256, rope=True, scale=1.0):
    T, K = xb.shape
    N = w.shape[1]
    tm = min(tm, S)
    tn = min(tn, N)
    assert S % tm == 0 and tm % tk == 0 and N % tn == 0
    n_pos = S // tm
    return pl.pallas_call(
        functools.partial(_proj_kt_body, tn=tn, tk=tk, rope=rope, scale=scale),
        grid=(N // tn, T // tm),
        in_specs=[
            pl.BlockSpec((tm, K), lambda j, i: (i, 0)),
            pl.BlockSpec((K, tn), lambda j, i: (0, j)),
            pl.BlockSpec((tm, LANES), lambda j, i: (i % n_pos, 0)),
            pl.BlockSpec((tm, LANES), lambda j, i: (i % n_pos, 0)),
        ],
        out_specs=pl.BlockSpec((1, tn // HEAD_DIM, tm // tk, HEAD_DIM, tk),
                               lambda j, i: (i // n_pos, j, i % n_pos, 0, 0)),
        out_shape=jax.ShapeDtypeStruct((B, N // HEAD_DIM, S // tk, HEAD_DIM, tk), BF16),
        compiler_params=_cparams(("parallel", "parallel")),
    )(xb, w, cos2, sin2)


def _proj_vaug_body(x_ref, w_ref, o_ref, *, tn):
    tm = x_ref.shape[0]
    rc = min(PROJ_ROW_CHUNK, tm)
    lane = lax.broadcasted_iota(jnp.int32, (rc, LANES), 1)
    tail = jnp.where(lane == HEAD_DIM, 1.0, 0.0)
    for r in range(0, tm, rc):
        acc = _dot(x_ref[r:r + rc, :], w_ref[...])
        for c in range(tn // LANES):
            a = acc[:, c * LANES:(c + 1) * LANES]
            for k, head in enumerate((a, pltpu.roll(a, HEAD_DIM, 1))):
                o_ref[r:r + rc, (2 * c + k) * LANES:(2 * c + k + 1) * LANES] = (
                    jnp.where(lane < HEAD_DIM, head, tail).astype(o_ref.dtype))


def _proj_vaug(xb, w, tm=1024, tn=512):
    T, K = xb.shape
    N = w.shape[1]
    tm = min(tm, T)
    assert T % tm == 0 and N % tn == 0
    return pl.pallas_call(
        functools.partial(_proj_vaug_body, tn=tn),
        grid=(N // tn, T // tm),
        in_specs=[pl.BlockSpec((tm, K), lambda j, i: (i, 0)), pl.BlockSpec((K, tn), lambda j, i: (0, j))],
        out_specs=pl.BlockSpec((tm, 2 * tn), lambda j, i: (i, j)),
        out_shape=jax.ShapeDtypeStruct((T, 2 * N), BF16),
        compiler_params=_cparams(("parallel", "parallel")),
    )(xb, w)


def _gelu_tanh(x):
    c = np.float32(np.sqrt(2.0 / np.pi))
    return 0.5 * x * (1.0 + jnp.tanh(c * (x + 0.044715 * (x * x * x))))


def _compress_body(r_ref, w1_ref, pos_ref, w2_ref, w2r_ref, cos_ref, sin_ref, o_ref, *, rope, n_cmp):
    r = r_ref[0]
    nc = r.shape[0]
    half = NSA_CMP_STRIDE * HEAD_DIM
    a = _dot(r, w1_ref[0])
    b = _dot(r, w1_ref[1])
    pos = pos_ref[...]
    pb = _dot(pos[:, :half], w1_ref[0]) + _dot(pos[:, half:], w1_ref[1])
    b_next = pltpu.roll(b, nc - 1, 0)
    h = _gelu_tanh(a + b_next + pb[0:1, :]).astype(BF16)
    o = _dot(h, w2_ref[...])
    if rope:
        o = o * cos_ref[...] + _dot(h, w2r_ref[...]) * sin_ref[...]
    row = lax.broadcasted_iota(jnp.int32, o.shape, 0)
    o_ref[0] = jnp.where(row < n_cmp, o, 0.0).astype(o_ref.dtype)


def _compress(r, w1, pos, w2, cos_c, sin_c, rope, n_cmp):
    BG, NC, K = r.shape
    hidden = w1.shape[1]
    w1s = w1.astype(BF16).reshape(2, K, hidden)
    pos8 = jnp.zeros((8, 2 * K), BF16).at[0].set(pos.reshape(-1).astype(BF16))
    w2r = jnp.concatenate([-w2[:, HALF:], w2[:, :HALF]], axis=1).astype(BF16)
    full = lambda shape: pl.BlockSpec(shape, lambda i: (0,) * len(shape))
    return pl.pallas_call(
        functools.partial(_compress_body, rope=rope, n_cmp=n_cmp),
        grid=(BG,),
        in_specs=[
            pl.BlockSpec((1, NC, K), lambda i: (i, 0, 0)),
            full((2, K, hidden)),
            full((8, 2 * K)),
            full((hidden, HEAD_DIM)),
            full((hidden, HEAD_DIM)),
            full((NC, HEAD_DIM)),
            full((NC, HEAD_DIM)),
        ],
        out_specs=pl.BlockSpec((1, NC, HEAD_DIM), lambda i: (i, 0, 0)),
        out_shape=jax.ShapeDtypeStruct((BG, NC, HEAD_DIM), BF16),
        compiler_params=_cparams(("parallel",)),
    )(r, w1s, pos8, w2.astype(BF16), w2r, cos_c, sin_c)


def _topk_mask_t(v, k):
    idx = lax.broadcasted_iota(jnp.int32, v.shape, 0).astype(F32)

    def step(_, cur):
        m = jnp.max(cur, axis=0, keepdims=True)
        first = jnp.min(jnp.where(cur == m, idx, float(LANES)), axis=0, keepdims=True)
        return jnp.where(idx == first, REMOVED, cur)

    return lax.fori_loop(0, k, step, v, unroll=True) != v


def _head_kt(t, parity):
    z = jnp.zeros_like(t)
    return jnp.concatenate([t, z] if parity == 0 else [z, t], axis=0)


def _pair_merge(even, odd):
    lane = lax.broadcasted_iota(jnp.int32, even.shape, 1)
    return jnp.where(lane < HEAD_DIM, even, pltpu.roll(odd, HEAD_DIM, 1))


def _gate_col(logits, branch, h):
    idx = branch * N_HEADS + pl.program_id(1) * NSA_HPG + h
    lane = lax.broadcasted_iota(jnp.int32, logits.shape, 1)
    return _sigmoid(jnp.sum(jnp.where(lane == idx, logits, 0.0), axis=-1, keepdims=True))


def _tile_iotas(rows, tk):
    return (lax.broadcasted_iota(jnp.int32, (rows, tk), 0),
            lax.broadcasted_iota(jnp.int32, (rows, tk), 1))


def _nsa_cmp_body(q_ref, kct_ref, vo_ref, g_ref, o_ref, selbt_ref, s_ref, m_ref, imp_ref,
                  *, tq, n_cmp, n_sel, top_n):
    s0 = pl.program_id(2) * tq
    q4 = q_ref[0]
    n_chunks = vo_ref.shape[2] // LANES
    rc = min(ROW_CHUNK, tq)

    def attend(nv):
        nc = nv * LANES
        kct = kct_ref[0, 0, :, :nc]
        vo = vo_ref[0, 0, :nc, :]
        tpos = s0 + lax.broadcasted_iota(jnp.int32, (tq, nc), 0)
        nidx = lax.broadcasted_iota(jnp.int32, (tq, nc), 1)
        cmask = (nidx * NSA_CMP_STRIDE + (NSA_CMP_LEN - 1) <= tpos) & (nidx < n_cmp)
        for h in range(NSA_HPG):
            pair = q4[:, (h // 2) * LANES:(h // 2 + 1) * LANES]
            s = jnp.where(cmask, _dot(pair, _head_kt(kct, h % 2)), NEG)
            s_ref[h * tq:(h + 1) * tq, :nc] = s
            cm = s[:, :LANES]
            for c in range(1, nv):
                cm = jnp.maximum(cm, s[:, c * LANES:(c + 1) * LANES])
            m_ref[h * tq:(h + 1) * tq, :] = jnp.broadcast_to(jnp.max(cm, axis=-1, keepdims=True), (tq, LANES))

        for r0 in range(0, tq, rc):
            imp = jnp.zeros((rc, LANES), F32)
            heads = []
            for h in range(NSA_HPG):
                r = h * tq + r0
                m = m_ref[r:r + rc, :]
                e = jnp.concatenate(
                    [jnp.exp2(s_ref[r:r + rc, c * LANES:(c + 1) * LANES] - m) for c in range(nv)],
                    axis=1).astype(BF16)
                res = _dot(e, vo)
                inv = jnp.where(m[:, :1] > 0.5 * NEG, 1.0 / res[:, HEAD_DIM:HEAD_DIM + 1], 0.0)
                gate = _gate_col(g_ref[0, r0:r0 + rc, :], 0, h)
                heads.append(res[:, :LANES] * (inv * gate))
                imp = imp + res[:, LANES:] * inv
            o_ref[0, r0:r0 + rc, :] = jnp.concatenate(
                [_pair_merge(heads[h], heads[h + 1]) for h in range(0, NSA_HPG, 2)], axis=1).astype(o_ref.dtype)
            imp_ref[r0:r0 + rc, :] = imp

    attend(n_chunks)
    imp = imp_ref[...]

    blk = lax.broadcasted_iota(jnp.int32, imp.shape, 1)
    jq = (s0 + lax.broadcasted_iota(jnp.int32, imp.shape, 0)) // NSA_SEL_BLOCK
    forced = (blk == 0) | (blk == jq) | (blk == jq - 1)
    imp = jnp.where(blk > jq, NEG, jnp.where(forced, FORCE, imp))
    imp = jnp.where(blk < n_sel, imp, REMOVED)
    sel_t = _topk_mask_t(imp.T, top_n)
    selbt_ref[0, 0] = jnp.where(sel_t, 0.0, MASK).astype(selbt_ref.dtype)


def _nsa_cmp(proj3, kct, vo, g_cmp, n_cmp, n_sel, top_n, tq=256):
    B, S, _ = proj3.shape
    G, HPG = NSA_KV_GROUPS, NSA_HPG
    NC = vo.shape[2]
    tq = min(tq, S)
    assert NC % LANES == 0
    return pl.pallas_call(
        functools.partial(_nsa_cmp_body, tq=tq, n_cmp=n_cmp, n_sel=n_sel, top_n=top_n),
        grid=(B, G, S // tq),
        in_specs=[
            pl.BlockSpec((1, tq, HPG * HEAD_DIM), lambda b, g, i: (b, i, g)),
            pl.BlockSpec((1, 1, HEAD_DIM, NC), lambda b, g, i: (b, g, 0, 0)),
            pl.BlockSpec((1, 1, NC, 2 * LANES), lambda b, g, i: (b, g, 0, 0)),
            pl.BlockSpec((1, tq, LANES), lambda b, g, i: (b, i, 0)),
        ],
        out_specs=[
            pl.BlockSpec((1, tq, HPG * HEAD_DIM), lambda b, g, i: (b, i, g)),
            pl.BlockSpec((1, 1, LANES, tq), lambda b, g, i: (b, g, 0, i)),
        ],
        out_shape=[
            jax.ShapeDtypeStruct((B, S, D_MODEL), BF16),
            jax.ShapeDtypeStruct((B, G, LANES, S), BF16),
        ],
        scratch_shapes=[pltpu.VMEM((HPG * tq, NC), F32), pltpu.VMEM((HPG * tq, LANES), F32),
                        pltpu.VMEM((tq, LANES), F32)],
        compiler_params=_cparams(("parallel", "parallel", "parallel")),
    )(proj3, kct, vo, g_cmp)


def _nsa_win_body(q_ref, kt_ref, v_ref, g_ref, o_ref, s_ref, m_ref, *, tq, n_tiles):
    s0 = pl.program_id(2) * tq
    tk = n_tiles * tq
    k_first = jnp.maximum(pl.program_id(2) - (n_tiles - 1), 0)
    k0 = pl.multiple_of(k_first * tq, tq)
    kt = jnp.concatenate([kt_ref[0, 0, k_first + j] for j in range(n_tiles)], axis=1)
    v = v_ref[0, pl.ds(k0, tk), :]
    row, col = _tile_iotas(tq, tk)
    dist = (s0 - k0) + row - col
    valid = (dist >= 0) & (dist < NSA_WINDOW)
    q4 = q_ref[0]
    for h in range(NSA_HPG):
        pair = q4[:, (h // 2) * LANES:(h // 2 + 1) * LANES]
        s = jnp.where(valid, _dot(pair, _head_kt(kt, h % 2)), MASK)
        s_ref[h * tq:(h + 1) * tq, :] = s
        cm = s[:, :LANES]
        for c in range(1, tk // LANES):
            cm = jnp.maximum(cm, s[:, c * LANES:(c + 1) * LANES])
        m_ref[h * tq:(h + 1) * tq, :] = jnp.broadcast_to(jnp.max(cm, axis=-1, keepdims=True), (tq, LANES))

    rc = min(ROW_CHUNK, tq)
    for r0 in range(0, tq, rc):
        heads = []
        for h in range(NSA_HPG):
            r = h * tq + r0
            m = m_ref[r:r + rc, :]
            p = jnp.concatenate(
                [jnp.exp2(s_ref[r:r + rc, c * LANES:(c + 1) * LANES] - m) for c in range(tk // LANES)],
                axis=1).astype(BF16)
            res = _dot(p, v)
            heads.append(res * (_gate_col(g_ref[0, r0:r0 + rc, :], 2, h) / res[:, HEAD_DIM:HEAD_DIM + 1]))
        o_ref[0, r0:r0 + rc, :] = jnp.concatenate(
            [_pair_merge(heads[h], heads[h + 1]) for h in range(0, NSA_HPG, 2)], axis=1).astype(o_ref.dtype)


def _nsa_win(proj3, kt, vaug3, v_block, g, tq):
    B, S, _ = proj3.shape
    G, HPG = NSA_KV_GROUPS, NSA_HPG
    nk = S // tq
    n_tiles = NSA_WINDOW // tq + 1
    assert NSA_WINDOW % tq == 0 and nk >= n_tiles and kt.shape == (B, G, nk, HEAD_DIM, tq)
    once = pl.Buffered(1)
    return pl.pallas_call(
        functools.partial(_nsa_win_body, tq=tq, n_tiles=n_tiles),
        grid=(B, G, S // tq),
        in_specs=[
            pl.BlockSpec((1, tq, HPG * HEAD_DIM), lambda b, g_, i: (b, i, g_)),
            pl.BlockSpec((1, 1, nk, HEAD_DIM, tq), lambda b, g_, i: (b, g_, 0, 0, 0), pipeline_mode=once),
            pl.BlockSpec((1, S, LANES), lambda b, g_, i: (b, 0, v_block + g_), pipeline_mode=once),
            pl.BlockSpec((1, tq, LANES), lambda b, g_, i: (b, i, 0)),
        ],
        out_specs=pl.BlockSpec((1, tq, HPG * HEAD_DIM), lambda b, g_, i: (b, i, g_)),
        out_shape=jax.ShapeDtypeStruct((B, S, D_MODEL), BF16),
        scratch_shapes=[pltpu.VMEM((HPG * tq, n_tiles * tq), F32), pltpu.VMEM((HPG * tq, LANES), F32)],
        compiler_params=_cparams(("parallel", "parallel", "parallel")),
    )(proj3, kt, vaug3, g)


VALUE_SUB = 2 * LANES


def _flash_t_scratch(n_heads, tq, tk):
    return [pltpu.VMEM((n_heads * 2 * LANES, tq), BF16),
            pltpu.VMEM((n_heads * tk, tq), F32),
            pltpu.VMEM((n_heads * 8, tq), F32),
            pltpu.VMEM((n_heads * 8, tq), F32),
            pltpu.VMEM((n_heads * LANES, tq), F32)]


def _flash_t_init(m_ref, acc_ref):
    m_ref[...] = jnp.full(m_ref.shape, -jnp.inf, F32)
    acc_ref[...] = jnp.zeros(acc_ref.shape, F32)


def _flash_t_tile(n_heads, keys_of, vals_of, mask, qa_ref, s_ref, m_ref, alpha_ref, acc_ref):
    aug = 2 * LANES
    tk, tq = s_ref.shape[0] // n_heads, s_ref.shape[1]
    for h in range(n_heads):
        st = _dot(keys_of(h), qa_ref[h * aug:(h + 1) * aug, :])
        if mask is not None:
            st = jnp.where(mask, st, MASK)
        s_ref[h * tk:(h + 1) * tk, :] = st
        m_prev = m_ref[h * 8:(h + 1) * 8, :]
        m_new = jnp.maximum(m_prev, jnp.max(st, axis=0, keepdims=True))
        alpha_ref[h * 8:(h + 1) * 8, :] = jnp.exp2(m_prev - m_new)
        m_ref[h * 8:(h + 1) * 8, :] = m_new
    for h in range(n_heads):
        vals = vals_of(h)
        for q0 in range(0, tq, VALUE_SUB):
            m = m_ref[h * 8:h * 8 + 1, q0:q0 + VALUE_SUB]
            part = jnp.zeros((LANES, VALUE_SUB), F32)
            for c0 in range(0, tk, VALUE_SUB):
                pt = jnp.exp2(s_ref[h * tk + c0:h * tk + c0 + VALUE_SUB, q0:q0 + VALUE_SUB] - m).astype(BF16)
                part = part + _dot(vals[:, c0:c0 + VALUE_SUB], pt)
            acc_ref[h * LANES:(h + 1) * LANES, q0:q0 + VALUE_SUB] = (
                alpha_ref[h * 8:h * 8 + 1, q0:q0 + VALUE_SUB]
                * acc_ref[h * LANES:(h + 1) * LANES, q0:q0 + VALUE_SUB] + part)


def _flash_t_loop(tile, s0, tq, tk):
    last = (s0 + tq - 1) // tk

    def full_tile(ki, carry):
        tile(ki, None)
        return carry

    lax.fori_loop(0, last, full_tile, 0)
    krow, qcol = _tile_iotas(tk, tq)
    tile(last, last * tk + krow <= s0 + qcol)


def _flash_t_out(acc_ref, n_heads, gates):
    def head(h):
        acc = acc_ref[h * LANES:(h + 1) * LANES, :]
        return (acc / acc[HEAD_DIM:HEAD_DIM + 1, :]).T * gates[h]
    return jnp.concatenate([_pair_merge(head(h), head(h + 1)) for h in range(0, n_heads, 2)], axis=1)


def _ones_row_block(tk):
    return jnp.where(lax.broadcasted_iota(jnp.int32, (HEAD_DIM, tk), 0) == 0, 1.0, 0.0).astype(BF16)


def _nsa_slc_t_body(q_ref, k_ref, vt_ref, et_ref, selbt_ref, g_ref, o_ref, qa_ref, s_ref, m_ref, alpha_ref,
                    acc_ref, *, tq, tk):
    s0 = pl.program_id(2) * tq
    _flash_t_init(m_ref, acc_ref)
    odd_group = (pl.program_id(1) % 2) == 1
    selbt = selbt_ref[0, 0]
    q4 = q_ref[0]
    zeros = jnp.zeros((HEAD_DIM, tq), BF16)
    for pair in range(NSA_HPG // 2):
        pair_t = q4[:, pair * LANES:(pair + 1) * LANES].astype(F32).T
        for hp in range(2):
            qt = pair_t[hp * HEAD_DIM:(hp + 1) * HEAD_DIM, :].astype(BF16)
            slot = jnp.where(odd_group, jnp.concatenate([zeros, qt], axis=0),
                             jnp.concatenate([qt, zeros], axis=0))
            h = 2 * pair + hp
            qa_ref[h * 2 * LANES:(h + 1) * 2 * LANES, :] = jnp.concatenate([selbt, slot], axis=0)
    ones_rows = _ones_row_block(tk)

    def tile(ki, mask):
        k0 = pl.multiple_of(ki * tk, tk)
        keys = jnp.concatenate([et_ref[ki], k_ref[0, pl.ds(k0, tk), :]], axis=1)
        vals = jnp.concatenate([vt_ref[0, 0, ki], ones_rows], axis=0)
        _flash_t_tile(NSA_HPG, lambda h: keys, lambda h: vals, mask, qa_ref, s_ref, m_ref, alpha_ref, acc_ref)

    _flash_t_loop(tile, s0, tq, tk)
    gates = [_gate_col(g_ref[0], 1, h) for h in range(NSA_HPG)]
    o_ref[0] = _flash_t_out(acc_ref, NSA_HPG, gates).astype(o_ref.dtype)


def _nsa_slc_t(proj3, k3, vt, et, selbt, g, tq, tk):
    B, S, _ = proj3.shape
    G, HPG = NSA_KV_GROUPS, NSA_HPG
    nk = S // tk
    assert tk == tq and G % 2 == 0 and vt.shape == (B, G, nk, HEAD_DIM, tk) and et.shape == (nk, tk, LANES)
    once = pl.Buffered(1)
    return pl.pallas_call(
        functools.partial(_nsa_slc_t_body, tq=tq, tk=tk),
        grid=(B, G, S // tq),
        in_specs=[
            pl.BlockSpec((1, tq, HPG * HEAD_DIM), lambda b, g_, i: (b, i, g_)),
            pl.BlockSpec((1, S, LANES), lambda b, g_, i: (b, 0, g_ // 2), pipeline_mode=once),
            pl.BlockSpec((1, 1, nk, HEAD_DIM, tk), lambda b, g_, i: (b, g_, 0, 0, 0), pipeline_mode=once),
            pl.BlockSpec((nk, tk, LANES), lambda b, g_, i: (0, 0, 0), pipeline_mode=once),
            pl.BlockSpec((1, 1, LANES, tq), lambda b, g_, i: (b, g_, 0, i)),
            pl.BlockSpec((1, tq, LANES), lambda b, g_, i: (b, i, 0)),
        ],
        out_specs=pl.BlockSpec((1, tq, HPG * HEAD_DIM), lambda b, g_, i: (b, i, g_)),
        out_shape=jax.ShapeDtypeStruct((B, S, D_MODEL), BF16),
        scratch_shapes=_flash_t_scratch(HPG, tq, tk),
        compiler_params=_cparams(("parallel", "parallel", "parallel")),
    )(proj3, k3, vt, et, selbt, g)


def _moba_kmean_pairs_body(k_ref, e_ref, hi_ref, lo_ref, *, tk):
    nk = e_ref.shape[0]
    km = jnp.zeros((LANES, LANES), F32)
    for ki in range(nk):
        km = km + _dot(e_ref[ki], k_ref[0, ki * tk:(ki + 1) * tk, :])
    hi, lo = _split_bf16(km * (1.0 / MOBA_BLOCK))
    hi_ref[0, 0] = hi
    lo_ref[0, 0] = lo


def _moba_kmean_pairs(k3, e, tk):
    B, S, N = k3.shape
    n_pairs = N // LANES
    nk = S // tk
    spec = pl.BlockSpec((1, 1, LANES, LANES), lambda b, p: (b, p, 0, 0))
    return pl.pallas_call(
        functools.partial(_moba_kmean_pairs_body, tk=tk),
        grid=(B, n_pairs),
        in_specs=[
            pl.BlockSpec((1, S, LANES), lambda b, p: (b, 0, p)),
            pl.BlockSpec((nk, LANES, tk), lambda b, p: (0, 0, 0)),
        ],
        out_specs=[spec, spec],
        out_shape=[jax.ShapeDtypeStruct((B, n_pairs, LANES, LANES), BF16)] * 2,
        compiler_params=_cparams(("parallel", "parallel")),
    )(k3, e)


def _moba_t_body(qt_ref, k_ref, vt_ref, et_ref, kh_ref, kl_ref, o_ref, qa_ref, s_ref, m_ref, alpha_ref,
                 acc_ref, *, tq, tk, hb, nb, top_k):
    s0 = pl.program_id(2) * tq
    _flash_t_init(m_ref, acc_ref)
    aug = 2 * LANES

    blk = lax.broadcasted_iota(jnp.int32, (LANES, tq), 0)
    cb = (s0 + lax.broadcasted_iota(jnp.int32, (LANES, tq), 1)) // MOBA_BLOCK
    for h in range(hb):
        qpad = _head_kt(qt_ref[0, h, 0], h % 2)
        gsc = _dot(kh_ref[0, h // 2], qpad) + _dot(kl_ref[0, h // 2], qpad)
        gsc = jnp.where(blk < cb, gsc, NEG)
        gsc = jnp.where(blk < nb, gsc, REMOVED)
        sel = (_topk_mask_t(gsc, top_k) & (blk < cb)) | (blk == cb)
        qa_ref[h * aug:(h + 1) * aug, :] = jnp.concatenate(
            [jnp.where(sel, 0.0, MASK).astype(BF16), qpad], axis=0)

    ones_rows = _ones_row_block(tk)

    def tile(ki, mask):
        k0 = pl.multiple_of(ki * tk, tk)
        et = et_ref[ki]
        keys_of = lambda h: jnp.concatenate(
            [et, k_ref[0, pl.ds(k0, tk), (h // 2) * LANES:(h // 2 + 1) * LANES]], axis=1)
        vals_of = lambda h: jnp.concatenate([vt_ref[0, h, ki], ones_rows], axis=0)
        _flash_t_tile(hb, keys_of, vals_of, mask, qa_ref, s_ref, m_ref, alpha_ref, acc_ref)

    _flash_t_loop(tile, s0, tq, tk)
    o_ref[0] = _flash_t_out(acc_ref, hb, [1.0] * hb).astype(o_ref.dtype)


def _moba_attn_t(qt, k3, vt, et, kh, kl, nb, top_k, tq, tk, hb=4):
    B, H, nq, _, _ = qt.shape
    S = k3.shape[1]
    nk = S // tk
    assert tk == tq and tk % MOBA_BLOCK == 0 and H % hb == 0 and hb % 2 == 0 and nb <= LANES
    once = pl.Buffered(1)
    return pl.pallas_call(
        functools.partial(_moba_t_body, tq=tq, tk=tk, hb=hb, nb=nb, top_k=top_k),
        grid=(B, H // hb, nq),
        in_specs=[
            pl.BlockSpec((1, hb, 1, HEAD_DIM, tq), lambda b, h, i: (b, h, i, 0, 0)),
            pl.BlockSpec((1, S, hb * HEAD_DIM), lambda b, h, i: (b, 0, h), pipeline_mode=once),
            pl.BlockSpec((1, hb, nk, HEAD_DIM, tk), lambda b, h, i: (b, h, 0, 0, 0), pipeline_mode=once),
            pl.BlockSpec((nk, tk, LANES), lambda b, h, i: (0, 0, 0), pipeline_mode=once),
            pl.BlockSpec((1, hb // 2, LANES, LANES), lambda b, h, i: (b, h, 0, 0), pipeline_mode=once),
            pl.BlockSpec((1, hb // 2, LANES, LANES), lambda b, h, i: (b, h, 0, 0), pipeline_mode=once),
        ],
        out_specs=pl.BlockSpec((1, tq, hb * HEAD_DIM), lambda b, h, i: (b, i, h)),
        out_shape=jax.ShapeDtypeStruct((B, S, D_MODEL), BF16),
        scratch_shapes=_flash_t_scratch(hb, tq, tk),
        compiler_params=_cparams(("parallel", "parallel", "parallel")),
    )(qt, k3, vt, et, kh, kl)


def _layer_norm(r, g, b):
    mu = jnp.mean(r, axis=-1, keepdims=True)
    c = r - mu
    var = jnp.mean(c * c, axis=-1, keepdims=True)
    return c * lax.rsqrt(var + LN_EPS) * g + b


def _outproj_body(*refs, n_o):
    o_refs = refs[:n_o]
    w_ref, x_ref, g_ref, b_ref, y_ref, yb_ref = refs[n_o:]
    tm = x_ref.shape[0]
    rc = min(PROJ_ROW_CHUNK, tm)
    for r0 in range(0, tm, rc):
        rows = slice(r0, r0 + rc)
        o = o_refs[0][rows, :].astype(F32)
        for ref in o_refs[1:]:
            o = o + ref[rows, :].astype(F32)
        mix = _dot(o.astype(BF16), w_ref[...])
        y = _layer_norm(DN_ALPHA * x_ref[rows, :] + mix, g_ref[...], b_ref[...])
        y_ref[rows, :] = y
        yb_ref[rows, :] = y.astype(BF16)


def _outproj_ln(os_, w, x, g, b, tm=512):
    T, D = x.shape
    tm = min(tm, T)
    n_o = len(os_)
    row = pl.BlockSpec((tm, D), lambda i: (i, 0))
    vec = pl.BlockSpec((1, D), lambda i: (0, 0))
    return pl.pallas_call(
        functools.partial(_outproj_body, n_o=n_o),
        grid=(T // tm,),
        in_specs=[row] * n_o + [pl.BlockSpec((D, D), lambda i: (0, 0)), row, vec, vec],
        out_specs=[row, row],
        out_shape=[jax.ShapeDtypeStruct((T, D), F32), jax.ShapeDtypeStruct((T, D), BF16)],
        compiler_params=_cparams(("parallel",)),
    )(*os_, w, x, g.reshape(1, D), b.reshape(1, D))


GID_LANE = N_EXPERTS
MOE_WINDOW = 1024
MOE_CHUNK = 128
MOE_VMEM_LIMIT = 60 * 1024 * 1024


def _router_sorted_body(x_ref, wh_ref, wl_ref, bias_ref, gate_ref, gidt_ref, cnt_ref):
    x_hi, x_lo = _split_bf16(x_ref[...])
    wh = wh_ref[...]
    logits = _dot(x_hi, wh) + _dot(x_lo, wh) + _dot(x_hi, wl_ref[...])
    scores = _sigmoid(logits)
    lane = lax.broadcasted_iota(jnp.int32, scores.shape, 1)
    lanef = lane.astype(F32)
    biased = jnp.where(lane < N_EXPERTS, scores + bias_ref[...], REMOVED)

    def top2(mask):
        v = jnp.where(mask, biased, REMOVED)
        m1 = jnp.max(v, axis=-1, keepdims=True)
        i1 = jnp.min(jnp.where(v == m1, lanef, float(LANES)), axis=-1, keepdims=True)
        v2 = jnp.where(lanef == i1, REMOVED, v)
        m2 = jnp.max(v2, axis=-1, keepdims=True)
        i2 = jnp.min(jnp.where(v2 == m2, lanef, float(LANES)), axis=-1, keepdims=True)
        return m1 + m2, jnp.where((lanef == i1) | (lanef == i2), 1.0, 0.0)

    best, best_sel = top2(lane // EXPERTS_PER_GROUP == 0)
    gid = jnp.zeros_like(best)
    for grp in range(1, N_GROUPS):
        score, sel = top2(lane // EXPERTS_PER_GROUP == grp)
        better = score > best
        best = jnp.where(better, score, best)
        best_sel = jnp.where(better, sel, best_sel)
        gid = jnp.where(better, float(grp), gid)
    w = best_sel * scores
    gate = w / jnp.sum(w, axis=-1, keepdims=True)
    gate_ref[...] = jnp.where(lane == GID_LANE, gid, gate)
    gid_b = jnp.broadcast_to(gid, scores.shape)
    gidt_ref[...] = gid_b.T[:8, :]
    cnt_ref[0] = jnp.broadcast_to(
        jnp.sum(jnp.where(lanef == gid_b, 1.0, 0.0), axis=0, keepdims=True), (8, LANES))


def _router_sorted(x, router_w, router_bias, tm):
    T, D = x.shape
    wpad = jnp.zeros((D, LANES), F32).at[:, :N_EXPERTS].set(router_w)
    wh, wl = _split_bf16(wpad)
    bpad = jnp.zeros((1, LANES), F32).at[0, :N_EXPERTS].set(router_bias)
    return pl.pallas_call(
        _router_sorted_body,
        grid=(T // tm,),
        in_specs=[
            pl.BlockSpec((tm, D), lambda i: (i, 0)),
            pl.BlockSpec((D, LANES), lambda i: (0, 0)),
            pl.BlockSpec((D, LANES), lambda i: (0, 0)),
            pl.BlockSpec((1, LANES), lambda i: (0, 0)),
        ],
        out_specs=[
            pl.BlockSpec((tm, LANES), lambda i: (i, 0)),
            pl.BlockSpec((8, tm), lambda i: (0, i)),
            pl.BlockSpec((1, 8, LANES), lambda i: (i, 0, 0)),
        ],
        out_shape=[
            jax.ShapeDtypeStruct((T, LANES), F32),
            jax.ShapeDtypeStruct((8, T), F32),
            jax.ShapeDtypeStruct((T // tm, 8, LANES), F32),
        ],
        compiler_params=_cparams(("parallel",)),
    )(x, wh, wl, bpad)


def _experts_sorted_body(cnt_ref, xb_ref, x_ref, gate_ref, gidt_ref, ltri_ref, utri_ref, wg_ref, wu_ref,
                         wd_ref, g_ref, b_ref, y_ref, yb_ref, xs_ref, gs_ref, acc_ref, rank_ref,
                         *, W, Wp, chunk):
    win = pl.program_id(0)
    e = pl.program_id(1)
    grp = e // EXPERTS_PER_GROUP
    padded = [((cnt_ref[win * N_GROUPS + g] + chunk - 1) // chunk) * chunk for g in range(N_GROUPS)]
    starts = [0]
    for g in range(N_GROUPS - 1):
        starts.append(starts[-1] + padded[g])
    start = starts[0]
    for g in range(1, N_GROUPS):
        start = jnp.where(grp == g, starts[g], start)
    n_chunks = (cnt_ref[win * N_GROUPS + grp] + chunk - 1) // chunk

    @pl.when(e == 0)
    def _():
        gate = gate_ref[...]
        lane = lax.broadcasted_iota(jnp.int32, gate.shape, 1)
        lanef = lane.astype(F32)
        gid = jnp.sum(jnp.where(lane == GID_LANE, gate, 0.0), axis=-1, keepdims=True)
        member = jnp.where((lanef == gid) & (lane < N_GROUPS), 1.0, 0.0)
        earlier = _dot(ltri_ref[...], member.astype(BF16))
        first = jnp.zeros(gate.shape, F32)
        for g in range(1, N_GROUPS):
            first = jnp.where(lane == g, starts[g].astype(F32), first)
        rank = jnp.sum(member * (first + earlier), axis=-1, keepdims=True)
        rank_ref[...] = jnp.broadcast_to(rank, gate.shape)

        gid_r = gidt_ref[...]
        sub = lax.broadcasted_iota(jnp.int32, gid_r.shape, 0)
        member_r = jnp.where(sub.astype(F32) == gid_r, 1.0, 0.0)
        earlier_r = _dot(member_r.astype(BF16), utri_ref[...])
        first_r = jnp.zeros(gid_r.shape, F32)
        for g in range(1, N_GROUPS):
            first_r = jnp.where(sub == g, starts[g].astype(F32), first_r)
        rank_r = jnp.sum(member_r * (first_r + earlier_r), axis=0, keepdims=True)
        rows = lax.broadcasted_iota(jnp.int32, (Wp, W), 0).astype(F32)
        perm = jnp.where(rows == rank_r, 1.0, 0.0).astype(BF16)
        xs_ref[...] = _dot(perm, xb_ref[...]).astype(BF16)
        g_hi, g_lo = _split_bf16(gate)
        gs_ref[...] = _dot(perm, g_hi) + _dot(perm, g_lo)
        acc_ref[...] = jnp.zeros(acc_ref.shape, F32)

    def expert_rows(r0, rows):
        r0 = pl.multiple_of(r0, chunk)
        xc = xs_ref[pl.ds(r0, rows), :]
        a = _dot(xc, wg_ref[0])
        u = _dot(xc, wu_ref[0])
        gs = gs_ref[pl.ds(r0, rows), :]
        lane = lax.broadcasted_iota(jnp.int32, gs.shape, 1)
        gcol = jnp.sum(jnp.where(lane == e, gs, 0.0), axis=-1, keepdims=True)
        h = a * _sigmoid(a) * u * gcol
        acc_ref[pl.ds(r0, rows), :] += _dot(h.astype(BF16), wd_ref[0])

    def two_chunks(i, carry):
        expert_rows(start + i * (2 * chunk), 2 * chunk)
        return carry

    lax.fori_loop(0, n_chunks // 2, two_chunks, 0)

    @pl.when(n_chunks % 2 == 1)
    def _():
        expert_rows(start + (n_chunks - 1) * chunk, chunk)

    @pl.when(e == N_EXPERTS - 1)
    def _():
        cols = lax.broadcasted_iota(jnp.int32, (W, Wp), 1).astype(F32)
        unperm = jnp.where(cols == rank_ref[...][:, :1], 1.0, 0.0).astype(BF16)
        ffn = _dot(unperm, acc_ref[...].astype(BF16))
        y = _layer_norm(DN_ALPHA * x_ref[...] + ffn, g_ref[...], b_ref[...])
        y_ref[...] = y
        yb_ref[...] = y.astype(BF16)


def _moe_ln(xb, x, router_w, router_bias, wg, wu, wd, g, b):
    T, D = x.shape
    W = min(MOE_WINDOW, T)
    chunk = MOE_CHUNK
    Wp = W + N_GROUPS * chunk
    E, _, DE = wg.shape
    assert T % W == 0 and W % chunk == 0
    gate, gidt, cnt = _router_sorted(x, router_w, router_bias, W)
    counts = cnt[:, 0, :N_GROUPS].astype(jnp.int32).reshape(-1)
    t = np.arange(W)
    ltri = jnp.asarray((t[None, :] < t[:, None]).astype(np.float32), BF16)
    once = pl.Buffered(1)
    row = lambda shape: pl.BlockSpec(shape, lambda i, e, c: (i, 0))
    vec = pl.BlockSpec((1, D), lambda i, e, c: (0, 0))
    tri = pl.BlockSpec((W, W), lambda i, e, c: (0, 0), pipeline_mode=once)
    grid_spec = pltpu.PrefetchScalarGridSpec(
        num_scalar_prefetch=1,
        grid=(T // W, E),
        in_specs=[
            row((W, D)),
            pl.BlockSpec((W, D), lambda i, e, c: (i, 0), pipeline_mode=once),
            row((W, LANES)),
            pl.BlockSpec((8, W), lambda i, e, c: (0, i)),
            tri, tri,
            pl.BlockSpec((1, D, DE), lambda i, e, c: (e, 0, 0)),
            pl.BlockSpec((1, D, DE), lambda i, e, c: (e, 0, 0)),
            pl.BlockSpec((1, DE, D), lambda i, e, c: (e, 0, 0)),
            vec, vec,
        ],
        out_specs=[row((W, D)), row((W, D))],
        scratch_shapes=[
            pltpu.VMEM((Wp, D), BF16),
            pltpu.VMEM((Wp, LANES), F32),
            pltpu.VMEM((Wp, D), F32),
            pltpu.VMEM((W, LANES), F32),
        ],
    )
    return pl.pallas_call(
        functools.partial(_experts_sorted_body, W=W, Wp=Wp, chunk=chunk),
        grid_spec=grid_spec,
        out_shape=[jax.ShapeDtypeStruct((T, D), F32), jax.ShapeDtypeStruct((T, D), BF16)],
        compiler_params=pltpu.CompilerParams(dimension_semantics=("parallel", "arbitrary"),
                                             vmem_limit_bytes=MOE_VMEM_LIMIT),
    )(counts, xb, x, gate, gidt, ltri, ltri.T, wg, wu, wd, g.reshape(1, D), b.reshape(1, D))


def _block_onehots(S, block, tk):
    key = np.arange(S).reshape(S // tk, 1, tk)
    r = np.arange(LANES).reshape(1, LANES, 1)
    return jnp.asarray((key // block == r).astype(np.float32), BF16)


def _rope_tiled(S):
    cos, sin = _rope_tables(jnp.arange(S))
    reps = LANES // HALF
    return jnp.tile(cos, (1, reps)), jnp.tile(sin, (1, reps))


def _nsa_mixer(xb, B, S, w_in, cmp_k_w1, cmp_k_w2, cmp_v_w1, cmp_v_w2, cmp_k_pos, cmp_v_pos):
    G, HPG, KV = NSA_KV_GROUPS, NSA_HPG, NSA_KV_DIM
    L, STR, SB = NSA_CMP_LEN, NSA_CMP_STRIDE, NSA_SEL_BLOCK
    assert L == 2 * STR and S % SB == 0 and S // SB <= LANES
    T = B * S
    n_cmp = (S - L) // STR + 1
    NC = S // STR
    n_sel = S // SB
    top_n = min(NSA_SEL_TOPN, n_sel)

    cos2, sin2 = _rope_tiled(S)
    wb = w_in.astype(BF16)
    wcol = lambda i: wb[:, D_MODEL + i * KV: D_MODEL + (i + 1) * KV]
    tn = 2 * KV
    tk_s = min(512, S)
    tq_w = NSA_WINDOW // 2
    proj = _proj(xb, wb[:, :D_MODEL + 2 * KV], cos2, sin2, [2] * (D_MODEL // tn) + [0], S, tn=tn)
    ks3 = _proj(xb, wcol(2), cos2, sin2, [1], S, tn=KV).reshape(B, S, KV)
    vt_s = _proj_kt(xb, wcol(3), cos2, sin2, B, S, tk_s, rope=False)
    kt_w = _proj_kt(xb, wcol(4), cos2, sin2, B, S, tq_w)
    vaug = _proj_vaug(xb, wcol(5), tn=KV)
    wg = jnp.zeros((D_MODEL, LANES), BF16).at[:, :3 * N_HEADS].set(wb[:, D_MODEL + 6 * KV:])
    gates = _proj(xb, wg, cos2, sin2, [0], S, out_dtype=F32, tn=LANES).reshape(B, S, LANES)

    col = lambda i: proj[:, D_MODEL + i * KV: D_MODEL + (i + 1) * KV]
    proj3 = proj.reshape(B, S, proj.shape[1])
    vaug3 = vaug.reshape(B, S, vaug.shape[1])

    ccos, csin = _rope_tables(jnp.arange(NC) * STR + (L - 1))
    ccos = jnp.concatenate([ccos, ccos], axis=1)
    csin = jnp.concatenate([csin, csin], axis=1)
    to_rows = lambda t: t.reshape(B, S, G, HEAD_DIM).transpose(0, 2, 1, 3).reshape(B * G, NC, STR * HEAD_DIM)
    kc = _compress(to_rows(col(0)), cmp_k_w1, cmp_k_pos, cmp_k_w2, ccos, csin, True, n_cmp)
    vc = _compress(to_rows(col(1)), cmp_v_w1, cmp_v_pos, cmp_v_w2, ccos, csin, False, n_cmp)
    kct = kc.reshape(B, G, NC, HEAD_DIM).transpose(0, 1, 3, 2)

    ci = np.arange(NC)[:, None]
    sj = np.arange(LANES)[None, :]
    overlap = ((ci * STR < (sj + 1) * SB) & (ci * STR + L > sj * SB) & (ci < n_cmp) & (sj < n_sel))
    overlap = jnp.broadcast_to(jnp.asarray(overlap.astype(np.float32), BF16), (B, G, NC, LANES))
    vo = jnp.concatenate([vc.reshape(B, G, NC, HEAD_DIM), jnp.ones((B, G, NC, 1), BF16),
                          jnp.zeros((B, G, NC, LANES - HEAD_DIM - 1), BF16), overlap], axis=-1)

    o_cmp, selbt = _nsa_cmp(proj3, kct, vo, gates, n_cmp, n_sel, top_n)
    o_slc = _nsa_slc_t(proj3, ks3, vt_s, _block_onehots(S, SB, tk_s).transpose(0, 2, 1), selbt, gates,
                       tq=tk_s, tk=tk_s)
    o_win = _nsa_win(proj3, kt_w, vaug3, 0, gates, tq=tq_w)
    return [o.reshape(T, D_MODEL) for o in (o_cmp, o_slc, o_win)]


def _moba_mixer(xb, B, S, w_in):
    H = N_HEADS
    nb = S // MOBA_BLOCK
    top_k = min(MOBA_TOPK, nb)
    cos2, sin2 = _rope_tiled(S)
    tn = 512
    n_t = D_MODEL // tn
    wb = w_in.astype(BF16)
    tk = min(2 * MOBA_BLOCK, S)
    qt = _proj_kt(xb, wb[:, :D_MODEL], cos2, sin2, B, S, tk, tn=tn, scale=Q_SCALE_LOG2)
    k3 = _proj(xb, wb[:, D_MODEL:2 * D_MODEL], cos2, sin2, [1] * n_t, S, tn=tn).reshape(B, S, D_MODEL)
    vt = _proj_kt(xb, wb[:, 2 * D_MODEL:], cos2, sin2, B, S, tk, tn=tn, rope=False)
    e = _block_onehots(S, MOBA_BLOCK, tk)
    kh, kl = _moba_kmean_pairs(k3, e, tk)
    o = _moba_attn_t(qt, k3, vt, e.transpose(0, 2, 1), kh, kl, nb, top_k, tq=tk, tk=tk)
    return [o.reshape(B * S, D_MODEL)]


def kernel(x, nsa_w_in, nsa_w_out, nsa_cmp_k_w1, nsa_cmp_k_w2, nsa_cmp_v_w1, nsa_cmp_v_w2, nsa_cmp_k_pos, nsa_cmp_v_pos, moba_w_in, moba_w_out, router_w, router_bias, moe_w_gate, moe_w_up, moe_w_down, ln_g, ln_b):
    B, S, D = x.shape
    xf = x.reshape(B * S, D)
    xb = xf.astype(BF16)
    for layer in range(DEPTH):
        j = layer // 2
        if layer % 2 == 0:
            os_ = _nsa_mixer(xb, B, S, nsa_w_in[j], nsa_cmp_k_w1[j], nsa_cmp_k_w2[j], nsa_cmp_v_w1[j],
                             nsa_cmp_v_w2[j], nsa_cmp_k_pos[j], nsa_cmp_v_pos[j])
            w_out = nsa_w_out[j]
        else:
            os_ = _moba_mixer(xb, B, S, moba_w_in[j])
            w_out = moba_w_out[j]
        xf, xb = _outproj_ln(os_, w_out.astype(BF16), xf, ln_g[layer, 0], ln_b[layer, 0])
        xf, xb = _moe_ln(xb, xf, router_w, router_bias, moe_w_gate[layer].astype(BF16),
                         moe_w_up[layer].astype(BF16), moe_w_down[layer].astype(BF16),
                         ln_g[layer, 1], ln_b[layer, 1])
    return xf.reshape(B, S, D)
```

```python
import functools

import jax
import jax.numpy as jnp
import numpy as np
from jax import lax
from jax.experimental import pallas as pl
from jax.experimental.pallas import tpu as pltpu

F32 = jnp.float32
BF16 = jnp.bfloat16

D_MODEL = 1024
N_HEADS = 16
HEAD_DIM = 64
HALF = HEAD_DIM // 2
ROPE_THETA = 10000.0
DEPTH = 2
DN_ALPHA = (2 * DEPTH) ** 0.25
LN_EPS = 1e-5
NEG = -1e30
FORCE = 1e9
MASK = -1e30
REMOVED = -3.0e38
LANES = 128
Q_SCALE_LOG2 = float(HEAD_DIM ** -0.5 * np.log2(np.e))
ROW_CHUNK = 128
PROJ_ROW_CHUNK = 256

NSA_KV_GROUPS = 4
NSA_HPG = N_HEADS // NSA_KV_GROUPS
NSA_KV_DIM = NSA_KV_GROUPS * HEAD_DIM
NSA_CMP_LEN = 32
NSA_CMP_STRIDE = 16
NSA_SEL_BLOCK = 64
NSA_SEL_TOPN = 16
NSA_WINDOW = 512

MOBA_BLOCK = 256
MOBA_TOPK = 3

N_EXPERTS = 16
N_GROUPS = 4
EXPERTS_PER_GROUP = N_EXPERTS // N_GROUPS
D_EXPERT = 512

VMEM_LIMIT = 48 * 1024 * 1024


def _cparams(sem):
    return pltpu.CompilerParams(dimension_semantics=sem, vmem_limit_bytes=VMEM_LIMIT)


def _dot(a, b):
    return jnp.dot(a, b, preferred_element_type=F32)


def _split_bf16(x):
    hi = x.astype(BF16)
    lo = (x - hi.astype(F32)).astype(BF16)
    return hi, lo


def _sigmoid(x):
    return 1.0 / (1.0 + jnp.exp(-x))


def _proj_body(mode_ref, x_ref, w_ref, cos_ref, sin_ref, o_ref, *, tn):
    mode = mode_ref[pl.program_id(0)]
    tm = x_ref.shape[0]
    rc = min(PROJ_ROW_CHUNK, tm)

    @pl.when(mode == 0)
    def _():
        for r in range(0, tm, rc):
            o_ref[r:r + rc, :] = _dot(x_ref[r:r + rc, :], w_ref[...]).astype(o_ref.dtype)

    @pl.when(mode != 0)
    def _():
        sc = jnp.where(mode == 2, Q_SCALE_LOG2, 1.0).astype(F32)
        for r in range(0, tm, rc):
            acc = _dot(x_ref[r:r + rc, :], w_ref[...])
            cos = cos_ref[r:r + rc, :] * sc
            sin = sin_ref[r:r + rc, :] * sc
            for c in range(tn // LANES):
                o_ref[r:r + rc, c * LANES:(c + 1) * LANES] = (
                    _rope_chunk(acc[:, c * LANES:(c + 1) * LANES], cos, sin).astype(o_ref.dtype))


def _proj(xb, w, cos2, sin2, modes, seq, out_dtype=BF16, tm=1024, tn=512):
    T, K = xb.shape
    N = w.shape[1]
    tm = min(tm, seq)
    assert T % tm == 0 and N % tn == 0 and seq % tm == 0 and len(modes) == N // tn
    n_pos = seq // tm
    grid_spec = pltpu.PrefetchScalarGridSpec(
        num_scalar_prefetch=1,
        grid=(N // tn, T // tm),
        in_specs=[
            pl.BlockSpec((tm, K), lambda j, i, m: (i, 0)),
            pl.BlockSpec((K, tn), lambda j, i, m: (0, j)),
            pl.BlockSpec((tm, LANES), lambda j, i, m: (i % n_pos, 0)),
            pl.BlockSpec((tm, LANES), lambda j, i, m: (i % n_pos, 0)),
        ],
        out_specs=pl.BlockSpec((tm, tn), lambda j, i, m: (i, j)),
    )
    return pl.pallas_call(
        functools.partial(_proj_body, tn=tn),
        grid_spec=grid_spec,
        out_shape=jax.ShapeDtypeStruct((T, N), out_dtype),
        compiler_params=_cparams(("parallel", "parallel")),
    )(jnp.asarray(modes, jnp.int32), xb, w, cos2, sin2)


def _rope_tables(pos):
    inv = 1.0 / (ROPE_THETA ** (jnp.arange(0, HEAD_DIM, 2, dtype=F32) / HEAD_DIM))
    ang = pos.astype(F32)[:, None] * inv[None, :]
    return jnp.cos(ang), jnp.sin(ang)


def _rope_chunk(a, cos, sin):
    lane = lax.broadcasted_iota(jnp.int32, a.shape, 1)
    up = pltpu.roll(a, LANES - HALF, 1)
    dn = pltpu.roll(a, HALF, 1)
    return a * cos + jnp.where((lane % HEAD_DIM) < HALF, -up, dn) * sin


def _proj_kt_body(x_ref, w_ref, cos_ref, sin_ref, o_ref, *, tn, tk, rope, scale):
    tm = x_ref.shape[0]
    rc = min(PROJ_ROW_CHUNK, tk)
    for r in range(0, tm, rc):
        acc = _dot(x_ref[r:r + rc, :], w_ref[...])
        cos = cos_ref[r:r + rc, :] * scale
        sin = sin_ref[r:r + rc, :] * scale
        for c in range(tn // LANES):
            a = acc[:, c * LANES:(c + 1) * LANES]
            kt = (_rope_chunk(a, cos, sin) if rope else a).T
            for hh in range(2):
                o_ref[0, 2 * c + hh, r // tk, :, r % tk:r % tk + rc] = (
                    kt[hh * HEAD_DIM:(hh + 1) * HEAD_DIM, :].astype(o_ref.dtype))


def _proj_kt(xb, w, cos2, sin2, B, S, tk, tm=1024, tn=256, rope=True, scale=1.0):
    T, K = xb.shape
    N = w.shape[1]
    tm = min(tm, S)
    tn = min(tn, N)
    assert S % tm == 0 and tm % tk == 0 and N % tn == 0
    n_pos = S // tm
    return pl.pallas_call(
        functools.partial(_proj_kt_body, tn=tn, tk=tk, rope=rope, scale=scale),
        grid=(N // tn, T // tm),
        in_specs=[
            pl.BlockSpec((tm, K), lambda j, i: (i, 0)),
            pl.BlockSpec((K, tn), lambda j, i: (0, j)),
            pl.BlockSpec((tm, LANES), lambda j, i: (i % n_pos, 0)),
            pl.BlockSpec((tm, LANES), lambda j, i: (i % n_pos, 0)),
        ],
        out_specs=pl.BlockSpec((1, tn // HEAD_DIM, tm // tk, HEAD_DIM, tk),
                               lambda j, i: (i // n_pos, j, i % n_pos, 0, 0)),
        out_shape=jax.ShapeDtypeStruct((B, N // HEAD_DIM, S // tk, HEAD_DIM, tk), BF16),
        compiler_params=_cparams(("parallel", "parallel")),
    )(xb, w, cos2, sin2)


def _proj_vaug_body(x_ref, w_ref, o_ref, *, tn):
    tm = x_ref.shape[0]
    rc = min(PROJ_ROW_CHUNK, tm)
    lane = lax.broadcasted_iota(jnp.int32, (rc, LANES), 1)
    tail = jnp.where(lane == HEAD_DIM, 1.0, 0.0)
    for r in range(0, tm, rc):
        acc = _dot(x_ref[r:r + rc, :], w_ref[...])
        for c in range(tn // LANES):
            a = acc[:, c * LANES:(c + 1) * LANES]
            for k, head in enumerate((a, pltpu.roll(a, HEAD_DIM, 1))):
                o_ref[r:r + rc, (2 * c + k) * LANES:(2 * c + k + 1) * LANES] = (
                    jnp.where(lane < HEAD_DIM, head, tail).astype(o_ref.dtype))


def _proj_vaug(xb, w, tm=1024, tn=512):
    T, K = xb.shape
    N = w.shape[1]
    tm = min(tm, T)
    assert T % tm == 0 and N % tn == 0
    return pl.pallas_call(
        functools.partial(_proj_vaug_body, tn=tn),
        grid=(N // tn, T // tm),
        in_specs=[pl.BlockSpec((tm, K), lambda j, i: (i, 0)), pl.BlockSpec((K, tn), lambda j, i: (0, j))],
        out_specs=pl.BlockSpec((tm, 2 * tn), lambda j, i: (i, j)),
        out_shape=jax.ShapeDtypeStruct((T, 2 * N), BF16),
        compiler_params=_cparams(("parallel", "parallel")),
    )(xb, w)


def _gelu_tanh(x):
    c = np.float32(np.sqrt(2.0 / np.pi))
    return 0.5 * x * (1.0 + jnp.tanh(c * (x + 0.044715 * (x * x * x))))


def _compress_body(r_ref, w1_ref, pos_ref, w2_ref, w2r_ref, cos_ref, sin_ref, o_ref, *, rope, n_cmp):
    r = r_ref[0]
    nc = r.shape[0]
    half = NSA_CMP_STRIDE * HEAD_DIM
    a = _dot(r, w1_ref[0])
    b = _dot(r, w1_ref[1])
    pos = pos_ref[...]
    pb = _dot(pos[:, :half], w1_ref[0]) + _dot(pos[:, half:], w1_ref[1])
    b_next = pltpu.roll(b, nc - 1, 0)
    h = _gelu_tanh(a + b_next + pb[0:1, :]).astype(BF16)
    o = _dot(h, w2_ref[...])
    if rope:
        o = o * cos_ref[...] + _dot(h, w2r_ref[...]) * sin_ref[...]
    row = lax.broadcasted_iota(jnp.int32, o.shape, 0)
    o_ref[0] = jnp.where(row < n_cmp, o, 0.0).astype(o_ref.dtype)


def _compress(r, w1, pos, w2, cos_c, sin_c, rope, n_cmp):
    BG, NC, K = r.shape
    hidden = w1.shape[1]
    w1s = w1.astype(BF16).reshape(2, K, hidden)
    pos8 = jnp.zeros((8, 2 * K), BF16).at[0].set(pos.reshape(-1).astype(BF16))
    w2r = jnp.concatenate([-w2[:, HALF:], w2[:, :HALF]], axis=1).astype(BF16)
    full = lambda shape: pl.BlockSpec(shape, lambda i: (0,) * len(shape))
    return pl.pallas_call(
        functools.partial(_compress_body, rope=rope, n_cmp=n_cmp),
        grid=(BG,),
        in_specs=[
            pl.BlockSpec((1, NC, K), lambda i: (i, 0, 0)),
            full((2, K, hidden)),
            full((8, 2 * K)),
            full((hidden, HEAD_DIM)),
            full((hidden, HEAD_DIM)),
            full((NC, HEAD_DIM)),
            full((NC, HEAD_DIM)),
        ],
        out_specs=pl.BlockSpec((1, NC, HEAD_DIM), lambda i: (i, 0, 0)),
        out_shape=jax.ShapeDtypeStruct((BG, NC, HEAD_DIM), BF16),
        compiler_params=_cparams(("parallel",)),
    )(r, w1s, pos8, w2.astype(BF16), w2r, cos_c, sin_c)


def _topk_mask_t(v, k):
    idx = lax.broadcasted_iota(jnp.int32, v.shape, 0).astype(F32)

    def step(_, cur):
        m = jnp.max(cur, axis=0, keepdims=True)
        first = jnp.min(jnp.where(cur == m, idx, float(LANES)), axis=0, keepdims=True)
        return jnp.where(idx == first, REMOVED, cur)

    return lax.fori_loop(0, k, step, v, unroll=True) != v


def _head_kt(t, parity):
    z = jnp.zeros_like(t)
    return jnp.concatenate([t, z] if parity == 0 else [z, t], axis=0)


def _pair_merge(even, odd):
    lane = lax.broadcasted_iota(jnp.int32, even.shape, 1)
    return jnp.where(lane < HEAD_DIM, even, pltpu.roll(odd, HEAD_DIM, 1))


def _gate_col(logits, branch, h, heads_per_step=NSA_HPG):
    idx = branch * N_HEADS + pl.program_id(1) * heads_per_step + h
    lane = lax.broadcasted_iota(jnp.int32, logits.shape, 1)
    return _sigmoid(jnp.sum(jnp.where(lane == idx, logits, 0.0), axis=-1, keepdims=True))


def _tile_iotas(rows, tk):
    return (lax.broadcasted_iota(jnp.int32, (rows, tk), 0),
            lax.broadcasted_iota(jnp.int32, (rows, tk), 1))


def _nsa_cmp_body(q_ref, kct_ref, vo_ref, g_ref, o_ref, selbt_ref, s_ref, m_ref, imp_ref,
                  *, tq, n_cmp, n_sel, top_n):
    s0 = pl.program_id(2) * tq
    q4 = q_ref[0]
    n_chunks = vo_ref.shape[2] // LANES
    rc = min(ROW_CHUNK, tq)

    def attend(nv):
        nc = nv * LANES
        kct = kct_ref[0, 0, :, :nc]
        vo = vo_ref[0, 0, :nc, :]
        tpos = s0 + lax.broadcasted_iota(jnp.int32, (tq, nc), 0)
        nidx = lax.broadcasted_iota(jnp.int32, (tq, nc), 1)
        cmask = (nidx * NSA_CMP_STRIDE + (NSA_CMP_LEN - 1) <= tpos) & (nidx < n_cmp)
        for h in range(NSA_HPG):
            pair = q4[:, (h // 2) * LANES:(h // 2 + 1) * LANES]
            s = jnp.where(cmask, _dot(pair, _head_kt(kct, h % 2)), NEG)
            s_ref[h * tq:(h + 1) * tq, :nc] = s
            cm = s[:, :LANES]
            for c in range(1, nv):
                cm = jnp.maximum(cm, s[:, c * LANES:(c + 1) * LANES])
            m_ref[h * tq:(h + 1) * tq, :] = jnp.broadcast_to(jnp.max(cm, axis=-1, keepdims=True), (tq, LANES))

        for r0 in range(0, tq, rc):
            imp = jnp.zeros((rc, LANES), F32)
            heads = []
            for h in range(NSA_HPG):
                r = h * tq + r0
                m = m_ref[r:r + rc, :]
                e = jnp.concatenate(
                    [jnp.exp2(s_ref[r:r + rc, c * LANES:(c + 1) * LANES] - m) for c in range(nv)],
                    axis=1).astype(BF16)
                res = _dot(e, vo)
                inv = jnp.where(m[:, :1] > 0.5 * NEG, 1.0 / res[:, HEAD_DIM:HEAD_DIM + 1], 0.0)
                gate = _gate_col(g_ref[0, r0:r0 + rc, :], 0, h)
                heads.append(res[:, :LANES] * (inv * gate))
                imp = imp + res[:, LANES:] * inv
            o_ref[0, r0:r0 + rc, :] = jnp.concatenate(
                [_pair_merge(heads[h], heads[h + 1]) for h in range(0, NSA_HPG, 2)], axis=1).astype(o_ref.dtype)
            imp_ref[r0:r0 + rc, :] = imp

    attend(n_chunks)
    imp = imp_ref[...]

    blk = lax.broadcasted_iota(jnp.int32, imp.shape, 1)
    jq = (s0 + lax.broadcasted_iota(jnp.int32, imp.shape, 0)) // NSA_SEL_BLOCK
    forced = (blk == 0) | (blk == jq) | (blk == jq - 1)
    imp = jnp.where(blk > jq, NEG, jnp.where(forced, FORCE, imp))
    imp = jnp.where(blk < n_sel, imp, REMOVED)
    sel_t = _topk_mask_t(imp.T, top_n)
    selbt_ref[0, 0] = jnp.where(sel_t, 0.0, MASK).astype(selbt_ref.dtype)


def _nsa_cmp(proj3, kct, vo, g_cmp, n_cmp, n_sel, top_n, tq=256):
    B, S, _ = proj3.shape
    G, HPG = NSA_KV_GROUPS, NSA_HPG
    NC = vo.shape[2]
    tq = min(tq, S)
    assert NC % LANES == 0
    return pl.pallas_call(
        functools.partial(_nsa_cmp_body, tq=tq, n_cmp=n_cmp, n_sel=n_sel, top_n=top_n),
        grid=(B, G, S // tq),
        in_specs=[
            pl.BlockSpec((1, tq, HPG * HEAD_DIM), lambda b, g, i: (b, i, g)),
            pl.BlockSpec((1, 1, HEAD_DIM, NC), lambda b, g, i: (b, g, 0, 0)),
            pl.BlockSpec((1, 1, NC, 2 * LANES), lambda b, g, i: (b, g, 0, 0)),
            pl.BlockSpec((1, tq, LANES), lambda b, g, i: (b, i, 0)),
        ],
        out_specs=[
            pl.BlockSpec((1, tq, HPG * HEAD_DIM), lambda b, g, i: (b, i, g)),
            pl.BlockSpec((1, 1, LANES, tq), lambda b, g, i: (b, g, 0, i)),
        ],
        out_shape=[
            jax.ShapeDtypeStruct((B, S, D_MODEL), BF16),
            jax.ShapeDtypeStruct((B, G, LANES, S), BF16),
        ],
        scratch_shapes=[pltpu.VMEM((HPG * tq, NC), F32), pltpu.VMEM((HPG * tq, LANES), F32),
                        pltpu.VMEM((tq, LANES), F32)],
        compiler_params=_cparams(("parallel", "parallel", "parallel")),
    )(proj3, kct, vo, g_cmp)


def _nsa_win_body(q_ref, kt_ref, v_ref, g_ref, o_ref, s_ref, m_ref, *, tq, n_tiles):
    s0 = pl.program_id(2) * tq
    tk = n_tiles * tq
    k_first = jnp.maximum(pl.program_id(2) - (n_tiles - 1), 0)
    k0 = pl.multiple_of(k_first * tq, tq)
    kt = jnp.concatenate([kt_ref[0, 0, k_first + j] for j in range(n_tiles)], axis=1)
    v = v_ref[0, pl.ds(k0, tk), :]
    row, col = _tile_iotas(tq, tk)
    dist = (s0 - k0) + row - col
    valid = (dist >= 0) & (dist < NSA_WINDOW)
    q4 = q_ref[0]
    for h in range(NSA_HPG):
        pair = q4[:, (h // 2) * LANES:(h // 2 + 1) * LANES]
        s = jnp.where(valid, _dot(pair, _head_kt(kt, h % 2)), MASK)
        s_ref[h * tq:(h + 1) * tq, :] = s
        cm = s[:, :LANES]
        for c in range(1, tk // LANES):
            cm = jnp.maximum(cm, s[:, c * LANES:(c + 1) * LANES])
        m_ref[h * tq:(h + 1) * tq, :] = jnp.broadcast_to(jnp.max(cm, axis=-1, keepdims=True), (tq, LANES))

    rc = min(ROW_CHUNK, tq)
    for r0 in range(0, tq, rc):
        heads = []
        for h in range(NSA_HPG):
            r = h * tq + r0
            m = m_ref[r:r + rc, :]
            p = jnp.concatenate(
                [jnp.exp2(s_ref[r:r + rc, c * LANES:(c + 1) * LANES] - m) for c in range(tk // LANES)],
                axis=1).astype(BF16)
            res = _dot(p, v)
            heads.append(res * (_gate_col(g_ref[0, r0:r0 + rc, :], 2, h) / res[:, HEAD_DIM:HEAD_DIM + 1]))
        o_ref[0, r0:r0 + rc, :] = jnp.concatenate(
            [_pair_merge(heads[h], heads[h + 1]) for h in range(0, NSA_HPG, 2)], axis=1).astype(o_ref.dtype)


def _nsa_win(proj3, kt, vaug3, v_block, g, tq):
    B, S, _ = proj3.shape
    G, HPG = NSA_KV_GROUPS, NSA_HPG
    nk = S // tq
    n_tiles = NSA_WINDOW // tq + 1
    assert NSA_WINDOW % tq == 0 and nk >= n_tiles and kt.shape == (B, G, nk, HEAD_DIM, tq)
    once = pl.Buffered(1)
    return pl.pallas_call(
        functools.partial(_nsa_win_body, tq=tq, n_tiles=n_tiles),
        grid=(B, G, S // tq),
        in_specs=[
            pl.BlockSpec((1, tq, HPG * HEAD_DIM), lambda b, g_, i: (b, i, g_)),
            pl.BlockSpec((1, 1, nk, HEAD_DIM, tq), lambda b, g_, i: (b, g_, 0, 0, 0), pipeline_mode=once),
            pl.BlockSpec((1, S, LANES), lambda b, g_, i: (b, 0, v_block + g_), pipeline_mode=once),
            pl.BlockSpec((1, tq, LANES), lambda b, g_, i: (b, i, 0)),
        ],
        out_specs=pl.BlockSpec((1, tq, HPG * HEAD_DIM), lambda b, g_, i: (b, i, g_)),
        out_shape=jax.ShapeDtypeStruct((B, S, D_MODEL), BF16),
        scratch_shapes=[pltpu.VMEM((HPG * tq, n_tiles * tq), F32), pltpu.VMEM((HPG * tq, LANES), F32)],
        compiler_params=_cparams(("parallel", "parallel", "parallel")),
    )(proj3, kt, vaug3, g)


VALUE_SUB = 2 * LANES


def _flash_t_scratch(n_heads, tq, tk):
    return [pltpu.VMEM((n_heads * 2 * LANES, tq), BF16),
            pltpu.VMEM((n_heads * tk, tq), F32),
            pltpu.VMEM((n_heads * 8, tq), F32),
            pltpu.VMEM((n_heads * 8, tq), F32),
            pltpu.VMEM((n_heads * LANES, tq), F32)]


def _flash_t_init(m_ref, acc_ref):
    m_ref[...] = jnp.full(m_ref.shape, -jnp.inf, F32)
    acc_ref[...] = jnp.zeros(acc_ref.shape, F32)


def _flash_t_tile(n_heads, keys_of, vals_of, mask, qa_ref, s_ref, m_ref, alpha_ref, acc_ref):
    aug = 2 * LANES
    tk, tq = s_ref.shape[0] // n_heads, s_ref.shape[1]
    for h in range(n_heads):
        st = _dot(keys_of(h), qa_ref[h * aug:(h + 1) * aug, :])
        if mask is not None:
            st = jnp.where(mask, st, MASK)
        s_ref[h * tk:(h + 1) * tk, :] = st
        m_prev = m_ref[h * 8:(h + 1) * 8, :]
        m_new = jnp.maximum(m_prev, jnp.max(st, axis=0, keepdims=True))
        alpha_ref[h * 8:(h + 1) * 8, :] = jnp.exp2(m_prev - m_new)
        m_ref[h * 8:(h + 1) * 8, :] = m_new
    for h in range(n_heads):
        vals = vals_of(h)
        for q0 in range(0, tq, VALUE_SUB):
            m = m_ref[h * 8:h * 8 + 1, q0:q0 + VALUE_SUB]
            part = jnp.zeros((LANES, VALUE_SUB), F32)
            for c0 in range(0, tk, VALUE_SUB):
                pt = jnp.exp2(s_ref[h * tk + c0:h * tk + c0 + VALUE_SUB, q0:q0 + VALUE_SUB] - m).astype(BF16)
                part = part + _dot(vals[:, c0:c0 + VALUE_SUB], pt)
            acc_ref[h * LANES:(h + 1) * LANES, q0:q0 + VALUE_SUB] = (
                alpha_ref[h * 8:h * 8 + 1, q0:q0 + VALUE_SUB]
                * acc_ref[h * LANES:(h + 1) * LANES, q0:q0 + VALUE_SUB] + part)


def _flash_t_loop(tile, s0, tq, tk):
    last = (s0 + tq - 1) // tk

    def full_tile(ki, carry):
        tile(ki, None)
        return carry

    lax.fori_loop(0, last, full_tile, 0)
    krow, qcol = _tile_iotas(tk, tq)
    tile(last, last * tk + krow <= s0 + qcol)


def _flash_t_out(acc_ref, n_heads, gates):
    def head(h):
        acc = acc_ref[h * LANES:(h + 1) * LANES, :]
        return (acc / acc[HEAD_DIM:HEAD_DIM + 1, :]).T * gates[h]
    return jnp.concatenate([_pair_merge(head(h), head(h + 1)) for h in range(0, n_heads, 2)], axis=1)


def _ones_row_block(tk):
    return jnp.where(lax.broadcasted_iota(jnp.int32, (HEAD_DIM, tk), 0) == 0, 1.0, 0.0).astype(BF16)


def _nsa_slc_t_body(q_ref, k_ref, vt_ref, et_ref, selbt_ref, g_ref, o_ref, qa_ref, s_ref, m_ref, alpha_ref,
                    acc_ref, *, tq, tk):
    s0 = pl.program_id(2) * tq
    n_heads = 2 * NSA_HPG
    _flash_t_init(m_ref, acc_ref)
    q8 = q_ref[0]
    for pair in range(n_heads // 2):
        grp = pair // (NSA_HPG // 2)
        pair_t = q8[:, pair * LANES:(pair + 1) * LANES].astype(F32).T
        for hp in range(2):
            qt = pair_t[hp * HEAD_DIM:(hp + 1) * HEAD_DIM, :].astype(BF16)
            h = 2 * pair + hp
            qa_ref[h * 2 * LANES:(h + 1) * 2 * LANES, :] = jnp.concatenate(
                [selbt_ref[0, grp], _head_kt(qt, grp)], axis=0)
    ones_rows = _ones_row_block(tk)

    def tile(ki, mask):
        k0 = pl.multiple_of(ki * tk, tk)
        keys = jnp.concatenate([et_ref[ki], k_ref[0, pl.ds(k0, tk), :]], axis=1)
        vals_of = lambda h: jnp.concatenate([vt_ref[0, h // NSA_HPG, ki], ones_rows], axis=0)
        _flash_t_tile(n_heads, lambda h: keys, vals_of, mask, qa_ref, s_ref, m_ref, alpha_ref, acc_ref)

    _flash_t_loop(tile, s0, tq, tk)
    gates = [_gate_col(g_ref[0], 1, h, n_heads) for h in range(n_heads)]
    o_ref[0] = _flash_t_out(acc_ref, n_heads, gates).astype(o_ref.dtype)


def _nsa_slc_t(proj3, k3, vt, et, selbt, g, tq, tk):
    B, S, _ = proj3.shape
    G, HPG = NSA_KV_GROUPS, NSA_HPG
    nk = S // tk
    assert tk == tq and G % 2 == 0 and vt.shape == (B, G, nk, HEAD_DIM, tk) and et.shape == (nk, tk, LANES)
    once = pl.Buffered(1)
    width = 2 * HPG * HEAD_DIM
    return pl.pallas_call(
        functools.partial(_nsa_slc_t_body, tq=tq, tk=tk),
        grid=(B, G // 2, S // tq),
        in_specs=[
            pl.BlockSpec((1, tq, width), lambda b, p, i: (b, i, p)),
            pl.BlockSpec((1, S, LANES), lambda b, p, i: (b, 0, p), pipeline_mode=once),
            pl.BlockSpec((1, 2, nk, HEAD_DIM, tk), lambda b, p, i: (b, p, 0, 0, 0), pipeline_mode=once),
            pl.BlockSpec((nk, tk, LANES), lambda b, p, i: (0, 0, 0), pipeline_mode=once),
            pl.BlockSpec((1, 2, LANES, tq), lambda b, p, i: (b, p, 0, i)),
            pl.BlockSpec((1, tq, LANES), lambda b, p, i: (b, i, 0)),
        ],
        out_specs=pl.BlockSpec((1, tq, width), lambda b, p, i: (b, i, p)),
        out_shape=jax.ShapeDtypeStruct((B, S, D_MODEL), BF16),
        scratch_shapes=_flash_t_scratch(2 * HPG, tq, tk),
        compiler_params=_cparams(("parallel", "parallel", "parallel")),
    )(proj3, k3, vt, et, selbt, g)


def _moba_kmean_pairs_body(k_ref, e_ref, hi_ref, lo_ref, *, tk):
    nk = e_ref.shape[0]
    km = jnp.zeros((LANES, LANES), F32)
    for ki in range(nk):
        km = km + _dot(e_ref[ki], k_ref[0, ki * tk:(ki + 1) * tk, :])
    hi, lo = _split_bf16(km * (1.0 / MOBA_BLOCK))
    hi_ref[0, 0] = hi
    lo_ref[0, 0] = lo


def _moba_kmean_pairs(k3, e, tk):
    B, S, N = k3.shape
    n_pairs = N // LANES
    nk = S // tk
    spec = pl.BlockSpec((1, 1, LANES, LANES), lambda b, p: (b, p, 0, 0))
    return pl.pallas_call(
        functools.partial(_moba_kmean_pairs_body, tk=tk),
        grid=(B, n_pairs),
        in_specs=[
            pl.BlockSpec((1, S, LANES), lambda b, p: (b, 0, p)),
            pl.BlockSpec((nk, LANES, tk), lambda b, p: (0, 0, 0)),
        ],
        out_specs=[spec, spec],
        out_shape=[jax.ShapeDtypeStruct((B, n_pairs, LANES, LANES), BF16)] * 2,
        compiler_params=_cparams(("parallel", "parallel")),
    )(k3, e)


def _moba_t_body(qt_ref, k_ref, vt_ref, et_ref, kh_ref, kl_ref, o_ref, qa_ref, s_ref, m_ref, alpha_ref,
                 acc_ref, *, tq, tk, hb, nb, top_k):
    s0 = pl.program_id(2) * tq
    _flash_t_init(m_ref, acc_ref)
    aug = 2 * LANES

    blk = lax.broadcasted_iota(jnp.int32, (LANES, tq), 0)
    cb = (s0 + lax.broadcasted_iota(jnp.int32, (LANES, tq), 1)) // MOBA_BLOCK
    for h in range(hb):
        qpad = _head_kt(qt_ref[0, h, 0], h % 2)
        gsc = _dot(kh_ref[0, h // 2], qpad) + _dot(kl_ref[0, h // 2], qpad)
        gsc = jnp.where(blk < cb, gsc, NEG)
        gsc = jnp.where(blk < nb, gsc, REMOVED)
        sel = (_topk_mask_t(gsc, top_k) & (blk < cb)) | (blk == cb)
        qa_ref[h * aug:(h + 1) * aug, :] = jnp.concatenate(
            [jnp.where(sel, 0.0, MASK).astype(BF16), qpad], axis=0)

    ones_rows = _ones_row_block(tk)

    def tile(ki, mask):
        k0 = pl.multiple_of(ki * tk, tk)
        et = et_ref[ki]
        keys_of = lambda h: jnp.concatenate(
            [et, k_ref[0, pl.ds(k0, tk), (h // 2) * LANES:(h // 2 + 1) * LANES]], axis=1)
        vals_of = lambda h: jnp.concatenate([vt_ref[0, h, ki], ones_rows], axis=0)
        _flash_t_tile(hb, keys_of, vals_of, mask, qa_ref, s_ref, m_ref, alpha_ref, acc_ref)

    _flash_t_loop(tile, s0, tq, tk)
    o_ref[0] = _flash_t_out(acc_ref, hb, [1.0] * hb).astype(o_ref.dtype)


def _moba_attn_t(qt, k3, vt, et, kh, kl, nb, top_k, tq, tk, hb=8):
    B, H, nq, _, _ = qt.shape
    S = k3.shape[1]
    nk = S // tk
    assert tk == tq and tk % MOBA_BLOCK == 0 and H % hb == 0 and hb % 2 == 0 and nb <= LANES
    once = pl.Buffered(1)
    return pl.pallas_call(
        functools.partial(_moba_t_body, tq=tq, tk=tk, hb=hb, nb=nb, top_k=top_k),
        grid=(B, H // hb, nq),
        in_specs=[
            pl.BlockSpec((1, hb, 1, HEAD_DIM, tq), lambda b, h, i: (b, h, i, 0, 0)),
            pl.BlockSpec((1, S, hb * HEAD_DIM), lambda b, h, i: (b, 0, h), pipeline_mode=once),
            pl.BlockSpec((1, hb, nk, HEAD_DIM, tk), lambda b, h, i: (b, h, 0, 0, 0), pipeline_mode=once),
            pl.BlockSpec((nk, tk, LANES), lambda b, h, i: (0, 0, 0), pipeline_mode=once),
            pl.BlockSpec((1, hb // 2, LANES, LANES), lambda b, h, i: (b, h, 0, 0), pipeline_mode=once),
            pl.BlockSpec((1, hb // 2, LANES, LANES), lambda b, h, i: (b, h, 0, 0), pipeline_mode=once),
        ],
        out_specs=pl.BlockSpec((1, tq, hb * HEAD_DIM), lambda b, h, i: (b, i, h)),
        out_shape=jax.ShapeDtypeStruct((B, S, D_MODEL), BF16),
        scratch_shapes=_flash_t_scratch(hb, tq, tk),
        compiler_params=_cparams(("parallel", "parallel", "parallel")),
    )(qt, k3, vt, et, kh, kl)


def _layer_norm(r, g, b):
    mu = jnp.mean(r, axis=-1, keepdims=True)
    c = r - mu
    var = jnp.mean(c * c, axis=-1, keepdims=True)
    return c * lax.rsqrt(var + LN_EPS) * g + b


def _outproj_body(*refs, n_o):
    o_refs = refs[:n_o]
    w_ref, x_ref, g_ref, b_ref, y_ref, yb_ref = refs[n_o:]
    tm = x_ref.shape[0]
    rc = min(PROJ_ROW_CHUNK, tm)
    for r0 in range(0, tm, rc):
        rows = slice(r0, r0 + rc)
        o = o_refs[0][rows, :].astype(F32)
        for ref in o_refs[1:]:
            o = o + ref[rows, :].astype(F32)
        mix = _dot(o.astype(BF16), w_ref[...])
        y = _layer_norm(DN_ALPHA * x_ref[rows, :] + mix, g_ref[...], b_ref[...])
        y_ref[rows, :] = y
        yb_ref[rows, :] = y.astype(BF16)


def _outproj_ln(os_, w, x, g, b, tm=512):
    T, D = x.shape
    tm = min(tm, T)
    n_o = len(os_)
    row = pl.BlockSpec((tm, D), lambda i: (i, 0))
    vec = pl.BlockSpec((1, D), lambda i: (0, 0))
    return pl.pallas_call(
        functools.partial(_outproj_body, n_o=n_o),
        grid=(T // tm,),
        in_specs=[row] * n_o + [pl.BlockSpec((D, D), lambda i: (0, 0)), row, vec, vec],
        out_specs=[row, row],
        out_shape=[jax.ShapeDtypeStruct((T, D), F32), jax.ShapeDtypeStruct((T, D), BF16)],
        compiler_params=_cparams(("parallel",)),
    )(*os_, w, x, g.reshape(1, D), b.reshape(1, D))


GID_LANE = N_EXPERTS
MOE_WINDOW = 1024
MOE_CHUNK = 128
MOE_VMEM_LIMIT = 60 * 1024 * 1024


def _router_sorted_body(x_ref, wh_ref, wl_ref, bias_ref, gate_ref, gidt_ref, cnt_ref):
    x_hi, x_lo = _split_bf16(x_ref[...])
    wh = wh_ref[...]
    logits = _dot(x_hi, wh) + _dot(x_lo, wh) + _dot(x_hi, wl_ref[...])
    scores = _sigmoid(logits)
    lane = lax.broadcasted_iota(jnp.int32, scores.shape, 1)
    lanef = lane.astype(F32)
    biased = jnp.where(lane < N_EXPERTS, scores + bias_ref[...], REMOVED)

    def top2(mask):
        v = jnp.where(mask, biased, REMOVED)
        m1 = jnp.max(v, axis=-1, keepdims=True)
        i1 = jnp.min(jnp.where(v == m1, lanef, float(LANES)), axis=-1, keepdims=True)
        v2 = jnp.where(lanef == i1, REMOVED, v)
        m2 = jnp.max(v2, axis=-1, keepdims=True)
        i2 = jnp.min(jnp.where(v2 == m2, lanef, float(LANES)), axis=-1, keepdims=True)
        return m1 + m2, jnp.where((lanef == i1) | (lanef == i2), 1.0, 0.0)

    best, best_sel = top2(lane // EXPERTS_PER_GROUP == 0)
    gid = jnp.zeros_like(best)
    for grp in range(1, N_GROUPS):
        score, sel = top2(lane // EXPERTS_PER_GROUP == grp)
        better = score > best
        best = jnp.where(better, score, best)
        best_sel = jnp.where(better, sel, best_sel)
        gid = jnp.where(better, float(grp), gid)
    w = best_sel * scores
    gate = w / jnp.sum(w, axis=-1, keepdims=True)
    gate_ref[...] = jnp.where(lane == GID_LANE, gid, gate)
    gid_b = jnp.broadcast_to(gid, scores.shape)
    gidt_ref[...] = gid_b.T[:8, :]
    cnt_ref[0] = jnp.broadcast_to(
        jnp.sum(jnp.where(lanef == gid_b, 1.0, 0.0), axis=0, keepdims=True), (8, LANES))


def _router_sorted(x, router_w, router_bias, tm):
    T, D = x.shape
    wpad = jnp.zeros((D, LANES), F32).at[:, :N_EXPERTS].set(router_w)
    wh, wl = _split_bf16(wpad)
    bpad = jnp.zeros((1, LANES), F32).at[0, :N_EXPERTS].set(router_bias)
    return pl.pallas_call(
        _router_sorted_body,
        grid=(T // tm,),
        in_specs=[
            pl.BlockSpec((tm, D), lambda i: (i, 0)),
            pl.BlockSpec((D, LANES), lambda i: (0, 0)),
            pl.BlockSpec((D, LANES), lambda i: (0, 0)),
            pl.BlockSpec((1, LANES), lambda i: (0, 0)),
        ],
        out_specs=[
            pl.BlockSpec((tm, LANES), lambda i: (i, 0)),
            pl.BlockSpec((8, tm), lambda i: (0, i)),
            pl.BlockSpec((1, 8, LANES), lambda i: (i, 0, 0)),
        ],
        out_shape=[
            jax.ShapeDtypeStruct((T, LANES), F32),
            jax.ShapeDtypeStruct((8, T), F32),
            jax.ShapeDtypeStruct((T // tm, 8, LANES), F32),
        ],
        compiler_params=_cparams(("parallel",)),
    )(x, wh, wl, bpad)


def _experts_sorted_body(cnt_ref, xb_ref, x_ref, gate_ref, gidt_ref, ltri_ref, utri_ref, wg_ref, wu_ref,
                         wd_ref, g_ref, b_ref, y_ref, yb_ref, xs_ref, gs_ref, acc_ref, rank_ref,
                         *, W, Wp, chunk):
    win = pl.program_id(0)
    e = pl.program_id(1)
    grp = e // EXPERTS_PER_GROUP
    padded = [((cnt_ref[win * N_GROUPS + g] + chunk - 1) // chunk) * chunk for g in range(N_GROUPS)]
    starts = [0]
    for g in range(N_GROUPS - 1):
        starts.append(starts[-1] + padded[g])
    start = starts[0]
    for g in range(1, N_GROUPS):
        start = jnp.where(grp == g, starts[g], start)
    n_chunks = (cnt_ref[win * N_GROUPS + grp] + chunk - 1) // chunk

    @pl.when(e == 0)
    def _():
        gate = gate_ref[...]
        lane = lax.broadcasted_iota(jnp.int32, gate.shape, 1)
        lanef = lane.astype(F32)
        gid = jnp.sum(jnp.where(lane == GID_LANE, gate, 0.0), axis=-1, keepdims=True)
        member = jnp.where((lanef == gid) & (lane < N_GROUPS), 1.0, 0.0)
        earlier = _dot(ltri_ref[...], member.astype(BF16))
        first = jnp.zeros(gate.shape, F32)
        for g in range(1, N_GROUPS):
            first = jnp.where(lane == g, starts[g].astype(F32), first)
        rank = jnp.sum(member * (first + earlier), axis=-1, keepdims=True)
        rank_ref[...] = jnp.broadcast_to(rank, gate.shape)

        gid_r = gidt_ref[...]
        sub = lax.broadcasted_iota(jnp.int32, gid_r.shape, 0)
        member_r = jnp.where(sub.astype(F32) == gid_r, 1.0, 0.0)
        earlier_r = _dot(member_r.astype(BF16), utri_ref[...])
        first_r = jnp.zeros(gid_r.shape, F32)
        for g in range(1, N_GROUPS):
            first_r = jnp.where(sub == g, starts[g].astype(F32), first_r)
        rank_r = jnp.sum(member_r * (first_r + earlier_r), axis=0, keepdims=True)
        rows = lax.broadcasted_iota(jnp.int32, (Wp, W), 0).astype(F32)
        perm = jnp.where(rows == rank_r, 1.0, 0.0).astype(BF16)
        xs_ref[...] = _dot(perm, xb_ref[...]).astype(BF16)
        g_hi, g_lo = _split_bf16(gate)
        gs_ref[...] = _dot(perm, g_hi) + _dot(perm, g_lo)
        acc_ref[...] = jnp.zeros(acc_ref.shape, F32)

    def expert_rows(r0, rows):
        r0 = pl.multiple_of(r0, chunk)
        xc = xs_ref[pl.ds(r0, rows), :]
        a = _dot(xc, wg_ref[0])
        u = _dot(xc, wu_ref[0])
        gs = gs_ref[pl.ds(r0, rows), :]
        lane = lax.broadcasted_iota(jnp.int32, gs.shape, 1)
        gcol = jnp.sum(jnp.where(lane == e, gs, 0.0), axis=-1, keepdims=True)
        h = a * _sigmoid(a) * u * gcol
        acc_ref[pl.ds(r0, rows), :] += _dot(h.astype(BF16), wd_ref[0])

    def two_chunks(i, carry):
        expert_rows(start + i * (2 * chunk), 2 * chunk)
        return carry

    lax.fori_loop(0, n_chunks // 2, two_chunks, 0)

    @pl.when(n_chunks % 2 == 1)
    def _():
        expert_rows(start + (n_chunks - 1) * chunk, chunk)

    @pl.when(e == N_EXPERTS - 1)
    def _():
        cols = lax.broadcasted_iota(jnp.int32, (W, Wp), 1).astype(F32)
        unperm = jnp.where(cols == rank_ref[...][:, :1], 1.0, 0.0).astype(BF16)
        ffn = _dot(unperm, acc_ref[...].astype(BF16))
        y = _layer_norm(DN_ALPHA * x_ref[...] + ffn, g_ref[...], b_ref[...])
        y_ref[...] = y
        yb_ref[...] = y.astype(BF16)


def _moe_ln(xb, x, router_w, router_bias, wg, wu, wd, g, b):
    T, D = x.shape
    W = min(MOE_WINDOW, T)
    chunk = MOE_CHUNK
    Wp = W + N_GROUPS * chunk
    E, _, DE = wg.shape
    assert T % W == 0 and W % chunk == 0
    gate, gidt, cnt = _router_sorted(x, router_w, router_bias, W)
    counts = cnt[:, 0, :N_GROUPS].astype(jnp.int32).reshape(-1)
    t = np.arange(W)
    ltri = jnp.asarray((t[None, :] < t[:, None]).astype(np.float32), BF16)
    once = pl.Buffered(1)
    row = lambda shape: pl.BlockSpec(shape, lambda i, e, c: (i, 0))
    vec = pl.BlockSpec((1, D), lambda i, e, c: (0, 0))
    tri = pl.BlockSpec((W, W), lambda i, e, c: (0, 0), pipeline_mode=once)
    grid_spec = pltpu.PrefetchScalarGridSpec(
        num_scalar_prefetch=1,
        grid=(T // W, E),
        in_specs=[
            row((W, D)),
            pl.BlockSpec((W, D), lambda i, e, c: (i, 0), pipeline_mode=once),
            row((W, LANES)),
            pl.BlockSpec((8, W), lambda i, e, c: (0, i)),
            tri, tri,
            pl.BlockSpec((1, D, DE), lambda i, e, c: (e, 0, 0)),
            pl.BlockSpec((1, D, DE), lambda i, e, c: (e, 0, 0)),
            pl.BlockSpec((1, DE, D), lambda i, e, c: (e, 0, 0)),
            vec, vec,
        ],
        out_specs=[row((W, D)), row((W, D))],
        scratch_shapes=[
            pltpu.VMEM((Wp, D), BF16),
            pltpu.VMEM((Wp, LANES), F32),
            pltpu.VMEM((Wp, D), F32),
            pltpu.VMEM((W, LANES), F32),
        ],
    )
    return pl.pallas_call(
        functools.partial(_experts_sorted_body, W=W, Wp=Wp, chunk=chunk),
        grid_spec=grid_spec,
        out_shape=[jax.ShapeDtypeStruct((T, D), F32), jax.ShapeDtypeStruct((T, D), BF16)],
        compiler_params=pltpu.CompilerParams(dimension_semantics=("parallel", "arbitrary"),
                                             vmem_limit_bytes=MOE_VMEM_LIMIT),
    )(counts, xb, x, gate, gidt, ltri, ltri.T, wg, wu, wd, g.reshape(1, D), b.reshape(1, D))


def _block_onehots(S, block, tk):
    key = np.arange(S).reshape(S // tk, 1, tk)
    r = np.arange(LANES).reshape(1, LANES, 1)
    return jnp.asarray((key // block == r).astype(np.float32), BF16)


def _rope_tiled(S):
    cos, sin = _rope_tables(jnp.arange(S))
    reps = LANES // HALF
    return jnp.tile(cos, (1, reps)), jnp.tile(sin, (1, reps))


def _nsa_mixer(xb, B, S, w_in, cmp_k_w1, cmp_k_w2, cmp_v_w1, cmp_v_w2, cmp_k_pos, cmp_v_pos):
    G, HPG, KV = NSA_KV_GROUPS, NSA_HPG, NSA_KV_DIM
    L, STR, SB = NSA_CMP_LEN, NSA_CMP_STRIDE, NSA_SEL_BLOCK
    assert L == 2 * STR and S % SB == 0 and S // SB <= LANES
    T = B * S
    n_cmp = (S - L) // STR + 1
    NC = S // STR
    n_sel = S // SB
    top_n = min(NSA_SEL_TOPN, n_sel)

    cos2, sin2 = _rope_tiled(S)
    wb = w_in.astype(BF16)
    wcol = lambda i: wb[:, D_MODEL + i * KV: D_MODEL + (i + 1) * KV]
    tn = 2 * KV
    tk_s = min(512, S)
    tq_w = NSA_WINDOW // 2
    proj = _proj(xb, wb[:, :D_MODEL + 2 * KV], cos2, sin2, [2] * (D_MODEL // tn) + [0], S, tn=tn)
    ks3 = _proj(xb, wcol(2), cos2, sin2, [1], S, tn=KV).reshape(B, S, KV)
    vt_s = _proj_kt(xb, wcol(3), cos2, sin2, B, S, tk_s, rope=False)
    kt_w = _proj_kt(xb, wcol(4), cos2, sin2, B, S, tq_w)
    vaug = _proj_vaug(xb, wcol(5), tn=KV)
    wg = jnp.zeros((D_MODEL, LANES), BF16).at[:, :3 * N_HEADS].set(wb[:, D_MODEL + 6 * KV:])
    gates = _proj(xb, wg, cos2, sin2, [0], S, out_dtype=F32, tn=LANES).reshape(B, S, LANES)

    col = lambda i: proj[:, D_MODEL + i * KV: D_MODEL + (i + 1) * KV]
    proj3 = proj.reshape(B, S, proj.shape[1])
    vaug3 = vaug.reshape(B, S, vaug.shape[1])

    ccos, csin = _rope_tables(jnp.arange(NC) * STR + (L - 1))
    ccos = jnp.concatenate([ccos, ccos], axis=1)
    csin = jnp.concatenate([csin, csin], axis=1)
    to_rows = lambda t: t.reshape(B, S, G, HEAD_DIM).transpose(0, 2, 1, 3).reshape(B * G, NC, STR * HEAD_DIM)
    kc = _compress(to_rows(col(0)), cmp_k_w1, cmp_k_pos, cmp_k_w2, ccos, csin, True, n_cmp)
    vc = _compress(to_rows(col(1)), cmp_v_w1, cmp_v_pos, cmp_v_w2, ccos, csin, False, n_cmp)
    kct = kc.reshape(B, G, NC, HEAD_DIM).transpose(0, 1, 3, 2)

    ci = np.arange(NC)[:, None]
    sj = np.arange(LANES)[None, :]
    overlap = ((ci * STR < (sj + 1) * SB) & (ci * STR + L > sj * SB) & (ci < n_cmp) & (sj < n_sel))
    overlap = jnp.broadcast_to(jnp.asarray(overlap.astype(np.float32), BF16), (B, G, NC, LANES))
    vo = jnp.concatenate([vc.reshape(B, G, NC, HEAD_DIM), jnp.ones((B, G, NC, 1), BF16),
                          jnp.zeros((B, G, NC, LANES - HEAD_DIM - 1), BF16), overlap], axis=-1)

    o_cmp, selbt = _nsa_cmp(proj3, kct, vo, gates, n_cmp, n_sel, top_n)
    o_slc = _nsa_slc_t(proj3, ks3, vt_s, _block_onehots(S, SB, tk_s).transpose(0, 2, 1), selbt, gates,
                       tq=tk_s, tk=tk_s)
    o_win = _nsa_win(proj3, kt_w, vaug3, 0, gates, tq=tq_w)
    return [o.reshape(T, D_MODEL) for o in (o_cmp, o_slc, o_win)]


def _moba_mixer(xb, B, S, w_in):
    H = N_HEADS
    nb = S // MOBA_BLOCK
    top_k = min(MOBA_TOPK, nb)
    cos2, sin2 = _rope_tiled(S)
    tn = 512
    n_t = D_MODEL // tn
    wb = w_in.astype(BF16)
    tk = min(2 * MOBA_BLOCK, S)
    qt = _proj_kt(xb, wb[:, :D_MODEL], cos2, sin2, B, S, tk, tn=tn, scale=Q_SCALE_LOG2)
    k3 = _proj(xb, wb[:, D_MODEL:2 * D_MODEL], cos2, sin2, [1] * n_t, S, tn=tn).reshape(B, S, D_MODEL)
    vt = _proj_kt(xb, wb[:, 2 * D_MODEL:], cos2, sin2, B, S, tk, tn=tn, rope=False)
    e = _block_onehots(S, MOBA_BLOCK, tk)
    kh, kl = _moba_kmean_pairs(k3, e, tk)
    o = _moba_attn_t(qt, k3, vt, e.transpose(0, 2, 1), kh, kl, nb, top_k, tq=tk, tk=tk)
    return [o.reshape(B * S, D_MODEL)]


def kernel(x, nsa_w_in, nsa_w_out, nsa_cmp_k_w1, nsa_cmp_k_w2, nsa_cmp_v_w1, nsa_cmp_v_w2, nsa_cmp_k_pos, nsa_cmp_v_pos, moba_w_in, moba_w_out, router_w, router_bias, moe_w_gate, moe_w_up, moe_w_down, ln_g, ln_b):
    B, S, D = x.shape
    xf = x.reshape(B * S, D)
    xb = xf.astype(BF16)
    for layer in range(DEPTH):
        j = layer // 2
        if layer % 2 == 0:
            os_ = _nsa_mixer(xb, B, S, nsa_w_in[j], nsa_cmp_k_w1[j], nsa_cmp_k_w2[j], nsa_cmp_v_w1[j],
                             nsa_cmp_v_w2[j], nsa_cmp_k_pos[j], nsa_cmp_v_pos[j])
            w_out = nsa_w_out[j]
        else:
            os_ = _moba_mixer(xb, B, S, moba_w_in[j])
            w_out = moba_w_out[j]
        xf, xb = _outproj_ln(os_, w_out.astype(BF16), xf, ln_g[layer, 0], ln_b[layer, 0])
        xf, xb = _moe_ln(xb, xf, router_w, router_bias, moe_w_gate[layer].astype(BF16),
                         moe_w_up[layer].astype(BF16), moe_w_down[layer].astype(BF16),
                         ln_g[layer, 1], ln_b[layer, 1])
    return xf.reshape(B, S, D)
```

```python
import functools

import jax
import jax.numpy as jnp
import numpy as np
from jax import lax
from jax.experimental import pallas as pl
from jax.experimental.pallas import tpu as pltpu

F32 = jnp.float32
BF16 = jnp.bfloat16

D_MODEL = 1024
N_HEADS = 16
HEAD_DIM = 64
HALF = HEAD_DIM // 2
ROPE_THETA = 10000.0
DEPTH = 2
DN_ALPHA = (2 * DEPTH) ** 0.25
LN_EPS = 1e-5
NEG = -1e30
N_FORCED = 3
MASK = -1e30
REMOVED = -3.0e38
LANES = 128
Q_SCALE_LOG2 = float(HEAD_DIM ** -0.5 * np.log2(np.e))
ROW_CHUNK = 128
PROJ_ROW_CHUNK = 256

NSA_KV_GROUPS = 4
NSA_HPG = N_HEADS // NSA_KV_GROUPS
NSA_KV_DIM = NSA_KV_GROUPS * HEAD_DIM
NSA_CMP_LEN = 32
NSA_CMP_STRIDE = 16
NSA_SEL_BLOCK = 64
NSA_SEL_TOPN = 16
NSA_WINDOW = 512

MOBA_BLOCK = 256
MOBA_TOPK = 3

N_EXPERTS = 16
N_GROUPS = 4
EXPERTS_PER_GROUP = N_EXPERTS // N_GROUPS
D_EXPERT = 512

VMEM_LIMIT = 48 * 1024 * 1024


def _cparams(sem):
    return pltpu.CompilerParams(dimension_semantics=sem, vmem_limit_bytes=VMEM_LIMIT)


def _dot(a, b):
    return jnp.dot(a, b, preferred_element_type=F32)


def _split_bf16(x):
    hi = x.astype(BF16)
    lo = (x - hi.astype(F32)).astype(BF16)
    return hi, lo


def _sigmoid(x):
    return 1.0 / (1.0 + jnp.exp(-x))


def _proj_body(mode_ref, x_ref, w_ref, cos_ref, sin_ref, o_ref, *, tn):
    mode = mode_ref[pl.program_id(0)]
    tm = x_ref.shape[0]
    rc = min(PROJ_ROW_CHUNK, tm)

    @pl.when(mode == 0)
    def _():
        for r in range(0, tm, rc):
            o_ref[r:r + rc, :] = _dot(x_ref[r:r + rc, :], w_ref[...]).astype(o_ref.dtype)

    @pl.when(mode != 0)
    def _():
        sc = jnp.where(mode == 2, Q_SCALE_LOG2, 1.0).astype(F32)
        for r in range(0, tm, rc):
            acc = _dot(x_ref[r:r + rc, :], w_ref[...])
            cos = cos_ref[r:r + rc, :] * sc
            sin = sin_ref[r:r + rc, :] * sc
            for c in range(tn // LANES):
                o_ref[r:r + rc, c * LANES:(c + 1) * LANES] = (
                    _rope_chunk(acc[:, c * LANES:(c + 1) * LANES], cos, sin).astype(o_ref.dtype))


def _proj(xb, w, cos2, sin2, modes, seq, out_dtype=BF16, tm=1024, tn=512):
    T, K = xb.shape
    N = w.shape[1]
    tm = min(tm, seq)
    assert T % tm == 0 and N % tn == 0 and seq % tm == 0 and len(modes) == N // tn
    n_pos = seq // tm
    grid_spec = pltpu.PrefetchScalarGridSpec(
        num_scalar_prefetch=1,
        grid=(N // tn, T // tm),
        in_specs=[
            pl.BlockSpec((tm, K), lambda j, i, m: (i, 0)),
            pl.BlockSpec((K, tn), lambda j, i, m: (0, j)),
            pl.BlockSpec((tm, LANES), lambda j, i, m: (i % n_pos, 0)),
            pl.BlockSpec((tm, LANES), lambda j, i, m: (i % n_pos, 0)),
        ],
        out_specs=pl.BlockSpec((tm, tn), lambda j, i, m: (i, j)),
    )
    return pl.pallas_call(
        functools.partial(_proj_body, tn=tn),
        grid_spec=grid_spec,
        out_shape=jax.ShapeDtypeStruct((T, N), out_dtype),
        compiler_params=_cparams(("parallel", "parallel")),
    )(jnp.asarray(modes, jnp.int32), xb, w, cos2, sin2)


def _rope_tables(pos):
    inv = 1.0 / (ROPE_THETA ** (jnp.arange(0, HEAD_DIM, 2, dtype=F32) / HEAD_DIM))
    ang = pos.astype(F32)[:, None] * inv[None, :]
    return jnp.cos(ang), jnp.sin(ang)


def _rope_chunk(a, cos, sin):
    lane = lax.broadcasted_iota(jnp.int32, a.shape, 1)
    up = pltpu.roll(a, LANES - HALF, 1)
    dn = pltpu.roll(a, HALF, 1)
    return a * cos + jnp.where((lane % HEAD_DIM) < HALF, -up, dn) * sin


def _proj_kt_body(x_ref, w_ref, cos_ref, sin_ref, o_ref, *, tn, tk, rope, scale):
    tm = x_ref.shape[0]
    rc = min(PROJ_ROW_CHUNK, tk)
    for r in range(0, tm, rc):
        acc = _dot(x_ref[r:r + rc, :], w_ref[...])
        cos = cos_ref[r:r + rc, :] * scale
        sin = sin_ref[r:r + rc, :] * scale
        for c in range(tn // LANES):
            a = acc[:, c * LANES:(c + 1) * LANES]
            kt = (_rope_chunk(a, cos, sin) if rope else a).T
            for hh in range(2):
                o_ref[0, 2 * c + hh, r // tk, :, r % tk:r % tk + rc] = (
                    kt[hh * HEAD_DIM:(hh + 1) * HEAD_DIM, :].astype(o_ref.dtype))


def _proj_kt(xb, w, cos2, sin2, B, S, tk, tm=1024, tn=256, rope=True, scale=1.0):
    T, K = xb.shape
    N = w.shape[1]
    tm = min(tm, S)
    tn = min(tn, N)
    assert S % tm == 0 and tm % tk == 0 and N % tn == 0
    n_pos = S // tm
    return pl.pallas_call(
        functools.partial(_proj_kt_body, tn=tn, tk=tk, rope=rope, scale=scale),
        grid=(N // tn, T // tm),
        in_specs=[
            pl.BlockSpec((tm, K), lambda j, i: (i, 0)),
            pl.BlockSpec((K, tn), lambda j, i: (0, j)),
            pl.BlockSpec((tm, LANES), lambda j, i: (i % n_pos, 0)),
            pl.BlockSpec((tm, LANES), lambda j, i: (i % n_pos, 0)),
        ],
        out_specs=pl.BlockSpec((1, tn // HEAD_DIM, tm // tk, HEAD_DIM, tk),
                               lambda j, i: (i // n_pos, j, i % n_pos, 0, 0)),
        out_shape=jax.ShapeDtypeStruct((B, N // HEAD_DIM, S // tk, HEAD_DIM, tk), BF16),
        compiler_params=_cparams(("parallel", "parallel")),
    )(xb, w, cos2, sin2)


def _proj_vaug_body(x_ref, w_ref, o_ref, *, tn):
    tm = x_ref.shape[0]
    rc = min(PROJ_ROW_CHUNK, tm)
    lane = lax.broadcasted_iota(jnp.int32, (rc, LANES), 1)
    tail = jnp.where(lane == HEAD_DIM, 1.0, 0.0)
    for r in range(0, tm, rc):
        acc = _dot(x_ref[r:r + rc, :], w_ref[...])
        for c in range(tn // LANES):
            a = acc[:, c * LANES:(c + 1) * LANES]
            for k, head in enumerate((a, pltpu.roll(a, HEAD_DIM, 1))):
                o_ref[r:r + rc, (2 * c + k) * LANES:(2 * c + k + 1) * LANES] = (
                    jnp.where(lane < HEAD_DIM, head, tail).astype(o_ref.dtype))


def _proj_vaug(xb, w, tm=1024, tn=512):
    T, K = xb.shape
    N = w.shape[1]
    tm = min(tm, T)
    assert T % tm == 0 and N % tn == 0
    return pl.pallas_call(
        functools.partial(_proj_vaug_body, tn=tn),
        grid=(N // tn, T // tm),
        in_specs=[pl.BlockSpec((tm, K), lambda j, i: (i, 0)), pl.BlockSpec((K, tn), lambda j, i: (0, j))],
        out_specs=pl.BlockSpec((tm, 2 * tn), lambda j, i: (i, j)),
        out_shape=jax.ShapeDtypeStruct((T, 2 * N), BF16),
        compiler_params=_cparams(("parallel", "parallel")),
    )(xb, w)


def _gelu_tanh(x):
    c = np.float32(np.sqrt(2.0 / np.pi))
    return 0.5 * x * (1.0 + jnp.tanh(c * (x + 0.044715 * (x * x * x))))


def _compress_body(r_ref, w1_ref, pos_ref, w2_ref, w2r_ref, cos_ref, sin_ref, o_ref, *, rope, n_cmp):
    r = r_ref[0]
    nc = r.shape[0]
    half = NSA_CMP_STRIDE * HEAD_DIM
    a = _dot(r, w1_ref[0])
    b = _dot(r, w1_ref[1])
    pos = pos_ref[...]
    pb = _dot(pos[:, :half], w1_ref[0]) + _dot(pos[:, half:], w1_ref[1])
    b_next = pltpu.roll(b, nc - 1, 0)
    h = _gelu_tanh(a + b_next + pb[0:1, :]).astype(BF16)
    o = _dot(h, w2_ref[...])
    if rope:
        o = o * cos_ref[...] + _dot(h, w2r_ref[...]) * sin_ref[...]
    row = lax.broadcasted_iota(jnp.int32, o.shape, 0)
    o_ref[0] = jnp.where(row < n_cmp, o, 0.0).astype(o_ref.dtype)


def _compress(r, w1, pos, w2, cos_c, sin_c, rope, n_cmp):
    BG, NC, K = r.shape
    hidden = w1.shape[1]
    w1s = w1.astype(BF16).reshape(2, K, hidden)
    pos8 = jnp.zeros((8, 2 * K), BF16).at[0].set(pos.reshape(-1).astype(BF16))
    w2r = jnp.concatenate([-w2[:, HALF:], w2[:, :HALF]], axis=1).astype(BF16)
    full = lambda shape: pl.BlockSpec(shape, lambda i: (0,) * len(shape))
    return pl.pallas_call(
        functools.partial(_compress_body, rope=rope, n_cmp=n_cmp),
        grid=(BG,),
        in_specs=[
            pl.BlockSpec((1, NC, K), lambda i: (i, 0, 0)),
            full((2, K, hidden)),
            full((8, 2 * K)),
            full((hidden, HEAD_DIM)),
            full((hidden, HEAD_DIM)),
            full((NC, HEAD_DIM)),
            full((NC, HEAD_DIM)),
        ],
        out_specs=pl.BlockSpec((1, NC, HEAD_DIM), lambda i: (i, 0, 0)),
        out_shape=jax.ShapeDtypeStruct((BG, NC, HEAD_DIM), BF16),
        compiler_params=_cparams(("parallel",)),
    )(r, w1s, pos8, w2.astype(BF16), w2r, cos_c, sin_c)


def _topk_mask_t(v, k):
    idx = lax.broadcasted_iota(jnp.int32, v.shape, 0).astype(F32)

    def step(_, cur):
        m = jnp.max(cur, axis=0, keepdims=True)
        first = jnp.min(jnp.where(cur == m, idx, float(LANES)), axis=0, keepdims=True)
        return jnp.where(idx == first, REMOVED, cur)

    return lax.fori_loop(0, k, step, v, unroll=True) != v


def _head_kt(t, parity):
    z = jnp.zeros_like(t)
    return jnp.concatenate([t, z] if parity == 0 else [z, t], axis=0)


def _pair_merge(even, odd):
    lane = lax.broadcasted_iota(jnp.int32, even.shape, 1)
    return jnp.where(lane < HEAD_DIM, even, pltpu.roll(odd, HEAD_DIM, 1))


def _gate_col(logits, branch, h, heads_per_step=NSA_HPG):
    idx = branch * N_HEADS + pl.program_id(1) * heads_per_step + h
    lane = lax.broadcasted_iota(jnp.int32, logits.shape, 1)
    return _sigmoid(jnp.sum(jnp.where(lane == idx, logits, 0.0), axis=-1, keepdims=True))


def _tile_iotas(rows, tk):
    return (lax.broadcasted_iota(jnp.int32, (rows, tk), 0),
            lax.broadcasted_iota(jnp.int32, (rows, tk), 1))


def _nsa_cmp_body(q_ref, kct_ref, vo_ref, g_ref, o_ref, selbt_ref, s_ref, m_ref, imp_ref,
                  *, tq, n_cmp, n_sel, top_n):
    s0 = pl.program_id(2) * tq
    q4 = q_ref[0]
    n_chunks = vo_ref.shape[2] // LANES
    rc = min(ROW_CHUNK, tq)

    def attend(nv):
        nc = nv * LANES
        kct = kct_ref[0, 0, :, :nc]
        vo = vo_ref[0, 0, :nc, :]
        tpos = s0 + lax.broadcasted_iota(jnp.int32, (tq, nc), 0)
        nidx = lax.broadcasted_iota(jnp.int32, (tq, nc), 1)
        cmask = (nidx * NSA_CMP_STRIDE + (NSA_CMP_LEN - 1) <= tpos) & (nidx < n_cmp)
        for h in range(NSA_HPG):
            pair = q4[:, (h // 2) * LANES:(h // 2 + 1) * LANES]
            s = jnp.where(cmask, _dot(pair, _head_kt(kct, h % 2)), NEG)
            s_ref[h * tq:(h + 1) * tq, :nc] = s
            cm = s[:, :LANES]
            for c in range(1, nv):
                cm = jnp.maximum(cm, s[:, c * LANES:(c + 1) * LANES])
            m_ref[h * tq:(h + 1) * tq, :] = jnp.broadcast_to(jnp.max(cm, axis=-1, keepdims=True), (tq, LANES))

        for r0 in range(0, tq, rc):
            imp = jnp.zeros((rc, LANES), F32)
            heads = []
            for h in range(NSA_HPG):
                r = h * tq + r0
                m = m_ref[r:r + rc, :]
                e = jnp.concatenate(
                    [jnp.exp2(s_ref[r:r + rc, c * LANES:(c + 1) * LANES] - m) for c in range(nv)],
                    axis=1).astype(BF16)
                res = _dot(e, vo)
                inv = jnp.where(m[:, :1] > 0.5 * NEG, 1.0 / res[:, HEAD_DIM:HEAD_DIM + 1], 0.0)
                gate = _gate_col(g_ref[0, r0:r0 + rc, :], 0, h)
                heads.append(res[:, :LANES] * (inv * gate))
                imp = imp + res[:, LANES:] * inv
            o_ref[0, r0:r0 + rc, :] = jnp.concatenate(
                [_pair_merge(heads[h], heads[h + 1]) for h in range(0, NSA_HPG, 2)], axis=1).astype(o_ref.dtype)
            imp_ref[r0:r0 + rc, :] = imp

    attend(n_chunks)
    imp = imp_ref[...]

    imp_t = imp.T
    blk = lax.broadcasted_iota(jnp.int32, imp_t.shape, 0)
    jq = (s0 + lax.broadcasted_iota(jnp.int32, imp_t.shape, 1)) // NSA_SEL_BLOCK
    forced = (blk == 0) | (blk == jq) | (blk == jq - 1)
    cand = jnp.where(forced | (blk >= n_sel), REMOVED, jnp.where(blk > jq, NEG, imp_t))
    sel_t = _topk_mask_t(cand, top_n - N_FORCED) | forced
    selbt_ref[0, 0] = jnp.where(sel_t, 0.0, MASK).astype(selbt_ref.dtype)


def _nsa_cmp(proj3, kct, vo, g_cmp, n_cmp, n_sel, top_n, tq=256):
    B, S, _ = proj3.shape
    G, HPG = NSA_KV_GROUPS, NSA_HPG
    NC = vo.shape[2]
    tq = min(tq, S)
    assert NC % LANES == 0
    return pl.pallas_call(
        functools.partial(_nsa_cmp_body, tq=tq, n_cmp=n_cmp, n_sel=n_sel, top_n=top_n),
        grid=(B, G, S // tq),
        in_specs=[
            pl.BlockSpec((1, tq, HPG * HEAD_DIM), lambda b, g, i: (b, i, g)),
            pl.BlockSpec((1, 1, HEAD_DIM, NC), lambda b, g, i: (b, g, 0, 0)),
            pl.BlockSpec((1, 1, NC, 2 * LANES), lambda b, g, i: (b, g, 0, 0)),
            pl.BlockSpec((1, tq, LANES), lambda b, g, i: (b, i, 0)),
        ],
        out_specs=[
            pl.BlockSpec((1, tq, HPG * HEAD_DIM), lambda b, g, i: (b, i, g)),
            pl.BlockSpec((1, 1, LANES, tq), lambda b, g, i: (b, g, 0, i)),
        ],
        out_shape=[
            jax.ShapeDtypeStruct((B, S, D_MODEL), BF16),
            jax.ShapeDtypeStruct((B, G, LANES, S), BF16),
        ],
        scratch_shapes=[pltpu.VMEM((HPG * tq, NC), F32), pltpu.VMEM((HPG * tq, LANES), F32),
                        pltpu.VMEM((tq, LANES), F32)],
        compiler_params=_cparams(("parallel", "parallel", "parallel")),
    )(proj3, kct, vo, g_cmp)


def _nsa_win_body(q_ref, kt_ref, v_ref, g_ref, o_ref, s_ref, m_ref, *, tq, n_tiles):
    s0 = pl.program_id(2) * tq
    tk = n_tiles * tq
    k_first = jnp.maximum(pl.program_id(2) - (n_tiles - 1), 0)
    k0 = pl.multiple_of(k_first * tq, tq)
    kt = jnp.concatenate([kt_ref[0, 0, k_first + j] for j in range(n_tiles)], axis=1)
    v = v_ref[0, pl.ds(k0, tk), :]
    row, col = _tile_iotas(tq, tk)
    dist = (s0 - k0) + row - col
    valid = (dist >= 0) & (dist < NSA_WINDOW)
    q4 = q_ref[0]
    for h in range(NSA_HPG):
        pair = q4[:, (h // 2) * LANES:(h // 2 + 1) * LANES]
        s = jnp.where(valid, _dot(pair, _head_kt(kt, h % 2)), MASK)
        s_ref[h * tq:(h + 1) * tq, :] = s
        cm = s[:, :LANES]
        for c in range(1, tk // LANES):
            cm = jnp.maximum(cm, s[:, c * LANES:(c + 1) * LANES])
        m_ref[h * tq:(h + 1) * tq, :] = jnp.broadcast_to(jnp.max(cm, axis=-1, keepdims=True), (tq, LANES))

    rc = min(ROW_CHUNK, tq)
    for r0 in range(0, tq, rc):
        heads = []
        for h in range(NSA_HPG):
            r = h * tq + r0
            m = m_ref[r:r + rc, :]
            p = jnp.concatenate(
                [jnp.exp2(s_ref[r:r + rc, c * LANES:(c + 1) * LANES] - m) for c in range(tk // LANES)],
                axis=1).astype(BF16)
            res = _dot(p, v)
            heads.append(res * (_gate_col(g_ref[0, r0:r0 + rc, :], 2, h) / res[:, HEAD_DIM:HEAD_DIM + 1]))
        o_ref[0, r0:r0 + rc, :] = jnp.concatenate(
            [_pair_merge(heads[h], heads[h + 1]) for h in range(0, NSA_HPG, 2)], axis=1).astype(o_ref.dtype)


def _nsa_win(proj3, kt, vaug3, v_block, g, tq):
    B, S, _ = proj3.shape
    G, HPG = NSA_KV_GROUPS, NSA_HPG
    nk = S // tq
    n_tiles = NSA_WINDOW // tq + 1
    assert NSA_WINDOW % tq == 0 and nk >= n_tiles and kt.shape == (B, G, nk, HEAD_DIM, tq)
    once = pl.Buffered(1)
    return pl.pallas_call(
        functools.partial(_nsa_win_body, tq=tq, n_tiles=n_tiles),
        grid=(B, G, S // tq),
        in_specs=[
            pl.BlockSpec((1, tq, HPG * HEAD_DIM), lambda b, g_, i: (b, i, g_)),
            pl.BlockSpec((1, 1, nk, HEAD_DIM, tq), lambda b, g_, i: (b, g_, 0, 0, 0), pipeline_mode=once),
            pl.BlockSpec((1, S, LANES), lambda b, g_, i: (b, 0, v_block + g_), pipeline_mode=once),
            pl.BlockSpec((1, tq, LANES), lambda b, g_, i: (b, i, 0)),
        ],
        out_specs=pl.BlockSpec((1, tq, HPG * HEAD_DIM), lambda b, g_, i: (b, i, g_)),
        out_shape=jax.ShapeDtypeStruct((B, S, D_MODEL), BF16),
        scratch_shapes=[pltpu.VMEM((HPG * tq, n_tiles * tq), F32), pltpu.VMEM((HPG * tq, LANES), F32)],
        compiler_params=_cparams(("parallel", "parallel", "parallel")),
    )(proj3, kt, vaug3, g)


VALUE_SUB = 2 * LANES


def _flash_t_scratch(n_heads, tq, tk):
    return [pltpu.VMEM((n_heads * 2 * LANES, tq), BF16),
            pltpu.VMEM((n_heads * tk, tq), F32),
            pltpu.VMEM((n_heads * 8, tq), F32),
            pltpu.VMEM((n_heads * 8, tq), F32),
            pltpu.VMEM((n_heads * LANES, tq), F32)]


def _flash_t_init(m_ref, acc_ref):
    m_ref[...] = jnp.full(m_ref.shape, -jnp.inf, F32)
    acc_ref[...] = jnp.zeros(acc_ref.shape, F32)


def _flash_t_tile(n_heads, keys_of, vals_of, mask, qa_ref, s_ref, m_ref, alpha_ref, acc_ref):
    aug = 2 * LANES
    tk, tq = s_ref.shape[0] // n_heads, s_ref.shape[1]
    for h in range(n_heads):
        st = _dot(keys_of(h), qa_ref[h * aug:(h + 1) * aug, :])
        if mask is not None:
            st = jnp.where(mask, st, MASK)
        s_ref[h * tk:(h + 1) * tk, :] = st
        m_prev = m_ref[h * 8:(h + 1) * 8, :]
        m_new = jnp.maximum(m_prev, jnp.max(st, axis=0, keepdims=True))
        alpha_ref[h * 8:(h + 1) * 8, :] = jnp.exp2(m_prev - m_new)
        m_ref[h * 8:(h + 1) * 8, :] = m_new
    for h in range(n_heads):
        vals = vals_of(h)
        for q0 in range(0, tq, VALUE_SUB):
            m = m_ref[h * 8:h * 8 + 1, q0:q0 + VALUE_SUB]
            part = jnp.zeros((LANES, VALUE_SUB), F32)
            for c0 in range(0, tk, VALUE_SUB):
                pt = jnp.exp2(s_ref[h * tk + c0:h * tk + c0 + VALUE_SUB, q0:q0 + VALUE_SUB] - m).astype(BF16)
                part = part + _dot(vals[:, c0:c0 + VALUE_SUB], pt)
            acc_ref[h * LANES:(h + 1) * LANES, q0:q0 + VALUE_SUB] = (
                alpha_ref[h * 8:h * 8 + 1, q0:q0 + VALUE_SUB]
                * acc_ref[h * LANES:(h + 1) * LANES, q0:q0 + VALUE_SUB] + part)


def _flash_t_loop(tile, s0, tq, tk):
    last = (s0 + tq - 1) // tk

    def full_tile(ki, carry):
        tile(ki, None)
        return carry

    lax.fori_loop(0, last, full_tile, 0)
    krow, qcol = _tile_iotas(tk, tq)
    tile(last, last * tk + krow <= s0 + qcol)


def _flash_t_out(acc_ref, n_heads, gates):
    def head(h):
        acc = acc_ref[h * LANES:(h + 1) * LANES, :]
        return (acc / acc[HEAD_DIM:HEAD_DIM + 1, :]).T * gates[h]
    return jnp.concatenate([_pair_merge(head(h), head(h + 1)) for h in range(0, n_heads, 2)], axis=1)


def _ones_row_block(tk):
    return jnp.where(lax.broadcasted_iota(jnp.int32, (HEAD_DIM, tk), 0) == 0, 1.0, 0.0).astype(BF16)


def _nsa_slc_t_body(q_ref, k_ref, vt_ref, et_ref, selbt_ref, g_ref, o_ref, qa_ref, s_ref, m_ref, alpha_ref,
                    acc_ref, *, tq, tk):
    s0 = pl.program_id(2) * tq
    n_heads = 2 * NSA_HPG
    _flash_t_init(m_ref, acc_ref)
    q8 = q_ref[0]
    for pair in range(n_heads // 2):
        grp = pair // (NSA_HPG // 2)
        pair_t = q8[:, pair * LANES:(pair + 1) * LANES].astype(F32).T
        for hp in range(2):
            qt = pair_t[hp * HEAD_DIM:(hp + 1) * HEAD_DIM, :].astype(BF16)
            h = 2 * pair + hp
            qa_ref[h * 2 * LANES:(h + 1) * 2 * LANES, :] = jnp.concatenate(
                [selbt_ref[0, grp], _head_kt(qt, grp)], axis=0)
    ones_rows = _ones_row_block(tk)

    def tile(ki, mask):
        k0 = pl.multiple_of(ki * tk, tk)
        keys = jnp.concatenate([et_ref[ki], k_ref[0, pl.ds(k0, tk), :]], axis=1)
        vals_of = lambda h: jnp.concatenate([vt_ref[0, h // NSA_HPG, ki], ones_rows], axis=0)
        _flash_t_tile(n_heads, lambda h: keys, vals_of, mask, qa_ref, s_ref, m_ref, alpha_ref, acc_ref)

    _flash_t_loop(tile, s0, tq, tk)
    gates = [_gate_col(g_ref[0], 1, h, n_heads) for h in range(n_heads)]
    o_ref[0] = _flash_t_out(acc_ref, n_heads, gates).astype(o_ref.dtype)


def _nsa_slc_t(proj3, k3, vt, et, selbt, g, tq, tk):
    B, S, _ = proj3.shape
    G, HPG = NSA_KV_GROUPS, NSA_HPG
    nk = S // tk
    assert tk == tq and G % 2 == 0 and vt.shape == (B, G, nk, HEAD_DIM, tk) and et.shape == (nk, tk, LANES)
    once = pl.Buffered(1)
    width = 2 * HPG * HEAD_DIM
    return pl.pallas_call(
        functools.partial(_nsa_slc_t_body, tq=tq, tk=tk),
        grid=(B, G // 2, S // tq),
        in_specs=[
            pl.BlockSpec((1, tq, width), lambda b, p, i: (b, i, p)),
            pl.BlockSpec((1, S, LANES), lambda b, p, i: (b, 0, p), pipeline_mode=once),
            pl.BlockSpec((1, 2, nk, HEAD_DIM, tk), lambda b, p, i: (b, p, 0, 0, 0), pipeline_mode=once),
            pl.BlockSpec((nk, tk, LANES), lambda b, p, i: (0, 0, 0), pipeline_mode=once),
            pl.BlockSpec((1, 2, LANES, tq), lambda b, p, i: (b, p, 0, i)),
            pl.BlockSpec((1, tq, LANES), lambda b, p, i: (b, i, 0)),
        ],
        out_specs=pl.BlockSpec((1, tq, width), lambda b, p, i: (b, i, p)),
        out_shape=jax.ShapeDtypeStruct((B, S, D_MODEL), BF16),
        scratch_shapes=_flash_t_scratch(2 * HPG, tq, tk),
        compiler_params=_cparams(("parallel", "parallel", "parallel")),
    )(proj3, k3, vt, et, selbt, g)


def _moba_kmean_pairs_body(k_ref, e_ref, hi_ref, lo_ref, *, tk):
    nk = e_ref.shape[0]
    km = jnp.zeros((LANES, LANES), F32)
    for ki in range(nk):
        km = km + _dot(e_ref[ki], k_ref[0, ki * tk:(ki + 1) * tk, :])
    hi, lo = _split_bf16(km * (1.0 / MOBA_BLOCK))
    hi_ref[0, 0] = hi
    lo_ref[0, 0] = lo


def _moba_kmean_pairs(k3, e, tk):
    B, S, N = k3.shape
    n_pairs = N // LANES
    nk = S // tk
    spec = pl.BlockSpec((1, 1, LANES, LANES), lambda b, p: (b, p, 0, 0))
    return pl.pallas_call(
        functools.partial(_moba_kmean_pairs_body, tk=tk),
        grid=(B, n_pairs),
        in_specs=[
            pl.BlockSpec((1, S, LANES), lambda b, p: (b, 0, p)),
            pl.BlockSpec((nk, LANES, tk), lambda b, p: (0, 0, 0)),
        ],
        out_specs=[spec, spec],
        out_shape=[jax.ShapeDtypeStruct((B, n_pairs, LANES, LANES), BF16)] * 2,
        compiler_params=_cparams(("parallel", "parallel")),
    )(k3, e)


def _moba_t_body(qt_ref, k_ref, vt_ref, et_ref, kh_ref, kl_ref, o_ref, qa_ref, s_ref, m_ref, alpha_ref,
                 acc_ref, *, tq, tk, hb, nb, top_k):
    s0 = pl.program_id(2) * tq
    _flash_t_init(m_ref, acc_ref)
    aug = 2 * LANES

    blk = lax.broadcasted_iota(jnp.int32, (LANES, tq), 0)
    cb = (s0 + lax.broadcasted_iota(jnp.int32, (LANES, tq), 1)) // MOBA_BLOCK
    for h in range(hb):
        qpad = _head_kt(qt_ref[0, h, 0], h % 2)
        gsc = _dot(kh_ref[0, h // 2], qpad) + _dot(kl_ref[0, h // 2], qpad)
        gsc = jnp.where(blk < cb, gsc, NEG)
        gsc = jnp.where(blk < nb, gsc, REMOVED)
        sel = (_topk_mask_t(gsc, top_k) & (blk < cb)) | (blk == cb)
        qa_ref[h * aug:(h + 1) * aug, :] = jnp.concatenate(
            [jnp.where(sel, 0.0, MASK).astype(BF16), qpad], axis=0)

    ones_rows = _ones_row_block(tk)

    def tile(ki, mask):
        k0 = pl.multiple_of(ki * tk, tk)
        et = et_ref[ki]
        keys_of = lambda h: jnp.concatenate(
            [et, k_ref[0, pl.ds(k0, tk), (h // 2) * LANES:(h // 2 + 1) * LANES]], axis=1)
        vals_of = lambda h: jnp.concatenate([vt_ref[0, h, ki], ones_rows], axis=0)
        _flash_t_tile(hb, keys_of, vals_of, mask, qa_ref, s_ref, m_ref, alpha_ref, acc_ref)

    _flash_t_loop(tile, s0, tq, tk)
    o_ref[0] = _flash_t_out(acc_ref, hb, [1.0] * hb).astype(o_ref.dtype)


def _moba_attn_t(qt, k3, vt, et, kh, kl, nb, top_k, tq, tk, hb=8):
    B, H, nq, _, _ = qt.shape
    S = k3.shape[1]
    nk = S // tk
    assert tk == tq and tk % MOBA_BLOCK == 0 and H % hb == 0 and hb % 2 == 0 and nb <= LANES
    once = pl.Buffered(1)
    return pl.pallas_call(
        functools.partial(_moba_t_body, tq=tq, tk=tk, hb=hb, nb=nb, top_k=top_k),
        grid=(B, H // hb, nq),
        in_specs=[
            pl.BlockSpec((1, hb, 1, HEAD_DIM, tq), lambda b, h, i: (b, h, i, 0, 0)),
            pl.BlockSpec((1, S, hb * HEAD_DIM), lambda b, h, i: (b, 0, h), pipeline_mode=once),
            pl.BlockSpec((1, hb, nk, HEAD_DIM, tk), lambda b, h, i: (b, h, 0, 0, 0), pipeline_mode=once),
            pl.BlockSpec((nk, tk, LANES), lambda b, h, i: (0, 0, 0), pipeline_mode=once),
            pl.BlockSpec((1, hb // 2, LANES, LANES), lambda b, h, i: (b, h, 0, 0), pipeline_mode=once),
            pl.BlockSpec((1, hb // 2, LANES, LANES), lambda b, h, i: (b, h, 0, 0), pipeline_mode=once),
        ],
        out_specs=pl.BlockSpec((1, tq, hb * HEAD_DIM), lambda b, h, i: (b, i, h)),
        out_shape=jax.ShapeDtypeStruct((B, S, D_MODEL), BF16),
        scratch_shapes=_flash_t_scratch(hb, tq, tk),
        compiler_params=_cparams(("parallel", "parallel", "parallel")),
    )(qt, k3, vt, et, kh, kl)


def _layer_norm(r, g, b):
    mu = jnp.mean(r, axis=-1, keepdims=True)
    c = r - mu
    var = jnp.mean(c * c, axis=-1, keepdims=True)
    return c * lax.rsqrt(var + LN_EPS) * g + b


def _outproj_body(*refs, n_o):
    o_refs = refs[:n_o]
    w_ref, x_ref, g_ref, b_ref, y_ref, yb_ref = refs[n_o:]
    tm = x_ref.shape[0]
    rc = min(PROJ_ROW_CHUNK, tm)
    for r0 in range(0, tm, rc):
        rows = slice(r0, r0 + rc)
        o = o_refs[0][rows, :].astype(F32)
        for ref in o_refs[1:]:
            o = o + ref[rows, :].astype(F32)
        mix = _dot(o.astype(BF16), w_ref[...])
        y = _layer_norm(DN_ALPHA * x_ref[rows, :] + mix, g_ref[...], b_ref[...])
        y_ref[rows, :] = y
        yb_ref[rows, :] = y.astype(BF16)


def _outproj_ln(os_, w, x, g, b, tm=512):
    T, D = x.shape
    tm = min(tm, T)
    n_o = len(os_)
    row = pl.BlockSpec((tm, D), lambda i: (i, 0))
    vec = pl.BlockSpec((1, D), lambda i: (0, 0))
    return pl.pallas_call(
        functools.partial(_outproj_body, n_o=n_o),
        grid=(T // tm,),
        in_specs=[row] * n_o + [pl.BlockSpec((D, D), lambda i: (0, 0)), row, vec, vec],
        out_specs=[row, row],
        out_shape=[jax.ShapeDtypeStruct((T, D), F32), jax.ShapeDtypeStruct((T, D), BF16)],
        compiler_params=_cparams(("parallel",)),
    )(*os_, w, x, g.reshape(1, D), b.reshape(1, D))


GID_LANE = N_EXPERTS
MOE_WINDOW = 1024
MOE_CHUNK = 128
MOE_VMEM_LIMIT = 60 * 1024 * 1024


def _router_sorted_body(x_ref, wh_ref, wl_ref, bias_ref, gate_ref, gidt_ref, cnt_ref):
    x_hi, x_lo = _split_bf16(x_ref[...])
    wh = wh_ref[...]
    logits = _dot(x_hi, wh) + _dot(x_lo, wh) + _dot(x_hi, wl_ref[...])
    scores = _sigmoid(logits)
    lane = lax.broadcasted_iota(jnp.int32, scores.shape, 1)
    lanef = lane.astype(F32)
    biased = jnp.where(lane < N_EXPERTS, scores + bias_ref[...], REMOVED)

    def top2(mask):
        v = jnp.where(mask, biased, REMOVED)
        m1 = jnp.max(v, axis=-1, keepdims=True)
        i1 = jnp.min(jnp.where(v == m1, lanef, float(LANES)), axis=-1, keepdims=True)
        v2 = jnp.where(lanef == i1, REMOVED, v)
        m2 = jnp.max(v2, axis=-1, keepdims=True)
        i2 = jnp.min(jnp.where(v2 == m2, lanef, float(LANES)), axis=-1, keepdims=True)
        return m1 + m2, jnp.where((lanef == i1) | (lanef == i2), 1.0, 0.0)

    best, best_sel = top2(lane // EXPERTS_PER_GROUP == 0)
    gid = jnp.zeros_like(best)
    for grp in range(1, N_GROUPS):
        score, sel = top2(lane // EXPERTS_PER_GROUP == grp)
        better = score > best
        best = jnp.where(better, score, best)
        best_sel = jnp.where(better, sel, best_sel)
        gid = jnp.where(better, float(grp), gid)
    w = best_sel * scores
    gate = w / jnp.sum(w, axis=-1, keepdims=True)
    gate_ref[...] = jnp.where(lane == GID_LANE, gid, gate)
    gid_b = jnp.broadcast_to(gid, scores.shape)
    gidt_ref[...] = gid_b.T[:8, :]
    cnt_ref[0] = jnp.broadcast_to(
        jnp.sum(jnp.where(lanef == gid_b, 1.0, 0.0), axis=0, keepdims=True), (8, LANES))


def _router_sorted(x, router_w, router_bias, tm):
    T, D = x.shape
    wpad = jnp.zeros((D, LANES), F32).at[:, :N_EXPERTS].set(router_w)
    wh, wl = _split_bf16(wpad)
    bpad = jnp.zeros((1, LANES), F32).at[0, :N_EXPERTS].set(router_bias)
    return pl.pallas_call(
        _router_sorted_body,
        grid=(T // tm,),
        in_specs=[
            pl.BlockSpec((tm, D), lambda i: (i, 0)),
            pl.BlockSpec((D, LANES), lambda i: (0, 0)),
            pl.BlockSpec((D, LANES), lambda i: (0, 0)),
            pl.BlockSpec((1, LANES), lambda i: (0, 0)),
        ],
        out_specs=[
            pl.BlockSpec((tm, LANES), lambda i: (i, 0)),
            pl.BlockSpec((8, tm), lambda i: (0, i)),
            pl.BlockSpec((1, 8, LANES), lambda i: (i, 0, 0)),
        ],
        out_shape=[
            jax.ShapeDtypeStruct((T, LANES), F32),
            jax.ShapeDtypeStruct((8, T), F32),
            jax.ShapeDtypeStruct((T // tm, 8, LANES), F32),
        ],
        compiler_params=_cparams(("parallel",)),
    )(x, wh, wl, bpad)


def _experts_sorted_body(cnt_ref, xb_ref, x_ref, gate_ref, gidt_ref, ltri_ref, utri_ref, wg_ref, wu_ref,
                         wd_ref, g_ref, b_ref, y_ref, yb_ref, xs_ref, gs_ref, acc_ref, rank_ref,
                         *, W, Wp, chunk):
    win = pl.program_id(0)
    e = pl.program_id(1)
    grp = e // EXPERTS_PER_GROUP
    padded = [((cnt_ref[win * N_GROUPS + g] + chunk - 1) // chunk) * chunk for g in range(N_GROUPS)]
    starts = [0]
    for g in range(N_GROUPS - 1):
        starts.append(starts[-1] + padded[g])
    start = starts[0]
    for g in range(1, N_GROUPS):
        start = jnp.where(grp == g, starts[g], start)
    n_chunks = (cnt_ref[win * N_GROUPS + grp] + chunk - 1) // chunk

    @pl.when(e == 0)
    def _():
        gate = gate_ref[...]
        lane = lax.broadcasted_iota(jnp.int32, gate.shape, 1)
        lanef = lane.astype(F32)
        gid = jnp.sum(jnp.where(lane == GID_LANE, gate, 0.0), axis=-1, keepdims=True)
        member = jnp.where((lanef == gid) & (lane < N_GROUPS), 1.0, 0.0)
        earlier = _dot(ltri_ref[...], member.astype(BF16))
        first = jnp.zeros(gate.shape, F32)
        for g in range(1, N_GROUPS):
            first = jnp.where(lane == g, starts[g].astype(F32), first)
        rank = jnp.sum(member * (first + earlier), axis=-1, keepdims=True)
        rank_ref[...] = jnp.broadcast_to(rank, gate.shape)

        gid_r = gidt_ref[...]
        sub = lax.broadcasted_iota(jnp.int32, gid_r.shape, 0)
        member_r = jnp.where(sub.astype(F32) == gid_r, 1.0, 0.0)
        earlier_r = _dot(member_r.astype(BF16), utri_ref[...])
        first_r = jnp.zeros(gid_r.shape, F32)
        for g in range(1, N_GROUPS):
            first_r = jnp.where(sub == g, starts[g].astype(F32), first_r)
        rank_r = jnp.sum(member_r * (first_r + earlier_r), axis=0, keepdims=True)
        rows = lax.broadcasted_iota(jnp.int32, (Wp, W), 0).astype(F32)
        perm = jnp.where(rows == rank_r, 1.0, 0.0).astype(BF16)
        xs_ref[...] = _dot(perm, xb_ref[...]).astype(BF16)
        g_hi, g_lo = _split_bf16(gate)
        gs_ref[...] = _dot(perm, g_hi) + _dot(perm, g_lo)
        acc_ref[...] = jnp.zeros(acc_ref.shape, F32)

    def expert_rows(r0, rows):
        r0 = pl.multiple_of(r0, chunk)
        xc = xs_ref[pl.ds(r0, rows), :]
        a = _dot(xc, wg_ref[0])
        u = _dot(xc, wu_ref[0])
        gs = gs_ref[pl.ds(r0, rows), :]
        lane = lax.broadcasted_iota(jnp.int32, gs.shape, 1)
        gcol = jnp.sum(jnp.where(lane == e, gs, 0.0), axis=-1, keepdims=True)
        h = a * _sigmoid(a) * u * gcol
        acc_ref[pl.ds(r0, rows), :] += _dot(h.astype(BF16), wd_ref[0])

    def three_chunks(i, carry):
        expert_rows(start + i * (3 * chunk), 3 * chunk)
        return carry

    lax.fori_loop(0, n_chunks // 3, three_chunks, 0)
    tail = start + (n_chunks // 3) * (3 * chunk)
    for left in (1, 2):
        pl.when(n_chunks % 3 == left)(functools.partial(expert_rows, tail, left * chunk))

    @pl.when(e == N_EXPERTS - 1)
    def _():
        cols = lax.broadcasted_iota(jnp.int32, (W, Wp), 1).astype(F32)
        unperm = jnp.where(cols == rank_ref[...][:, :1], 1.0, 0.0).astype(BF16)
        ffn = _dot(unperm, acc_ref[...].astype(BF16))
        y = _layer_norm(DN_ALPHA * x_ref[...] + ffn, g_ref[...], b_ref[...])
        y_ref[...] = y
        yb_ref[...] = y.astype(BF16)


def _moe_ln(xb, x, router_w, router_bias, wg, wu, wd, g, b):
    T, D = x.shape
    W = min(MOE_WINDOW, T)
    chunk = MOE_CHUNK
    Wp = W + N_GROUPS * chunk
    E, _, DE = wg.shape
    assert T % W == 0 and W % chunk == 0
    gate, gidt, cnt = _router_sorted(x, router_w, router_bias, W)
    counts = cnt[:, 0, :N_GROUPS].astype(jnp.int32).reshape(-1)
    t = np.arange(W)
    ltri = jnp.asarray((t[None, :] < t[:, None]).astype(np.float32), BF16)
    once = pl.Buffered(1)
    row = lambda shape: pl.BlockSpec(shape, lambda i, e, c: (i, 0))
    vec = pl.BlockSpec((1, D), lambda i, e, c: (0, 0))
    tri = pl.BlockSpec((W, W), lambda i, e, c: (0, 0), pipeline_mode=once)
    grid_spec = pltpu.PrefetchScalarGridSpec(
        num_scalar_prefetch=1,
        grid=(T // W, E),
        in_specs=[
            row((W, D)),
            pl.BlockSpec((W, D), lambda i, e, c: (i, 0), pipeline_mode=once),
            row((W, LANES)),
            pl.BlockSpec((8, W), lambda i, e, c: (0, i)),
            tri, tri,
            pl.BlockSpec((1, D, DE), lambda i, e, c: (e, 0, 0)),
            pl.BlockSpec((1, D, DE), lambda i, e, c: (e, 0, 0)),
            pl.BlockSpec((1, DE, D), lambda i, e, c: (e, 0, 0)),
            vec, vec,
        ],
        out_specs=[row((W, D)), row((W, D))],
        scratch_shapes=[
            pltpu.VMEM((Wp, D), BF16),
            pltpu.VMEM((Wp, LANES), F32),
            pltpu.VMEM((Wp, D), F32),
            pltpu.VMEM((W, LANES), F32),
        ],
    )
    return pl.pallas_call(
        functools.partial(_experts_sorted_body, W=W, Wp=Wp, chunk=chunk),
        grid_spec=grid_spec,
        out_shape=[jax.ShapeDtypeStruct((T, D), F32), jax.ShapeDtypeStruct((T, D), BF16)],
        compiler_params=pltpu.CompilerParams(dimension_semantics=("parallel", "arbitrary"),
                                             vmem_limit_bytes=MOE_VMEM_LIMIT),
    )(counts, xb, x, gate, gidt, ltri, ltri.T, wg, wu, wd, g.reshape(1, D), b.reshape(1, D))


def _block_onehots(S, block, tk):
    key = np.arange(S).reshape(S // tk, 1, tk)
    r = np.arange(LANES).reshape(1, LANES, 1)
    return jnp.asarray((key // block == r).astype(np.float32), BF16)


def _rope_tiled(S):
    cos, sin = _rope_tables(jnp.arange(S))
    reps = LANES // HALF
    return jnp.tile(cos, (1, reps)), jnp.tile(sin, (1, reps))


def _nsa_mixer(xb, B, S, w_in, cmp_k_w1, cmp_k_w2, cmp_v_w1, cmp_v_w2, cmp_k_pos, cmp_v_pos):
    G, HPG, KV = NSA_KV_GROUPS, NSA_HPG, NSA_KV_DIM
    L, STR, SB = NSA_CMP_LEN, NSA_CMP_STRIDE, NSA_SEL_BLOCK
    assert L == 2 * STR and S % SB == 0 and S // SB <= LANES
    T = B * S
    n_cmp = (S - L) // STR + 1
    NC = S // STR
    n_sel = S // SB
    top_n = min(NSA_SEL_TOPN, n_sel)
    assert top_n >= N_FORCED

    cos2, sin2 = _rope_tiled(S)
    wb = w_in.astype(BF16)
    wcol = lambda i: wb[:, D_MODEL + i * KV: D_MODEL + (i + 1) * KV]
    tn = 2 * KV
    tk_s = min(512, S)
    tq_w = NSA_WINDOW // 2
    proj = _proj(xb, wb[:, :D_MODEL + 2 * KV], cos2, sin2, [2] * (D_MODEL // tn) + [0], S, tn=tn)
    ks3 = _proj(xb, wcol(2), cos2, sin2, [1], S, tn=KV).reshape(B, S, KV)
    vt_s = _proj_kt(xb, wcol(3), cos2, sin2, B, S, tk_s, rope=False)
    kt_w = _proj_kt(xb, wcol(4), cos2, sin2, B, S, tq_w)
    vaug = _proj_vaug(xb, wcol(5), tn=KV)
    wg = jnp.zeros((D_MODEL, LANES), BF16).at[:, :3 * N_HEADS].set(wb[:, D_MODEL + 6 * KV:])
    gates = _proj(xb, wg, cos2, sin2, [0], S, out_dtype=F32, tn=LANES).reshape(B, S, LANES)

    col = lambda i: proj[:, D_MODEL + i * KV: D_MODEL + (i + 1) * KV]
    proj3 = proj.reshape(B, S, proj.shape[1])
    vaug3 = vaug.reshape(B, S, vaug.shape[1])

    ccos, csin = _rope_tables(jnp.arange(NC) * STR + (L - 1))
    ccos = jnp.concatenate([ccos, ccos], axis=1)
    csin = jnp.concatenate([csin, csin], axis=1)
    to_rows = lambda t: t.reshape(B, S, G, HEAD_DIM).transpose(0, 2, 1, 3).reshape(B * G, NC, STR * HEAD_DIM)
    kc = _compress(to_rows(col(0)), cmp_k_w1, cmp_k_pos, cmp_k_w2, ccos, csin, True, n_cmp)
    vc = _compress(to_rows(col(1)), cmp_v_w1, cmp_v_pos, cmp_v_w2, ccos, csin, False, n_cmp)
    kct = kc.reshape(B, G, NC, HEAD_DIM).transpose(0, 1, 3, 2)

    ci = np.arange(NC)[:, None]
    sj = np.arange(LANES)[None, :]
    overlap = ((ci * STR < (sj + 1) * SB) & (ci * STR + L > sj * SB) & (ci < n_cmp) & (sj < n_sel))
    overlap = jnp.broadcast_to(jnp.asarray(overlap.astype(np.float32), BF16), (B, G, NC, LANES))
    vo = jnp.concatenate([vc.reshape(B, G, NC, HEAD_DIM), jnp.ones((B, G, NC, 1), BF16),
                          jnp.zeros((B, G, NC, LANES - HEAD_DIM - 1), BF16), overlap], axis=-1)

    o_cmp, selbt = _nsa_cmp(proj3, kct, vo, gates, n_cmp, n_sel, top_n)
    o_slc = _nsa_slc_t(proj3, ks3, vt_s, _block_onehots(S, SB, tk_s).transpose(0, 2, 1), selbt, gates,
                       tq=tk_s, tk=tk_s)
    o_win = _nsa_win(proj3, kt_w, vaug3, 0, gates, tq=tq_w)
    return [o.reshape(T, D_MODEL) for o in (o_cmp, o_slc, o_win)]


def _moba_mixer(xb, B, S, w_in):
    H = N_HEADS
    nb = S // MOBA_BLOCK
    top_k = min(MOBA_TOPK, nb)
    cos2, sin2 = _rope_tiled(S)
    tn = 512
    n_t = D_MODEL // tn
    wb = w_in.astype(BF16)
    tk = min(2 * MOBA_BLOCK, S)
    qt = _proj_kt(xb, wb[:, :D_MODEL], cos2, sin2, B, S, tk, tn=tn, scale=Q_SCALE_LOG2)
    k3 = _proj(xb, wb[:, D_MODEL:2 * D_MODEL], cos2, sin2, [1] * n_t, S, tn=tn).reshape(B, S, D_MODEL)
    vt = _proj_kt(xb, wb[:, 2 * D_MODEL:], cos2, sin2, B, S, tk, tn=tn, rope=False)
    e = _block_onehots(S, MOBA_BLOCK, tk)
    kh, kl = _moba_kmean_pairs(k3, e, tk)
    o = _moba_attn_t(qt, k3, vt, e.transpose(0, 2, 1), kh, kl, nb, top_k, tq=tk, tk=tk)
    return [o.reshape(B * S, D_MODEL)]


def kernel(x, nsa_w_in, nsa_w_out, nsa_cmp_k_w1, nsa_cmp_k_w2, nsa_cmp_v_w1, nsa_cmp_v_w2, nsa_cmp_k_pos, nsa_cmp_v_pos, moba_w_in, moba_w_out, router_w, router_bias, moe_w_gate, moe_w_up, moe_w_down, ln_g, ln_b):
    B, S, D = x.shape
    xf = x.reshape(B * S, D)
    xb = xf.astype(BF16)
    for layer in range(DEPTH):
        j = layer // 2
        if layer % 2 == 0:
            os_ = _nsa_mixer(xb, B, S, nsa_w_in[j], nsa_cmp_k_w1[j], nsa_cmp_k_w2[j], nsa_cmp_v_w1[j],
                             nsa_cmp_v_w2[j], nsa_cmp_k_pos[j], nsa_cmp_v_pos[j])
            w_out = nsa_w_out[j]
        else:
            os_ = _moba_mixer(xb, B, S, moba_w_in[j])
            w_out = moba_w_out[j]
        xf, xb = _outproj_ln(os_, w_out.astype(BF16), xf, ln_g[layer, 0], ln_b[layer, 0])
        xf, xb = _moe_ln(xb, xf, router_w, router_bias, moe_w_gate[layer].astype(BF16),
                         moe_w_up[layer].astype(BF16), moe_w_down[layer].astype(BF16),
                         ln_g[layer, 1], ln_b[layer, 1])
    return xf.reshape(B, S, D)
```

```python
import functools

import jax
import jax.numpy as jnp
import numpy as np
from jax import lax
from jax.experimental import pallas as pl
from jax.experimental.pallas import tpu as pltpu

F32 = jnp.float32
BF16 = jnp.bfloat16

D_MODEL = 1024
N_HEADS = 16
HEAD_DIM = 64
HALF = HEAD_DIM // 2
ROPE_THETA = 10000.0
DEPTH = 2
DN_ALPHA = (2 * DEPTH) ** 0.25
LN_EPS = 1e-5
NEG = -1e30
N_FORCED = 3
MASK = -1e30
REMOVED = -3.0e38
LANES = 128
SUBLANES = 8
Q_SCALE_LOG2 = float(HEAD_DIM ** -0.5 * np.log2(np.e))
ROW_CHUNK = 128
PROJ_ROW_CHUNK = 256

NSA_KV_GROUPS = 4
NSA_HPG = N_HEADS // NSA_KV_GROUPS
NSA_KV_DIM = NSA_KV_GROUPS * HEAD_DIM
NSA_CMP_LEN = 32
NSA_CMP_STRIDE = 16
NSA_SEL_BLOCK = 64
NSA_SEL_TOPN = 16
NSA_WINDOW = 512

MOBA_BLOCK = 256
MOBA_TOPK = 3

N_EXPERTS = 16
N_GROUPS = 4
EXPERTS_PER_GROUP = N_EXPERTS // N_GROUPS

VMEM_LIMIT = 48 * 1024 * 1024


def _cparams(sem):
    return pltpu.CompilerParams(dimension_semantics=sem, vmem_limit_bytes=VMEM_LIMIT)


def _dot(a, b):
    return jnp.dot(a, b, preferred_element_type=F32)


def _split_bf16(x):
    hi = x.astype(BF16)
    lo = (x - hi.astype(F32)).astype(BF16)
    return hi, lo


def _sigmoid(x):
    return 1.0 / (1.0 + jnp.exp(-x))


def _proj_body(mode_ref, x_ref, w_ref, cos_ref, sin_ref, o_ref, *, tn):
    mode = mode_ref[pl.program_id(0)]
    tm = x_ref.shape[0]
    rc = min(PROJ_ROW_CHUNK, tm)

    @pl.when(mode == 0)
    def _():
        for r in range(0, tm, rc):
            o_ref[r:r + rc, :] = _dot(x_ref[r:r + rc, :], w_ref[...]).astype(o_ref.dtype)

    @pl.when(mode != 0)
    def _():
        sc = jnp.where(mode == 2, Q_SCALE_LOG2, 1.0).astype(F32)
        for r in range(0, tm, rc):
            acc = _dot(x_ref[r:r + rc, :], w_ref[...])
            cos = cos_ref[r:r + rc, :] * sc
            sin = sin_ref[r:r + rc, :] * sc
            for c in range(tn // LANES):
                o_ref[r:r + rc, c * LANES:(c + 1) * LANES] = (
                    _rope_chunk(acc[:, c * LANES:(c + 1) * LANES], cos, sin).astype(o_ref.dtype))


def _proj(xb, w, cos2, sin2, modes, seq, out_dtype=BF16, tm=1024, tn=512):
    T, K = xb.shape
    N = w.shape[1]
    tm = min(tm, seq)
    assert T % tm == 0 and N % tn == 0 and seq % tm == 0 and len(modes) == N // tn
    n_pos = seq // tm
    grid_spec = pltpu.PrefetchScalarGridSpec(
        num_scalar_prefetch=1,
        grid=(N // tn, T // tm),
        in_specs=[
            pl.BlockSpec((tm, K), lambda j, i, m: (i, 0)),
            pl.BlockSpec((K, tn), lambda j, i, m: (0, j)),
            pl.BlockSpec((tm, LANES), lambda j, i, m: (i % n_pos, 0)),
            pl.BlockSpec((tm, LANES), lambda j, i, m: (i % n_pos, 0)),
        ],
        out_specs=pl.BlockSpec((tm, tn), lambda j, i, m: (i, j)),
    )
    return pl.pallas_call(
        functools.partial(_proj_body, tn=tn),
        grid_spec=grid_spec,
        out_shape=jax.ShapeDtypeStruct((T, N), out_dtype),
        compiler_params=_cparams(("parallel", "parallel")),
    )(jnp.asarray(modes, jnp.int32), xb, w, cos2, sin2)


def _rope_tables(pos):
    inv = 1.0 / (ROPE_THETA ** (jnp.arange(0, HEAD_DIM, 2, dtype=F32) / HEAD_DIM))
    ang = pos.astype(F32)[:, None] * inv[None, :]
    return jnp.cos(ang), jnp.sin(ang)


def _rope_chunk(a, cos, sin):
    lane = lax.broadcasted_iota(jnp.int32, a.shape, 1)
    up = pltpu.roll(a, LANES - HALF, 1)
    dn = pltpu.roll(a, HALF, 1)
    return a * cos + jnp.where((lane % HEAD_DIM) < HALF, -up, dn) * sin


def _proj_kt_body(x_ref, w_ref, cos_ref, sin_ref, o_ref, *, tn, tk, rope, scale):
    tm = x_ref.shape[0]
    rc = min(PROJ_ROW_CHUNK, tk)
    for r in range(0, tm, rc):
        acc = _dot(x_ref[r:r + rc, :], w_ref[...])
        cos = cos_ref[r:r + rc, :] * scale
        sin = sin_ref[r:r + rc, :] * scale
        for c in range(tn // LANES):
            a = acc[:, c * LANES:(c + 1) * LANES]
            kt = (_rope_chunk(a, cos, sin) if rope else a).T
            for hh in range(2):
                o_ref[0, 2 * c + hh, r // tk, :, r % tk:r % tk + rc] = (
                    kt[hh * HEAD_DIM:(hh + 1) * HEAD_DIM, :].astype(o_ref.dtype))


def _proj_kt(xb, w, cos2, sin2, B, S, tk, tm=1024, tn=256, rope=True, scale=1.0):
    T, K = xb.shape
    N = w.shape[1]
    tm = min(tm, S)
    tn = min(tn, N)
    assert S % tm == 0 and tm % tk == 0 and N % tn == 0
    n_pos = S // tm
    return pl.pallas_call(
        functools.partial(_proj_kt_body, tn=tn, tk=tk, rope=rope, scale=scale),
        grid=(N // tn, T // tm),
        in_specs=[
            pl.BlockSpec((tm, K), lambda j, i: (i, 0)),
            pl.BlockSpec((K, tn), lambda j, i: (0, j)),
            pl.BlockSpec((tm, LANES), lambda j, i: (i % n_pos, 0)),
            pl.BlockSpec((tm, LANES), lambda j, i: (i % n_pos, 0)),
        ],
        out_specs=pl.BlockSpec((1, tn // HEAD_DIM, tm // tk, HEAD_DIM, tk),
                               lambda j, i: (i // n_pos, j, i % n_pos, 0, 0)),
        out_shape=jax.ShapeDtypeStruct((B, N // HEAD_DIM, S // tk, HEAD_DIM, tk), BF16),
        compiler_params=_cparams(("parallel", "parallel")),
    )(xb, w, cos2, sin2)


def _proj_vaug_body(x_ref, w_ref, o_ref, *, tn):
    tm = x_ref.shape[0]
    rc = min(PROJ_ROW_CHUNK, tm)
    lane = lax.broadcasted_iota(jnp.int32, (rc, LANES), 1)
    tail = jnp.where(lane == HEAD_DIM, 1.0, 0.0)
    for r in range(0, tm, rc):
        acc = _dot(x_ref[r:r + rc, :], w_ref[...])
        for c in range(tn // LANES):
            a = acc[:, c * LANES:(c + 1) * LANES]
            for k, head in enumerate((a, pltpu.roll(a, HEAD_DIM, 1))):
                o_ref[r:r + rc, (2 * c + k) * LANES:(2 * c + k + 1) * LANES] = (
                    jnp.where(lane < HEAD_DIM, head, tail).astype(o_ref.dtype))


def _proj_vaug(xb, w, tm=1024, tn=512):
    T, K = xb.shape
    N = w.shape[1]
    tm = min(tm, T)
    assert T % tm == 0 and N % tn == 0
    return pl.pallas_call(
        functools.partial(_proj_vaug_body, tn=tn),
        grid=(N // tn, T // tm),
        in_specs=[pl.BlockSpec((tm, K), lambda j, i: (i, 0)), pl.BlockSpec((K, tn), lambda j, i: (0, j))],
        out_specs=pl.BlockSpec((tm, 2 * tn), lambda j, i: (i, j)),
        out_shape=jax.ShapeDtypeStruct((T, 2 * N), BF16),
        compiler_params=_cparams(("parallel", "parallel")),
    )(xb, w)


def _gelu_tanh(x):
    c = np.float32(np.sqrt(2.0 / np.pi))
    return 0.5 * x * (1.0 + jnp.tanh(c * (x + 0.044715 * (x * x * x))))


def _compress_body(r_ref, w1_ref, pos_ref, w2_ref, w2r_ref, cos_ref, sin_ref, o_ref, *, rope, n_cmp):
    r = r_ref[0]
    nc = r.shape[0]
    half = NSA_CMP_STRIDE * HEAD_DIM
    a = _dot(r, w1_ref[0])
    b = _dot(r, w1_ref[1])
    pos = pos_ref[...]
    pb = _dot(pos[:, :half], w1_ref[0]) + _dot(pos[:, half:], w1_ref[1])
    b_next = pltpu.roll(b, nc - 1, 0)
    h = _gelu_tanh(a + b_next + pb[0:1, :]).astype(BF16)
    o = _dot(h, w2_ref[...])
    if rope:
        o = o * cos_ref[...] + _dot(h, w2r_ref[...]) * sin_ref[...]
    row = lax.broadcasted_iota(jnp.int32, o.shape, 0)
    o_ref[0] = jnp.where(row < n_cmp, o, 0.0).astype(o_ref.dtype)


def _compress(r, w1, pos, w2, cos_c, sin_c, rope, n_cmp):
    BG, NC, K = r.shape
    hidden = w1.shape[1]
    w1s = w1.astype(BF16).reshape(2, K, hidden)
    pos8 = jnp.zeros((SUBLANES, 2 * K), BF16).at[0].set(pos.reshape(-1).astype(BF16))
    w2r = jnp.concatenate([-w2[:, HALF:], w2[:, :HALF]], axis=1).astype(BF16)
    full = lambda shape: pl.BlockSpec(shape, lambda i: (0,) * len(shape))
    return pl.pallas_call(
        functools.partial(_compress_body, rope=rope, n_cmp=n_cmp),
        grid=(BG,),
        in_specs=[
            pl.BlockSpec((1, NC, K), lambda i: (i, 0, 0)),
            full((2, K, hidden)),
            full((SUBLANES, 2 * K)),
            full((hidden, HEAD_DIM)),
            full((hidden, HEAD_DIM)),
            full((NC, HEAD_DIM)),
            full((NC, HEAD_DIM)),
        ],
        out_specs=pl.BlockSpec((1, NC, HEAD_DIM), lambda i: (i, 0, 0)),
        out_shape=jax.ShapeDtypeStruct((BG, NC, HEAD_DIM), BF16),
        compiler_params=_cparams(("parallel",)),
    )(r, w1s, pos8, w2.astype(BF16), w2r, cos_c, sin_c)


def _topk_mask_t(v, k):
    idx = lax.broadcasted_iota(jnp.int32, v.shape, 0).astype(F32)

    def step(_, cur):
        m = jnp.max(cur, axis=0, keepdims=True)
        first = jnp.min(jnp.where(cur == m, idx, float(LANES)), axis=0, keepdims=True)
        return jnp.where(idx == first, REMOVED, cur)

    return lax.fori_loop(0, k, step, v, unroll=True) != v


def _head_kt(t, parity):
    z = jnp.zeros_like(t)
    return jnp.concatenate([t, z] if parity == 0 else [z, t], axis=0)


def _pair_merge(even, odd):
    lane = lax.broadcasted_iota(jnp.int32, even.shape, 1)
    return jnp.where(lane < HEAD_DIM, even, pltpu.roll(odd, HEAD_DIM, 1))


def _gate_col(logits, branch, h, heads_per_step=NSA_HPG):
    idx = branch * N_HEADS + pl.program_id(1) * heads_per_step + h
    lane = lax.broadcasted_iota(jnp.int32, logits.shape, 1)
    return _sigmoid(jnp.sum(jnp.where(lane == idx, logits, 0.0), axis=-1, keepdims=True))


def _tile_iotas(rows, tk):
    return (lax.broadcasted_iota(jnp.int32, (rows, tk), 0),
            lax.broadcasted_iota(jnp.int32, (rows, tk), 1))


def _nsa_cmp_body(q_ref, kct_ref, vo_ref, g_ref, o_ref, selbt_ref, s_ref, m_ref, imp_ref,
                  *, tq, n_cmp, n_sel, top_n):
    s0 = pl.program_id(2) * tq
    q4 = q_ref[0]
    n_chunks = vo_ref.shape[2] // LANES
    rc = min(ROW_CHUNK, tq)

    def attend(nv):
        nc = nv * LANES
        kct = kct_ref[0, 0, :, :nc]
        vo = vo_ref[0, 0, :nc, :]
        tpos = s0 + lax.broadcasted_iota(jnp.int32, (tq, nc), 0)
        nidx = lax.broadcasted_iota(jnp.int32, (tq, nc), 1)
        cmask = (nidx * NSA_CMP_STRIDE + (NSA_CMP_LEN - 1) <= tpos) & (nidx < n_cmp)
        for h in range(NSA_HPG):
            pair = q4[:, (h // 2) * LANES:(h // 2 + 1) * LANES]
            s = jnp.where(cmask, _dot(pair, _head_kt(kct, h % 2)), NEG)
            s_ref[h * tq:(h + 1) * tq, :nc] = s
            cm = s[:, :LANES]
            for c in range(1, nv):
                cm = jnp.maximum(cm, s[:, c * LANES:(c + 1) * LANES])
            m_ref[h * tq:(h + 1) * tq, :] = jnp.broadcast_to(jnp.max(cm, axis=-1, keepdims=True), (tq, LANES))

        for r0 in range(0, tq, rc):
            imp = jnp.zeros((rc, LANES), F32)
            heads = []
            for h in range(NSA_HPG):
                r = h * tq + r0
                m = m_ref[r:r + rc, :]
                e = jnp.concatenate(
                    [jnp.exp2(s_ref[r:r + rc, c * LANES:(c + 1) * LANES] - m) for c in range(nv)],
                    axis=1).astype(BF16)
                res = _dot(e, vo)
                inv = jnp.where(m[:, :1] > 0.5 * NEG, 1.0 / res[:, HEAD_DIM:HEAD_DIM + 1], 0.0)
                gate = _gate_col(g_ref[0, r0:r0 + rc, :], 0, h)
                heads.append(res[:, :LANES] * (inv * gate))
                imp = imp + res[:, LANES:] * inv
            o_ref[0, r0:r0 + rc, :] = jnp.concatenate(
                [_pair_merge(heads[h], heads[h + 1]) for h in range(0, NSA_HPG, 2)], axis=1).astype(o_ref.dtype)
            imp_ref[r0:r0 + rc, :] = imp

    attend(n_chunks)
    imp = imp_ref[...]

    imp_t = imp.T
    blk = lax.broadcasted_iota(jnp.int32, imp_t.shape, 0)
    jq = (s0 + lax.broadcasted_iota(jnp.int32, imp_t.shape, 1)) // NSA_SEL_BLOCK
    forced = (blk == 0) | (blk == jq) | (blk == jq - 1)
    cand = jnp.where(forced | (blk >= n_sel), REMOVED, jnp.where(blk > jq, NEG, imp_t))
    sel_t = _topk_mask_t(cand, top_n - N_FORCED) | forced
    selbt_ref[0, 0] = jnp.where(sel_t, 0.0, MASK).astype(selbt_ref.dtype)


def _nsa_cmp(proj3, kct, vo, g_cmp, n_cmp, n_sel, top_n, tq=256):
    B, S, _ = proj3.shape
    G, HPG = NSA_KV_GROUPS, NSA_HPG
    NC = vo.shape[2]
    tq = min(tq, S)
    assert NC % LANES == 0
    return pl.pallas_call(
        functools.partial(_nsa_cmp_body, tq=tq, n_cmp=n_cmp, n_sel=n_sel, top_n=top_n),
        grid=(B, G, S // tq),
        in_specs=[
            pl.BlockSpec((1, tq, HPG * HEAD_DIM), lambda b, g, i: (b, i, g)),
            pl.BlockSpec((1, 1, HEAD_DIM, NC), lambda b, g, i: (b, g, 0, 0)),
            pl.BlockSpec((1, 1, NC, 2 * LANES), lambda b, g, i: (b, g, 0, 0)),
            pl.BlockSpec((1, tq, LANES), lambda b, g, i: (b, i, 0)),
        ],
        out_specs=[
            pl.BlockSpec((1, tq, HPG * HEAD_DIM), lambda b, g, i: (b, i, g)),
            pl.BlockSpec((1, 1, LANES, tq), lambda b, g, i: (b, g, 0, i)),
        ],
        out_shape=[
            jax.ShapeDtypeStruct((B, S, D_MODEL), BF16),
            jax.ShapeDtypeStruct((B, G, LANES, S), BF16),
        ],
        scratch_shapes=[pltpu.VMEM((HPG * tq, NC), F32), pltpu.VMEM((HPG * tq, LANES), F32),
                        pltpu.VMEM((tq, LANES), F32)],
        compiler_params=_cparams(("parallel", "parallel", "parallel")),
    )(proj3, kct, vo, g_cmp)


def _nsa_win_body(q_ref, kt_ref, v_ref, g_ref, o_ref, s_ref, m_ref, *, tq, n_tiles):
    s0 = pl.program_id(2) * tq
    tk = n_tiles * tq
    k_first = jnp.maximum(pl.program_id(2) - (n_tiles - 1), 0)
    k0 = pl.multiple_of(k_first * tq, tq)
    kt = jnp.concatenate([kt_ref[0, 0, k_first + j] for j in range(n_tiles)], axis=1)
    v = v_ref[0, pl.ds(k0, tk), :]
    row, col = _tile_iotas(tq, tk)
    dist = (s0 - k0) + row - col
    valid = (dist >= 0) & (dist < NSA_WINDOW)
    q4 = q_ref[0]
    for h in range(NSA_HPG):
        pair = q4[:, (h // 2) * LANES:(h // 2 + 1) * LANES]
        s = jnp.where(valid, _dot(pair, _head_kt(kt, h % 2)), MASK)
        s_ref[h * tq:(h + 1) * tq, :] = s
        cm = s[:, :LANES]
        for c in range(1, tk // LANES):
            cm = jnp.maximum(cm, s[:, c * LANES:(c + 1) * LANES])
        m_ref[h * tq:(h + 1) * tq, :] = jnp.broadcast_to(jnp.max(cm, axis=-1, keepdims=True), (tq, LANES))

    rc = min(ROW_CHUNK, tq)
    for r0 in range(0, tq, rc):
        heads = []
        for h in range(NSA_HPG):
            r = h * tq + r0
            m = m_ref[r:r + rc, :]
            p = jnp.concatenate(
                [jnp.exp2(s_ref[r:r + rc, c * LANES:(c + 1) * LANES] - m) for c in range(tk // LANES)],
                axis=1).astype(BF16)
            res = _dot(p, v)
            heads.append(res * (_gate_col(g_ref[0, r0:r0 + rc, :], 2, h) / res[:, HEAD_DIM:HEAD_DIM + 1]))
        o_ref[0, r0:r0 + rc, :] = jnp.concatenate(
            [_pair_merge(heads[h], heads[h + 1]) for h in range(0, NSA_HPG, 2)], axis=1).astype(o_ref.dtype)


def _nsa_win(proj3, kt, vaug3, v_block, g, tq):
    B, S, _ = proj3.shape
    G, HPG = NSA_KV_GROUPS, NSA_HPG
    nk = S // tq
    n_tiles = NSA_WINDOW // tq + 1
    assert NSA_WINDOW % tq == 0 and nk >= n_tiles and kt.shape == (B, G, nk, HEAD_DIM, tq)
    once = pl.Buffered(1)
    return pl.pallas_call(
        functools.partial(_nsa_win_body, tq=tq, n_tiles=n_tiles),
        grid=(B, G, S // tq),
        in_specs=[
            pl.BlockSpec((1, tq, HPG * HEAD_DIM), lambda b, g_, i: (b, i, g_)),
            pl.BlockSpec((1, 1, nk, HEAD_DIM, tq), lambda b, g_, i: (b, g_, 0, 0, 0), pipeline_mode=once),
            pl.BlockSpec((1, S, LANES), lambda b, g_, i: (b, 0, v_block + g_), pipeline_mode=once),
            pl.BlockSpec((1, tq, LANES), lambda b, g_, i: (b, i, 0)),
        ],
        out_specs=pl.BlockSpec((1, tq, HPG * HEAD_DIM), lambda b, g_, i: (b, i, g_)),
        out_shape=jax.ShapeDtypeStruct((B, S, D_MODEL), BF16),
        scratch_shapes=[pltpu.VMEM((HPG * tq, n_tiles * tq), F32), pltpu.VMEM((HPG * tq, LANES), F32)],
        compiler_params=_cparams(("parallel", "parallel", "parallel")),
    )(proj3, kt, vaug3, g)


VALUE_SUB = 2 * LANES


def _flash_t_scratch(n_heads, tq, tk):
    return [pltpu.VMEM((n_heads * 2 * LANES, tq), BF16),
            pltpu.VMEM((n_heads * tk, tq), F32),
            pltpu.VMEM((n_heads * SUBLANES, tq), F32),
            pltpu.VMEM((n_heads * SUBLANES, tq), F32),
            pltpu.VMEM((n_heads * LANES, tq), F32)]


def _flash_t_init(m_ref, acc_ref):
    m_ref[...] = jnp.full(m_ref.shape, -jnp.inf, F32)
    acc_ref[...] = jnp.zeros(acc_ref.shape, F32)


def _flash_t_tile(n_heads, keys_of, vals_of, mask, qa_ref, s_ref, m_ref, alpha_ref, acc_ref):
    aug = 2 * LANES
    tk, tq = s_ref.shape[0] // n_heads, s_ref.shape[1]
    for h in range(n_heads):
        st = _dot(keys_of(h), qa_ref[h * aug:(h + 1) * aug, :])
        if mask is not None:
            st = jnp.where(mask, st, MASK)
        s_ref[h * tk:(h + 1) * tk, :] = st
        m_prev = m_ref[h * SUBLANES:(h + 1) * SUBLANES, :]
        m_new = jnp.maximum(m_prev, jnp.max(st, axis=0, keepdims=True))
        alpha_ref[h * SUBLANES:(h + 1) * SUBLANES, :] = jnp.exp2(m_prev - m_new)
        m_ref[h * SUBLANES:(h + 1) * SUBLANES, :] = m_new
    for h in range(n_heads):
        vals = vals_of(h)
        for q0 in range(0, tq, VALUE_SUB):
            m = m_ref[h * SUBLANES:h * SUBLANES + 1, q0:q0 + VALUE_SUB]
            part = jnp.zeros((LANES, VALUE_SUB), F32)
            for c0 in range(0, tk, VALUE_SUB):
                pt = jnp.exp2(s_ref[h * tk + c0:h * tk + c0 + VALUE_SUB, q0:q0 + VALUE_SUB] - m).astype(BF16)
                part = part + _dot(vals[:, c0:c0 + VALUE_SUB], pt)
            acc_ref[h * LANES:(h + 1) * LANES, q0:q0 + VALUE_SUB] = (
                alpha_ref[h * SUBLANES:h * SUBLANES + 1, q0:q0 + VALUE_SUB]
                * acc_ref[h * LANES:(h + 1) * LANES, q0:q0 + VALUE_SUB] + part)


def _flash_t_loop(tile, s0, tq, tk):
    last = (s0 + tq - 1) // tk

    def full_tile(ki, carry):
        tile(ki, None)
        return carry

    lax.fori_loop(0, last, full_tile, 0)
    krow, qcol = _tile_iotas(tk, tq)
    tile(last, last * tk + krow <= s0 + qcol)


def _flash_t_out(acc_ref, n_heads, gates):
    def head(h):
        acc = acc_ref[h * LANES:(h + 1) * LANES, :]
        return (acc / acc[HEAD_DIM:HEAD_DIM + 1, :]).T * gates[h]
    return jnp.concatenate([_pair_merge(head(h), head(h + 1)) for h in range(0, n_heads, 2)], axis=1)


def _ones_row_block(tk):
    return jnp.where(lax.broadcasted_iota(jnp.int32, (HEAD_DIM, tk), 0) == 0, 1.0, 0.0).astype(BF16)


def _nsa_slc_t_body(q_ref, k_ref, vt_ref, et_ref, selbt_ref, g_ref, o_ref, qa_ref, s_ref, m_ref, alpha_ref,
                    acc_ref, *, tq, tk):
    s0 = pl.program_id(2) * tq
    n_heads = 2 * NSA_HPG
    _flash_t_init(m_ref, acc_ref)
    q8 = q_ref[0]
    for pair in range(n_heads // 2):
        grp = pair // (NSA_HPG // 2)
        pair_t = q8[:, pair * LANES:(pair + 1) * LANES].astype(F32).T
        for hp in range(2):
            qt = pair_t[hp * HEAD_DIM:(hp + 1) * HEAD_DIM, :].astype(BF16)
            h = 2 * pair + hp
            qa_ref[h * 2 * LANES:(h + 1) * 2 * LANES, :] = jnp.concatenate(
                [selbt_ref[0, grp], _head_kt(qt, grp)], axis=0)
    ones_rows = _ones_row_block(tk)

    def tile(ki, mask):
        k0 = pl.multiple_of(ki * tk, tk)
        keys = jnp.concatenate([et_ref[ki], k_ref[0, pl.ds(k0, tk), :]], axis=1)
        vals_of = lambda h: jnp.concatenate([vt_ref[0, h // NSA_HPG, ki], ones_rows], axis=0)
        _flash_t_tile(n_heads, lambda h: keys, vals_of, mask, qa_ref, s_ref, m_ref, alpha_ref, acc_ref)

    _flash_t_loop(tile, s0, tq, tk)
    gates = [_gate_col(g_ref[0], 1, h, n_heads) for h in range(n_heads)]
    o_ref[0] = _flash_t_out(acc_ref, n_heads, gates).astype(o_ref.dtype)


def _nsa_slc_t(proj3, k3, vt, et, selbt, g, tq, tk):
    B, S, _ = proj3.shape
    G, HPG = NSA_KV_GROUPS, NSA_HPG
    nk = S // tk
    assert tk == tq and G % 2 == 0 and vt.shape == (B, G, nk, HEAD_DIM, tk) and et.shape == (nk, tk, LANES)
    once = pl.Buffered(1)
    width = 2 * HPG * HEAD_DIM
    return pl.pallas_call(
        functools.partial(_nsa_slc_t_body, tq=tq, tk=tk),
        grid=(B, G // 2, S // tq),
        in_specs=[
            pl.BlockSpec((1, tq, width), lambda b, p, i: (b, i, p)),
            pl.BlockSpec((1, S, LANES), lambda b, p, i: (b, 0, p), pipeline_mode=once),
            pl.BlockSpec((1, 2, nk, HEAD_DIM, tk), lambda b, p, i: (b, p, 0, 0, 0), pipeline_mode=once),
            pl.BlockSpec((nk, tk, LANES), lambda b, p, i: (0, 0, 0), pipeline_mode=once),
            pl.BlockSpec((1, 2, LANES, tq), lambda b, p, i: (b, p, 0, i)),
            pl.BlockSpec((1, tq, LANES), lambda b, p, i: (b, i, 0)),
        ],
        out_specs=pl.BlockSpec((1, tq, width), lambda b, p, i: (b, i, p)),
        out_shape=jax.ShapeDtypeStruct((B, S, D_MODEL), BF16),
        scratch_shapes=_flash_t_scratch(2 * HPG, tq, tk),
        compiler_params=_cparams(("parallel", "parallel", "parallel")),
    )(proj3, k3, vt, et, selbt, g)


def _moba_kmean_pairs_body(k_ref, e_ref, hi_ref, lo_ref, *, tk):
    nk = e_ref.shape[0]
    km = jnp.zeros((LANES, LANES), F32)
    for ki in range(nk):
        km = km + _dot(e_ref[ki], k_ref[0, ki * tk:(ki + 1) * tk, :])
    hi, lo = _split_bf16(km * (1.0 / MOBA_BLOCK))
    hi_ref[0, 0] = hi
    lo_ref[0, 0] = lo


def _moba_kmean_pairs(k3, e, tk):
    B, S, N = k3.shape
    n_pairs = N // LANES
    nk = S // tk
    spec = pl.BlockSpec((1, 1, LANES, LANES), lambda b, p: (b, p, 0, 0))
    return pl.pallas_call(
        functools.partial(_moba_kmean_pairs_body, tk=tk),
        grid=(B, n_pairs),
        in_specs=[
            pl.BlockSpec((1, S, LANES), lambda b, p: (b, 0, p)),
            pl.BlockSpec((nk, LANES, tk), lambda b, p: (0, 0, 0)),
        ],
        out_specs=[spec, spec],
        out_shape=[jax.ShapeDtypeStruct((B, n_pairs, LANES, LANES), BF16)] * 2,
        compiler_params=_cparams(("parallel", "parallel")),
    )(k3, e)


def _moba_t_body(qt_ref, k_ref, vt_ref, et_ref, kh_ref, kl_ref, o_ref, qa_ref, s_ref, m_ref, alpha_ref,
                 acc_ref, *, tq, tk, hb, nb, top_k):
    s0 = pl.program_id(2) * tq
    _flash_t_init(m_ref, acc_ref)
    aug = 2 * LANES

    blk = lax.broadcasted_iota(jnp.int32, (LANES, tq), 0)
    cb = (s0 + lax.broadcasted_iota(jnp.int32, (LANES, tq), 1)) // MOBA_BLOCK
    for h in range(hb):
        qpad = _head_kt(qt_ref[0, h, 0], h % 2)
        gsc = _dot(kh_ref[0, h // 2], qpad) + _dot(kl_ref[0, h // 2], qpad)
        gsc = jnp.where(blk < cb, gsc, NEG)
        gsc = jnp.where(blk < nb, gsc, REMOVED)
        sel = (_topk_mask_t(gsc, top_k) & (blk < cb)) | (blk == cb)
        qa_ref[h * aug:(h + 1) * aug, :] = jnp.concatenate(
            [jnp.where(sel, 0.0, MASK).astype(BF16), qpad], axis=0)

    ones_rows = _ones_row_block(tk)

    def tile(ki, mask):
        k0 = pl.multiple_of(ki * tk, tk)
        et = et_ref[ki]
        keys_of = lambda h: jnp.concatenate(
            [et, k_ref[0, pl.ds(k0, tk), (h // 2) * LANES:(h // 2 + 1) * LANES]], axis=1)
        vals_of = lambda h: jnp.concatenate([vt_ref[0, h, ki], ones_rows], axis=0)
        _flash_t_tile(hb, keys_of, vals_of, mask, qa_ref, s_ref, m_ref, alpha_ref, acc_ref)

    _flash_t_loop(tile, s0, tq, tk)
    o_ref[0] = _flash_t_out(acc_ref, hb, [1.0] * hb).astype(o_ref.dtype)


def _moba_attn_t(qt, k3, vt, et, kh, kl, nb, top_k, tq, tk, hb=8):
    B, H, nq, _, _ = qt.shape
    S = k3.shape[1]
    nk = S // tk
    assert tk == tq and tk % MOBA_BLOCK == 0 and H % hb == 0 and hb % 2 == 0 and nb <= LANES
    once = pl.Buffered(1)
    return pl.pallas_call(
        functools.partial(_moba_t_body, tq=tq, tk=tk, hb=hb, nb=nb, top_k=top_k),
        grid=(B, H // hb, nq),
        in_specs=[
            pl.BlockSpec((1, hb, 1, HEAD_DIM, tq), lambda b, h, i: (b, h, i, 0, 0)),
            pl.BlockSpec((1, S, hb * HEAD_DIM), lambda b, h, i: (b, 0, h), pipeline_mode=once),
            pl.BlockSpec((1, hb, nk, HEAD_DIM, tk), lambda b, h, i: (b, h, 0, 0, 0), pipeline_mode=once),
            pl.BlockSpec((nk, tk, LANES), lambda b, h, i: (0, 0, 0), pipeline_mode=once),
            pl.BlockSpec((1, hb // 2, LANES, LANES), lambda b, h, i: (b, h, 0, 0), pipeline_mode=once),
            pl.BlockSpec((1, hb // 2, LANES, LANES), lambda b, h, i: (b, h, 0, 0), pipeline_mode=once),
        ],
        out_specs=pl.BlockSpec((1, tq, hb * HEAD_DIM), lambda b, h, i: (b, i, h)),
        out_shape=jax.ShapeDtypeStruct((B, S, D_MODEL), BF16),
        scratch_shapes=_flash_t_scratch(hb, tq, tk),
        compiler_params=_cparams(("parallel", "parallel", "parallel")),
    )(qt, k3, vt, et, kh, kl)


def _layer_norm(r, g, b):
    mu = jnp.mean(r, axis=-1, keepdims=True)
    c = r - mu
    var = jnp.mean(c * c, axis=-1, keepdims=True)
    return c * lax.rsqrt(var + LN_EPS) * g + b


def _outproj_body(*refs, n_o):
    o_refs = refs[:n_o]
    w_ref, x_ref, g_ref, b_ref, y_ref, yb_ref = refs[n_o:]
    tm = x_ref.shape[0]
    rc = min(PROJ_ROW_CHUNK, tm)
    for r0 in range(0, tm, rc):
        rows = slice(r0, r0 + rc)
        o = o_refs[0][rows, :].astype(F32)
        for ref in o_refs[1:]:
            o = o + ref[rows, :].astype(F32)
        mix = _dot(o.astype(BF16), w_ref[...])
        y = _layer_norm(DN_ALPHA * x_ref[rows, :] + mix, g_ref[...], b_ref[...])
        y_ref[rows, :] = y
        yb_ref[rows, :] = y.astype(BF16)


def _outproj_ln(os_, w, x, g, b, tm=512):
    T, D = x.shape
    tm = min(tm, T)
    n_o = len(os_)
    row = pl.BlockSpec((tm, D), lambda i: (i, 0))
    vec = pl.BlockSpec((1, D), lambda i: (0, 0))
    return pl.pallas_call(
        functools.partial(_outproj_body, n_o=n_o),
        grid=(T // tm,),
        in_specs=[row] * n_o + [pl.BlockSpec((D, D), lambda i: (0, 0)), row, vec, vec],
        out_specs=[row, row],
        out_shape=[jax.ShapeDtypeStruct((T, D), F32), jax.ShapeDtypeStruct((T, D), BF16)],
        compiler_params=_cparams(("parallel",)),
    )(*os_, w, x, g.reshape(1, D), b.reshape(1, D))


GID_LANE = N_EXPERTS
MOE_WINDOW = 1024
MOE_CHUNK = 128
MOE_VMEM_LIMIT = 60 * 1024 * 1024


def _router_sorted_body(x_ref, wh_ref, wl_ref, bias_ref, gate_ref, gidt_ref, cnt_ref):
    x_hi, x_lo = _split_bf16(x_ref[...])
    wh = wh_ref[...]
    logits = _dot(x_hi, wh) + _dot(x_lo, wh) + _dot(x_hi, wl_ref[...])
    scores = _sigmoid(logits)
    score_t = scores.T
    biased_t = (scores + bias_ref[...]).T
    tm = score_t.shape[1]
    s_rows = [score_t[e:e + 1, :] for e in range(N_EXPERTS)]
    b_rows = [biased_t[e:e + 1, :] for e in range(N_EXPERTS)]

    def first_max(vals, live):
        m = None
        for v, ok in zip(vals, live):
            cand = jnp.where(ok > 0.5, v, REMOVED)
            m = cand if m is None else jnp.maximum(m, cand)
        hits, found = [], jnp.zeros_like(m)
        for v, ok in zip(vals, live):
            hit = jnp.where((v == m) & (ok > 0.5) & (found < 0.5), 1.0, 0.0)
            found = found + hit
            hits.append(hit)
        return m, hits

    ones = jnp.ones((1, tm), F32)
    best, gid, sels = None, None, []
    for grp in range(N_GROUPS):
        vals = b_rows[grp * EXPERTS_PER_GROUP:(grp + 1) * EXPERTS_PER_GROUP]
        m1, h1 = first_max(vals, [ones] * EXPERTS_PER_GROUP)
        m2, h2 = first_max(vals, [1.0 - h for h in h1])
        sels.append([a + b for a, b in zip(h1, h2)])
        score = m1 + m2
        if grp == 0:
            best, gid = score, jnp.zeros_like(score)
        else:
            better = score > best
            best = jnp.where(better, score, best)
            gid = jnp.where(better, float(grp), gid)
    w_rows = []
    for e in range(N_EXPERTS):
        grp, j = divmod(e, EXPERTS_PER_GROUP)
        w_rows.append(jnp.where((gid == float(grp)) & (sels[grp][j] > 0.5), s_rows[e], 0.0))
    total = w_rows[0]
    for w in w_rows[1:]:
        total = total + w
    rows_out = 3 * SUBLANES
    row = lax.broadcasted_iota(jnp.int32, (rows_out, tm), 0)
    gate_t = jnp.where(row == GID_LANE, gid, 0.0)
    for e in range(N_EXPERTS):
        gate_t = jnp.where(row == e, w_rows[e] / total, gate_t)
    gate_t = jnp.concatenate([gate_t, jnp.zeros((LANES - rows_out, tm), F32)], axis=0)
    gate_ref[...] = gate_t.T
    gidt_ref[...] = jnp.broadcast_to(gid, (SUBLANES, tm))
    lane = lax.broadcasted_iota(jnp.int32, (SUBLANES, LANES), 1)
    counts = jnp.zeros((SUBLANES, LANES), F32)
    for grp in range(N_GROUPS):
        n = jnp.sum(jnp.where(gid == float(grp), 1.0, 0.0), axis=-1, keepdims=True)
        counts = jnp.where(lane == grp, n, counts)
    cnt_ref[0] = counts


def _router_sorted(x, router_w, router_bias, tm):
    T, D = x.shape
    wpad = jnp.zeros((D, LANES), F32).at[:, :N_EXPERTS].set(router_w)
    wh, wl = _split_bf16(wpad)
    bpad = jnp.zeros((1, LANES), F32).at[0, :N_EXPERTS].set(router_bias)
    return pl.pallas_call(
        _router_sorted_body,
        grid=(T // tm,),
        in_specs=[
            pl.BlockSpec((tm, D), lambda i: (i, 0)),
            pl.BlockSpec((D, LANES), lambda i: (0, 0)),
            pl.BlockSpec((D, LANES), lambda i: (0, 0)),
            pl.BlockSpec((1, LANES), lambda i: (0, 0)),
        ],
        out_specs=[
            pl.BlockSpec((tm, LANES), lambda i: (i, 0)),
            pl.BlockSpec((SUBLANES, tm), lambda i: (0, i)),
            pl.BlockSpec((1, SUBLANES, LANES), lambda i: (i, 0, 0)),
        ],
        out_shape=[
            jax.ShapeDtypeStruct((T, LANES), F32),
            jax.ShapeDtypeStruct((SUBLANES, T), F32),
            jax.ShapeDtypeStruct((T // tm, SUBLANES, LANES), F32),
        ],
        compiler_params=_cparams(("parallel",)),
    )(x, wh, wl, bpad)


def _experts_sorted_body(cnt_ref, xb_ref, x_ref, gate_ref, gidt_ref, ltri_ref, utri_ref, wg_ref, wu_ref,
                         wd_ref, g_ref, b_ref, y_ref, yb_ref, xs_ref, gs_ref, acc_ref, rank_ref,
                         *, W, Wp, chunk):
    win = pl.program_id(0)
    e = pl.program_id(1)
    grp = e // EXPERTS_PER_GROUP
    padded = [((cnt_ref[win * N_GROUPS + g] + chunk - 1) // chunk) * chunk for g in range(N_GROUPS)]
    starts = [0]
    for g in range(N_GROUPS - 1):
        starts.append(starts[-1] + padded[g])
    start = starts[0]
    for g in range(1, N_GROUPS):
        start = jnp.where(grp == g, starts[g], start)
    n_chunks = (cnt_ref[win * N_GROUPS + grp] + chunk - 1) // chunk

    @pl.when(e == 0)
    def _():
        gate = gate_ref[...]
        lane = lax.broadcasted_iota(jnp.int32, gate.shape, 1)
        lanef = lane.astype(F32)
        gid = jnp.sum(jnp.where(lane == GID_LANE, gate, 0.0), axis=-1, keepdims=True)
        member = jnp.where((lanef == gid) & (lane < N_GROUPS), 1.0, 0.0)
        earlier = _dot(ltri_ref[...], member.astype(BF16))
        first = jnp.zeros(gate.shape, F32)
        for g in range(1, N_GROUPS):
            first = jnp.where(lane == g, starts[g].astype(F32), first)
        rank = jnp.sum(member * (first + earlier), axis=-1, keepdims=True)
        rank_ref[...] = jnp.broadcast_to(rank, gate.shape)

        gid_r = gidt_ref[...]
        sub = lax.broadcasted_iota(jnp.int32, gid_r.shape, 0)
        member_r = jnp.where(sub.astype(F32) == gid_r, 1.0, 0.0)
        earlier_r = _dot(member_r.astype(BF16), utri_ref[...])
        first_r = jnp.zeros(gid_r.shape, F32)
        for g in range(1, N_GROUPS):
            first_r = jnp.where(sub == g, starts[g].astype(F32), first_r)
        rank_r = jnp.sum(member_r * (first_r + earlier_r), axis=0, keepdims=True)
        rows = lax.broadcasted_iota(jnp.int32, (Wp, W), 0).astype(F32)
        perm = jnp.where(rows == rank_r, 1.0, 0.0).astype(BF16)
        xs_ref[...] = _dot(perm, xb_ref[...]).astype(BF16)
        g_hi, g_lo = _split_bf16(gate)
        gs_ref[...] = _dot(perm, g_hi) + _dot(perm, g_lo)
        acc_ref[...] = jnp.zeros(acc_ref.shape, F32)

    def expert_rows(r0, rows):
        r0 = pl.multiple_of(r0, chunk)
        xc = xs_ref[pl.ds(r0, rows), :]
        a = _dot(xc, wg_ref[0])
        u = _dot(xc, wu_ref[0])
        gs = gs_ref[pl.ds(r0, rows), :]
        lane = lax.broadcasted_iota(jnp.int32, gs.shape, 1)
        gcol = jnp.sum(jnp.where(lane == e, gs, 0.0), axis=-1, keepdims=True)
        h = a * _sigmoid(a) * u * gcol
        acc_ref[pl.ds(r0, rows), :] += _dot(h.astype(BF16), wd_ref[0])

    def three_chunks(i, carry):
        expert_rows(start + i * (3 * chunk), 3 * chunk)
        return carry

    lax.fori_loop(0, n_chunks // 3, three_chunks, 0)
    tail = start + (n_chunks // 3) * (3 * chunk)
    for left in (1, 2):
        pl.when(n_chunks % 3 == left)(functools.partial(expert_rows, tail, left * chunk))

    @pl.when(e == N_EXPERTS - 1)
    def _():
        cols = lax.broadcasted_iota(jnp.int32, (W, Wp), 1).astype(F32)
        unperm = jnp.where(cols == rank_ref[...][:, :1], 1.0, 0.0).astype(BF16)
        ffn = _dot(unperm, acc_ref[...].astype(BF16))
        y = _layer_norm(DN_ALPHA * x_ref[...] + ffn, g_ref[...], b_ref[...])
        y_ref[...] = y
        yb_ref[...] = y.astype(BF16)


def _moe_ln(xb, x, router_w, router_bias, wg, wu, wd, g, b):
    T, D = x.shape
    W = min(MOE_WINDOW, T)
    chunk = MOE_CHUNK
    Wp = W + N_GROUPS * chunk
    E, _, DE = wg.shape
    assert T % W == 0 and W % chunk == 0
    gate, gidt, cnt = _router_sorted(x, router_w, router_bias, W)
    counts = cnt[:, 0, :N_GROUPS].astype(jnp.int32).reshape(-1)
    t = np.arange(W)
    ltri = jnp.asarray((t[None, :] < t[:, None]).astype(np.float32), BF16)
    once = pl.Buffered(1)
    row = lambda shape: pl.BlockSpec(shape, lambda i, e, c: (i, 0))
    vec = pl.BlockSpec((1, D), lambda i, e, c: (0, 0))
    tri = pl.BlockSpec((W, W), lambda i, e, c: (0, 0), pipeline_mode=once)
    grid_spec = pltpu.PrefetchScalarGridSpec(
        num_scalar_prefetch=1,
        grid=(T // W, E),
        in_specs=[
            row((W, D)),
            pl.BlockSpec((W, D), lambda i, e, c: (i, 0), pipeline_mode=once),
            row((W, LANES)),
            pl.BlockSpec((SUBLANES, W), lambda i, e, c: (0, i)),
            tri, tri,
            pl.BlockSpec((1, D, DE), lambda i, e, c: (e, 0, 0)),
            pl.BlockSpec((1, D, DE), lambda i, e, c: (e, 0, 0)),
            pl.BlockSpec((1, DE, D), lambda i, e, c: (e, 0, 0)),
            vec, vec,
        ],
        out_specs=[row((W, D)), row((W, D))],
        scratch_shapes=[
            pltpu.VMEM((Wp, D), BF16),
            pltpu.VMEM((Wp, LANES), F32),
            pltpu.VMEM((Wp, D), F32),
            pltpu.VMEM((W, LANES), F32),
        ],
    )
    return pl.pallas_call(
        functools.partial(_experts_sorted_body, W=W, Wp=Wp, chunk=chunk),
        grid_spec=grid_spec,
        out_shape=[jax.ShapeDtypeStruct((T, D), F32), jax.ShapeDtypeStruct((T, D), BF16)],
        compiler_params=pltpu.CompilerParams(dimension_semantics=("parallel", "arbitrary"),
                                             vmem_limit_bytes=MOE_VMEM_LIMIT),
    )(counts, xb, x, gate, gidt, ltri, ltri.T, wg, wu, wd, g.reshape(1, D), b.reshape(1, D))


def _block_onehots(S, block, tk):
    key = np.arange(S).reshape(S // tk, 1, tk)
    r = np.arange(LANES).reshape(1, LANES, 1)
    return jnp.asarray((key // block == r).astype(np.float32), BF16)


def _rope_tiled(S):
    cos, sin = _rope_tables(jnp.arange(S))
    reps = LANES // HALF
    return jnp.tile(cos, (1, reps)), jnp.tile(sin, (1, reps))


def _nsa_mixer(xb, B, S, w_in, cmp_k_w1, cmp_k_w2, cmp_v_w1, cmp_v_w2, cmp_k_pos, cmp_v_pos):
    G, HPG, KV = NSA_KV_GROUPS, NSA_HPG, NSA_KV_DIM
    L, STR, SB = NSA_CMP_LEN, NSA_CMP_STRIDE, NSA_SEL_BLOCK
    assert L == 2 * STR and S % SB == 0 and S // SB <= LANES
    T = B * S
    n_cmp = (S - L) // STR + 1
    NC = S // STR
    n_sel = S // SB
    top_n = min(NSA_SEL_TOPN, n_sel)
    assert top_n >= N_FORCED

    cos2, sin2 = _rope_tiled(S)
    wb = w_in.astype(BF16)
    wcol = lambda i: wb[:, D_MODEL + i * KV: D_MODEL + (i + 1) * KV]
    tn = 2 * KV
    tk_s = min(512, S)
    tq_w = NSA_WINDOW // 2
    proj = _proj(xb, wb[:, :D_MODEL + 2 * KV], cos2, sin2, [2] * (D_MODEL // tn) + [0], S, tn=tn)
    ks3 = _proj(xb, wcol(2), cos2, sin2, [1], S, tn=KV).reshape(B, S, KV)
    vt_s = _proj_kt(xb, wcol(3), cos2, sin2, B, S, tk_s, rope=False)
    kt_w = _proj_kt(xb, wcol(4), cos2, sin2, B, S, tq_w)
    vaug = _proj_vaug(xb, wcol(5), tn=KV)
    wg = jnp.zeros((D_MODEL, LANES), BF16).at[:, :3 * N_HEADS].set(wb[:, D_MODEL + 6 * KV:])
    gates = _proj(xb, wg, cos2, sin2, [0], S, out_dtype=F32, tn=LANES).reshape(B, S, LANES)

    col = lambda i: proj[:, D_MODEL + i * KV: D_MODEL + (i + 1) * KV]
    proj3 = proj.reshape(B, S, proj.shape[1])
    vaug3 = vaug.reshape(B, S, vaug.shape[1])

    ccos, csin = _rope_tables(jnp.arange(NC) * STR + (L - 1))
    ccos = jnp.concatenate([ccos, ccos], axis=1)
    csin = jnp.concatenate([csin, csin], axis=1)
    to_rows = lambda t: t.reshape(B, S, G, HEAD_DIM).transpose(0, 2, 1, 3).reshape(B * G, NC, STR * HEAD_DIM)
    kc = _compress(to_rows(col(0)), cmp_k_w1, cmp_k_pos, cmp_k_w2, ccos, csin, True, n_cmp)
    vc = _compress(to_rows(col(1)), cmp_v_w1, cmp_v_pos, cmp_v_w2, ccos, csin, False, n_cmp)
    kct = kc.reshape(B, G, NC, HEAD_DIM).transpose(0, 1, 3, 2)

    ci = np.arange(NC)[:, None]
    sj = np.arange(LANES)[None, :]
    overlap = ((ci * STR < (sj + 1) * SB) & (ci * STR + L > sj * SB) & (ci < n_cmp) & (sj < n_sel))
    overlap = jnp.broadcast_to(jnp.asarray(overlap.astype(np.float32), BF16), (B, G, NC, LANES))
    vo = jnp.concatenate([vc.reshape(B, G, NC, HEAD_DIM), jnp.ones((B, G, NC, 1), BF16),
                          jnp.zeros((B, G, NC, LANES - HEAD_DIM - 1), BF16), overlap], axis=-1)

    o_cmp, selbt = _nsa_cmp(proj3, kct, vo, gates, n_cmp, n_sel, top_n)
    o_slc = _nsa_slc_t(proj3, ks3, vt_s, _block_onehots(S, SB, tk_s).transpose(0, 2, 1), selbt, gates,
                       tq=tk_s, tk=tk_s)
    o_win = _nsa_win(proj3, kt_w, vaug3, 0, gates, tq=tq_w)
    return [o.reshape(T, D_MODEL) for o in (o_cmp, o_slc, o_win)]


def _moba_mixer(xb, B, S, w_in):
    H = N_HEADS
    nb = S // MOBA_BLOCK
    top_k = min(MOBA_TOPK, nb)
    cos2, sin2 = _rope_tiled(S)
    tn = 512
    n_t = D_MODEL // tn
    wb = w_in.astype(BF16)
    tk = min(2 * MOBA_BLOCK, S)
    qt = _proj_kt(xb, wb[:, :D_MODEL], cos2, sin2, B, S, tk, tn=tn, scale=Q_SCALE_LOG2)
    k3 = _proj(xb, wb[:, D_MODEL:2 * D_MODEL], cos2, sin2, [1] * n_t, S, tn=tn).reshape(B, S, D_MODEL)
    vt = _proj_kt(xb, wb[:, 2 * D_MODEL:], cos2, sin2, B, S, tk, tn=tn, rope=False)
    e = _block_onehots(S, MOBA_BLOCK, tk)
    kh, kl = _moba_kmean_pairs(k3, e, tk)
    o = _moba_attn_t(qt, k3, vt, e.transpose(0, 2, 1), kh, kl, nb, top_k, tq=tk, tk=tk)
    return [o.reshape(B * S, D_MODEL)]


def kernel(x, nsa_w_in, nsa_w_out, nsa_cmp_k_w1, nsa_cmp_k_w2, nsa_cmp_v_w1, nsa_cmp_v_w2, nsa_cmp_k_pos, nsa_cmp_v_pos, moba_w_in, moba_w_out, router_w, router_bias, moe_w_gate, moe_w_up, moe_w_down, ln_g, ln_b):
    B, S, D = x.shape
    xf = x.reshape(B * S, D)
    xb = xf.astype(BF16)
    for layer in range(DEPTH):
        j = layer // 2
        if layer % 2 == 0:
            os_ = _nsa_mixer(xb, B, S, nsa_w_in[j], nsa_cmp_k_w1[j], nsa_cmp_k_w2[j], nsa_cmp_v_w1[j],
                             nsa_cmp_v_w2[j], nsa_cmp_k_pos[j], nsa_cmp_v_pos[j])
            w_out = nsa_w_out[j]
        else:
            os_ = _moba_mixer(xb, B, S, moba_w_in[j])
            w_out = moba_w_out[j]
        xf, xb = _outproj_ln(os_, w_out.astype(BF16), xf, ln_g[layer, 0], ln_b[layer, 0])
        xf, xb = _moe_ln(xb, xf, router_w, router_bias, moe_w_gate[layer].astype(BF16),
                         moe_w_up[layer].astype(BF16), moe_w_down[layer].astype(BF16),
                         ln_g[layer, 1], ln_b[layer, 1])
    return xf.reshape(B, S, D)
```

```python
import functools

import jax
import jax.numpy as jnp
import numpy as np
from jax import lax
from jax.experimental import pallas as pl
from jax.experimental.pallas import tpu as pltpu

F32 = jnp.float32
BF16 = jnp.bfloat16

D_MODEL = 1024
N_HEADS = 16
HEAD_DIM = 64
HALF = HEAD_DIM // 2
ROPE_THETA = 10000.0
DEPTH = 2
DN_ALPHA = (2 * DEPTH) ** 0.25
LN_EPS = 1e-5
NEG = -1e30
N_FORCED = 3
MASK = -1e30
REMOVED = -3.0e38
LANES = 128
SUBLANES = 8
Q_SCALE_LOG2 = float(HEAD_DIM ** -0.5 * np.log2(np.e))
ROW_CHUNK = 128
PROJ_ROW_CHUNK = 256

NSA_KV_GROUPS = 4
NSA_HPG = N_HEADS // NSA_KV_GROUPS
NSA_KV_DIM = NSA_KV_GROUPS * HEAD_DIM
NSA_CMP_LEN = 32
NSA_CMP_STRIDE = 16
NSA_SEL_BLOCK = 64
NSA_SEL_TOPN = 16
NSA_WINDOW = 512

MOBA_BLOCK = 256
MOBA_TOPK = 3

N_EXPERTS = 16
N_GROUPS = 4
EXPERTS_PER_GROUP = N_EXPERTS // N_GROUPS

VMEM_LIMIT = 48 * 1024 * 1024


def _cparams(sem):
    return pltpu.CompilerParams(dimension_semantics=sem, vmem_limit_bytes=VMEM_LIMIT)


def _dot(a, b):
    return jnp.dot(a, b, preferred_element_type=F32)


def _split_bf16(x):
    hi = x.astype(BF16)
    lo = (x - hi.astype(F32)).astype(BF16)
    return hi, lo


def _sigmoid(x):
    return 1.0 / (1.0 + jnp.exp(-x))


def _proj_body(mode_ref, x_ref, w_ref, cos_ref, sin_ref, o_ref, *, tn):
    mode = mode_ref[pl.program_id(0)]
    tm = x_ref.shape[0]
    rc = min(PROJ_ROW_CHUNK, tm)

    @pl.when(mode == 0)
    def _():
        for r in range(0, tm, rc):
            o_ref[r:r + rc, :] = _dot(x_ref[r:r + rc, :], w_ref[...]).astype(o_ref.dtype)

    @pl.when(mode != 0)
    def _():
        sc = jnp.where(mode == 2, Q_SCALE_LOG2, 1.0).astype(F32)
        for r in range(0, tm, rc):
            acc = _dot(x_ref[r:r + rc, :], w_ref[...])
            cos = cos_ref[r:r + rc, :] * sc
            sin = sin_ref[r:r + rc, :] * sc
            for c in range(tn // LANES):
                o_ref[r:r + rc, c * LANES:(c + 1) * LANES] = (
                    _rope_chunk(acc[:, c * LANES:(c + 1) * LANES], cos, sin).astype(o_ref.dtype))


def _proj(xb, w, cos2, sin2, modes, seq, out_dtype=BF16, tm=1024, tn=512):
    T, K = xb.shape
    N = w.shape[1]
    tm = min(tm, seq)
    assert T % tm == 0 and N % tn == 0 and seq % tm == 0 and len(modes) == N // tn
    n_pos = seq // tm
    grid_spec = pltpu.PrefetchScalarGridSpec(
        num_scalar_prefetch=1,
        grid=(N // tn, T // tm),
        in_specs=[
            pl.BlockSpec((tm, K), lambda j, i, m: (i, 0)),
            pl.BlockSpec((K, tn), lambda j, i, m: (0, j)),
            pl.BlockSpec((tm, LANES), lambda j, i, m: (i % n_pos, 0)),
            pl.BlockSpec((tm, LANES), lambda j, i, m: (i % n_pos, 0)),
        ],
        out_specs=pl.BlockSpec((tm, tn), lambda j, i, m: (i, j)),
    )
    return pl.pallas_call(
        functools.partial(_proj_body, tn=tn),
        grid_spec=grid_spec,
        out_shape=jax.ShapeDtypeStruct((T, N), out_dtype),
        compiler_params=_cparams(("parallel", "parallel")),
    )(jnp.asarray(modes, jnp.int32), xb, w, cos2, sin2)


def _rope_tables(pos):
    inv = 1.0 / (ROPE_THETA ** (jnp.arange(0, HEAD_DIM, 2, dtype=F32) / HEAD_DIM))
    ang = pos.astype(F32)[:, None] * inv[None, :]
    return jnp.cos(ang), jnp.sin(ang)


def _rope_chunk(a, cos, sin):
    lane = lax.broadcasted_iota(jnp.int32, a.shape, 1)
    up = pltpu.roll(a, LANES - HALF, 1)
    dn = pltpu.roll(a, HALF, 1)
    return a * cos + jnp.where((lane % HEAD_DIM) < HALF, -up, dn) * sin


def _proj_kt_body(x_ref, w_ref, cos_ref, sin_ref, o_ref, *, tn, tk, rope, scale):
    tm = x_ref.shape[0]
    rc = min(PROJ_ROW_CHUNK, tk)
    for r in range(0, tm, rc):
        acc = _dot(x_ref[r:r + rc, :], w_ref[...])
        cos = cos_ref[r:r + rc, :] * scale
        sin = sin_ref[r:r + rc, :] * scale
        for c in range(tn // LANES):
            a = acc[:, c * LANES:(c + 1) * LANES]
            kt = (_rope_chunk(a, cos, sin) if rope else a).T
            for hh in range(2):
                o_ref[0, 2 * c + hh, r // tk, :, r % tk:r % tk + rc] = (
                    kt[hh * HEAD_DIM:(hh + 1) * HEAD_DIM, :].astype(o_ref.dtype))


def _proj_kt(xb, w, cos2, sin2, B, S, tk, tm=1024, tn=256, rope=True, scale=1.0):
    T, K = xb.shape
    N = w.shape[1]
    tm = min(tm, S)
    tn = min(tn, N)
    assert S % tm == 0 and tm % tk == 0 and N % tn == 0
    n_pos = S // tm
    return pl.pallas_call(
        functools.partial(_proj_kt_body, tn=tn, tk=tk, rope=rope, scale=scale),
        grid=(N // tn, T // tm),
        in_specs=[
            pl.BlockSpec((tm, K), lambda j, i: (i, 0)),
            pl.BlockSpec((K, tn), lambda j, i: (0, j)),
            pl.BlockSpec((tm, LANES), lambda j, i: (i % n_pos, 0)),
            pl.BlockSpec((tm, LANES), lambda j, i: (i % n_pos, 0)),
        ],
        out_specs=pl.BlockSpec((1, tn // HEAD_DIM, tm // tk, HEAD_DIM, tk),
                               lambda j, i: (i // n_pos, j, i % n_pos, 0, 0)),
        out_shape=jax.ShapeDtypeStruct((B, N // HEAD_DIM, S // tk, HEAD_DIM, tk), BF16),
        compiler_params=_cparams(("parallel", "parallel")),
    )(xb, w, cos2, sin2)


def _gelu_tanh(x):
    c = np.float32(np.sqrt(2.0 / np.pi))
    return 0.5 * x * (1.0 + jnp.tanh(c * (x + 0.044715 * (x * x * x))))


def _compress_body(r_ref, w1_ref, pos_ref, w2_ref, w2r_ref, cos_ref, sin_ref, o_ref, *, rope, n_cmp):
    r = r_ref[0]
    nc = r.shape[0]
    half = NSA_CMP_STRIDE * HEAD_DIM
    a = _dot(r, w1_ref[0])
    b = _dot(r, w1_ref[1])
    pos = pos_ref[...]
    pb = _dot(pos[:, :half], w1_ref[0]) + _dot(pos[:, half:], w1_ref[1])
    b_next = pltpu.roll(b, nc - 1, 0)
    h = _gelu_tanh(a + b_next + pb[0:1, :]).astype(BF16)
    o = _dot(h, w2_ref[...])
    if rope:
        o = o * cos_ref[...] + _dot(h, w2r_ref[...]) * sin_ref[...]
    row = lax.broadcasted_iota(jnp.int32, o.shape, 0)
    o_ref[0] = jnp.where(row < n_cmp, o, 0.0).astype(o_ref.dtype)


def _compress(r, w1, pos, w2, cos_c, sin_c, rope, n_cmp):
    BG, NC, K = r.shape
    hidden = w1.shape[1]
    w1s = w1.astype(BF16).reshape(2, K, hidden)
    pos8 = jnp.zeros((SUBLANES, 2 * K), BF16).at[0].set(pos.reshape(-1).astype(BF16))
    w2r = jnp.concatenate([-w2[:, HALF:], w2[:, :HALF]], axis=1).astype(BF16)
    full = lambda shape: pl.BlockSpec(shape, lambda i: (0,) * len(shape))
    return pl.pallas_call(
        functools.partial(_compress_body, rope=rope, n_cmp=n_cmp),
        grid=(BG,),
        in_specs=[
            pl.BlockSpec((1, NC, K), lambda i: (i, 0, 0)),
            full((2, K, hidden)),
            full((SUBLANES, 2 * K)),
            full((hidden, HEAD_DIM)),
            full((hidden, HEAD_DIM)),
            full((NC, HEAD_DIM)),
            full((NC, HEAD_DIM)),
        ],
        out_specs=pl.BlockSpec((1, NC, HEAD_DIM), lambda i: (i, 0, 0)),
        out_shape=jax.ShapeDtypeStruct((BG, NC, HEAD_DIM), BF16),
        compiler_params=_cparams(("parallel",)),
    )(r, w1s, pos8, w2.astype(BF16), w2r, cos_c, sin_c)


def _topk_mask_t(v, k):
    idx = lax.broadcasted_iota(jnp.int32, v.shape, 0).astype(F32)

    def step(_, cur):
        m = jnp.max(cur, axis=0, keepdims=True)
        first = jnp.min(jnp.where(cur == m, idx, float(LANES)), axis=0, keepdims=True)
        return jnp.where(idx == first, REMOVED, cur)

    return lax.fori_loop(0, k, step, v, unroll=True) != v


def _head_kt(t, parity):
    z = jnp.zeros_like(t)
    return jnp.concatenate([t, z] if parity == 0 else [z, t], axis=0)


def _pair_merge(even, odd):
    lane = lax.broadcasted_iota(jnp.int32, even.shape, 1)
    return jnp.where(lane < HEAD_DIM, even, pltpu.roll(odd, HEAD_DIM, 1))


def _gate_col(logits, branch, h, heads_per_step=NSA_HPG):
    idx = branch * N_HEADS + pl.program_id(1) * heads_per_step + h
    lane = lax.broadcasted_iota(jnp.int32, logits.shape, 1)
    return _sigmoid(jnp.sum(jnp.where(lane == idx, logits, 0.0), axis=-1, keepdims=True))


def _tile_iotas(rows, tk):
    return (lax.broadcasted_iota(jnp.int32, (rows, tk), 0),
            lax.broadcasted_iota(jnp.int32, (rows, tk), 1))


def _nsa_cmp_body(q_ref, kct_ref, vo_ref, g_ref, o_ref, selbt_ref, s_ref, m_ref, imp_ref,
                  *, tq, n_cmp, n_sel, top_n):
    s0 = pl.program_id(2) * tq
    q4 = q_ref[0]
    n_chunks = vo_ref.shape[2] // LANES
    rc = min(ROW_CHUNK, tq)

    def attend(nv):
        nc = nv * LANES
        kct = kct_ref[0, 0, :, :nc]
        vo = vo_ref[0, 0, :nc, :]
        tpos = s0 + lax.broadcasted_iota(jnp.int32, (tq, nc), 0)
        nidx = lax.broadcasted_iota(jnp.int32, (tq, nc), 1)
        cmask = (nidx * NSA_CMP_STRIDE + (NSA_CMP_LEN - 1) <= tpos) & (nidx < n_cmp)
        for h in range(NSA_HPG):
            pair = q4[:, (h // 2) * LANES:(h // 2 + 1) * LANES]
            s = jnp.where(cmask, _dot(pair, _head_kt(kct, h % 2)), NEG)
            s_ref[h * tq:(h + 1) * tq, :nc] = s
            cm = s[:, :LANES]
            for c in range(1, nv):
                cm = jnp.maximum(cm, s[:, c * LANES:(c + 1) * LANES])
            m_ref[h * tq:(h + 1) * tq, :] = jnp.broadcast_to(jnp.max(cm, axis=-1, keepdims=True), (tq, LANES))

        for r0 in range(0, tq, rc):
            imp = jnp.zeros((rc, LANES), F32)
            heads = []
            for h in range(NSA_HPG):
                r = h * tq + r0
                m = m_ref[r:r + rc, :]
                e = jnp.concatenate(
                    [jnp.exp2(s_ref[r:r + rc, c * LANES:(c + 1) * LANES] - m) for c in range(nv)],
                    axis=1).astype(BF16)
                res = _dot(e, vo)
                inv = jnp.where(m[:, :1] > 0.5 * NEG, 1.0 / res[:, HEAD_DIM:HEAD_DIM + 1], 0.0)
                gate = _gate_col(g_ref[0, r0:r0 + rc, :], 0, h)
                heads.append(res[:, :LANES] * (inv * gate))
                imp = imp + res[:, LANES:] * inv
            o_ref[0, r0:r0 + rc, :] = jnp.concatenate(
                [_pair_merge(heads[h], heads[h + 1]) for h in range(0, NSA_HPG, 2)], axis=1).astype(o_ref.dtype)
            imp_ref[r0:r0 + rc, :] = imp

    attend(n_chunks)
    imp = imp_ref[...]

    imp_t = imp.T
    blk = lax.broadcasted_iota(jnp.int32, imp_t.shape, 0)
    jq = (s0 + lax.broadcasted_iota(jnp.int32, imp_t.shape, 1)) // NSA_SEL_BLOCK
    forced = (blk == 0) | (blk == jq) | (blk == jq - 1)
    cand = jnp.where(forced | (blk >= n_sel), REMOVED, jnp.where(blk > jq, NEG, imp_t))
    sel_t = _topk_mask_t(cand, top_n - N_FORCED) | forced
    selbt_ref[0, 0] = jnp.where(sel_t, 0.0, MASK).astype(selbt_ref.dtype)


def _nsa_cmp(proj3, kct, vo, g_cmp, n_cmp, n_sel, top_n, tq=256):
    B, S, _ = proj3.shape
    G, HPG = NSA_KV_GROUPS, NSA_HPG
    NC = vo.shape[2]
    tq = min(tq, S)
    assert NC % LANES == 0
    return pl.pallas_call(
        functools.partial(_nsa_cmp_body, tq=tq, n_cmp=n_cmp, n_sel=n_sel, top_n=top_n),
        grid=(B, G, S // tq),
        in_specs=[
            pl.BlockSpec((1, tq, HPG * HEAD_DIM), lambda b, g, i: (b, i, g)),
            pl.BlockSpec((1, 1, HEAD_DIM, NC), lambda b, g, i: (b, g, 0, 0)),
            pl.BlockSpec((1, 1, NC, 2 * LANES), lambda b, g, i: (b, g, 0, 0)),
            pl.BlockSpec((1, tq, LANES), lambda b, g, i: (b, i, 0)),
        ],
        out_specs=[
            pl.BlockSpec((1, tq, HPG * HEAD_DIM), lambda b, g, i: (b, i, g)),
            pl.BlockSpec((1, 1, LANES, tq), lambda b, g, i: (b, g, 0, i)),
        ],
        out_shape=[
            jax.ShapeDtypeStruct((B, S, D_MODEL), BF16),
            jax.ShapeDtypeStruct((B, G, LANES, S), BF16),
        ],
        scratch_shapes=[pltpu.VMEM((HPG * tq, NC), F32), pltpu.VMEM((HPG * tq, LANES), F32),
                        pltpu.VMEM((tq, LANES), F32)],
        compiler_params=_cparams(("parallel", "parallel", "parallel")),
    )(proj3, kct, vo, g_cmp)


VALUE_SUB = 2 * LANES


def _flash_t_scratch(n_heads, tq, tk, qa_rows=2 * LANES):
    return [pltpu.VMEM((n_heads * qa_rows, tq), BF16),
            pltpu.VMEM((n_heads * tk, tq), F32),
            pltpu.VMEM((n_heads * SUBLANES, tq), F32),
            pltpu.VMEM((n_heads * SUBLANES, tq), F32),
            pltpu.VMEM((n_heads * LANES, tq), F32)]


def _flash_t_init(m_ref, acc_ref):
    m_ref[...] = jnp.full(m_ref.shape, -jnp.inf, F32)
    acc_ref[...] = jnp.zeros(acc_ref.shape, F32)


def _flash_t_tile(n_heads, keys_of, vals_of, mask, qa_ref, s_ref, m_ref, alpha_ref, acc_ref):
    aug = qa_ref.shape[0] // n_heads
    tk, tq = s_ref.shape[0] // n_heads, s_ref.shape[1]
    for h in range(n_heads):
        st = _dot(keys_of(h), qa_ref[h * aug:(h + 1) * aug, :])
        if callable(mask):
            st = st + mask()
        elif mask is not None:
            st = jnp.where(mask, st, MASK)
        s_ref[h * tk:(h + 1) * tk, :] = st
        m_prev = m_ref[h * SUBLANES:(h + 1) * SUBLANES, :]
        m_new = jnp.maximum(m_prev, jnp.max(st, axis=0, keepdims=True))
        alpha_ref[h * SUBLANES:(h + 1) * SUBLANES, :] = jnp.exp2(m_prev - m_new)
        m_ref[h * SUBLANES:(h + 1) * SUBLANES, :] = m_new
    for h in range(n_heads):
        vals = vals_of(h)
        for q0 in range(0, tq, VALUE_SUB):
            m = m_ref[h * SUBLANES:h * SUBLANES + 1, q0:q0 + VALUE_SUB]
            part = jnp.zeros((LANES, VALUE_SUB), F32)
            for c0 in range(0, tk, VALUE_SUB):
                pt = jnp.exp2(s_ref[h * tk + c0:h * tk + c0 + VALUE_SUB, q0:q0 + VALUE_SUB] - m).astype(BF16)
                part = part + _dot(vals[:, c0:c0 + VALUE_SUB], pt)
            acc_ref[h * LANES:(h + 1) * LANES, q0:q0 + VALUE_SUB] = (
                alpha_ref[h * SUBLANES:h * SUBLANES + 1, q0:q0 + VALUE_SUB]
                * acc_ref[h * LANES:(h + 1) * LANES, q0:q0 + VALUE_SUB] + part)


def _flash_t_loop(tile, s0, tq, tk):
    last = (s0 + tq - 1) // tk

    def full_tile(ki, carry):
        tile(ki, None)
        return carry

    lax.fori_loop(0, last, full_tile, 0)
    krow, qcol = _tile_iotas(tk, tq)
    tile(last, last * tk + krow <= s0 + qcol)


def _flash_t_out(acc_ref, n_heads, gates):
    def head(h):
        acc = acc_ref[h * LANES:(h + 1) * LANES, :]
        return (acc / acc[HEAD_DIM:HEAD_DIM + 1, :]).T * gates[h]
    return jnp.concatenate([_pair_merge(head(h), head(h + 1)) for h in range(0, n_heads, 2)], axis=1)


def _ones_row_block(tk):
    return jnp.where(lax.broadcasted_iota(jnp.int32, (HEAD_DIM, tk), 0) == 0, 1.0, 0.0).astype(BF16)


def _nsa_slc_t_body(q_ref, k_ref, vt_ref, et_ref, selbt_ref, g_ref, o_ref, qa_ref, s_ref, m_ref, alpha_ref,
                    acc_ref, *, tq, tk):
    s0 = pl.program_id(2) * tq
    n_heads = 2 * NSA_HPG
    _flash_t_init(m_ref, acc_ref)
    q8 = q_ref[0]
    for pair in range(n_heads // 2):
        grp = pair // (NSA_HPG // 2)
        pair_t = q8[:, pair * LANES:(pair + 1) * LANES].astype(F32).T
        for hp in range(2):
            qt = pair_t[hp * HEAD_DIM:(hp + 1) * HEAD_DIM, :].astype(BF16)
            h = 2 * pair + hp
            qa_ref[h * 2 * LANES:(h + 1) * 2 * LANES, :] = jnp.concatenate(
                [selbt_ref[0, grp], _head_kt(qt, grp)], axis=0)
    ones_rows = _ones_row_block(tk)

    def tile(ki, mask):
        k0 = pl.multiple_of(ki * tk, tk)
        keys = jnp.concatenate([et_ref[ki], k_ref[0, pl.ds(k0, tk), :]], axis=1)
        vals_of = lambda h: jnp.concatenate([vt_ref[0, h // NSA_HPG, ki], ones_rows], axis=0)
        _flash_t_tile(n_heads, lambda h: keys, vals_of, mask, qa_ref, s_ref, m_ref, alpha_ref, acc_ref)

    _flash_t_loop(tile, s0, tq, tk)
    gates = [_gate_col(g_ref[0], 1, h, n_heads) for h in range(n_heads)]
    o_ref[0] = _flash_t_out(acc_ref, n_heads, gates).astype(o_ref.dtype)


def _nsa_slc_t(proj3, k3, k_block, vt, et, selbt, g, tq, tk):
    B, S, _ = proj3.shape
    G, HPG = NSA_KV_GROUPS, NSA_HPG
    nk = S // tk
    assert tk == tq and G % 2 == 0 and vt.shape == (B, G, nk, HEAD_DIM, tk) and et.shape == (nk, tk, LANES)
    once = pl.Buffered(1)
    width = 2 * HPG * HEAD_DIM
    return pl.pallas_call(
        functools.partial(_nsa_slc_t_body, tq=tq, tk=tk),
        grid=(B, G // 2, S // tq),
        in_specs=[
            pl.BlockSpec((1, tq, width), lambda b, p, i: (b, i, p)),
            pl.BlockSpec((1, S, LANES), lambda b, p, i: (b, 0, k_block + p), pipeline_mode=once),
            pl.BlockSpec((1, 2, nk, HEAD_DIM, tk), lambda b, p, i: (b, p, 0, 0, 0), pipeline_mode=once),
            pl.BlockSpec((nk, tk, LANES), lambda b, p, i: (0, 0, 0), pipeline_mode=once),
            pl.BlockSpec((1, 2, LANES, tq), lambda b, p, i: (b, p, 0, i)),
            pl.BlockSpec((1, tq, LANES), lambda b, p, i: (b, i, 0)),
        ],
        out_specs=pl.BlockSpec((1, tq, width), lambda b, p, i: (b, i, p)),
        out_shape=jax.ShapeDtypeStruct((B, S, D_MODEL), BF16),
        scratch_shapes=_flash_t_scratch(2 * HPG, tq, tk),
        compiler_params=_cparams(("parallel", "parallel", "parallel")),
    )(proj3, k3, vt, et, selbt, g)


def _nsa_win_t_body(q_ref, k_ref, vt_ref, g_ref, o_ref, qa_ref, s_ref, m_ref, alpha_ref, acc_ref, bias_ref,
                    *, tq, n_tiles):
    s0 = pl.program_id(2) * tq
    tk = n_tiles * tq
    n_heads = 2 * NSA_HPG
    k_first = jnp.maximum(pl.program_id(2) - (n_tiles - 1), 0)
    k0 = pl.multiple_of(k_first * tq, tq)
    _flash_t_init(m_ref, acc_ref)
    q8 = q_ref[0]
    for pair in range(n_heads // 2):
        grp = pair // (NSA_HPG // 2)
        pair_t = q8[:, pair * LANES:(pair + 1) * LANES].astype(F32).T
        for hp in range(2):
            h = 2 * pair + hp
            qa_ref[h * LANES:(h + 1) * LANES, :] = _head_kt(
                pair_t[hp * HEAD_DIM:(hp + 1) * HEAD_DIM, :].astype(BF16), grp)
    krow, qcol = _tile_iotas(tk, tq)
    dist = (s0 - k0) + qcol - krow
    bias_ref[...] = jnp.where((dist >= 0) & (dist < NSA_WINDOW), 0.0, MASK)
    keys_of = lambda h: k_ref[0, pl.ds(k0, tk), :]
    ones_rows = _ones_row_block(tk)
    vals_of = lambda h: jnp.concatenate(
        [jnp.concatenate([vt_ref[0, h // NSA_HPG, k_first + j] for j in range(n_tiles)], axis=1), ones_rows],
        axis=0)
    _flash_t_tile(n_heads, keys_of, vals_of, lambda: bias_ref[...], qa_ref, s_ref, m_ref, alpha_ref, acc_ref)
    gates = [_gate_col(g_ref[0], 2, h, n_heads) for h in range(n_heads)]
    o_ref[0] = _flash_t_out(acc_ref, n_heads, gates).astype(o_ref.dtype)


def _nsa_win_t(proj3, k3, k_block, vt, g, tq):
    B, S, _ = proj3.shape
    G, HPG = NSA_KV_GROUPS, NSA_HPG
    nk = S // tq
    n_tiles = NSA_WINDOW // tq + 1
    assert NSA_WINDOW % tq == 0 and nk >= n_tiles and G % 2 == 0 and vt.shape == (B, G, nk, HEAD_DIM, tq)
    once = pl.Buffered(1)
    width = 2 * HPG * HEAD_DIM
    return pl.pallas_call(
        functools.partial(_nsa_win_t_body, tq=tq, n_tiles=n_tiles),
        grid=(B, G // 2, S // tq),
        in_specs=[
            pl.BlockSpec((1, tq, width), lambda b, p, i: (b, i, p)),
            pl.BlockSpec((1, S, LANES), lambda b, p, i: (b, 0, k_block + p), pipeline_mode=once),
            pl.BlockSpec((1, 2, nk, HEAD_DIM, tq), lambda b, p, i: (b, p, 0, 0, 0), pipeline_mode=once),
            pl.BlockSpec((1, tq, LANES), lambda b, p, i: (b, i, 0)),
        ],
        out_specs=pl.BlockSpec((1, tq, width), lambda b, p, i: (b, i, p)),
        out_shape=jax.ShapeDtypeStruct((B, S, D_MODEL), BF16),
        scratch_shapes=(_flash_t_scratch(2 * HPG, tq, n_tiles * tq, qa_rows=LANES)
                        + [pltpu.VMEM((n_tiles * tq, tq), F32)]),
        compiler_params=_cparams(("parallel", "parallel", "parallel")),
    )(proj3, k3, vt, g)


def _moba_kmean_pairs_body(k_ref, e_ref, hi_ref, lo_ref, *, tk):
    nk = e_ref.shape[0]
    km = jnp.zeros((LANES, LANES), F32)
    for ki in range(nk):
        km = km + _dot(e_ref[ki], k_ref[0, ki * tk:(ki + 1) * tk, :])
    hi, lo = _split_bf16(km * (1.0 / MOBA_BLOCK))
    hi_ref[0, 0] = hi
    lo_ref[0, 0] = lo


def _moba_kmean_pairs(k3, e, tk):
    B, S, N = k3.shape
    n_pairs = N // LANES
    nk = S // tk
    spec = pl.BlockSpec((1, 1, LANES, LANES), lambda b, p: (b, p, 0, 0))
    return pl.pallas_call(
        functools.partial(_moba_kmean_pairs_body, tk=tk),
        grid=(B, n_pairs),
        in_specs=[
            pl.BlockSpec((1, S, LANES), lambda b, p: (b, 0, p)),
            pl.BlockSpec((nk, LANES, tk), lambda b, p: (0, 0, 0)),
        ],
        out_specs=[spec, spec],
        out_shape=[jax.ShapeDtypeStruct((B, n_pairs, LANES, LANES), BF16)] * 2,
        compiler_params=_cparams(("parallel", "parallel")),
    )(k3, e)


def _moba_t_body(qt_ref, k_ref, vt_ref, et_ref, kh_ref, kl_ref, o_ref, qa_ref, s_ref, m_ref, alpha_ref,
                 acc_ref, *, tq, tk, hb, nb, top_k):
    s0 = pl.program_id(2) * tq
    _flash_t_init(m_ref, acc_ref)
    aug = 2 * LANES

    blk = lax.broadcasted_iota(jnp.int32, (LANES, tq), 0)
    cb = (s0 + lax.broadcasted_iota(jnp.int32, (LANES, tq), 1)) // MOBA_BLOCK
    for h in range(hb):
        qpad = _head_kt(qt_ref[0, h, 0], h % 2)
        gsc = _dot(kh_ref[0, h // 2], qpad) + _dot(kl_ref[0, h // 2], qpad)
        gsc = jnp.where(blk < cb, gsc, NEG)
        gsc = jnp.where(blk < nb, gsc, REMOVED)
        sel = (_topk_mask_t(gsc, top_k) & (blk < cb)) | (blk == cb)
        qa_ref[h * aug:(h + 1) * aug, :] = jnp.concatenate(
            [jnp.where(sel, 0.0, MASK).astype(BF16), qpad], axis=0)

    ones_rows = _ones_row_block(tk)

    def tile(ki, mask):
        k0 = pl.multiple_of(ki * tk, tk)
        et = et_ref[ki]
        keys_of = lambda h: jnp.concatenate(
            [et, k_ref[0, pl.ds(k0, tk), (h // 2) * LANES:(h // 2 + 1) * LANES]], axis=1)
        vals_of = lambda h: jnp.concatenate([vt_ref[0, h, ki], ones_rows], axis=0)
        _flash_t_tile(hb, keys_of, vals_of, mask, qa_ref, s_ref, m_ref, alpha_ref, acc_ref)

    _flash_t_loop(tile, s0, tq, tk)
    o_ref[0] = _flash_t_out(acc_ref, hb, [1.0] * hb).astype(o_ref.dtype)


def _moba_attn_t(qt, k3, vt, et, kh, kl, nb, top_k, tq, tk, hb=8):
    B, H, nq, _, _ = qt.shape
    S = k3.shape[1]
    nk = S // tk
    assert tk == tq and tk % MOBA_BLOCK == 0 and H % hb == 0 and hb % 2 == 0 and nb <= LANES
    once = pl.Buffered(1)
    return pl.pallas_call(
        functools.partial(_moba_t_body, tq=tq, tk=tk, hb=hb, nb=nb, top_k=top_k),
        grid=(B, H // hb, nq),
        in_specs=[
            pl.BlockSpec((1, hb, 1, HEAD_DIM, tq), lambda b, h, i: (b, h, i, 0, 0)),
            pl.BlockSpec((1, S, hb * HEAD_DIM), lambda b, h, i: (b, 0, h), pipeline_mode=once),
            pl.BlockSpec((1, hb, nk, HEAD_DIM, tk), lambda b, h, i: (b, h, 0, 0, 0), pipeline_mode=once),
            pl.BlockSpec((nk, tk, LANES), lambda b, h, i: (0, 0, 0), pipeline_mode=once),
            pl.BlockSpec((1, hb // 2, LANES, LANES), lambda b, h, i: (b, h, 0, 0), pipeline_mode=once),
            pl.BlockSpec((1, hb // 2, LANES, LANES), lambda b, h, i: (b, h, 0, 0), pipeline_mode=once),
        ],
        out_specs=pl.BlockSpec((1, tq, hb * HEAD_DIM), lambda b, h, i: (b, i, h)),
        out_shape=jax.ShapeDtypeStruct((B, S, D_MODEL), BF16),
        scratch_shapes=_flash_t_scratch(hb, tq, tk),
        compiler_params=_cparams(("parallel", "parallel", "parallel")),
    )(qt, k3, vt, et, kh, kl)


def _layer_norm(r, g, b):
    mu = jnp.mean(r, axis=-1, keepdims=True)
    c = r - mu
    var = jnp.mean(c * c, axis=-1, keepdims=True)
    return c * lax.rsqrt(var + LN_EPS) * g + b


def _outproj_body(*refs, n_o):
    o_refs = refs[:n_o]
    w_ref, x_ref, g_ref, b_ref, y_ref, yb_ref = refs[n_o:]
    tm = x_ref.shape[0]
    rc = min(PROJ_ROW_CHUNK, tm)
    for r0 in range(0, tm, rc):
        rows = slice(r0, r0 + rc)
        o = o_refs[0][rows, :].astype(F32)
        for ref in o_refs[1:]:
            o = o + ref[rows, :].astype(F32)
        mix = _dot(o.astype(BF16), w_ref[...])
        y = _layer_norm(DN_ALPHA * x_ref[rows, :] + mix, g_ref[...], b_ref[...])
        y_ref[rows, :] = y
        yb_ref[rows, :] = y.astype(BF16)


def _outproj_ln(os_, w, x, g, b, tm=512):
    T, D = x.shape
    tm = min(tm, T)
    n_o = len(os_)
    row = pl.BlockSpec((tm, D), lambda i: (i, 0))
    vec = pl.BlockSpec((1, D), lambda i: (0, 0))
    return pl.pallas_call(
        functools.partial(_outproj_body, n_o=n_o),
        grid=(T // tm,),
        in_specs=[row] * n_o + [pl.BlockSpec((D, D), lambda i: (0, 0)), row, vec, vec],
        out_specs=[row, row],
        out_shape=[jax.ShapeDtypeStruct((T, D), F32), jax.ShapeDtypeStruct((T, D), BF16)],
        compiler_params=_cparams(("parallel",)),
    )(*os_, w, x, g.reshape(1, D), b.reshape(1, D))


GID_LANE = N_EXPERTS
MOE_WINDOW = 1024
MOE_CHUNK = 128
MOE_VMEM_LIMIT = 60 * 1024 * 1024


def _router_sorted_body(x_ref, wh_ref, wl_ref, bias_ref, gate_ref, gidt_ref, cnt_ref):
    x_hi, x_lo = _split_bf16(x_ref[...])
    wh = wh_ref[...]
    logits = _dot(x_hi, wh) + _dot(x_lo, wh) + _dot(x_hi, wl_ref[...])
    scores = _sigmoid(logits)
    score_t = scores.T
    biased_t = (scores + bias_ref[...]).T
    tm = score_t.shape[1]
    s_rows = [score_t[e:e + 1, :] for e in range(N_EXPERTS)]
    b_rows = [biased_t[e:e + 1, :] for e in range(N_EXPERTS)]

    def first_max(vals, live):
        m = None
        for v, ok in zip(vals, live):
            cand = jnp.where(ok > 0.5, v, REMOVED)
            m = cand if m is None else jnp.maximum(m, cand)
        hits, found = [], jnp.zeros_like(m)
        for v, ok in zip(vals, live):
            hit = jnp.where((v == m) & (ok > 0.5) & (found < 0.5), 1.0, 0.0)
            found = found + hit
            hits.append(hit)
        return m, hits

    ones = jnp.ones((1, tm), F32)
    best, gid, sels = None, None, []
    for grp in range(N_GROUPS):
        vals = b_rows[grp * EXPERTS_PER_GROUP:(grp + 1) * EXPERTS_PER_GROUP]
        m1, h1 = first_max(vals, [ones] * EXPERTS_PER_GROUP)
        m2, h2 = first_max(vals, [1.0 - h for h in h1])
        sels.append([a + b for a, b in zip(h1, h2)])
        score = m1 + m2
        if grp == 0:
            best, gid = score, jnp.zeros_like(score)
        else:
            better = score > best
            best = jnp.where(better, score, best)
            gid = jnp.where(better, float(grp), gid)
    w_rows = []
    for e in range(N_EXPERTS):
        grp, j = divmod(e, EXPERTS_PER_GROUP)
        w_rows.append(jnp.where((gid == float(grp)) & (sels[grp][j] > 0.5), s_rows[e], 0.0))
    total = w_rows[0]
    for w in w_rows[1:]:
        total = total + w
    rows_out = 3 * SUBLANES
    row = lax.broadcasted_iota(jnp.int32, (rows_out, tm), 0)
    gate_t = jnp.where(row == GID_LANE, gid, 0.0)
    for e in range(N_EXPERTS):
        gate_t = jnp.where(row == e, w_rows[e] / total, gate_t)
    gate_t = jnp.concatenate([gate_t, jnp.zeros((LANES - rows_out, tm), F32)], axis=0)
    gate_ref[...] = gate_t.T
    gidt_ref[...] = jnp.broadcast_to(gid, (SUBLANES, tm))
    lane = lax.broadcasted_iota(jnp.int32, (SUBLANES, LANES), 1)
    counts = jnp.zeros((SUBLANES, LANES), F32)
    for grp in range(N_GROUPS):
        n = jnp.sum(jnp.where(gid == float(grp), 1.0, 0.0), axis=-1, keepdims=True)
        counts = jnp.where(lane == grp, n, counts)
    cnt_ref[0] = counts


def _router_sorted(x, router_w, router_bias, tm):
    T, D = x.shape
    wpad = jnp.zeros((D, LANES), F32).at[:, :N_EXPERTS].set(router_w)
    wh, wl = _split_bf16(wpad)
    bpad = jnp.zeros((1, LANES), F32).at[0, :N_EXPERTS].set(router_bias)
    return pl.pallas_call(
        _router_sorted_body,
        grid=(T // tm,),
        in_specs=[
            pl.BlockSpec((tm, D), lambda i: (i, 0)),
            pl.BlockSpec((D, LANES), lambda i: (0, 0)),
            pl.BlockSpec((D, LANES), lambda i: (0, 0)),
            pl.BlockSpec((1, LANES), lambda i: (0, 0)),
        ],
        out_specs=[
            pl.BlockSpec((tm, LANES), lambda i: (i, 0)),
            pl.BlockSpec((SUBLANES, tm), lambda i: (0, i)),
            pl.BlockSpec((1, SUBLANES, LANES), lambda i: (i, 0, 0)),
        ],
        out_shape=[
            jax.ShapeDtypeStruct((T, LANES), F32),
            jax.ShapeDtypeStruct((SUBLANES, T), F32),
            jax.ShapeDtypeStruct((T // tm, SUBLANES, LANES), F32),
        ],
        compiler_params=_cparams(("parallel",)),
    )(x, wh, wl, bpad)


def _experts_sorted_body(cnt_ref, xb_ref, x_ref, gate_ref, gidt_ref, ltri_ref, utri_ref, wg_ref, wu_ref,
                         wd_ref, g_ref, b_ref, y_ref, yb_ref, xs_ref, gs_ref, acc_ref, rank_ref,
                         *, W, Wp, chunk):
    win = pl.program_id(0)
    e = pl.program_id(1)
    grp = e // EXPERTS_PER_GROUP
    padded = [((cnt_ref[win * N_GROUPS + g] + chunk - 1) // chunk) * chunk for g in range(N_GROUPS)]
    starts = [0]
    for g in range(N_GROUPS - 1):
        starts.append(starts[-1] + padded[g])
    start = starts[0]
    for g in range(1, N_GROUPS):
        start = jnp.where(grp == g, starts[g], start)
    n_chunks = (cnt_ref[win * N_GROUPS + grp] + chunk - 1) // chunk

    @pl.when(e == 0)
    def _():
        gate = gate_ref[...]
        lane = lax.broadcasted_iota(jnp.int32, gate.shape, 1)
        lanef = lane.astype(F32)
        gid = jnp.sum(jnp.where(lane == GID_LANE, gate, 0.0), axis=-1, keepdims=True)
        member = jnp.where((lanef == gid) & (lane < N_GROUPS), 1.0, 0.0)
        earlier = _dot(ltri_ref[...], member.astype(BF16))
        first = jnp.zeros(gate.shape, F32)
        for g in range(1, N_GROUPS):
            first = jnp.where(lane == g, starts[g].astype(F32), first)
        rank = jnp.sum(member * (first + earlier), axis=-1, keepdims=True)
        rank_ref[...] = jnp.broadcast_to(rank, gate.shape)

        gid_r = gidt_ref[...]
        sub = lax.broadcasted_iota(jnp.int32, gid_r.shape, 0)
        member_r = jnp.where(sub.astype(F32) == gid_r, 1.0, 0.0)
        earlier_r = _dot(member_r.astype(BF16), utri_ref[...])
        first_r = jnp.zeros(gid_r.shape, F32)
        for g in range(1, N_GROUPS):
            first_r = jnp.where(sub == g, starts[g].astype(F32), first_r)
        rank_r = jnp.sum(member_r * (first_r + earlier_r), axis=0, keepdims=True)
        rows = lax.broadcasted_iota(jnp.int32, (Wp, W), 0).astype(F32)
        perm = jnp.where(rows == rank_r, 1.0, 0.0).astype(BF16)
        xs_ref[...] = _dot(perm, xb_ref[...]).astype(BF16)
        g_hi, g_lo = _split_bf16(gate)
        gs_ref[...] = _dot(perm, g_hi) + _dot(perm, g_lo)
        acc_ref[...] = jnp.zeros(acc_ref.shape, F32)

    def expert_rows(r0, rows):
        r0 = pl.multiple_of(r0, chunk)
        xc = xs_ref[pl.ds(r0, rows), :]
        a = _dot(xc, wg_ref[0])
        u = _dot(xc, wu_ref[0])
        gs = gs_ref[pl.ds(r0, rows), :]
        lane = lax.broadcasted_iota(jnp.int32, gs.shape, 1)
        gcol = jnp.sum(jnp.where(lane == e, gs, 0.0), axis=-1, keepdims=True)
        h = a * _sigmoid(a) * u * gcol
        acc_ref[pl.ds(r0, rows), :] += _dot(h.astype(BF16), wd_ref[0])

    def three_chunks(i, carry):
        expert_rows(start + i * (3 * chunk), 3 * chunk)
        return carry

    lax.fori_loop(0, n_chunks // 3, three_chunks, 0)
    tail = start + (n_chunks // 3) * (3 * chunk)
    for left in (1, 2):
        pl.when(n_chunks % 3 == left)(functools.partial(expert_rows, tail, left * chunk))

    @pl.when(e == N_EXPERTS - 1)
    def _():
        cols = lax.broadcasted_iota(jnp.int32, (W, Wp), 1).astype(F32)
        unperm = jnp.where(cols == rank_ref[...][:, :1], 1.0, 0.0).astype(BF16)
        ffn = _dot(unperm, acc_ref[...].astype(BF16))
        y = _layer_norm(DN_ALPHA * x_ref[...] + ffn, g_ref[...], b_ref[...])
        y_ref[...] = y
        yb_ref[...] = y.astype(BF16)


def _moe_ln(xb, x, router_w, router_bias, wg, wu, wd, g, b):
    T, D = x.shape
    W = min(MOE_WINDOW, T)
    chunk = MOE_CHUNK
    Wp = W + N_GROUPS * chunk
    E, _, DE = wg.shape
    assert T % W == 0 and W % chunk == 0
    gate, gidt, cnt = _router_sorted(x, router_w, router_bias, W)
    counts = cnt[:, 0, :N_GROUPS].astype(jnp.int32).reshape(-1)
    t = np.arange(W)
    ltri = jnp.asarray((t[None, :] < t[:, None]).astype(np.float32), BF16)
    once = pl.Buffered(1)
    row = lambda shape: pl.BlockSpec(shape, lambda i, e, c: (i, 0))
    vec = pl.BlockSpec((1, D), lambda i, e, c: (0, 0))
    tri = pl.BlockSpec((W, W), lambda i, e, c: (0, 0), pipeline_mode=once)
    grid_spec = pltpu.PrefetchScalarGridSpec(
        num_scalar_prefetch=1,
        grid=(T // W, E),
        in_specs=[
            row((W, D)),
            pl.BlockSpec((W, D), lambda i, e, c: (i, 0), pipeline_mode=once),
            row((W, LANES)),
            pl.BlockSpec((SUBLANES, W), lambda i, e, c: (0, i)),
            tri, tri,
            pl.BlockSpec((1, D, DE), lambda i, e, c: (e, 0, 0)),
            pl.BlockSpec((1, D, DE), lambda i, e, c: (e, 0, 0)),
            pl.BlockSpec((1, DE, D), lambda i, e, c: (e, 0, 0)),
            vec, vec,
        ],
        out_specs=[row((W, D)), row((W, D))],
        scratch_shapes=[
            pltpu.VMEM((Wp, D), BF16),
            pltpu.VMEM((Wp, LANES), F32),
            pltpu.VMEM((Wp, D), F32),
            pltpu.VMEM((W, LANES), F32),
        ],
    )
    return pl.pallas_call(
        functools.partial(_experts_sorted_body, W=W, Wp=Wp, chunk=chunk),
        grid_spec=grid_spec,
        out_shape=[jax.ShapeDtypeStruct((T, D), F32), jax.ShapeDtypeStruct((T, D), BF16)],
        compiler_params=pltpu.CompilerParams(dimension_semantics=("parallel", "arbitrary"),
                                             vmem_limit_bytes=MOE_VMEM_LIMIT),
    )(counts, xb, x, gate, gidt, ltri, ltri.T, wg, wu, wd, g.reshape(1, D), b.reshape(1, D))


def _block_onehots(S, block, tk):
    key = np.arange(S).reshape(S // tk, 1, tk)
    r = np.arange(LANES).reshape(1, LANES, 1)
    return jnp.asarray((key // block == r).astype(np.float32), BF16)


def _rope_tiled(S):
    cos, sin = _rope_tables(jnp.arange(S))
    reps = LANES // HALF
    return jnp.tile(cos, (1, reps)), jnp.tile(sin, (1, reps))


def _nsa_mixer(xb, B, S, w_in, cmp_k_w1, cmp_k_w2, cmp_v_w1, cmp_v_w2, cmp_k_pos, cmp_v_pos):
    G, HPG, KV = NSA_KV_GROUPS, NSA_HPG, NSA_KV_DIM
    L, STR, SB = NSA_CMP_LEN, NSA_CMP_STRIDE, NSA_SEL_BLOCK
    assert L == 2 * STR and S % SB == 0 and S // SB <= LANES
    T = B * S
    n_cmp = (S - L) // STR + 1
    NC = S // STR
    n_sel = S // SB
    top_n = min(NSA_SEL_TOPN, n_sel)
    assert top_n >= N_FORCED

    cos2, sin2 = _rope_tiled(S)
    wb = w_in.astype(BF16)
    wcol = lambda i: wb[:, D_MODEL + i * KV: D_MODEL + (i + 1) * KV]
    tn = 2 * KV
    tk_s = min(512, S)
    tq_w = NSA_WINDOW // 2
    proj = _proj(xb, wb[:, :D_MODEL + 2 * KV], cos2, sin2, [2] * (D_MODEL // tn) + [0], S, tn=tn)
    kk3 = _proj(xb, jnp.concatenate([wcol(2), wcol(4)], axis=1), cos2, sin2, [1], S, tn=tn).reshape(B, S, tn)
    vt_s = _proj_kt(xb, wcol(3), cos2, sin2, B, S, tk_s, rope=False)
    vt_w = _proj_kt(xb, wcol(5), cos2, sin2, B, S, tq_w, rope=False)
    wg = jnp.zeros((D_MODEL, LANES), BF16).at[:, :3 * N_HEADS].set(wb[:, D_MODEL + 6 * KV:])
    gates = _proj(xb, wg, cos2, sin2, [0], S, out_dtype=F32, tn=LANES).reshape(B, S, LANES)

    col = lambda i: proj[:, D_MODEL + i * KV: D_MODEL + (i + 1) * KV]
    proj3 = proj.reshape(B, S, proj.shape[1])

    ccos, csin = _rope_tables(jnp.arange(NC) * STR + (L - 1))
    ccos = jnp.concatenate([ccos, ccos], axis=1)
    csin = jnp.concatenate([csin, csin], axis=1)
    to_rows = lambda t: t.reshape(B, S, G, HEAD_DIM).transpose(0, 2, 1, 3).reshape(B * G, NC, STR * HEAD_DIM)
    kc = _compress(to_rows(col(0)), cmp_k_w1, cmp_k_pos, cmp_k_w2, ccos, csin, True, n_cmp)
    vc = _compress(to_rows(col(1)), cmp_v_w1, cmp_v_pos, cmp_v_w2, ccos, csin, False, n_cmp)
    kct = kc.reshape(B, G, NC, HEAD_DIM).transpose(0, 1, 3, 2)

    ci = np.arange(NC)[:, None]
    sj = np.arange(LANES)[None, :]
    overlap = ((ci * STR < (sj + 1) * SB) & (ci * STR + L > sj * SB) & (ci < n_cmp) & (sj < n_sel))
    overlap = jnp.broadcast_to(jnp.asarray(overlap.astype(np.float32), BF16), (B, G, NC, LANES))
    vo = jnp.concatenate([vc.reshape(B, G, NC, HEAD_DIM), jnp.ones((B, G, NC, 1), BF16),
                          jnp.zeros((B, G, NC, LANES - HEAD_DIM - 1), BF16), overlap], axis=-1)

    o_cmp, selbt = _nsa_cmp(proj3, kct, vo, gates, n_cmp, n_sel, top_n)
    o_slc = _nsa_slc_t(proj3, kk3, 0, vt_s, _block_onehots(S, SB, tk_s).transpose(0, 2, 1), selbt, gates,
                       tq=tk_s, tk=tk_s)
    o_win = _nsa_win_t(proj3, kk3, G // 2, vt_w, gates, tq=tq_w)
    return [o.reshape(T, D_MODEL) for o in (o_cmp, o_slc, o_win)]


def _moba_mixer(xb, B, S, w_in):
    H = N_HEADS
    nb = S // MOBA_BLOCK
    top_k = min(MOBA_TOPK, nb)
    cos2, sin2 = _rope_tiled(S)
    tn = 512
    n_t = D_MODEL // tn
    wb = w_in.astype(BF16)
    tk = min(2 * MOBA_BLOCK, S)
    qt = _proj_kt(xb, wb[:, :D_MODEL], cos2, sin2, B, S, tk, tn=tn, scale=Q_SCALE_LOG2)
    k3 = _proj(xb, wb[:, D_MODEL:2 * D_MODEL], cos2, sin2, [1] * n_t, S, tn=tn).reshape(B, S, D_MODEL)
    vt = _proj_kt(xb, wb[:, 2 * D_MODEL:], cos2, sin2, B, S, tk, tn=tn, rope=False)
    e = _block_onehots(S, MOBA_BLOCK, tk)
    kh, kl = _moba_kmean_pairs(k3, e, tk)
    o = _moba_attn_t(qt, k3, vt, e.transpose(0, 2, 1), kh, kl, nb, top_k, tq=tk, tk=tk)
    return [o.reshape(B * S, D_MODEL)]


def kernel(x, nsa_w_in, nsa_w_out, nsa_cmp_k_w1, nsa_cmp_k_w2, nsa_cmp_v_w1, nsa_cmp_v_w2, nsa_cmp_k_pos, nsa_cmp_v_pos, moba_w_in, moba_w_out, router_w, router_bias, moe_w_gate, moe_w_up, moe_w_down, ln_g, ln_b):
    B, S, D = x.shape
    xf = x.reshape(B * S, D)
    xb = xf.astype(BF16)
    for layer in range(DEPTH):
        j = layer // 2
        if layer % 2 == 0:
            os_ = _nsa_mixer(xb, B, S, nsa_w_in[j], nsa_cmp_k_w1[j], nsa_cmp_k_w2[j], nsa_cmp_v_w1[j],
                             nsa_cmp_v_w2[j], nsa_cmp_k_pos[j], nsa_cmp_v_pos[j])
            w_out = nsa_w_out[j]
        else:
            os_ = _moba_mixer(xb, B, S, moba_w_in[j])
            w_out = moba_w_out[j]
        xf, xb = _outproj_ln(os_, w_out.astype(BF16), xf, ln_g[layer, 0], ln_b[layer, 0])
        xf, xb = _moe_ln(xb, xf, router_w, router_bias, moe_w_gate[layer].astype(BF16),
                         moe_w_up[layer].astype(BF16), moe_w_down[layer].astype(BF16),
                         ln_g[layer, 1], ln_b[layer, 1])
    return xf.reshape(B, S, D)
```

```python
import functools

import jax
import jax.numpy as jnp
import numpy as np
from jax import lax
from jax.experimental import pallas as pl
from jax.experimental.pallas import tpu as pltpu

F32 = jnp.float32
BF16 = jnp.bfloat16

D_MODEL = 1024
N_HEADS = 16
HEAD_DIM = 64
HALF = HEAD_DIM // 2
ROPE_THETA = 10000.0
DEPTH = 2
DN_ALPHA = (2 * DEPTH) ** 0.25
LN_EPS = 1e-5
NEG = -1e30
N_FORCED = 3
MASK = -1e30
REMOVED = -3.0e38
LANES = 128
SUBLANES = 8
Q_SCALE_LOG2 = float(HEAD_DIM ** -0.5 * np.log2(np.e))
PROJ_ROW_CHUNK = 256

NSA_KV_GROUPS = 4
NSA_HPG = N_HEADS // NSA_KV_GROUPS
NSA_KV_DIM = NSA_KV_GROUPS * HEAD_DIM
NSA_CMP_LEN = 32
NSA_CMP_STRIDE = 16
NSA_SEL_BLOCK = 64
NSA_SEL_TOPN = 16
NSA_WINDOW = 512

MOBA_BLOCK = 256
MOBA_TOPK = 3

N_EXPERTS = 16
N_GROUPS = 4
EXPERTS_PER_GROUP = N_EXPERTS // N_GROUPS

VMEM_LIMIT = 48 * 1024 * 1024


def _cparams(sem):
    return pltpu.CompilerParams(dimension_semantics=sem, vmem_limit_bytes=VMEM_LIMIT)


def _dot(a, b):
    return jnp.dot(a, b, preferred_element_type=F32)


def _split_bf16(x):
    hi = x.astype(BF16)
    lo = (x - hi.astype(F32)).astype(BF16)
    return hi, lo


def _sigmoid(x):
    return 1.0 / (1.0 + jnp.exp(-x))


def _proj_body(mode_ref, x_ref, w_ref, cos_ref, sin_ref, o_ref, *, tn):
    mode = mode_ref[pl.program_id(0)]
    tm = x_ref.shape[0]
    rc = min(PROJ_ROW_CHUNK, tm)

    @pl.when(mode == 0)
    def _():
        for r in range(0, tm, rc):
            o_ref[r:r + rc, :] = _dot(x_ref[r:r + rc, :], w_ref[...]).astype(o_ref.dtype)

    @pl.when(mode != 0)
    def _():
        sc = jnp.where(mode == 2, Q_SCALE_LOG2, 1.0).astype(F32)
        for r in range(0, tm, rc):
            acc = _dot(x_ref[r:r + rc, :], w_ref[...])
            cos = cos_ref[r:r + rc, :] * sc
            sin = sin_ref[r:r + rc, :] * sc
            for c in range(tn // LANES):
                o_ref[r:r + rc, c * LANES:(c + 1) * LANES] = (
                    _rope_chunk(acc[:, c * LANES:(c + 1) * LANES], cos, sin).astype(o_ref.dtype))


def _proj(xb, w, cos2, sin2, modes, seq, out_dtype=BF16, tm=1024, tn=512):
    T, K = xb.shape
    N = w.shape[1]
    tm = min(tm, seq)
    assert T % tm == 0 and N % tn == 0 and seq % tm == 0 and len(modes) == N // tn
    n_pos = seq // tm
    grid_spec = pltpu.PrefetchScalarGridSpec(
        num_scalar_prefetch=1,
        grid=(N // tn, T // tm),
        in_specs=[
            pl.BlockSpec((tm, K), lambda j, i, m: (i, 0)),
            pl.BlockSpec((K, tn), lambda j, i, m: (0, j)),
            pl.BlockSpec((tm, LANES), lambda j, i, m: (i % n_pos, 0)),
            pl.BlockSpec((tm, LANES), lambda j, i, m: (i % n_pos, 0)),
        ],
        out_specs=pl.BlockSpec((tm, tn), lambda j, i, m: (i, j)),
    )
    return pl.pallas_call(
        functools.partial(_proj_body, tn=tn),
        grid_spec=grid_spec,
        out_shape=jax.ShapeDtypeStruct((T, N), out_dtype),
        compiler_params=_cparams(("parallel", "parallel")),
    )(jnp.asarray(modes, jnp.int32), xb, w, cos2, sin2)


def _rope_tables(pos):
    inv = 1.0 / (ROPE_THETA ** (jnp.arange(0, HEAD_DIM, 2, dtype=F32) / HEAD_DIM))
    ang = pos.astype(F32)[:, None] * inv[None, :]
    return jnp.cos(ang), jnp.sin(ang)


def _rope_chunk(a, cos, sin):
    lane = lax.broadcasted_iota(jnp.int32, a.shape, 1)
    up = pltpu.roll(a, LANES - HALF, 1)
    dn = pltpu.roll(a, HALF, 1)
    return a * cos + jnp.where((lane % HEAD_DIM) < HALF, -up, dn) * sin


def _proj_kt_body(x_ref, w_ref, cos_ref, sin_ref, o_ref, *, tn, tk, rope, scale):
    tm = x_ref.shape[0]
    rc = min(PROJ_ROW_CHUNK, tk)
    for r in range(0, tm, rc):
        acc = _dot(x_ref[r:r + rc, :], w_ref[...])
        cos = cos_ref[r:r + rc, :] * scale
        sin = sin_ref[r:r + rc, :] * scale
        for c in range(tn // LANES):
            a = acc[:, c * LANES:(c + 1) * LANES]
            kt = (_rope_chunk(a, cos, sin) if rope else a).T
            for hh in range(2):
                o_ref[0, 2 * c + hh, r // tk, :, r % tk:r % tk + rc] = (
                    kt[hh * HEAD_DIM:(hh + 1) * HEAD_DIM, :].astype(o_ref.dtype))


def _proj_kt(xb, w, cos2, sin2, B, S, tk, tm=1024, tn=256, rope=True, scale=1.0):
    T, K = xb.shape
    N = w.shape[1]
    tm = min(tm, S)
    tn = min(tn, N)
    assert S % tm == 0 and tm % tk == 0 and N % tn == 0
    n_pos = S // tm
    return pl.pallas_call(
        functools.partial(_proj_kt_body, tn=tn, tk=tk, rope=rope, scale=scale),
        grid=(N // tn, T // tm),
        in_specs=[
            pl.BlockSpec((tm, K), lambda j, i: (i, 0)),
            pl.BlockSpec((K, tn), lambda j, i: (0, j)),
            pl.BlockSpec((tm, LANES), lambda j, i: (i % n_pos, 0)),
            pl.BlockSpec((tm, LANES), lambda j, i: (i % n_pos, 0)),
        ],
        out_specs=pl.BlockSpec((1, tn // HEAD_DIM, tm // tk, HEAD_DIM, tk),
                               lambda j, i: (i // n_pos, j, i % n_pos, 0, 0)),
        out_shape=jax.ShapeDtypeStruct((B, N // HEAD_DIM, S // tk, HEAD_DIM, tk), BF16),
        compiler_params=_cparams(("parallel", "parallel")),
    )(xb, w, cos2, sin2)


def _gelu_tanh(x):
    c = np.float32(np.sqrt(2.0 / np.pi))
    return 0.5 * x * (1.0 + jnp.tanh(c * (x + 0.044715 * (x * x * x))))


def _compress_body(r_ref, w1_ref, pos_ref, w2_ref, w2r_ref, cos_ref, sin_ref, o_ref, *, rope, n_cmp):
    r = r_ref[0]
    nc = r.shape[0]
    half = NSA_CMP_STRIDE * HEAD_DIM
    a = _dot(r, w1_ref[0])
    b = _dot(r, w1_ref[1])
    pos = pos_ref[...]
    pb = _dot(pos[:, :half], w1_ref[0]) + _dot(pos[:, half:], w1_ref[1])
    b_next = pltpu.roll(b, nc - 1, 0)
    h = _gelu_tanh(a + b_next + pb[0:1, :]).astype(BF16)
    o = _dot(h, w2_ref[...])
    if rope:
        o = o * cos_ref[...] + _dot(h, w2r_ref[...]) * sin_ref[...]
    row = lax.broadcasted_iota(jnp.int32, o.shape, 0)
    o_ref[0] = jnp.where(row < n_cmp, o, 0.0).astype(o_ref.dtype)


def _compress(r, w1, pos, w2, cos_c, sin_c, rope, n_cmp):
    BG, NC, K = r.shape
    hidden = w1.shape[1]
    w1s = w1.astype(BF16).reshape(2, K, hidden)
    pos8 = jnp.zeros((SUBLANES, 2 * K), BF16).at[0].set(pos.reshape(-1).astype(BF16))
    w2r = jnp.concatenate([-w2[:, HALF:], w2[:, :HALF]], axis=1).astype(BF16)
    full = lambda shape: pl.BlockSpec(shape, lambda i: (0,) * len(shape))
    return pl.pallas_call(
        functools.partial(_compress_body, rope=rope, n_cmp=n_cmp),
        grid=(BG,),
        in_specs=[
            pl.BlockSpec((1, NC, K), lambda i: (i, 0, 0)),
            full((2, K, hidden)),
            full((SUBLANES, 2 * K)),
            full((hidden, HEAD_DIM)),
            full((hidden, HEAD_DIM)),
            full((NC, HEAD_DIM)),
            full((NC, HEAD_DIM)),
        ],
        out_specs=pl.BlockSpec((1, NC, HEAD_DIM), lambda i: (i, 0, 0)),
        out_shape=jax.ShapeDtypeStruct((BG, NC, HEAD_DIM), BF16),
        compiler_params=_cparams(("parallel",)),
    )(r, w1s, pos8, w2.astype(BF16), w2r, cos_c, sin_c)


def _topk_mask_t(v, k):
    idx = lax.broadcasted_iota(jnp.int32, v.shape, 0).astype(F32)

    def step(_, cur):
        m = jnp.max(cur, axis=0, keepdims=True)
        first = jnp.min(jnp.where(cur == m, idx, float(LANES)), axis=0, keepdims=True)
        return jnp.where(idx == first, REMOVED, cur)

    return lax.fori_loop(0, k, step, v, unroll=True) != v


def _head_kt(t, parity):
    z = jnp.zeros_like(t)
    return jnp.concatenate([t, z] if parity == 0 else [z, t], axis=0)


def _pair_merge(even, odd):
    lane = lax.broadcasted_iota(jnp.int32, even.shape, 1)
    return jnp.where(lane < HEAD_DIM, even, pltpu.roll(odd, HEAD_DIM, 1))


def _gate_col(logits, branch, h, heads_per_step=NSA_HPG):
    idx = branch * N_HEADS + pl.program_id(1) * heads_per_step + h
    lane = lax.broadcasted_iota(jnp.int32, logits.shape, 1)
    return _sigmoid(jnp.sum(jnp.where(lane == idx, logits, 0.0), axis=-1, keepdims=True))


def _tile_iotas(rows, tk):
    return (lax.broadcasted_iota(jnp.int32, (rows, tk), 0),
            lax.broadcasted_iota(jnp.int32, (rows, tk), 1))


VALUE_SUB = 2 * LANES


def _flash_t_scratch(n_heads, tq, tk, qa_rows=2 * LANES):
    return [pltpu.VMEM((n_heads * qa_rows, tq), BF16),
            pltpu.VMEM((n_heads * tk, tq), F32),
            pltpu.VMEM((n_heads * SUBLANES, tq), F32),
            pltpu.VMEM((n_heads * SUBLANES, tq), F32),
            pltpu.VMEM((n_heads * LANES, tq), F32)]


def _flash_t_init(m_ref, acc_ref):
    m_ref[...] = jnp.full(m_ref.shape, -jnp.inf, F32)
    acc_ref[...] = jnp.zeros(acc_ref.shape, F32)


def _flash_t_tile(n_heads, keys_of, vals_of, mask, qa_ref, s_ref, m_ref, alpha_ref, acc_ref):
    aug = qa_ref.shape[0] // n_heads
    tk, tq = s_ref.shape[0] // n_heads, s_ref.shape[1]
    for h in range(n_heads):
        st = _dot(keys_of(h), qa_ref[h * aug:(h + 1) * aug, :])
        if callable(mask):
            st = st + mask()
        elif mask is not None:
            st = jnp.where(mask, st, MASK)
        s_ref[h * tk:(h + 1) * tk, :] = st
        m_prev = m_ref[h * SUBLANES:(h + 1) * SUBLANES, :]
        m_new = jnp.maximum(m_prev, jnp.max(st, axis=0, keepdims=True))
        alpha_ref[h * SUBLANES:(h + 1) * SUBLANES, :] = jnp.exp2(m_prev - m_new)
        m_ref[h * SUBLANES:(h + 1) * SUBLANES, :] = m_new
    for h in range(n_heads):
        vals = vals_of(h)
        for q0 in range(0, tq, VALUE_SUB):
            m = m_ref[h * SUBLANES:h * SUBLANES + 1, q0:q0 + VALUE_SUB]
            part = jnp.zeros((LANES, VALUE_SUB), F32)
            for c0 in range(0, tk, VALUE_SUB):
                pt = jnp.exp2(s_ref[h * tk + c0:h * tk + c0 + VALUE_SUB, q0:q0 + VALUE_SUB] - m).astype(BF16)
                part = part + _dot(vals[:, c0:c0 + VALUE_SUB], pt)
            acc_ref[h * LANES:(h + 1) * LANES, q0:q0 + VALUE_SUB] = (
                alpha_ref[h * SUBLANES:h * SUBLANES + 1, q0:q0 + VALUE_SUB]
                * acc_ref[h * LANES:(h + 1) * LANES, q0:q0 + VALUE_SUB] + part)


def _flash_t_loop(tile, s0, tq, tk):
    last = (s0 + tq - 1) // tk

    def full_tile(ki, carry):
        tile(ki, None)
        return carry

    lax.fori_loop(0, last, full_tile, 0)
    krow, qcol = _tile_iotas(tk, tq)
    tile(last, last * tk + krow <= s0 + qcol)


def _flash_t_out(acc_ref, n_heads, gates):
    def head(h):
        acc = acc_ref[h * LANES:(h + 1) * LANES, :]
        return (acc / acc[HEAD_DIM:HEAD_DIM + 1, :]).T * gates[h]
    return jnp.concatenate([_pair_merge(head(h), head(h + 1)) for h in range(0, n_heads, 2)], axis=1)


def _ones_row_block(tk):
    return jnp.where(lax.broadcasted_iota(jnp.int32, (HEAD_DIM, tk), 0) == 0, 1.0, 0.0).astype(BF16)


def _nsa_slc_t_body(q_ref, k_ref, vt_ref, et_ref, selbt_ref, g_ref, o_ref, qa_ref, s_ref, m_ref, alpha_ref,
                    acc_ref, *, tq, tk):
    s0 = pl.program_id(2) * tq
    n_heads = 2 * NSA_HPG
    _flash_t_init(m_ref, acc_ref)
    q8 = q_ref[0]
    for pair in range(n_heads // 2):
        grp = pair // (NSA_HPG // 2)
        pair_t = q8[:, pair * LANES:(pair + 1) * LANES].astype(F32).T
        for hp in range(2):
            qt = pair_t[hp * HEAD_DIM:(hp + 1) * HEAD_DIM, :].astype(BF16)
            h = 2 * pair + hp
            qa_ref[h * 2 * LANES:(h + 1) * 2 * LANES, :] = jnp.concatenate(
                [selbt_ref[0, grp], _head_kt(qt, grp)], axis=0)
    ones_rows = _ones_row_block(tk)

    def tile(ki, mask):
        k0 = pl.multiple_of(ki * tk, tk)
        keys = jnp.concatenate([et_ref[ki], k_ref[0, pl.ds(k0, tk), :]], axis=1)
        vals_of = lambda h: jnp.concatenate([vt_ref[0, h // NSA_HPG, ki], ones_rows], axis=0)
        _flash_t_tile(n_heads, lambda h: keys, vals_of, mask, qa_ref, s_ref, m_ref, alpha_ref, acc_ref)

    _flash_t_loop(tile, s0, tq, tk)
    gates = [_gate_col(g_ref[0], 1, h, n_heads) for h in range(n_heads)]
    o_ref[0] = _flash_t_out(acc_ref, n_heads, gates).astype(o_ref.dtype)


def _nsa_slc_t(proj3, k3, k_block, vt, et, selbt, g, tq, tk):
    B, S, _ = proj3.shape
    G, HPG = NSA_KV_GROUPS, NSA_HPG
    nk = S // tk
    assert tk == tq and G % 2 == 0 and vt.shape == (B, G, nk, HEAD_DIM, tk) and et.shape == (nk, tk, LANES)
    once = pl.Buffered(1)
    width = 2 * HPG * HEAD_DIM
    return pl.pallas_call(
        functools.partial(_nsa_slc_t_body, tq=tq, tk=tk),
        grid=(B, G // 2, S // tq),
        in_specs=[
            pl.BlockSpec((1, tq, width), lambda b, p, i: (b, i, p)),
            pl.BlockSpec((1, S, LANES), lambda b, p, i: (b, 0, k_block + p), pipeline_mode=once),
            pl.BlockSpec((1, 2, nk, HEAD_DIM, tk), lambda b, p, i: (b, p, 0, 0, 0), pipeline_mode=once),
            pl.BlockSpec((nk, tk, LANES), lambda b, p, i: (0, 0, 0), pipeline_mode=once),
            pl.BlockSpec((1, 2, LANES, tq), lambda b, p, i: (b, p, 0, i)),
            pl.BlockSpec((1, tq, LANES), lambda b, p, i: (b, i, 0)),
        ],
        out_specs=pl.BlockSpec((1, tq, width), lambda b, p, i: (b, i, p)),
        out_shape=jax.ShapeDtypeStruct((B, S, D_MODEL), BF16),
        scratch_shapes=_flash_t_scratch(2 * HPG, tq, tk),
        compiler_params=_cparams(("parallel", "parallel", "parallel")),
    )(proj3, k3, vt, et, selbt, g)


def _nsa_cmp_t_body(q_ref, kc_ref, vot_ref, g_ref, o_ref, selbt_ref, s_ref, m_ref, bias_ref,
                    *, tq, n_cmp, n_sel, top_n):
    s0 = pl.program_id(2) * tq
    kc = kc_ref[0, 0]
    vot = vot_ref[0, 0]
    nc = kc.shape[0]
    nidx = lax.broadcasted_iota(jnp.int32, (nc, tq), 0)
    tpos = s0 + lax.broadcasted_iota(jnp.int32, (nc, tq), 1)
    cmask = (nidx * NSA_CMP_STRIDE + (NSA_CMP_LEN - 1) <= tpos) & (nidx < n_cmp)
    bias_ref[...] = jnp.where(cmask, 0.0, NEG)
    q4 = q_ref[0]
    for pair in range(NSA_HPG // 2):
        pair_t = q4[:, pair * LANES:(pair + 1) * LANES].astype(F32).T
        for hp in range(2):
            h = 2 * pair + hp
            st = _dot(kc, pair_t[hp * HEAD_DIM:(hp + 1) * HEAD_DIM, :].astype(BF16)) + bias_ref[...]
            s_ref[h * nc:(h + 1) * nc, :] = st
            m_ref[h * SUBLANES:(h + 1) * SUBLANES, :] = jnp.broadcast_to(
                jnp.max(st, axis=0, keepdims=True), (SUBLANES, tq))

    imp_t = jnp.zeros((LANES, tq), F32)
    heads = []
    for h in range(NSA_HPG):
        m = m_ref[h * SUBLANES:h * SUBLANES + 1, :]
        res = jnp.zeros((2 * LANES, tq), F32)
        for c0 in range(0, nc, VALUE_SUB):
            et = jnp.exp2(s_ref[h * nc + c0:h * nc + c0 + VALUE_SUB, :] - m).astype(BF16)
            res = res + _dot(vot[:, c0:c0 + VALUE_SUB], et)
        inv = jnp.where(m > 0.5 * NEG, 1.0 / res[HEAD_DIM:HEAD_DIM + 1, :], 0.0)
        heads.append((res[:LANES, :] * inv).T * _gate_col(g_ref[0], 0, h))
        imp_t = imp_t + res[LANES:, :] * inv
    o_ref[0] = jnp.concatenate(
        [_pair_merge(heads[h], heads[h + 1]) for h in range(0, NSA_HPG, 2)], axis=1).astype(o_ref.dtype)

    blk = lax.broadcasted_iota(jnp.int32, imp_t.shape, 0)
    jq = (s0 + lax.broadcasted_iota(jnp.int32, imp_t.shape, 1)) // NSA_SEL_BLOCK
    forced = (blk == 0) | (blk == jq) | (blk == jq - 1)
    cand = jnp.where(forced | (blk >= n_sel), REMOVED, jnp.where(blk > jq, NEG, imp_t))
    sel_t = _topk_mask_t(cand, top_n - N_FORCED) | forced
    selbt_ref[0, 0] = jnp.where(sel_t, 0.0, MASK).astype(selbt_ref.dtype)


def _nsa_cmp_t(proj3, kc, vot, g_cmp, n_cmp, n_sel, top_n, tq=256):
    B, S, _ = proj3.shape
    G, HPG = NSA_KV_GROUPS, NSA_HPG
    NC = kc.shape[2]
    tq = min(tq, S)
    assert NC % VALUE_SUB == 0 and vot.shape == (B, G, 2 * LANES, NC)
    return pl.pallas_call(
        functools.partial(_nsa_cmp_t_body, tq=tq, n_cmp=n_cmp, n_sel=n_sel, top_n=top_n),
        grid=(B, G, S // tq),
        in_specs=[
            pl.BlockSpec((1, tq, HPG * HEAD_DIM), lambda b, g, i: (b, i, g)),
            pl.BlockSpec((1, 1, NC, HEAD_DIM), lambda b, g, i: (b, g, 0, 0)),
            pl.BlockSpec((1, 1, 2 * LANES, NC), lambda b, g, i: (b, g, 0, 0)),
            pl.BlockSpec((1, tq, LANES), lambda b, g, i: (b, i, 0)),
        ],
        out_specs=[
            pl.BlockSpec((1, tq, HPG * HEAD_DIM), lambda b, g, i: (b, i, g)),
            pl.BlockSpec((1, 1, LANES, tq), lambda b, g, i: (b, g, 0, i)),
        ],
        out_shape=[
            jax.ShapeDtypeStruct((B, S, D_MODEL), BF16),
            jax.ShapeDtypeStruct((B, G, LANES, S), BF16),
        ],
        scratch_shapes=[pltpu.VMEM((HPG * NC, tq), F32), pltpu.VMEM((HPG * SUBLANES, tq), F32),
                        pltpu.VMEM((NC, tq), F32)],
        compiler_params=_cparams(("parallel", "parallel", "parallel")),
    )(proj3, kc, vot, g_cmp)


def _nsa_win_t_body(q_ref, k_ref, vt_ref, g_ref, o_ref, qa_ref, s_ref, m_ref, alpha_ref, acc_ref, bias_ref,
                    *, tq, n_tiles):
    s0 = pl.program_id(2) * tq
    tk = n_tiles * tq
    n_heads = 2 * NSA_HPG
    k_first = jnp.maximum(pl.program_id(2) - (n_tiles - 1), 0)
    k0 = pl.multiple_of(k_first * tq, tq)
    _flash_t_init(m_ref, acc_ref)
    q8 = q_ref[0]
    for pair in range(n_heads // 2):
        grp = pair // (NSA_HPG // 2)
        pair_t = q8[:, pair * LANES:(pair + 1) * LANES].astype(F32).T
        for hp in range(2):
            h = 2 * pair + hp
            qa_ref[h * LANES:(h + 1) * LANES, :] = _head_kt(
                pair_t[hp * HEAD_DIM:(hp + 1) * HEAD_DIM, :].astype(BF16), grp)
    krow, qcol = _tile_iotas(tk, tq)
    dist = (s0 - k0) + qcol - krow
    bias_ref[...] = jnp.where((dist >= 0) & (dist < NSA_WINDOW), 0.0, MASK)
    keys_of = lambda h: k_ref[0, pl.ds(k0, tk), :]
    ones_rows = _ones_row_block(tk)
    vals_of = lambda h: jnp.concatenate(
        [jnp.concatenate([vt_ref[0, h // NSA_HPG, k_first + j] for j in range(n_tiles)], axis=1), ones_rows],
        axis=0)
    _flash_t_tile(n_heads, keys_of, vals_of, lambda: bias_ref[...], qa_ref, s_ref, m_ref, alpha_ref, acc_ref)
    gates = [_gate_col(g_ref[0], 2, h, n_heads) for h in range(n_heads)]
    o_ref[0] = _flash_t_out(acc_ref, n_heads, gates).astype(o_ref.dtype)


def _nsa_win_t(proj3, k3, k_block, vt, g, tq):
    B, S, _ = proj3.shape
    G, HPG = NSA_KV_GROUPS, NSA_HPG
    nk = S // tq
    n_tiles = NSA_WINDOW // tq + 1
    assert NSA_WINDOW % tq == 0 and nk >= n_tiles and G % 2 == 0 and vt.shape == (B, G, nk, HEAD_DIM, tq)
    once = pl.Buffered(1)
    width = 2 * HPG * HEAD_DIM
    return pl.pallas_call(
        functools.partial(_nsa_win_t_body, tq=tq, n_tiles=n_tiles),
        grid=(B, G // 2, S // tq),
        in_specs=[
            pl.BlockSpec((1, tq, width), lambda b, p, i: (b, i, p)),
            pl.BlockSpec((1, S, LANES), lambda b, p, i: (b, 0, k_block + p), pipeline_mode=once),
            pl.BlockSpec((1, 2, nk, HEAD_DIM, tq), lambda b, p, i: (b, p, 0, 0, 0), pipeline_mode=once),
            pl.BlockSpec((1, tq, LANES), lambda b, p, i: (b, i, 0)),
        ],
        out_specs=pl.BlockSpec((1, tq, width), lambda b, p, i: (b, i, p)),
        out_shape=jax.ShapeDtypeStruct((B, S, D_MODEL), BF16),
        scratch_shapes=(_flash_t_scratch(2 * HPG, tq, n_tiles * tq, qa_rows=LANES)
                        + [pltpu.VMEM((n_tiles * tq, tq), F32)]),
        compiler_params=_cparams(("parallel", "parallel", "parallel")),
    )(proj3, k3, vt, g)


def _moba_kmean_pairs_body(k_ref, e_ref, hi_ref, lo_ref, *, tk):
    nk = e_ref.shape[0]
    km = jnp.zeros((LANES, LANES), F32)
    for ki in range(nk):
        km = km + _dot(e_ref[ki], k_ref[0, ki * tk:(ki + 1) * tk, :])
    hi, lo = _split_bf16(km * (1.0 / MOBA_BLOCK))
    hi_ref[0, 0] = hi
    lo_ref[0, 0] = lo


def _moba_kmean_pairs(k3, e, tk):
    B, S, N = k3.shape
    n_pairs = N // LANES
    nk = S // tk
    spec = pl.BlockSpec((1, 1, LANES, LANES), lambda b, p: (b, p, 0, 0))
    return pl.pallas_call(
        functools.partial(_moba_kmean_pairs_body, tk=tk),
        grid=(B, n_pairs),
        in_specs=[
            pl.BlockSpec((1, S, LANES), lambda b, p: (b, 0, p)),
            pl.BlockSpec((nk, LANES, tk), lambda b, p: (0, 0, 0)),
        ],
        out_specs=[spec, spec],
        out_shape=[jax.ShapeDtypeStruct((B, n_pairs, LANES, LANES), BF16)] * 2,
        compiler_params=_cparams(("parallel", "parallel")),
    )(k3, e)


def _moba_t_body(qt_ref, k_ref, vt_ref, et_ref, kh_ref, kl_ref, o_ref, qa_ref, s_ref, m_ref, alpha_ref,
                 acc_ref, *, tq, tk, hb, nb, top_k):
    s0 = pl.program_id(2) * tq
    _flash_t_init(m_ref, acc_ref)
    aug = 2 * LANES

    blk = lax.broadcasted_iota(jnp.int32, (LANES, tq), 0)
    cb = (s0 + lax.broadcasted_iota(jnp.int32, (LANES, tq), 1)) // MOBA_BLOCK
    for h in range(hb):
        qpad = _head_kt(qt_ref[0, h, 0], h % 2)
        gsc = _dot(kh_ref[0, h // 2], qpad) + _dot(kl_ref[0, h // 2], qpad)
        gsc = jnp.where(blk < cb, gsc, NEG)
        gsc = jnp.where(blk < nb, gsc, REMOVED)
        sel = (_topk_mask_t(gsc, top_k) & (blk < cb)) | (blk == cb)
        qa_ref[h * aug:(h + 1) * aug, :] = jnp.concatenate(
            [jnp.where(sel, 0.0, MASK).astype(BF16), qpad], axis=0)

    ones_rows = _ones_row_block(tk)

    def tile(ki, mask):
        k0 = pl.multiple_of(ki * tk, tk)
        et = et_ref[ki]
        keys_of = lambda h: jnp.concatenate(
            [et, k_ref[0, pl.ds(k0, tk), (h // 2) * LANES:(h // 2 + 1) * LANES]], axis=1)
        vals_of = lambda h: jnp.concatenate([vt_ref[0, h, ki], ones_rows], axis=0)
        _flash_t_tile(hb, keys_of, vals_of, mask, qa_ref, s_ref, m_ref, alpha_ref, acc_ref)

    _flash_t_loop(tile, s0, tq, tk)
    o_ref[0] = _flash_t_out(acc_ref, hb, [1.0] * hb).astype(o_ref.dtype)


def _moba_attn_t(qt, k3, vt, et, kh, kl, nb, top_k, tq, tk, hb=8):
    B, H, nq, _, _ = qt.shape
    S = k3.shape[1]
    nk = S // tk
    assert tk == tq and tk % MOBA_BLOCK == 0 and H % hb == 0 and hb % 2 == 0 and nb <= LANES
    once = pl.Buffered(1)
    return pl.pallas_call(
        functools.partial(_moba_t_body, tq=tq, tk=tk, hb=hb, nb=nb, top_k=top_k),
        grid=(B, H // hb, nq),
        in_specs=[
            pl.BlockSpec((1, hb, 1, HEAD_DIM, tq), lambda b, h, i: (b, h, i, 0, 0)),
            pl.BlockSpec((1, S, hb * HEAD_DIM), lambda b, h, i: (b, 0, h), pipeline_mode=once),
            pl.BlockSpec((1, hb, nk, HEAD_DIM, tk), lambda b, h, i: (b, h, 0, 0, 0), pipeline_mode=once),
            pl.BlockSpec((nk, tk, LANES), lambda b, h, i: (0, 0, 0), pipeline_mode=once),
            pl.BlockSpec((1, hb // 2, LANES, LANES), lambda b, h, i: (b, h, 0, 0), pipeline_mode=once),
            pl.BlockSpec((1, hb // 2, LANES, LANES), lambda b, h, i: (b, h, 0, 0), pipeline_mode=once),
        ],
        out_specs=pl.BlockSpec((1, tq, hb * HEAD_DIM), lambda b, h, i: (b, i, h)),
        out_shape=jax.ShapeDtypeStruct((B, S, D_MODEL), BF16),
        scratch_shapes=_flash_t_scratch(hb, tq, tk),
        compiler_params=_cparams(("parallel", "parallel", "parallel")),
    )(qt, k3, vt, et, kh, kl)


def _layer_norm(r, g, b):
    mu = jnp.mean(r, axis=-1, keepdims=True)
    c = r - mu
    var = jnp.mean(c * c, axis=-1, keepdims=True)
    return c * lax.rsqrt(var + LN_EPS) * g + b


def _outproj_body(*refs, n_o):
    o_refs = refs[:n_o]
    w_ref, x_ref, g_ref, b_ref, y_ref, yb_ref = refs[n_o:]
    tm = x_ref.shape[0]
    rc = min(PROJ_ROW_CHUNK, tm)
    for r0 in range(0, tm, rc):
        rows = slice(r0, r0 + rc)
        o = o_refs[0][rows, :].astype(F32)
        for ref in o_refs[1:]:
            o = o + ref[rows, :].astype(F32)
        mix = _dot(o.astype(BF16), w_ref[...])
        y = _layer_norm(DN_ALPHA * x_ref[rows, :] + mix, g_ref[...], b_ref[...])
        y_ref[rows, :] = y
        yb_ref[rows, :] = y.astype(BF16)


def _outproj_ln(os_, w, x, g, b, tm=512):
    T, D = x.shape
    tm = min(tm, T)
    n_o = len(os_)
    row = pl.BlockSpec((tm, D), lambda i: (i, 0))
    vec = pl.BlockSpec((1, D), lambda i: (0, 0))
    return pl.pallas_call(
        functools.partial(_outproj_body, n_o=n_o),
        grid=(T // tm,),
        in_specs=[row] * n_o + [pl.BlockSpec((D, D), lambda i: (0, 0)), row, vec, vec],
        out_specs=[row, row],
        out_shape=[jax.ShapeDtypeStruct((T, D), F32), jax.ShapeDtypeStruct((T, D), BF16)],
        compiler_params=_cparams(("parallel",)),
    )(*os_, w, x, g.reshape(1, D), b.reshape(1, D))


GID_LANE = N_EXPERTS
MOE_WINDOW = 1024
MOE_CHUNK = 128
MOE_VMEM_LIMIT = 60 * 1024 * 1024


def _router_sorted_body(x_ref, wh_ref, wl_ref, bias_ref, gate_ref, gidt_ref, cnt_ref):
    x_hi, x_lo = _split_bf16(x_ref[...])
    wh = wh_ref[...]
    logits = _dot(x_hi, wh) + _dot(x_lo, wh) + _dot(x_hi, wl_ref[...])
    scores = _sigmoid(logits)
    score_t = scores.T
    biased_t = (scores + bias_ref[...]).T
    tm = score_t.shape[1]
    s_rows = [score_t[e:e + 1, :] for e in range(N_EXPERTS)]
    b_rows = [biased_t[e:e + 1, :] for e in range(N_EXPERTS)]

    def first_max(vals, live):
        m = None
        for v, ok in zip(vals, live):
            cand = jnp.where(ok > 0.5, v, REMOVED)
            m = cand if m is None else jnp.maximum(m, cand)
        hits, found = [], jnp.zeros_like(m)
        for v, ok in zip(vals, live):
            hit = jnp.where((v == m) & (ok > 0.5) & (found < 0.5), 1.0, 0.0)
            found = found + hit
            hits.append(hit)
        return m, hits

    ones = jnp.ones((1, tm), F32)
    best, gid, sels = None, None, []
    for grp in range(N_GROUPS):
        vals = b_rows[grp * EXPERTS_PER_GROUP:(grp + 1) * EXPERTS_PER_GROUP]
        m1, h1 = first_max(vals, [ones] * EXPERTS_PER_GROUP)
        m2, h2 = first_max(vals, [1.0 - h for h in h1])
        sels.append([a + b for a, b in zip(h1, h2)])
        score = m1 + m2
        if grp == 0:
            best, gid = score, jnp.zeros_like(score)
        else:
            better = score > best
            best = jnp.where(better, score, best)
            gid = jnp.where(better, float(grp), gid)
    w_rows = []
    for e in range(N_EXPERTS):
        grp, j = divmod(e, EXPERTS_PER_GROUP)
        w_rows.append(jnp.where((gid == float(grp)) & (sels[grp][j] > 0.5), s_rows[e], 0.0))
    total = w_rows[0]
    for w in w_rows[1:]:
        total = total + w
    rows_out = 3 * SUBLANES
    row = lax.broadcasted_iota(jnp.int32, (rows_out, tm), 0)
    gate_t = jnp.where(row == GID_LANE, gid, 0.0)
    for e in range(N_EXPERTS):
        gate_t = jnp.where(row == e, w_rows[e] / total, gate_t)
    gate_t = jnp.concatenate([gate_t, jnp.zeros((LANES - rows_out, tm), F32)], axis=0)
    gate_ref[...] = gate_t.T
    gidt_ref[...] = jnp.broadcast_to(gid, (SUBLANES, tm))
    lane = lax.broadcasted_iota(jnp.int32, (SUBLANES, LANES), 1)
    counts = jnp.zeros((SUBLANES, LANES), F32)
    for grp in range(N_GROUPS):
        n = jnp.sum(jnp.where(gid == float(grp), 1.0, 0.0), axis=-1, keepdims=True)
        counts = jnp.where(lane == grp, n, counts)
    cnt_ref[0] = counts


def _router_sorted(x, router_w, router_bias, tm):
    T, D = x.shape
    wpad = jnp.zeros((D, LANES), F32).at[:, :N_EXPERTS].set(router_w)
    wh, wl = _split_bf16(wpad)
    bpad = jnp.zeros((1, LANES), F32).at[0, :N_EXPERTS].set(router_bias)
    return pl.pallas_call(
        _router_sorted_body,
        grid=(T // tm,),
        in_specs=[
            pl.BlockSpec((tm, D), lambda i: (i, 0)),
            pl.BlockSpec((D, LANES), lambda i: (0, 0)),
            pl.BlockSpec((D, LANES), lambda i: (0, 0)),
            pl.BlockSpec((1, LANES), lambda i: (0, 0)),
        ],
        out_specs=[
            pl.BlockSpec((tm, LANES), lambda i: (i, 0)),
            pl.BlockSpec((SUBLANES, tm), lambda i: (0, i)),
            pl.BlockSpec((1, SUBLANES, LANES), lambda i: (i, 0, 0)),
        ],
        out_shape=[
            jax.ShapeDtypeStruct((T, LANES), F32),
            jax.ShapeDtypeStruct((SUBLANES, T), F32),
            jax.ShapeDtypeStruct((T // tm, SUBLANES, LANES), F32),
        ],
        compiler_params=_cparams(("parallel",)),
    )(x, wh, wl, bpad)


def _experts_sorted_body(cnt_ref, xb_ref, x_ref, gate_ref, gidt_ref, ltri_ref, utri_ref, wg_ref, wu_ref,
                         wd_ref, g_ref, b_ref, y_ref, yb_ref, xs_ref, gs_ref, acc_ref, rank_ref,
                         *, W, Wp, chunk):
    win = pl.program_id(0)
    e = pl.program_id(1)
    grp = e // EXPERTS_PER_GROUP
    padded = [((cnt_ref[win * N_GROUPS + g] + chunk - 1) // chunk) * chunk for g in range(N_GROUPS)]
    starts = [0]
    for g in range(N_GROUPS - 1):
        starts.append(starts[-1] + padded[g])
    start = starts[0]
    for g in range(1, N_GROUPS):
        start = jnp.where(grp == g, starts[g], start)
    n_chunks = (cnt_ref[win * N_GROUPS + grp] + chunk - 1) // chunk

    @pl.when(e == 0)
    def _():
        gate = gate_ref[...]
        lane = lax.broadcasted_iota(jnp.int32, gate.shape, 1)
        lanef = lane.astype(F32)
        gid = jnp.sum(jnp.where(lane == GID_LANE, gate, 0.0), axis=-1, keepdims=True)
        member = jnp.where((lanef == gid) & (lane < N_GROUPS), 1.0, 0.0)
        earlier = _dot(ltri_ref[...], member.astype(BF16))
        first = jnp.zeros(gate.shape, F32)
        for g in range(1, N_GROUPS):
            first = jnp.where(lane == g, starts[g].astype(F32), first)
        rank = jnp.sum(member * (first + earlier), axis=-1, keepdims=True)
        rank_ref[...] = jnp.broadcast_to(rank, gate.shape)

        gid_r = gidt_ref[...]
        sub = lax.broadcasted_iota(jnp.int32, gid_r.shape, 0)
        member_r = jnp.where(sub.astype(F32) == gid_r, 1.0, 0.0)
        earlier_r = _dot(member_r.astype(BF16), utri_ref[...])
        first_r = jnp.zeros(gid_r.shape, F32)
        for g in range(1, N_GROUPS):
            first_r = jnp.where(sub == g, starts[g].astype(F32), first_r)
        rank_r = jnp.sum(member_r * (first_r + earlier_r), axis=0, keepdims=True)
        rows = lax.broadcasted_iota(jnp.int32, (Wp, W), 0).astype(F32)
        perm = jnp.where(rows == rank_r, 1.0, 0.0).astype(BF16)
        xs_ref[...] = _dot(perm, xb_ref[...]).astype(BF16)
        g_hi, g_lo = _split_bf16(gate)
        gs_ref[...] = _dot(perm, g_hi) + _dot(perm, g_lo)
        acc_ref[...] = jnp.zeros(acc_ref.shape, F32)

    def expert_rows(r0, rows):
        r0 = pl.multiple_of(r0, chunk)
        xc = xs_ref[pl.ds(r0, rows), :]
        a = _dot(xc, wg_ref[0])
        u = _dot(xc, wu_ref[0])
        gs = gs_ref[pl.ds(r0, rows), :]
        lane = lax.broadcasted_iota(jnp.int32, gs.shape, 1)
        gcol = jnp.sum(jnp.where(lane == e, gs, 0.0), axis=-1, keepdims=True)
        h = a * _sigmoid(a) * u * gcol
        acc_ref[pl.ds(r0, rows), :] += _dot(h.astype(BF16), wd_ref[0])

    def three_chunks(i, carry):
        expert_rows(start + i * (3 * chunk), 3 * chunk)
        return carry

    lax.fori_loop(0, n_chunks // 3, three_chunks, 0)
    tail = start + (n_chunks // 3) * (3 * chunk)
    for left in (1, 2):
        pl.when(n_chunks % 3 == left)(functools.partial(expert_rows, tail, left * chunk))

    @pl.when(e == N_EXPERTS - 1)
    def _():
        cols = lax.broadcasted_iota(jnp.int32, (W, Wp), 1).astype(F32)
        unperm = jnp.where(cols == rank_ref[...][:, :1], 1.0, 0.0).astype(BF16)
        ffn = _dot(unperm, acc_ref[...].astype(BF16))
        y = _layer_norm(DN_ALPHA * x_ref[...] + ffn, g_ref[...], b_ref[...])
        y_ref[...] = y
        yb_ref[...] = y.astype(BF16)


def _moe_ln(xb, x, router_w, router_bias, wg, wu, wd, g, b):
    T, D = x.shape
    W = min(MOE_WINDOW, T)
    chunk = MOE_CHUNK
    Wp = W + N_GROUPS * chunk
    E, _, DE = wg.shape
    assert T % W == 0 and W % chunk == 0
    gate, gidt, cnt = _router_sorted(x, router_w, router_bias, W)
    counts = cnt[:, 0, :N_GROUPS].astype(jnp.int32).reshape(-1)
    t = np.arange(W)
    ltri = jnp.asarray((t[None, :] < t[:, None]).astype(np.float32), BF16)
    once = pl.Buffered(1)
    row = lambda shape: pl.BlockSpec(shape, lambda i, e, c: (i, 0))
    vec = pl.BlockSpec((1, D), lambda i, e, c: (0, 0))
    tri = pl.BlockSpec((W, W), lambda i, e, c: (0, 0), pipeline_mode=once)
    grid_spec = pltpu.PrefetchScalarGridSpec(
        num_scalar_prefetch=1,
        grid=(T // W, E),
        in_specs=[
            row((W, D)),
            pl.BlockSpec((W, D), lambda i, e, c: (i, 0), pipeline_mode=once),
            row((W, LANES)),
            pl.BlockSpec((SUBLANES, W), lambda i, e, c: (0, i)),
            tri, tri,
            pl.BlockSpec((1, D, DE), lambda i, e, c: (e, 0, 0)),
            pl.BlockSpec((1, D, DE), lambda i, e, c: (e, 0, 0)),
            pl.BlockSpec((1, DE, D), lambda i, e, c: (e, 0, 0)),
            vec, vec,
        ],
        out_specs=[row((W, D)), row((W, D))],
        scratch_shapes=[
            pltpu.VMEM((Wp, D), BF16),
            pltpu.VMEM((Wp, LANES), F32),
            pltpu.VMEM((Wp, D), F32),
            pltpu.VMEM((W, LANES), F32),
        ],
    )
    return pl.pallas_call(
        functools.partial(_experts_sorted_body, W=W, Wp=Wp, chunk=chunk),
        grid_spec=grid_spec,
        out_shape=[jax.ShapeDtypeStruct((T, D), F32), jax.ShapeDtypeStruct((T, D), BF16)],
        compiler_params=pltpu.CompilerParams(dimension_semantics=("parallel", "arbitrary"),
                                             vmem_limit_bytes=MOE_VMEM_LIMIT),
    )(counts, xb, x, gate, gidt, ltri, ltri.T, wg, wu, wd, g.reshape(1, D), b.reshape(1, D))


def _block_onehots(S, block, tk):
    key = np.arange(S).reshape(S // tk, 1, tk)
    r = np.arange(LANES).reshape(1, LANES, 1)
    return jnp.asarray((key // block == r).astype(np.float32), BF16)


def _rope_tiled(S):
    cos, sin = _rope_tables(jnp.arange(S))
    reps = LANES // HALF
    return jnp.tile(cos, (1, reps)), jnp.tile(sin, (1, reps))


def _nsa_mixer(xb, B, S, w_in, cmp_k_w1, cmp_k_w2, cmp_v_w1, cmp_v_w2, cmp_k_pos, cmp_v_pos):
    G, HPG, KV = NSA_KV_GROUPS, NSA_HPG, NSA_KV_DIM
    L, STR, SB = NSA_CMP_LEN, NSA_CMP_STRIDE, NSA_SEL_BLOCK
    assert L == 2 * STR and S % SB == 0 and S // SB <= LANES
    T = B * S
    n_cmp = (S - L) // STR + 1
    NC = S // STR
    n_sel = S // SB
    top_n = min(NSA_SEL_TOPN, n_sel)
    assert top_n >= N_FORCED

    cos2, sin2 = _rope_tiled(S)
    wb = w_in.astype(BF16)
    wcol = lambda i: wb[:, D_MODEL + i * KV: D_MODEL + (i + 1) * KV]
    tn = 2 * KV
    tk_s = min(512, S)
    tq_w = NSA_WINDOW // 2
    proj = _proj(xb, wb[:, :D_MODEL + 2 * KV], cos2, sin2, [2] * (D_MODEL // tn) + [0], S, tn=tn)
    kk3 = _proj(xb, jnp.concatenate([wcol(2), wcol(4)], axis=1), cos2, sin2, [1], S, tn=tn).reshape(B, S, tn)
    vt_s = _proj_kt(xb, wcol(3), cos2, sin2, B, S, tk_s, rope=False)
    vt_w = _proj_kt(xb, wcol(5), cos2, sin2, B, S, tq_w, rope=False)
    wg = jnp.zeros((D_MODEL, LANES), BF16).at[:, :3 * N_HEADS].set(wb[:, D_MODEL + 6 * KV:])
    gates = _proj(xb, wg, cos2, sin2, [0], S, out_dtype=F32, tn=LANES).reshape(B, S, LANES)

    col = lambda i: proj[:, D_MODEL + i * KV: D_MODEL + (i + 1) * KV]
    proj3 = proj.reshape(B, S, proj.shape[1])

    ccos, csin = _rope_tables(jnp.arange(NC) * STR + (L - 1))
    ccos = jnp.concatenate([ccos, ccos], axis=1)
    csin = jnp.concatenate([csin, csin], axis=1)
    to_rows = lambda t: t.reshape(B, S, G, HEAD_DIM).transpose(0, 2, 1, 3).reshape(B * G, NC, STR * HEAD_DIM)
    kc = _compress(to_rows(col(0)), cmp_k_w1, cmp_k_pos, cmp_k_w2, ccos, csin, True, n_cmp)
    vc = _compress(to_rows(col(1)), cmp_v_w1, cmp_v_pos, cmp_v_w2, ccos, csin, False, n_cmp)

    ci = np.arange(NC)[:, None]
    sj = np.arange(LANES)[None, :]
    overlap = ((ci * STR < (sj + 1) * SB) & (ci * STR + L > sj * SB) & (ci < n_cmp) & (sj < n_sel))
    overlap = jnp.broadcast_to(jnp.asarray(overlap.astype(np.float32), BF16), (B, G, NC, LANES))
    vo = jnp.concatenate([vc.reshape(B, G, NC, HEAD_DIM), jnp.ones((B, G, NC, 1), BF16),
                          jnp.zeros((B, G, NC, LANES - HEAD_DIM - 1), BF16), overlap], axis=-1)

    o_cmp, selbt = _nsa_cmp_t(proj3, kc.reshape(B, G, NC, HEAD_DIM), vo.transpose(0, 1, 3, 2), gates,
                              n_cmp, n_sel, top_n)
    o_slc = _nsa_slc_t(proj3, kk3, 0, vt_s, _block_onehots(S, SB, tk_s).transpose(0, 2, 1), selbt, gates,
                       tq=tk_s, tk=tk_s)
    o_win = _nsa_win_t(proj3, kk3, G // 2, vt_w, gates, tq=tq_w)
    return [o.reshape(T, D_MODEL) for o in (o_cmp, o_slc, o_win)]


def _moba_mixer(xb, B, S, w_in):
    H = N_HEADS
    nb = S // MOBA_BLOCK
    top_k = min(MOBA_TOPK, nb)
    cos2, sin2 = _rope_tiled(S)
    tn = 512
    n_t = D_MODEL // tn
    wb = w_in.astype(BF16)
    tk = min(2 * MOBA_BLOCK, S)
    qt = _proj_kt(xb, wb[:, :D_MODEL], cos2, sin2, B, S, tk, tn=tn, scale=Q_SCALE_LOG2)
    k3 = _proj(xb, wb[:, D_MODEL:2 * D_MODEL], cos2, sin2, [1] * n_t, S, tn=tn).reshape(B, S, D_MODEL)
    vt = _proj_kt(xb, wb[:, 2 * D_MODEL:], cos2, sin2, B, S, tk, tn=tn, rope=False)
    e = _block_onehots(S, MOBA_BLOCK, tk)
    kh, kl = _moba_kmean_pairs(k3, e, tk)
    o = _moba_attn_t(qt, k3, vt, e.transpose(0, 2, 1), kh, kl, nb, top_k, tq=tk, tk=tk)
    return [o.reshape(B * S, D_MODEL)]


def kernel(x, nsa_w_in, nsa_w_out, nsa_cmp_k_w1, nsa_cmp_k_w2, nsa_cmp_v_w1, nsa_cmp_v_w2, nsa_cmp_k_pos, nsa_cmp_v_pos, moba_w_in, moba_w_out, router_w, router_bias, moe_w_gate, moe_w_up, moe_w_down, ln_g, ln_b):
    B, S, D = x.shape
    xf = x.reshape(B * S, D)
    xb = xf.astype(BF16)
    for layer in range(DEPTH):
        j = layer // 2
        if layer % 2 == 0:
            os_ = _nsa_mixer(xb, B, S, nsa_w_in[j], nsa_cmp_k_w1[j], nsa_cmp_k_w2[j], nsa_cmp_v_w1[j],
                             nsa_cmp_v_w2[j], nsa_cmp_k_pos[j], nsa_cmp_v_pos[j])
            w_out = nsa_w_out[j]
        else:
            os_ = _moba_mixer(xb, B, S, moba_w_in[j])
            w_out = moba_w_out[j]
        xf, xb = _outproj_ln(os_, w_out.astype(BF16), xf, ln_g[layer, 0], ln_b[layer, 0])
        xf, xb = _moe_ln(xb, xf, router_w, router_bias, moe_w_gate[layer].astype(BF16),
                         moe_w_up[layer].astype(BF16), moe_w_down[layer].astype(BF16),
                         ln_g[layer, 1], ln_b[layer, 1])
    return xf.reshape(B, S, D)
```

```python
import functools

import jax
import jax.numpy as jnp
import numpy as np
from jax import lax
from jax.experimental import pallas as pl
from jax.experimental.pallas import tpu as pltpu

F32 = jnp.float32
BF16 = jnp.bfloat16

D_MODEL = 1024
N_HEADS = 16
HEAD_DIM = 64
HALF = HEAD_DIM // 2
ROPE_THETA = 10000.0
DEPTH = 2
DN_ALPHA = (2 * DEPTH) ** 0.25
LN_EPS = 1e-5
NEG = -1e30
N_FORCED = 3
MASK = -1e30
REMOVED = -3.0e38
LANES = 128
SUBLANES = 8
Q_SCALE_LOG2 = float(HEAD_DIM ** -0.5 * np.log2(np.e))
PROJ_ROW_CHUNK = 256

NSA_KV_GROUPS = 4
NSA_HPG = N_HEADS // NSA_KV_GROUPS
NSA_KV_DIM = NSA_KV_GROUPS * HEAD_DIM
NSA_CMP_LEN = 32
NSA_CMP_STRIDE = 16
NSA_SEL_BLOCK = 64
NSA_SEL_TOPN = 16
NSA_WINDOW = 512

MOBA_BLOCK = 256
MOBA_TOPK = 3

N_EXPERTS = 16
N_GROUPS = 4
EXPERTS_PER_GROUP = N_EXPERTS // N_GROUPS

VMEM_LIMIT = 48 * 1024 * 1024


def _cparams(sem):
    return pltpu.CompilerParams(dimension_semantics=sem, vmem_limit_bytes=VMEM_LIMIT)


def _dot(a, b):
    return jnp.dot(a, b, preferred_element_type=F32)


def _split_bf16(x):
    hi = x.astype(BF16)
    lo = (x - hi.astype(F32)).astype(BF16)
    return hi, lo


def _sigmoid(x):
    return 1.0 / (1.0 + jnp.exp(-x))


def _proj_body(mode_ref, x_ref, w_ref, cos_ref, sin_ref, o_ref, *, tn):
    mode = mode_ref[pl.program_id(0)]
    tm = x_ref.shape[0]
    rc = min(PROJ_ROW_CHUNK, tm)

    @pl.when(mode == 0)
    def _():
        for r in range(0, tm, rc):
            o_ref[r:r + rc, :] = _dot(x_ref[r:r + rc, :], w_ref[...]).astype(o_ref.dtype)

    @pl.when(mode != 0)
    def _():
        sc = jnp.where(mode == 2, Q_SCALE_LOG2, 1.0).astype(F32)
        for r in range(0, tm, rc):
            acc = _dot(x_ref[r:r + rc, :], w_ref[...])
            cos = cos_ref[r:r + rc, :] * sc
            sin = sin_ref[r:r + rc, :] * sc
            for c in range(tn // LANES):
                o_ref[r:r + rc, c * LANES:(c + 1) * LANES] = (
                    _rope_chunk(acc[:, c * LANES:(c + 1) * LANES], cos, sin).astype(o_ref.dtype))


def _proj(xb, w, cos2, sin2, modes, seq, out_dtype=BF16, tm=1024, tn=512):
    T, K = xb.shape
    N = w.shape[1]
    tm = min(tm, seq)
    assert T % tm == 0 and N % tn == 0 and seq % tm == 0 and len(modes) == N // tn
    n_pos = seq // tm
    grid_spec = pltpu.PrefetchScalarGridSpec(
        num_scalar_prefetch=1,
        grid=(N // tn, T // tm),
        in_specs=[
            pl.BlockSpec((tm, K), lambda j, i, m: (i, 0)),
            pl.BlockSpec((K, tn), lambda j, i, m: (0, j)),
            pl.BlockSpec((tm, LANES), lambda j, i, m: (i % n_pos, 0)),
            pl.BlockSpec((tm, LANES), lambda j, i, m: (i % n_pos, 0)),
        ],
        out_specs=pl.BlockSpec((tm, tn), lambda j, i, m: (i, j)),
    )
    return pl.pallas_call(
        functools.partial(_proj_body, tn=tn),
        grid_spec=grid_spec,
        out_shape=jax.ShapeDtypeStruct((T, N), out_dtype),
        compiler_params=_cparams(("parallel", "parallel")),
    )(jnp.asarray(modes, jnp.int32), xb, w, cos2, sin2)


def _rope_tables(pos):
    inv = 1.0 / (ROPE_THETA ** (jnp.arange(0, HEAD_DIM, 2, dtype=F32) / HEAD_DIM))
    ang = pos.astype(F32)[:, None] * inv[None, :]
    return jnp.cos(ang), jnp.sin(ang)


def _rope_chunk(a, cos, sin):
    lane = lax.broadcasted_iota(jnp.int32, a.shape, 1)
    up = pltpu.roll(a, LANES - HALF, 1)
    dn = pltpu.roll(a, HALF, 1)
    return a * cos + jnp.where((lane % HEAD_DIM) < HALF, -up, dn) * sin


def _proj_kt_body(x_ref, w_ref, cos_ref, sin_ref, o_ref, *, tn, tk, rope, scale):
    tm = x_ref.shape[0]
    rc = min(PROJ_ROW_CHUNK, tk)
    for r in range(0, tm, rc):
        acc = _dot(x_ref[r:r + rc, :], w_ref[...])
        cos = cos_ref[r:r + rc, :] * scale
        sin = sin_ref[r:r + rc, :] * scale
        for c in range(tn // LANES):
            a = acc[:, c * LANES:(c + 1) * LANES]
            kt = (_rope_chunk(a, cos, sin) if rope else a).T
            for hh in range(2):
                o_ref[0, 2 * c + hh, r // tk, :, r % tk:r % tk + rc] = (
                    kt[hh * HEAD_DIM:(hh + 1) * HEAD_DIM, :].astype(o_ref.dtype))


def _proj_kt(xb, w, cos2, sin2, B, S, tk, tm=1024, tn=256, rope=True, scale=1.0):
    T, K = xb.shape
    N = w.shape[1]
    tm = min(tm, S)
    tn = min(tn, N)
    assert S % tm == 0 and tm % tk == 0 and N % tn == 0
    n_pos = S // tm
    return pl.pallas_call(
        functools.partial(_proj_kt_body, tn=tn, tk=tk, rope=rope, scale=scale),
        grid=(N // tn, T // tm),
        in_specs=[
            pl.BlockSpec((tm, K), lambda j, i: (i, 0)),
            pl.BlockSpec((K, tn), lambda j, i: (0, j)),
            pl.BlockSpec((tm, LANES), lambda j, i: (i % n_pos, 0)),
            pl.BlockSpec((tm, LANES), lambda j, i: (i % n_pos, 0)),
        ],
        out_specs=pl.BlockSpec((1, tn // HEAD_DIM, tm // tk, HEAD_DIM, tk),
                               lambda j, i: (i // n_pos, j, i % n_pos, 0, 0)),
        out_shape=jax.ShapeDtypeStruct((B, N // HEAD_DIM, S // tk, HEAD_DIM, tk), BF16),
        compiler_params=_cparams(("parallel", "parallel")),
    )(xb, w, cos2, sin2)


def _gelu_tanh(x):
    c = np.float32(np.sqrt(2.0 / np.pi))
    return 0.5 * x * (1.0 + jnp.tanh(c * (x + 0.044715 * (x * x * x))))


def _compress_body(r_ref, w1_ref, pos_ref, w2_ref, w2r_ref, cos_ref, sin_ref, o_ref, *, rope, n_cmp):
    r = r_ref[0]
    nc = r.shape[0]
    half = NSA_CMP_STRIDE * HEAD_DIM
    a = _dot(r, w1_ref[0])
    b = _dot(r, w1_ref[1])
    pos = pos_ref[...]
    pb = _dot(pos[:, :half], w1_ref[0]) + _dot(pos[:, half:], w1_ref[1])
    b_next = pltpu.roll(b, nc - 1, 0)
    h = _gelu_tanh(a + b_next + pb[0:1, :]).astype(BF16)
    o = _dot(h, w2_ref[...])
    if rope:
        o = o * cos_ref[...] + _dot(h, w2r_ref[...]) * sin_ref[...]
    row = lax.broadcasted_iota(jnp.int32, o.shape, 0)
    o_ref[0] = jnp.where(row < n_cmp, o, 0.0).astype(o_ref.dtype)


def _compress(r, w1, pos, w2, cos_c, sin_c, rope, n_cmp):
    BG, NC, K = r.shape
    hidden = w1.shape[1]
    w1s = w1.astype(BF16).reshape(2, K, hidden)
    pos8 = jnp.zeros((SUBLANES, 2 * K), BF16).at[0].set(pos.reshape(-1).astype(BF16))
    w2r = jnp.concatenate([-w2[:, HALF:], w2[:, :HALF]], axis=1).astype(BF16)
    full = lambda shape: pl.BlockSpec(shape, lambda i: (0,) * len(shape))
    return pl.pallas_call(
        functools.partial(_compress_body, rope=rope, n_cmp=n_cmp),
        grid=(BG,),
        in_specs=[
            pl.BlockSpec((1, NC, K), lambda i: (i, 0, 0)),
            full((2, K, hidden)),
            full((SUBLANES, 2 * K)),
            full((hidden, HEAD_DIM)),
            full((hidden, HEAD_DIM)),
            full((NC, HEAD_DIM)),
            full((NC, HEAD_DIM)),
        ],
        out_specs=pl.BlockSpec((1, NC, HEAD_DIM), lambda i: (i, 0, 0)),
        out_shape=jax.ShapeDtypeStruct((BG, NC, HEAD_DIM), BF16),
        compiler_params=_cparams(("parallel",)),
    )(r, w1s, pos8, w2.astype(BF16), w2r, cos_c, sin_c)


def _topk_mask_t(v, k):
    idx = lax.broadcasted_iota(jnp.int32, v.shape, 0).astype(F32)

    def step(_, cur):
        m = jnp.max(cur, axis=0, keepdims=True)
        first = jnp.min(jnp.where(cur == m, idx, float(LANES)), axis=0, keepdims=True)
        return jnp.where(idx == first, REMOVED, cur)

    return lax.fori_loop(0, k, step, v, unroll=True) != v


def _head_kt(t, parity):
    z = jnp.zeros_like(t)
    return jnp.concatenate([t, z] if parity == 0 else [z, t], axis=0)


def _pair_merge(even, odd):
    lane = lax.broadcasted_iota(jnp.int32, even.shape, 1)
    return jnp.where(lane < HEAD_DIM, even, pltpu.roll(odd, HEAD_DIM, 1))


def _gate_col(logits, branch, h, heads_per_step=NSA_HPG):
    idx = branch * N_HEADS + pl.program_id(1) * heads_per_step + h
    lane = lax.broadcasted_iota(jnp.int32, logits.shape, 1)
    return _sigmoid(jnp.sum(jnp.where(lane == idx, logits, 0.0), axis=-1, keepdims=True))


def _tile_iotas(rows, tk):
    return (lax.broadcasted_iota(jnp.int32, (rows, tk), 0),
            lax.broadcasted_iota(jnp.int32, (rows, tk), 1))


VALUE_SUB = 2 * LANES


def _flash_t_scratch(n_heads, tq, tk, qa_rows=2 * LANES):
    return [pltpu.VMEM((n_heads * qa_rows, tq), BF16),
            pltpu.VMEM((n_heads * tk, tq), F32),
            pltpu.VMEM((n_heads * SUBLANES, tq), F32),
            pltpu.VMEM((n_heads * SUBLANES, tq), F32),
            pltpu.VMEM((n_heads * LANES, tq), F32)]


def _flash_t_init(m_ref, acc_ref):
    m_ref[...] = jnp.full(m_ref.shape, -jnp.inf, F32)
    acc_ref[...] = jnp.zeros(acc_ref.shape, F32)


def _flash_t_tile(n_heads, keys_of, vals_of, mask, qa_ref, s_ref, m_ref, alpha_ref, acc_ref):
    aug = qa_ref.shape[0] // n_heads
    tk, tq = s_ref.shape[0] // n_heads, s_ref.shape[1]
    for h in range(n_heads):
        st = _dot(keys_of(h), qa_ref[h * aug:(h + 1) * aug, :])
        if callable(mask):
            st = st + mask()
        elif mask is not None:
            st = jnp.where(mask, st, MASK)
        s_ref[h * tk:(h + 1) * tk, :] = st
        m_prev = m_ref[h * SUBLANES:(h + 1) * SUBLANES, :]
        m_new = jnp.maximum(m_prev, jnp.max(st, axis=0, keepdims=True))
        alpha_ref[h * SUBLANES:(h + 1) * SUBLANES, :] = jnp.exp2(m_prev - m_new)
        m_ref[h * SUBLANES:(h + 1) * SUBLANES, :] = m_new
    for h in range(n_heads):
        vals = vals_of(h)
        for q0 in range(0, tq, VALUE_SUB):
            m = m_ref[h * SUBLANES:h * SUBLANES + 1, q0:q0 + VALUE_SUB]
            part = jnp.zeros((LANES, VALUE_SUB), F32)
            for c0 in range(0, tk, VALUE_SUB):
                pt = jnp.exp2(s_ref[h * tk + c0:h * tk + c0 + VALUE_SUB, q0:q0 + VALUE_SUB] - m).astype(BF16)
                part = part + _dot(vals[:, c0:c0 + VALUE_SUB], pt)
            acc_ref[h * LANES:(h + 1) * LANES, q0:q0 + VALUE_SUB] = (
                alpha_ref[h * SUBLANES:h * SUBLANES + 1, q0:q0 + VALUE_SUB]
                * acc_ref[h * LANES:(h + 1) * LANES, q0:q0 + VALUE_SUB] + part)


def _flash_t_loop(tile, s0, tq, tk):
    last = (s0 + tq - 1) // tk

    def full_tile(ki, carry):
        tile(ki, None)
        return carry

    lax.fori_loop(0, last, full_tile, 0)
    krow, qcol = _tile_iotas(tk, tq)
    tile(last, last * tk + krow <= s0 + qcol)


def _flash_t_out(acc_ref, n_heads, gates):
    def head(h):
        acc = acc_ref[h * LANES:(h + 1) * LANES, :]
        return (acc / acc[HEAD_DIM:HEAD_DIM + 1, :]).T * gates[h]
    return jnp.concatenate([_pair_merge(head(h), head(h + 1)) for h in range(0, n_heads, 2)], axis=1)


def _ones_row_block(tk):
    return jnp.where(lax.broadcasted_iota(jnp.int32, (HEAD_DIM, tk), 0) == 0, 1.0, 0.0).astype(BF16)


def _nsa_slc_t_body(q_ref, k_ref, vt_ref, et_ref, selbt_ref, g_ref, o_ref, qa_ref, s_ref, m_ref, alpha_ref,
                    acc_ref, *, tq, tk):
    s0 = pl.program_id(2) * tq
    n_heads = 2 * NSA_HPG
    _flash_t_init(m_ref, acc_ref)
    q8 = q_ref[0]
    for pair in range(n_heads // 2):
        grp = pair // (NSA_HPG // 2)
        pair_t = q8[:, pair * LANES:(pair + 1) * LANES].astype(F32).T
        for hp in range(2):
            qt = pair_t[hp * HEAD_DIM:(hp + 1) * HEAD_DIM, :].astype(BF16)
            h = 2 * pair + hp
            qa_ref[h * 2 * LANES:(h + 1) * 2 * LANES, :] = jnp.concatenate(
                [selbt_ref[0, grp], _head_kt(qt, grp)], axis=0)
    ones_rows = _ones_row_block(tk)

    def tile(ki, mask):
        k0 = pl.multiple_of(ki * tk, tk)
        keys = jnp.concatenate([et_ref[ki], k_ref[0, pl.ds(k0, tk), :]], axis=1)
        vals_of = lambda h: jnp.concatenate([vt_ref[0, h // NSA_HPG, ki], ones_rows], axis=0)
        _flash_t_tile(n_heads, lambda h: keys, vals_of, mask, qa_ref, s_ref, m_ref, alpha_ref, acc_ref)

    _flash_t_loop(tile, s0, tq, tk)
    gates = [_gate_col(g_ref[0], 1, h, n_heads) for h in range(n_heads)]
    o_ref[0] = _flash_t_out(acc_ref, n_heads, gates).astype(o_ref.dtype)


def _nsa_slc_t(proj3, k3, k_block, vt, et, selbt, g, tq, tk):
    B, S, _ = proj3.shape
    G, HPG = NSA_KV_GROUPS, NSA_HPG
    nk = S // tk
    assert tk == tq and G % 2 == 0 and vt.shape == (B, G, nk, HEAD_DIM, tk) and et.shape == (nk, tk, LANES)
    once = pl.Buffered(1)
    width = 2 * HPG * HEAD_DIM
    return pl.pallas_call(
        functools.partial(_nsa_slc_t_body, tq=tq, tk=tk),
        grid=(B, G // 2, S // tq),
        in_specs=[
            pl.BlockSpec((1, tq, width), lambda b, p, i: (b, i, p)),
            pl.BlockSpec((1, S, LANES), lambda b, p, i: (b, 0, k_block + p), pipeline_mode=once),
            pl.BlockSpec((1, 2, nk, HEAD_DIM, tk), lambda b, p, i: (b, p, 0, 0, 0), pipeline_mode=once),
            pl.BlockSpec((nk, tk, LANES), lambda b, p, i: (0, 0, 0), pipeline_mode=once),
            pl.BlockSpec((1, 2, LANES, tq), lambda b, p, i: (b, p, 0, i)),
            pl.BlockSpec((1, tq, LANES), lambda b, p, i: (b, i, 0)),
        ],
        out_specs=pl.BlockSpec((1, tq, width), lambda b, p, i: (b, i, p)),
        out_shape=jax.ShapeDtypeStruct((B, S, D_MODEL), BF16),
        scratch_shapes=_flash_t_scratch(2 * HPG, tq, tk),
        compiler_params=_cparams(("parallel", "parallel", "parallel")),
    )(proj3, k3, vt, et, selbt, g)


def _nsa_cmp_t_body(q_ref, kc_ref, vot_ref, g_ref, o_ref, selbt_ref, s_ref, m_ref, bias_ref,
                    *, tq, n_cmp, n_sel, top_n):
    s0 = pl.program_id(2) * tq
    kc = kc_ref[0, 0]
    vot = vot_ref[0, 0]
    nc = kc.shape[0]
    nidx = lax.broadcasted_iota(jnp.int32, (nc, tq), 0)
    tpos = s0 + lax.broadcasted_iota(jnp.int32, (nc, tq), 1)
    cmask = (nidx * NSA_CMP_STRIDE + (NSA_CMP_LEN - 1) <= tpos) & (nidx < n_cmp)
    bias_ref[...] = jnp.where(cmask, 0.0, NEG)
    q4 = q_ref[0]
    for pair in range(NSA_HPG // 2):
        pair_t = q4[:, pair * LANES:(pair + 1) * LANES].astype(F32).T
        for hp in range(2):
            h = 2 * pair + hp
            st = _dot(kc, pair_t[hp * HEAD_DIM:(hp + 1) * HEAD_DIM, :].astype(BF16)) + bias_ref[...]
            s_ref[h * nc:(h + 1) * nc, :] = st
            m_ref[h * SUBLANES:(h + 1) * SUBLANES, :] = jnp.broadcast_to(
                jnp.max(st, axis=0, keepdims=True), (SUBLANES, tq))

    imp_t = jnp.zeros((LANES, tq), F32)
    heads = []
    for h in range(NSA_HPG):
        m = m_ref[h * SUBLANES:h * SUBLANES + 1, :]
        res = jnp.zeros((2 * LANES, tq), F32)
        for c0 in range(0, nc, VALUE_SUB):
            et = jnp.exp2(s_ref[h * nc + c0:h * nc + c0 + VALUE_SUB, :] - m).astype(BF16)
            res = res + _dot(vot[:, c0:c0 + VALUE_SUB], et)
        inv = jnp.where(m > 0.5 * NEG, 1.0 / res[HEAD_DIM:HEAD_DIM + 1, :], 0.0)
        heads.append((res[:LANES, :] * inv).T * _gate_col(g_ref[0], 0, h))
        imp_t = imp_t + res[LANES:, :] * inv
    o_ref[0] = jnp.concatenate(
        [_pair_merge(heads[h], heads[h + 1]) for h in range(0, NSA_HPG, 2)], axis=1).astype(o_ref.dtype)

    blk = lax.broadcasted_iota(jnp.int32, imp_t.shape, 0)
    jq = (s0 + lax.broadcasted_iota(jnp.int32, imp_t.shape, 1)) // NSA_SEL_BLOCK
    forced = (blk == 0) | (blk == jq) | (blk == jq - 1)
    cand = jnp.where(forced | (blk >= n_sel), REMOVED, jnp.where(blk > jq, NEG, imp_t))
    sel_t = _topk_mask_t(cand, top_n - N_FORCED) | forced
    selbt_ref[0, 0] = jnp.where(sel_t, 0.0, MASK).astype(selbt_ref.dtype)


def _nsa_cmp_t(proj3, kc, vot, g_cmp, n_cmp, n_sel, top_n, tq=1024):
    B, S, _ = proj3.shape
    G, HPG = NSA_KV_GROUPS, NSA_HPG
    NC = kc.shape[2]
    tq = min(tq, S)
    assert NC % VALUE_SUB == 0 and vot.shape == (B, G, 2 * LANES, NC)
    return pl.pallas_call(
        functools.partial(_nsa_cmp_t_body, tq=tq, n_cmp=n_cmp, n_sel=n_sel, top_n=top_n),
        grid=(B, G, S // tq),
        in_specs=[
            pl.BlockSpec((1, tq, HPG * HEAD_DIM), lambda b, g, i: (b, i, g)),
            pl.BlockSpec((1, 1, NC, HEAD_DIM), lambda b, g, i: (b, g, 0, 0)),
            pl.BlockSpec((1, 1, 2 * LANES, NC), lambda b, g, i: (b, g, 0, 0)),
            pl.BlockSpec((1, tq, LANES), lambda b, g, i: (b, i, 0)),
        ],
        out_specs=[
            pl.BlockSpec((1, tq, HPG * HEAD_DIM), lambda b, g, i: (b, i, g)),
            pl.BlockSpec((1, 1, LANES, tq), lambda b, g, i: (b, g, 0, i)),
        ],
        out_shape=[
            jax.ShapeDtypeStruct((B, S, D_MODEL), BF16),
            jax.ShapeDtypeStruct((B, G, LANES, S), BF16),
        ],
        scratch_shapes=[pltpu.VMEM((HPG * NC, tq), F32), pltpu.VMEM((HPG * SUBLANES, tq), F32),
                        pltpu.VMEM((NC, tq), F32)],
        compiler_params=_cparams(("parallel", "parallel", "parallel")),
    )(proj3, kc, vot, g_cmp)


def _nsa_win_t_body(q_ref, k_ref, vt_ref, g_ref, o_ref, qa_ref, s_ref, m_ref, alpha_ref, acc_ref, bias_ref,
                    *, tq, n_tiles):
    s0 = pl.program_id(2) * tq
    tk = n_tiles * tq
    n_heads = 2 * NSA_HPG
    k_first = jnp.maximum(pl.program_id(2) - (n_tiles - 1), 0)
    k0 = pl.multiple_of(k_first * tq, tq)
    _flash_t_init(m_ref, acc_ref)
    q8 = q_ref[0]
    for pair in range(n_heads // 2):
        grp = pair // (NSA_HPG // 2)
        pair_t = q8[:, pair * LANES:(pair + 1) * LANES].astype(F32).T
        for hp in range(2):
            h = 2 * pair + hp
            qa_ref[h * LANES:(h + 1) * LANES, :] = _head_kt(
                pair_t[hp * HEAD_DIM:(hp + 1) * HEAD_DIM, :].astype(BF16), grp)
    krow, qcol = _tile_iotas(tk, tq)
    dist = (s0 - k0) + qcol - krow
    bias_ref[...] = jnp.where((dist >= 0) & (dist < NSA_WINDOW), 0.0, MASK)
    keys_of = lambda h: k_ref[0, pl.ds(k0, tk), :]
    ones_rows = _ones_row_block(tk)
    vals_of = lambda h: jnp.concatenate(
        [jnp.concatenate([vt_ref[0, h // NSA_HPG, k_first + j] for j in range(n_tiles)], axis=1), ones_rows],
        axis=0)
    _flash_t_tile(n_heads, keys_of, vals_of, lambda: bias_ref[...], qa_ref, s_ref, m_ref, alpha_ref, acc_ref)
    gates = [_gate_col(g_ref[0], 2, h, n_heads) for h in range(n_heads)]
    o_ref[0] = _flash_t_out(acc_ref, n_heads, gates).astype(o_ref.dtype)


def _nsa_win_t(proj3, k3, k_block, vt, g, tq):
    B, S, _ = proj3.shape
    G, HPG = NSA_KV_GROUPS, NSA_HPG
    nk = S // tq
    n_tiles = NSA_WINDOW // tq + 1
    assert NSA_WINDOW % tq == 0 and nk >= n_tiles and G % 2 == 0 and vt.shape == (B, G, nk, HEAD_DIM, tq)
    once = pl.Buffered(1)
    width = 2 * HPG * HEAD_DIM
    return pl.pallas_call(
        functools.partial(_nsa_win_t_body, tq=tq, n_tiles=n_tiles),
        grid=(B, G // 2, S // tq),
        in_specs=[
            pl.BlockSpec((1, tq, width), lambda b, p, i: (b, i, p)),
            pl.BlockSpec((1, S, LANES), lambda b, p, i: (b, 0, k_block + p), pipeline_mode=once),
            pl.BlockSpec((1, 2, nk, HEAD_DIM, tq), lambda b, p, i: (b, p, 0, 0, 0), pipeline_mode=once),
            pl.BlockSpec((1, tq, LANES), lambda b, p, i: (b, i, 0)),
        ],
        out_specs=pl.BlockSpec((1, tq, width), lambda b, p, i: (b, i, p)),
        out_shape=jax.ShapeDtypeStruct((B, S, D_MODEL), BF16),
        scratch_shapes=(_flash_t_scratch(2 * HPG, tq, n_tiles * tq, qa_rows=LANES)
                        + [pltpu.VMEM((n_tiles * tq, tq), F32)]),
        compiler_params=_cparams(("parallel", "parallel", "parallel")),
    )(proj3, k3, vt, g)


def _moba_kmean_pairs_body(k_ref, e_ref, hi_ref, lo_ref, *, tk):
    nk = e_ref.shape[0]
    km = jnp.zeros((LANES, LANES), F32)
    for ki in range(nk):
        km = km + _dot(e_ref[ki], k_ref[0, ki * tk:(ki + 1) * tk, :])
    hi, lo = _split_bf16(km * (1.0 / MOBA_BLOCK))
    hi_ref[0, 0] = hi
    lo_ref[0, 0] = lo


def _moba_kmean_pairs(k3, e, tk):
    B, S, N = k3.shape
    n_pairs = N // LANES
    nk = S // tk
    spec = pl.BlockSpec((1, 1, LANES, LANES), lambda b, p: (b, p, 0, 0))
    return pl.pallas_call(
        functools.partial(_moba_kmean_pairs_body, tk=tk),
        grid=(B, n_pairs),
        in_specs=[
            pl.BlockSpec((1, S, LANES), lambda b, p: (b, 0, p)),
            pl.BlockSpec((nk, LANES, tk), lambda b, p: (0, 0, 0)),
        ],
        out_specs=[spec, spec],
        out_shape=[jax.ShapeDtypeStruct((B, n_pairs, LANES, LANES), BF16)] * 2,
        compiler_params=_cparams(("parallel", "parallel")),
    )(k3, e)


def _moba_t_body(qt_ref, k_ref, vt_ref, et_ref, kh_ref, kl_ref, o_ref, qa_ref, s_ref, m_ref, alpha_ref,
                 acc_ref, *, tq, tk, hb, nb, top_k):
    s0 = pl.program_id(2) * tq
    _flash_t_init(m_ref, acc_ref)
    aug = 2 * LANES

    blk = lax.broadcasted_iota(jnp.int32, (LANES, tq), 0)
    cb = (s0 + lax.broadcasted_iota(jnp.int32, (LANES, tq), 1)) // MOBA_BLOCK
    for h in range(hb):
        qpad = _head_kt(qt_ref[0, h, 0], h % 2)
        gsc = _dot(kh_ref[0, h // 2], qpad) + _dot(kl_ref[0, h // 2], qpad)
        gsc = jnp.where(blk < cb, gsc, NEG)
        gsc = jnp.where(blk < nb, gsc, REMOVED)
        sel = (_topk_mask_t(gsc, top_k) & (blk < cb)) | (blk == cb)
        qa_ref[h * aug:(h + 1) * aug, :] = jnp.concatenate(
            [jnp.where(sel, 0.0, MASK).astype(BF16), qpad], axis=0)

    ones_rows = _ones_row_block(tk)

    def tile(ki, mask):
        k0 = pl.multiple_of(ki * tk, tk)
        et = et_ref[ki]
        keys_of = lambda h: jnp.concatenate(
            [et, k_ref[0, pl.ds(k0, tk), (h // 2) * LANES:(h // 2 + 1) * LANES]], axis=1)
        vals_of = lambda h: jnp.concatenate([vt_ref[0, h, ki], ones_rows], axis=0)
        _flash_t_tile(hb, keys_of, vals_of, mask, qa_ref, s_ref, m_ref, alpha_ref, acc_ref)

    _flash_t_loop(tile, s0, tq, tk)
    o_ref[0] = _flash_t_out(acc_ref, hb, [1.0] * hb).astype(o_ref.dtype)


def _moba_attn_t(qt, k3, vt, et, kh, kl, nb, top_k, tq, tk, hb=8):
    B, H, nq, _, _ = qt.shape
    S = k3.shape[1]
    nk = S // tk
    assert tk == tq and tk % MOBA_BLOCK == 0 and H % hb == 0 and hb % 2 == 0 and nb <= LANES
    once = pl.Buffered(1)
    return pl.pallas_call(
        functools.partial(_moba_t_body, tq=tq, tk=tk, hb=hb, nb=nb, top_k=top_k),
        grid=(B, H // hb, nq),
        in_specs=[
            pl.BlockSpec((1, hb, 1, HEAD_DIM, tq), lambda b, h, i: (b, h, i, 0, 0)),
            pl.BlockSpec((1, S, hb * HEAD_DIM), lambda b, h, i: (b, 0, h), pipeline_mode=once),
            pl.BlockSpec((1, hb, nk, HEAD_DIM, tk), lambda b, h, i: (b, h, 0, 0, 0), pipeline_mode=once),
            pl.BlockSpec((nk, tk, LANES), lambda b, h, i: (0, 0, 0), pipeline_mode=once),
            pl.BlockSpec((1, hb // 2, LANES, LANES), lambda b, h, i: (b, h, 0, 0), pipeline_mode=once),
            pl.BlockSpec((1, hb // 2, LANES, LANES), lambda b, h, i: (b, h, 0, 0), pipeline_mode=once),
        ],
        out_specs=pl.BlockSpec((1, tq, hb * HEAD_DIM), lambda b, h, i: (b, i, h)),
        out_shape=jax.ShapeDtypeStruct((B, S, D_MODEL), BF16),
        scratch_shapes=_flash_t_scratch(hb, tq, tk),
        compiler_params=_cparams(("parallel", "parallel", "parallel")),
    )(qt, k3, vt, et, kh, kl)


def _layer_norm(r, g, b):
    mu = jnp.mean(r, axis=-1, keepdims=True)
    c = r - mu
    var = jnp.mean(c * c, axis=-1, keepdims=True)
    return c * lax.rsqrt(var + LN_EPS) * g + b


def _outproj_body(*refs, n_o):
    o_refs = refs[:n_o]
    w_ref, x_ref, g_ref, b_ref, y_ref, yb_ref = refs[n_o:]
    tm = x_ref.shape[0]
    rc = min(PROJ_ROW_CHUNK, tm)
    for r0 in range(0, tm, rc):
        rows = slice(r0, r0 + rc)
        o = o_refs[0][rows, :].astype(F32)
        for ref in o_refs[1:]:
            o = o + ref[rows, :].astype(F32)
        mix = _dot(o.astype(BF16), w_ref[...])
        y = _layer_norm(DN_ALPHA * x_ref[rows, :] + mix, g_ref[...], b_ref[...])
        y_ref[rows, :] = y
        yb_ref[rows, :] = y.astype(BF16)


def _outproj_ln(os_, w, x, g, b, tm=512):
    T, D = x.shape
    tm = min(tm, T)
    n_o = len(os_)
    row = pl.BlockSpec((tm, D), lambda i: (i, 0))
    vec = pl.BlockSpec((1, D), lambda i: (0, 0))
    return pl.pallas_call(
        functools.partial(_outproj_body, n_o=n_o),
        grid=(T // tm,),
        in_specs=[row] * n_o + [pl.BlockSpec((D, D), lambda i: (0, 0)), row, vec, vec],
        out_specs=[row, row],
        out_shape=[jax.ShapeDtypeStruct((T, D), F32), jax.ShapeDtypeStruct((T, D), BF16)],
        compiler_params=_cparams(("parallel",)),
    )(*os_, w, x, g.reshape(1, D), b.reshape(1, D))


GID_LANE = N_EXPERTS
MOE_WINDOW = 1024
MOE_CHUNK = 128
MOE_VMEM_LIMIT = 60 * 1024 * 1024


def _router_sorted_body(x_ref, wh_ref, wl_ref, bias_ref, gate_ref, gidt_ref, cnt_ref):
    x_hi, x_lo = _split_bf16(x_ref[...])
    wh = wh_ref[...]
    logits = _dot(x_hi, wh) + _dot(x_lo, wh) + _dot(x_hi, wl_ref[...])
    scores = _sigmoid(logits)
    score_t = scores.T
    biased_t = (scores + bias_ref[...]).T
    tm = score_t.shape[1]
    s_rows = [score_t[e:e + 1, :] for e in range(N_EXPERTS)]
    b_rows = [biased_t[e:e + 1, :] for e in range(N_EXPERTS)]

    def first_max(vals, live):
        m = None
        for v, ok in zip(vals, live):
            cand = jnp.where(ok > 0.5, v, REMOVED)
            m = cand if m is None else jnp.maximum(m, cand)
        hits, found = [], jnp.zeros_like(m)
        for v, ok in zip(vals, live):
            hit = jnp.where((v == m) & (ok > 0.5) & (found < 0.5), 1.0, 0.0)
            found = found + hit
            hits.append(hit)
        return m, hits

    ones = jnp.ones((1, tm), F32)
    best, gid, sels = None, None, []
    for grp in range(N_GROUPS):
        vals = b_rows[grp * EXPERTS_PER_GROUP:(grp + 1) * EXPERTS_PER_GROUP]
        m1, h1 = first_max(vals, [ones] * EXPERTS_PER_GROUP)
        m2, h2 = first_max(vals, [1.0 - h for h in h1])
        sels.append([a + b for a, b in zip(h1, h2)])
        score = m1 + m2
        if grp == 0:
            best, gid = score, jnp.zeros_like(score)
        else:
            better = score > best
            best = jnp.where(better, score, best)
            gid = jnp.where(better, float(grp), gid)
    w_rows = []
    for e in range(N_EXPERTS):
        grp, j = divmod(e, EXPERTS_PER_GROUP)
        w_rows.append(jnp.where((gid == float(grp)) & (sels[grp][j] > 0.5), s_rows[e], 0.0))
    total = w_rows[0]
    for w in w_rows[1:]:
        total = total + w
    rows_out = 3 * SUBLANES
    row = lax.broadcasted_iota(jnp.int32, (rows_out, tm), 0)
    gate_t = jnp.where(row == GID_LANE, gid, 0.0)
    for e in range(N_EXPERTS):
        gate_t = jnp.where(row == e, w_rows[e] / total, gate_t)
    gate_t = jnp.concatenate([gate_t, jnp.zeros((LANES - rows_out, tm), F32)], axis=0)
    gate_ref[...] = gate_t.T
    gidt_ref[...] = jnp.broadcast_to(gid, (SUBLANES, tm))
    lane = lax.broadcasted_iota(jnp.int32, (SUBLANES, LANES), 1)
    counts = jnp.zeros((SUBLANES, LANES), F32)
    for grp in range(N_GROUPS):
        n = jnp.sum(jnp.where(gid == float(grp), 1.0, 0.0), axis=-1, keepdims=True)
        counts = jnp.where(lane == grp, n, counts)
    cnt_ref[0] = counts


def _router_sorted(x, router_w, router_bias, tm):
    T, D = x.shape
    wpad = jnp.zeros((D, LANES), F32).at[:, :N_EXPERTS].set(router_w)
    wh, wl = _split_bf16(wpad)
    bpad = jnp.zeros((1, LANES), F32).at[0, :N_EXPERTS].set(router_bias)
    return pl.pallas_call(
        _router_sorted_body,
        grid=(T // tm,),
        in_specs=[
            pl.BlockSpec((tm, D), lambda i: (i, 0)),
            pl.BlockSpec((D, LANES), lambda i: (0, 0)),
            pl.BlockSpec((D, LANES), lambda i: (0, 0)),
            pl.BlockSpec((1, LANES), lambda i: (0, 0)),
        ],
        out_specs=[
            pl.BlockSpec((tm, LANES), lambda i: (i, 0)),
            pl.BlockSpec((SUBLANES, tm), lambda i: (0, i)),
            pl.BlockSpec((1, SUBLANES, LANES), lambda i: (i, 0, 0)),
        ],
        out_shape=[
            jax.ShapeDtypeStruct((T, LANES), F32),
            jax.ShapeDtypeStruct((SUBLANES, T), F32),
            jax.ShapeDtypeStruct((T // tm, SUBLANES, LANES), F32),
        ],
        compiler_params=_cparams(("parallel",)),
    )(x, wh, wl, bpad)


def _experts_sorted_body(cnt_ref, xb_ref, x_ref, gate_ref, gidt_ref, ltri_ref, utri_ref, wg_ref, wu_ref,
                         wd_ref, g_ref, b_ref, y_ref, yb_ref, xs_ref, gs_ref, acc_ref, rank_ref,
                         *, W, Wp, chunk):
    win = pl.program_id(0)
    e = pl.program_id(1)
    grp = e // EXPERTS_PER_GROUP
    padded = [((cnt_ref[win * N_GROUPS + g] + chunk - 1) // chunk) * chunk for g in range(N_GROUPS)]
    starts = [0]
    for g in range(N_GROUPS - 1):
        starts.append(starts[-1] + padded[g])
    start = starts[0]
    for g in range(1, N_GROUPS):
        start = jnp.where(grp == g, starts[g], start)
    n_chunks = (cnt_ref[win * N_GROUPS + grp] + chunk - 1) // chunk

    @pl.when(e == 0)
    def _():
        gate = gate_ref[...]
        lane = lax.broadcasted_iota(jnp.int32, gate.shape, 1)
        lanef = lane.astype(F32)
        gid = jnp.sum(jnp.where(lane == GID_LANE, gate, 0.0), axis=-1, keepdims=True)
        member = jnp.where((lanef == gid) & (lane < N_GROUPS), 1.0, 0.0)
        earlier = _dot(ltri_ref[...], member.astype(BF16))
        first = jnp.zeros(gate.shape, F32)
        for g in range(1, N_GROUPS):
            first = jnp.where(lane == g, starts[g].astype(F32), first)
        rank = jnp.sum(member * (first + earlier), axis=-1, keepdims=True)
        rank_ref[...] = jnp.broadcast_to(rank, gate.shape)

        gid_r = gidt_ref[...]
        sub = lax.broadcasted_iota(jnp.int32, gid_r.shape, 0)
        member_r = jnp.where(sub.astype(F32) == gid_r, 1.0, 0.0)
        earlier_r = _dot(member_r.astype(BF16), utri_ref[...])
        first_r = jnp.zeros(gid_r.shape, F32)
        for g in range(1, N_GROUPS):
            first_r = jnp.where(sub == g, starts[g].astype(F32), first_r)
        rank_r = jnp.sum(member_r * (first_r + earlier_r), axis=0, keepdims=True)
        rows = lax.broadcasted_iota(jnp.int32, (Wp, W), 0).astype(F32)
        perm = jnp.where(rows == rank_r, 1.0, 0.0).astype(BF16)
        xs_ref[...] = _dot(perm, xb_ref[...]).astype(BF16)
        g_hi, g_lo = _split_bf16(gate)
        gs_ref[...] = _dot(perm, g_hi) + _dot(perm, g_lo)
        acc_ref[...] = jnp.zeros(acc_ref.shape, F32)

    def expert_rows(r0, rows):
        r0 = pl.multiple_of(r0, chunk)
        xc = xs_ref[pl.ds(r0, rows), :]
        a = _dot(xc, wg_ref[0])
        u = _dot(xc, wu_ref[0])
        gs = gs_ref[pl.ds(r0, rows), :]
        lane = lax.broadcasted_iota(jnp.int32, gs.shape, 1)
        gcol = jnp.sum(jnp.where(lane == e, gs, 0.0), axis=-1, keepdims=True)
        h = a * _sigmoid(a) * u * gcol
        acc_ref[pl.ds(r0, rows), :] += _dot(h.astype(BF16), wd_ref[0])

    def three_chunks(i, carry):
        expert_rows(start + i * (3 * chunk), 3 * chunk)
        return carry

    lax.fori_loop(0, n_chunks // 3, three_chunks, 0)
    tail = start + (n_chunks // 3) * (3 * chunk)
    for left in (1, 2):
        pl.when(n_chunks % 3 == left)(functools.partial(expert_rows, tail, left * chunk))

    @pl.when(e == N_EXPERTS - 1)
    def _():
        cols = lax.broadcasted_iota(jnp.int32, (W, Wp), 1).astype(F32)
        unperm = jnp.where(cols == rank_ref[...][:, :1], 1.0, 0.0).astype(BF16)
        ffn = _dot(unperm, acc_ref[...].astype(BF16))
        y = _layer_norm(DN_ALPHA * x_ref[...] + ffn, g_ref[...], b_ref[...])
        y_ref[...] = y
        yb_ref[...] = y.astype(BF16)


def _moe_ln(xb, x, router_w, router_bias, wg, wu, wd, g, b):
    T, D = x.shape
    W = min(MOE_WINDOW, T)
    chunk = MOE_CHUNK
    Wp = W + N_GROUPS * chunk
    E, _, DE = wg.shape
    assert T % W == 0 and W % chunk == 0
    gate, gidt, cnt = _router_sorted(x, router_w, router_bias, W)
    counts = cnt[:, 0, :N_GROUPS].astype(jnp.int32).reshape(-1)
    t = np.arange(W)
    ltri = jnp.asarray((t[None, :] < t[:, None]).astype(np.float32), BF16)
    once = pl.Buffered(1)
    row = lambda shape: pl.BlockSpec(shape, lambda i, e, c: (i, 0))
    vec = pl.BlockSpec((1, D), lambda i, e, c: (0, 0))
    tri = pl.BlockSpec((W, W), lambda i, e, c: (0, 0), pipeline_mode=once)
    grid_spec = pltpu.PrefetchScalarGridSpec(
        num_scalar_prefetch=1,
        grid=(T // W, E),
        in_specs=[
            row((W, D)),
            pl.BlockSpec((W, D), lambda i, e, c: (i, 0), pipeline_mode=once),
            row((W, LANES)),
            pl.BlockSpec((SUBLANES, W), lambda i, e, c: (0, i)),
            tri, tri,
            pl.BlockSpec((1, D, DE), lambda i, e, c: (e, 0, 0)),
            pl.BlockSpec((1, D, DE), lambda i, e, c: (e, 0, 0)),
            pl.BlockSpec((1, DE, D), lambda i, e, c: (e, 0, 0)),
            vec, vec,
        ],
        out_specs=[row((W, D)), row((W, D))],
        scratch_shapes=[
            pltpu.VMEM((Wp, D), BF16),
            pltpu.VMEM((Wp, LANES), F32),
            pltpu.VMEM((Wp, D), F32),
            pltpu.VMEM((W, LANES), F32),
        ],
    )
    return pl.pallas_call(
        functools.partial(_experts_sorted_body, W=W, Wp=Wp, chunk=chunk),
        grid_spec=grid_spec,
        out_shape=[jax.ShapeDtypeStruct((T, D), F32), jax.ShapeDtypeStruct((T, D), BF16)],
        compiler_params=pltpu.CompilerParams(dimension_semantics=("parallel", "arbitrary"),
                                             vmem_limit_bytes=MOE_VMEM_LIMIT),
    )(counts, xb, x, gate, gidt, ltri, ltri.T, wg, wu, wd, g.reshape(1, D), b.reshape(1, D))


def _block_onehots(S, block, tk):
    key = np.arange(S).reshape(S // tk, 1, tk)
    r = np.arange(LANES).reshape(1, LANES, 1)
    return jnp.asarray((key // block == r).astype(np.float32), BF16)


def _rope_tiled(S):
    cos, sin = _rope_tables(jnp.arange(S))
    reps = LANES // HALF
    return jnp.tile(cos, (1, reps)), jnp.tile(sin, (1, reps))


def _nsa_mixer(xb, B, S, w_in, cmp_k_w1, cmp_k_w2, cmp_v_w1, cmp_v_w2, cmp_k_pos, cmp_v_pos):
    G, HPG, KV = NSA_KV_GROUPS, NSA_HPG, NSA_KV_DIM
    L, STR, SB = NSA_CMP_LEN, NSA_CMP_STRIDE, NSA_SEL_BLOCK
    assert L == 2 * STR and S % SB == 0 and S // SB <= LANES
    T = B * S
    n_cmp = (S - L) // STR + 1
    NC = S // STR
    n_sel = S // SB
    top_n = min(NSA_SEL_TOPN, n_sel)
    assert top_n >= N_FORCED

    cos2, sin2 = _rope_tiled(S)
    wb = w_in.astype(BF16)
    wcol = lambda i: wb[:, D_MODEL + i * KV: D_MODEL + (i + 1) * KV]
    tn = 2 * KV
    tk_s = min(512, S)
    tq_w = NSA_WINDOW // 2
    proj = _proj(xb, wb[:, :D_MODEL + 2 * KV], cos2, sin2, [2] * (D_MODEL // tn) + [0], S, tn=tn)
    kk3 = _proj(xb, jnp.concatenate([wcol(2), wcol(4)], axis=1), cos2, sin2, [1], S, tn=tn).reshape(B, S, tn)
    vt_s = _proj_kt(xb, wcol(3), cos2, sin2, B, S, tk_s, rope=False)
    vt_w = _proj_kt(xb, wcol(5), cos2, sin2, B, S, tq_w, rope=False)
    wg = jnp.zeros((D_MODEL, LANES), BF16).at[:, :3 * N_HEADS].set(wb[:, D_MODEL + 6 * KV:])
    gates = _proj(xb, wg, cos2, sin2, [0], S, out_dtype=F32, tn=LANES).reshape(B, S, LANES)

    col = lambda i: proj[:, D_MODEL + i * KV: D_MODEL + (i + 1) * KV]
    proj3 = proj.reshape(B, S, proj.shape[1])

    ccos, csin = _rope_tables(jnp.arange(NC) * STR + (L - 1))
    ccos = jnp.concatenate([ccos, ccos], axis=1)
    csin = jnp.concatenate([csin, csin], axis=1)
    to_rows = lambda t: t.reshape(B, S, G, HEAD_DIM).transpose(0, 2, 1, 3).reshape(B * G, NC, STR * HEAD_DIM)
    kc = _compress(to_rows(col(0)), cmp_k_w1, cmp_k_pos, cmp_k_w2, ccos, csin, True, n_cmp)
    vc = _compress(to_rows(col(1)), cmp_v_w1, cmp_v_pos, cmp_v_w2, ccos, csin, False, n_cmp)

    ci = np.arange(NC)[:, None]
    sj = np.arange(LANES)[None, :]
    overlap = ((ci * STR < (sj + 1) * SB) & (ci * STR + L > sj * SB) & (ci < n_cmp) & (sj < n_sel))
    overlap = jnp.broadcast_to(jnp.asarray(overlap.astype(np.float32), BF16), (B, G, NC, LANES))
    vo = jnp.concatenate([vc.reshape(B, G, NC, HEAD_DIM), jnp.ones((B, G, NC, 1), BF16),
                          jnp.zeros((B, G, NC, LANES - HEAD_DIM - 1), BF16), overlap], axis=-1)

    o_cmp, selbt = _nsa_cmp_t(proj3, kc.reshape(B, G, NC, HEAD_DIM), vo.transpose(0, 1, 3, 2), gates,
                              n_cmp, n_sel, top_n)
    o_slc = _nsa_slc_t(proj3, kk3, 0, vt_s, _block_onehots(S, SB, tk_s).transpose(0, 2, 1), selbt, gates,
                       tq=tk_s, tk=tk_s)
    o_win = _nsa_win_t(proj3, kk3, G // 2, vt_w, gates, tq=tq_w)
    return [o.reshape(T, D_MODEL) for o in (o_cmp, o_slc, o_win)]


def _moba_mixer(xb, B, S, w_in):
    H = N_HEADS
    nb = S // MOBA_BLOCK
    top_k = min(MOBA_TOPK, nb)
    cos2, sin2 = _rope_tiled(S)
    tn = 512
    n_t = D_MODEL // tn
    wb = w_in.astype(BF16)
    tk = min(2 * MOBA_BLOCK, S)
    qt = _proj_kt(xb, wb[:, :D_MODEL], cos2, sin2, B, S, tk, tn=tn, scale=Q_SCALE_LOG2)
    k3 = _proj(xb, wb[:, D_MODEL:2 * D_MODEL], cos2, sin2, [1] * n_t, S, tn=tn).reshape(B, S, D_MODEL)
    vt = _proj_kt(xb, wb[:, 2 * D_MODEL:], cos2, sin2, B, S, tk, tn=tn, rope=False)
    e = _block_onehots(S, MOBA_BLOCK, tk)
    kh, kl = _moba_kmean_pairs(k3, e, tk)
    o = _moba_attn_t(qt, k3, vt, e.transpose(0, 2, 1), kh, kl, nb, top_k, tq=tk, tk=tk)
    return [o.reshape(B * S, D_MODEL)]


def kernel(x, nsa_w_in, nsa_w_out, nsa_cmp_k_w1, nsa_cmp_k_w2, nsa_cmp_v_w1, nsa_cmp_v_w2, nsa_cmp_k_pos, nsa_cmp_v_pos, moba_w_in, moba_w_out, router_w, router_bias, moe_w_gate, moe_w_up, moe_w_down, ln_g, ln_b):
    B, S, D = x.shape
    xf = x.reshape(B * S, D)
    xb = xf.astype(BF16)
    for layer in range(DEPTH):
        j = layer // 2
        if layer % 2 == 0:
            os_ = _nsa_mixer(xb, B, S, nsa_w_in[j], nsa_cmp_k_w1[j], nsa_cmp_k_w2[j], nsa_cmp_v_w1[j],
                             nsa_cmp_v_w2[j], nsa_cmp_k_pos[j], nsa_cmp_v_pos[j])
            w_out = nsa_w_out[j]
        else:
            os_ = _moba_mixer(xb, B, S, moba_w_in[j])
            w_out = moba_w_out[j]
        xf, xb = _outproj_ln(os_, w_out.astype(BF16), xf, ln_g[layer, 0], ln_b[layer, 0])
        xf, xb = _moe_ln(xb, xf, router_w, router_bias, moe_w_gate[layer].astype(BF16),
                         moe_w_up[layer].astype(BF16), moe_w_down[layer].astype(BF16),
                         ln_g[layer, 1], ln_b[layer, 1])
    return xf.reshape(B, S, D)
```

```python
import functools

import jax
import jax.numpy as jnp
import numpy as np
from jax import lax
from jax.experimental import pallas as pl
from jax.experimental.pallas import tpu as pltpu

F32 = jnp.float32
BF16 = jnp.bfloat16

D_MODEL = 1024
N_HEADS = 16
HEAD_DIM = 64
HALF = HEAD_DIM // 2
ROPE_THETA = 10000.0
DEPTH = 2
DN_ALPHA = (2 * DEPTH) ** 0.25
LN_EPS = 1e-5
NEG = -1e30
N_FORCED = 3
MASK = -1e30
REMOVED = -3.0e38
LANES = 128
SUBLANES = 8
Q_SCALE_LOG2 = float(HEAD_DIM ** -0.5 * np.log2(np.e))
PROJ_ROW_CHUNK = 256

NSA_KV_GROUPS = 4
NSA_HPG = N_HEADS // NSA_KV_GROUPS
NSA_KV_DIM = NSA_KV_GROUPS * HEAD_DIM
NSA_CMP_LEN = 32
NSA_CMP_STRIDE = 16
NSA_SEL_BLOCK = 64
NSA_SEL_TOPN = 16
NSA_WINDOW = 512

MOBA_BLOCK = 256
MOBA_TOPK = 3

N_EXPERTS = 16
N_GROUPS = 4
EXPERTS_PER_GROUP = N_EXPERTS // N_GROUPS

VMEM_LIMIT = 48 * 1024 * 1024


def _cparams(sem):
    return pltpu.CompilerParams(dimension_semantics=sem, vmem_limit_bytes=VMEM_LIMIT)


def _dot(a, b):
    return jnp.dot(a, b, preferred_element_type=F32)


def _split_bf16(x):
    hi = x.astype(BF16)
    lo = (x - hi.astype(F32)).astype(BF16)
    return hi, lo


def _sigmoid(x):
    return 1.0 / (1.0 + jnp.exp(-x))


def _proj_body(mode_ref, x_ref, w_ref, cos_ref, sin_ref, o_ref, *, tn):
    mode = mode_ref[pl.program_id(0)]
    tm = x_ref.shape[0]
    rc = min(PROJ_ROW_CHUNK, tm)

    @pl.when(mode == 0)
    def _():
        for r in range(0, tm, rc):
            o_ref[r:r + rc, :] = _dot(x_ref[r:r + rc, :], w_ref[...]).astype(o_ref.dtype)

    @pl.when(mode != 0)
    def _():
        sc = jnp.where(mode == 2, Q_SCALE_LOG2, 1.0).astype(F32)
        for r in range(0, tm, rc):
            acc = _dot(x_ref[r:r + rc, :], w_ref[...])
            cos = cos_ref[r:r + rc, :] * sc
            sin = sin_ref[r:r + rc, :] * sc
            for c in range(tn // LANES):
                o_ref[r:r + rc, c * LANES:(c + 1) * LANES] = (
                    _rope_chunk(acc[:, c * LANES:(c + 1) * LANES], cos, sin).astype(o_ref.dtype))


def _proj(xb, w, cos2, sin2, modes, seq, out_dtype=BF16, tm=1024, tn=512):
    T, K = xb.shape
    N = w.shape[1]
    tm = min(tm, seq)
    assert T % tm == 0 and N % tn == 0 and seq % tm == 0 and len(modes) == N // tn
    n_pos = seq // tm
    grid_spec = pltpu.PrefetchScalarGridSpec(
        num_scalar_prefetch=1,
        grid=(N // tn, T // tm),
        in_specs=[
            pl.BlockSpec((tm, K), lambda j, i, m: (i, 0)),
            pl.BlockSpec((K, tn), lambda j, i, m: (0, j)),
            pl.BlockSpec((tm, LANES), lambda j, i, m: (i % n_pos, 0)),
            pl.BlockSpec((tm, LANES), lambda j, i, m: (i % n_pos, 0)),
        ],
        out_specs=pl.BlockSpec((tm, tn), lambda j, i, m: (i, j)),
    )
    return pl.pallas_call(
        functools.partial(_proj_body, tn=tn),
        grid_spec=grid_spec,
        out_shape=jax.ShapeDtypeStruct((T, N), out_dtype),
        compiler_params=_cparams(("parallel", "parallel")),
    )(jnp.asarray(modes, jnp.int32), xb, w, cos2, sin2)


def _rope_tables(pos):
    inv = 1.0 / (ROPE_THETA ** (jnp.arange(0, HEAD_DIM, 2, dtype=F32) / HEAD_DIM))
    ang = pos.astype(F32)[:, None] * inv[None, :]
    return jnp.cos(ang), jnp.sin(ang)


def _rope_chunk(a, cos, sin):
    lane = lax.broadcasted_iota(jnp.int32, a.shape, 1)
    up = pltpu.roll(a, LANES - HALF, 1)
    dn = pltpu.roll(a, HALF, 1)
    return a * cos + jnp.where((lane % HEAD_DIM) < HALF, -up, dn) * sin


def _proj_kt_body(x_ref, w_ref, cos_ref, sin_ref, o_ref, *, tn, tk, rope, scale):
    tm = x_ref.shape[0]
    rc = min(PROJ_ROW_CHUNK, tk)
    for r in range(0, tm, rc):
        acc = _dot(x_ref[r:r + rc, :], w_ref[...])
        cos = cos_ref[r:r + rc, :] * scale
        sin = sin_ref[r:r + rc, :] * scale
        for c in range(tn // LANES):
            a = acc[:, c * LANES:(c + 1) * LANES]
            kt = (_rope_chunk(a, cos, sin) if rope else a).T
            for hh in range(2):
                o_ref[0, 2 * c + hh, r // tk, :, r % tk:r % tk + rc] = (
                    kt[hh * HEAD_DIM:(hh + 1) * HEAD_DIM, :].astype(o_ref.dtype))


def _proj_kt(xb, w, cos2, sin2, B, S, tk, tm=1024, tn=256, rope=True, scale=1.0):
    T, K = xb.shape
    N = w.shape[1]
    tm = min(tm, S)
    tn = min(tn, N)
    assert S % tm == 0 and tm % tk == 0 and N % tn == 0
    n_pos = S // tm
    return pl.pallas_call(
        functools.partial(_proj_kt_body, tn=tn, tk=tk, rope=rope, scale=scale),
        grid=(N // tn, T // tm),
        in_specs=[
            pl.BlockSpec((tm, K), lambda j, i: (i, 0)),
            pl.BlockSpec((K, tn), lambda j, i: (0, j)),
            pl.BlockSpec((tm, LANES), lambda j, i: (i % n_pos, 0)),
            pl.BlockSpec((tm, LANES), lambda j, i: (i % n_pos, 0)),
        ],
        out_specs=pl.BlockSpec((1, tn // HEAD_DIM, tm // tk, HEAD_DIM, tk),
                               lambda j, i: (i // n_pos, j, i % n_pos, 0, 0)),
        out_shape=jax.ShapeDtypeStruct((B, N // HEAD_DIM, S // tk, HEAD_DIM, tk), BF16),
        compiler_params=_cparams(("parallel", "parallel")),
    )(xb, w, cos2, sin2)


def _gelu_tanh(x):
    c = np.float32(np.sqrt(2.0 / np.pi))
    return 0.5 * x * (1.0 + jnp.tanh(c * (x + 0.044715 * (x * x * x))))


def _compress_body(r_ref, w1_ref, pos_ref, w2_ref, w2r_ref, cos_ref, sin_ref, o_ref, *, rope, n_cmp):
    r = r_ref[0]
    nc = r.shape[0]
    half = NSA_CMP_STRIDE * HEAD_DIM
    a = _dot(r, w1_ref[0])
    b = _dot(r, w1_ref[1])
    pos = pos_ref[...]
    pb = _dot(pos[:, :half], w1_ref[0]) + _dot(pos[:, half:], w1_ref[1])
    b_next = pltpu.roll(b, nc - 1, 0)
    h = _gelu_tanh(a + b_next + pb[0:1, :]).astype(BF16)
    o = _dot(h, w2_ref[...])
    if rope:
        o = o * cos_ref[...] + _dot(h, w2r_ref[...]) * sin_ref[...]
    row = lax.broadcasted_iota(jnp.int32, o.shape, 0)
    o_ref[0] = jnp.where(row < n_cmp, o, 0.0).astype(o_ref.dtype)


def _compress(r, w1, pos, w2, cos_c, sin_c, rope, n_cmp):
    BG, NC, K = r.shape
    hidden = w1.shape[1]
    w1s = w1.astype(BF16).reshape(2, K, hidden)
    pos8 = jnp.zeros((SUBLANES, 2 * K), BF16).at[0].set(pos.reshape(-1).astype(BF16))
    w2r = jnp.concatenate([-w2[:, HALF:], w2[:, :HALF]], axis=1).astype(BF16)
    full = lambda shape: pl.BlockSpec(shape, lambda i: (0,) * len(shape))
    return pl.pallas_call(
        functools.partial(_compress_body, rope=rope, n_cmp=n_cmp),
        grid=(BG,),
        in_specs=[
            pl.BlockSpec((1, NC, K), lambda i: (i, 0, 0)),
            full((2, K, hidden)),
            full((SUBLANES, 2 * K)),
            full((hidden, HEAD_DIM)),
            full((hidden, HEAD_DIM)),
            full((NC, HEAD_DIM)),
            full((NC, HEAD_DIM)),
        ],
        out_specs=pl.BlockSpec((1, NC, HEAD_DIM), lambda i: (i, 0, 0)),
        out_shape=jax.ShapeDtypeStruct((BG, NC, HEAD_DIM), BF16),
        compiler_params=_cparams(("parallel",)),
    )(r, w1s, pos8, w2.astype(BF16), w2r, cos_c, sin_c)


def _topk_mask_t(v, k):
    idx = lax.broadcasted_iota(jnp.int32, v.shape, 0).astype(F32)

    def step(_, cur):
        m = jnp.max(cur, axis=0, keepdims=True)
        first = jnp.min(jnp.where(cur == m, idx, float(LANES)), axis=0, keepdims=True)
        return jnp.where(idx == first, REMOVED, cur)

    return lax.fori_loop(0, k, step, v, unroll=True) != v


def _head_kt(t, parity):
    z = jnp.zeros_like(t)
    return jnp.concatenate([t, z] if parity == 0 else [z, t], axis=0)


def _pair_merge(even, odd):
    lane = lax.broadcasted_iota(jnp.int32, even.shape, 1)
    return jnp.where(lane < HEAD_DIM, even, pltpu.roll(odd, HEAD_DIM, 1))


def _gate_col(logits, branch, h, heads_per_step=NSA_HPG):
    idx = branch * N_HEADS + pl.program_id(1) * heads_per_step + h
    lane = lax.broadcasted_iota(jnp.int32, logits.shape, 1)
    return _sigmoid(jnp.sum(jnp.where(lane == idx, logits, 0.0), axis=-1, keepdims=True))


def _tile_iotas(rows, tk):
    return (lax.broadcasted_iota(jnp.int32, (rows, tk), 0),
            lax.broadcasted_iota(jnp.int32, (rows, tk), 1))


VALUE_SUB = 2 * LANES


def _flash_t_scratch(n_heads, tq, tk, qa_rows=2 * LANES):
    return [pltpu.VMEM((n_heads * qa_rows, tq), BF16),
            pltpu.VMEM((n_heads * tk, tq), F32),
            pltpu.VMEM((n_heads * SUBLANES, tq), F32),
            pltpu.VMEM((n_heads * SUBLANES, tq), F32),
            pltpu.VMEM((n_heads * LANES, tq), F32)]


def _flash_t_init(m_ref, acc_ref):
    m_ref[...] = jnp.full(m_ref.shape, -jnp.inf, F32)
    acc_ref[...] = jnp.zeros(acc_ref.shape, F32)


def _flash_t_tile(n_heads, keys_of, vals_of, mask, qa_ref, s_ref, m_ref, alpha_ref, acc_ref):
    aug = qa_ref.shape[0] // n_heads
    tk, tq = s_ref.shape[0] // n_heads, s_ref.shape[1]
    for h in range(n_heads):
        st = _dot(keys_of(h), qa_ref[h * aug:(h + 1) * aug, :])
        if callable(mask):
            st = st + mask()
        elif mask is not None:
            st = jnp.where(mask, st, MASK)
        s_ref[h * tk:(h + 1) * tk, :] = st
        m_prev = m_ref[h * SUBLANES:(h + 1) * SUBLANES, :]
        m_new = jnp.maximum(m_prev, jnp.max(st, axis=0, keepdims=True))
        alpha_ref[h * SUBLANES:(h + 1) * SUBLANES, :] = jnp.exp2(m_prev - m_new)
        m_ref[h * SUBLANES:(h + 1) * SUBLANES, :] = m_new
    for h in range(n_heads):
        vals = vals_of(h)
        for q0 in range(0, tq, VALUE_SUB):
            m = m_ref[h * SUBLANES:h * SUBLANES + 1, q0:q0 + VALUE_SUB]
            part = jnp.zeros((LANES, VALUE_SUB), F32)
            for c0 in range(0, tk, VALUE_SUB):
                pt = jnp.exp2(s_ref[h * tk + c0:h * tk + c0 + VALUE_SUB, q0:q0 + VALUE_SUB] - m).astype(BF16)
                part = part + _dot(vals[:, c0:c0 + VALUE_SUB], pt)
            acc_ref[h * LANES:(h + 1) * LANES, q0:q0 + VALUE_SUB] = (
                alpha_ref[h * SUBLANES:h * SUBLANES + 1, q0:q0 + VALUE_SUB]
                * acc_ref[h * LANES:(h + 1) * LANES, q0:q0 + VALUE_SUB] + part)


def _flash_t_loop(tile, s0, tq, tk):
    last = (s0 + tq - 1) // tk

    def full_tile(ki, carry):
        tile(ki, None)
        return carry

    lax.fori_loop(0, last, full_tile, 0)
    krow, qcol = _tile_iotas(tk, tq)
    tile(last, last * tk + krow <= s0 + qcol)


def _flash_t_out(acc_ref, n_heads, gates):
    def head(h):
        acc = acc_ref[h * LANES:(h + 1) * LANES, :]
        return (acc / acc[HEAD_DIM:HEAD_DIM + 1, :]).T * gates[h]
    return jnp.concatenate([_pair_merge(head(h), head(h + 1)) for h in range(0, n_heads, 2)], axis=1)


def _ones_row_block(tk):
    return jnp.where(lax.broadcasted_iota(jnp.int32, (HEAD_DIM, tk), 0) == 0, 1.0, 0.0).astype(BF16)


def _nsa_slc_t_body(q_ref, k_ref, vt_ref, et_ref, selbt_ref, g_ref, o_ref, qa_ref, s_ref, m_ref, alpha_ref,
                    acc_ref, *, tq, tk):
    s0 = pl.program_id(2) * tq
    n_heads = 2 * NSA_HPG
    _flash_t_init(m_ref, acc_ref)
    q8 = q_ref[0]
    for pair in range(n_heads // 2):
        grp = pair // (NSA_HPG // 2)
        pair_t = q8[:, pair * LANES:(pair + 1) * LANES].astype(F32).T
        for hp in range(2):
            qt = pair_t[hp * HEAD_DIM:(hp + 1) * HEAD_DIM, :].astype(BF16)
            h = 2 * pair + hp
            qa_ref[h * 2 * LANES:(h + 1) * 2 * LANES, :] = jnp.concatenate(
                [selbt_ref[0, grp], _head_kt(qt, grp)], axis=0)
    ones_rows = _ones_row_block(tk)

    def tile(ki, mask):
        k0 = pl.multiple_of(ki * tk, tk)
        keys = jnp.concatenate([et_ref[ki], k_ref[0, pl.ds(k0, tk), :]], axis=1)
        vals_of = lambda h: jnp.concatenate([vt_ref[0, h // NSA_HPG, ki], ones_rows], axis=0)
        _flash_t_tile(n_heads, lambda h: keys, vals_of, mask, qa_ref, s_ref, m_ref, alpha_ref, acc_ref)

    _flash_t_loop(tile, s0, tq, tk)
    gates = [_gate_col(g_ref[0], 1, h, n_heads) for h in range(n_heads)]
    o_ref[0] = _flash_t_out(acc_ref, n_heads, gates).astype(o_ref.dtype)


def _nsa_slc_t(proj3, k3, k_block, vt, et, selbt, g, tq, tk):
    B, S, _ = proj3.shape
    G, HPG = NSA_KV_GROUPS, NSA_HPG
    nk = S // tk
    assert tk == tq and G % 2 == 0 and vt.shape == (B, G, nk, HEAD_DIM, tk) and et.shape == (nk, tk, LANES)
    once = pl.Buffered(1)
    width = 2 * HPG * HEAD_DIM
    return pl.pallas_call(
        functools.partial(_nsa_slc_t_body, tq=tq, tk=tk),
        grid=(B, G // 2, S // tq),
        in_specs=[
            pl.BlockSpec((1, tq, width), lambda b, p, i: (b, i, p)),
            pl.BlockSpec((1, S, LANES), lambda b, p, i: (b, 0, k_block + p), pipeline_mode=once),
            pl.BlockSpec((1, 2, nk, HEAD_DIM, tk), lambda b, p, i: (b, p, 0, 0, 0), pipeline_mode=once),
            pl.BlockSpec((nk, tk, LANES), lambda b, p, i: (0, 0, 0), pipeline_mode=once),
            pl.BlockSpec((1, 2, LANES, tq), lambda b, p, i: (b, p, 0, i)),
            pl.BlockSpec((1, tq, LANES), lambda b, p, i: (b, i, 0)),
        ],
        out_specs=pl.BlockSpec((1, tq, width), lambda b, p, i: (b, i, p)),
        out_shape=jax.ShapeDtypeStruct((B, S, D_MODEL), BF16),
        scratch_shapes=_flash_t_scratch(2 * HPG, tq, tk),
        compiler_params=_cparams(("parallel", "parallel", "parallel")),
    )(proj3, k3, vt, et, selbt, g)


def _nsa_cmp_t_body(q_ref, kc_ref, vot_ref, g_ref, o_ref, selbt_ref, s_ref, m_ref, bias_ref,
                    *, tq, n_cmp, n_sel, top_n):
    s0 = pl.program_id(2) * tq
    kc = kc_ref[0, 0]
    vot = vot_ref[0, 0]
    nc = kc.shape[0]
    nidx = lax.broadcasted_iota(jnp.int32, (nc, tq), 0)
    tpos = s0 + lax.broadcasted_iota(jnp.int32, (nc, tq), 1)
    cmask = (nidx * NSA_CMP_STRIDE + (NSA_CMP_LEN - 1) <= tpos) & (nidx < n_cmp)
    bias_ref[...] = jnp.where(cmask, 0.0, NEG)
    q4 = q_ref[0]
    for pair in range(NSA_HPG // 2):
        pair_t = q4[:, pair * LANES:(pair + 1) * LANES].astype(F32).T
        for hp in range(2):
            h = 2 * pair + hp
            st = _dot(kc, pair_t[hp * HEAD_DIM:(hp + 1) * HEAD_DIM, :].astype(BF16)) + bias_ref[...]
            s_ref[h * nc:(h + 1) * nc, :] = st
            m_ref[h * SUBLANES:(h + 1) * SUBLANES, :] = jnp.broadcast_to(
                jnp.max(st, axis=0, keepdims=True), (SUBLANES, tq))

    imp_t = jnp.zeros((LANES, tq), F32)
    heads = []
    for h in range(NSA_HPG):
        m = m_ref[h * SUBLANES:h * SUBLANES + 1, :]
        res = jnp.zeros((2 * LANES, tq), F32)
        for c0 in range(0, nc, VALUE_SUB):
            et = jnp.exp2(s_ref[h * nc + c0:h * nc + c0 + VALUE_SUB, :] - m).astype(BF16)
            res = res + _dot(vot[:, c0:c0 + VALUE_SUB], et)
        inv = jnp.where(m > 0.5 * NEG, 1.0 / res[HEAD_DIM:HEAD_DIM + 1, :], 0.0)
        heads.append((res[:LANES, :] * inv).T * _gate_col(g_ref[0], 0, h))
        imp_t = imp_t + res[LANES:, :] * inv
    o_ref[0] = jnp.concatenate(
        [_pair_merge(heads[h], heads[h + 1]) for h in range(0, NSA_HPG, 2)], axis=1).astype(o_ref.dtype)

    blk = lax.broadcasted_iota(jnp.int32, imp_t.shape, 0)
    jq = (s0 + lax.broadcasted_iota(jnp.int32, imp_t.shape, 1)) // NSA_SEL_BLOCK
    forced = (blk == 0) | (blk == jq) | (blk == jq - 1)
    cand = jnp.where(forced | (blk >= n_sel), REMOVED, jnp.where(blk > jq, NEG, imp_t))
    sel_t = _topk_mask_t(cand, top_n - N_FORCED) | forced
    selbt_ref[0, 0] = jnp.where(sel_t, 0.0, MASK).astype(selbt_ref.dtype)


def _nsa_cmp_t(proj3, kc, vot, g_cmp, n_cmp, n_sel, top_n, tq=1024):
    B, S, _ = proj3.shape
    G, HPG = NSA_KV_GROUPS, NSA_HPG
    NC = kc.shape[2]
    tq = min(tq, S)
    assert NC % VALUE_SUB == 0 and vot.shape == (B, G, 2 * LANES, NC)
    return pl.pallas_call(
        functools.partial(_nsa_cmp_t_body, tq=tq, n_cmp=n_cmp, n_sel=n_sel, top_n=top_n),
        grid=(B, G, S // tq),
        in_specs=[
            pl.BlockSpec((1, tq, HPG * HEAD_DIM), lambda b, g, i: (b, i, g)),
            pl.BlockSpec((1, 1, NC, HEAD_DIM), lambda b, g, i: (b, g, 0, 0)),
            pl.BlockSpec((1, 1, 2 * LANES, NC), lambda b, g, i: (b, g, 0, 0)),
            pl.BlockSpec((1, tq, LANES), lambda b, g, i: (b, i, 0)),
        ],
        out_specs=[
            pl.BlockSpec((1, tq, HPG * HEAD_DIM), lambda b, g, i: (b, i, g)),
            pl.BlockSpec((1, 1, LANES, tq), lambda b, g, i: (b, g, 0, i)),
        ],
        out_shape=[
            jax.ShapeDtypeStruct((B, S, D_MODEL), BF16),
            jax.ShapeDtypeStruct((B, G, LANES, S), BF16),
        ],
        scratch_shapes=[pltpu.VMEM((HPG * NC, tq), F32), pltpu.VMEM((HPG * SUBLANES, tq), F32),
                        pltpu.VMEM((NC, tq), F32)],
        compiler_params=_cparams(("parallel", "parallel", "parallel")),
    )(proj3, kc, vot, g_cmp)


def _nsa_win_t_body(q_ref, k_ref, vt_ref, g_ref, o_ref, qa_ref, s_ref, m_ref, alpha_ref, acc_ref, bias_ref,
                    *, tq, n_tiles):
    s0 = pl.program_id(2) * tq
    tk = n_tiles * tq
    n_heads = 2 * NSA_HPG
    k_first = jnp.maximum(pl.program_id(2) - (n_tiles - 1), 0)
    k0 = pl.multiple_of(k_first * tq, tq)
    _flash_t_init(m_ref, acc_ref)
    q8 = q_ref[0]
    for pair in range(n_heads // 2):
        grp = pair // (NSA_HPG // 2)
        pair_t = q8[:, pair * LANES:(pair + 1) * LANES].astype(F32).T
        for hp in range(2):
            h = 2 * pair + hp
            qa_ref[h * LANES:(h + 1) * LANES, :] = _head_kt(
                pair_t[hp * HEAD_DIM:(hp + 1) * HEAD_DIM, :].astype(BF16), grp)
    krow, qcol = _tile_iotas(tk, tq)
    dist = (s0 - k0) + qcol - krow
    bias_ref[...] = jnp.where((dist >= 0) & (dist < NSA_WINDOW), 0.0, MASK)
    keys_of = lambda h: k_ref[0, pl.ds(k0, tk), :]
    ones_rows = _ones_row_block(tk)
    vals_of = lambda h: jnp.concatenate(
        [jnp.concatenate([vt_ref[0, h // NSA_HPG, k_first + j] for j in range(n_tiles)], axis=1), ones_rows],
        axis=0)
    _flash_t_tile(n_heads, keys_of, vals_of, lambda: bias_ref[...], qa_ref, s_ref, m_ref, alpha_ref, acc_ref)
    gates = [_gate_col(g_ref[0], 2, h, n_heads) for h in range(n_heads)]
    o_ref[0] = _flash_t_out(acc_ref, n_heads, gates).astype(o_ref.dtype)


def _nsa_win_t(proj3, k3, k_block, vt, g, tq):
    B, S, _ = proj3.shape
    G, HPG = NSA_KV_GROUPS, NSA_HPG
    nk = S // tq
    n_tiles = NSA_WINDOW // tq + 1
    assert NSA_WINDOW % tq == 0 and nk >= n_tiles and G % 2 == 0 and vt.shape == (B, G, nk, HEAD_DIM, tq)
    once = pl.Buffered(1)
    width = 2 * HPG * HEAD_DIM
    return pl.pallas_call(
        functools.partial(_nsa_win_t_body, tq=tq, n_tiles=n_tiles),
        grid=(B, G // 2, S // tq),
        in_specs=[
            pl.BlockSpec((1, tq, width), lambda b, p, i: (b, i, p)),
            pl.BlockSpec((1, S, LANES), lambda b, p, i: (b, 0, k_block + p), pipeline_mode=once),
            pl.BlockSpec((1, 2, nk, HEAD_DIM, tq), lambda b, p, i: (b, p, 0, 0, 0), pipeline_mode=once),
            pl.BlockSpec((1, tq, LANES), lambda b, p, i: (b, i, 0)),
        ],
        out_specs=pl.BlockSpec((1, tq, width), lambda b, p, i: (b, i, p)),
        out_shape=jax.ShapeDtypeStruct((B, S, D_MODEL), BF16),
        scratch_shapes=(_flash_t_scratch(2 * HPG, tq, n_tiles * tq, qa_rows=LANES)
                        + [pltpu.VMEM((n_tiles * tq, tq), F32)]),
        compiler_params=_cparams(("parallel", "parallel", "parallel")),
    )(proj3, k3, vt, g)


def _moba_kmean_pairs_body(k_ref, e_ref, hi_ref, lo_ref, *, tk):
    nk = e_ref.shape[0]
    km = jnp.zeros((LANES, LANES), F32)
    for ki in range(nk):
        km = km + _dot(e_ref[ki], k_ref[0, ki * tk:(ki + 1) * tk, :])
    hi, lo = _split_bf16(km * (1.0 / MOBA_BLOCK))
    hi_ref[0, 0] = hi
    lo_ref[0, 0] = lo


def _moba_kmean_pairs(k3, e, tk):
    B, S, N = k3.shape
    n_pairs = N // LANES
    nk = S // tk
    spec = pl.BlockSpec((1, 1, LANES, LANES), lambda b, p: (b, p, 0, 0))
    return pl.pallas_call(
        functools.partial(_moba_kmean_pairs_body, tk=tk),
        grid=(B, n_pairs),
        in_specs=[
            pl.BlockSpec((1, S, LANES), lambda b, p: (b, 0, p)),
            pl.BlockSpec((nk, LANES, tk), lambda b, p: (0, 0, 0)),
        ],
        out_specs=[spec, spec],
        out_shape=[jax.ShapeDtypeStruct((B, n_pairs, LANES, LANES), BF16)] * 2,
        compiler_params=_cparams(("parallel", "parallel")),
    )(k3, e)


def _moba_t_body(qt_ref, k_ref, vt_ref, et_ref, kh_ref, kl_ref, o_ref, qa_ref, s_ref, m_ref, alpha_ref,
                 acc_ref, *, tq, tk, hb, nb, top_k):
    s0 = pl.program_id(2) * tq
    _flash_t_init(m_ref, acc_ref)
    aug = 2 * LANES

    blk = lax.broadcasted_iota(jnp.int32, (LANES, tq), 0)
    cb = (s0 + lax.broadcasted_iota(jnp.int32, (LANES, tq), 1)) // MOBA_BLOCK
    for h in range(hb):
        qpad = _head_kt(qt_ref[0, h, 0], h % 2)
        gsc = _dot(kh_ref[0, h // 2], qpad) + _dot(kl_ref[0, h // 2], qpad)
        gsc = jnp.where(blk < cb, gsc, NEG)
        gsc = jnp.where(blk < nb, gsc, REMOVED)
        sel = (_topk_mask_t(gsc, top_k) & (blk < cb)) | (blk == cb)
        qa_ref[h * aug:(h + 1) * aug, :] = jnp.concatenate(
            [jnp.where(sel, 0.0, MASK).astype(BF16), qpad], axis=0)

    ones_rows = _ones_row_block(tk)

    def tile(ki, mask):
        k0 = pl.multiple_of(ki * tk, tk)
        et = et_ref[ki]
        keys_of = lambda h: jnp.concatenate(
            [et, k_ref[0, pl.ds(k0, tk), (h // 2) * LANES:(h // 2 + 1) * LANES]], axis=1)
        vals_of = lambda h: jnp.concatenate([vt_ref[0, h, ki], ones_rows], axis=0)
        _flash_t_tile(hb, keys_of, vals_of, mask, qa_ref, s_ref, m_ref, alpha_ref, acc_ref)

    _flash_t_loop(tile, s0, tq, tk)
    o_ref[0] = _flash_t_out(acc_ref, hb, [1.0] * hb).astype(o_ref.dtype)


def _moba_attn_t(qt, k3, vt, et, kh, kl, nb, top_k, tq, tk, hb=8):
    B, H, nq, _, _ = qt.shape
    S = k3.shape[1]
    nk = S // tk
    assert tk == tq and tk % MOBA_BLOCK == 0 and H % hb == 0 and hb % 2 == 0 and nb <= LANES
    once = pl.Buffered(1)
    return pl.pallas_call(
        functools.partial(_moba_t_body, tq=tq, tk=tk, hb=hb, nb=nb, top_k=top_k),
        grid=(B, H // hb, nq),
        in_specs=[
            pl.BlockSpec((1, hb, 1, HEAD_DIM, tq), lambda b, h, i: (b, h, i, 0, 0)),
            pl.BlockSpec((1, S, hb * HEAD_DIM), lambda b, h, i: (b, 0, h), pipeline_mode=once),
            pl.BlockSpec((1, hb, nk, HEAD_DIM, tk), lambda b, h, i: (b, h, 0, 0, 0), pipeline_mode=once),
            pl.BlockSpec((nk, tk, LANES), lambda b, h, i: (0, 0, 0), pipeline_mode=once),
            pl.BlockSpec((1, hb // 2, LANES, LANES), lambda b, h, i: (b, h, 0, 0), pipeline_mode=once),
            pl.BlockSpec((1, hb // 2, LANES, LANES), lambda b, h, i: (b, h, 0, 0), pipeline_mode=once),
        ],
        out_specs=pl.BlockSpec((1, tq, hb * HEAD_DIM), lambda b, h, i: (b, i, h)),
        out_shape=jax.ShapeDtypeStruct((B, S, D_MODEL), BF16),
        scratch_shapes=_flash_t_scratch(hb, tq, tk),
        compiler_params=_cparams(("parallel", "parallel", "parallel")),
    )(qt, k3, vt, et, kh, kl)


def _layer_norm(r, g, b):
    mu = jnp.mean(r, axis=-1, keepdims=True)
    c = r - mu
    var = jnp.mean(c * c, axis=-1, keepdims=True)
    return c * lax.rsqrt(var + LN_EPS) * g + b


def _outproj_body(*refs, n_o):
    o_refs = refs[:n_o]
    (w_ref, x_ref, g_ref, b_ref, wh_ref, wl_ref, bias_ref,
     y_ref, yb_ref, gate_ref, gidt_ref, cnt_ref) = refs[n_o:]
    tm = x_ref.shape[0]
    rc = min(PROJ_ROW_CHUNK, tm)
    for r0 in range(0, tm, rc):
        rows = slice(r0, r0 + rc)
        o = o_refs[0][rows, :].astype(F32)
        for ref in o_refs[1:]:
            o = o + ref[rows, :].astype(F32)
        mix = _dot(o.astype(BF16), w_ref[...])
        y = _layer_norm(DN_ALPHA * x_ref[rows, :] + mix, g_ref[...], b_ref[...])
        y_ref[rows, :] = y
        yb_ref[rows, :] = y.astype(BF16)
    _router_sorted_body(y_ref, wh_ref, wl_ref, bias_ref, gate_ref, gidt_ref, cnt_ref)


def _outproj_ln_route(os_, w, x, g, b, router_w, router_bias):
    T, D = x.shape
    tm = min(MOE_WINDOW, T)
    n_o = len(os_)
    wpad = jnp.zeros((D, LANES), F32).at[:, :N_EXPERTS].set(router_w)
    wh, wl = _split_bf16(wpad)
    bpad = jnp.zeros((1, LANES), F32).at[0, :N_EXPERTS].set(router_bias)
    row = pl.BlockSpec((tm, D), lambda i: (i, 0))
    vec = pl.BlockSpec((1, D), lambda i: (0, 0))
    rw = pl.BlockSpec((D, LANES), lambda i: (0, 0))
    return pl.pallas_call(
        functools.partial(_outproj_body, n_o=n_o),
        grid=(T // tm,),
        in_specs=[row] * n_o + [pl.BlockSpec((D, D), lambda i: (0, 0)), row, vec, vec, rw, rw,
                                pl.BlockSpec((1, LANES), lambda i: (0, 0))],
        out_specs=[row, row,
                   pl.BlockSpec((tm, LANES), lambda i: (i, 0)),
                   pl.BlockSpec((SUBLANES, tm), lambda i: (0, i)),
                   pl.BlockSpec((1, SUBLANES, LANES), lambda i: (i, 0, 0))],
        out_shape=[jax.ShapeDtypeStruct((T, D), F32), jax.ShapeDtypeStruct((T, D), BF16),
                   jax.ShapeDtypeStruct((T, LANES), F32),
                   jax.ShapeDtypeStruct((SUBLANES, T), F32),
                   jax.ShapeDtypeStruct((T // tm, SUBLANES, LANES), F32)],
        compiler_params=_cparams(("parallel",)),
    )(*os_, w, x, g.reshape(1, D), b.reshape(1, D), wh, wl, bpad)


GID_LANE = N_EXPERTS
MOE_WINDOW = 1024
MOE_CHUNK = 128
MOE_VMEM_LIMIT = 60 * 1024 * 1024


def _router_sorted_body(x_ref, wh_ref, wl_ref, bias_ref, gate_ref, gidt_ref, cnt_ref):
    x_hi, x_lo = _split_bf16(x_ref[...])
    wh = wh_ref[...]
    logits = _dot(x_hi, wh) + _dot(x_lo, wh) + _dot(x_hi, wl_ref[...])
    scores = _sigmoid(logits)
    score_t = scores.T
    biased_t = (scores + bias_ref[...]).T
    tm = score_t.shape[1]
    s_rows = [score_t[e:e + 1, :] for e in range(N_EXPERTS)]
    b_rows = [biased_t[e:e + 1, :] for e in range(N_EXPERTS)]

    def first_max(vals, live):
        m = None
        for v, ok in zip(vals, live):
            cand = jnp.where(ok > 0.5, v, REMOVED)
            m = cand if m is None else jnp.maximum(m, cand)
        hits, found = [], jnp.zeros_like(m)
        for v, ok in zip(vals, live):
            hit = jnp.where((v == m) & (ok > 0.5) & (found < 0.5), 1.0, 0.0)
            found = found + hit
            hits.append(hit)
        return m, hits

    ones = jnp.ones((1, tm), F32)
    best, gid, sels = None, None, []
    for grp in range(N_GROUPS):
        vals = b_rows[grp * EXPERTS_PER_GROUP:(grp + 1) * EXPERTS_PER_GROUP]
        m1, h1 = first_max(vals, [ones] * EXPERTS_PER_GROUP)
        m2, h2 = first_max(vals, [1.0 - h for h in h1])
        sels.append([a + b for a, b in zip(h1, h2)])
        score = m1 + m2
        if grp == 0:
            best, gid = score, jnp.zeros_like(score)
        else:
            better = score > best
            best = jnp.where(better, score, best)
            gid = jnp.where(better, float(grp), gid)
    w_rows = []
    for e in range(N_EXPERTS):
        grp, j = divmod(e, EXPERTS_PER_GROUP)
        w_rows.append(jnp.where((gid == float(grp)) & (sels[grp][j] > 0.5), s_rows[e], 0.0))
    total = w_rows[0]
    for w in w_rows[1:]:
        total = total + w
    rows_out = 3 * SUBLANES
    row = lax.broadcasted_iota(jnp.int32, (rows_out, tm), 0)
    gate_t = jnp.where(row == GID_LANE, gid, 0.0)
    for e in range(N_EXPERTS):
        gate_t = jnp.where(row == e, w_rows[e] / total, gate_t)
    gate_t = jnp.concatenate([gate_t, jnp.zeros((LANES - rows_out, tm), F32)], axis=0)
    gate_ref[...] = gate_t.T
    gidt_ref[...] = jnp.broadcast_to(gid, (SUBLANES, tm))
    lane = lax.broadcasted_iota(jnp.int32, (SUBLANES, LANES), 1)
    counts = jnp.zeros((SUBLANES, LANES), F32)
    for grp in range(N_GROUPS):
        n = jnp.sum(jnp.where(gid == float(grp), 1.0, 0.0), axis=-1, keepdims=True)
        counts = jnp.where(lane == grp, n, counts)
    cnt_ref[0] = counts


def _experts_sorted_body(cnt_ref, xb_ref, x_ref, gate_ref, gidt_ref, ltri_ref, utri_ref, wg_ref, wu_ref,
                         wd_ref, g_ref, b_ref, y_ref, yb_ref, xs_ref, gs_ref, acc_ref, rank_ref,
                         *, W, Wp, chunk):
    win = pl.program_id(0)
    e = pl.program_id(1)
    grp = e // EXPERTS_PER_GROUP
    padded = [((cnt_ref[win * N_GROUPS + g] + chunk - 1) // chunk) * chunk for g in range(N_GROUPS)]
    starts = [0]
    for g in range(N_GROUPS - 1):
        starts.append(starts[-1] + padded[g])
    start = starts[0]
    for g in range(1, N_GROUPS):
        start = jnp.where(grp == g, starts[g], start)
    n_chunks = (cnt_ref[win * N_GROUPS + grp] + chunk - 1) // chunk

    @pl.when(e == 0)
    def _():
        gate = gate_ref[...]
        lane = lax.broadcasted_iota(jnp.int32, gate.shape, 1)
        lanef = lane.astype(F32)
        gid = jnp.sum(jnp.where(lane == GID_LANE, gate, 0.0), axis=-1, keepdims=True)
        member = jnp.where((lanef == gid) & (lane < N_GROUPS), 1.0, 0.0)
        earlier = _dot(ltri_ref[...], member.astype(BF16))
        first = jnp.zeros(gate.shape, F32)
        for g in range(1, N_GROUPS):
            first = jnp.where(lane == g, starts[g].astype(F32), first)
        rank = jnp.sum(member * (first + earlier), axis=-1, keepdims=True)
        rank_ref[...] = jnp.broadcast_to(rank, gate.shape)

        gid_r = gidt_ref[...]
        sub = lax.broadcasted_iota(jnp.int32, gid_r.shape, 0)
        member_r = jnp.where(sub.astype(F32) == gid_r, 1.0, 0.0)
        earlier_r = _dot(member_r.astype(BF16), utri_ref[...])
        first_r = jnp.zeros(gid_r.shape, F32)
        for g in range(1, N_GROUPS):
            first_r = jnp.where(sub == g, starts[g].astype(F32), first_r)
        rank_r = jnp.sum(member_r * (first_r + earlier_r), axis=0, keepdims=True)
        rows = lax.broadcasted_iota(jnp.int32, (Wp, W), 0).astype(F32)
        perm = jnp.where(rows == rank_r, 1.0, 0.0).astype(BF16)
        xs_ref[...] = _dot(perm, xb_ref[...]).astype(BF16)
        g_hi, g_lo = _split_bf16(gate)
        gs_ref[...] = _dot(perm, g_hi) + _dot(perm, g_lo)
        acc_ref[...] = jnp.zeros(acc_ref.shape, F32)

    def expert_rows(r0, rows):
        r0 = pl.multiple_of(r0, chunk)
        xc = xs_ref[pl.ds(r0, rows), :]
        a = _dot(xc, wg_ref[0])
        u = _dot(xc, wu_ref[0])
        gs = gs_ref[pl.ds(r0, rows), :]
        lane = lax.broadcasted_iota(jnp.int32, gs.shape, 1)
        gcol = jnp.sum(jnp.where(lane == e, gs, 0.0), axis=-1, keepdims=True)
        h = a * _sigmoid(a) * u * gcol
        acc_ref[pl.ds(r0, rows), :] += _dot(h.astype(BF16), wd_ref[0])

    def three_chunks(i, carry):
        expert_rows(start + i * (3 * chunk), 3 * chunk)
        return carry

    lax.fori_loop(0, n_chunks // 3, three_chunks, 0)
    tail = start + (n_chunks // 3) * (3 * chunk)
    for left in (1, 2):
        pl.when(n_chunks % 3 == left)(functools.partial(expert_rows, tail, left * chunk))

    @pl.when(e == N_EXPERTS - 1)
    def _():
        cols = lax.broadcasted_iota(jnp.int32, (W, Wp), 1).astype(F32)
        unperm = jnp.where(cols == rank_ref[...][:, :1], 1.0, 0.0).astype(BF16)
        ffn = _dot(unperm, acc_ref[...].astype(BF16))
        y = _layer_norm(DN_ALPHA * x_ref[...] + ffn, g_ref[...], b_ref[...])
        y_ref[...] = y
        yb_ref[...] = y.astype(BF16)


def _moe_ln(xb, x, gate, gidt, cnt, wg, wu, wd, g, b):
    T, D = x.shape
    W = min(MOE_WINDOW, T)
    chunk = MOE_CHUNK
    Wp = W + N_GROUPS * chunk
    E, _, DE = wg.shape
    assert T % W == 0 and W % chunk == 0
    counts = cnt[:, 0, :N_GROUPS].astype(jnp.int32).reshape(-1)
    t = np.arange(W)
    ltri = jnp.asarray((t[None, :] < t[:, None]).astype(np.float32), BF16)
    once = pl.Buffered(1)
    row = lambda shape: pl.BlockSpec(shape, lambda i, e, c: (i, 0))
    vec = pl.BlockSpec((1, D), lambda i, e, c: (0, 0))
    tri = pl.BlockSpec((W, W), lambda i, e, c: (0, 0), pipeline_mode=once)
    grid_spec = pltpu.PrefetchScalarGridSpec(
        num_scalar_prefetch=1,
        grid=(T // W, E),
        in_specs=[
            row((W, D)),
            pl.BlockSpec((W, D), lambda i, e, c: (i, 0), pipeline_mode=once),
            row((W, LANES)),
            pl.BlockSpec((SUBLANES, W), lambda i, e, c: (0, i)),
            tri, tri,
            pl.BlockSpec((1, D, DE), lambda i, e, c: (e, 0, 0)),
            pl.BlockSpec((1, D, DE), lambda i, e, c: (e, 0, 0)),
            pl.BlockSpec((1, DE, D), lambda i, e, c: (e, 0, 0)),
            vec, vec,
        ],
        out_specs=[row((W, D)), row((W, D))],
        scratch_shapes=[
            pltpu.VMEM((Wp, D), BF16),
            pltpu.VMEM((Wp, LANES), F32),
            pltpu.VMEM((Wp, D), F32),
            pltpu.VMEM((W, LANES), F32),
        ],
    )
    return pl.pallas_call(
        functools.partial(_experts_sorted_body, W=W, Wp=Wp, chunk=chunk),
        grid_spec=grid_spec,
        out_shape=[jax.ShapeDtypeStruct((T, D), F32), jax.ShapeDtypeStruct((T, D), BF16)],
        compiler_params=pltpu.CompilerParams(dimension_semantics=("parallel", "arbitrary"),
                                             vmem_limit_bytes=MOE_VMEM_LIMIT),
    )(counts, xb, x, gate, gidt, ltri, ltri.T, wg, wu, wd, g.reshape(1, D), b.reshape(1, D))


def _block_onehots(S, block, tk):
    key = np.arange(S).reshape(S // tk, 1, tk)
    r = np.arange(LANES).reshape(1, LANES, 1)
    return jnp.asarray((key // block == r).astype(np.float32), BF16)


def _rope_tiled(S):
    cos, sin = _rope_tables(jnp.arange(S))
    reps = LANES // HALF
    return jnp.tile(cos, (1, reps)), jnp.tile(sin, (1, reps))


def _nsa_mixer(xb, B, S, w_in, cmp_k_w1, cmp_k_w2, cmp_v_w1, cmp_v_w2, cmp_k_pos, cmp_v_pos):
    G, HPG, KV = NSA_KV_GROUPS, NSA_HPG, NSA_KV_DIM
    L, STR, SB = NSA_CMP_LEN, NSA_CMP_STRIDE, NSA_SEL_BLOCK
    assert L == 2 * STR and S % SB == 0 and S // SB <= LANES
    T = B * S
    n_cmp = (S - L) // STR + 1
    NC = S // STR
    n_sel = S // SB
    top_n = min(NSA_SEL_TOPN, n_sel)
    assert top_n >= N_FORCED

    cos2, sin2 = _rope_tiled(S)
    wb = w_in.astype(BF16)
    wcol = lambda i: wb[:, D_MODEL + i * KV: D_MODEL + (i + 1) * KV]
    tn = 2 * KV
    tk_s = min(512, S)
    tq_w = NSA_WINDOW // 2
    proj = _proj(xb, wb[:, :D_MODEL + 2 * KV], cos2, sin2, [2] * (D_MODEL // tn) + [0], S, tn=tn)
    kk3 = _proj(xb, jnp.concatenate([wcol(2), wcol(4)], axis=1), cos2, sin2, [1], S, tn=tn).reshape(B, S, tn)
    vt_s = _proj_kt(xb, wcol(3), cos2, sin2, B, S, tk_s, rope=False)
    vt_w = _proj_kt(xb, wcol(5), cos2, sin2, B, S, tq_w, rope=False)
    wg = jnp.zeros((D_MODEL, LANES), BF16).at[:, :3 * N_HEADS].set(wb[:, D_MODEL + 6 * KV:])
    gates = _proj(xb, wg, cos2, sin2, [0], S, out_dtype=F32, tn=LANES).reshape(B, S, LANES)

    col = lambda i: proj[:, D_MODEL + i * KV: D_MODEL + (i + 1) * KV]
    proj3 = proj.reshape(B, S, proj.shape[1])

    ccos, csin = _rope_tables(jnp.arange(NC) * STR + (L - 1))
    ccos = jnp.concatenate([ccos, ccos], axis=1)
    csin = jnp.concatenate([csin, csin], axis=1)
    to_rows = lambda t: t.reshape(B, S, G, HEAD_DIM).transpose(0, 2, 1, 3).reshape(B * G, NC, STR * HEAD_DIM)
    kc = _compress(to_rows(col(0)), cmp_k_w1, cmp_k_pos, cmp_k_w2, ccos, csin, True, n_cmp)
    vc = _compress(to_rows(col(1)), cmp_v_w1, cmp_v_pos, cmp_v_w2, ccos, csin, False, n_cmp)

    ci = np.arange(NC)[:, None]
    sj = np.arange(LANES)[None, :]
    overlap = ((ci * STR < (sj + 1) * SB) & (ci * STR + L > sj * SB) & (ci < n_cmp) & (sj < n_sel))
    overlap = jnp.broadcast_to(jnp.asarray(overlap.astype(np.float32), BF16), (B, G, NC, LANES))
    vo = jnp.concatenate([vc.reshape(B, G, NC, HEAD_DIM), jnp.ones((B, G, NC, 1), BF16),
                          jnp.zeros((B, G, NC, LANES - HEAD_DIM - 1), BF16), overlap], axis=-1)

    o_cmp, selbt = _nsa_cmp_t(proj3, kc.reshape(B, G, NC, HEAD_DIM), vo.transpose(0, 1, 3, 2), gates,
                              n_cmp, n_sel, top_n)
    o_slc = _nsa_slc_t(proj3, kk3, 0, vt_s, _block_onehots(S, SB, tk_s).transpose(0, 2, 1), selbt, gates,
                       tq=tk_s, tk=tk_s)
    o_win = _nsa_win_t(proj3, kk3, G // 2, vt_w, gates, tq=tq_w)
    return [o.reshape(T, D_MODEL) for o in (o_cmp, o_slc, o_win)]


def _moba_mixer(xb, B, S, w_in):
    H = N_HEADS
    nb = S // MOBA_BLOCK
    top_k = min(MOBA_TOPK, nb)
    cos2, sin2 = _rope_tiled(S)
    tn = 512
    n_t = D_MODEL // tn
    wb = w_in.astype(BF16)
    tk = min(2 * MOBA_BLOCK, S)
    qt = _proj_kt(xb, wb[:, :D_MODEL], cos2, sin2, B, S, tk, tn=tn, scale=Q_SCALE_LOG2)
    k3 = _proj(xb, wb[:, D_MODEL:2 * D_MODEL], cos2, sin2, [1] * n_t, S, tn=tn).reshape(B, S, D_MODEL)
    vt = _proj_kt(xb, wb[:, 2 * D_MODEL:], cos2, sin2, B, S, tk, tn=tn, rope=False)
    e = _block_onehots(S, MOBA_BLOCK, tk)
    kh, kl = _moba_kmean_pairs(k3, e, tk)
    o = _moba_attn_t(qt, k3, vt, e.transpose(0, 2, 1), kh, kl, nb, top_k, tq=tk, tk=tk)
    return [o.reshape(B * S, D_MODEL)]


def kernel(x, nsa_w_in, nsa_w_out, nsa_cmp_k_w1, nsa_cmp_k_w2, nsa_cmp_v_w1, nsa_cmp_v_w2, nsa_cmp_k_pos, nsa_cmp_v_pos, moba_w_in, moba_w_out, router_w, router_bias, moe_w_gate, moe_w_up, moe_w_down, ln_g, ln_b):
    B, S, D = x.shape
    xf = x.reshape(B * S, D)
    xb = xf.astype(BF16)
    for layer in range(DEPTH):
        j = layer // 2
        if layer % 2 == 0:
            os_ = _nsa_mixer(xb, B, S, nsa_w_in[j], nsa_cmp_k_w1[j], nsa_cmp_k_w2[j], nsa_cmp_v_w1[j],
                             nsa_cmp_v_w2[j], nsa_cmp_k_pos[j], nsa_cmp_v_pos[j])
            w_out = nsa_w_out[j]
        else:
            os_ = _moba_mixer(xb, B, S, moba_w_in[j])
            w_out = moba_w_out[j]
        xf, xb, gate, gidt, cnt = _outproj_ln_route(os_, w_out.astype(BF16), xf, ln_g[layer, 0], ln_b[layer, 0],
                                                    router_w, router_bias)
        xf, xb = _moe_ln(xb, xf, gate, gidt, cnt, moe_w_gate[layer].astype(BF16),
                         moe_w_up[layer].astype(BF16), moe_w_down[layer].astype(BF16),
                         ln_g[layer, 1], ln_b[layer, 1])
    return xf.reshape(B, S, D)
```
